```python
import math
import jax, jax.numpy as jnp
from jax import lax
import numpy as np

D_MODEL = 1024
BATCH = 16
SEQ = 2048
DEPTH = 2

N_MIXERS = 2
N_HEADS = 16
HEAD_DIM = D_MODEL // N_HEADS
Q_BLOCK = 128
SSM_GROUP = 16
N_GROUPS = D_MODEL // SSM_GROUP
STATE = 64
D_FF = ((8 * D_MODEL // 3 + 127) // 128) * 128
CONV_W = 3
N_ATTN = (DEPTH + 1) // 2
N_SSM = DEPTH // 2
EPS = 1e-6
DT_MIN = 1e-3
DT_MAX = 1e-1

kernel_name = "hybrid_stickbreak_s5_convffn_adaln"


def rms_norm(x, g):
    xf = x.astype(jnp.float32)
    y = xf * lax.rsqrt(jnp.mean(xf * xf, axis=-1, keepdims=True) + EPS)
    return (y * g.astype(jnp.float32)).astype(x.dtype)


def modulate(h, shift, scale):
    return h * (1 + scale[:, None, :]) + shift[:, None, :]


def stick_breaking_attention(h, w_qkv, w_o):
    b, s, d = h.shape
    q, k, v = jnp.split(h @ w_qkv, 3, axis=-1)
    to_heads = lambda t: t.reshape(b, s, N_HEADS, HEAD_DIM).transpose(0, 2, 1, 3)
    q, k, v = to_heads(q), to_heads(k), to_heads(v)
    kf = k.astype(jnp.float32)
    vf = v.astype(jnp.float32)
    n_blk = s // Q_BLOCK
    qb = q.reshape(b, N_HEADS, n_blk, Q_BLOCK, HEAD_DIM).transpose(2, 0, 1, 3, 4)
    key_pos = jnp.arange(s)
    scale = HEAD_DIM ** -0.5

    def block(args):
        q_blk, blk_idx = args
        q_pos = blk_idx * Q_BLOCK + jnp.arange(Q_BLOCK)
        z = jnp.einsum('bhqd,bhkd->bhqk', q_blk.astype(jnp.float32), kf) * scale
        mask = key_pos[None, :] < q_pos[:, None]
        log_beta = jax.nn.log_sigmoid(z)
        log_1mb = jnp.where(mask, jax.nn.log_sigmoid(-z), 0.0)
        suffix = lax.cumsum(log_1mb, axis=3, reverse=True) - log_1mb
        w = jnp.where(mask, jnp.exp(log_beta + suffix), 0.0)
        o = jnp.einsum('bhqk,bhkd->bhqd', w, vf)
        return o.astype(h.dtype)

    o = lax.map(block, (qb, jnp.arange(n_blk)))
    o = o.transpose(1, 0, 3, 2, 4).reshape(b, s, d)
    return o @ w_o


def s5_ssm(h, w_in, a_re, a_im, log_dt, b_re, b_im, c_re, c_im, d_skip, w_glu, b_glu, w_o):
    b, s, d = h.shape
    u = h @ w_in
    uf = u.astype(jnp.float32)
    ug = uf.reshape(b, s, N_GROUPS, SSM_GROUP)
    lam = lax.complex(a_re.astype(jnp.float32), a_im.astype(jnp.float32))
    dt = jnp.exp(log_dt.astype(jnp.float32))[:, None]
    lam_bar = jnp.exp(lam * dt)
    b_mat = lax.complex(b_re.astype(jnp.float32), b_im.astype(jnp.float32))
    b_bar = ((lam_bar - 1) / lam)[..., None] * b_mat
    bu = jnp.einsum('gph,bsgh->bsgp', b_bar, ug.astype(jnp.complex64))
    a_seq = jnp.broadcast_to(lam_bar, (1, s, N_GROUPS, STATE))

    def combine(e1, e2):
        a1, x1 = e1
        a2, x2 = e2
        return a2 * a1, a2 * x1 + x2

    _, states = lax.associative_scan(combine, (a_seq, bu), axis=1)
    c_mat = lax.complex(c_re.astype(jnp.float32), c_im.astype(jnp.float32))
    y = jnp.einsum('ghp,bsgp->bsgh', c_mat, states).real.reshape(b, s, d)
    y = (y + d_skip.astype(jnp.float32) * uf).astype(h.dtype)
    z = jax.nn.gelu(y)
    g = z * jax.nn.sigmoid(z @ w_glu + b_glu)
    return g @ w_o


def conv_ffn(h, w_up, conv_w, conv_b, w_down):
    up = h @ w_up
    up = lax.conv_general_dilated(
        up, conv_w[:, None, :], window_strides=(1,), padding=[(CONV_W - 1, 0)],
        dimension_numbers=('NWC', 'WIO', 'NWC'), feature_group_count=2 * D_FF) + conv_b
    gate, val = jnp.split(up, 2, axis=-1)
    return (jax.nn.silu(gate) * val) @ w_down


def _fwd_setup_inputs(seed: int = 0) -> dict:
    key = jax.random.key(seed)
    ks = iter(jax.random.split(key, 40))
    nrm = lambda shape, std: jax.random.normal(next(ks), shape, jnp.float32) * std
    D, G, P, H, F = D_MODEL, N_GROUPS, STATE, SSM_GROUP, D_FF
    n_idx = jnp.arange(P, dtype=jnp.float32)
    inp = {}
    inp["x"] = nrm((BATCH, SEQ, D), 1.0)
    inp["c"] = nrm((BATCH, D), 1.0)
    inp["norm_mix"] = 1.0 + nrm((DEPTH, D), 0.02)
    inp["norm_ffn"] = 1.0 + nrm((DEPTH, D), 0.02)
    inp["w_mod"] = nrm((DEPTH, D, 6 * D), 0.5 * D ** -0.5)
    inp["b_mod"] = nrm((DEPTH, 6 * D), 0.02)
    inp["w_qkv"] = nrm((N_ATTN, D, 3 * D), D ** -0.5)
    inp["w_o_attn"] = nrm((N_ATTN, D, D), D ** -0.5)
    inp["w_in_ssm"] = nrm((N_SSM, D, D), D ** -0.5)
    inp["a_re"] = -0.5 + nrm((N_SSM, G, P), 0.01)
    inp["a_im"] = math.pi * n_idx + nrm((N_SSM, G, P), 0.01)
    inp["log_dt"] = jax.random.uniform(next(ks), (N_SSM, G), jnp.float32,
                                       math.log(DT_MIN), math.log(DT_MAX))
    inp["b_re"] = nrm((N_SSM, G, P, H), (2 * H) ** -0.5)
    inp["b_im"] = nrm((N_SSM, G, P, H), (2 * H) ** -0.5)
    inp["c_re"] = nrm((N_SSM, G, H, P), (2 * P) ** -0.5 * 4.0)
    inp["c_im"] = nrm((N_SSM, G, H, P), (2 * P) ** -0.5 * 4.0)
    inp["d_skip"] = nrm((N_SSM, D), 1.0)
    inp["w_glu"] = nrm((N_SSM, D, D), D ** -0.5)
    inp["b_glu"] = nrm((N_SSM, D), 0.02)
    inp["w_o_ssm"] = nrm((N_SSM, D, D), D ** -0.5)
    inp["w_up"] = nrm((DEPTH, D, 2 * F), D ** -0.5)
    inp["conv_w"] = nrm((DEPTH, CONV_W, 2 * F), CONV_W ** -0.5)
    inp["conv_b"] = nrm((DEPTH, 2 * F), 0.02)
    inp["w_down"] = nrm((DEPTH, F, D), F ** -0.5)
    inp["norm_out"] = 1.0 + nrm((D,), 0.02)
    inp["w_fin"] = nrm((D, 2 * D), 0.5 * D ** -0.5)
    inp["b_fin"] = nrm((2 * D,), 0.02)
    return inp


def _fwd_reference(x, c, norm_mix, norm_ffn, w_mod, b_mod, w_qkv, w_o_attn, w_in_ssm,
              a_re, a_im, log_dt, b_re, b_im, c_re, c_im, d_skip, w_glu, b_glu, w_o_ssm,
              w_up, conv_w, conv_b, w_down, norm_out, w_fin, b_fin):
    c_act = jax.nn.silu(c)
    for i in range(DEPTH):
        mod = c_act @ w_mod[i] + b_mod[i]
        sh1, sc1, g1, sh2, sc2, g2 = jnp.split(mod, 6, axis=-1)
        h = modulate(rms_norm(x, norm_mix[i]), sh1, sc1)
        j = i // N_MIXERS
        if i % N_MIXERS == 0:
            y = stick_breaking_attention(h, w_qkv[j], w_o_attn[j])
        else:
            y = s5_ssm(h, w_in_ssm[j], a_re[j], a_im[j], log_dt[j], b_re[j], b_im[j],
                       c_re[j], c_im[j], d_skip[j], w_glu[j], b_glu[j], w_o_ssm[j])
        x = x + g1[:, None, :] * y
        h = modulate(rms_norm(x, norm_ffn[i]), sh2, sc2)
        x = x + g2[:, None, :] * conv_ffn(h, w_up[i], conv_w[i], conv_b[i], w_down[i])
    fin = c_act @ w_fin + b_fin
    sh, sc = jnp.split(fin, 2, axis=-1)
    return modulate(rms_norm(x, norm_out), sh, sc)


import jax as _jax
import jax.numpy as _jnp

TWIN_FORMAT = 'train_step'
FWD_PARAMS = ['x', 'c', 'norm_mix', 'norm_ffn', 'w_mod', 'b_mod', 'w_qkv', 'w_o_attn', 'w_in_ssm', 'a_re', 'a_im', 'log_dt', 'b_re', 'b_im', 'c_re', 'c_im', 'd_skip', 'w_glu', 'b_glu', 'w_o_ssm', 'w_up', 'conv_w', 'conv_b', 'w_down', 'norm_out', 'w_fin', 'b_fin']
TWIN_WEIGHTS = ['norm_mix', 'norm_ffn', 'w_mod', 'b_mod', 'w_qkv', 'w_o_attn', 'w_in_ssm', 'a_re', 'a_im', 'log_dt', 'b_re', 'b_im', 'c_re', 'c_im', 'd_skip', 'w_glu', 'b_glu', 'w_o_ssm', 'w_up', 'conv_w', 'conv_b', 'w_down', 'norm_out', 'w_fin', 'b_fin']
TWIN_DIFF_INPUT = 'x'
TWIN_INPUTS = ['x', 'c', 'norm_mix', 'norm_ffn', 'w_mod', 'b_mod', 'w_qkv', 'w_o_attn', 'w_in_ssm', 'a_re', 'a_im', 'log_dt', 'b_re', 'b_im', 'c_re', 'c_im', 'd_skip', 'w_glu', 'b_glu', 'w_o_ssm', 'w_up', 'conv_w', 'conv_b', 'w_down', 'norm_out', 'w_fin', 'b_fin', 'loss_target', 'm_norm_mix', 'm_norm_ffn', 'm_w_mod', 'm_b_mod', 'm_w_qkv', 'm_w_o_attn', 'm_w_in_ssm', 'm_a_re', 'm_a_im', 'm_log_dt', 'm_b_re', 'm_b_im', 'm_c_re', 'm_c_im', 'm_d_skip', 'm_w_glu', 'm_b_glu', 'm_w_o_ssm', 'm_w_up', 'm_conv_w', 'm_conv_b', 'm_w_down', 'm_norm_out', 'm_w_fin', 'm_b_fin', 'v_norm_mix', 'v_norm_ffn', 'v_w_mod', 'v_b_mod', 'v_w_qkv', 'v_w_o_attn', 'v_w_in_ssm', 'v_a_re', 'v_a_im', 'v_log_dt', 'v_b_re', 'v_b_im', 'v_c_re', 'v_c_im', 'v_d_skip', 'v_w_glu', 'v_b_glu', 'v_w_o_ssm', 'v_w_up', 'v_conv_w', 'v_conv_b', 'v_w_down', 'v_norm_out', 'v_w_fin', 'v_b_fin']
TWIN_OUTPUTS = ['loss', 'grad_x', 'grad_norm_mix', 'grad_norm_ffn', 'grad_w_mod', 'grad_b_mod', 'grad_w_qkv', 'grad_w_o_attn', 'grad_w_in_ssm', 'grad_a_re', 'grad_a_im', 'grad_log_dt', 'grad_b_re', 'grad_b_im', 'grad_c_re', 'grad_c_im', 'grad_d_skip', 'grad_w_glu', 'grad_b_glu', 'grad_w_o_ssm', 'grad_w_up', 'grad_conv_w', 'grad_conv_b', 'grad_w_down', 'grad_norm_out', 'grad_w_fin', 'grad_b_fin', 'delta_norm_mix', 'delta_norm_ffn', 'delta_w_mod', 'delta_b_mod', 'delta_w_qkv', 'delta_w_o_attn', 'delta_w_in_ssm', 'delta_a_re', 'delta_a_im', 'delta_log_dt', 'delta_b_re', 'delta_b_im', 'delta_c_re', 'delta_c_im', 'delta_d_skip', 'delta_w_glu', 'delta_b_glu', 'delta_w_o_ssm', 'delta_w_up', 'delta_conv_w', 'delta_conv_b', 'delta_w_down', 'delta_norm_out', 'delta_w_fin', 'delta_b_fin', 'new_m_norm_mix', 'new_m_norm_ffn', 'new_m_w_mod', 'new_m_b_mod', 'new_m_w_qkv', 'new_m_w_o_attn', 'new_m_w_in_ssm', 'new_m_a_re', 'new_m_a_im', 'new_m_log_dt', 'new_m_b_re', 'new_m_b_im', 'new_m_c_re', 'new_m_c_im', 'new_m_d_skip', 'new_m_w_glu', 'new_m_b_glu', 'new_m_w_o_ssm', 'new_m_w_up', 'new_m_conv_w', 'new_m_conv_b', 'new_m_w_down', 'new_m_norm_out', 'new_m_w_fin', 'new_m_b_fin', 'new_v_norm_mix', 'new_v_norm_ffn', 'new_v_w_mod', 'new_v_b_mod', 'new_v_w_qkv', 'new_v_w_o_attn', 'new_v_w_in_ssm', 'new_v_a_re', 'new_v_a_im', 'new_v_log_dt', 'new_v_b_re', 'new_v_b_im', 'new_v_c_re', 'new_v_c_im', 'new_v_d_skip', 'new_v_w_glu', 'new_v_b_glu', 'new_v_w_o_ssm', 'new_v_w_up', 'new_v_conv_w', 'new_v_conv_b', 'new_v_w_down', 'new_v_norm_out', 'new_v_w_fin', 'new_v_b_fin']
TWIN_LEAF_KINDS = {'loss': 'loss', 'grad_x': 'grad_x', 'grad_norm_mix': 'grad_w', 'grad_norm_ffn': 'grad_w', 'grad_w_mod': 'grad_w', 'grad_b_mod': 'grad_w', 'grad_w_qkv': 'grad_w', 'grad_w_o_attn': 'grad_w', 'grad_w_in_ssm': 'grad_w', 'grad_a_re': 'grad_w', 'grad_a_im': 'grad_w', 'grad_log_dt': 'grad_w', 'grad_b_re': 'grad_w', 'grad_b_im': 'grad_w', 'grad_c_re': 'grad_w', 'grad_c_im': 'grad_w', 'grad_d_skip': 'grad_w', 'grad_w_glu': 'grad_w', 'grad_b_glu': 'grad_w', 'grad_w_o_ssm': 'grad_w', 'grad_w_up': 'grad_w', 'grad_conv_w': 'grad_w', 'grad_conv_b': 'grad_w', 'grad_w_down': 'grad_w', 'grad_norm_out': 'grad_w', 'grad_w_fin': 'grad_w', 'grad_b_fin': 'grad_w', 'delta_norm_mix': 'delta_w', 'delta_norm_ffn': 'delta_w', 'delta_w_mod': 'delta_w', 'delta_b_mod': 'delta_w', 'delta_w_qkv': 'delta_w', 'delta_w_o_attn': 'delta_w', 'delta_w_in_ssm': 'delta_w', 'delta_a_re': 'delta_w', 'delta_a_im': 'delta_w', 'delta_log_dt': 'delta_w', 'delta_b_re': 'delta_w', 'delta_b_im': 'delta_w', 'delta_c_re': 'delta_w', 'delta_c_im': 'delta_w', 'delta_d_skip': 'delta_w', 'delta_w_glu': 'delta_w', 'delta_b_glu': 'delta_w', 'delta_w_o_ssm': 'delta_w', 'delta_w_up': 'delta_w', 'delta_conv_w': 'delta_w', 'delta_conv_b': 'delta_w', 'delta_w_down': 'delta_w', 'delta_norm_out': 'delta_w', 'delta_w_fin': 'delta_w', 'delta_b_fin': 'delta_w', 'new_m_norm_mix': 'new_m', 'new_m_norm_ffn': 'new_m', 'new_m_w_mod': 'new_m', 'new_m_b_mod': 'new_m', 'new_m_w_qkv': 'new_m', 'new_m_w_o_attn': 'new_m', 'new_m_w_in_ssm': 'new_m', 'new_m_a_re': 'new_m', 'new_m_a_im': 'new_m', 'new_m_log_dt': 'new_m', 'new_m_b_re': 'new_m', 'new_m_b_im': 'new_m', 'new_m_c_re': 'new_m', 'new_m_c_im': 'new_m', 'new_m_d_skip': 'new_m', 'new_m_w_glu': 'new_m', 'new_m_b_glu': 'new_m', 'new_m_w_o_ssm': 'new_m', 'new_m_w_up': 'new_m', 'new_m_conv_w': 'new_m', 'new_m_conv_b': 'new_m', 'new_m_w_down': 'new_m', 'new_m_norm_out': 'new_m', 'new_m_w_fin': 'new_m', 'new_m_b_fin': 'new_m', 'new_v_norm_mix': 'new_v', 'new_v_norm_ffn': 'new_v', 'new_v_w_mod': 'new_v', 'new_v_b_mod': 'new_v', 'new_v_w_qkv': 'new_v', 'new_v_w_o_attn': 'new_v', 'new_v_w_in_ssm': 'new_v', 'new_v_a_re': 'new_v', 'new_v_a_im': 'new_v', 'new_v_log_dt': 'new_v', 'new_v_b_re': 'new_v', 'new_v_b_im': 'new_v', 'new_v_c_re': 'new_v', 'new_v_c_im': 'new_v', 'new_v_d_skip': 'new_v', 'new_v_w_glu': 'new_v', 'new_v_b_glu': 'new_v', 'new_v_w_o_ssm': 'new_v', 'new_v_w_up': 'new_v', 'new_v_conv_w': 'new_v', 'new_v_conv_b': 'new_v', 'new_v_w_down': 'new_v', 'new_v_norm_out': 'new_v', 'new_v_w_fin': 'new_v', 'new_v_b_fin': 'new_v'}


def _forward(args):
    return _fwd_reference(*[args[k] for k in FWD_PARAMS])


def _output_shape():
    out = _jax.eval_shape(lambda: _forward(_fwd_setup_inputs(0)))
    return out.shape, out.dtype

N_MICROBATCH = 1
ADAM_LR = 0.001
ADAM_B1 = 0.9
ADAM_B2 = 0.999
ADAM_EPS = 1e-08
ADAM_WD = 0.01
ADAM_STEP = 10
PER_EXAMPLE_BATCH_AXIS = {'x': 0, 'c': 0, 'loss_target': 0}
SHARED_INPUTS = []
_WEIGHT_DTYPES = {'norm_mix': _jnp.float32, 'norm_ffn': _jnp.float32, 'w_mod': _jnp.float32, 'b_mod': _jnp.float32, 'w_qkv': _jnp.float32, 'w_o_attn': _jnp.float32, 'w_in_ssm': _jnp.float32, 'a_re': _jnp.float32, 'a_im': _jnp.float32, 'log_dt': _jnp.float32, 'b_re': _jnp.float32, 'b_im': _jnp.float32, 'c_re': _jnp.float32, 'c_im': _jnp.float32, 'd_skip': _jnp.float32, 'w_glu': _jnp.float32, 'b_glu': _jnp.float32, 'w_o_ssm': _jnp.float32, 'w_up': _jnp.float32, 'conv_w': _jnp.float32, 'conv_b': _jnp.float32, 'w_down': _jnp.float32, 'norm_out': _jnp.float32, 'w_fin': _jnp.float32, 'b_fin': _jnp.float32}
MOMENT_SCALE = {'norm_mix': 5.828571e-02, 'norm_ffn': 8.079048e-02, 'w_mod': 6.232426e-01, 'b_mod': 1.044511e+00, 'w_qkv': 2.348015e-01, 'w_o_attn': 4.234537e-01, 'w_in_ssm': 1.303545e-01, 'a_re': 3.998197e-02, 'a_im': 4.227341e-02, 'log_dt': 5.496307e+00, 'b_re': 3.268720e-02, 'b_im': 3.396746e-02, 'c_re': 1.807967e-02, 'c_im': 1.694121e-02, 'd_skip': 3.203510e-01, 'w_glu': 6.283198e-02, 'b_glu': 1.590968e-01, 'w_o_ssm': 3.422161e-01, 'w_up': 5.826724e-02, 'conv_w': 6.386339e-02, 'conv_b': 2.059892e-01, 'w_down': 1.203441e-01, 'norm_out': 3.692600e+01, 'w_fin': 8.774996e+00, 'b_fin': 2.338789e+01}


def _to_microbatches(a, axis):
    t = _jnp.moveaxis(a, axis, 0)
    t = t.reshape((N_MICROBATCH, t.shape[0] // N_MICROBATCH) + t.shape[1:])
    return _jnp.moveaxis(t, 1, axis + 1)


def setup_inputs(seed: int = 0) -> dict:
    inp = _fwd_setup_inputs(seed)
    key = _jax.random.fold_in(_jax.random.key(seed), 7919)
    shape, _ = _output_shape()
    out = dict(inp)
    out["loss_target"] = _jax.random.normal(_jax.random.fold_in(key, 0), shape, _jnp.float32)
    for i, name in enumerate(TWIN_WEIGHTS):
        w = inp[name].astype(_jnp.float32)
        if MOMENT_SCALE is None:
            s = _jnp.sqrt(_jnp.mean(_jnp.square(w)) + 1e-30)
        else:
            s = MOMENT_SCALE[name]
        km, kv = _jax.random.split(_jax.random.fold_in(key, i + 1))
        out[name] = w
        out["m_" + name] = s * _jax.random.normal(km, w.shape, _jnp.float32)
        out["v_" + name] = (s * s) * _jax.random.uniform(kv, w.shape, _jnp.float32, 0.5, 1.5)
    if N_MICROBATCH > 1:
        for name, axis in PER_EXAMPLE_BATCH_AXIS.items():
            out[name] = _to_microbatches(out[name], axis)
    return {'x': out['x'], 'c': out['c'], 'norm_mix': out['norm_mix'], 'norm_ffn': out['norm_ffn'], 'w_mod': out['w_mod'], 'b_mod': out['b_mod'], 'w_qkv': out['w_qkv'], 'w_o_attn': out['w_o_attn'], 'w_in_ssm': out['w_in_ssm'], 'a_re': out['a_re'], 'a_im': out['a_im'], 'log_dt': out['log_dt'], 'b_re': out['b_re'], 'b_im': out['b_im'], 'c_re': out['c_re'], 'c_im': out['c_im'], 'd_skip': out['d_skip'], 'w_glu': out['w_glu'], 'b_glu': out['b_glu'], 'w_o_ssm': out['w_o_ssm'], 'w_up': out['w_up'], 'conv_w': out['conv_w'], 'conv_b': out['conv_b'], 'w_down': out['w_down'], 'norm_out': out['norm_out'], 'w_fin': out['w_fin'], 'b_fin': out['b_fin'], 'loss_target': out['loss_target'], 'm_norm_mix': out['m_norm_mix'], 'm_norm_ffn': out['m_norm_ffn'], 'm_w_mod': out['m_w_mod'], 'm_b_mod': out['m_b_mod'], 'm_w_qkv': out['m_w_qkv'], 'm_w_o_attn': out['m_w_o_attn'], 'm_w_in_ssm': out['m_w_in_ssm'], 'm_a_re': out['m_a_re'], 'm_a_im': out['m_a_im'], 'm_log_dt': out['m_log_dt'], 'm_b_re': out['m_b_re'], 'm_b_im': out['m_b_im'], 'm_c_re': out['m_c_re'], 'm_c_im': out['m_c_im'], 'm_d_skip': out['m_d_skip'], 'm_w_glu': out['m_w_glu'], 'm_b_glu': out['m_b_glu'], 'm_w_o_ssm': out['m_w_o_ssm'], 'm_w_up': out['m_w_up'], 'm_conv_w': out['m_conv_w'], 'm_conv_b': out['m_conv_b'], 'm_w_down': out['m_w_down'], 'm_norm_out': out['m_norm_out'], 'm_w_fin': out['m_w_fin'], 'm_b_fin': out['m_b_fin'], 'v_norm_mix': out['v_norm_mix'], 'v_norm_ffn': out['v_norm_ffn'], 'v_w_mod': out['v_w_mod'], 'v_b_mod': out['v_b_mod'], 'v_w_qkv': out['v_w_qkv'], 'v_w_o_attn': out['v_w_o_attn'], 'v_w_in_ssm': out['v_w_in_ssm'], 'v_a_re': out['v_a_re'], 'v_a_im': out['v_a_im'], 'v_log_dt': out['v_log_dt'], 'v_b_re': out['v_b_re'], 'v_b_im': out['v_b_im'], 'v_c_re': out['v_c_re'], 'v_c_im': out['v_c_im'], 'v_d_skip': out['v_d_skip'], 'v_w_glu': out['v_w_glu'], 'v_b_glu': out['v_b_glu'], 'v_w_o_ssm': out['v_w_o_ssm'], 'v_w_up': out['v_w_up'], 'v_conv_w': out['v_conv_w'], 'v_conv_b': out['v_conv_b'], 'v_w_down': out['v_w_down'], 'v_norm_out': out['v_norm_out'], 'v_w_fin': out['v_w_fin'], 'v_b_fin': out['v_b_fin']}


def _loss(weights, diff, rest, loss_target):
    with _jax.named_scope("forward"):
        args = {**rest, TWIN_DIFF_INPUT: diff, **{k: w.astype(_WEIGHT_DTYPES[k]) for k, w in weights.items()}}
        y = _forward(args)
    with _jax.named_scope("loss_head"):
        err = _jnp.square(y.astype(_jnp.float32) - loss_target)
        return 0.5 * _jnp.sum(_jnp.mean(err, axis=-1)) if err.ndim else 0.5 * err


def _adamw(w, g, m, v):
    m = ADAM_B1 * m + (1.0 - ADAM_B1) * g
    v = ADAM_B2 * v + (1.0 - ADAM_B2) * _jnp.square(g)
    m_hat = m / (1.0 - ADAM_B1 ** ADAM_STEP)
    v_hat = v / (1.0 - ADAM_B2 ** ADAM_STEP)
    delta = -ADAM_LR * (m_hat / (_jnp.sqrt(v_hat) + ADAM_EPS) + ADAM_WD * w)
    return delta, m, v


def reference(x, c, norm_mix, norm_ffn, w_mod, b_mod, w_qkv, w_o_attn, w_in_ssm, a_re, a_im, log_dt, b_re, b_im, c_re, c_im, d_skip, w_glu, b_glu, w_o_ssm, w_up, conv_w, conv_b, w_down, norm_out, w_fin, b_fin, loss_target, m_norm_mix, m_norm_ffn, m_w_mod, m_b_mod, m_w_qkv, m_w_o_attn, m_w_in_ssm, m_a_re, m_a_im, m_log_dt, m_b_re, m_b_im, m_c_re, m_c_im, m_d_skip, m_w_glu, m_b_glu, m_w_o_ssm, m_w_up, m_conv_w, m_conv_b, m_w_down, m_norm_out, m_w_fin, m_b_fin, v_norm_mix, v_norm_ffn, v_w_mod, v_b_mod, v_w_qkv, v_w_o_attn, v_w_in_ssm, v_a_re, v_a_im, v_log_dt, v_b_re, v_b_im, v_c_re, v_c_im, v_d_skip, v_w_glu, v_b_glu, v_w_o_ssm, v_w_up, v_conv_w, v_conv_b, v_w_down, v_norm_out, v_w_fin, v_b_fin):
    given = dict(x=x, c=c, norm_mix=norm_mix, norm_ffn=norm_ffn, w_mod=w_mod, b_mod=b_mod, w_qkv=w_qkv, w_o_attn=w_o_attn, w_in_ssm=w_in_ssm, a_re=a_re, a_im=a_im, log_dt=log_dt, b_re=b_re, b_im=b_im, c_re=c_re, c_im=c_im, d_skip=d_skip, w_glu=w_glu, b_glu=b_glu, w_o_ssm=w_o_ssm, w_up=w_up, conv_w=conv_w, conv_b=conv_b, w_down=w_down, norm_out=norm_out, w_fin=w_fin, b_fin=b_fin, loss_target=loss_target, m_norm_mix=m_norm_mix, m_norm_ffn=m_norm_ffn, m_w_mod=m_w_mod, m_b_mod=m_b_mod, m_w_qkv=m_w_qkv, m_w_o_attn=m_w_o_attn, m_w_in_ssm=m_w_in_ssm, m_a_re=m_a_re, m_a_im=m_a_im, m_log_dt=m_log_dt, m_b_re=m_b_re, m_b_im=m_b_im, m_c_re=m_c_re, m_c_im=m_c_im, m_d_skip=m_d_skip, m_w_glu=m_w_glu, m_b_glu=m_b_glu, m_w_o_ssm=m_w_o_ssm, m_w_up=m_w_up, m_conv_w=m_conv_w, m_conv_b=m_conv_b, m_w_down=m_w_down, m_norm_out=m_norm_out, m_w_fin=m_w_fin, m_b_fin=m_b_fin, v_norm_mix=v_norm_mix, v_norm_ffn=v_norm_ffn, v_w_mod=v_w_mod, v_b_mod=v_b_mod, v_w_qkv=v_w_qkv, v_w_o_attn=v_w_o_attn, v_w_in_ssm=v_w_in_ssm, v_a_re=v_a_re, v_a_im=v_a_im, v_log_dt=v_log_dt, v_b_re=v_b_re, v_b_im=v_b_im, v_c_re=v_c_re, v_c_im=v_c_im, v_d_skip=v_d_skip, v_w_glu=v_w_glu, v_b_glu=v_b_glu, v_w_o_ssm=v_w_o_ssm, v_w_up=v_w_up, v_conv_w=v_conv_w, v_conv_b=v_conv_b, v_w_down=v_w_down, v_norm_out=v_norm_out, v_w_fin=v_w_fin, v_b_fin=v_b_fin)
    weights = {n: given[n] for n in TWIN_WEIGHTS}
    shared = {n: given[n] for n in SHARED_INPUTS}
    per_example = {n: given[n] for n in ['x', 'c']}
    grad_fn = _jax.value_and_grad(_loss, argnums=(0, 1))

    def one_microbatch(ex, loss_target):
        ex = dict(ex)
        diff = ex.pop(TWIN_DIFF_INPUT)
        return grad_fn(weights, diff, {**shared, **ex}, loss_target)

    if N_MICROBATCH == 1:
        loss, (grad_w, grad_x) = one_microbatch(per_example, given["loss_target"])
    else:
        def body(carry, xs):
            loss_sum, grad_sum = carry
            l_k, (gw_k, gx_k) = one_microbatch(xs[0], xs[1])
            with _jax.named_scope("update"):
                return (loss_sum + l_k, _jax.tree.map(_jnp.add, grad_sum, gw_k)), gx_k

        init = (_jnp.zeros((), _jnp.float32), _jax.tree.map(_jnp.zeros_like, weights))
        (loss, grad_w), grad_x = _jax.lax.scan(body, init, (per_example, given["loss_target"]))
    with _jax.named_scope("update"):
        delta_w, new_m, new_v = {}, {}, {}
        for n in TWIN_WEIGHTS:
            delta_w[n], new_m[n], new_v[n] = _adamw(weights[n], grad_w[n], given["m_" + n], given["v_" + n])
    return (loss, grad_x, *[grad_w[n] for n in TWIN_WEIGHTS], *[delta_w[n] for n in TWIN_WEIGHTS],
            *[new_m[n] for n in TWIN_WEIGHTS], *[new_v[n] for n in TWIN_WEIGHTS])
```

```python
import functools
import math

import jax
import jax.numpy as jnp
from jax import lax
from jax.experimental import pallas as pl
from jax.experimental.pallas import tpu as pltpu

F32 = jnp.float32
BF16 = jnp.bfloat16
MESH = pl.DeviceIdType.MESH

HEAD_DIM = 64
SSM_GROUP = 16
STATE = 64
GROUPS_PER_BLOCK = 8
SEGMENTS = 8
EPS = 1e-6
ADAM_LR = 0.001
ADAM_B1 = 0.9
ADAM_B2 = 0.999
ADAM_EPS = 1e-08
ADAM_WD = 0.01
ADAM_STEP = 10
N_CHIPS = 4
N_DEV = 8
V7X_VMEM_LIMIT = 56 * 1024 * 1024
ATT_BLOCK = 128


def _tile(n, prefs):
    for p in prefs:
        if n % p == 0:
            return p
    return n


def _params(sem, vmem=V7X_VMEM_LIMIT):
    return pltpu.CompilerParams(dimension_semantics=sem, vmem_limit_bytes=vmem)


def matmul(a, b, *, ta=False, tb=False, bias=None, out_dtype=F32, name):
    if ta:
        K, M = a.shape
    else:
        M, K = a.shape
    if tb:
        N, Kb = b.shape
    else:
        Kb, N = b.shape
    assert K == Kb, (a.shape, b.shape, ta, tb)
    tm = _tile(M, (512, 256, 128))
    tn = _tile(N, (512, 256, 128))
    tk = _tile(K, (512, 256, 128))
    nk = K // tk
    dims = (((0,) if ta else (1,), (1,) if tb else (0,)), ((), ()))

    def body(*refs):
        if bias is None:
            a_ref, b_ref, o_ref, acc_ref = refs
            bias_ref = None
        else:
            a_ref, b_ref, bias_ref, o_ref, acc_ref = refs
        k = pl.program_id(2)

        @pl.when(k == 0)
        def _():
            acc_ref[...] = jnp.zeros_like(acc_ref)

        acc_ref[...] += lax.dot_general(a_ref[...].astype(BF16), b_ref[...].astype(BF16), dims,
                                        preferred_element_type=F32)

        @pl.when(k == nk - 1)
        def _():
            r = acc_ref[...]
            if bias_ref is not None:
                r = r + bias_ref[...]
            o_ref[...] = r.astype(o_ref.dtype)

    a_spec = pl.BlockSpec((tk, tm), lambda i, j, k: (k, i)) if ta else pl.BlockSpec((tm, tk), lambda i, j, k: (i, k))
    b_spec = pl.BlockSpec((tn, tk), lambda i, j, k: (j, k)) if tb else pl.BlockSpec((tk, tn), lambda i, j, k: (k, j))
    in_specs = [a_spec, b_spec]
    args = [a, b]
    if bias is not None:
        in_specs.append(pl.BlockSpec((1, tn), lambda i, j, k: (0, j)))
        args.append(bias.reshape(1, N).astype(F32))
    return pl.pallas_call(
        body, name=name,
        out_shape=jax.ShapeDtypeStruct((M, N), out_dtype),
        grid=(M // tm, N // tn, nk),
        in_specs=in_specs,
        out_specs=pl.BlockSpec((tm, tn), lambda i, j, k: (i, j)),
        scratch_shapes=[pltpu.VMEM((tm, tn), F32)],
        compiler_params=_params(("parallel", "parallel", "arbitrary")),
    )(*args)


def rowwise(fn, tiled, per_seq, glob, out_tiled, out_seq, *, B, S, name, rows=256):
    tm = _tile(S, (rows, 128, 64, 32, 16, 8))
    nt = S // tm
    n_in = len(tiled) + len(per_seq) + len(glob)
    n_ot = len(out_tiled)

    def body(*refs):
        ins = refs[:n_in]
        outs = refs[n_in:]
        vals = fn(*[r[...] for r in ins])
        if not isinstance(vals, (tuple, list)):
            vals = (vals,)
        assert len(vals) == len(outs), (name, len(vals), len(outs))
        for o_ref, v in zip(outs[:n_ot], vals[:n_ot]):
            o_ref[...] = v.astype(o_ref.dtype)
        t = pl.program_id(1)
        for o_ref, v in zip(outs[n_ot:], vals[n_ot:]):
            def first(o_ref=o_ref, v=v):
                o_ref[...] = v.astype(F32)

            def later(o_ref=o_ref, v=v):
                o_ref[...] += v.astype(F32)

            pl.when(t == 0)(first)
            pl.when(t > 0)(later)

    in_specs = [pl.BlockSpec((tm, a.shape[1]), lambda b, t: (b * nt + t, 0)) for a in tiled]
    in_specs += [pl.BlockSpec((None, 1, a.shape[1]), lambda b, t: (b, 0, 0)) for a in per_seq]
    in_specs += [pl.BlockSpec(a.shape, lambda b, t: (0,) * a.ndim) for a in glob]
    out_shape = [jax.ShapeDtypeStruct((B * S, w), dt) for w, dt in out_tiled]
    out_shape += [jax.ShapeDtypeStruct((B, 1, w), F32) for w in out_seq]
    out_specs = [pl.BlockSpec((tm, w), lambda b, t: (b * nt + t, 0)) for w, _ in out_tiled]
    out_specs += [pl.BlockSpec((None, 1, w), lambda b, t: (b, 0, 0)) for w in out_seq]
    res = pl.pallas_call(
        body, name=name, out_shape=out_shape, grid=(B, nt), in_specs=in_specs, out_specs=out_specs,
        compiler_params=_params(("parallel", "arbitrary")),
    )(*tiled, *[a.reshape(B, 1, a.shape[1]) for a in per_seq], *glob)
    res = list(res)
    for i in range(n_ot, len(res)):
        res[i] = res[i].reshape(B, res[i].shape[-1])
    return res


def _rms(x):
    r = lax.rsqrt(jnp.mean(x * x, axis=-1, keepdims=True) + EPS)
    return x * r, r


def norm_mod_fwd(x, g, sh, sc, *, B, S, name):
    def fn(x, sh, sc, g):
        xn, _ = _rms(x)
        return (xn * g) * (1.0 + sc) + sh

    return rowwise(fn, [x], [sh, sc], [g.reshape(1, -1)], [(x.shape[1], BF16)], [], B=B, S=S, name=name)[0]


def _norm_mod_bwd_math(dh, x, sc, g):
    xn, r = _rms(x)
    y = xn * g
    dy = dh * (1.0 + sc)
    dxn = dy * g
    dx = r * (dxn - xn * jnp.mean(dxn * xn, axis=-1, keepdims=True))
    dsh = jnp.sum(dh, axis=0, keepdims=True)
    dsc = jnp.sum(dh * y, axis=0, keepdims=True)
    dg = jnp.sum(dy * xn, axis=0, keepdims=True)
    return dx, dsh, dsc, dg


def norm_mod_bwd(dh, x, dres, g, sc, *, B, S, name):
    D = x.shape[1]

    def fn(dh, x, dres, sc, g):
        dx, dsh, dsc, dg = _norm_mod_bwd_math(dh.astype(F32), x, sc, g)
        return dres + dx, dsh, dsc, dg

    return rowwise(fn, [dh, x, dres], [sc], [g.reshape(1, -1)], [(D, F32)], [D, D, D], B=B, S=S, name=name)


def gate_res_fwd(x, y, gate, *, B, S, name):
    return rowwise(lambda x, y, g: x + g * y, [x, y], [gate], [], [(x.shape[1], F32)], [], B=B, S=S, name=name)[0]


def gate_res_bwd(dx, y, gate, *, B, S, name):
    D = dx.shape[1]

    def fn(dx, y, g):
        return g * dx, jnp.sum(dx * y, axis=0, keepdims=True)

    return rowwise(fn, [dx, y], [gate], [], [(D, BF16)], [D], B=B, S=S, name=name)


def final_loss(x, tgt, g, sh, sc, *, B, S, name):
    D = x.shape[1]

    def fn(x, tgt, sh, sc, g):
        xn, _ = _rms(x)
        y = (xn * g) * (1.0 + sc) + sh
        err = y - tgt
        loss = 0.5 * jnp.sum(err * err, axis=0, keepdims=True) * (1.0 / D)
        dx, dsh, dsc, dg = _norm_mod_bwd_math(err * (1.0 / D), x, sc, g)
        return dx, loss, dsh, dsc, dg

    return rowwise(fn, [x, tgt], [sh, sc], [g.reshape(1, -1)], [(D, F32)], [D, D, D, D], B=B, S=S, name=name)


def _gelu(y):
    c0 = math.sqrt(2.0 / math.pi)
    t = jnp.tanh(c0 * (y + 0.044715 * (y * y * y)))
    return 0.5 * y * (1.0 + t), t


def _sigmoid(s):
    return 1.0 / (1.0 + jnp.exp(-s))


def gelu_fwd(y, *, B, S, name):
    return rowwise(lambda y: _gelu(y)[0], [y], [], [], [(y.shape[1], BF16)], [], B=B, S=S, name=name)[0]


def glu_fwd(y, s, *, B, S, name):
    return rowwise(lambda y, s: _gelu(y)[0] * _sigmoid(s), [y, s], [], [], [(y.shape[1], BF16)], [], B=B, S=S,
                   name=name)[0]


def glu_bwd1(y, s, dg, *, B, S, name):
    D = y.shape[1]

    def fn(y, s, dg):
        z = _gelu(y)[0]
        sig = _sigmoid(s)
        ds = dg * z * sig * (1.0 - sig)
        return ds, dg * sig, jnp.sum(ds, axis=0, keepdims=True)

    return rowwise(fn, [y, s, dg], [], [], [(D, BF16), (D, F32)], [D], B=B, S=S, name=name)


def glu_bwd2(y, dz1, dz2, *, B, S, name):
    D = y.shape[1]
    c0 = math.sqrt(2.0 / math.pi)

    def fn(y, dz1, dz2):
        _, t = _gelu(y)
        dgelu = 0.5 * (1.0 + t) + 0.5 * y * (1.0 - t * t) * c0 * (1.0 + 3.0 * 0.044715 * y * y)
        return (dz1 + dz2) * dgelu

    return rowwise(fn, [y, dz1, dz2], [], [], [(D, F32)], [], B=B, S=S, name=name)[0]


def silu_rows(c, *, name):
    R, W = c.shape
    return rowwise(lambda c: c * _sigmoid(c), [c], [], [], [(W, F32)], [], B=1, S=R, name=name)[0]


def _shift_down(cur, h6, h7):
    rows = lax.broadcasted_iota(jnp.int32, cur.shape, 0)
    m1 = jnp.where(rows == 0, h7, pltpu.roll(cur, 1, 0))
    m2 = jnp.where(rows == 0, h6, jnp.where(rows == 1, h7, pltpu.roll(cur, 2, 0)))
    return m1, m2


def _conv3(cur, halo_ref, w_ref, has_prev):
    h6 = jnp.where(has_prev, halo_ref[6:7, :], 0.0)
    h7 = jnp.where(has_prev, halo_ref[7:8, :], 0.0)
    m1, m2 = _shift_down(cur, h6, h7)
    return w_ref[2:3, :] * cur + w_ref[1:2, :] * m1 + w_ref[0:1, :] * m2, m1, m2


def _conv_tiles(S, F):
    ts = _tile(S, (256, 128, 64, 32, 16, 8))
    tn = _tile(F, (256, 128))
    return ts, tn, S // ts, F // tn


def conv_gate_fwd(up, cw, cb, *, B, S, name):
    F = up.shape[1] // 2
    ts, tn, nts, nF = _conv_tiles(S, F)
    hb = ts // 8

    def body(g_ref, gh_ref, v_ref, vh_ref, wg_ref, wv_ref, bg_ref, bv_ref, o_ref):
        has_prev = pl.program_id(2) > 0
        gc = _conv3(g_ref[...], gh_ref, wg_ref, has_prev)[0] + bg_ref[...]
        vc = _conv3(v_ref[...], vh_ref, wv_ref, has_prev)[0] + bv_ref[...]
        o_ref[...] = (gc * _sigmoid(gc) * vc).astype(o_ref.dtype)

    def cur(off):
        return pl.BlockSpec((ts, tn), lambda b, j, t: (b * nts + t, j + off))

    def halo(off):
        return pl.BlockSpec((8, tn), lambda b, j, t: (jnp.maximum((b * nts + t) * hb - 1, 0), j + off))

    def vec(rows, off):
        return pl.BlockSpec((rows, tn), lambda b, j, t: (0, j + off))

    return pl.pallas_call(
        body, name=name, out_shape=jax.ShapeDtypeStruct((B * S, F), BF16), grid=(B, nF, nts),
        in_specs=[cur(0), halo(0), cur(nF), halo(nF), vec(3, 0), vec(3, nF), vec(1, 0), vec(1, nF)],
        out_specs=pl.BlockSpec((ts, tn), lambda b, j, t: (b * nts + t, j)),
        compiler_params=_params(("parallel", "parallel", "arbitrary")),
    )(up, up, up, up, cw, cw, cb, cb)


def conv_gate_bwd1(up, dact, cw, cb, *, B, S, name):
    F = up.shape[1] // 2
    ts, tn, nts, nF = _conv_tiles(S, F)
    hb = ts // 8

    def body(g_ref, gh_ref, v_ref, vh_ref, da_ref, wg_ref, wv_ref, bg_ref, bv_ref, d_ref, db_ref):
        t = pl.program_id(2)
        has_prev = t > 0
        gc = _conv3(g_ref[...], gh_ref, wg_ref, has_prev)[0] + bg_ref[...]
        vc = _conv3(v_ref[...], vh_ref, wv_ref, has_prev)[0] + bv_ref[...]
        sig = _sigmoid(gc)
        da = da_ref[...]
        dg = da * vc * (sig * (1.0 + gc * (1.0 - sig)))
        dv = da * (gc * sig)
        d_ref[0] = dg
        d_ref[1] = dv
        part = jnp.concatenate([jnp.sum(dg, axis=0, keepdims=True), jnp.sum(dv, axis=0, keepdims=True)], axis=0)

        @pl.when(t == 0)
        def _():
            db_ref[...] = part

        @pl.when(t > 0)
        def _():
            db_ref[...] += part

    def cur(off):
        return pl.BlockSpec((ts, tn), lambda b, j, t: (b * nts + t, j + off))

    def halo(off):
        return pl.BlockSpec((8, tn), lambda b, j, t: (jnp.maximum((b * nts + t) * hb - 1, 0), j + off))

    def vec(rows, off):
        return pl.BlockSpec((rows, tn), lambda b, j, t: (0, j + off))

    return pl.pallas_call(
        body, name=name,
        out_shape=[jax.ShapeDtypeStruct((2, B * S, F), F32), jax.ShapeDtypeStruct((B, 2, F), F32)],
        grid=(B, nF, nts),
        in_specs=[cur(0), halo(0), cur(nF), halo(nF), cur(0), vec(3, 0), vec(3, nF), vec(1, 0), vec(1, nF)],
        out_specs=[pl.BlockSpec((2, ts, tn), lambda b, j, t: (0, b * nts + t, j)),
                   pl.BlockSpec((None, 2, tn), lambda b, j, t: (b, 0, j))],
        compiler_params=_params(("parallel", "parallel", "arbitrary")),
    )(up, up, up, up, dact, cw, cw, cb, cb)


def conv_bwd2(d3, up, cw, *, B, S, name):
    F = up.shape[1] // 2
    ts, tn, nts, nF = _conv_tiles(S, F)
    hb = ts // 8
    last8 = B * S // 8 - 1

    def body(d_ref, da_ref, u_ref, uh_ref, w_ref, o_ref, dw_ref):
        t = pl.program_id(3)
        d = d_ref[...]
        has_next = t < nts - 1
        a0 = jnp.where(has_next, da_ref[0:1, :], 0.0)
        a1 = jnp.where(has_next, da_ref[1:2, :], 0.0)
        rows = lax.broadcasted_iota(jnp.int32, d.shape, 0)
        p1 = jnp.where(rows == ts - 1, a0, pltpu.roll(d, ts - 1, 0))
        p2 = jnp.where(rows == ts - 1, a1, jnp.where(rows == ts - 2, a0, pltpu.roll(d, ts - 2, 0)))
        o_ref[...] = (w_ref[2:3, :] * d + w_ref[1:2, :] * p1 + w_ref[0:1, :] * p2).astype(o_ref.dtype)
        u = u_ref[...]
        has_prev = t > 0
        h6 = jnp.where(has_prev, uh_ref[6:7, :], 0.0)
        h7 = jnp.where(has_prev, uh_ref[7:8, :], 0.0)
        m1, m2 = _shift_down(u, h6, h7)
        part = jnp.concatenate([jnp.sum(d * m2, axis=0, keepdims=True), jnp.sum(d * m1, axis=0, keepdims=True),
                                jnp.sum(d * u, axis=0, keepdims=True)], axis=0)

        @pl.when(t == 0)
        def _():
            dw_ref[...] = part

        @pl.when(t > 0)
        def _():
            dw_ref[...] += part

    return pl.pallas_call(
        body, name=name,
        out_shape=[jax.ShapeDtypeStruct((B * S, 2 * F), BF16), jax.ShapeDtypeStruct((B, 3, 2 * F), F32)],
        grid=(B, 2, nF, nts),
        in_specs=[
            pl.BlockSpec((None, ts, tn), lambda b, g, j, t: (g, b * nts + t, j)),
            pl.BlockSpec((None, 8, tn), lambda b, g, j, t: (g, jnp.minimum((b * nts + t + 1) * hb, last8), j)),
            pl.BlockSpec((ts, tn), lambda b, g, j, t: (b * nts + t, g * nF + j)),
            pl.BlockSpec((8, tn), lambda b, g, j, t: (jnp.maximum((b * nts + t) * hb - 1, 0), g * nF + j)),
            pl.BlockSpec((3, tn), lambda b, g, j, t: (0, g * nF + j)),
        ],
        out_specs=[pl.BlockSpec((ts, tn), lambda b, g, j, t: (b * nts + t, g * nF + j)),
                   pl.BlockSpec((None, 3, tn), lambda b, g, j, t: (b, 0, g * nF + j))],
        compiler_params=_params(("parallel", "parallel", "parallel", "arbitrary")),
    )(d3, d3, up, up, cw)


def _split_dot(x, m):
    hi = x.astype(BF16)
    lo = (x - hi.astype(F32)).astype(BF16)
    return jnp.dot(hi, m, preferred_element_type=F32) + jnp.dot(lo, m, preferred_element_type=F32)


def _nt(a, b):
    return lax.dot_general(a, b, (((1,), (1,)), ((), ())), preferred_element_type=F32)


def _tn(a, b):
    return lax.dot_general(a, b, (((0,), (0,)), ((), ())), preferred_element_type=F32)


def _att_scores(q, k, q0, k0):
    T = ATT_BLOCK
    z = _nt(q, k) * (HEAD_DIM ** -0.5)
    qpos = q0 + lax.broadcasted_iota(jnp.int32, (T, T), 0)
    kpos = k0 + lax.broadcasted_iota(jnp.int32, (T, T), 1)
    mask = kpos < qpos
    e = jnp.exp(-jnp.abs(z))
    sp = jnp.log1p(e)
    lb = jnp.minimum(z, 0.0) - sp
    l1 = jnp.where(mask, jnp.minimum(-z, 0.0) - sp, 0.0)
    return z, lb, l1, mask, e


def _col_to_row(col, eye):
    return jnp.sum(jnp.where(eye, col, 0.0), axis=0, keepdims=True)


def _row_to_col(row, eye):
    return jnp.sum(jnp.where(eye, row, 0.0), axis=1, keepdims=True)


def attn_fwd(q, k, v, *, name):
    B, H, S, dh = q.shape
    T = ATT_BLOCK
    nq = S // T

    def body(q_ref, k_ref, v_ref, o_ref, l_ref):
        r = lax.broadcasted_iota(jnp.int32, (T, T), 0)
        c = lax.broadcasted_iota(jnp.int32, (T, T), 1)
        later = (r > c).astype(BF16)
        eye = r == c
        blk = lax.broadcasted_iota(jnp.int32, (nq, T), 0)

        def qblock(i, totals):
            q0 = pl.multiple_of(i * T, T)
            qb = q_ref[pl.ds(q0, T), :]

            def kblock(jj, st):
                acc, run = st
                k0 = pl.multiple_of((i - jj) * T, T)
                kb = k_ref[pl.ds(k0, T), :]
                vb = v_ref[pl.ds(k0, T), :]
                _, lb, l1, mask, _ = _att_scores(qb, kb, q0, k0)
                suf = _split_dot(l1, later) + run
                w = jnp.where(mask, jnp.exp(lb + suf), 0.0)
                acc = acc + jnp.dot(w.astype(BF16), vb, preferred_element_type=F32)
                return acc, run + jnp.sum(l1, axis=1, keepdims=True)

            acc, run = lax.fori_loop(0, i + 1, kblock, (jnp.zeros((T, dh), F32), jnp.zeros((T, 1), F32)))
            o_ref[pl.ds(q0, T), :] = acc
            return jnp.where(blk == i, _col_to_row(run, eye), totals)

        l_ref[...] = lax.fori_loop(0, nq, qblock, jnp.zeros((nq, T), F32))

    spec = pl.BlockSpec((None, None, S, dh), lambda b, h: (b, h, 0, 0))
    lspec = pl.BlockSpec((None, None, nq, T), lambda b, h: (b, h, 0, 0))
    return pl.pallas_call(
        body, name=name,
        out_shape=[jax.ShapeDtypeStruct((B, H, S, dh), F32), jax.ShapeDtypeStruct((B, H, nq, T), F32)], grid=(B, H),
        in_specs=[spec, spec, spec], out_specs=[spec, lspec],
        compiler_params=_params(("parallel", "parallel")),
    )(q, k, v)


def attn_bwd(q, k, v, ltot, do, *, name):
    B, H, S, dh = q.shape
    T = ATT_BLOCK
    nq = S // T
    scale = HEAD_DIM ** -0.5

    def body(q_ref, k_ref, v_ref, l_ref, do_ref, dq_ref, dk_ref, dv_ref, dk_acc, dv_acc):
        r = lax.broadcasted_iota(jnp.int32, (T, T), 0)
        c = lax.broadcasted_iota(jnp.int32, (T, T), 1)
        upto = (r <= c).astype(BF16)
        before = (r < c).astype(BF16)
        eye = r == c
        blk = lax.broadcasted_iota(jnp.int32, (nq, T), 0)
        totals = l_ref[...]
        dk_acc[...] = jnp.zeros_like(dk_acc)
        dv_acc[...] = jnp.zeros_like(dv_acc)

        def qblock(i, carry0):
            q0 = pl.multiple_of(i * T, T)
            qb = q_ref[pl.ds(q0, T), :]
            dob = do_ref[pl.ds(q0, T), :]
            total = _row_to_col(jnp.sum(jnp.where(blk == i, totals, 0.0), axis=0, keepdims=True), eye)

            def kblock(j, st):
                dq, run_l, run_d = st
                k0 = pl.multiple_of(j * T, T)
                kb = k_ref[pl.ds(k0, T), :]
                vb = v_ref[pl.ds(k0, T), :]
                z, lb, l1, mask, e = _att_scores(qb, kb, q0, k0)
                suf = total - (_split_dot(l1, upto) + run_l)
                w = jnp.where(mask, jnp.exp(lb + suf), 0.0)
                dv_acc[pl.ds(k0, T), :] += _tn(w.astype(BF16), dob)
                dlw = _nt(dob, vb) * w
                dl1 = jnp.where(mask, _split_dot(dlw, before) + run_d, 0.0)
                inv = 1.0 / (1.0 + e)
                small = e * inv
                pos = z >= 0.0
                beta = jnp.where(pos, inv, small)
                omb = jnp.where(pos, small, inv)
                dz = ((dlw * omb - dl1 * beta) * scale).astype(BF16)
                dq = dq + jnp.dot(dz, kb, preferred_element_type=F32)
                dk_acc[pl.ds(k0, T), :] += _tn(dz, qb)
                return (dq, run_l + jnp.sum(l1, axis=1, keepdims=True),
                        run_d + jnp.sum(dlw, axis=1, keepdims=True))

            z1 = jnp.zeros((T, 1), F32)
            dq, _, _ = lax.fori_loop(0, i + 1, kblock, (jnp.zeros((T, dh), F32), z1, z1))
            dq_ref[pl.ds(q0, T), :] = dq.astype(dq_ref.dtype)
            return carry0

        lax.fori_loop(0, nq, qblock, 0)
        dk_ref[...] = dk_acc[...].astype(dk_ref.dtype)
        dv_ref[...] = dv_acc[...].astype(dv_ref.dtype)

    spec = pl.BlockSpec((None, None, S, dh), lambda b, h: (b, h, 0, 0))
    lspec = pl.BlockSpec((None, None, nq, T), lambda b, h: (b, h, 0, 0))
    shp = jax.ShapeDtypeStruct((B, H, S, dh), BF16)
    return pl.pallas_call(
        body, name=name, out_shape=[shp, shp, shp], grid=(B, H),
        in_specs=[spec, spec, spec, lspec, spec], out_specs=[spec] * 3,
        scratch_shapes=[pltpu.VMEM((S, dh), F32), pltpu.VMEM((S, dh), F32)],
        compiler_params=_params(("parallel", "parallel")),
    )(q, k, v, ltot, do)


def _cmul(ar, ai, br, bi):
    return ar * br - ai * bi, ar * bi + ai * br


def _ssm_scan(sr, si, lr, li, n_steps, reverse):
    W = sr.shape[1]
    lim = -li if reverse else li
    zero = jnp.zeros((8, W), F32)

    def row(k):
        i = (n_steps - 1 - k) if reverse else k
        return pl.multiple_of(i * 8, 8)

    def local(k, st):
        cr, ci = st
        r0 = row(k)
        pr, pi = _cmul(lr, lim, cr, ci)
        nr = pr + sr[pl.ds(r0, 8), :]
        ni = pi + si[pl.ds(r0, 8), :]
        sr[pl.ds(r0, 8), :] = nr
        si[pl.ds(r0, 8), :] = ni
        return nr, ni

    er, ei = lax.fori_loop(0, n_steps, local, (zero, zero))

    def power(k, st):
        return _cmul(lr, lim, st[0], st[1])

    lnr, lni = lax.fori_loop(0, n_steps - 1, power, (lr, lim))
    rows = lax.broadcasted_iota(jnp.int32, (8, W), 0)
    cr, ci = zero, zero
    for step in range(1, SEGMENTS):
        tr, ti = _cmul(lnr, lni, cr, ci)
        tr, ti = tr + er, ti + ei
        if reverse:
            seg = SEGMENTS - 1 - step
            tr, ti = pltpu.roll(tr, SEGMENTS - 1, 0), pltpu.roll(ti, SEGMENTS - 1, 0)
        else:
            seg = step
            tr, ti = pltpu.roll(tr, 1, 0), pltpu.roll(ti, 1, 0)
        cr = jnp.where(rows == seg, tr, cr)
        ci = jnp.where(rows == seg, ti, ci)

    def fix(k, st):
        pr, pi = st
        r0 = row(k)
        ar, ai = _cmul(pr, pi, cr, ci)
        sr[pl.ds(r0, 8), :] += ar
        si[pl.ds(r0, 8), :] += ai
        return _cmul(lr, lim, pr, pi)

    lax.fori_loop(0, n_steps, fix, (lr, lim))
    return cr, ci


def _ssm_specs(S, W):
    CH = GROUPS_PER_BLOCK * SSM_GROUP
    return dict(
        rows=pl.BlockSpec((S, CH), lambda b, j: (b, j)),
        b=pl.BlockSpec((None, CH, W), lambda b, j: (j, 0, 0)),
        c=pl.BlockSpec((None, W, CH), lambda b, j: (j, 0, 0)),
        lam=pl.BlockSpec((None, 8, W), lambda b, j: (j, 0, 0)),
        vec=pl.BlockSpec((1, CH), lambda b, j: (0, j)),
    )


def ssm_fwd(u, bre, bim, cre, cim, lr8, li8, dsk, *, B, S, name):
    D = u.shape[1]
    J, CH, W = bre.shape
    n_steps = S // SEGMENTS
    sp = _ssm_specs(S, W)

    def body(u_ref, bre_ref, bim_ref, cre_ref, cim_ref, lr_ref, li_ref, dsk_ref, y_ref, sr, si):
        u = u_ref[...]
        ub = u.astype(BF16)
        sr[...] = jnp.dot(ub, bre_ref[...], preferred_element_type=F32)
        si[...] = jnp.dot(ub, bim_ref[...], preferred_element_type=F32)
        _ssm_scan(sr, si, lr_ref[...], li_ref[...], n_steps, False)
        y = jnp.dot(sr[...].astype(BF16), cre_ref[...], preferred_element_type=F32)
        y = y - jnp.dot(si[...].astype(BF16), cim_ref[...], preferred_element_type=F32)
        y_ref[...] = y + dsk_ref[...] * u

    return pl.pallas_call(
        body, name=name, out_shape=jax.ShapeDtypeStruct((B * S, D), F32), grid=(B, J),
        in_specs=[sp["rows"], sp["b"], sp["b"], sp["c"], sp["c"], sp["lam"], sp["lam"], sp["vec"]],
        out_specs=sp["rows"],
        scratch_shapes=[pltpu.VMEM((S, W), F32), pltpu.VMEM((S, W), F32)],
        compiler_params=_params(("parallel", "parallel")),
    )(u, bre, bim, cre, cim, lr8, li8, dsk)


def ssm_bwd(u, dy, bre, bim, cre, cim, lr8, li8, dsk, *, B, S, name):
    D = u.shape[1]
    J, CH, W = bre.shape
    n_steps = S // SEGMENTS
    sp = _ssm_specs(S, W)

    def body(u_ref, dy_ref, bre_ref, bim_ref, cre_ref, cim_ref, lr_ref, li_ref, dsk_ref,
             du_ref, dbre_ref, dbim_ref, dcre_ref, dcim_ref, dlr_ref, dli_ref, ddsk_ref, sr, si, ar, ai):
        u = u_ref[...]
        dy = dy_ref[...]
        ub = u.astype(BF16)
        dyb = dy.astype(BF16)
        lr, li = lr_ref[...], li_ref[...]
        sr[...] = jnp.dot(ub, bre_ref[...], preferred_element_type=F32)
        si[...] = jnp.dot(ub, bim_ref[...], preferred_element_type=F32)
        cr, ci = _ssm_scan(sr, si, lr, li, n_steps, False)
        ar[...] = _nt(dyb, cre_ref[...])
        ai[...] = -_nt(dyb, cim_ref[...])
        _ssm_scan(ar, ai, lr, li, n_steps, True)

        def dlam(k, st):
            dr, di = st
            r0 = pl.multiple_of((k + 1) * 8, 8)
            p0 = pl.multiple_of(k * 8, 8)
            pr, pi = sr[pl.ds(p0, 8), :], si[pl.ds(p0, 8), :]
            xr, xi = ar[pl.ds(r0, 8), :], ai[pl.ds(r0, 8), :]
            return dr + pr * xr + pi * xi, di + pr * xi - pi * xr

        xr, xi = ar[0:8, :], ai[0:8, :]
        dr, di = lax.fori_loop(0, n_steps - 1, dlam, (cr * xr + ci * xi, cr * xi - ci * xr))
        dlr_ref[...] = dr
        dli_ref[...] = di
        arb = ar[...].astype(BF16)
        aib = ai[...].astype(BF16)
        du_ref[...] = _nt(arb, bre_ref[...]) + _nt(aib, bim_ref[...]) + dsk_ref[...] * dy
        dbre_ref[...] = _tn(ub, arb)
        dbim_ref[...] = _tn(ub, aib)
        dcre_ref[...] = _tn(sr[...].astype(BF16), dyb)
        dcim_ref[...] = -_tn(si[...].astype(BF16), dyb)
        ddsk_ref[...] = jnp.sum(dy * u, axis=0, keepdims=True)

    def per(shape):
        return pl.BlockSpec((None, None) + shape, lambda b, j: (b, j, 0, 0))

    return pl.pallas_call(
        body, name=name,
        out_shape=[jax.ShapeDtypeStruct((B * S, D), F32),
                   jax.ShapeDtypeStruct((B, J, CH, W), F32), jax.ShapeDtypeStruct((B, J, CH, W), F32),
                   jax.ShapeDtypeStruct((B, J, W, CH), F32), jax.ShapeDtypeStruct((B, J, W, CH), F32),
                   jax.ShapeDtypeStruct((B, J, 8, W), F32), jax.ShapeDtypeStruct((B, J, 8, W), F32),
                   jax.ShapeDtypeStruct((B, J, 1, CH), F32)],
        grid=(B, J),
        in_specs=[sp["rows"], sp["rows"], sp["b"], sp["b"], sp["c"], sp["c"], sp["lam"], sp["lam"], sp["vec"]],
        out_specs=[sp["rows"], per((CH, W)), per((CH, W)), per((W, CH)), per((W, CH)), per((8, W)), per((8, W)),
                   per((1, CH))],
        scratch_shapes=[pltpu.VMEM((S, W), F32)] * 4,
        compiler_params=_params(("parallel", "parallel")),
    )(u, dy, bre, bim, cre, cim, lr8, li8, dsk)


def _ssm_discretize(a_re, a_im, log_dt, b_re, b_im):
    dt = jnp.exp(log_dt)[:, None]
    er = jnp.exp(a_re * dt)
    lr = er * jnp.cos(a_im * dt)
    li = er * jnp.sin(a_im * dt)
    den = a_re * a_re + a_im * a_im
    fr = ((lr - 1.0) * a_re + li * a_im) / den
    fi = (li * a_re - (lr - 1.0) * a_im) / den
    bbr = fr[..., None] * b_re - fi[..., None] * b_im
    bbi = fr[..., None] * b_im + fi[..., None] * b_re
    return lr, li, bbr, bbi


def _block_diag_in(m):
    G, P, H = m.shape
    J = G // GROUPS_PER_BLOCK
    m = m.reshape(J, GROUPS_PER_BLOCK, P, H).transpose(0, 1, 3, 2)
    eye = jnp.eye(GROUPS_PER_BLOCK, dtype=m.dtype)
    out = m[:, :, :, None, :] * eye[None, :, None, :, None]
    return out.reshape(J, GROUPS_PER_BLOCK * H, GROUPS_PER_BLOCK * P)


def _block_diag_in_grad(d, G, P, H):
    J = G // GROUPS_PER_BLOCK
    d = d.reshape(J, GROUPS_PER_BLOCK, H, GROUPS_PER_BLOCK, P)
    idx = jnp.arange(GROUPS_PER_BLOCK)
    d = d[:, idx, :, idx, :]
    return d.transpose(1, 0, 3, 2).reshape(G, P, H)


def _block_diag_out(m):
    G, H, P = m.shape
    J = G // GROUPS_PER_BLOCK
    m = m.reshape(J, GROUPS_PER_BLOCK, H, P).transpose(0, 1, 3, 2)
    eye = jnp.eye(GROUPS_PER_BLOCK, dtype=m.dtype)
    out = m[:, :, :, None, :] * eye[None, :, None, :, None]
    return out.reshape(J, GROUPS_PER_BLOCK * P, GROUPS_PER_BLOCK * H)


def _block_diag_out_grad(d, G, H, P):
    J = G // GROUPS_PER_BLOCK
    d = d.reshape(J, GROUPS_PER_BLOCK, P, GROUPS_PER_BLOCK, H)
    idx = jnp.arange(GROUPS_PER_BLOCK)
    d = d[:, idx, :, idx, :]
    return d.transpose(1, 0, 3, 2).reshape(G, H, P)


def _interleave(a, B, S):
    L = S // SEGMENTS
    return a.reshape(B, SEGMENTS, L, a.shape[-1]).transpose(0, 2, 1, 3).reshape(B * S, a.shape[-1])


def _deinterleave(a, B, S):
    L = S // SEGMENTS
    return a.reshape(B, L, SEGMENTS, a.shape[-1]).transpose(0, 2, 1, 3).reshape(B * S, a.shape[-1])


def _adamw_math(w, g, m, v):
    m = ADAM_B1 * m + (1.0 - ADAM_B1) * g
    v = ADAM_B2 * v + (1.0 - ADAM_B2) * (g * g)
    m_hat = m / (1.0 - ADAM_B1 ** ADAM_STEP)
    v_hat = v / (1.0 - ADAM_B2 ** ADAM_STEP)
    delta = -ADAM_LR * (m_hat / (jnp.sqrt(v_hat) + ADAM_EPS) + ADAM_WD * w)
    return delta, m, v


def adamw(w, g, m, v, *, name):
    R, C = w.shape
    tr = _tile(R, (max(8, (1 << 18) // C // 8 * 8), 256, 128, 64, 32, 16, 8))

    def body(w_ref, g_ref, m_ref, v_ref, d_ref, nm_ref, nv_ref):
        d, nm, nv = _adamw_math(w_ref[...], g_ref[...], m_ref[...], v_ref[...])
        d_ref[...] = d
        nm_ref[...] = nm
        nv_ref[...] = nv

    spec = pl.BlockSpec((tr, C), lambda i: (i, 0))
    shp = jax.ShapeDtypeStruct((R, C), F32)
    return pl.pallas_call(
        body, name=name, out_shape=[shp, shp, shp], grid=(R // tr,), in_specs=[spec] * 4, out_specs=[spec] * 3,
        compiler_params=_params(("parallel",)),
    )(w, g, m, v)


def sum_leading(a, *, name, out_dtype=F32):
    n, R, C = a.shape
    tr = _tile(R, (256, 128, 64, 32, 16, 8))

    def body(a_ref, o_ref):
        acc = a_ref[0].astype(F32)
        for i in range(1, n):
            acc = acc + a_ref[i].astype(F32)
        o_ref[...] = acc.astype(o_ref.dtype)

    return pl.pallas_call(
        body, name=name, out_shape=jax.ShapeDtypeStruct((R, C), out_dtype), grid=(R // tr,),
        in_specs=[pl.BlockSpec((n, tr, C), lambda i: (0, i, 0))], out_specs=pl.BlockSpec((tr, C), lambda i: (i, 0)),
        compiler_params=_params(("parallel",)),
    )(a)


def _any_specs(n):
    return [pl.BlockSpec(memory_space=pl.ANY) for _ in range(n)]


def _coords():
    return lax.axis_index("x"), lax.axis_index("y"), lax.axis_index("c")


def _flip(v, bit):
    return (v + bit) % 2


def all_gather8(a, *, name):
    shape = a.shape

    def body(a_ref, o_ref, send_sems, recv_sems, local_sem):
        x, y, c = _coords()
        me = 4 * x + 2 * y + c
        mine = pltpu.make_async_copy(a_ref, o_ref.at[me], local_sem)
        mine.start()
        sends = []
        for k in range(1, N_DEV):
            peer = (_flip(x, (k >> 2) & 1), _flip(y, (k >> 1) & 1), _flip(c, k & 1))
            cp = pltpu.make_async_remote_copy(a_ref, o_ref.at[me], send_sems.at[k - 1], recv_sems.at[k - 1],
                                              device_id=peer, device_id_type=MESH)
            cp.start()
            sends.append(cp)
        for k in range(1, N_DEV):
            px, py, pc = _flip(x, (k >> 2) & 1), _flip(y, (k >> 1) & 1), _flip(c, k & 1)
            src = 4 * px + 2 * py + pc
            pltpu.make_async_remote_copy(a_ref, o_ref.at[src], send_sems.at[k - 1], recv_sems.at[k - 1],
                                         device_id=(px, py, pc), device_id_type=MESH).wait_recv()
        for cp in sends:
            cp.wait_send()
        mine.wait()

    return pl.pallas_call(
        body, name=name, out_shape=jax.ShapeDtypeStruct((N_DEV,) + shape, a.dtype),
        in_specs=_any_specs(1), out_specs=pl.BlockSpec(memory_space=pl.ANY),
        scratch_shapes=[pltpu.SemaphoreType.DMA((N_DEV - 1,)), pltpu.SemaphoreType.DMA((N_DEV - 1,)),
                        pltpu.SemaphoreType.DMA(())],
    )(a)


def _chip_of(x, y, p):
    px, py = _flip(x, (p >> 1) & 1), _flip(y, p & 1)
    return 2 * px + py, px, py


def gather_chip_shards(arrs, *, name):
    n = len(arrs)

    def body(*refs):
        ins, outs = refs[:n], refs[n:2 * n]
        ici_send, ici_recv, d2d_send, d2d_recv, local_sems = refs[2 * n:]
        x, y, c = _coords()
        me = 2 * x + y
        local = []
        for i in range(n):
            cp = pltpu.make_async_copy(ins[i], outs[i].at[me], local_sems.at[i])
            cp.start()
            local.append(cp)
        sends = []
        for i in range(n):
            half = ins[i].shape[0] // 2
            rows = pl.ds(c * half, half)
            for p in range(1, N_CHIPS):
                _, px, py = _chip_of(x, y, p)
                s = i * 3 + p - 1
                cp = pltpu.make_async_remote_copy(ins[i].at[rows], outs[i].at[me, rows], ici_send.at[s], ici_recv.at[s],
                                                  device_id=(px, py, c), device_id_type=MESH)
                cp.start()
                sends.append(cp)
        for i in range(n):
            half = ins[i].shape[0] // 2
            rows = pl.ds(c * half, half)
            for p in range(1, N_CHIPS):
                src, px, py = _chip_of(x, y, p)
                s = i * 3 + p - 1
                pltpu.make_async_remote_copy(ins[i].at[rows], outs[i].at[src, rows], ici_send.at[s], ici_recv.at[s],
                                             device_id=(px, py, c), device_id_type=MESH).wait_recv()
                cp = pltpu.make_async_remote_copy(outs[i].at[src, rows], outs[i].at[src, rows], d2d_send.at[s],
                                                  d2d_recv.at[s], device_id=(x, y, 1 - c), device_id_type=MESH)
                cp.start()
                sends.append(cp)
        for i in range(n):
            half = ins[i].shape[0] // 2
            theirs = pl.ds((1 - c) * half, half)
            for p in range(1, N_CHIPS):
                src, _, _ = _chip_of(x, y, p)
                s = i * 3 + p - 1
                pltpu.make_async_remote_copy(outs[i].at[src, theirs], outs[i].at[src, theirs], d2d_send.at[s],
                                             d2d_recv.at[s], device_id=(x, y, 1 - c), device_id_type=MESH).wait_recv()
        for cp in sends:
            cp.wait_send()
        for cp in local:
            cp.wait()

    dma = pltpu.SemaphoreType.DMA
    return pl.pallas_call(
        body, name=name,
        out_shape=[jax.ShapeDtypeStruct((N_CHIPS,) + a.shape, a.dtype) for a in arrs],
        in_specs=_any_specs(n), out_specs=_any_specs(n),
        scratch_shapes=[dma((3 * n,)), dma((3 * n,)), dma((3 * n,)), dma((3 * n,)), dma((n,))],
    )(*arrs)


def swap_halves(arrs, *, name):
    n = len(arrs)

    def body(*refs):
        ins, outs = refs[:n], refs[n:2 * n]
        send_sems, recv_sems = refs[2 * n:]
        x, y, c = _coords()
        cps = []
        for i in range(n):
            half = ins[i].shape[1] // 2
            cp = pltpu.make_async_remote_copy(ins[i].at[:, pl.ds((1 - c) * half, half)], outs[i], send_sems.at[i],
                                              recv_sems.at[i], device_id=(x, y, 1 - c), device_id_type=MESH)
            cp.start()
            cps.append(cp)
        for cp in cps:
            cp.wait()

    dma = pltpu.SemaphoreType.DMA
    return pl.pallas_call(
        body, name=name,
        out_shape=[jax.ShapeDtypeStruct((N_CHIPS, a.shape[1] // 2, a.shape[2]), a.dtype) for a in arrs],
        in_specs=_any_specs(n), out_specs=_any_specs(n), scratch_shapes=[dma((n,)), dma((n,))],
    )(*arrs)


def add_half(g, other, c_idx, *, name, out_dtype):
    _, R, C = g.shape
    half = R // 2
    tr = _tile(half, (256, 128, 64, 32, 16, 8))
    nt = half // tr

    def body(c_ref, g_ref, o_ref, out_ref):
        out_ref[...] = (g_ref[...] + o_ref[...]).astype(out_ref.dtype)

    return pl.pallas_call(
        body, name=name, out_shape=jax.ShapeDtypeStruct((N_CHIPS, half, C), out_dtype),
        grid_spec=pltpu.PrefetchScalarGridSpec(
            num_scalar_prefetch=1, grid=(N_CHIPS, nt),
            in_specs=[pl.BlockSpec((None, tr, C), lambda r, t, c_ref: (r, c_ref[0] * nt + t, 0)),
                      pl.BlockSpec((None, tr, C), lambda r, t, c_ref: (r, t, 0))],
            out_specs=pl.BlockSpec((None, tr, C), lambda r, t, c_ref: (r, t, 0))),
        compiler_params=_params(("parallel", "parallel")),
    )(c_idx, g, other)


def scatter_to_chips(arrs, *, name):
    n = len(arrs)

    def body(*refs):
        ins, outs = refs[:n], refs[n:2 * n]
        send_sems, recv_sems = refs[2 * n:]
        x, y, c = _coords()
        cps = []
        for i in range(n):
            for p in range(1, N_CHIPS):
                dst, px, py = _chip_of(x, y, p)
                s = i * 3 + p - 1
                cp = pltpu.make_async_remote_copy(ins[i].at[dst], outs[i].at[p - 1], send_sems.at[s], recv_sems.at[s],
                                                  device_id=(px, py, c), device_id_type=MESH)
                cp.start()
                cps.append(cp)
        for cp in cps:
            cp.wait()

    dma = pltpu.SemaphoreType.DMA
    return pl.pallas_call(
        body, name=name,
        out_shape=[jax.ShapeDtypeStruct((N_CHIPS - 1,) + a.shape[1:], a.dtype) for a in arrs],
        in_specs=_any_specs(n), out_specs=_any_specs(n), scratch_shapes=[dma((3 * n,)), dma((3 * n,))],
    )(*arrs)


def add_chips(h, got, r_idx, *, name):
    _, R, C = h.shape
    tr = _tile(R, (256, 128, 64, 32, 16, 8))

    def body(r_ref, h_ref, g_ref, out_ref):
        acc = h_ref[...].astype(F32)
        for p in range(N_CHIPS - 1):
            acc = acc + g_ref[p].astype(F32)
        out_ref[...] = acc

    return pl.pallas_call(
        body, name=name, out_shape=jax.ShapeDtypeStruct((R, C), F32),
        grid_spec=pltpu.PrefetchScalarGridSpec(
            num_scalar_prefetch=1, grid=(R // tr,),
            in_specs=[pl.BlockSpec((None, tr, C), lambda t, r_ref: (r_ref[0], t, 0)),
                      pl.BlockSpec((N_CHIPS - 1, tr, C), lambda t, r_ref: (0, t, 0))],
            out_specs=pl.BlockSpec((tr, C), lambda t, r_ref: (t, 0))),
        compiler_params=_params(("parallel",)),
    )(r_idx, h, got)


def join_halves(arrs, *, name):
    n = len(arrs)

    def body(*refs):
        ins, outs = refs[:n], refs[n:2 * n]
        send_sems, recv_sems, local_sems = refs[2 * n:]
        x, y, c = _coords()
        cps = []
        for i in range(n):
            half = ins[i].shape[0]
            mine = pl.ds(c * half, half)
            lc = pltpu.make_async_copy(ins[i], outs[i].at[mine], local_sems.at[i])
            lc.start()
            cp = pltpu.make_async_remote_copy(ins[i], outs[i].at[mine], send_sems.at[i], recv_sems.at[i],
                                              device_id=(x, y, 1 - c), device_id_type=MESH)
            cp.start()
            cps.append((lc, cp))
        for i in range(n):
            half = ins[i].shape[0]
            theirs = pl.ds((1 - c) * half, half)
            pltpu.make_async_remote_copy(ins[i], outs[i].at[theirs], send_sems.at[i], recv_sems.at[i],
                                         device_id=(x, y, 1 - c), device_id_type=MESH).wait_recv()
        for lc, cp in cps:
            cp.wait_send()
            lc.wait()

    dma = pltpu.SemaphoreType.DMA
    return pl.pallas_call(
        body, name=name,
        out_shape=[jax.ShapeDtypeStruct((2 * a.shape[0], a.shape[1]), a.dtype) for a in arrs],
        in_specs=_any_specs(n), out_specs=_any_specs(n), scratch_shapes=[dma((n,)), dma((n,)), dma((n,))],
    )(*arrs)


def reduce_scatter_chips(grads):
    x, y, c = _coords()
    c_idx = jnp.reshape(c, (1,)).astype(jnp.int32)
    r_idx = jnp.reshape(2 * x + y, (1,)).astype(jnp.int32)
    theirs = swap_halves(grads, name="rs_swap_halves")
    pair = [add_half(g, o, c_idx, name=f"rs_add_half_{i}", out_dtype=BF16) for i, (g, o) in enumerate(zip(grads, theirs))]
    got = scatter_to_chips(pair, name="rs_scatter_to_chips")
    mine = [add_chips(h, g, r_idx, name=f"rs_add_chips_{i}") for i, (h, g) in enumerate(zip(pair, got))]
    return join_halves(mine, name="rs_join_halves")


def _to_heads(t, B, S):
    return t.reshape(B, S, -1, HEAD_DIM).transpose(0, 2, 1, 3)


def _from_heads(t, B, S):
    return t.transpose(0, 2, 1, 3).reshape(B * S, -1)


def _chip_major(w, axis):
    n = w.shape[axis] // N_CHIPS
    parts = w.reshape(w.shape[:axis] + (N_CHIPS, n) + w.shape[axis + 1:])
    return jnp.moveaxis(parts, axis, 0)


def _from_chip_major(g, axis):
    g = jnp.moveaxis(g, 0, axis)
    return g.reshape(g.shape[:axis] + (g.shape[axis] * g.shape[axis + 1],) + g.shape[axis + 2:])


def kernel(x, c, norm_mix, norm_ffn, w_mod, b_mod, w_qkv, w_o_attn, w_in_ssm, a_re, a_im, log_dt, b_re, b_im, c_re, c_im, d_skip, w_glu, b_glu, w_o_ssm, w_up, conv_w, conv_b, w_down, norm_out, w_fin, b_fin, loss_target, m_norm_mix, m_norm_ffn, m_w_mod, m_b_mod, m_w_qkv, m_w_o_attn, m_w_in_ssm, m_a_re, m_a_im, m_log_dt, m_b_re, m_b_im, m_c_re, m_c_im, m_d_skip, m_w_glu, m_b_glu, m_w_o_ssm, m_w_up, m_conv_w, m_conv_b, m_w_down, m_norm_out, m_w_fin, m_b_fin, v_norm_mix, v_norm_ffn, v_w_mod, v_b_mod, v_w_qkv, v_w_o_attn, v_w_in_ssm, v_a_re, v_a_im, v_log_dt, v_b_re, v_b_im, v_c_re, v_c_im, v_d_skip, v_w_glu, v_b_glu, v_w_o_ssm, v_w_up, v_conv_w, v_conv_b, v_w_down, v_norm_out, v_w_fin, v_b_fin):
    B, S, D = x.shape
    T = B * S
    F2 = conv_b.shape[1]
    F = F2 // 2
    G, P = a_re.shape[1], a_re.shape[2]
    H = b_re.shape[3]
    mx, my, mc = _coords()
    chip = 2 * mx + my
    dev = 4 * mx + 2 * my + mc
    BG = N_DEV * B
    mod_w = w_mod.shape[2]
    fin_w = w_fin.shape[1]

    c_all = all_gather8(c, name="gather_c").reshape(BG, D)
    c_act = silu_rows(c_all, name="silu_c")
    b_mod_mine = lax.dynamic_slice(b_mod, (0, chip * mod_w), (2, mod_w))
    b_fin_mine = lax.dynamic_slice(b_fin, (chip * fin_w,), (fin_w,))
    cond = [matmul(c_act, w_mod[i], bias=b_mod_mine[i], name=f"mod_proj_{i}") for i in range(2)]
    cond.append(matmul(c_act, w_fin, bias=b_fin_mine, name="fin_proj"))
    cond_all = all_gather8(jnp.concatenate(cond, axis=1), name="gather_cond")
    cond_all = cond_all[::2]
    cond_rows = lax.dynamic_slice(cond_all, (0, dev * B, 0), (N_CHIPS, B, cond_all.shape[2]))
    mods = []
    for i in range(2):
        full = cond_rows[:, :, i * mod_w:(i + 1) * mod_w].transpose(1, 0, 2).reshape(B, N_CHIPS * mod_w)
        mods.append([full[:, k * D:(k + 1) * D] for k in range(6)])
    fin = cond_rows[:, :, 2 * mod_w:].transpose(1, 0, 2).reshape(B, N_CHIPS * fin_w)
    sh_f, sc_f = fin[:, :D], fin[:, D:]

    rows1024 = jnp.concatenate([w_o_attn[0], w_in_ssm[0], w_glu[0], w_o_ssm[0], w_down.reshape(-1, D)], axis=0)
    g_qkv, g_rows, g_up = gather_chip_shards(
        [w_qkv[0].astype(BF16), rows1024.astype(BF16), w_up.reshape(2 * D, -1).astype(BF16)], name="gather_weights")
    Dq = D // N_CHIPS
    Fq = F // N_CHIPS
    W_qkv = _from_chip_major(g_qkv, 1)
    W_o_attn = g_rows[:, 0 * Dq:1 * Dq].reshape(D, D)
    W_in = g_rows[:, 1 * Dq:2 * Dq].reshape(D, D)
    W_glu = g_rows[:, 2 * Dq:3 * Dq].reshape(D, D)
    W_o_ssm = g_rows[:, 3 * Dq:4 * Dq].reshape(D, D)
    W_down = [g_rows[:, 4 * Dq + i * Fq:4 * Dq + (i + 1) * Fq].reshape(F, D) for i in range(2)]
    W_up = [_from_chip_major(g_up[:, i * D:(i + 1) * D], 1) for i in range(2)]
    small = jnp.concatenate([conv_w.reshape(6, -1), jnp.pad(d_skip, ((0, 0), (0, conv_w.shape[2] - Dq))),
                             jnp.pad(b_glu, ((0, 0), (0, conv_w.shape[2] - Dq)))], axis=0)
    small_all = all_gather8(small, name="gather_small")[::2]
    conv_w_full = _from_chip_major(small_all[:, :6].reshape(N_CHIPS, 2, 3, -1), 2)
    d_skip_full = small_all[:, 6, :Dq].reshape(1, D)
    b_glu_full = small_all[:, 7, :Dq].reshape(D)

    x0 = x.reshape(T, D)
    tgt = loss_target.reshape(T, D)

    def ffn_fwd(xin, i):
        sh2, sc2, g2 = mods[i][3], mods[i][4], mods[i][5]
        h2 = norm_mod_fwd(xin, norm_ffn[i], sh2, sc2, B=B, S=S, name=f"ffn_norm_{i}")
        up = matmul(h2, W_up[i], name=f"ffn_up_{i}")
        act = conv_gate_fwd(up, conv_w_full[i], conv_b[i:i + 1], B=B, S=S, name=f"ffn_conv_{i}")
        yf = matmul(act, W_down[i], name=f"ffn_down_{i}")
        xout = gate_res_fwd(xin, yf, g2, B=B, S=S, name=f"ffn_res_{i}")
        return xout, (xin, h2, up, act, yf)

    sh1, sc1, g1 = mods[0][0], mods[0][1], mods[0][2]
    h1a = norm_mod_fwd(x0, norm_mix[0], sh1, sc1, B=B, S=S, name="att_norm")
    qkv = matmul(h1a, W_qkv, out_dtype=BF16, name="att_qkv")
    q, k, v = [_to_heads(qkv[:, i * D:(i + 1) * D], B, S) for i in range(3)]
    o, ltot = attn_fwd(q, k, v, name="att_fwd")
    o2 = _from_heads(o, B, S).astype(BF16)
    ya = matmul(o2, W_o_attn, name="att_out")
    x1 = gate_res_fwd(x0, ya, g1, B=B, S=S, name="att_res")
    x2, ffn0 = ffn_fwd(x1, 0)

    lr, li, bbr, bbi = _ssm_discretize(a_re[0], a_im[0], log_dt[0], b_re[0], b_im[0])
    J = G // GROUPS_PER_BLOCK
    Wst = GROUPS_PER_BLOCK * P
    bre_blk = _block_diag_in(bbr).astype(BF16)
    bim_blk = _block_diag_in(bbi).astype(BF16)
    cre_blk = _block_diag_out(c_re[0]).astype(BF16)
    cim_blk = _block_diag_out(c_im[0]).astype(BF16)
    lr8 = jnp.broadcast_to(lr.reshape(J, 1, Wst), (J, 8, Wst))
    li8 = jnp.broadcast_to(li.reshape(J, 1, Wst), (J, 8, Wst))
    sh1s, sc1s, g1s = mods[1][0], mods[1][1], mods[1][2]
    h1s = norm_mod_fwd(x2, norm_mix[1], sh1s, sc1s, B=B, S=S, name="ssm_norm")
    h1p = _interleave(h1s, B, S)
    u = matmul(h1p, W_in, name="ssm_in")
    y_ssm = ssm_fwd(u, bre_blk, bim_blk, cre_blk, cim_blk, lr8, li8, d_skip_full, B=B, S=S, name="ssm_scan_fwd")
    zb = gelu_fwd(y_ssm, B=B, S=S, name="ssm_gelu")
    s_glu = matmul(zb, W_glu, bias=b_glu_full, name="ssm_glu_proj")
    gb = glu_fwd(y_ssm, s_glu, B=B, S=S, name="ssm_glu")
    ys_p = matmul(gb, W_o_ssm, name="ssm_out")
    ys = _deinterleave(ys_p, B, S)
    x3 = gate_res_fwd(x2, ys, g1s, B=B, S=S, name="ssm_res")
    x4, ffn1 = ffn_fwd(x3, 1)

    dx4, loss_p, dsh_f, dsc_f, dnorm_out = final_loss(x4, tgt, norm_out, sh_f, sc_f, B=B, S=S, name="loss_head")
    loss = lax.psum(jnp.sum(loss_p), ("x", "y", "c"))

    def ffn_bwd(dxo, i, saved):
        xin, h2, up, act, yf = saved
        sc2, g2 = mods[i][4], mods[i][5]
        dyf, dg2 = gate_res_bwd(dxo, yf, g2, B=B, S=S, name=f"ffn_res_bwd_{i}")
        dact = matmul(dyf, W_down[i], tb=True, name=f"ffn_down_dx_{i}")
        dW_down = matmul(act, dyf, ta=True, name=f"ffn_down_dw_{i}")
        d3, dcb = conv_gate_bwd1(up, dact, conv_w_full[i], conv_b[i:i + 1], B=B, S=S, name=f"ffn_conv_bwd1_{i}")
        dup, dcw = conv_bwd2(d3, up, conv_w_full[i], B=B, S=S, name=f"ffn_conv_bwd2_{i}")
        dh2 = matmul(dup, W_up[i], tb=True, name=f"ffn_up_dx_{i}")
        dW_up = matmul(h2, dup, ta=True, name=f"ffn_up_dw_{i}")
        dxin, dsh2, dsc2, dnf = norm_mod_bwd(dh2, xin, dxo, norm_ffn[i], sc2, B=B, S=S, name=f"ffn_norm_bwd_{i}")
        return dxin, dict(dW_down=dW_down, dW_up=dW_up, dconv_b=jnp.sum(dcb, axis=0).reshape(F2),
                          dconv_w=jnp.sum(dcw, axis=0), dnorm_ffn=jnp.sum(dnf, axis=0), dsh2=dsh2, dsc2=dsc2, dg2=dg2)

    dx3, gf1 = ffn_bwd(dx4, 1, ffn1)

    dys_p, dg1s = gate_res_bwd(_interleave(dx3, B, S), ys_p, g1s, B=B, S=S, name="ssm_res_bwd")
    dgb = matmul(dys_p, W_o_ssm, tb=True, name="ssm_out_dx")
    dW_o_ssm = matmul(gb, dys_p, ta=True, name="ssm_out_dw")
    ds_glu, dz1, db_glu = glu_bwd1(y_ssm, s_glu, dgb, B=B, S=S, name="ssm_glu_bwd1")
    dz2 = matmul(ds_glu, W_glu, tb=True, name="ssm_glu_dx")
    dW_glu = matmul(zb, ds_glu, ta=True, name="ssm_glu_dw")
    dy_ssm = glu_bwd2(y_ssm, dz1, dz2, B=B, S=S, name="ssm_glu_bwd2")
    du, dbre, dbim, dcre, dcim, dlr8, dli8, ddsk = ssm_bwd(u, dy_ssm, bre_blk, bim_blk, cre_blk, cim_blk, lr8, li8,
                                                           d_skip_full, B=B, S=S, name="ssm_scan_bwd")
    dub = du.astype(BF16)
    dh1p = matmul(dub, W_in, tb=True, name="ssm_in_dx")
    dW_in = matmul(h1p, dub, ta=True, name="ssm_in_dw")
    dx2, dsh1s, dsc1s, dnm1 = norm_mod_bwd(_deinterleave(dh1p, B, S), x2, dx3, norm_mix[1], sc1s, B=B, S=S,
                                           name="ssm_norm_bwd")
    dlr = jnp.sum(dlr8, axis=(0, 2)).reshape(G, P)
    dli = jnp.sum(dli8, axis=(0, 2)).reshape(G, P)
    dbbr = _block_diag_in_grad(jnp.sum(dbre, axis=0), G, P, H)
    dbbi = _block_diag_in_grad(jnp.sum(dbim, axis=0), G, P, H)
    dc_re = _block_diag_out_grad(jnp.sum(dcre, axis=0), G, H, P)
    dc_im = _block_diag_out_grad(jnp.sum(dcim, axis=0), G, H, P)
    dd_skip = jnp.sum(ddsk, axis=0).reshape(D)

    dx1, gf0 = ffn_bwd(dx2, 0, ffn0)

    dya, dg1 = gate_res_bwd(dx1, ya, g1, B=B, S=S, name="att_res_bwd")
    do2 = matmul(dya, W_o_attn, tb=True, out_dtype=BF16, name="att_out_dx")
    dW_o_attn = matmul(o2, dya, ta=True, name="att_out_dw")
    dq, dk, dv = attn_bwd(q, k, v, ltot, _to_heads(do2, B, S), name="att_bwd")
    dqkv = jnp.concatenate([_from_heads(t, B, S) for t in (dq, dk, dv)], axis=1)
    dh1a = matmul(dqkv, W_qkv, tb=True, name="att_qkv_dx")
    dW_qkv = matmul(h1a, dqkv, ta=True, name="att_qkv_dw")
    grad_x, dsh1, dsc1, dnm0 = norm_mod_bwd(dh1a, x0, dx1, norm_mix[0], sc1, B=B, S=S, name="att_norm_bwd")

    g_rows_cm = jnp.concatenate([dW_o_attn.reshape(N_CHIPS, Dq, D), dW_in.reshape(N_CHIPS, Dq, D),
                                 dW_glu.reshape(N_CHIPS, Dq, D), dW_o_ssm.reshape(N_CHIPS, Dq, D),
                                 gf0["dW_down"].reshape(N_CHIPS, Fq, D), gf1["dW_down"].reshape(N_CHIPS, Fq, D)], axis=1)
    g_up_cm = jnp.concatenate([_chip_major(gf0["dW_up"], 1), _chip_major(gf1["dW_up"], 1)], axis=1)
    r_qkv, r_rows, r_up = reduce_scatter_chips([_chip_major(dW_qkv, 1), g_rows_cm, g_up_cm])
    grad_w_qkv = r_qkv[None]
    grad_w_o_attn = r_rows[0 * Dq:1 * Dq][None]
    grad_w_in_ssm = r_rows[1 * Dq:2 * Dq][None]
    grad_w_glu = r_rows[2 * Dq:3 * Dq][None]
    grad_w_o_ssm = r_rows[3 * Dq:4 * Dq][None]
    grad_w_down = r_rows[4 * Dq:].reshape(2, Fq, D)
    grad_w_up = r_up.reshape(2, D, -1)

    dmod_rows = jnp.concatenate([dsh1, dsc1, dg1, gf0["dsh2"], gf0["dsc2"], gf0["dg2"],
                                 dsh1s, dsc1s, dg1s, gf1["dsh2"], gf1["dsc2"], gf1["dg2"], dsh_f, dsc_f], axis=1)
    dmod_all = all_gather8(dmod_rows, name="gather_dmod").reshape(BG, 14 * D)
    grad_w_mod = jnp.stack([
        matmul(c_act, lax.dynamic_slice(dmod_all, (0, i * 6 * D + chip * mod_w), (BG, mod_w)), ta=True,
               name=f"mod_dw_{i}") for i in range(2)])
    grad_w_fin = matmul(c_act, lax.dynamic_slice(dmod_all, (0, 12 * D + chip * fin_w), (BG, fin_w)), ta=True,
                        name="fin_dw")

    parts = [jnp.concatenate([jnp.sum(dnm0, axis=0), jnp.sum(dnm1, axis=0)]),
             jnp.concatenate([gf0["dnorm_ffn"], gf1["dnorm_ffn"]]),
             jnp.sum(dmod_rows[:, :12 * D], axis=0),
             dlr.reshape(-1), dli.reshape(-1), dbbr.reshape(-1), dbbi.reshape(-1), dc_re.reshape(-1), dc_im.reshape(-1),
             dd_skip, jnp.sum(db_glu, axis=0),
             gf0["dconv_w"].reshape(-1), gf1["dconv_w"].reshape(-1), gf0["dconv_b"], gf1["dconv_b"],
             jnp.sum(dnorm_out, axis=0), jnp.sum(dmod_rows[:, 12 * D:], axis=0)]
    sizes = [int(p.shape[0]) for p in parts]
    flat = jnp.concatenate(parts)
    width = 1024
    padded = -(-flat.shape[0] // (8 * width)) * (8 * width)
    flat = jnp.pad(flat, (0, padded - flat.shape[0])).reshape(-1, width)
    summed = sum_leading(all_gather8(flat, name="gather_small_grads"), name="sum_small_grads").reshape(-1)
    offs = [0]
    for s_ in sizes:
        offs.append(offs[-1] + s_)
    (s_nm, s_nf, s_bmod, s_lr, s_li, s_bbr, s_bbi, s_cre, s_cim, s_dsk, s_bglu, s_cw0, s_cw1, s_cb0, s_cb1, s_no,
     s_bfin) = [summed[offs[i]:offs[i + 1]] for i in range(len(sizes))]
    _, disc_vjp = jax.vjp(_ssm_discretize, a_re[0], a_im[0], log_dt[0], b_re[0], b_im[0])
    ga_re, ga_im, glog_dt, gb_re, gb_im = disc_vjp((s_lr.reshape(G, P), s_li.reshape(G, P), s_bbr.reshape(G, P, H),
                                                    s_bbi.reshape(G, P, H)))
    grad_norm_mix = s_nm.reshape(2, D)
    grad_norm_ffn = s_nf.reshape(2, D)
    grad_b_mod = s_bmod.reshape(2, 6 * D)
    grad_c_re = s_cre.reshape(1, G, H, P)
    grad_c_im = s_cim.reshape(1, G, H, P)
    grad_d_skip = lax.dynamic_slice(s_dsk, (chip * Dq,), (Dq,)).reshape(1, Dq)
    grad_b_glu = lax.dynamic_slice(s_bglu, (chip * Dq,), (Dq,)).reshape(1, Dq)
    cw_full = jnp.stack([s_cw0.reshape(3, F2), s_cw1.reshape(3, F2)])
    grad_conv_w = lax.dynamic_slice(cw_full, (0, 0, chip * (F2 // N_CHIPS)), (2, 3, F2 // N_CHIPS))
    grad_conv_b = jnp.stack([s_cb0, s_cb1])
    grad_norm_out = s_no
    grad_b_fin = s_bfin

    grads = dict(
        norm_mix=grad_norm_mix, norm_ffn=grad_norm_ffn, w_mod=grad_w_mod, b_mod=grad_b_mod, w_qkv=grad_w_qkv,
        w_o_attn=grad_w_o_attn, w_in_ssm=grad_w_in_ssm, a_re=ga_re[None], a_im=ga_im[None], log_dt=glog_dt[None],
        b_re=gb_re[None], b_im=gb_im[None], c_re=grad_c_re, c_im=grad_c_im, d_skip=grad_d_skip, w_glu=grad_w_glu,
        b_glu=grad_b_glu, w_o_ssm=grad_w_o_ssm, w_up=grad_w_up, conv_w=grad_conv_w, conv_b=grad_conv_b,
        w_down=grad_w_down, norm_out=grad_norm_out, w_fin=grad_w_fin, b_fin=grad_b_fin)
    weights = dict(
        norm_mix=norm_mix, norm_ffn=norm_ffn, w_mod=w_mod, b_mod=b_mod, w_qkv=w_qkv, w_o_attn=w_o_attn,
        w_in_ssm=w_in_ssm, a_re=a_re, a_im=a_im, log_dt=log_dt, b_re=b_re, b_im=b_im, c_re=c_re, c_im=c_im,
        d_skip=d_skip, w_glu=w_glu, b_glu=b_glu, w_o_ssm=w_o_ssm, w_up=w_up, conv_w=conv_w, conv_b=conv_b,
        w_down=w_down, norm_out=norm_out, w_fin=w_fin, b_fin=b_fin)
    m_in = dict(
        norm_mix=m_norm_mix, norm_ffn=m_norm_ffn, w_mod=m_w_mod, b_mod=m_b_mod, w_qkv=m_w_qkv, w_o_attn=m_w_o_attn,
        w_in_ssm=m_w_in_ssm, a_re=m_a_re, a_im=m_a_im, log_dt=m_log_dt, b_re=m_b_re, b_im=m_b_im, c_re=m_c_re,
        c_im=m_c_im, d_skip=m_d_skip, w_glu=m_w_glu, b_glu=m_b_glu, w_o_ssm=m_w_o_ssm, w_up=m_w_up, conv_w=m_conv_w,
        conv_b=m_conv_b, w_down=m_w_down, norm_out=m_norm_out, w_fin=m_w_fin, b_fin=m_b_fin)
    v_in = dict(
        norm_mix=v_norm_mix, norm_ffn=v_norm_ffn, w_mod=v_w_mod, b_mod=v_b_mod, w_qkv=v_w_qkv, w_o_attn=v_w_o_attn,
        w_in_ssm=v_w_in_ssm, a_re=v_a_re, a_im=v_a_im, log_dt=v_log_dt, b_re=v_b_re, b_im=v_b_im, c_re=v_c_re,
        c_im=v_c_im, d_skip=v_d_skip, w_glu=v_w_glu, b_glu=v_b_glu, w_o_ssm=v_w_o_ssm, w_up=v_w_up, conv_w=v_conv_w,
        conv_b=v_conv_b, w_down=v_w_down, norm_out=v_norm_out, w_fin=v_w_fin, b_fin=v_b_fin)
    names = list(weights)
    for n_ in names:
        grads[n_] = grads[n_].reshape(weights[n_].shape)

    big = ("w_mod", "w_qkv", "w_o_attn", "w_in_ssm", "w_glu", "w_o_ssm", "w_up", "w_down", "w_fin")
    delta, new_m, new_v = {}, {}, {}
    for n_ in big:
        shp = weights[n_].shape
        two_d = lambda a: a.reshape(-1, shp[-1])
        d_, m_, v_ = adamw(two_d(weights[n_]), two_d(grads[n_]), two_d(m_in[n_]), two_d(v_in[n_]), name=f"adamw_{n_}")
        delta[n_], new_m[n_], new_v[n_] = d_.reshape(shp), m_.reshape(shp), v_.reshape(shp)
    rest = [n_ for n_ in names if n_ not in big]

    def pack(tree):
        f = jnp.concatenate([tree[n_].reshape(-1) for n_ in rest])
        pad_to = -(-f.shape[0] // (8 * width)) * (8 * width)
        return jnp.pad(f, (0, pad_to - f.shape[0]), constant_values=1.0).reshape(-1, width)

    d_, m_, v_ = adamw(pack(weights), pack(grads), pack(m_in), pack(v_in), name="adamw_small")
    off = 0
    for n_ in rest:
        sz = int(math.prod(weights[n_].shape))
        shp = weights[n_].shape
        delta[n_] = d_.reshape(-1)[off:off + sz].reshape(shp)
        new_m[n_] = m_.reshape(-1)[off:off + sz].reshape(shp)
        new_v[n_] = v_.reshape(-1)[off:off + sz].reshape(shp)
        off += sz

    return (loss, grad_x.reshape(B, S, D), *[grads[n_] for n_ in names], *[delta[n_] for n_ in names],
            *[new_m[n_] for n_ in names], *[new_v[n_] for n_ in names])
```

```python
import functools
import math

import jax
import jax.numpy as jnp
from jax import lax
from jax.experimental import pallas as pl
from jax.experimental.pallas import tpu as pltpu

F32 = jnp.float32
BF16 = jnp.bfloat16
MESH = pl.DeviceIdType.MESH

HEAD_DIM = 64
SSM_GROUP = 16
STATE = 64
GROUPS_PER_BLOCK = 8
SEGMENTS = 8
EPS = 1e-6
ADAM_LR = 0.001
ADAM_B1 = 0.9
ADAM_B2 = 0.999
ADAM_EPS = 1e-08
ADAM_WD = 0.01
ADAM_STEP = 10
N_CHIPS = 4
N_DEV = 8
V7X_VMEM_LIMIT = 56 * 1024 * 1024
ATT_BLOCK = 128
ATT_HEADS = 4
ATT_HEADS_BWD = 2


def _tile(n, prefs):
    for p in prefs:
        if n % p == 0:
            return p
    return n


def _params(sem, vmem=V7X_VMEM_LIMIT):
    return pltpu.CompilerParams(dimension_semantics=sem, vmem_limit_bytes=vmem)


def matmul(a, b, *, ta=False, tb=False, bias=None, out_dtype=F32, name):
    if ta:
        K, M = a.shape
    else:
        M, K = a.shape
    if tb:
        N, Kb = b.shape
    else:
        Kb, N = b.shape
    assert K == Kb, (a.shape, b.shape, ta, tb)
    tm = _tile(M, (1024, 512, 256, 128))
    tn = _tile(N, (1024, 1408, 768, 512, 256, 128))
    tk = K if K <= 2816 else _tile(K, (1024, 512, 256, 128))
    nk = K // tk
    dims = (((0,) if ta else (1,), (1,) if tb else (0,)), ((), ()))

    def body(*refs):
        a_ref, b_ref = refs[:2]
        bias_ref = refs[2] if bias is not None else None
        o_ref = refs[-2] if nk > 1 else refs[-1]

        def finish(r):
            if bias_ref is not None:
                r = r + bias_ref[...]
            o_ref[...] = r.astype(o_ref.dtype)

        prod = lax.dot_general(a_ref[...].astype(BF16), b_ref[...].astype(BF16), dims, preferred_element_type=F32)
        if nk == 1:
            finish(prod)
            return
        acc_ref = refs[-1]
        k = pl.program_id(2)

        @pl.when(k == 0)
        def _():
            acc_ref[...] = prod

        @pl.when(k > 0)
        def _():
            acc_ref[...] += prod

        @pl.when(k == nk - 1)
        def _():
            finish(acc_ref[...])

    a_spec = pl.BlockSpec((tk, tm), lambda i, j, k: (k, i)) if ta else pl.BlockSpec((tm, tk), lambda i, j, k: (i, k))
    b_spec = pl.BlockSpec((tn, tk), lambda i, j, k: (j, k)) if tb else pl.BlockSpec((tk, tn), lambda i, j, k: (k, j))
    in_specs = [a_spec, b_spec]
    args = [a, b]
    if bias is not None:
        in_specs.append(pl.BlockSpec((1, tn), lambda i, j, k: (0, j)))
        args.append(bias.reshape(1, N).astype(F32))
    return pl.pallas_call(
        body, name=name,
        out_shape=jax.ShapeDtypeStruct((M, N), out_dtype),
        grid=(M // tm, N // tn, nk),
        in_specs=in_specs,
        out_specs=pl.BlockSpec((tm, tn), lambda i, j, k: (i, j)),
        scratch_shapes=[pltpu.VMEM((tm, tn), F32)] if nk > 1 else [],
        compiler_params=_params(("parallel", "parallel", "arbitrary")),
    )(*args)


def rowwise(fn, tiled, per_seq, glob, out_tiled, out_seq, *, B, S, name, rows=512):
    tm = _tile(S, (rows, 128, 64, 32, 16, 8))
    nt = S // tm
    n_in = len(tiled) + len(per_seq) + len(glob)
    n_ot = len(out_tiled)

    def body(*refs):
        ins = refs[:n_in]
        outs = refs[n_in:]
        vals = fn(*[r[...] for r in ins])
        if not isinstance(vals, (tuple, list)):
            vals = (vals,)
        assert len(vals) == len(outs), (name, len(vals), len(outs))
        for o_ref, v in zip(outs[:n_ot], vals[:n_ot]):
            o_ref[...] = v.astype(o_ref.dtype)
        t = pl.program_id(1)
        for o_ref, v in zip(outs[n_ot:], vals[n_ot:]):
            def first(o_ref=o_ref, v=v):
                o_ref[...] = v.astype(F32)

            def later(o_ref=o_ref, v=v):
                o_ref[...] += v.astype(F32)

            pl.when(t == 0)(first)
            pl.when(t > 0)(later)

    in_specs = [pl.BlockSpec((tm, a.shape[1]), lambda b, t: (b * nt + t, 0)) for a in tiled]
    in_specs += [pl.BlockSpec((None, 1, a.shape[1]), lambda b, t: (b, 0, 0)) for a in per_seq]
    in_specs += [pl.BlockSpec(a.shape, lambda b, t: (0,) * a.ndim) for a in glob]
    out_shape = [jax.ShapeDtypeStruct((B * S, w), dt) for w, dt in out_tiled]
    out_shape += [jax.ShapeDtypeStruct((B, 1, w), F32) for w in out_seq]
    out_specs = [pl.BlockSpec((tm, w), lambda b, t: (b * nt + t, 0)) for w, _ in out_tiled]
    out_specs += [pl.BlockSpec((None, 1, w), lambda b, t: (b, 0, 0)) for w in out_seq]
    res = pl.pallas_call(
        body, name=name, out_shape=out_shape, grid=(B, nt), in_specs=in_specs, out_specs=out_specs,
        compiler_params=_params(("parallel", "arbitrary")),
    )(*tiled, *[a.reshape(B, 1, a.shape[1]) for a in per_seq], *glob)
    res = list(res)
    for i in range(n_ot, len(res)):
        res[i] = res[i].reshape(B, res[i].shape[-1])
    return res


def _rms(x):
    r = lax.rsqrt(jnp.mean(x * x, axis=-1, keepdims=True) + EPS)
    return x * r, r


def norm_mod_fwd(x, g, sh, sc, *, B, S, name):
    def fn(x, sh, sc, g):
        xn, _ = _rms(x)
        return (xn * g) * (1.0 + sc) + sh

    return rowwise(fn, [x], [sh, sc], [g.reshape(1, -1)], [(x.shape[1], BF16)], [], B=B, S=S, name=name)[0]


def _norm_mod_bwd_math(dh, x, sc, g):
    xn, r = _rms(x)
    y = xn * g
    dy = dh * (1.0 + sc)
    dxn = dy * g
    dx = r * (dxn - xn * jnp.mean(dxn * xn, axis=-1, keepdims=True))
    dsh = jnp.sum(dh, axis=0, keepdims=True)
    dsc = jnp.sum(dh * y, axis=0, keepdims=True)
    dg = jnp.sum(dy * xn, axis=0, keepdims=True)
    return dx, dsh, dsc, dg


def norm_mod_bwd(dh, x, dres, g, sc, *, B, S, name):
    D = x.shape[1]

    def fn(dh, x, dres, sc, g):
        dx, dsh, dsc, dg = _norm_mod_bwd_math(dh.astype(F32), x, sc, g)
        return dres + dx, dsh, dsc, dg

    return rowwise(fn, [dh, x, dres], [sc], [g.reshape(1, -1)], [(D, F32)], [D, D, D], B=B, S=S, name=name)


def gate_res_fwd(x, y, gate, *, B, S, name):
    return rowwise(lambda x, y, g: x + g * y, [x, y], [gate], [], [(x.shape[1], F32)], [], B=B, S=S, name=name)[0]


def gate_res_bwd(dx, y, gate, *, B, S, name):
    D = dx.shape[1]

    def fn(dx, y, g):
        return g * dx, jnp.sum(dx * y, axis=0, keepdims=True)

    return rowwise(fn, [dx, y], [gate], [], [(D, BF16)], [D], B=B, S=S, name=name)


def final_loss(x, tgt, g, sh, sc, *, B, S, name):
    D = x.shape[1]

    def fn(x, tgt, sh, sc, g):
        xn, _ = _rms(x)
        y = (xn * g) * (1.0 + sc) + sh
        err = y - tgt
        loss = 0.5 * jnp.sum(err * err, axis=0, keepdims=True) * (1.0 / D)
        dx, dsh, dsc, dg = _norm_mod_bwd_math(err * (1.0 / D), x, sc, g)
        return dx, loss, dsh, dsc, dg

    return rowwise(fn, [x, tgt], [sh, sc], [g.reshape(1, -1)], [(D, F32)], [D, D, D, D], B=B, S=S, name=name)


def _gelu(y):
    c0 = math.sqrt(2.0 / math.pi)
    t = jnp.tanh(c0 * (y + 0.044715 * (y * y * y)))
    return 0.5 * y * (1.0 + t), t


def _sigmoid(s):
    return 1.0 / (1.0 + jnp.exp(-s))


def gelu_fwd(y, *, B, S, name):
    return rowwise(lambda y: _gelu(y)[0], [y], [], [], [(y.shape[1], BF16)], [], B=B, S=S, name=name)[0]


def glu_fwd(y, s, *, B, S, name):
    return rowwise(lambda y, s: _gelu(y)[0] * _sigmoid(s), [y, s], [], [], [(y.shape[1], BF16)], [], B=B, S=S,
                   name=name)[0]


def glu_bwd1(y, s, dg, *, B, S, name):
    D = y.shape[1]

    def fn(y, s, dg):
        z = _gelu(y)[0]
        sig = _sigmoid(s)
        ds = dg * z * sig * (1.0 - sig)
        return ds, dg * sig, jnp.sum(ds, axis=0, keepdims=True)

    return rowwise(fn, [y, s, dg], [], [], [(D, BF16), (D, F32)], [D], B=B, S=S, name=name)


def glu_bwd2(y, dz1, dz2, *, B, S, name):
    D = y.shape[1]
    c0 = math.sqrt(2.0 / math.pi)

    def fn(y, dz1, dz2):
        _, t = _gelu(y)
        dgelu = 0.5 * (1.0 + t) + 0.5 * y * (1.0 - t * t) * c0 * (1.0 + 3.0 * 0.044715 * y * y)
        return (dz1 + dz2) * dgelu

    return rowwise(fn, [y, dz1, dz2], [], [], [(D, F32)], [], B=B, S=S, name=name)[0]


def silu_rows(c, *, name):
    R, W = c.shape
    return rowwise(lambda c: c * _sigmoid(c), [c], [], [], [(W, F32)], [], B=1, S=R, name=name)[0]


def _shift_down(cur, h6, h7):
    rows = lax.broadcasted_iota(jnp.int32, cur.shape, 0)
    m1 = jnp.where(rows == 0, h7, pltpu.roll(cur, 1, 0))
    m2 = jnp.where(rows == 0, h6, jnp.where(rows == 1, h7, pltpu.roll(cur, 2, 0)))
    return m1, m2


def _conv3(cur, halo_ref, w_ref, has_prev):
    h6 = jnp.where(has_prev, halo_ref[6:7, :], 0.0)
    h7 = jnp.where(has_prev, halo_ref[7:8, :], 0.0)
    m1, m2 = _shift_down(cur, h6, h7)
    return w_ref[2:3, :] * cur + w_ref[1:2, :] * m1 + w_ref[0:1, :] * m2, m1, m2


def _conv_tiles(S, F):
    ts = _tile(S, (1024, 512, 256, 128, 64, 32, 16, 8))
    tn = _tile(F, (256, 128))
    return ts, tn, S // ts, F // tn


def conv_gate_fwd(up, cw, cb, *, B, S, name):
    F = up.shape[1] // 2
    ts, tn, nts, nF = _conv_tiles(S, F)
    hb = ts // 8

    def body(g_ref, gh_ref, v_ref, vh_ref, wg_ref, wv_ref, bg_ref, bv_ref, o_ref):
        has_prev = pl.program_id(2) > 0
        gc = _conv3(g_ref[...], gh_ref, wg_ref, has_prev)[0] + bg_ref[...]
        vc = _conv3(v_ref[...], vh_ref, wv_ref, has_prev)[0] + bv_ref[...]
        o_ref[...] = (gc * _sigmoid(gc) * vc).astype(o_ref.dtype)

    def cur(off):
        return pl.BlockSpec((ts, tn), lambda b, j, t: (b * nts + t, j + off))

    def halo(off):
        return pl.BlockSpec((8, tn), lambda b, j, t: (jnp.maximum((b * nts + t) * hb - 1, 0), j + off))

    def vec(rows, off):
        return pl.BlockSpec((rows, tn), lambda b, j, t: (0, j + off))

    return pl.pallas_call(
        body, name=name, out_shape=jax.ShapeDtypeStruct((B * S, F), BF16), grid=(B, nF, nts),
        in_specs=[cur(0), halo(0), cur(nF), halo(nF), vec(3, 0), vec(3, nF), vec(1, 0), vec(1, nF)],
        out_specs=pl.BlockSpec((ts, tn), lambda b, j, t: (b * nts + t, j)),
        compiler_params=_params(("parallel", "parallel", "arbitrary")),
    )(up, up, up, up, cw, cw, cb, cb)


def conv_gate_bwd1(up, dact, cw, cb, *, B, S, name):
    F = up.shape[1] // 2
    ts, tn, nts, nF = _conv_tiles(S, F)
    hb = ts // 8

    def body(g_ref, gh_ref, v_ref, vh_ref, da_ref, wg_ref, wv_ref, bg_ref, bv_ref, d_ref, db_ref):
        t = pl.program_id(2)
        has_prev = t > 0
        gc = _conv3(g_ref[...], gh_ref, wg_ref, has_prev)[0] + bg_ref[...]
        vc = _conv3(v_ref[...], vh_ref, wv_ref, has_prev)[0] + bv_ref[...]
        sig = _sigmoid(gc)
        da = da_ref[...]
        dg = da * vc * (sig * (1.0 + gc * (1.0 - sig)))
        dv = da * (gc * sig)
        d_ref[0] = dg
        d_ref[1] = dv
        part = jnp.concatenate([jnp.sum(dg, axis=0, keepdims=True), jnp.sum(dv, axis=0, keepdims=True)], axis=0)

        @pl.when(t == 0)
        def _():
            db_ref[...] = part

        @pl.when(t > 0)
        def _():
            db_ref[...] += part

    def cur(off):
        return pl.BlockSpec((ts, tn), lambda b, j, t: (b * nts + t, j + off))

    def halo(off):
        return pl.BlockSpec((8, tn), lambda b, j, t: (jnp.maximum((b * nts + t) * hb - 1, 0), j + off))

    def vec(rows, off):
        return pl.BlockSpec((rows, tn), lambda b, j, t: (0, j + off))

    return pl.pallas_call(
        body, name=name,
        out_shape=[jax.ShapeDtypeStruct((2, B * S, F), F32), jax.ShapeDtypeStruct((B, 2, F), F32)],
        grid=(B, nF, nts),
        in_specs=[cur(0), halo(0), cur(nF), halo(nF), cur(0), vec(3, 0), vec(3, nF), vec(1, 0), vec(1, nF)],
        out_specs=[pl.BlockSpec((2, ts, tn), lambda b, j, t: (0, b * nts + t, j)),
                   pl.BlockSpec((None, 2, tn), lambda b, j, t: (b, 0, j))],
        compiler_params=_params(("parallel", "parallel", "arbitrary")),
    )(up, up, up, up, dact, cw, cw, cb, cb)


def conv_bwd2(d3, up, cw, *, B, S, name):
    F = up.shape[1] // 2
    ts, tn, nts, nF = _conv_tiles(S, F)
    hb = ts // 8
    last8 = B * S // 8 - 1

    def body(d_ref, da_ref, u_ref, uh_ref, w_ref, o_ref, dw_ref):
        t = pl.program_id(3)
        d = d_ref[...]
        has_next = t < nts - 1
        a0 = jnp.where(has_next, da_ref[0:1, :], 0.0)
        a1 = jnp.where(has_next, da_ref[1:2, :], 0.0)
        rows = lax.broadcasted_iota(jnp.int32, d.shape, 0)
        p1 = jnp.where(rows == ts - 1, a0, pltpu.roll(d, ts - 1, 0))
        p2 = jnp.where(rows == ts - 1, a1, jnp.where(rows == ts - 2, a0, pltpu.roll(d, ts - 2, 0)))
        o_ref[...] = (w_ref[2:3, :] * d + w_ref[1:2, :] * p1 + w_ref[0:1, :] * p2).astype(o_ref.dtype)
        u = u_ref[...]
        has_prev = t > 0
        h6 = jnp.where(has_prev, uh_ref[6:7, :], 0.0)
        h7 = jnp.where(has_prev, uh_ref[7:8, :], 0.0)
        m1, m2 = _shift_down(u, h6, h7)
        part = jnp.concatenate([jnp.sum(d * m2, axis=0, keepdims=True), jnp.sum(d * m1, axis=0, keepdims=True),
                                jnp.sum(d * u, axis=0, keepdims=True)], axis=0)

        @pl.when(t == 0)
        def _():
            dw_ref[...] = part

        @pl.when(t > 0)
        def _():
            dw_ref[...] += part

    return pl.pallas_call(
        body, name=name,
        out_shape=[jax.ShapeDtypeStruct((B * S, 2 * F), BF16), jax.ShapeDtypeStruct((B, 3, 2 * F), F32)],
        grid=(B, 2, nF, nts),
        in_specs=[
            pl.BlockSpec((None, ts, tn), lambda b, g, j, t: (g, b * nts + t, j)),
            pl.BlockSpec((None, 8, tn), lambda b, g, j, t: (g, jnp.minimum((b * nts + t + 1) * hb, last8), j)),
            pl.BlockSpec((ts, tn), lambda b, g, j, t: (b * nts + t, g * nF + j)),
            pl.BlockSpec((8, tn), lambda b, g, j, t: (jnp.maximum((b * nts + t) * hb - 1, 0), g * nF + j)),
            pl.BlockSpec((3, tn), lambda b, g, j, t: (0, g * nF + j)),
        ],
        out_specs=[pl.BlockSpec((ts, tn), lambda b, g, j, t: (b * nts + t, g * nF + j)),
                   pl.BlockSpec((None, 3, tn), lambda b, g, j, t: (b, 0, g * nF + j))],
        compiler_params=_params(("parallel", "parallel", "parallel", "arbitrary")),
    )(d3, d3, up, up, cw)


MASKED_LOG = -1e30


def _split2(x):
    bits = lax.bitcast_convert_type(x, jnp.uint32) & jnp.uint32(0xFFFF0000)
    hi = lax.bitcast_convert_type(bits, F32)
    return hi.astype(BF16), (x - hi).astype(BF16)


def _split_dot(x, m):
    hi, lo = _split2(x)
    return jnp.dot(hi, m, preferred_element_type=F32) + jnp.dot(lo, m, preferred_element_type=F32)


def _nt(a, b):
    return lax.dot_general(a, b, (((1,), (1,)), ((), ())), preferred_element_type=F32)


def _tn(a, b):
    return lax.dot_general(a, b, (((0,), (0,)), ((), ())), preferred_element_type=F32)


def _att_scores(q, k, mask):
    z = _nt(q, k) * (HEAD_DIM ** -0.5)
    e = jnp.exp(-jnp.abs(z))
    sp = jnp.log(1.0 + e)
    lb = jnp.minimum(z, 0.0) - sp
    l1 = jnp.minimum(-z, 0.0) - sp
    if mask is not None:
        lb = jnp.where(mask, lb, MASKED_LOG)
        l1 = jnp.where(mask, l1, 0.0)
    return z, lb, l1, e


def _col_to_row(col, eye):
    return jnp.sum(jnp.where(eye, col, 0.0), axis=0, keepdims=True)


def _row_to_col(row, eye):
    return jnp.sum(jnp.where(eye, row, 0.0), axis=1, keepdims=True)


def attn_fwd(q, k, v, *, name):
    B, H, S, dh = q.shape
    T = ATT_BLOCK
    nq = S // T

    G = _tile(H, (ATT_HEADS, 2))

    def body(q_ref, k_ref, v_ref, o_ref, l_ref):
        r = lax.broadcasted_iota(jnp.int32, (T, T), 0)
        c = lax.broadcasted_iota(jnp.int32, (T, T), 1)
        later = (r > c).astype(BF16)
        eye = r == c
        diag = c < r
        blk = lax.broadcasted_iota(jnp.int32, (nq, T), 0)

        def scores(g, qb, k0, mask):
            _, lb, l1, _ = _att_scores(qb, k_ref[g, pl.ds(k0, T), :], mask)
            hi, lo = _split2(l1)
            return lb, hi, lo, jnp.sum(l1, axis=1, keepdims=True)

        def weigh(g, k0, sc, st):
            lb, hi, lo, rowsum = sc
            acc, run = st
            suf = jnp.dot(hi, later, preferred_element_type=F32) + jnp.dot(lo, later, preferred_element_type=F32)
            w = jnp.exp(lb + suf + run)
            acc = acc + jnp.dot(w.astype(BF16), v_ref[g, pl.ds(k0, T), :], preferred_element_type=F32)
            return acc, run + rowsum

        def qblock(i, totals):
            q0 = pl.multiple_of(i * T, T)
            qbs = [q_ref[g, pl.ds(q0, T), :] for g in range(G)]
            sc0 = tuple(scores(g, qbs[g], q0, diag) for g in range(G))
            st0 = tuple((jnp.zeros((T, dh), F32), jnp.zeros((T, 1), F32)) for _ in range(G))

            def kblock(jj, carry):
                sc, st = carry
                k_next = pl.multiple_of((i - jj) * T, T)
                k_cur = pl.multiple_of((i - jj + 1) * T, T)
                sc_next = tuple(scores(g, qbs[g], k_next, None) for g in range(G))
                st = tuple(weigh(g, k_cur, sc[g], st[g]) for g in range(G))
                return sc_next, st

            sc, st = lax.fori_loop(1, i + 1, kblock, (sc0, st0))
            st = tuple(weigh(g, 0, sc[g], st[g]) for g in range(G))
            for g in range(G):
                o_ref[g, pl.ds(q0, T), :] = st[g][0]
            return tuple(jnp.where(blk == i, _col_to_row(st[g][1], eye), totals[g]) for g in range(G))

        totals = lax.fori_loop(0, nq, qblock, tuple(jnp.zeros((nq, T), F32) for _ in range(G)))
        for g in range(G):
            l_ref[g] = totals[g]

    spec = pl.BlockSpec((None, G, S, dh), lambda b, h: (b, h, 0, 0))
    lspec = pl.BlockSpec((None, G, nq, T), lambda b, h: (b, h, 0, 0))
    return pl.pallas_call(
        body, name=name,
        out_shape=[jax.ShapeDtypeStruct((B, H, S, dh), F32), jax.ShapeDtypeStruct((B, H, nq, T), F32)],
        grid=(B, H // G), in_specs=[spec, spec, spec], out_specs=[spec, lspec],
        compiler_params=_params(("parallel", "parallel")),
    )(q, k, v)


def attn_bwd(q, k, v, ltot, do, *, name):
    B, H, S, dh = q.shape
    T = ATT_BLOCK
    nq = S // T
    scale = HEAD_DIM ** -0.5

    G = _tile(H, (ATT_HEADS_BWD, 2))

    def body(q_ref, k_ref, v_ref, l_ref, do_ref, dq_ref, dk_ref, dv_ref, dk_acc, dv_acc):
        r = lax.broadcasted_iota(jnp.int32, (T, T), 0)
        c = lax.broadcasted_iota(jnp.int32, (T, T), 1)
        upto = (r <= c).astype(BF16)
        before = (r < c).astype(BF16)
        eye = r == c
        diag = c < r
        blk = lax.broadcasted_iota(jnp.int32, (nq, T), 0)
        dk_acc[...] = jnp.zeros_like(dk_acc)
        dv_acc[...] = jnp.zeros_like(dv_acc)

        def scores(g, qb, dob, k0, mask):
            z, lb, l1, e = _att_scores(qb, k_ref[g, pl.ds(k0, T), :], mask)
            inv = 1.0 / (1.0 + e)
            small = e * inv
            pos = z >= 0.0
            beta = jnp.where(pos, inv, small)
            omb = jnp.where(pos, small, inv)
            if mask is not None:
                beta = jnp.where(mask, beta, 0.0)
            hi, lo = _split2(l1)
            dw = _nt(dob, v_ref[g, pl.ds(k0, T), :])
            return lb, hi, lo, jnp.sum(l1, axis=1, keepdims=True), dw, beta, omb

        def grads(g, qb, dob, total, k0, sc, st):
            lb, hi, lo, rowsum, dw, beta, omb = sc
            dq, run_l, run_d = st
            pre = jnp.dot(hi, upto, preferred_element_type=F32) + jnp.dot(lo, upto, preferred_element_type=F32)
            w = jnp.exp(lb + (total - (pre + run_l)))
            dv_acc[g, pl.ds(k0, T), :] += _tn(w.astype(BF16), dob)
            dlw = dw * w
            dl1 = _split_dot(dlw, before) + run_d
            dz = ((dlw * omb - dl1 * beta) * scale).astype(BF16)
            dq = dq + jnp.dot(dz, k_ref[g, pl.ds(k0, T), :], preferred_element_type=F32)
            dk_acc[g, pl.ds(k0, T), :] += _tn(dz, qb)
            return dq, run_l + rowsum, run_d + jnp.sum(dlw, axis=1, keepdims=True)

        def block_inputs(i, q0):
            qbs = [q_ref[g, pl.ds(q0, T), :] for g in range(G)]
            dobs = [do_ref[g, pl.ds(q0, T), :] for g in range(G)]
            tots = [_row_to_col(jnp.sum(jnp.where(blk == i, l_ref[g], 0.0), axis=0, keepdims=True), eye)
                    for g in range(G)]
            z1 = jnp.zeros((T, 1), F32)
            return qbs, dobs, tots, tuple((jnp.zeros((T, dh), F32), z1, z1) for _ in range(G))

        qbs, dobs, tots, st = block_inputs(0, 0)
        for g in range(G):
            dq = grads(g, qbs[g], dobs[g], tots[g], 0, scores(g, qbs[g], dobs[g], 0, diag), st[g])[0]
            dq_ref[g, 0:T, :] = dq.astype(dq_ref.dtype)

        def qblock(i, carry0):
            q0 = pl.multiple_of(i * T, T)
            qbs, dobs, tots, st0 = block_inputs(i, q0)
            sc0 = tuple(scores(g, qbs[g], dobs[g], 0, None) for g in range(G))

            def kblock(j, carry):
                sc, st = carry
                k_cur = pl.multiple_of(j * T, T)
                k_next = pl.multiple_of((j + 1) * T, T)
                sc_next = tuple(scores(g, qbs[g], dobs[g], k_next, None) for g in range(G))
                st = tuple(grads(g, qbs[g], dobs[g], tots[g], k_cur, sc[g], st[g]) for g in range(G))
                return sc_next, st

            sc, st = lax.fori_loop(0, i - 1, kblock, (sc0, st0))
            k_last = pl.multiple_of((i - 1) * T, T)
            sc_diag = tuple(scores(g, qbs[g], dobs[g], q0, diag) for g in range(G))
            st = tuple(grads(g, qbs[g], dobs[g], tots[g], k_last, sc[g], st[g]) for g in range(G))
            for g in range(G):
                dq = grads(g, qbs[g], dobs[g], tots[g], q0, sc_diag[g], st[g])[0]
                dq_ref[g, pl.ds(q0, T), :] = dq.astype(dq_ref.dtype)
            return carry0

        lax.fori_loop(1, nq, qblock, 0)
        dk_ref[...] = dk_acc[...].astype(dk_ref.dtype)
        dv_ref[...] = dv_acc[...].astype(dv_ref.dtype)

    spec = pl.BlockSpec((None, G, S, dh), lambda b, h: (b, h, 0, 0))
    lspec = pl.BlockSpec((None, G, nq, T), lambda b, h: (b, h, 0, 0))
    shp = jax.ShapeDtypeStruct((B, H, S, dh), BF16)
    return pl.pallas_call(
        body, name=name, out_shape=[shp, shp, shp], grid=(B, H // G),
        in_specs=[spec, spec, spec, lspec, spec], out_specs=[spec] * 3,
        scratch_shapes=[pltpu.VMEM((G, S, dh), F32), pltpu.VMEM((G, S, dh), F32)],
        compiler_params=_params(("parallel", "parallel")),
    )(q, k, v, ltot, do)


def _cmul(ar, ai, br, bi):
    return ar * br - ai * bi, ar * bi + ai * br


def _ssm_scan(sr, si, lr, li, n_steps, reverse):
    W = sr.shape[1]
    lim = -li if reverse else li
    zero = jnp.zeros((8, W), F32)

    def row(k):
        i = (n_steps - 1 - k) if reverse else k
        return pl.multiple_of(i * 8, 8)

    def local(k, st):
        cr, ci = st
        r0 = row(k)
        pr, pi = _cmul(lr, lim, cr, ci)
        nr = pr + sr[pl.ds(r0, 8), :]
        ni = pi + si[pl.ds(r0, 8), :]
        sr[pl.ds(r0, 8), :] = nr
        si[pl.ds(r0, 8), :] = ni
        return nr, ni

    er, ei = lax.fori_loop(0, n_steps, local, (zero, zero))

    def power(k, st):
        return _cmul(lr, lim, st[0], st[1])

    lnr, lni = lax.fori_loop(0, n_steps - 1, power, (lr, lim))
    rows = lax.broadcasted_iota(jnp.int32, (8, W), 0)
    cr, ci = zero, zero
    for step in range(1, SEGMENTS):
        tr, ti = _cmul(lnr, lni, cr, ci)
        tr, ti = tr + er, ti + ei
        if reverse:
            seg = SEGMENTS - 1 - step
            tr, ti = pltpu.roll(tr, SEGMENTS - 1, 0), pltpu.roll(ti, SEGMENTS - 1, 0)
        else:
            seg = step
            tr, ti = pltpu.roll(tr, 1, 0), pltpu.roll(ti, 1, 0)
        cr = jnp.where(rows == seg, tr, cr)
        ci = jnp.where(rows == seg, ti, ci)

    def fix(k, st):
        pr, pi = st
        r0 = row(k)
        ar, ai = _cmul(pr, pi, cr, ci)
        sr[pl.ds(r0, 8), :] += ar
        si[pl.ds(r0, 8), :] += ai
        return _cmul(lr, lim, pr, pi)

    lax.fori_loop(0, n_steps, fix, (lr, lim))
    return cr, ci


def _ssm_specs(S, W):
    CH = GROUPS_PER_BLOCK * SSM_GROUP
    return dict(
        rows=pl.BlockSpec((S, CH), lambda b, j: (b, j)),
        b=pl.BlockSpec((None, CH, W), lambda b, j: (j, 0, 0)),
        c=pl.BlockSpec((None, W, CH), lambda b, j: (j, 0, 0)),
        lam=pl.BlockSpec((None, 8, W), lambda b, j: (j, 0, 0)),
        vec=pl.BlockSpec((1, CH), lambda b, j: (0, j)),
    )


def ssm_fwd(u, bre, bim, cre, cim, lr8, li8, dsk, *, B, S, name):
    D = u.shape[1]
    J, CH, W = bre.shape
    n_steps = S // SEGMENTS
    sp = _ssm_specs(S, W)

    def body(u_ref, bre_ref, bim_ref, cre_ref, cim_ref, lr_ref, li_ref, dsk_ref, y_ref, sr, si):
        u = u_ref[...]
        ub = u.astype(BF16)
        sr[...] = jnp.dot(ub, bre_ref[...], preferred_element_type=F32)
        si[...] = jnp.dot(ub, bim_ref[...], preferred_element_type=F32)
        _ssm_scan(sr, si, lr_ref[...], li_ref[...], n_steps, False)
        y = jnp.dot(sr[...].astype(BF16), cre_ref[...], preferred_element_type=F32)
        y = y - jnp.dot(si[...].astype(BF16), cim_ref[...], preferred_element_type=F32)
        y_ref[...] = y + dsk_ref[...] * u

    return pl.pallas_call(
        body, name=name, out_shape=jax.ShapeDtypeStruct((B * S, D), F32), grid=(B, J),
        in_specs=[sp["rows"], sp["b"], sp["b"], sp["c"], sp["c"], sp["lam"], sp["lam"], sp["vec"]],
        out_specs=sp["rows"],
        scratch_shapes=[pltpu.VMEM((S, W), F32), pltpu.VMEM((S, W), F32)],
        compiler_params=_params(("parallel", "parallel")),
    )(u, bre, bim, cre, cim, lr8, li8, dsk)


def ssm_bwd(u, dy, bre, bim, cre, cim, lr8, li8, dsk, *, B, S, name):
    D = u.shape[1]
    J, CH, W = bre.shape
    n_steps = S // SEGMENTS
    sp = _ssm_specs(S, W)

    def body(u_ref, dy_ref, bre_ref, bim_ref, cre_ref, cim_ref, lr_ref, li_ref, dsk_ref,
             du_ref, dbre_ref, dbim_ref, dcre_ref, dcim_ref, dlr_ref, dli_ref, ddsk_ref, sr, si, ar, ai):
        u = u_ref[...]
        dy = dy_ref[...]
        ub = u.astype(BF16)
        dyb = dy.astype(BF16)
        lr, li = lr_ref[...], li_ref[...]
        sr[...] = jnp.dot(ub, bre_ref[...], preferred_element_type=F32)
        si[...] = jnp.dot(ub, bim_ref[...], preferred_element_type=F32)
        cr, ci = _ssm_scan(sr, si, lr, li, n_steps, False)
        ar[...] = _nt(dyb, cre_ref[...])
        ai[...] = -_nt(dyb, cim_ref[...])
        _ssm_scan(ar, ai, lr, li, n_steps, True)

        def dlam(k, st):
            dr, di = st
            r0 = pl.multiple_of((k + 1) * 8, 8)
            p0 = pl.multiple_of(k * 8, 8)
            pr, pi = sr[pl.ds(p0, 8), :], si[pl.ds(p0, 8), :]
            xr, xi = ar[pl.ds(r0, 8), :], ai[pl.ds(r0, 8), :]
            return dr + pr * xr + pi * xi, di + pr * xi - pi * xr

        xr, xi = ar[0:8, :], ai[0:8, :]
        dr, di = lax.fori_loop(0, n_steps - 1, dlam, (cr * xr + ci * xi, cr * xi - ci * xr))
        dlr_ref[...] = dr
        dli_ref[...] = di
        arb = ar[...].astype(BF16)
        aib = ai[...].astype(BF16)
        du_ref[...] = _nt(arb, bre_ref[...]) + _nt(aib, bim_ref[...]) + dsk_ref[...] * dy
        dbre_ref[...] = _tn(ub, arb)
        dbim_ref[...] = _tn(ub, aib)
        dcre_ref[...] = _tn(sr[...].astype(BF16), dyb)
        dcim_ref[...] = -_tn(si[...].astype(BF16), dyb)
        ddsk_ref[...] = jnp.sum(dy * u, axis=0, keepdims=True)

    def per(shape):
        return pl.BlockSpec((None, None) + shape, lambda b, j: (b, j, 0, 0))

    return pl.pallas_call(
        body, name=name,
        out_shape=[jax.ShapeDtypeStruct((B * S, D), F32),
                   jax.ShapeDtypeStruct((B, J, CH, W), F32), jax.ShapeDtypeStruct((B, J, CH, W), F32),
                   jax.ShapeDtypeStruct((B, J, W, CH), F32), jax.ShapeDtypeStruct((B, J, W, CH), F32),
                   jax.ShapeDtypeStruct((B, J, 8, W), F32), jax.ShapeDtypeStruct((B, J, 8, W), F32),
                   jax.ShapeDtypeStruct((B, J, 1, CH), F32)],
        grid=(B, J),
        in_specs=[sp["rows"], sp["rows"], sp["b"], sp["b"], sp["c"], sp["c"], sp["lam"], sp["lam"], sp["vec"]],
        out_specs=[sp["rows"], per((CH, W)), per((CH, W)), per((W, CH)), per((W, CH)), per((8, W)), per((8, W)),
                   per((1, CH))],
        scratch_shapes=[pltpu.VMEM((S, W), F32)] * 4,
        compiler_params=_params(("parallel", "parallel")),
    )(u, dy, bre, bim, cre, cim, lr8, li8, dsk)


def _ssm_discretize(a_re, a_im, log_dt, b_re, b_im):
    dt = jnp.exp(log_dt)[:, None]
    er = jnp.exp(a_re * dt)
    lr = er * jnp.cos(a_im * dt)
    li = er * jnp.sin(a_im * dt)
    den = a_re * a_re + a_im * a_im
    fr = ((lr - 1.0) * a_re + li * a_im) / den
    fi = (li * a_re - (lr - 1.0) * a_im) / den
    bbr = fr[..., None] * b_re - fi[..., None] * b_im
    bbi = fr[..., None] * b_im + fi[..., None] * b_re
    return lr, li, bbr, bbi


def _block_diag_in(m):
    G, P, H = m.shape
    J = G // GROUPS_PER_BLOCK
    m = m.reshape(J, GROUPS_PER_BLOCK, P, H).transpose(0, 1, 3, 2)
    eye = jnp.eye(GROUPS_PER_BLOCK, dtype=m.dtype)
    out = m[:, :, :, None, :] * eye[None, :, None, :, None]
    return out.reshape(J, GROUPS_PER_BLOCK * H, GROUPS_PER_BLOCK * P)


def _block_diag_in_grad(d, G, P, H):
    J = G // GROUPS_PER_BLOCK
    d = d.reshape(J, GROUPS_PER_BLOCK, H, GROUPS_PER_BLOCK, P)
    idx = jnp.arange(GROUPS_PER_BLOCK)
    d = d[:, idx, :, idx, :]
    return d.transpose(1, 0, 3, 2).reshape(G, P, H)


def _block_diag_out(m):
    G, H, P = m.shape
    J = G // GROUPS_PER_BLOCK
    m = m.reshape(J, GROUPS_PER_BLOCK, H, P).transpose(0, 1, 3, 2)
    eye = jnp.eye(GROUPS_PER_BLOCK, dtype=m.dtype)
    out = m[:, :, :, None, :] * eye[None, :, None, :, None]
    return out.reshape(J, GROUPS_PER_BLOCK * P, GROUPS_PER_BLOCK * H)


def _block_diag_out_grad(d, G, H, P):
    J = G // GROUPS_PER_BLOCK
    d = d.reshape(J, GROUPS_PER_BLOCK, P, GROUPS_PER_BLOCK, H)
    idx = jnp.arange(GROUPS_PER_BLOCK)
    d = d[:, idx, :, idx, :]
    return d.transpose(1, 0, 3, 2).reshape(G, H, P)


def _interleave(a, B, S):
    L = S // SEGMENTS
    return a.reshape(B, SEGMENTS, L, a.shape[-1]).transpose(0, 2, 1, 3).reshape(B * S, a.shape[-1])


def _deinterleave(a, B, S):
    L = S // SEGMENTS
    return a.reshape(B, L, SEGMENTS, a.shape[-1]).transpose(0, 2, 1, 3).reshape(B * S, a.shape[-1])


def _adamw_math(w, g, m, v):
    m = ADAM_B1 * m + (1.0 - ADAM_B1) * g
    v = ADAM_B2 * v + (1.0 - ADAM_B2) * (g * g)
    m_hat = m / (1.0 - ADAM_B1 ** ADAM_STEP)
    v_hat = v / (1.0 - ADAM_B2 ** ADAM_STEP)
    delta = -ADAM_LR * (m_hat / (jnp.sqrt(v_hat) + ADAM_EPS) + ADAM_WD * w)
    return delta, m, v


def adamw(w, g, m, v, *, name):
    R, C = w.shape
    tr = _tile(R, (max(8, (1 << 18) // C // 8 * 8), 256, 128, 64, 32, 16, 8))

    def body(w_ref, g_ref, m_ref, v_ref, d_ref, nm_ref, nv_ref):
        d, nm, nv = _adamw_math(w_ref[...], g_ref[...], m_ref[...], v_ref[...])
        d_ref[...] = d
        nm_ref[...] = nm
        nv_ref[...] = nv

    spec = pl.BlockSpec((tr, C), lambda i: (i, 0))
    shp = jax.ShapeDtypeStruct((R, C), F32)
    return pl.pallas_call(
        body, name=name, out_shape=[shp, shp, shp], grid=(R // tr,), in_specs=[spec] * 4, out_specs=[spec] * 3,
        compiler_params=_params(("parallel",)),
    )(w, g, m, v)


def sum_leading(a, *, name, out_dtype=F32):
    n, R, C = a.shape
    tr = _tile(R, (256, 128, 64, 32, 16, 8))

    def body(a_ref, o_ref):
        acc = a_ref[0].astype(F32)
        for i in range(1, n):
            acc = acc + a_ref[i].astype(F32)
        o_ref[...] = acc.astype(o_ref.dtype)

    return pl.pallas_call(
        body, name=name, out_shape=jax.ShapeDtypeStruct((R, C), out_dtype), grid=(R // tr,),
        in_specs=[pl.BlockSpec((n, tr, C), lambda i: (0, i, 0))], out_specs=pl.BlockSpec((tr, C), lambda i: (i, 0)),
        compiler_params=_params(("parallel",)),
    )(a)


def _any_specs(n):
    return [pl.BlockSpec(memory_space=pl.ANY) for _ in range(n)]


def _coords():
    return lax.axis_index("x"), lax.axis_index("y"), lax.axis_index("c")


def _flip(v, bit):
    return (v + bit) % 2


def all_gather8(a, *, name):
    shape = a.shape

    def body(a_ref, o_ref, send_sems, recv_sems, local_sem):
        x, y, c = _coords()
        me = 4 * x + 2 * y + c
        mine = pltpu.make_async_copy(a_ref, o_ref.at[me], local_sem)
        mine.start()
        sends = []
        for k in range(1, N_DEV):
            peer = (_flip(x, (k >> 2) & 1), _flip(y, (k >> 1) & 1), _flip(c, k & 1))
            cp = pltpu.make_async_remote_copy(a_ref, o_ref.at[me], send_sems.at[k - 1], recv_sems.at[k - 1],
                                              device_id=peer, device_id_type=MESH)
            cp.start()
            sends.append(cp)
        for k in range(1, N_DEV):
            px, py, pc = _flip(x, (k >> 2) & 1), _flip(y, (k >> 1) & 1), _flip(c, k & 1)
            src = 4 * px + 2 * py + pc
            pltpu.make_async_remote_copy(a_ref, o_ref.at[src], send_sems.at[k - 1], recv_sems.at[k - 1],
                                         device_id=(px, py, pc), device_id_type=MESH).wait_recv()
        for cp in sends:
            cp.wait_send()
        mine.wait()

    return pl.pallas_call(
        body, name=name, out_shape=jax.ShapeDtypeStruct((N_DEV,) + shape, a.dtype),
        in_specs=_any_specs(1), out_specs=pl.BlockSpec(memory_space=pl.ANY),
        scratch_shapes=[pltpu.SemaphoreType.DMA((N_DEV - 1,)), pltpu.SemaphoreType.DMA((N_DEV - 1,)),
                        pltpu.SemaphoreType.DMA(())],
    )(a)


def _chip_of(x, y, p):
    px, py = _flip(x, (p >> 1) & 1), _flip(y, p & 1)
    return 2 * px + py, px, py


def gather_chip_shards(arrs, *, name):
    n = len(arrs)

    def body(*refs):
        ins, outs = refs[:n], refs[n:2 * n]
        ici_send, ici_recv, d2d_send, d2d_recv, local_sems = refs[2 * n:]
        x, y, c = _coords()
        me = 2 * x + y
        local = []
        for i in range(n):
            cp = pltpu.make_async_copy(ins[i], outs[i].at[me], local_sems.at[i])
            cp.start()
            local.append(cp)
        sends = []
        for i in range(n):
            half = ins[i].shape[0] // 2
            rows = pl.ds(c * half, half)
            for p in range(1, N_CHIPS):
                _, px, py = _chip_of(x, y, p)
                s = i * 3 + p - 1
                cp = pltpu.make_async_remote_copy(ins[i].at[rows], outs[i].at[me, rows], ici_send.at[s], ici_recv.at[s],
                                                  device_id=(px, py, c), device_id_type=MESH)
                cp.start()
                sends.append(cp)
        for i in range(n):
            half = ins[i].shape[0] // 2
            rows = pl.ds(c * half, half)
            for p in range(1, N_CHIPS):
                src, px, py = _chip_of(x, y, p)
                s = i * 3 + p - 1
                pltpu.make_async_remote_copy(ins[i].at[rows], outs[i].at[src, rows], ici_send.at[s], ici_recv.at[s],
                                             device_id=(px, py, c), device_id_type=MESH).wait_recv()
                cp = pltpu.make_async_remote_copy(outs[i].at[src, rows], outs[i].at[src, rows], d2d_send.at[s],
                                                  d2d_recv.at[s], device_id=(x, y, 1 - c), device_id_type=MESH)
                cp.start()
                sends.append(cp)
        for i in range(n):
            half = ins[i].shape[0] // 2
            theirs = pl.ds((1 - c) * half, half)
            for p in range(1, N_CHIPS):
                src, _, _ = _chip_of(x, y, p)
                s = i * 3 + p - 1
                pltpu.make_async_remote_copy(outs[i].at[src, theirs], outs[i].at[src, theirs], d2d_send.at[s],
                                             d2d_recv.at[s], device_id=(x, y, 1 - c), device_id_type=MESH).wait_recv()
        for cp in sends:
            cp.wait_send()
        for cp in local:
            cp.wait()

    dma = pltpu.SemaphoreType.DMA
    return pl.pallas_call(
        body, name=name,
        out_shape=[jax.ShapeDtypeStruct((N_CHIPS,) + a.shape, a.dtype) for a in arrs],
        in_specs=_any_specs(n), out_specs=_any_specs(n),
        scratch_shapes=[dma((3 * n,)), dma((3 * n,)), dma((3 * n,)), dma((3 * n,)), dma((n,))],
    )(*arrs)


def swap_halves(arrs, *, name):
    n = len(arrs)

    def body(*refs):
        ins, outs = refs[:n], refs[n:2 * n]
        send_sems, recv_sems = refs[2 * n:]
        x, y, c = _coords()
        cps = []
        for i in range(n):
            half = ins[i].shape[1] // 2
            cp = pltpu.make_async_remote_copy(ins[i].at[:, pl.ds((1 - c) * half, half)], outs[i], send_sems.at[i],
                                              recv_sems.at[i], device_id=(x, y, 1 - c), device_id_type=MESH)
            cp.start()
            cps.append(cp)
        for cp in cps:
            cp.wait()

    dma = pltpu.SemaphoreType.DMA
    return pl.pallas_call(
        body, name=name,
        out_shape=[jax.ShapeDtypeStruct((N_CHIPS, a.shape[1] // 2, a.shape[2]), a.dtype) for a in arrs],
        in_specs=_any_specs(n), out_specs=_any_specs(n), scratch_shapes=[dma((n,)), dma((n,))],
    )(*arrs)


def add_half(g, other, c_idx, *, name, out_dtype):
    _, R, C = g.shape
    half = R // 2
    tr = _tile(half, (256, 128, 64, 32, 16, 8))
    nt = half // tr

    def body(c_ref, g_ref, o_ref, out_ref):
        out_ref[...] = (g_ref[...] + o_ref[...]).astype(out_ref.dtype)

    return pl.pallas_call(
        body, name=name, out_shape=jax.ShapeDtypeStruct((N_CHIPS, half, C), out_dtype),
        grid_spec=pltpu.PrefetchScalarGridSpec(
            num_scalar_prefetch=1, grid=(N_CHIPS, nt),
            in_specs=[pl.BlockSpec((None, tr, C), lambda r, t, c_ref: (r, c_ref[0] * nt + t, 0)),
                      pl.BlockSpec((None, tr, C), lambda r, t, c_ref: (r, t, 0))],
            out_specs=pl.BlockSpec((None, tr, C), lambda r, t, c_ref: (r, t, 0))),
        compiler_params=_params(("parallel", "parallel")),
    )(c_idx, g, other)


def scatter_to_chips(arrs, *, name):
    n = len(arrs)

    def body(*refs):
        ins, outs = refs[:n], refs[n:2 * n]
        send_sems, recv_sems = refs[2 * n:]
        x, y, c = _coords()
        cps = []
        for i in range(n):
            for p in range(1, N_CHIPS):
                dst, px, py = _chip_of(x, y, p)
                s = i * 3 + p - 1
                cp = pltpu.make_async_remote_copy(ins[i].at[dst], outs[i].at[p - 1], send_sems.at[s], recv_sems.at[s],
                                                  device_id=(px, py, c), device_id_type=MESH)
                cp.start()
                cps.append(cp)
        for cp in cps:
            cp.wait()

    dma = pltpu.SemaphoreType.DMA
    return pl.pallas_call(
        body, name=name,
        out_shape=[jax.ShapeDtypeStruct((N_CHIPS - 1,) + a.shape[1:], a.dtype) for a in arrs],
        in_specs=_any_specs(n), out_specs=_any_specs(n), scratch_shapes=[dma((3 * n,)), dma((3 * n,))],
    )(*arrs)


def add_chips(h, got, r_idx, *, name):
    _, R, C = h.shape
    tr = _tile(R, (256, 128, 64, 32, 16, 8))

    def body(r_ref, h_ref, g_ref, out_ref):
        acc = h_ref[...].astype(F32)
        for p in range(N_CHIPS - 1):
            acc = acc + g_ref[p].astype(F32)
        out_ref[...] = acc

    return pl.pallas_call(
        body, name=name, out_shape=jax.ShapeDtypeStruct((R, C), F32),
        grid_spec=pltpu.PrefetchScalarGridSpec(
            num_scalar_prefetch=1, grid=(R // tr,),
            in_specs=[pl.BlockSpec((None, tr, C), lambda t, r_ref: (r_ref[0], t, 0)),
                      pl.BlockSpec((N_CHIPS - 1, tr, C), lambda t, r_ref: (0, t, 0))],
            out_specs=pl.BlockSpec((tr, C), lambda t, r_ref: (t, 0))),
        compiler_params=_params(("parallel",)),
    )(r_idx, h, got)


def join_halves(arrs, *, name):
    n = len(arrs)

    def body(*refs):
        ins, outs = refs[:n], refs[n:2 * n]
        send_sems, recv_sems, local_sems = refs[2 * n:]
        x, y, c = _coords()
        cps = []
        for i in range(n):
            half = ins[i].shape[0]
            mine = pl.ds(c * half, half)
            lc = pltpu.make_async_copy(ins[i], outs[i].at[mine], local_sems.at[i])
            lc.start()
            cp = pltpu.make_async_remote_copy(ins[i], outs[i].at[mine], send_sems.at[i], recv_sems.at[i],
                                              device_id=(x, y, 1 - c), device_id_type=MESH)
            cp.start()
            cps.append((lc, cp))
        for i in range(n):
            half = ins[i].shape[0]
            theirs = pl.ds((1 - c) * half, half)
            pltpu.make_async_remote_copy(ins[i], outs[i].at[theirs], send_sems.at[i], recv_sems.at[i],
                                         device_id=(x, y, 1 - c), device_id_type=MESH).wait_recv()
        for lc, cp in cps:
            cp.wait_send()
            lc.wait()

    dma = pltpu.SemaphoreType.DMA
    return pl.pallas_call(
        body, name=name,
        out_shape=[jax.ShapeDtypeStruct((2 * a.shape[0], a.shape[1]), a.dtype) for a in arrs],
        in_specs=_any_specs(n), out_specs=_any_specs(n), scratch_shapes=[dma((n,)), dma((n,)), dma((n,))],
    )(*arrs)


def reduce_scatter_chips(grads):
    x, y, c = _coords()
    c_idx = jnp.reshape(c, (1,)).astype(jnp.int32)
    r_idx = jnp.reshape(2 * x + y, (1,)).astype(jnp.int32)
    theirs = swap_halves(grads, name="rs_swap_halves")
    pair = [add_half(g, o, c_idx, name=f"rs_add_half_{i}", out_dtype=BF16) for i, (g, o) in enumerate(zip(grads, theirs))]
    got = scatter_to_chips(pair, name="rs_scatter_to_chips")
    mine = [add_chips(h, g, r_idx, name=f"rs_add_chips_{i}") for i, (h, g) in enumerate(zip(pair, got))]
    return join_halves(mine, name="rs_join_halves")


def _to_heads(t, B, S):
    return t.reshape(B, S, -1, HEAD_DIM).transpose(0, 2, 1, 3)


def _from_heads(t, B, S):
    return t.transpose(0, 2, 1, 3).reshape(B * S, -1)


def _chip_major(w, axis):
    n = w.shape[axis] // N_CHIPS
    parts = w.reshape(w.shape[:axis] + (N_CHIPS, n) + w.shape[axis + 1:])
    return jnp.moveaxis(parts, axis, 0)


def _from_chip_major(g, axis):
    g = jnp.moveaxis(g, 0, axis)
    return g.reshape(g.shape[:axis] + (g.shape[axis] * g.shape[axis + 1],) + g.shape[axis + 2:])


def kernel(x, c, norm_mix, norm_ffn, w_mod, b_mod, w_qkv, w_o_attn, w_in_ssm, a_re, a_im, log_dt, b_re, b_im, c_re, c_im, d_skip, w_glu, b_glu, w_o_ssm, w_up, conv_w, conv_b, w_down, norm_out, w_fin, b_fin, loss_target, m_norm_mix, m_norm_ffn, m_w_mod, m_b_mod, m_w_qkv, m_w_o_attn, m_w_in_ssm, m_a_re, m_a_im, m_log_dt, m_b_re, m_b_im, m_c_re, m_c_im, m_d_skip, m_w_glu, m_b_glu, m_w_o_ssm, m_w_up, m_conv_w, m_conv_b, m_w_down, m_norm_out, m_w_fin, m_b_fin, v_norm_mix, v_norm_ffn, v_w_mod, v_b_mod, v_w_qkv, v_w_o_attn, v_w_in_ssm, v_a_re, v_a_im, v_log_dt, v_b_re, v_b_im, v_c_re, v_c_im, v_d_skip, v_w_glu, v_b_glu, v_w_o_ssm, v_w_up, v_conv_w, v_conv_b, v_w_down, v_norm_out, v_w_fin, v_b_fin):
    B, S, D = x.shape
    T = B * S
    F2 = conv_b.shape[1]
    F = F2 // 2
    G, P = a_re.shape[1], a_re.shape[2]
    H = b_re.shape[3]
    mx, my, mc = _coords()
    chip = 2 * mx + my
    dev = 4 * mx + 2 * my + mc
    BG = N_DEV * B
    mod_w = w_mod.shape[2]
    fin_w = w_fin.shape[1]

    c_all = all_gather8(c, name="gather_c").reshape(BG, D)
    c_act = silu_rows(c_all, name="silu_c")
    b_mod_mine = lax.dynamic_slice(b_mod, (0, chip * mod_w), (2, mod_w))
    b_fin_mine = lax.dynamic_slice(b_fin, (chip * fin_w,), (fin_w,))
    cond = [matmul(c_act, w_mod[i], bias=b_mod_mine[i], name=f"mod_proj_{i}") for i in range(2)]
    cond.append(matmul(c_act, w_fin, bias=b_fin_mine, name="fin_proj"))
    cond_all = all_gather8(jnp.concatenate(cond, axis=1), name="gather_cond")
    cond_all = cond_all[::2]
    cond_rows = lax.dynamic_slice(cond_all, (0, dev * B, 0), (N_CHIPS, B, cond_all.shape[2]))
    mods = []
    for i in range(2):
        full = cond_rows[:, :, i * mod_w:(i + 1) * mod_w].transpose(1, 0, 2).reshape(B, N_CHIPS * mod_w)
        mods.append([full[:, k * D:(k + 1) * D] for k in range(6)])
    fin = cond_rows[:, :, 2 * mod_w:].transpose(1, 0, 2).reshape(B, N_CHIPS * fin_w)
    sh_f, sc_f = fin[:, :D], fin[:, D:]

    rows1024 = jnp.concatenate([w_o_attn[0], w_in_ssm[0], w_glu[0], w_o_ssm[0], w_down.reshape(-1, D)], axis=0)
    g_qkv, g_rows, g_up = gather_chip_shards(
        [w_qkv[0].astype(BF16), rows1024.astype(BF16), w_up.reshape(2 * D, -1).astype(BF16)], name="gather_weights")
    Dq = D // N_CHIPS
    Fq = F // N_CHIPS
    W_qkv = _from_chip_major(g_qkv, 1)
    W_o_attn = g_rows[:, 0 * Dq:1 * Dq].reshape(D, D)
    W_in = g_rows[:, 1 * Dq:2 * Dq].reshape(D, D)
    W_glu = g_rows[:, 2 * Dq:3 * Dq].reshape(D, D)
    W_o_ssm = g_rows[:, 3 * Dq:4 * Dq].reshape(D, D)
    W_down = [g_rows[:, 4 * Dq + i * Fq:4 * Dq + (i + 1) * Fq].reshape(F, D) for i in range(2)]
    W_up = [_from_chip_major(g_up[:, i * D:(i + 1) * D], 1) for i in range(2)]
    small = jnp.concatenate([conv_w.reshape(6, -1), jnp.pad(d_skip, ((0, 0), (0, conv_w.shape[2] - Dq))),
                             jnp.pad(b_glu, ((0, 0), (0, conv_w.shape[2] - Dq)))], axis=0)
    small_all = all_gather8(small, name="gather_small")[::2]
    conv_w_full = _from_chip_major(small_all[:, :6].reshape(N_CHIPS, 2, 3, -1), 2)
    d_skip_full = small_all[:, 6, :Dq].reshape(1, D)
    b_glu_full = small_all[:, 7, :Dq].reshape(D)

    x0 = x.reshape(T, D)
    tgt = loss_target.reshape(T, D)

    def ffn_fwd(xin, i):
        sh2, sc2, g2 = mods[i][3], mods[i][4], mods[i][5]
        h2 = norm_mod_fwd(xin, norm_ffn[i], sh2, sc2, B=B, S=S, name=f"ffn_norm_{i}")
        up = matmul(h2, W_up[i], name=f"ffn_up_{i}")
        act = conv_gate_fwd(up, conv_w_full[i], conv_b[i:i + 1], B=B, S=S, name=f"ffn_conv_{i}")
        yf = matmul(act, W_down[i], name=f"ffn_down_{i}")
        xout = gate_res_fwd(xin, yf, g2, B=B, S=S, name=f"ffn_res_{i}")
        return xout, (xin, h2, up, act, yf)

    sh1, sc1, g1 = mods[0][0], mods[0][1], mods[0][2]
    h1a = norm_mod_fwd(x0, norm_mix[0], sh1, sc1, B=B, S=S, name="att_norm")
    qkv = matmul(h1a, W_qkv, out_dtype=BF16, name="att_qkv")
    q, k, v = [_to_heads(qkv[:, i * D:(i + 1) * D], B, S) for i in range(3)]
    o, ltot = attn_fwd(q, k, v, name="att_fwd")
    o2 = _from_heads(o, B, S).astype(BF16)
    ya = matmul(o2, W_o_attn, name="att_out")
    x1 = gate_res_fwd(x0, ya, g1, B=B, S=S, name="att_res")
    x2, ffn0 = ffn_fwd(x1, 0)

    lr, li, bbr, bbi = _ssm_discretize(a_re[0], a_im[0], log_dt[0], b_re[0], b_im[0])
    J = G // GROUPS_PER_BLOCK
    Wst = GROUPS_PER_BLOCK * P
    bre_blk = _block_diag_in(bbr).astype(BF16)
    bim_blk = _block_diag_in(bbi).astype(BF16)
    cre_blk = _block_diag_out(c_re[0]).astype(BF16)
    cim_blk = _block_diag_out(c_im[0]).astype(BF16)
    lr8 = jnp.broadcast_to(lr.reshape(J, 1, Wst), (J, 8, Wst))
    li8 = jnp.broadcast_to(li.reshape(J, 1, Wst), (J, 8, Wst))
    sh1s, sc1s, g1s = mods[1][0], mods[1][1], mods[1][2]
    h1s = norm_mod_fwd(x2, norm_mix[1], sh1s, sc1s, B=B, S=S, name="ssm_norm")
    h1p = _interleave(h1s, B, S)
    u = matmul(h1p, W_in, name="ssm_in")
    y_ssm = ssm_fwd(u, bre_blk, bim_blk, cre_blk, cim_blk, lr8, li8, d_skip_full, B=B, S=S, name="ssm_scan_fwd")
    zb = gelu_fwd(y_ssm, B=B, S=S, name="ssm_gelu")
    s_glu = matmul(zb, W_glu, bias=b_glu_full, name="ssm_glu_proj")
    gb = glu_fwd(y_ssm, s_glu, B=B, S=S, name="ssm_glu")
    ys_p = matmul(gb, W_o_ssm, name="ssm_out")
    ys = _deinterleave(ys_p, B, S)
    x3 = gate_res_fwd(x2, ys, g1s, B=B, S=S, name="ssm_res")
    x4, ffn1 = ffn_fwd(x3, 1)

    dx4, loss_p, dsh_f, dsc_f, dnorm_out = final_loss(x4, tgt, norm_out, sh_f, sc_f, B=B, S=S, name="loss_head")
    loss = lax.psum(jnp.sum(loss_p), ("x", "y", "c"))

    def ffn_bwd(dxo, i, saved):
        xin, h2, up, act, yf = saved
        sc2, g2 = mods[i][4], mods[i][5]
        dyf, dg2 = gate_res_bwd(dxo, yf, g2, B=B, S=S, name=f"ffn_res_bwd_{i}")
        dact = matmul(dyf, W_down[i], tb=True, name=f"ffn_down_dx_{i}")
        dW_down = matmul(act, dyf, ta=True, name=f"ffn_down_dw_{i}")
        d3, dcb = conv_gate_bwd1(up, dact, conv_w_full[i], conv_b[i:i + 1], B=B, S=S, name=f"ffn_conv_bwd1_{i}")
        dup, dcw = conv_bwd2(d3, up, conv_w_full[i], B=B, S=S, name=f"ffn_conv_bwd2_{i}")
        dh2 = matmul(dup, W_up[i], tb=True, name=f"ffn_up_dx_{i}")
        dW_up = matmul(h2, dup, ta=True, name=f"ffn_up_dw_{i}")
        dxin, dsh2, dsc2, dnf = norm_mod_bwd(dh2, xin, dxo, norm_ffn[i], sc2, B=B, S=S, name=f"ffn_norm_bwd_{i}")
        return dxin, dict(dW_down=dW_down, dW_up=dW_up, dconv_b=jnp.sum(dcb, axis=0).reshape(F2),
                          dconv_w=jnp.sum(dcw, axis=0), dnorm_ffn=jnp.sum(dnf, axis=0), dsh2=dsh2, dsc2=dsc2, dg2=dg2)

    dx3, gf1 = ffn_bwd(dx4, 1, ffn1)

    dys_p, dg1s = gate_res_bwd(_interleave(dx3, B, S), ys_p, g1s, B=B, S=S, name="ssm_res_bwd")
    dgb = matmul(dys_p, W_o_ssm, tb=True, name="ssm_out_dx")
    dW_o_ssm = matmul(gb, dys_p, ta=True, name="ssm_out_dw")
    ds_glu, dz1, db_glu = glu_bwd1(y_ssm, s_glu, dgb, B=B, S=S, name="ssm_glu_bwd1")
    dz2 = matmul(ds_glu, W_glu, tb=True, name="ssm_glu_dx")
    dW_glu = matmul(zb, ds_glu, ta=True, name="ssm_glu_dw")
    dy_ssm = glu_bwd2(y_ssm, dz1, dz2, B=B, S=S, name="ssm_glu_bwd2")
    du, dbre, dbim, dcre, dcim, dlr8, dli8, ddsk = ssm_bwd(u, dy_ssm, bre_blk, bim_blk, cre_blk, cim_blk, lr8, li8,
                                                           d_skip_full, B=B, S=S, name="ssm_scan_bwd")
    dub = du.astype(BF16)
    dh1p = matmul(dub, W_in, tb=True, name="ssm_in_dx")
    dW_in = matmul(h1p, dub, ta=True, name="ssm_in_dw")
    dx2, dsh1s, dsc1s, dnm1 = norm_mod_bwd(_deinterleave(dh1p, B, S), x2, dx3, norm_mix[1], sc1s, B=B, S=S,
                                           name="ssm_norm_bwd")
    dlr = jnp.sum(dlr8, axis=(0, 2)).reshape(G, P)
    dli = jnp.sum(dli8, axis=(0, 2)).reshape(G, P)
    dbbr = _block_diag_in_grad(jnp.sum(dbre, axis=0), G, P, H)
    dbbi = _block_diag_in_grad(jnp.sum(dbim, axis=0), G, P, H)
    dc_re = _block_diag_out_grad(jnp.sum(dcre, axis=0), G, H, P)
    dc_im = _block_diag_out_grad(jnp.sum(dcim, axis=0), G, H, P)
    dd_skip = jnp.sum(ddsk, axis=0).reshape(D)

    dx1, gf0 = ffn_bwd(dx2, 0, ffn0)

    dya, dg1 = gate_res_bwd(dx1, ya, g1, B=B, S=S, name="att_res_bwd")
    do2 = matmul(dya, W_o_attn, tb=True, out_dtype=BF16, name="att_out_dx")
    dW_o_attn = matmul(o2, dya, ta=True, name="att_out_dw")
    dq, dk, dv = attn_bwd(q, k, v, ltot, _to_heads(do2, B, S), name="att_bwd")
    dqkv = jnp.concatenate([_from_heads(t, B, S) for t in (dq, dk, dv)], axis=1)
    dh1a = matmul(dqkv, W_qkv, tb=True, name="att_qkv_dx")
    dW_qkv = matmul(h1a, dqkv, ta=True, name="att_qkv_dw")
    grad_x, dsh1, dsc1, dnm0 = norm_mod_bwd(dh1a, x0, dx1, norm_mix[0], sc1, B=B, S=S, name="att_norm_bwd")

    g_rows_cm = jnp.concatenate([dW_o_attn.reshape(N_CHIPS, Dq, D), dW_in.reshape(N_CHIPS, Dq, D),
                                 dW_glu.reshape(N_CHIPS, Dq, D), dW_o_ssm.reshape(N_CHIPS, Dq, D),
                                 gf0["dW_down"].reshape(N_CHIPS, Fq, D), gf1["dW_down"].reshape(N_CHIPS, Fq, D)], axis=1)
    g_up_cm = jnp.concatenate([_chip_major(gf0["dW_up"], 1), _chip_major(gf1["dW_up"], 1)], axis=1)
    r_qkv, r_rows, r_up = reduce_scatter_chips([_chip_major(dW_qkv, 1), g_rows_cm, g_up_cm])
    grad_w_qkv = r_qkv[None]
    grad_w_o_attn = r_rows[0 * Dq:1 * Dq][None]
    grad_w_in_ssm = r_rows[1 * Dq:2 * Dq][None]
    grad_w_glu = r_rows[2 * Dq:3 * Dq][None]
    grad_w_o_ssm = r_rows[3 * Dq:4 * Dq][None]
    grad_w_down = r_rows[4 * Dq:].reshape(2, Fq, D)
    grad_w_up = r_up.reshape(2, D, -1)

    dmod_rows = jnp.concatenate([dsh1, dsc1, dg1, gf0["dsh2"], gf0["dsc2"], gf0["dg2"],
                                 dsh1s, dsc1s, dg1s, gf1["dsh2"], gf1["dsc2"], gf1["dg2"], dsh_f, dsc_f], axis=1)
    dmod_all = all_gather8(dmod_rows, name="gather_dmod").reshape(BG, 14 * D)
    grad_w_mod = jnp.stack([
        matmul(c_act, lax.dynamic_slice(dmod_all, (0, i * 6 * D + chip * mod_w), (BG, mod_w)), ta=True,
               name=f"mod_dw_{i}") for i in range(2)])
    grad_w_fin = matmul(c_act, lax.dynamic_slice(dmod_all, (0, 12 * D + chip * fin_w), (BG, fin_w)), ta=True,
                        name="fin_dw")

    parts = [jnp.concatenate([jnp.sum(dnm0, axis=0), jnp.sum(dnm1, axis=0)]),
             jnp.concatenate([gf0["dnorm_ffn"], gf1["dnorm_ffn"]]),
             jnp.sum(dmod_rows[:, :12 * D], axis=0),
             dlr.reshape(-1), dli.reshape(-1), dbbr.reshape(-1), dbbi.reshape(-1), dc_re.reshape(-1), dc_im.reshape(-1),
             dd_skip, jnp.sum(db_glu, axis=0),
             gf0["dconv_w"].reshape(-1), gf1["dconv_w"].reshape(-1), gf0["dconv_b"], gf1["dconv_b"],
             jnp.sum(dnorm_out, axis=0), jnp.sum(dmod_rows[:, 12 * D:], axis=0)]
    sizes = [int(p.shape[0]) for p in parts]
    flat = jnp.concatenate(parts)
    width = 1024
    padded = -(-flat.shape[0] // (8 * width)) * (8 * width)
    flat = jnp.pad(flat, (0, padded - flat.shape[0])).reshape(-1, width)
    summed = sum_leading(all_gather8(flat, name="gather_small_grads"), name="sum_small_grads").reshape(-1)
    offs = [0]
    for s_ in sizes:
        offs.append(offs[-1] + s_)
    (s_nm, s_nf, s_bmod, s_lr, s_li, s_bbr, s_bbi, s_cre, s_cim, s_dsk, s_bglu, s_cw0, s_cw1, s_cb0, s_cb1, s_no,
     s_bfin) = [summed[offs[i]:offs[i + 1]] for i in range(len(sizes))]
    _, disc_vjp = jax.vjp(_ssm_discretize, a_re[0], a_im[0], log_dt[0], b_re[0], b_im[0])
    ga_re, ga_im, glog_dt, gb_re, gb_im = disc_vjp((s_lr.reshape(G, P), s_li.reshape(G, P), s_bbr.reshape(G, P, H),
                                                    s_bbi.reshape(G, P, H)))
    grad_norm_mix = s_nm.reshape(2, D)
    grad_norm_ffn = s_nf.reshape(2, D)
    grad_b_mod = s_bmod.reshape(2, 6 * D)
    grad_c_re = s_cre.reshape(1, G, H, P)
    grad_c_im = s_cim.reshape(1, G, H, P)
    grad_d_skip = lax.dynamic_slice(s_dsk, (chip * Dq,), (Dq,)).reshape(1, Dq)
    grad_b_glu = lax.dynamic_slice(s_bglu, (chip * Dq,), (Dq,)).reshape(1, Dq)
    cw_full = jnp.stack([s_cw0.reshape(3, F2), s_cw1.reshape(3, F2)])
    grad_conv_w = lax.dynamic_slice(cw_full, (0, 0, chip * (F2 // N_CHIPS)), (2, 3, F2 // N_CHIPS))
    grad_conv_b = jnp.stack([s_cb0, s_cb1])
    grad_norm_out = s_no
    grad_b_fin = s_bfin

    grads = dict(
        norm_mix=grad_norm_mix, norm_ffn=grad_norm_ffn, w_mod=grad_w_mod, b_mod=grad_b_mod, w_qkv=grad_w_qkv,
        w_o_attn=grad_w_o_attn, w_in_ssm=grad_w_in_ssm, a_re=ga_re[None], a_im=ga_im[None], log_dt=glog_dt[None],
        b_re=gb_re[None], b_im=gb_im[None], c_re=grad_c_re, c_im=grad_c_im, d_skip=grad_d_skip, w_glu=grad_w_glu,
        b_glu=grad_b_glu, w_o_ssm=grad_w_o_ssm, w_up=grad_w_up, conv_w=grad_conv_w, conv_b=grad_conv_b,
        w_down=grad_w_down, norm_out=grad_norm_out, w_fin=grad_w_fin, b_fin=grad_b_fin)
    weights = dict(
        norm_mix=norm_mix, norm_ffn=norm_ffn, w_mod=w_mod, b_mod=b_mod, w_qkv=w_qkv, w_o_attn=w_o_attn,
        w_in_ssm=w_in_ssm, a_re=a_re, a_im=a_im, log_dt=log_dt, b_re=b_re, b_im=b_im, c_re=c_re, c_im=c_im,
        d_skip=d_skip, w_glu=w_glu, b_glu=b_glu, w_o_ssm=w_o_ssm, w_up=w_up, conv_w=conv_w, conv_b=conv_b,
        w_down=w_down, norm_out=norm_out, w_fin=w_fin, b_fin=b_fin)
    m_in = dict(
        norm_mix=m_norm_mix, norm_ffn=m_norm_ffn, w_mod=m_w_mod, b_mod=m_b_mod, w_qkv=m_w_qkv, w_o_attn=m_w_o_attn,
        w_in_ssm=m_w_in_ssm, a_re=m_a_re, a_im=m_a_im, log_dt=m_log_dt, b_re=m_b_re, b_im=m_b_im, c_re=m_c_re,
        c_im=m_c_im, d_skip=m_d_skip, w_glu=m_w_glu, b_glu=m_b_glu, w_o_ssm=m_w_o_ssm, w_up=m_w_up, conv_w=m_conv_w,
        conv_b=m_conv_b, w_down=m_w_down, norm_out=m_norm_out, w_fin=m_w_fin, b_fin=m_b_fin)
    v_in = dict(
        norm_mix=v_norm_mix, norm_ffn=v_norm_ffn, w_mod=v_w_mod, b_mod=v_b_mod, w_qkv=v_w_qkv, w_o_attn=v_w_o_attn,
        w_in_ssm=v_w_in_ssm, a_re=v_a_re, a_im=v_a_im, log_dt=v_log_dt, b_re=v_b_re, b_im=v_b_im, c_re=v_c_re,
        c_im=v_c_im, d_skip=v_d_skip, w_glu=v_w_glu, b_glu=v_b_glu, w_o_ssm=v_w_o_ssm, w_up=v_w_up, conv_w=v_conv_w,
        conv_b=v_conv_b, w_down=v_w_down, norm_out=v_norm_out, w_fin=v_w_fin, b_fin=v_b_fin)
    names = list(weights)
    for n_ in names:
        grads[n_] = grads[n_].reshape(weights[n_].shape)

    big = ("w_mod", "w_qkv", "w_o_attn", "w_in_ssm", "w_glu", "w_o_ssm", "w_up", "w_down", "w_fin")
    delta, new_m, new_v = {}, {}, {}
    for n_ in big:
        shp = weights[n_].shape
        two_d = lambda a: a.reshape(-1, shp[-1])
        d_, m_, v_ = adamw(two_d(weights[n_]), two_d(grads[n_]), two_d(m_in[n_]), two_d(v_in[n_]), name=f"adamw_{n_}")
        delta[n_], new_m[n_], new_v[n_] = d_.reshape(shp), m_.reshape(shp), v_.reshape(shp)
    rest = [n_ for n_ in names if n_ not in big]

    def pack(tree):
        f = jnp.concatenate([tree[n_].reshape(-1) for n_ in rest])
        pad_to = -(-f.shape[0] // (8 * width)) * (8 * width)
        return jnp.pad(f, (0, pad_to - f.shape[0]), constant_values=1.0).reshape(-1, width)

    d_, m_, v_ = adamw(pack(weights), pack(grads), pack(m_in), pack(v_in), name="adamw_small")
    off = 0
    for n_ in rest:
        sz = int(math.prod(weights[n_].shape))
        shp = weights[n_].shape
        delta[n_] = d_.reshape(-1)[off:off + sz].reshape(shp)
        new_m[n_] = m_.reshape(-1)[off:off + sz].reshape(shp)
        new_v[n_] = v_.reshape(-1)[off:off + sz].reshape(shp)
        off += sz

    return (loss, grad_x.reshape(B, S, D), *[grads[n_] for n_ in names], *[delta[n_] for n_ in names],
            *[new_m[n_] for n_ in names], *[new_v[n_] for n_ in names])
```

```python
import functools
import math

import jax
import jax.numpy as jnp
from jax import lax
from jax.experimental import pallas as pl
from jax.experimental.pallas import tpu as pltpu

F32 = jnp.float32
BF16 = jnp.bfloat16
MESH = pl.DeviceIdType.MESH

HEAD_DIM = 64
SSM_GROUP = 16
STATE = 64
GROUPS_PER_BLOCK = 8
SEGMENTS = 8
EPS = 1e-6
ADAM_LR = 0.001
ADAM_B1 = 0.9
ADAM_B2 = 0.999
ADAM_EPS = 1e-08
ADAM_WD = 0.01
ADAM_STEP = 10
N_CHIPS = 4
N_DEV = 8
V7X_VMEM_LIMIT = 56 * 1024 * 1024
ATT_BLOCK = 128
ATT_HEADS = 4
ATT_HEADS_BWD = 2


def _tile(n, prefs):
    for p in prefs:
        if n % p == 0:
            return p
    return n


def _params(sem, vmem=V7X_VMEM_LIMIT):
    return pltpu.CompilerParams(dimension_semantics=sem, vmem_limit_bytes=vmem)


def matmul(a, b, *, ta=False, tb=False, bias=None, out_dtype=F32, name):
    if ta:
        K, M = a.shape
    else:
        M, K = a.shape
    if tb:
        N, Kb = b.shape
    else:
        Kb, N = b.shape
    assert K == Kb, (a.shape, b.shape, ta, tb)
    tm = _tile(M, (1024, 512, 256, 128))
    tn = _tile(N, (1024, 1408, 768, 512, 256, 128))
    tk = K if K <= 2816 else _tile(K, (1024, 512, 256, 128))
    nk = K // tk
    dims = (((0,) if ta else (1,), (1,) if tb else (0,)), ((), ()))

    def body(*refs):
        a_ref, b_ref = refs[:2]
        bias_ref = refs[2] if bias is not None else None
        o_ref = refs[-2] if nk > 1 else refs[-1]

        def finish(r):
            if bias_ref is not None:
                r = r + bias_ref[...]
            o_ref[...] = r.astype(o_ref.dtype)

        prod = lax.dot_general(a_ref[...].astype(BF16), b_ref[...].astype(BF16), dims, preferred_element_type=F32)
        if nk == 1:
            finish(prod)
            return
        acc_ref = refs[-1]
        k = pl.program_id(2)

        @pl.when(k == 0)
        def _():
            acc_ref[...] = prod

        @pl.when(k > 0)
        def _():
            acc_ref[...] += prod

        @pl.when(k == nk - 1)
        def _():
            finish(acc_ref[...])

    a_spec = pl.BlockSpec((tk, tm), lambda i, j, k: (k, i)) if ta else pl.BlockSpec((tm, tk), lambda i, j, k: (i, k))
    b_spec = pl.BlockSpec((tn, tk), lambda i, j, k: (j, k)) if tb else pl.BlockSpec((tk, tn), lambda i, j, k: (k, j))
    in_specs = [a_spec, b_spec]
    args = [a, b]
    if bias is not None:
        in_specs.append(pl.BlockSpec((1, tn), lambda i, j, k: (0, j)))
        args.append(bias.reshape(1, N).astype(F32))
    return pl.pallas_call(
        body, name=name,
        out_shape=jax.ShapeDtypeStruct((M, N), out_dtype),
        grid=(M // tm, N // tn, nk),
        in_specs=in_specs,
        out_specs=pl.BlockSpec((tm, tn), lambda i, j, k: (i, j)),
        scratch_shapes=[pltpu.VMEM((tm, tn), F32)] if nk > 1 else [],
        compiler_params=_params(("parallel", "parallel", "arbitrary")),
    )(*args)


def rowwise(fn, tiled, per_seq, glob, out_tiled, out_seq, *, B, S, name, rows=512):
    tm = _tile(S, (rows, 128, 64, 32, 16, 8))
    nt = S // tm
    n_in = len(tiled) + len(per_seq) + len(glob)
    n_ot = len(out_tiled)

    def body(*refs):
        ins = refs[:n_in]
        outs = refs[n_in:]
        vals = fn(*[r[...] for r in ins])
        if not isinstance(vals, (tuple, list)):
            vals = (vals,)
        assert len(vals) == len(outs), (name, len(vals), len(outs))
        for o_ref, v in zip(outs[:n_ot], vals[:n_ot]):
            o_ref[...] = v.astype(o_ref.dtype)
        t = pl.program_id(1)
        for o_ref, v in zip(outs[n_ot:], vals[n_ot:]):
            def first(o_ref=o_ref, v=v):
                o_ref[...] = v.astype(F32)

            def later(o_ref=o_ref, v=v):
                o_ref[...] += v.astype(F32)

            pl.when(t == 0)(first)
            pl.when(t > 0)(later)

    in_specs = [pl.BlockSpec((tm, a.shape[1]), lambda b, t: (b * nt + t, 0)) for a in tiled]
    in_specs += [pl.BlockSpec((None, 1, a.shape[1]), lambda b, t: (b, 0, 0)) for a in per_seq]
    in_specs += [pl.BlockSpec(a.shape, lambda b, t: (0,) * a.ndim) for a in glob]
    out_shape = [jax.ShapeDtypeStruct((B * S, w), dt) for w, dt in out_tiled]
    out_shape += [jax.ShapeDtypeStruct((B, 1, w), F32) for w in out_seq]
    out_specs = [pl.BlockSpec((tm, w), lambda b, t: (b * nt + t, 0)) for w, _ in out_tiled]
    out_specs += [pl.BlockSpec((None, 1, w), lambda b, t: (b, 0, 0)) for w in out_seq]
    res = pl.pallas_call(
        body, name=name, out_shape=out_shape, grid=(B, nt), in_specs=in_specs, out_specs=out_specs,
        compiler_params=_params(("parallel", "arbitrary")),
    )(*tiled, *[a.reshape(B, 1, a.shape[1]) for a in per_seq], *glob)
    res = list(res)
    for i in range(n_ot, len(res)):
        res[i] = res[i].reshape(B, res[i].shape[-1])
    return res


def _rms(x):
    r = lax.rsqrt(jnp.mean(x * x, axis=-1, keepdims=True) + EPS)
    return x * r, r


def norm_mod_fwd(x, g, sh, sc, *, B, S, name):
    def fn(x, sh, sc, g):
        xn, _ = _rms(x)
        return (xn * g) * (1.0 + sc) + sh

    return rowwise(fn, [x], [sh, sc], [g.reshape(1, -1)], [(x.shape[1], BF16)], [], B=B, S=S, name=name)[0]


def _norm_mod_bwd_math(dh, x, sc, g):
    xn, r = _rms(x)
    y = xn * g
    dy = dh * (1.0 + sc)
    dxn = dy * g
    dx = r * (dxn - xn * jnp.mean(dxn * xn, axis=-1, keepdims=True))
    dsh = jnp.sum(dh, axis=0, keepdims=True)
    dsc = jnp.sum(dh * y, axis=0, keepdims=True)
    dg = jnp.sum(dy * xn, axis=0, keepdims=True)
    return dx, dsh, dsc, dg


def norm_mod_bwd(dh, x, dres, g, sc, *, B, S, name):
    D = x.shape[1]

    def fn(dh, x, dres, sc, g):
        dx, dsh, dsc, dg = _norm_mod_bwd_math(dh.astype(F32), x, sc, g)
        return dres + dx, dsh, dsc, dg

    return rowwise(fn, [dh, x, dres], [sc], [g.reshape(1, -1)], [(D, F32)], [D, D, D], B=B, S=S, name=name)


def gate_res_fwd(x, y, gate, *, B, S, name):
    return rowwise(lambda x, y, g: x + g * y, [x, y], [gate], [], [(x.shape[1], F32)], [], B=B, S=S, name=name)[0]


def gate_res_bwd(dx, y, gate, *, B, S, name):
    D = dx.shape[1]

    def fn(dx, y, g):
        return g * dx, jnp.sum(dx * y, axis=0, keepdims=True)

    return rowwise(fn, [dx, y], [gate], [], [(D, BF16)], [D], B=B, S=S, name=name)


def final_loss(x, tgt, g, sh, sc, *, B, S, name):
    D = x.shape[1]

    def fn(x, tgt, sh, sc, g):
        xn, _ = _rms(x)
        y = (xn * g) * (1.0 + sc) + sh
        err = y - tgt
        loss = 0.5 * jnp.sum(err * err, axis=0, keepdims=True) * (1.0 / D)
        dx, dsh, dsc, dg = _norm_mod_bwd_math(err * (1.0 / D), x, sc, g)
        return dx, loss, dsh, dsc, dg

    return rowwise(fn, [x, tgt], [sh, sc], [g.reshape(1, -1)], [(D, F32)], [D, D, D, D], B=B, S=S, name=name)


def _gelu(y):
    c0 = math.sqrt(2.0 / math.pi)
    t = jnp.tanh(c0 * (y + 0.044715 * (y * y * y)))
    return 0.5 * y * (1.0 + t), t


def _sigmoid(s):
    return 1.0 / (1.0 + jnp.exp(-s))


def gelu_fwd(y, *, B, S, name):
    return rowwise(lambda y: _gelu(y)[0], [y], [], [], [(y.shape[1], BF16)], [], B=B, S=S, name=name)[0]


def glu_fwd(y, s, *, B, S, name):
    return rowwise(lambda y, s: _gelu(y)[0] * _sigmoid(s), [y, s], [], [], [(y.shape[1], BF16)], [], B=B, S=S,
                   name=name)[0]


def glu_bwd1(y, s, dg, *, B, S, name):
    D = y.shape[1]

    def fn(y, s, dg):
        z = _gelu(y)[0]
        sig = _sigmoid(s)
        ds = dg * z * sig * (1.0 - sig)
        return ds, dg * sig, jnp.sum(ds, axis=0, keepdims=True)

    return rowwise(fn, [y, s, dg], [], [], [(D, BF16), (D, F32)], [D], B=B, S=S, name=name)


def glu_bwd2(y, dz1, dz2, *, B, S, name):
    D = y.shape[1]
    c0 = math.sqrt(2.0 / math.pi)

    def fn(y, dz1, dz2):
        _, t = _gelu(y)
        dgelu = 0.5 * (1.0 + t) + 0.5 * y * (1.0 - t * t) * c0 * (1.0 + 3.0 * 0.044715 * y * y)
        return (dz1 + dz2) * dgelu

    return rowwise(fn, [y, dz1, dz2], [], [], [(D, F32)], [], B=B, S=S, name=name)[0]


def silu_rows(c, *, name):
    R, W = c.shape
    return rowwise(lambda c: c * _sigmoid(c), [c], [], [], [(W, F32)], [], B=1, S=R, name=name)[0]


def _shift_down(cur, h6, h7):
    rows = lax.broadcasted_iota(jnp.int32, cur.shape, 0)
    m1 = jnp.where(rows == 0, h7, pltpu.roll(cur, 1, 0))
    m2 = jnp.where(rows == 0, h6, jnp.where(rows == 1, h7, pltpu.roll(cur, 2, 0)))
    return m1, m2


def _conv3(cur, halo_ref, w_ref, has_prev):
    h6 = jnp.where(has_prev, halo_ref[6:7, :], 0.0)
    h7 = jnp.where(has_prev, halo_ref[7:8, :], 0.0)
    m1, m2 = _shift_down(cur, h6, h7)
    return w_ref[2:3, :] * cur + w_ref[1:2, :] * m1 + w_ref[0:1, :] * m2, m1, m2


def _conv_tiles(S, F):
    ts = _tile(S, (1024, 512, 256, 128, 64, 32, 16, 8))
    tn = _tile(F, (256, 128))
    return ts, tn, S // ts, F // tn


def conv_gate_fwd(up, cw, cb, *, B, S, name):
    F = up.shape[1] // 2
    ts, tn, nts, nF = _conv_tiles(S, F)
    hb = ts // 8

    def body(g_ref, gh_ref, v_ref, vh_ref, wg_ref, wv_ref, bg_ref, bv_ref, o_ref):
        has_prev = pl.program_id(2) > 0
        gc = _conv3(g_ref[...], gh_ref, wg_ref, has_prev)[0] + bg_ref[...]
        vc = _conv3(v_ref[...], vh_ref, wv_ref, has_prev)[0] + bv_ref[...]
        o_ref[...] = (gc * _sigmoid(gc) * vc).astype(o_ref.dtype)

    def cur(off):
        return pl.BlockSpec((ts, tn), lambda b, j, t: (b * nts + t, j + off))

    def halo(off):
        return pl.BlockSpec((8, tn), lambda b, j, t: (jnp.maximum((b * nts + t) * hb - 1, 0), j + off))

    def vec(rows, off):
        return pl.BlockSpec((rows, tn), lambda b, j, t: (0, j + off))

    return pl.pallas_call(
        body, name=name, out_shape=jax.ShapeDtypeStruct((B * S, F), BF16), grid=(B, nF, nts),
        in_specs=[cur(0), halo(0), cur(nF), halo(nF), vec(3, 0), vec(3, nF), vec(1, 0), vec(1, nF)],
        out_specs=pl.BlockSpec((ts, tn), lambda b, j, t: (b * nts + t, j)),
        compiler_params=_params(("parallel", "parallel", "arbitrary")),
    )(up, up, up, up, cw, cw, cb, cb)


def conv_gate_bwd1(up, dact, cw, cb, *, B, S, name):
    F = up.shape[1] // 2
    ts, tn, nts, nF = _conv_tiles(S, F)
    hb = ts // 8

    def body(g_ref, gh_ref, v_ref, vh_ref, da_ref, wg_ref, wv_ref, bg_ref, bv_ref, d_ref, db_ref):
        t = pl.program_id(2)
        has_prev = t > 0
        gc = _conv3(g_ref[...], gh_ref, wg_ref, has_prev)[0] + bg_ref[...]
        vc = _conv3(v_ref[...], vh_ref, wv_ref, has_prev)[0] + bv_ref[...]
        sig = _sigmoid(gc)
        da = da_ref[...]
        dg = da * vc * (sig * (1.0 + gc * (1.0 - sig)))
        dv = da * (gc * sig)
        d_ref[0] = dg
        d_ref[1] = dv
        part = jnp.concatenate([jnp.sum(dg, axis=0, keepdims=True), jnp.sum(dv, axis=0, keepdims=True)], axis=0)

        @pl.when(t == 0)
        def _():
            db_ref[...] = part

        @pl.when(t > 0)
        def _():
            db_ref[...] += part

    def cur(off):
        return pl.BlockSpec((ts, tn), lambda b, j, t: (b * nts + t, j + off))

    def halo(off):
        return pl.BlockSpec((8, tn), lambda b, j, t: (jnp.maximum((b * nts + t) * hb - 1, 0), j + off))

    def vec(rows, off):
        return pl.BlockSpec((rows, tn), lambda b, j, t: (0, j + off))

    return pl.pallas_call(
        body, name=name,
        out_shape=[jax.ShapeDtypeStruct((2, B * S, F), F32), jax.ShapeDtypeStruct((B, 2, F), F32)],
        grid=(B, nF, nts),
        in_specs=[cur(0), halo(0), cur(nF), halo(nF), cur(0), vec(3, 0), vec(3, nF), vec(1, 0), vec(1, nF)],
        out_specs=[pl.BlockSpec((2, ts, tn), lambda b, j, t: (0, b * nts + t, j)),
                   pl.BlockSpec((None, 2, tn), lambda b, j, t: (b, 0, j))],
        compiler_params=_params(("parallel", "parallel", "arbitrary")),
    )(up, up, up, up, dact, cw, cw, cb, cb)


def conv_bwd2(d3, up, cw, *, B, S, name):
    F = up.shape[1] // 2
    ts, tn, nts, nF = _conv_tiles(S, F)
    hb = ts // 8
    last8 = B * S // 8 - 1

    def body(d_ref, da_ref, u_ref, uh_ref, w_ref, o_ref, dw_ref):
        t = pl.program_id(3)
        d = d_ref[...]
        has_next = t < nts - 1
        a0 = jnp.where(has_next, da_ref[0:1, :], 0.0)
        a1 = jnp.where(has_next, da_ref[1:2, :], 0.0)
        rows = lax.broadcasted_iota(jnp.int32, d.shape, 0)
        p1 = jnp.where(rows == ts - 1, a0, pltpu.roll(d, ts - 1, 0))
        p2 = jnp.where(rows == ts - 1, a1, jnp.where(rows == ts - 2, a0, pltpu.roll(d, ts - 2, 0)))
        o_ref[...] = (w_ref[2:3, :] * d + w_ref[1:2, :] * p1 + w_ref[0:1, :] * p2).astype(o_ref.dtype)
        u = u_ref[...]
        has_prev = t > 0
        h6 = jnp.where(has_prev, uh_ref[6:7, :], 0.0)
        h7 = jnp.where(has_prev, uh_ref[7:8, :], 0.0)
        m1, m2 = _shift_down(u, h6, h7)
        part = jnp.concatenate([jnp.sum(d * m2, axis=0, keepdims=True), jnp.sum(d * m1, axis=0, keepdims=True),
                                jnp.sum(d * u, axis=0, keepdims=True)], axis=0)

        @pl.when(t == 0)
        def _():
            dw_ref[...] = part

        @pl.when(t > 0)
        def _():
            dw_ref[...] += part

    return pl.pallas_call(
        body, name=name,
        out_shape=[jax.ShapeDtypeStruct((B * S, 2 * F), BF16), jax.ShapeDtypeStruct((B, 3, 2 * F), F32)],
        grid=(B, 2, nF, nts),
        in_specs=[
            pl.BlockSpec((None, ts, tn), lambda b, g, j, t: (g, b * nts + t, j)),
            pl.BlockSpec((None, 8, tn), lambda b, g, j, t: (g, jnp.minimum((b * nts + t + 1) * hb, last8), j)),
            pl.BlockSpec((ts, tn), lambda b, g, j, t: (b * nts + t, g * nF + j)),
            pl.BlockSpec((8, tn), lambda b, g, j, t: (jnp.maximum((b * nts + t) * hb - 1, 0), g * nF + j)),
            pl.BlockSpec((3, tn), lambda b, g, j, t: (0, g * nF + j)),
        ],
        out_specs=[pl.BlockSpec((ts, tn), lambda b, g, j, t: (b * nts + t, g * nF + j)),
                   pl.BlockSpec((None, 3, tn), lambda b, g, j, t: (b, 0, g * nF + j))],
        compiler_params=_params(("parallel", "parallel", "parallel", "arbitrary")),
    )(d3, d3, up, up, cw)


MASKED_LOG = -1e30


def _split2(x):
    bits = lax.bitcast_convert_type(x, jnp.uint32) & jnp.uint32(0xFFFF0000)
    hi = lax.bitcast_convert_type(bits, F32)
    return hi.astype(BF16), (x - hi).astype(BF16)


def _split_dot(x, m):
    hi, lo = _split2(x)
    return jnp.dot(hi, m, preferred_element_type=F32) + jnp.dot(lo, m, preferred_element_type=F32)


def _nt(a, b):
    return lax.dot_general(a, b, (((1,), (1,)), ((), ())), preferred_element_type=F32)


def _tn(a, b):
    return lax.dot_general(a, b, (((0,), (0,)), ((), ())), preferred_element_type=F32)


def _att_scores(q, k, mask):
    z = _nt(q, k) * (HEAD_DIM ** -0.5)
    e = jnp.exp(-jnp.abs(z))
    sp = jnp.log(1.0 + e)
    lb = jnp.minimum(z, 0.0) - sp
    l1 = jnp.minimum(-z, 0.0) - sp
    if mask is not None:
        lb = jnp.where(mask, lb, MASKED_LOG)
        l1 = jnp.where(mask, l1, 0.0)
    return z, lb, l1, e


def _col_to_row(col, eye):
    return jnp.sum(jnp.where(eye, col, 0.0), axis=0, keepdims=True)


def _row_to_col(row, eye):
    return jnp.sum(jnp.where(eye, row, 0.0), axis=1, keepdims=True)


def attn_fwd(q, k, v, *, name):
    B, H, S, dh = q.shape
    T = ATT_BLOCK
    nq = S // T

    G = _tile(H, (ATT_HEADS, 2))

    def body(q_ref, k_ref, v_ref, o_ref, l_ref):
        r = lax.broadcasted_iota(jnp.int32, (T, T), 0)
        c = lax.broadcasted_iota(jnp.int32, (T, T), 1)
        later = (r > c).astype(BF16)
        eye = r == c
        diag = c < r
        blk = lax.broadcasted_iota(jnp.int32, (nq, T), 0)

        later2 = jnp.concatenate([later, later], axis=0)

        def scores(g, qb, k0, mask):
            _, lb, l1, _ = _att_scores(qb, k_ref[g, pl.ds(k0, T), :], mask)
            return lb, jnp.concatenate(_split2(l1), axis=1), jnp.sum(l1, axis=1, keepdims=True)

        def weigh_all(k0, sc, st):
            suf = jnp.dot(jnp.concatenate([s[1] for s in sc], axis=0), later2, preferred_element_type=F32)
            out = []
            for g in range(G):
                lb, _, rowsum = sc[g]
                acc, run = st[g]
                w = jnp.exp(lb + suf[g * T:(g + 1) * T] + run)
                acc = acc + jnp.dot(w.astype(BF16), v_ref[g, pl.ds(k0, T), :], preferred_element_type=F32)
                out.append((acc, run + rowsum))
            return tuple(out)

        def qblock(i, totals):
            q0 = pl.multiple_of(i * T, T)
            qbs = [q_ref[g, pl.ds(q0, T), :] for g in range(G)]
            sc0 = tuple(scores(g, qbs[g], q0, diag) for g in range(G))
            st0 = tuple((jnp.zeros((T, dh), F32), jnp.zeros((T, 1), F32)) for _ in range(G))

            def kblock(jj, carry):
                sc, st = carry
                k_next = pl.multiple_of((i - jj) * T, T)
                k_cur = pl.multiple_of((i - jj + 1) * T, T)
                st = weigh_all(k_cur, sc, st)
                sc_next = tuple(scores(g, qbs[g], k_next, None) for g in range(G))
                return sc_next, st

            sc, st = lax.fori_loop(1, i + 1, kblock, (sc0, st0))
            st = weigh_all(0, sc, st)
            for g in range(G):
                o_ref[g, pl.ds(q0, T), :] = st[g][0]
            return tuple(jnp.where(blk == i, _col_to_row(st[g][1], eye), totals[g]) for g in range(G))

        totals = lax.fori_loop(0, nq, qblock, tuple(jnp.zeros((nq, T), F32) for _ in range(G)))
        for g in range(G):
            l_ref[g] = totals[g]

    spec = pl.BlockSpec((None, G, S, dh), lambda b, h: (b, h, 0, 0))
    lspec = pl.BlockSpec((None, G, nq, T), lambda b, h: (b, h, 0, 0))
    return pl.pallas_call(
        body, name=name,
        out_shape=[jax.ShapeDtypeStruct((B, H, S, dh), F32), jax.ShapeDtypeStruct((B, H, nq, T), F32)],
        grid=(B, H // G), in_specs=[spec, spec, spec], out_specs=[spec, lspec],
        compiler_params=_params(("parallel", "parallel")),
    )(q, k, v)


def attn_bwd(q, k, v, ltot, do, *, name):
    B, H, S, dh = q.shape
    T = ATT_BLOCK
    nq = S // T
    scale = HEAD_DIM ** -0.5

    G = _tile(H, (ATT_HEADS_BWD, 2))

    def body(q_ref, k_ref, v_ref, l_ref, do_ref, dq_ref, dk_ref, dv_ref, dk_acc, dv_acc):
        r = lax.broadcasted_iota(jnp.int32, (T, T), 0)
        c = lax.broadcasted_iota(jnp.int32, (T, T), 1)
        upto = (r <= c).astype(BF16)
        before = (r < c).astype(BF16)
        upto2 = jnp.concatenate([upto, upto], axis=0)
        before2 = jnp.concatenate([before, before], axis=0)
        eye = r == c
        diag = c < r
        blk = lax.broadcasted_iota(jnp.int32, (nq, T), 0)
        dk_acc[...] = jnp.zeros_like(dk_acc)
        dv_acc[...] = jnp.zeros_like(dv_acc)

        def scores(g, qb, dob, k0, mask):
            z, lb, l1, e = _att_scores(qb, k_ref[g, pl.ds(k0, T), :], mask)
            inv = 1.0 / (1.0 + e)
            small = e * inv
            pos = z >= 0.0
            beta = jnp.where(pos, inv, small)
            omb = jnp.where(pos, small, inv)
            if mask is not None:
                beta = jnp.where(mask, beta, 0.0)
            dw = _nt(dob, v_ref[g, pl.ds(k0, T), :])
            return lb, jnp.concatenate(_split2(l1), axis=1), jnp.sum(l1, axis=1, keepdims=True), dw, beta, omb

        def grads_all(qbs, dobs, tots, k0, sc, st):
            pre = jnp.dot(jnp.concatenate([s[1] for s in sc], axis=0), upto2, preferred_element_type=F32)
            dlws = []
            for g in range(G):
                lb = sc[g][0]
                w = jnp.exp(lb + (tots[g] - (pre[g * T:(g + 1) * T] + st[g][1])))
                dv_acc[g, pl.ds(k0, T), :] += _tn(w.astype(BF16), dobs[g])
                dlws.append(sc[g][3] * w)
            pre_d = jnp.dot(jnp.concatenate([jnp.concatenate(_split2(d), axis=1) for d in dlws], axis=0), before2,
                            preferred_element_type=F32)
            out = []
            for g in range(G):
                _, _, rowsum, _, beta, omb = sc[g]
                dq, run_l, run_d = st[g]
                dl1 = pre_d[g * T:(g + 1) * T] + run_d
                dz = ((dlws[g] * omb - dl1 * beta) * scale).astype(BF16)
                dq = dq + jnp.dot(dz, k_ref[g, pl.ds(k0, T), :], preferred_element_type=F32)
                dk_acc[g, pl.ds(k0, T), :] += _tn(dz, qbs[g])
                out.append((dq, run_l + rowsum, run_d + jnp.sum(dlws[g], axis=1, keepdims=True)))
            return tuple(out)

        def block_inputs(i, q0):
            qbs = [q_ref[g, pl.ds(q0, T), :] for g in range(G)]
            dobs = [do_ref[g, pl.ds(q0, T), :] for g in range(G)]
            tots = [_row_to_col(jnp.sum(jnp.where(blk == i, l_ref[g], 0.0), axis=0, keepdims=True), eye)
                    for g in range(G)]
            z1 = jnp.zeros((T, 1), F32)
            return qbs, dobs, tots, tuple((jnp.zeros((T, dh), F32), z1, z1) for _ in range(G))

        qbs, dobs, tots, st = block_inputs(0, 0)
        st = grads_all(qbs, dobs, tots, 0, tuple(scores(g, qbs[g], dobs[g], 0, diag) for g in range(G)), st)
        for g in range(G):
            dq_ref[g, 0:T, :] = st[g][0].astype(dq_ref.dtype)

        def qblock(i, carry0):
            q0 = pl.multiple_of(i * T, T)
            qbs, dobs, tots, st0 = block_inputs(i, q0)
            sc0 = tuple(scores(g, qbs[g], dobs[g], 0, None) for g in range(G))

            def kblock(j, carry):
                sc, st = carry
                k_cur = pl.multiple_of(j * T, T)
                k_next = pl.multiple_of((j + 1) * T, T)
                sc_next = tuple(scores(g, qbs[g], dobs[g], k_next, None) for g in range(G))
                st = grads_all(qbs, dobs, tots, k_cur, sc, st)
                return sc_next, st

            sc, st = lax.fori_loop(0, i - 1, kblock, (sc0, st0))
            k_last = pl.multiple_of((i - 1) * T, T)
            st = grads_all(qbs, dobs, tots, k_last, sc, st)
            sc_diag = tuple(scores(g, qbs[g], dobs[g], q0, diag) for g in range(G))
            st = grads_all(qbs, dobs, tots, q0, sc_diag, st)
            for g in range(G):
                dq_ref[g, pl.ds(q0, T), :] = st[g][0].astype(dq_ref.dtype)
            return carry0

        lax.fori_loop(1, nq, qblock, 0)
        dk_ref[...] = dk_acc[...].astype(dk_ref.dtype)
        dv_ref[...] = dv_acc[...].astype(dv_ref.dtype)

    spec = pl.BlockSpec((None, G, S, dh), lambda b, h: (b, h, 0, 0))
    lspec = pl.BlockSpec((None, G, nq, T), lambda b, h: (b, h, 0, 0))
    shp = jax.ShapeDtypeStruct((B, H, S, dh), BF16)
    return pl.pallas_call(
        body, name=name, out_shape=[shp, shp, shp], grid=(B, H // G),
        in_specs=[spec, spec, spec, lspec, spec], out_specs=[spec] * 3,
        scratch_shapes=[pltpu.VMEM((G, S, dh), F32), pltpu.VMEM((G, S, dh), F32)],
        compiler_params=_params(("parallel", "parallel")),
    )(q, k, v, ltot, do)


def _cmul(ar, ai, br, bi):
    return ar * br - ai * bi, ar * bi + ai * br


def _ssm_scan(sr, si, lr, li, n_steps, reverse):
    W = sr.shape[1]
    lim = -li if reverse else li
    zero = jnp.zeros((8, W), F32)

    def row(k):
        i = (n_steps - 1 - k) if reverse else k
        return pl.multiple_of(i * 8, 8)

    def local(k, st):
        cr, ci = st
        r0 = row(k)
        pr, pi = _cmul(lr, lim, cr, ci)
        nr = pr + sr[pl.ds(r0, 8), :]
        ni = pi + si[pl.ds(r0, 8), :]
        sr[pl.ds(r0, 8), :] = nr
        si[pl.ds(r0, 8), :] = ni
        return nr, ni

    er, ei = lax.fori_loop(0, n_steps, local, (zero, zero))

    def power(k, st):
        return _cmul(lr, lim, st[0], st[1])

    lnr, lni = lax.fori_loop(0, n_steps - 1, power, (lr, lim))
    rows = lax.broadcasted_iota(jnp.int32, (8, W), 0)
    cr, ci = zero, zero
    for step in range(1, SEGMENTS):
        tr, ti = _cmul(lnr, lni, cr, ci)
        tr, ti = tr + er, ti + ei
        if reverse:
            seg = SEGMENTS - 1 - step
            tr, ti = pltpu.roll(tr, SEGMENTS - 1, 0), pltpu.roll(ti, SEGMENTS - 1, 0)
        else:
            seg = step
            tr, ti = pltpu.roll(tr, 1, 0), pltpu.roll(ti, 1, 0)
        cr = jnp.where(rows == seg, tr, cr)
        ci = jnp.where(rows == seg, ti, ci)

    def fix(k, st):
        pr, pi = st
        r0 = row(k)
        ar, ai = _cmul(pr, pi, cr, ci)
        sr[pl.ds(r0, 8), :] += ar
        si[pl.ds(r0, 8), :] += ai
        return _cmul(lr, lim, pr, pi)

    lax.fori_loop(0, n_steps, fix, (lr, lim))
    return cr, ci


def _ssm_specs(S, W):
    CH = GROUPS_PER_BLOCK * SSM_GROUP
    return dict(
        rows=pl.BlockSpec((S, CH), lambda b, j: (b, j)),
        b=pl.BlockSpec((None, CH, W), lambda b, j: (j, 0, 0)),
        c=pl.BlockSpec((None, W, CH), lambda b, j: (j, 0, 0)),
        lam=pl.BlockSpec((None, 8, W), lambda b, j: (j, 0, 0)),
        vec=pl.BlockSpec((1, CH), lambda b, j: (0, j)),
    )


def ssm_fwd(u, bre, bim, cre, cim, lr8, li8, dsk, *, B, S, name):
    D = u.shape[1]
    J, CH, W = bre.shape
    n_steps = S // SEGMENTS
    sp = _ssm_specs(S, W)

    def body(u_ref, bre_ref, bim_ref, cre_ref, cim_ref, lr_ref, li_ref, dsk_ref, y_ref, sr, si):
        u = u_ref[...]
        ub = u.astype(BF16)
        sr[...] = jnp.dot(ub, bre_ref[...], preferred_element_type=F32)
        si[...] = jnp.dot(ub, bim_ref[...], preferred_element_type=F32)
        _ssm_scan(sr, si, lr_ref[...], li_ref[...], n_steps, False)
        y = jnp.dot(sr[...].astype(BF16), cre_ref[...], preferred_element_type=F32)
        y = y - jnp.dot(si[...].astype(BF16), cim_ref[...], preferred_element_type=F32)
        y_ref[...] = y + dsk_ref[...] * u

    return pl.pallas_call(
        body, name=name, out_shape=jax.ShapeDtypeStruct((B * S, D), F32), grid=(B, J),
        in_specs=[sp["rows"], sp["b"], sp["b"], sp["c"], sp["c"], sp["lam"], sp["lam"], sp["vec"]],
        out_specs=sp["rows"],
        scratch_shapes=[pltpu.VMEM((S, W), F32), pltpu.VMEM((S, W), F32)],
        compiler_params=_params(("parallel", "parallel")),
    )(u, bre, bim, cre, cim, lr8, li8, dsk)


def ssm_bwd(u, dy, bre, bim, cre, cim, lr8, li8, dsk, *, B, S, name):
    D = u.shape[1]
    J, CH, W = bre.shape
    n_steps = S // SEGMENTS
    sp = _ssm_specs(S, W)

    def body(u_ref, dy_ref, bre_ref, bim_ref, cre_ref, cim_ref, lr_ref, li_ref, dsk_ref,
             du_ref, dbre_ref, dbim_ref, dcre_ref, dcim_ref, dlr_ref, dli_ref, ddsk_ref, sr, si, ar, ai):
        u = u_ref[...]
        dy = dy_ref[...]
        ub = u.astype(BF16)
        dyb = dy.astype(BF16)
        lr, li = lr_ref[...], li_ref[...]
        sr[...] = jnp.dot(ub, bre_ref[...], preferred_element_type=F32)
        si[...] = jnp.dot(ub, bim_ref[...], preferred_element_type=F32)
        cr, ci = _ssm_scan(sr, si, lr, li, n_steps, False)
        ar[...] = _nt(dyb, cre_ref[...])
        ai[...] = -_nt(dyb, cim_ref[...])
        _ssm_scan(ar, ai, lr, li, n_steps, True)

        def dlam(k, st):
            dr, di = st
            r0 = pl.multiple_of((k + 1) * 8, 8)
            p0 = pl.multiple_of(k * 8, 8)
            pr, pi = sr[pl.ds(p0, 8), :], si[pl.ds(p0, 8), :]
            xr, xi = ar[pl.ds(r0, 8), :], ai[pl.ds(r0, 8), :]
            return dr + pr * xr + pi * xi, di + pr * xi - pi * xr

        xr, xi = ar[0:8, :], ai[0:8, :]
        dr, di = lax.fori_loop(0, n_steps - 1, dlam, (cr * xr + ci * xi, cr * xi - ci * xr))
        dlr_ref[...] = dr
        dli_ref[...] = di
        arb = ar[...].astype(BF16)
        aib = ai[...].astype(BF16)
        du_ref[...] = _nt(arb, bre_ref[...]) + _nt(aib, bim_ref[...]) + dsk_ref[...] * dy
        dbre_ref[...] = _tn(ub, arb)
        dbim_ref[...] = _tn(ub, aib)
        dcre_ref[...] = _tn(sr[...].astype(BF16), dyb)
        dcim_ref[...] = -_tn(si[...].astype(BF16), dyb)
        ddsk_ref[...] = jnp.sum(dy * u, axis=0, keepdims=True)

    def per(shape):
        return pl.BlockSpec((None, None) + shape, lambda b, j: (b, j, 0, 0))

    return pl.pallas_call(
        body, name=name,
        out_shape=[jax.ShapeDtypeStruct((B * S, D), F32),
                   jax.ShapeDtypeStruct((B, J, CH, W), F32), jax.ShapeDtypeStruct((B, J, CH, W), F32),
                   jax.ShapeDtypeStruct((B, J, W, CH), F32), jax.ShapeDtypeStruct((B, J, W, CH), F32),
                   jax.ShapeDtypeStruct((B, J, 8, W), F32), jax.ShapeDtypeStruct((B, J, 8, W), F32),
                   jax.ShapeDtypeStruct((B, J, 1, CH), F32)],
        grid=(B, J),
        in_specs=[sp["rows"], sp["rows"], sp["b"], sp["b"], sp["c"], sp["c"], sp["lam"], sp["lam"], sp["vec"]],
        out_specs=[sp["rows"], per((CH, W)), per((CH, W)), per((W, CH)), per((W, CH)), per((8, W)), per((8, W)),
                   per((1, CH))],
        scratch_shapes=[pltpu.VMEM((S, W), F32)] * 4,
        compiler_params=_params(("parallel", "parallel")),
    )(u, dy, bre, bim, cre, cim, lr8, li8, dsk)


def _ssm_discretize(a_re, a_im, log_dt, b_re, b_im):
    dt = jnp.exp(log_dt)[:, None]
    er = jnp.exp(a_re * dt)
    lr = er * jnp.cos(a_im * dt)
    li = er * jnp.sin(a_im * dt)
    den = a_re * a_re + a_im * a_im
    fr = ((lr - 1.0) * a_re + li * a_im) / den
    fi = (li * a_re - (lr - 1.0) * a_im) / den
    bbr = fr[..., None] * b_re - fi[..., None] * b_im
    bbi = fr[..., None] * b_im + fi[..., None] * b_re
    return lr, li, bbr, bbi


def _block_diag_in(m):
    G, P, H = m.shape
    J = G // GROUPS_PER_BLOCK
    m = m.reshape(J, GROUPS_PER_BLOCK, P, H).transpose(0, 1, 3, 2)
    eye = jnp.eye(GROUPS_PER_BLOCK, dtype=m.dtype)
    out = m[:, :, :, None, :] * eye[None, :, None, :, None]
    return out.reshape(J, GROUPS_PER_BLOCK * H, GROUPS_PER_BLOCK * P)


def _block_diag_in_grad(d, G, P, H):
    J = G // GROUPS_PER_BLOCK
    d = d.reshape(J, GROUPS_PER_BLOCK, H, GROUPS_PER_BLOCK, P)
    idx = jnp.arange(GROUPS_PER_BLOCK)
    d = d[:, idx, :, idx, :]
    return d.transpose(1, 0, 3, 2).reshape(G, P, H)


def _block_diag_out(m):
    G, H, P = m.shape
    J = G // GROUPS_PER_BLOCK
    m = m.reshape(J, GROUPS_PER_BLOCK, H, P).transpose(0, 1, 3, 2)
    eye = jnp.eye(GROUPS_PER_BLOCK, dtype=m.dtype)
    out = m[:, :, :, None, :] * eye[None, :, None, :, None]
    return out.reshape(J, GROUPS_PER_BLOCK * P, GROUPS_PER_BLOCK * H)


def _block_diag_out_grad(d, G, H, P):
    J = G // GROUPS_PER_BLOCK
    d = d.reshape(J, GROUPS_PER_BLOCK, P, GROUPS_PER_BLOCK, H)
    idx = jnp.arange(GROUPS_PER_BLOCK)
    d = d[:, idx, :, idx, :]
    return d.transpose(1, 0, 3, 2).reshape(G, H, P)


def _interleave(a, B, S):
    L = S // SEGMENTS
    return a.reshape(B, SEGMENTS, L, a.shape[-1]).transpose(0, 2, 1, 3).reshape(B * S, a.shape[-1])


def _deinterleave(a, B, S):
    L = S // SEGMENTS
    return a.reshape(B, L, SEGMENTS, a.shape[-1]).transpose(0, 2, 1, 3).reshape(B * S, a.shape[-1])


def _adamw_math(w, g, m, v):
    m = ADAM_B1 * m + (1.0 - ADAM_B1) * g
    v = ADAM_B2 * v + (1.0 - ADAM_B2) * (g * g)
    m_hat = m / (1.0 - ADAM_B1 ** ADAM_STEP)
    v_hat = v / (1.0 - ADAM_B2 ** ADAM_STEP)
    delta = -ADAM_LR * (m_hat / (jnp.sqrt(v_hat) + ADAM_EPS) + ADAM_WD * w)
    return delta, m, v


def adamw(w, g, m, v, *, name):
    R, C = w.shape
    tr = _tile(R, (max(8, (1 << 18) // C // 8 * 8), 256, 128, 64, 32, 16, 8))

    def body(w_ref, g_ref, m_ref, v_ref, d_ref, nm_ref, nv_ref):
        d, nm, nv = _adamw_math(w_ref[...], g_ref[...], m_ref[...], v_ref[...])
        d_ref[...] = d
        nm_ref[...] = nm
        nv_ref[...] = nv

    spec = pl.BlockSpec((tr, C), lambda i: (i, 0))
    shp = jax.ShapeDtypeStruct((R, C), F32)
    return pl.pallas_call(
        body, name=name, out_shape=[shp, shp, shp], grid=(R // tr,), in_specs=[spec] * 4, out_specs=[spec] * 3,
        compiler_params=_params(("parallel",)),
    )(w, g, m, v)


def sum_leading(a, *, name, out_dtype=F32):
    n, R, C = a.shape
    tr = _tile(R, (256, 128, 64, 32, 16, 8))

    def body(a_ref, o_ref):
        acc = a_ref[0].astype(F32)
        for i in range(1, n):
            acc = acc + a_ref[i].astype(F32)
        o_ref[...] = acc.astype(o_ref.dtype)

    return pl.pallas_call(
        body, name=name, out_shape=jax.ShapeDtypeStruct((R, C), out_dtype), grid=(R // tr,),
        in_specs=[pl.BlockSpec((n, tr, C), lambda i: (0, i, 0))], out_specs=pl.BlockSpec((tr, C), lambda i: (i, 0)),
        compiler_params=_params(("parallel",)),
    )(a)


def _any_specs(n):
    return [pl.BlockSpec(memory_space=pl.ANY) for _ in range(n)]


def _coords():
    return lax.axis_index("x"), lax.axis_index("y"), lax.axis_index("c")


def _flip(v, bit):
    return (v + bit) % 2


def all_gather8(a, *, name):
    shape = a.shape

    def body(a_ref, o_ref, send_sems, recv_sems, local_sem):
        x, y, c = _coords()
        me = 4 * x + 2 * y + c
        mine = pltpu.make_async_copy(a_ref, o_ref.at[me], local_sem)
        mine.start()
        sends = []
        for k in range(1, N_DEV):
            peer = (_flip(x, (k >> 2) & 1), _flip(y, (k >> 1) & 1), _flip(c, k & 1))
            cp = pltpu.make_async_remote_copy(a_ref, o_ref.at[me], send_sems.at[k - 1], recv_sems.at[k - 1],
                                              device_id=peer, device_id_type=MESH)
            cp.start()
            sends.append(cp)
        for k in range(1, N_DEV):
            px, py, pc = _flip(x, (k >> 2) & 1), _flip(y, (k >> 1) & 1), _flip(c, k & 1)
            src = 4 * px + 2 * py + pc
            pltpu.make_async_remote_copy(a_ref, o_ref.at[src], send_sems.at[k - 1], recv_sems.at[k - 1],
                                         device_id=(px, py, pc), device_id_type=MESH).wait_recv()
        for cp in sends:
            cp.wait_send()
        mine.wait()

    return pl.pallas_call(
        body, name=name, out_shape=jax.ShapeDtypeStruct((N_DEV,) + shape, a.dtype),
        in_specs=_any_specs(1), out_specs=pl.BlockSpec(memory_space=pl.ANY),
        scratch_shapes=[pltpu.SemaphoreType.DMA((N_DEV - 1,)), pltpu.SemaphoreType.DMA((N_DEV - 1,)),
                        pltpu.SemaphoreType.DMA(())],
    )(a)


def _chip_of(x, y, p):
    px, py = _flip(x, (p >> 1) & 1), _flip(y, p & 1)
    return 2 * px + py, px, py


def gather_chip_shards(arrs, *, name):
    n = len(arrs)

    def body(*refs):
        ins, outs = refs[:n], refs[n:2 * n]
        ici_send, ici_recv, d2d_send, d2d_recv = refs[2 * n:]
        x, y, c = _coords()
        me = 2 * x + y
        sends = []
        for i in range(n):
            half = ins[i].shape[0] // 2
            rows = pl.ds(c * half, half)
            for p in range(1, N_CHIPS):
                _, px, py = _chip_of(x, y, p)
                s = i * 3 + p - 1
                cp = pltpu.make_async_remote_copy(ins[i].at[rows], outs[i].at[me, rows], ici_send.at[s], ici_recv.at[s],
                                                  device_id=(px, py, c), device_id_type=MESH)
                cp.start()
                sends.append(cp)
        for i in range(n):
            half = ins[i].shape[0] // 2
            rows = pl.ds(c * half, half)
            for p in range(1, N_CHIPS):
                src, px, py = _chip_of(x, y, p)
                s = i * 3 + p - 1
                pltpu.make_async_remote_copy(ins[i].at[rows], outs[i].at[src, rows], ici_send.at[s], ici_recv.at[s],
                                             device_id=(px, py, c), device_id_type=MESH).wait_recv()
                cp = pltpu.make_async_remote_copy(outs[i].at[src, rows], outs[i].at[src, rows], d2d_send.at[s],
                                                  d2d_recv.at[s], device_id=(x, y, 1 - c), device_id_type=MESH)
                cp.start()
                sends.append(cp)
        for i in range(n):
            half = ins[i].shape[0] // 2
            theirs = pl.ds((1 - c) * half, half)
            for p in range(1, N_CHIPS):
                src, _, _ = _chip_of(x, y, p)
                s = i * 3 + p - 1
                pltpu.make_async_remote_copy(outs[i].at[src, theirs], outs[i].at[src, theirs], d2d_send.at[s],
                                             d2d_recv.at[s], device_id=(x, y, 1 - c), device_id_type=MESH).wait_recv()
        for cp in sends:
            cp.wait_send()

    dma = pltpu.SemaphoreType.DMA
    got = pl.pallas_call(
        body, name=name,
        out_shape=[jax.ShapeDtypeStruct((N_CHIPS,) + a.shape, a.dtype) for a in arrs],
        in_specs=_any_specs(n), out_specs=_any_specs(n),
        scratch_shapes=[dma((3 * n,)), dma((3 * n,)), dma((3 * n,)), dma((3 * n,))],
    )(*arrs)
    chip = 2 * lax.axis_index("x") + lax.axis_index("y")
    own = lax.broadcasted_iota(jnp.int32, (N_CHIPS, 1, 1), 0) == chip
    return [jnp.where(own, a[None], g) for g, a in zip(got, arrs)]


def swap_halves(arrs, *, name):
    n = len(arrs)

    def body(*refs):
        ins, outs = refs[:n], refs[n:2 * n]
        send_sems, recv_sems = refs[2 * n:]
        x, y, c = _coords()
        cps = []
        for i in range(n):
            half = ins[i].shape[1] // 2
            cp = pltpu.make_async_remote_copy(ins[i].at[:, pl.ds((1 - c) * half, half)], outs[i], send_sems.at[i],
                                              recv_sems.at[i], device_id=(x, y, 1 - c), device_id_type=MESH)
            cp.start()
            cps.append(cp)
        for cp in cps:
            cp.wait()

    dma = pltpu.SemaphoreType.DMA
    return pl.pallas_call(
        body, name=name,
        out_shape=[jax.ShapeDtypeStruct((N_CHIPS, a.shape[1] // 2, a.shape[2]), a.dtype) for a in arrs],
        in_specs=_any_specs(n), out_specs=_any_specs(n), scratch_shapes=[dma((n,)), dma((n,))],
    )(*arrs)


def add_half(g, other, c_idx, *, name, out_dtype):
    _, R, C = g.shape
    half = R // 2
    tr = _tile(half, (256, 128, 64, 32, 16, 8))
    nt = half // tr

    def body(c_ref, g_ref, o_ref, out_ref):
        out_ref[...] = (g_ref[...] + o_ref[...]).astype(out_ref.dtype)

    return pl.pallas_call(
        body, name=name, out_shape=jax.ShapeDtypeStruct((N_CHIPS, half, C), out_dtype),
        grid_spec=pltpu.PrefetchScalarGridSpec(
            num_scalar_prefetch=1, grid=(N_CHIPS, nt),
            in_specs=[pl.BlockSpec((None, tr, C), lambda r, t, c_ref: (r, c_ref[0] * nt + t, 0)),
                      pl.BlockSpec((None, tr, C), lambda r, t, c_ref: (r, t, 0))],
            out_specs=pl.BlockSpec((None, tr, C), lambda r, t, c_ref: (r, t, 0))),
        compiler_params=_params(("parallel", "parallel")),
    )(c_idx, g, other)


def scatter_to_chips(arrs, *, name):
    n = len(arrs)

    def body(*refs):
        ins, outs = refs[:n], refs[n:2 * n]
        send_sems, recv_sems = refs[2 * n:]
        x, y, c = _coords()
        cps = []
        for i in range(n):
            for p in range(1, N_CHIPS):
                dst, px, py = _chip_of(x, y, p)
                s = i * 3 + p - 1
                cp = pltpu.make_async_remote_copy(ins[i].at[dst], outs[i].at[p - 1], send_sems.at[s], recv_sems.at[s],
                                                  device_id=(px, py, c), device_id_type=MESH)
                cp.start()
                cps.append(cp)
        for cp in cps:
            cp.wait()

    dma = pltpu.SemaphoreType.DMA
    return pl.pallas_call(
        body, name=name,
        out_shape=[jax.ShapeDtypeStruct((N_CHIPS - 1,) + a.shape[1:], a.dtype) for a in arrs],
        in_specs=_any_specs(n), out_specs=_any_specs(n), scratch_shapes=[dma((3 * n,)), dma((3 * n,))],
    )(*arrs)


def add_chips(h, got, r_idx, *, name):
    _, R, C = h.shape
    tr = _tile(R, (256, 128, 64, 32, 16, 8))

    def body(r_ref, h_ref, g_ref, out_ref):
        acc = h_ref[...].astype(F32)
        for p in range(N_CHIPS - 1):
            acc = acc + g_ref[p].astype(F32)
        out_ref[...] = acc

    return pl.pallas_call(
        body, name=name, out_shape=jax.ShapeDtypeStruct((R, C), F32),
        grid_spec=pltpu.PrefetchScalarGridSpec(
            num_scalar_prefetch=1, grid=(R // tr,),
            in_specs=[pl.BlockSpec((None, tr, C), lambda t, r_ref: (r_ref[0], t, 0)),
                      pl.BlockSpec((N_CHIPS - 1, tr, C), lambda t, r_ref: (0, t, 0))],
            out_specs=pl.BlockSpec((tr, C), lambda t, r_ref: (t, 0))),
        compiler_params=_params(("parallel",)),
    )(r_idx, h, got)


def swap_with_sibling(arrs, *, name):
    n = len(arrs)

    def body(*refs):
        ins, outs = refs[:n], refs[n:2 * n]
        send_sems, recv_sems = refs[2 * n:]
        x, y, c = _coords()
        cps = []
        for i in range(n):
            cp = pltpu.make_async_remote_copy(ins[i], outs[i], send_sems.at[i], recv_sems.at[i],
                                              device_id=(x, y, 1 - c), device_id_type=MESH)
            cp.start()
            cps.append(cp)
        for cp in cps:
            cp.wait()

    dma = pltpu.SemaphoreType.DMA
    return pl.pallas_call(
        body, name=name, out_shape=[jax.ShapeDtypeStruct(a.shape, a.dtype) for a in arrs],
        in_specs=_any_specs(n), out_specs=_any_specs(n), scratch_shapes=[dma((n,)), dma((n,))],
    )(*arrs)


def join_halves(arrs, *, name):
    theirs = swap_with_sibling(arrs, name=name)
    first = lax.axis_index("c") == 0
    return [jnp.concatenate([jnp.where(first, a, b), jnp.where(first, b, a)], axis=0) for a, b in zip(arrs, theirs)]


def reduce_scatter_chips(grads):
    x, y, c = _coords()
    c_idx = jnp.reshape(c, (1,)).astype(jnp.int32)
    r_idx = jnp.reshape(2 * x + y, (1,)).astype(jnp.int32)
    theirs = swap_halves(grads, name="rs_swap_halves")
    pair = [add_half(g, o, c_idx, name=f"rs_add_half_{i}", out_dtype=BF16) for i, (g, o) in enumerate(zip(grads, theirs))]
    got = scatter_to_chips(pair, name="rs_scatter_to_chips")
    mine = [add_chips(h, g, r_idx, name=f"rs_add_chips_{i}") for i, (h, g) in enumerate(zip(pair, got))]
    return join_halves(mine, name="rs_join_halves")


def _to_heads(t, B, S):
    return t.reshape(B, S, -1, HEAD_DIM).transpose(0, 2, 1, 3)


def _from_heads(t, B, S):
    return t.transpose(0, 2, 1, 3).reshape(B * S, -1)


def _chip_major(w, axis):
    n = w.shape[axis] // N_CHIPS
    parts = w.reshape(w.shape[:axis] + (N_CHIPS, n) + w.shape[axis + 1:])
    return jnp.moveaxis(parts, axis, 0)


def _from_chip_major(g, axis):
    g = jnp.moveaxis(g, 0, axis)
    return g.reshape(g.shape[:axis] + (g.shape[axis] * g.shape[axis + 1],) + g.shape[axis + 2:])


def kernel(x, c, norm_mix, norm_ffn, w_mod, b_mod, w_qkv, w_o_attn, w_in_ssm, a_re, a_im, log_dt, b_re, b_im, c_re, c_im, d_skip, w_glu, b_glu, w_o_ssm, w_up, conv_w, conv_b, w_down, norm_out, w_fin, b_fin, loss_target, m_norm_mix, m_norm_ffn, m_w_mod, m_b_mod, m_w_qkv, m_w_o_attn, m_w_in_ssm, m_a_re, m_a_im, m_log_dt, m_b_re, m_b_im, m_c_re, m_c_im, m_d_skip, m_w_glu, m_b_glu, m_w_o_ssm, m_w_up, m_conv_w, m_conv_b, m_w_down, m_norm_out, m_w_fin, m_b_fin, v_norm_mix, v_norm_ffn, v_w_mod, v_b_mod, v_w_qkv, v_w_o_attn, v_w_in_ssm, v_a_re, v_a_im, v_log_dt, v_b_re, v_b_im, v_c_re, v_c_im, v_d_skip, v_w_glu, v_b_glu, v_w_o_ssm, v_w_up, v_conv_w, v_conv_b, v_w_down, v_norm_out, v_w_fin, v_b_fin):
    B, S, D = x.shape
    T = B * S
    F2 = conv_b.shape[1]
    F = F2 // 2
    G, P = a_re.shape[1], a_re.shape[2]
    H = b_re.shape[3]
    mx, my, mc = _coords()
    chip = 2 * mx + my
    dev = 4 * mx + 2 * my + mc
    BG = N_DEV * B
    mod_w = w_mod.shape[2]
    fin_w = w_fin.shape[1]

    c_all = all_gather8(c, name="gather_c").reshape(BG, D)
    c_act = silu_rows(c_all, name="silu_c")
    b_mod_mine = lax.dynamic_slice(b_mod, (0, chip * mod_w), (2, mod_w))
    b_fin_mine = lax.dynamic_slice(b_fin, (chip * fin_w,), (fin_w,))
    cond = [matmul(c_act, w_mod[i], bias=b_mod_mine[i], name=f"mod_proj_{i}") for i in range(2)]
    cond.append(matmul(c_act, w_fin, bias=b_fin_mine, name="fin_proj"))
    cond_all = all_gather8(jnp.concatenate(cond, axis=1), name="gather_cond")
    cond_all = cond_all[::2]
    cond_rows = lax.dynamic_slice(cond_all, (0, dev * B, 0), (N_CHIPS, B, cond_all.shape[2]))
    mods = []
    for i in range(2):
        full = cond_rows[:, :, i * mod_w:(i + 1) * mod_w].transpose(1, 0, 2).reshape(B, N_CHIPS * mod_w)
        mods.append([full[:, k * D:(k + 1) * D] for k in range(6)])
    fin = cond_rows[:, :, 2 * mod_w:].transpose(1, 0, 2).reshape(B, N_CHIPS * fin_w)
    sh_f, sc_f = fin[:, :D], fin[:, D:]

    rows1024 = jnp.concatenate([w_o_attn[0], w_in_ssm[0], w_glu[0], w_o_ssm[0], w_down.reshape(-1, D)], axis=0)
    g_qkv, g_rows, g_up = gather_chip_shards(
        [w_qkv[0].astype(BF16), rows1024.astype(BF16), w_up.reshape(2 * D, -1).astype(BF16)], name="gather_weights")
    Dq = D // N_CHIPS
    Fq = F // N_CHIPS
    W_qkv = _from_chip_major(g_qkv, 1)
    W_o_attn = g_rows[:, 0 * Dq:1 * Dq].reshape(D, D)
    W_in = g_rows[:, 1 * Dq:2 * Dq].reshape(D, D)
    W_glu = g_rows[:, 2 * Dq:3 * Dq].reshape(D, D)
    W_o_ssm = g_rows[:, 3 * Dq:4 * Dq].reshape(D, D)
    W_down = [g_rows[:, 4 * Dq + i * Fq:4 * Dq + (i + 1) * Fq].reshape(F, D) for i in range(2)]
    W_up = [_from_chip_major(g_up[:, i * D:(i + 1) * D], 1) for i in range(2)]
    small = jnp.concatenate([conv_w.reshape(6, -1), jnp.pad(d_skip, ((0, 0), (0, conv_w.shape[2] - Dq))),
                             jnp.pad(b_glu, ((0, 0), (0, conv_w.shape[2] - Dq)))], axis=0)
    small_all = all_gather8(small, name="gather_small")[::2]
    conv_w_full = _from_chip_major(small_all[:, :6].reshape(N_CHIPS, 2, 3, -1), 2)
    d_skip_full = small_all[:, 6, :Dq].reshape(1, D)
    b_glu_full = small_all[:, 7, :Dq].reshape(D)

    x0 = x.reshape(T, D)
    tgt = loss_target.reshape(T, D)

    def ffn_fwd(xin, i):
        sh2, sc2, g2 = mods[i][3], mods[i][4], mods[i][5]
        h2 = norm_mod_fwd(xin, norm_ffn[i], sh2, sc2, B=B, S=S, name=f"ffn_norm_{i}")
        up = matmul(h2, W_up[i], name=f"ffn_up_{i}")
        act = conv_gate_fwd(up, conv_w_full[i], conv_b[i:i + 1], B=B, S=S, name=f"ffn_conv_{i}")
        yf = matmul(act, W_down[i], name=f"ffn_down_{i}")
        xout = gate_res_fwd(xin, yf, g2, B=B, S=S, name=f"ffn_res_{i}")
        return xout, (xin, h2, up, act, yf)

    sh1, sc1, g1 = mods[0][0], mods[0][1], mods[0][2]
    h1a = norm_mod_fwd(x0, norm_mix[0], sh1, sc1, B=B, S=S, name="att_norm")
    qkv = matmul(h1a, W_qkv, out_dtype=BF16, name="att_qkv")
    q, k, v = [_to_heads(qkv[:, i * D:(i + 1) * D], B, S) for i in range(3)]
    o, ltot = attn_fwd(q, k, v, name="att_fwd")
    o2 = _from_heads(o, B, S).astype(BF16)
    ya = matmul(o2, W_o_attn, name="att_out")
    x1 = gate_res_fwd(x0, ya, g1, B=B, S=S, name="att_res")
    x2, ffn0 = ffn_fwd(x1, 0)

    lr, li, bbr, bbi = _ssm_discretize(a_re[0], a_im[0], log_dt[0], b_re[0], b_im[0])
    J = G // GROUPS_PER_BLOCK
    Wst = GROUPS_PER_BLOCK * P
    bre_blk = _block_diag_in(bbr).astype(BF16)
    bim_blk = _block_diag_in(bbi).astype(BF16)
    cre_blk = _block_diag_out(c_re[0]).astype(BF16)
    cim_blk = _block_diag_out(c_im[0]).astype(BF16)
    lr8 = jnp.broadcast_to(lr.reshape(J, 1, Wst), (J, 8, Wst))
    li8 = jnp.broadcast_to(li.reshape(J, 1, Wst), (J, 8, Wst))
    sh1s, sc1s, g1s = mods[1][0], mods[1][1], mods[1][2]
    h1s = norm_mod_fwd(x2, norm_mix[1], sh1s, sc1s, B=B, S=S, name="ssm_norm")
    h1p = _interleave(h1s, B, S)
    u = matmul(h1p, W_in, name="ssm_in")
    y_ssm = ssm_fwd(u, bre_blk, bim_blk, cre_blk, cim_blk, lr8, li8, d_skip_full, B=B, S=S, name="ssm_scan_fwd")
    zb = gelu_fwd(y_ssm, B=B, S=S, name="ssm_gelu")
    s_glu = matmul(zb, W_glu, bias=b_glu_full, name="ssm_glu_proj")
    gb = glu_fwd(y_ssm, s_glu, B=B, S=S, name="ssm_glu")
    ys_p = matmul(gb, W_o_ssm, name="ssm_out")
    ys = _deinterleave(ys_p, B, S)
    x3 = gate_res_fwd(x2, ys, g1s, B=B, S=S, name="ssm_res")
    x4, ffn1 = ffn_fwd(x3, 1)

    dx4, loss_p, dsh_f, dsc_f, dnorm_out = final_loss(x4, tgt, norm_out, sh_f, sc_f, B=B, S=S, name="loss_head")
    loss = lax.psum(jnp.sum(loss_p), ("x", "y", "c"))

    def ffn_bwd(dxo, i, saved):
        xin, h2, up, act, yf = saved
        sc2, g2 = mods[i][4], mods[i][5]
        dyf, dg2 = gate_res_bwd(dxo, yf, g2, B=B, S=S, name=f"ffn_res_bwd_{i}")
        dact = matmul(dyf, W_down[i], tb=True, name=f"ffn_down_dx_{i}")
        dW_down = matmul(act, dyf, ta=True, name=f"ffn_down_dw_{i}")
        d3, dcb = conv_gate_bwd1(up, dact, conv_w_full[i], conv_b[i:i + 1], B=B, S=S, name=f"ffn_conv_bwd1_{i}")
        dup, dcw = conv_bwd2(d3, up, conv_w_full[i], B=B, S=S, name=f"ffn_conv_bwd2_{i}")
        dh2 = matmul(dup, W_up[i], tb=True, name=f"ffn_up_dx_{i}")
        dW_up = matmul(h2, dup, ta=True, name=f"ffn_up_dw_{i}")
        dxin, dsh2, dsc2, dnf = norm_mod_bwd(dh2, xin, dxo, norm_ffn[i], sc2, B=B, S=S, name=f"ffn_norm_bwd_{i}")
        return dxin, dict(dW_down=dW_down, dW_up=dW_up, dconv_b=jnp.sum(dcb, axis=0).reshape(F2),
                          dconv_w=jnp.sum(dcw, axis=0), dnorm_ffn=jnp.sum(dnf, axis=0), dsh2=dsh2, dsc2=dsc2, dg2=dg2)

    dx3, gf1 = ffn_bwd(dx4, 1, ffn1)

    dys_p, dg1s = gate_res_bwd(_interleave(dx3, B, S), ys_p, g1s, B=B, S=S, name="ssm_res_bwd")
    dgb = matmul(dys_p, W_o_ssm, tb=True, name="ssm_out_dx")
    dW_o_ssm = matmul(gb, dys_p, ta=True, name="ssm_out_dw")
    ds_glu, dz1, db_glu = glu_bwd1(y_ssm, s_glu, dgb, B=B, S=S, name="ssm_glu_bwd1")
    dz2 = matmul(ds_glu, W_glu, tb=True, name="ssm_glu_dx")
    dW_glu = matmul(zb, ds_glu, ta=True, name="ssm_glu_dw")
    dy_ssm = glu_bwd2(y_ssm, dz1, dz2, B=B, S=S, name="ssm_glu_bwd2")
    du, dbre, dbim, dcre, dcim, dlr8, dli8, ddsk = ssm_bwd(u, dy_ssm, bre_blk, bim_blk, cre_blk, cim_blk, lr8, li8,
                                                           d_skip_full, B=B, S=S, name="ssm_scan_bwd")
    dub = du.astype(BF16)
    dh1p = matmul(dub, W_in, tb=True, name="ssm_in_dx")
    dW_in = matmul(h1p, dub, ta=True, name="ssm_in_dw")
    dx2, dsh1s, dsc1s, dnm1 = norm_mod_bwd(_deinterleave(dh1p, B, S), x2, dx3, norm_mix[1], sc1s, B=B, S=S,
                                           name="ssm_norm_bwd")
    dlr = jnp.sum(dlr8, axis=(0, 2)).reshape(G, P)
    dli = jnp.sum(dli8, axis=(0, 2)).reshape(G, P)
    dbbr = _block_diag_in_grad(jnp.sum(dbre, axis=0), G, P, H)
    dbbi = _block_diag_in_grad(jnp.sum(dbim, axis=0), G, P, H)
    dc_re = _block_diag_out_grad(jnp.sum(dcre, axis=0), G, H, P)
    dc_im = _block_diag_out_grad(jnp.sum(dcim, axis=0), G, H, P)
    dd_skip = jnp.sum(ddsk, axis=0).reshape(D)

    dx1, gf0 = ffn_bwd(dx2, 0, ffn0)

    dya, dg1 = gate_res_bwd(dx1, ya, g1, B=B, S=S, name="att_res_bwd")
    do2 = matmul(dya, W_o_attn, tb=True, out_dtype=BF16, name="att_out_dx")
    dW_o_attn = matmul(o2, dya, ta=True, name="att_out_dw")
    dq, dk, dv = attn_bwd(q, k, v, ltot, _to_heads(do2, B, S), name="att_bwd")
    dqkv = jnp.concatenate([_from_heads(t, B, S) for t in (dq, dk, dv)], axis=1)
    dh1a = matmul(dqkv, W_qkv, tb=True, name="att_qkv_dx")
    dW_qkv = matmul(h1a, dqkv, ta=True, name="att_qkv_dw")
    grad_x, dsh1, dsc1, dnm0 = norm_mod_bwd(dh1a, x0, dx1, norm_mix[0], sc1, B=B, S=S, name="att_norm_bwd")

    g_rows_cm = jnp.concatenate([dW_o_attn.reshape(N_CHIPS, Dq, D), dW_in.reshape(N_CHIPS, Dq, D),
                                 dW_glu.reshape(N_CHIPS, Dq, D), dW_o_ssm.reshape(N_CHIPS, Dq, D),
                                 gf0["dW_down"].reshape(N_CHIPS, Fq, D), gf1["dW_down"].reshape(N_CHIPS, Fq, D)], axis=1)
    g_up_cm = jnp.concatenate([_chip_major(gf0["dW_up"], 1), _chip_major(gf1["dW_up"], 1)], axis=1)
    r_qkv, r_rows, r_up = reduce_scatter_chips([_chip_major(dW_qkv, 1), g_rows_cm, g_up_cm])
    grad_w_qkv = r_qkv[None]
    grad_w_o_attn = r_rows[0 * Dq:1 * Dq][None]
    grad_w_in_ssm = r_rows[1 * Dq:2 * Dq][None]
    grad_w_glu = r_rows[2 * Dq:3 * Dq][None]
    grad_w_o_ssm = r_rows[3 * Dq:4 * Dq][None]
    grad_w_down = r_rows[4 * Dq:].reshape(2, Fq, D)
    grad_w_up = r_up.reshape(2, D, -1)

    dmod_rows = jnp.concatenate([dsh1, dsc1, dg1, gf0["dsh2"], gf0["dsc2"], gf0["dg2"],
                                 dsh1s, dsc1s, dg1s, gf1["dsh2"], gf1["dsc2"], gf1["dg2"], dsh_f, dsc_f], axis=1)
    dmod_all = all_gather8(dmod_rows, name="gather_dmod").reshape(BG, 14 * D)
    grad_w_mod = jnp.stack([
        matmul(c_act, lax.dynamic_slice(dmod_all, (0, i * 6 * D + chip * mod_w), (BG, mod_w)), ta=True,
               name=f"mod_dw_{i}") for i in range(2)])
    grad_w_fin = matmul(c_act, lax.dynamic_slice(dmod_all, (0, 12 * D + chip * fin_w), (BG, fin_w)), ta=True,
                        name="fin_dw")

    parts = [jnp.concatenate([jnp.sum(dnm0, axis=0), jnp.sum(dnm1, axis=0)]),
             jnp.concatenate([gf0["dnorm_ffn"], gf1["dnorm_ffn"]]),
             jnp.sum(dmod_rows[:, :12 * D], axis=0),
             dlr.reshape(-1), dli.reshape(-1), dbbr.reshape(-1), dbbi.reshape(-1), dc_re.reshape(-1), dc_im.reshape(-1),
             dd_skip, jnp.sum(db_glu, axis=0),
             gf0["dconv_w"].reshape(-1), gf1["dconv_w"].reshape(-1), gf0["dconv_b"], gf1["dconv_b"],
             jnp.sum(dnorm_out, axis=0), jnp.sum(dmod_rows[:, 12 * D:], axis=0)]
    sizes = [int(p.shape[0]) for p in parts]
    flat = jnp.concatenate(parts)
    width = 1024
    padded = -(-flat.shape[0] // (8 * width)) * (8 * width)
    flat = jnp.pad(flat, (0, padded - flat.shape[0])).reshape(-1, width)
    summed = sum_leading(all_gather8(flat, name="gather_small_grads"), name="sum_small_grads").reshape(-1)
    offs = [0]
    for s_ in sizes:
        offs.append(offs[-1] + s_)
    (s_nm, s_nf, s_bmod, s_lr, s_li, s_bbr, s_bbi, s_cre, s_cim, s_dsk, s_bglu, s_cw0, s_cw1, s_cb0, s_cb1, s_no,
     s_bfin) = [summed[offs[i]:offs[i + 1]] for i in range(len(sizes))]
    _, disc_vjp = jax.vjp(_ssm_discretize, a_re[0], a_im[0], log_dt[0], b_re[0], b_im[0])
    ga_re, ga_im, glog_dt, gb_re, gb_im = disc_vjp((s_lr.reshape(G, P), s_li.reshape(G, P), s_bbr.reshape(G, P, H),
                                                    s_bbi.reshape(G, P, H)))
    grad_norm_mix = s_nm.reshape(2, D)
    grad_norm_ffn = s_nf.reshape(2, D)
    grad_b_mod = s_bmod.reshape(2, 6 * D)
    grad_c_re = s_cre.reshape(1, G, H, P)
    grad_c_im = s_cim.reshape(1, G, H, P)
    grad_d_skip = lax.dynamic_slice(s_dsk, (chip * Dq,), (Dq,)).reshape(1, Dq)
    grad_b_glu = lax.dynamic_slice(s_bglu, (chip * Dq,), (Dq,)).reshape(1, Dq)
    cw_full = jnp.stack([s_cw0.reshape(3, F2), s_cw1.reshape(3, F2)])
    grad_conv_w = lax.dynamic_slice(cw_full, (0, 0, chip * (F2 // N_CHIPS)), (2, 3, F2 // N_CHIPS))
    grad_conv_b = jnp.stack([s_cb0, s_cb1])
    grad_norm_out = s_no
    grad_b_fin = s_bfin

    grads = dict(
        norm_mix=grad_norm_mix, norm_ffn=grad_norm_ffn, w_mod=grad_w_mod, b_mod=grad_b_mod, w_qkv=grad_w_qkv,
        w_o_attn=grad_w_o_attn, w_in_ssm=grad_w_in_ssm, a_re=ga_re[None], a_im=ga_im[None], log_dt=glog_dt[None],
        b_re=gb_re[None], b_im=gb_im[None], c_re=grad_c_re, c_im=grad_c_im, d_skip=grad_d_skip, w_glu=grad_w_glu,
        b_glu=grad_b_glu, w_o_ssm=grad_w_o_ssm, w_up=grad_w_up, conv_w=grad_conv_w, conv_b=grad_conv_b,
        w_down=grad_w_down, norm_out=grad_norm_out, w_fin=grad_w_fin, b_fin=grad_b_fin)
    weights = dict(
        norm_mix=norm_mix, norm_ffn=norm_ffn, w_mod=w_mod, b_mod=b_mod, w_qkv=w_qkv, w_o_attn=w_o_attn,
        w_in_ssm=w_in_ssm, a_re=a_re, a_im=a_im, log_dt=log_dt, b_re=b_re, b_im=b_im, c_re=c_re, c_im=c_im,
        d_skip=d_skip, w_glu=w_glu, b_glu=b_glu, w_o_ssm=w_o_ssm, w_up=w_up, conv_w=conv_w, conv_b=conv_b,
        w_down=w_down, norm_out=norm_out, w_fin=w_fin, b_fin=b_fin)
    m_in = dict(
        norm_mix=m_norm_mix, norm_ffn=m_norm_ffn, w_mod=m_w_mod, b_mod=m_b_mod, w_qkv=m_w_qkv, w_o_attn=m_w_o_attn,
        w_in_ssm=m_w_in_ssm, a_re=m_a_re, a_im=m_a_im, log_dt=m_log_dt, b_re=m_b_re, b_im=m_b_im, c_re=m_c_re,
        c_im=m_c_im, d_skip=m_d_skip, w_glu=m_w_glu, b_glu=m_b_glu, w_o_ssm=m_w_o_ssm, w_up=m_w_up, conv_w=m_conv_w,
        conv_b=m_conv_b, w_down=m_w_down, norm_out=m_norm_out, w_fin=m_w_fin, b_fin=m_b_fin)
    v_in = dict(
        norm_mix=v_norm_mix, norm_ffn=v_norm_ffn, w_mod=v_w_mod, b_mod=v_b_mod, w_qkv=v_w_qkv, w_o_attn=v_w_o_attn,
        w_in_ssm=v_w_in_ssm, a_re=v_a_re, a_im=v_a_im, log_dt=v_log_dt, b_re=v_b_re, b_im=v_b_im, c_re=v_c_re,
        c_im=v_c_im, d_skip=v_d_skip, w_glu=v_w_glu, b_glu=v_b_glu, w_o_ssm=v_w_o_ssm, w_up=v_w_up, conv_w=v_conv_w,
        conv_b=v_conv_b, w_down=v_w_down, norm_out=v_norm_out, w_fin=v_w_fin, b_fin=v_b_fin)
    names = list(weights)
    for n_ in names:
        grads[n_] = grads[n_].reshape(weights[n_].shape)

    big = ("w_mod", "w_qkv", "w_o_attn", "w_in_ssm", "w_glu", "w_o_ssm", "w_up", "w_down", "w_fin")
    delta, new_m, new_v = {}, {}, {}
    for n_ in big:
        shp = weights[n_].shape
        two_d = lambda a: a.reshape(-1, shp[-1])
        d_, m_, v_ = adamw(two_d(weights[n_]), two_d(grads[n_]), two_d(m_in[n_]), two_d(v_in[n_]), name=f"adamw_{n_}")
        delta[n_], new_m[n_], new_v[n_] = d_.reshape(shp), m_.reshape(shp), v_.reshape(shp)
    rest = [n_ for n_ in names if n_ not in big]

    def pack(tree):
        f = jnp.concatenate([tree[n_].reshape(-1) for n_ in rest])
        pad_to = -(-f.shape[0] // (8 * width)) * (8 * width)
        return jnp.pad(f, (0, pad_to - f.shape[0]), constant_values=1.0).reshape(-1, width)

    d_, m_, v_ = adamw(pack(weights), pack(grads), pack(m_in), pack(v_in), name="adamw_small")
    off = 0
    for n_ in rest:
        sz = int(math.prod(weights[n_].shape))
        shp = weights[n_].shape
        delta[n_] = d_.reshape(-1)[off:off + sz].reshape(shp)
        new_m[n_] = m_.reshape(-1)[off:off + sz].reshape(shp)
        new_v[n_] = v_.reshape(-1)[off:off + sz].reshape(shp)
        off += sz

    return (loss, grad_x.reshape(B, S, D), *[grads[n_] for n_ in names], *[delta[n_] for n_ in names],
            *[new_m[n_] for n_ in names], *[new_v[n_] for n_ in names])
```

```python
import functools
import math

import jax
import jax.numpy as jnp
from jax import lax
from jax.experimental import pallas as pl
from jax.experimental.pallas import tpu as pltpu

F32 = jnp.float32
BF16 = jnp.bfloat16
MESH = pl.DeviceIdType.MESH

HEAD_DIM = 64
SSM_GROUP = 16
STATE = 64
GROUPS_PER_BLOCK = 8
SEGMENTS = 16
SCAN_UNROLL = 4
EPS = 1e-6
ADAM_LR = 0.001
ADAM_B1 = 0.9
ADAM_B2 = 0.999
ADAM_EPS = 1e-08
ADAM_WD = 0.01
ADAM_STEP = 10
N_CHIPS = 4
N_DEV = 8
V7X_VMEM_LIMIT = 56 * 1024 * 1024
ATT_BLOCK = 128
ATT_HEADS = 8
ATT_HEADS_BWD = 4


def _tile(n, prefs):
    for p in prefs:
        if n % p == 0:
            return p
    return n


def _params(sem, vmem=V7X_VMEM_LIMIT):
    return pltpu.CompilerParams(dimension_semantics=sem, vmem_limit_bytes=vmem)


def matmul(a, b, *, ta=False, tb=False, bias=None, out_dtype=F32, name):
    if ta:
        K, M = a.shape
    else:
        M, K = a.shape
    if tb:
        N, Kb = b.shape
    else:
        Kb, N = b.shape
    assert K == Kb, (a.shape, b.shape, ta, tb)
    tm = _tile(M, (1024, 512, 256, 128))
    tn = _tile(N, (1024, 1408, 768, 512, 256, 128))
    tk = K if K <= 2816 else _tile(K, (1024, 512, 256, 128))
    nk = K // tk
    dims = (((0,) if ta else (1,), (1,) if tb else (0,)), ((), ()))

    def body(*refs):
        a_ref, b_ref = refs[:2]
        bias_ref = refs[2] if bias is not None else None
        o_ref = refs[-2] if nk > 1 else refs[-1]

        def finish(r):
            if bias_ref is not None:
                r = r + bias_ref[...]
            o_ref[...] = r.astype(o_ref.dtype)

        prod = lax.dot_general(a_ref[...].astype(BF16), b_ref[...].astype(BF16), dims, preferred_element_type=F32)
        if nk == 1:
            finish(prod)
            return
        acc_ref = refs[-1]
        k = pl.program_id(2)

        @pl.when(k == 0)
        def _():
            acc_ref[...] = prod

        @pl.when(k > 0)
        def _():
            acc_ref[...] += prod

        @pl.when(k == nk - 1)
        def _():
            finish(acc_ref[...])

    a_spec = pl.BlockSpec((tk, tm), lambda i, j, k: (k, i)) if ta else pl.BlockSpec((tm, tk), lambda i, j, k: (i, k))
    b_spec = pl.BlockSpec((tn, tk), lambda i, j, k: (j, k)) if tb else pl.BlockSpec((tk, tn), lambda i, j, k: (k, j))
    in_specs = [a_spec, b_spec]
    args = [a, b]
    if bias is not None:
        in_specs.append(pl.BlockSpec((1, tn), lambda i, j, k: (0, j)))
        args.append(bias.reshape(1, N).astype(F32))
    return pl.pallas_call(
        body, name=name,
        out_shape=jax.ShapeDtypeStruct((M, N), out_dtype),
        grid=(M // tm, N // tn, nk),
        in_specs=in_specs,
        out_specs=pl.BlockSpec((tm, tn), lambda i, j, k: (i, j)),
        scratch_shapes=[pltpu.VMEM((tm, tn), F32)] if nk > 1 else [],
        compiler_params=_params(("parallel", "parallel", "arbitrary")),
    )(*args)


def rowwise(fn, tiled, per_seq, glob, out_tiled, out_seq, *, B, S, name, rows=512):
    tm = _tile(S, (rows, 128, 64, 32, 16, 8))
    nt = S // tm
    n_in = len(tiled) + len(per_seq) + len(glob)
    n_ot = len(out_tiled)

    def body(*refs):
        ins = refs[:n_in]
        outs = refs[n_in:]
        vals = fn(*[r[...] for r in ins])
        if not isinstance(vals, (tuple, list)):
            vals = (vals,)
        assert len(vals) == len(outs), (name, len(vals), len(outs))
        for o_ref, v in zip(outs[:n_ot], vals[:n_ot]):
            o_ref[...] = v.astype(o_ref.dtype)
        t = pl.program_id(1)
        for o_ref, v in zip(outs[n_ot:], vals[n_ot:]):
            def first(o_ref=o_ref, v=v):
                o_ref[...] = v.astype(F32)

            def later(o_ref=o_ref, v=v):
                o_ref[...] += v.astype(F32)

            pl.when(t == 0)(first)
            pl.when(t > 0)(later)

    in_specs = [pl.BlockSpec((tm, a.shape[1]), lambda b, t: (b * nt + t, 0)) for a in tiled]
    in_specs += [pl.BlockSpec((None, 1, a.shape[1]), lambda b, t: (b, 0, 0)) for a in per_seq]
    in_specs += [pl.BlockSpec(a.shape, lambda b, t: (0,) * a.ndim) for a in glob]
    out_shape = [jax.ShapeDtypeStruct((B * S, w), dt) for w, dt in out_tiled]
    out_shape += [jax.ShapeDtypeStruct((B, 1, w), F32) for w in out_seq]
    out_specs = [pl.BlockSpec((tm, w), lambda b, t: (b * nt + t, 0)) for w, _ in out_tiled]
    out_specs += [pl.BlockSpec((None, 1, w), lambda b, t: (b, 0, 0)) for w in out_seq]
    res = pl.pallas_call(
        body, name=name, out_shape=out_shape, grid=(B, nt), in_specs=in_specs, out_specs=out_specs,
        compiler_params=_params(("parallel", "arbitrary")),
    )(*tiled, *[a.reshape(B, 1, a.shape[1]) for a in per_seq], *glob)
    res = list(res)
    for i in range(n_ot, len(res)):
        res[i] = res[i].reshape(B, res[i].shape[-1])
    return res


def _rms(x):
    r = lax.rsqrt(jnp.mean(x * x, axis=-1, keepdims=True) + EPS)
    return x * r, r


def norm_mod_fwd(x, g, sh, sc, *, B, S, name):
    def fn(x, sh, sc, g):
        xn, _ = _rms(x)
        return (xn * g) * (1.0 + sc) + sh

    return rowwise(fn, [x], [sh, sc], [g.reshape(1, -1)], [(x.shape[1], BF16)], [], B=B, S=S, name=name)[0]


def _norm_mod_bwd_math(dh, x, sc, g):
    xn, r = _rms(x)
    y = xn * g
    dy = dh * (1.0 + sc)
    dxn = dy * g
    dx = r * (dxn - xn * jnp.mean(dxn * xn, axis=-1, keepdims=True))
    dsh = jnp.sum(dh, axis=0, keepdims=True)
    dsc = jnp.sum(dh * y, axis=0, keepdims=True)
    dg = jnp.sum(dy * xn, axis=0, keepdims=True)
    return dx, dsh, dsc, dg


def norm_mod_bwd(dh, x, dres, g, sc, *, B, S, name):
    D = x.shape[1]

    def fn(dh, x, dres, sc, g):
        dx, dsh, dsc, dg = _norm_mod_bwd_math(dh.astype(F32), x, sc, g)
        return dres + dx, dsh, dsc, dg

    return rowwise(fn, [dh, x, dres], [sc], [g.reshape(1, -1)], [(D, F32)], [D, D, D], B=B, S=S, name=name)


def gate_res_fwd(x, y, gate, *, B, S, name):
    return rowwise(lambda x, y, g: x + g * y, [x, y], [gate], [], [(x.shape[1], F32)], [], B=B, S=S, name=name)[0]


def gate_res_bwd(dx, y, gate, *, B, S, name):
    D = dx.shape[1]

    def fn(dx, y, g):
        return g * dx, jnp.sum(dx * y, axis=0, keepdims=True)

    return rowwise(fn, [dx, y], [gate], [], [(D, BF16)], [D], B=B, S=S, name=name)


def final_loss(x, tgt, g, sh, sc, *, B, S, name):
    D = x.shape[1]

    def fn(x, tgt, sh, sc, g):
        xn, _ = _rms(x)
        y = (xn * g) * (1.0 + sc) + sh
        err = y - tgt
        loss = 0.5 * jnp.sum(err * err, axis=0, keepdims=True) * (1.0 / D)
        dx, dsh, dsc, dg = _norm_mod_bwd_math(err * (1.0 / D), x, sc, g)
        return dx, loss, dsh, dsc, dg

    return rowwise(fn, [x, tgt], [sh, sc], [g.reshape(1, -1)], [(D, F32)], [D, D, D, D], B=B, S=S, name=name)


def _gelu(y):
    c0 = math.sqrt(2.0 / math.pi)
    t = jnp.tanh(c0 * (y + 0.044715 * (y * y * y)))
    return 0.5 * y * (1.0 + t), t


def _sigmoid(s):
    return 1.0 / (1.0 + jnp.exp(-s))


def gelu_fwd(y, *, B, S, name):
    return rowwise(lambda y: _gelu(y)[0], [y], [], [], [(y.shape[1], BF16)], [], B=B, S=S, name=name)[0]


def glu_fwd(y, s, *, B, S, name):
    return rowwise(lambda y, s: _gelu(y)[0] * _sigmoid(s), [y, s], [], [], [(y.shape[1], BF16)], [], B=B, S=S,
                   name=name)[0]


def glu_bwd1(y, s, dg, *, B, S, name):
    D = y.shape[1]

    def fn(y, s, dg):
        z = _gelu(y)[0]
        sig = _sigmoid(s)
        ds = dg * z * sig * (1.0 - sig)
        return ds, dg * sig, jnp.sum(ds, axis=0, keepdims=True)

    return rowwise(fn, [y, s, dg], [], [], [(D, BF16), (D, F32)], [D], B=B, S=S, name=name)


def glu_bwd2(y, dz1, dz2, *, B, S, name):
    D = y.shape[1]
    c0 = math.sqrt(2.0 / math.pi)

    def fn(y, dz1, dz2):
        _, t = _gelu(y)
        dgelu = 0.5 * (1.0 + t) + 0.5 * y * (1.0 - t * t) * c0 * (1.0 + 3.0 * 0.044715 * y * y)
        return (dz1 + dz2) * dgelu

    return rowwise(fn, [y, dz1, dz2], [], [], [(D, F32)], [], B=B, S=S, name=name)[0]


def silu_rows(c, *, name):
    R, W = c.shape
    return rowwise(lambda c: c * _sigmoid(c), [c], [], [], [(W, F32)], [], B=1, S=R, name=name)[0]


def _shift_down(cur, h6, h7):
    rows = lax.broadcasted_iota(jnp.int32, cur.shape, 0)
    m1 = jnp.where(rows == 0, h7, pltpu.roll(cur, 1, 0))
    m2 = jnp.where(rows == 0, h6, jnp.where(rows == 1, h7, pltpu.roll(cur, 2, 0)))
    return m1, m2


def _conv3(cur, halo_ref, w_ref, has_prev):
    h6 = jnp.where(has_prev, halo_ref[6:7, :], 0.0)
    h7 = jnp.where(has_prev, halo_ref[7:8, :], 0.0)
    m1, m2 = _shift_down(cur, h6, h7)
    return w_ref[2:3, :] * cur + w_ref[1:2, :] * m1 + w_ref[0:1, :] * m2, m1, m2


def _conv_tiles(S, F):
    ts = _tile(S, (1024, 512, 256, 128, 64, 32, 16, 8))
    tn = _tile(F, (256, 128))
    return ts, tn, S // ts, F // tn


def conv_gate_fwd(up, cw, cb, *, B, S, name):
    F = up.shape[1] // 2
    ts, tn, nts, nF = _conv_tiles(S, F)
    hb = ts // 8

    def body(g_ref, gh_ref, v_ref, vh_ref, wg_ref, wv_ref, bg_ref, bv_ref, o_ref):
        has_prev = pl.program_id(2) > 0
        gc = _conv3(g_ref[...], gh_ref, wg_ref, has_prev)[0] + bg_ref[...]
        vc = _conv3(v_ref[...], vh_ref, wv_ref, has_prev)[0] + bv_ref[...]
        o_ref[...] = (gc * _sigmoid(gc) * vc).astype(o_ref.dtype)

    def cur(off):
        return pl.BlockSpec((ts, tn), lambda b, j, t: (b * nts + t, j + off))

    def halo(off):
        return pl.BlockSpec((8, tn), lambda b, j, t: (jnp.maximum((b * nts + t) * hb - 1, 0), j + off))

    def vec(rows, off):
        return pl.BlockSpec((rows, tn), lambda b, j, t: (0, j + off))

    return pl.pallas_call(
        body, name=name, out_shape=jax.ShapeDtypeStruct((B * S, F), BF16), grid=(B, nF, nts),
        in_specs=[cur(0), halo(0), cur(nF), halo(nF), vec(3, 0), vec(3, nF), vec(1, 0), vec(1, nF)],
        out_specs=pl.BlockSpec((ts, tn), lambda b, j, t: (b * nts + t, j)),
        compiler_params=_params(("parallel", "parallel", "arbitrary")),
    )(up, up, up, up, cw, cw, cb, cb)


def conv_gate_bwd1(up, dact, cw, cb, *, B, S, name):
    F = up.shape[1] // 2
    ts, tn, nts, nF = _conv_tiles(S, F)
    hb = ts // 8

    def body(g_ref, gh_ref, v_ref, vh_ref, da_ref, wg_ref, wv_ref, bg_ref, bv_ref, d_ref, db_ref):
        t = pl.program_id(2)
        has_prev = t > 0
        gc = _conv3(g_ref[...], gh_ref, wg_ref, has_prev)[0] + bg_ref[...]
        vc = _conv3(v_ref[...], vh_ref, wv_ref, has_prev)[0] + bv_ref[...]
        sig = _sigmoid(gc)
        da = da_ref[...]
        dg = da * vc * (sig * (1.0 + gc * (1.0 - sig)))
        dv = da * (gc * sig)
        d_ref[0] = dg
        d_ref[1] = dv
        part = jnp.concatenate([jnp.sum(dg, axis=0, keepdims=True), jnp.sum(dv, axis=0, keepdims=True)], axis=0)

        @pl.when(t == 0)
        def _():
            db_ref[...] = part

        @pl.when(t > 0)
        def _():
            db_ref[...] += part

    def cur(off):
        return pl.BlockSpec((ts, tn), lambda b, j, t: (b * nts + t, j + off))

    def halo(off):
        return pl.BlockSpec((8, tn), lambda b, j, t: (jnp.maximum((b * nts + t) * hb - 1, 0), j + off))

    def vec(rows, off):
        return pl.BlockSpec((rows, tn), lambda b, j, t: (0, j + off))

    return pl.pallas_call(
        body, name=name,
        out_shape=[jax.ShapeDtypeStruct((2, B * S, F), F32), jax.ShapeDtypeStruct((B, 2, F), F32)],
        grid=(B, nF, nts),
        in_specs=[cur(0), halo(0), cur(nF), halo(nF), cur(0), vec(3, 0), vec(3, nF), vec(1, 0), vec(1, nF)],
        out_specs=[pl.BlockSpec((2, ts, tn), lambda b, j, t: (0, b * nts + t, j)),
                   pl.BlockSpec((None, 2, tn), lambda b, j, t: (b, 0, j))],
        compiler_params=_params(("parallel", "parallel", "arbitrary")),
    )(up, up, up, up, dact, cw, cw, cb, cb)


def conv_bwd2(d3, up, cw, *, B, S, name):
    F = up.shape[1] // 2
    ts, tn, nts, nF = _conv_tiles(S, F)
    hb = ts // 8
    last8 = B * S // 8 - 1

    def body(d_ref, da_ref, u_ref, uh_ref, w_ref, o_ref, dw_ref):
        t = pl.program_id(3)
        d = d_ref[...]
        has_next = t < nts - 1
        a0 = jnp.where(has_next, da_ref[0:1, :], 0.0)
        a1 = jnp.where(has_next, da_ref[1:2, :], 0.0)
        rows = lax.broadcasted_iota(jnp.int32, d.shape, 0)
        p1 = jnp.where(rows == ts - 1, a0, pltpu.roll(d, ts - 1, 0))
        p2 = jnp.where(rows == ts - 1, a1, jnp.where(rows == ts - 2, a0, pltpu.roll(d, ts - 2, 0)))
        o_ref[...] = (w_ref[2:3, :] * d + w_ref[1:2, :] * p1 + w_ref[0:1, :] * p2).astype(o_ref.dtype)
        u = u_ref[...]
        has_prev = t > 0
        h6 = jnp.where(has_prev, uh_ref[6:7, :], 0.0)
        h7 = jnp.where(has_prev, uh_ref[7:8, :], 0.0)
        m1, m2 = _shift_down(u, h6, h7)
        part = jnp.concatenate([jnp.sum(d * m2, axis=0, keepdims=True), jnp.sum(d * m1, axis=0, keepdims=True),
                                jnp.sum(d * u, axis=0, keepdims=True)], axis=0)

        @pl.when(t == 0)
        def _():
            dw_ref[...] = part

        @pl.when(t > 0)
        def _():
            dw_ref[...] += part

    return pl.pallas_call(
        body, name=name,
        out_shape=[jax.ShapeDtypeStruct((B * S, 2 * F), BF16), jax.ShapeDtypeStruct((B, 3, 2 * F), F32)],
        grid=(B, 2, nF, nts),
        in_specs=[
            pl.BlockSpec((None, ts, tn), lambda b, g, j, t: (g, b * nts + t, j)),
            pl.BlockSpec((None, 8, tn), lambda b, g, j, t: (g, jnp.minimum((b * nts + t + 1) * hb, last8), j)),
            pl.BlockSpec((ts, tn), lambda b, g, j, t: (b * nts + t, g * nF + j)),
            pl.BlockSpec((8, tn), lambda b, g, j, t: (jnp.maximum((b * nts + t) * hb - 1, 0), g * nF + j)),
            pl.BlockSpec((3, tn), lambda b, g, j, t: (0, g * nF + j)),
        ],
        out_specs=[pl.BlockSpec((ts, tn), lambda b, g, j, t: (b * nts + t, g * nF + j)),
                   pl.BlockSpec((None, 3, tn), lambda b, g, j, t: (b, 0, g * nF + j))],
        compiler_params=_params(("parallel", "parallel", "parallel", "arbitrary")),
    )(d3, d3, up, up, cw)


MASKED_LOG = -1e30


def _split2(x):
    bits = lax.bitcast_convert_type(x, jnp.uint32) & jnp.uint32(0xFFFF0000)
    hi = lax.bitcast_convert_type(bits, F32)
    return hi.astype(BF16), (x - hi).astype(BF16)


def _split_dot(x, m):
    hi, lo = _split2(x)
    return jnp.dot(hi, m, preferred_element_type=F32) + jnp.dot(lo, m, preferred_element_type=F32)


def _nt(a, b):
    return lax.dot_general(a, b, (((1,), (1,)), ((), ())), preferred_element_type=F32)


def _tn(a, b):
    return lax.dot_general(a, b, (((0,), (0,)), ((), ())), preferred_element_type=F32)


def _att_scores(q, k, mask):
    z = _nt(q, k) * (HEAD_DIM ** -0.5)
    e = jnp.exp(-jnp.abs(z))
    sp = jnp.log(1.0 + e)
    lb = jnp.minimum(z, 0.0) - sp
    l1 = jnp.minimum(-z, 0.0) - sp
    if mask is not None:
        lb = jnp.where(mask, lb, MASKED_LOG)
        l1 = jnp.where(mask, l1, 0.0)
    return z, lb, l1, e


def _col_to_row(col, eye):
    return jnp.sum(jnp.where(eye, col, 0.0), axis=0, keepdims=True)


def _row_to_col(row, eye):
    return jnp.sum(jnp.where(eye, row, 0.0), axis=1, keepdims=True)


def attn_fwd(q, k, v, *, name):
    B, H, S, dh = q.shape
    T = ATT_BLOCK
    nq = S // T

    G = _tile(H, (ATT_HEADS, 2))

    def body(q_ref, k_ref, v_ref, o_ref, l_ref):
        r = lax.broadcasted_iota(jnp.int32, (T, T), 0)
        c = lax.broadcasted_iota(jnp.int32, (T, T), 1)
        later = (r > c).astype(BF16)
        eye = r == c
        diag = c < r
        blk = lax.broadcasted_iota(jnp.int32, (nq, T), 0)

        later2 = jnp.concatenate([later, later], axis=0)

        def scores(g, qb, k0, mask):
            _, lb, l1, _ = _att_scores(qb, k_ref[g, pl.ds(k0, T), :], mask)
            return lb, jnp.concatenate(_split2(l1), axis=1), jnp.sum(l1, axis=1, keepdims=True)

        def weigh_all(k0, sc, st):
            suf = jnp.dot(jnp.concatenate([s[1] for s in sc], axis=0), later2, preferred_element_type=F32)
            out = []
            for g in range(G):
                lb, _, rowsum = sc[g]
                acc, run = st[g]
                w = jnp.exp(lb + suf[g * T:(g + 1) * T] + run)
                acc = acc + jnp.dot(w.astype(BF16), v_ref[g, pl.ds(k0, T), :], preferred_element_type=F32)
                out.append((acc, run + rowsum))
            return tuple(out)

        def qblock(i, totals):
            q0 = pl.multiple_of(i * T, T)
            qbs = [q_ref[g, pl.ds(q0, T), :] for g in range(G)]
            sc0 = tuple(scores(g, qbs[g], q0, diag) for g in range(G))
            st0 = tuple((jnp.zeros((T, dh), F32), jnp.zeros((T, 1), F32)) for _ in range(G))

            def kblock(jj, carry):
                sc, st = carry
                k_next = pl.multiple_of((i - jj) * T, T)
                k_cur = pl.multiple_of((i - jj + 1) * T, T)
                st = weigh_all(k_cur, sc, st)
                sc_next = tuple(scores(g, qbs[g], k_next, None) for g in range(G))
                return sc_next, st

            sc, st = lax.fori_loop(1, i + 1, kblock, (sc0, st0))
            st = weigh_all(0, sc, st)
            for g in range(G):
                o_ref[g, pl.ds(q0, T), :] = st[g][0]
            return tuple(jnp.where(blk == i, _col_to_row(st[g][1], eye), totals[g]) for g in range(G))

        totals = lax.fori_loop(0, nq, qblock, tuple(jnp.zeros((nq, T), F32) for _ in range(G)))
        for g in range(G):
            l_ref[g] = totals[g]

    spec = pl.BlockSpec((None, G, S, dh), lambda b, h: (b, h, 0, 0))
    lspec = pl.BlockSpec((None, G, nq, T), lambda b, h: (b, h, 0, 0))
    return pl.pallas_call(
        body, name=name,
        out_shape=[jax.ShapeDtypeStruct((B, H, S, dh), F32), jax.ShapeDtypeStruct((B, H, nq, T), F32)],
        grid=(B, H // G), in_specs=[spec, spec, spec], out_specs=[spec, lspec],
        compiler_params=_params(("parallel", "parallel")),
    )(q, k, v)


def attn_bwd(q, k, v, ltot, do, *, name):
    B, H, S, dh = q.shape
    T = ATT_BLOCK
    nq = S // T
    scale = HEAD_DIM ** -0.5

    G = _tile(H, (ATT_HEADS_BWD, 2))

    def body(q_ref, k_ref, v_ref, l_ref, do_ref, dq_ref, dk_ref, dv_ref, dk_acc, dv_acc):
        r = lax.broadcasted_iota(jnp.int32, (T, T), 0)
        c = lax.broadcasted_iota(jnp.int32, (T, T), 1)
        upto = (r <= c).astype(BF16)
        before = (r < c).astype(BF16)
        upto2 = jnp.concatenate([upto, upto], axis=0)
        before2 = jnp.concatenate([before, before], axis=0)
        eye = r == c
        diag = c < r
        blk = lax.broadcasted_iota(jnp.int32, (nq, T), 0)
        dk_acc[...] = jnp.zeros_like(dk_acc)
        dv_acc[...] = jnp.zeros_like(dv_acc)

        def scores(g, qb, dob, k0, mask):
            z, lb, l1, e = _att_scores(qb, k_ref[g, pl.ds(k0, T), :], mask)
            inv = 1.0 / (1.0 + e)
            small = e * inv
            pos = z >= 0.0
            beta = jnp.where(pos, inv, small)
            omb = jnp.where(pos, small, inv)
            if mask is not None:
                beta = jnp.where(mask, beta, 0.0)
            dw = _nt(dob, v_ref[g, pl.ds(k0, T), :])
            return lb, jnp.concatenate(_split2(l1), axis=1), jnp.sum(l1, axis=1, keepdims=True), dw, beta, omb

        def grads_all(qbs, dobs, tots, k0, sc, st):
            pre = jnp.dot(jnp.concatenate([s[1] for s in sc], axis=0), upto2, preferred_element_type=F32)
            dlws = []
            for g in range(G):
                lb = sc[g][0]
                w = jnp.exp(lb + (tots[g] - (pre[g * T:(g + 1) * T] + st[g][1])))
                dv_acc[g, pl.ds(k0, T), :] += _tn(w.astype(BF16), dobs[g])
                dlws.append(sc[g][3] * w)
            pre_d = jnp.dot(jnp.concatenate([jnp.concatenate(_split2(d), axis=1) for d in dlws], axis=0), before2,
                            preferred_element_type=F32)
            out = []
            for g in range(G):
                _, _, rowsum, _, beta, omb = sc[g]
                dq, run_l, run_d = st[g]
                dl1 = pre_d[g * T:(g + 1) * T] + run_d
                dz = ((dlws[g] * omb - dl1 * beta) * scale).astype(BF16)
                dq = dq + jnp.dot(dz, k_ref[g, pl.ds(k0, T), :], preferred_element_type=F32)
                dk_acc[g, pl.ds(k0, T), :] += _tn(dz, qbs[g])
                out.append((dq, run_l + rowsum, run_d + jnp.sum(dlws[g], axis=1, keepdims=True)))
            return tuple(out)

        def block_inputs(i, q0):
            qbs = [q_ref[g, pl.ds(q0, T), :] for g in range(G)]
            dobs = [do_ref[g, pl.ds(q0, T), :] for g in range(G)]
            tots = [_row_to_col(jnp.sum(jnp.where(blk == i, l_ref[g], 0.0), axis=0, keepdims=True), eye)
                    for g in range(G)]
            z1 = jnp.zeros((T, 1), F32)
            return qbs, dobs, tots, tuple((jnp.zeros((T, dh), F32), z1, z1) for _ in range(G))

        qbs, dobs, tots, st = block_inputs(0, 0)
        st = grads_all(qbs, dobs, tots, 0, tuple(scores(g, qbs[g], dobs[g], 0, diag) for g in range(G)), st)
        for g in range(G):
            dq_ref[g, 0:T, :] = st[g][0].astype(dq_ref.dtype)

        def qblock(i, carry0):
            q0 = pl.multiple_of(i * T, T)
            qbs, dobs, tots, st0 = block_inputs(i, q0)
            sc0 = tuple(scores(g, qbs[g], dobs[g], 0, None) for g in range(G))

            def kblock(j, carry):
                sc, st = carry
                k_cur = pl.multiple_of(j * T, T)
                k_next = pl.multiple_of((j + 1) * T, T)
                sc_next = tuple(scores(g, qbs[g], dobs[g], k_next, None) for g in range(G))
                st = grads_all(qbs, dobs, tots, k_cur, sc, st)
                return sc_next, st

            sc, st = lax.fori_loop(0, i - 1, kblock, (sc0, st0))
            k_last = pl.multiple_of((i - 1) * T, T)
            st = grads_all(qbs, dobs, tots, k_last, sc, st)
            sc_diag = tuple(scores(g, qbs[g], dobs[g], q0, diag) for g in range(G))
            st = grads_all(qbs, dobs, tots, q0, sc_diag, st)
            for g in range(G):
                dq_ref[g, pl.ds(q0, T), :] = st[g][0].astype(dq_ref.dtype)
            return carry0

        lax.fori_loop(1, nq, qblock, 0)
        dk_ref[...] = dk_acc[...].astype(dk_ref.dtype)
        dv_ref[...] = dv_acc[...].astype(dv_ref.dtype)

    spec = pl.BlockSpec((None, G, S, dh), lambda b, h: (b, h, 0, 0))
    lspec = pl.BlockSpec((None, G, nq, T), lambda b, h: (b, h, 0, 0))
    shp = jax.ShapeDtypeStruct((B, H, S, dh), BF16)
    return pl.pallas_call(
        body, name=name, out_shape=[shp, shp, shp], grid=(B, H // G),
        in_specs=[spec, spec, spec, lspec, spec], out_specs=[spec] * 3,
        scratch_shapes=[pltpu.VMEM((G, S, dh), F32), pltpu.VMEM((G, S, dh), F32)],
        compiler_params=_params(("parallel", "parallel")),
    )(q, k, v, ltot, do)


def _wide_consts(T, W):
    r = lax.broadcasted_iota(jnp.int32, (W, W), 0)
    c = lax.broadcasted_iota(jnp.int32, (W, W), 1)
    two = lambda m: jnp.concatenate([m.astype(BF16)] * 2, axis=0)
    qrow = lax.broadcasted_iota(jnp.int32, (T, W), 0)
    kcol = lax.broadcasted_iota(jnp.int32, (T, W), 1)
    er = lax.broadcasted_iota(jnp.int32, (T, T), 0)
    ec = lax.broadcasted_iota(jnp.int32, (T, T), 1)
    return two(r > c), two(r <= c), two(r < c), qrow, kcol, er == ec


def attn_fwd_wide(q, k, v, *, name):
    B, H, S, dh = q.shape
    T = ATT_BLOCK
    W = 2 * T
    nq = S // T
    G = _tile(H, (ATT_HEADS, 2))

    def body(q_ref, k_ref, v_ref, o_ref, l_ref):
        later2, _, _, qrow, kcol, eye = _wide_consts(T, W)
        blk = lax.broadcasted_iota(jnp.int32, (nq, T), 0)

        def step(qbs, k0, st, mask):
            parts, lbs, sums = [], [], []
            for g in range(G):
                _, lb, l1, _ = _att_scores(qbs[g], k_ref[g, pl.ds(k0, W), :], mask)
                parts.append(jnp.concatenate(_split2(l1), axis=1))
                lbs.append(lb)
                sums.append(jnp.sum(l1, axis=1, keepdims=True))
            suf = jnp.dot(jnp.concatenate(parts, axis=0), later2, preferred_element_type=F32)
            out = []
            for g in range(G):
                acc, run = st[g]
                w = jnp.exp(lbs[g] + suf[g * T:(g + 1) * T] + run)
                acc = acc + jnp.dot(w.astype(BF16), v_ref[g, pl.ds(k0, W), :], preferred_element_type=F32)
                out.append((acc, run + sums[g]))
            return tuple(out)

        def qblock(i, totals):
            q0 = pl.multiple_of(i * T, T)
            qbs = [q_ref[g, pl.ds(q0, T), :] for g in range(G)]
            half = jnp.right_shift(i, 1)
            last = half * W
            k_last = pl.multiple_of(last, W)
            mask = (k_last + kcol) < (q0 + qrow)
            st = tuple((jnp.zeros((T, dh), F32), jnp.zeros((T, 1), F32)) for _ in range(G))
            st = step(qbs, k_last, st, mask)

            def kblock(jj, st):
                return step(qbs, pl.multiple_of(last - jj * W, W), st, None)

            st = lax.fori_loop(1, half + 1, kblock, st)
            for g in range(G):
                o_ref[g, pl.ds(q0, T), :] = st[g][0]
            return tuple(jnp.where(blk == i, _col_to_row(st[g][1], eye), totals[g]) for g in range(G))

        totals = lax.fori_loop(0, nq, qblock, tuple(jnp.zeros((nq, T), F32) for _ in range(G)))
        for g in range(G):
            l_ref[g] = totals[g]

    spec = pl.BlockSpec((None, G, S, dh), lambda b, h: (b, h, 0, 0))
    lspec = pl.BlockSpec((None, G, nq, T), lambda b, h: (b, h, 0, 0))
    return pl.pallas_call(
        body, name=name,
        out_shape=[jax.ShapeDtypeStruct((B, H, S, dh), F32), jax.ShapeDtypeStruct((B, H, nq, T), F32)],
        grid=(B, H // G), in_specs=[spec, spec, spec], out_specs=[spec, lspec],
        compiler_params=_params(("parallel", "parallel")),
    )(q, k, v)


def attn_bwd_wide(q, k, v, ltot, do, *, name):
    B, H, S, dh = q.shape
    T = ATT_BLOCK
    W = 2 * T
    nq = S // T
    scale = HEAD_DIM ** -0.5
    G = _tile(H, (ATT_HEADS_BWD, 2))

    def body(q_ref, k_ref, v_ref, l_ref, do_ref, dq_ref, dk_ref, dv_ref, dk_acc, dv_acc):
        _, upto2, before2, qrow, kcol, eye = _wide_consts(T, W)
        blk = lax.broadcasted_iota(jnp.int32, (nq, T), 0)
        dk_acc[...] = jnp.zeros_like(dk_acc)
        dv_acc[...] = jnp.zeros_like(dv_acc)

        def step(qbs, dobs, tots, k0, st, mask):
            sc = []
            for g in range(G):
                z, lb, l1, e = _att_scores(qbs[g], k_ref[g, pl.ds(k0, W), :], mask)
                inv = 1.0 / (1.0 + e)
                small = e * inv
                pos = z >= 0.0
                beta = jnp.where(pos, inv, small)
                omb = jnp.where(pos, small, inv)
                if mask is not None:
                    beta = jnp.where(mask, beta, 0.0)
                dw = _nt(dobs[g], v_ref[g, pl.ds(k0, W), :])
                sc.append((lb, jnp.concatenate(_split2(l1), axis=1), jnp.sum(l1, axis=1, keepdims=True), dw, beta, omb))
            pre = jnp.dot(jnp.concatenate([s[1] for s in sc], axis=0), upto2, preferred_element_type=F32)
            dlws = []
            for g in range(G):
                w = jnp.exp(sc[g][0] + (tots[g] - (pre[g * T:(g + 1) * T] + st[g][1])))
                dv_acc[g, pl.ds(k0, W), :] += _tn(w.astype(BF16), dobs[g])
                dlws.append(sc[g][3] * w)
            pre_d = jnp.dot(jnp.concatenate([jnp.concatenate(_split2(d), axis=1) for d in dlws], axis=0), before2,
                            preferred_element_type=F32)
            out = []
            for g in range(G):
                _, _, rowsum, _, beta, omb = sc[g]
                dq, run_l, run_d = st[g]
                dl1 = pre_d[g * T:(g + 1) * T] + run_d
                dz = ((dlws[g] * omb - dl1 * beta) * scale).astype(BF16)
                dq = dq + jnp.dot(dz, k_ref[g, pl.ds(k0, W), :], preferred_element_type=F32)
                dk_acc[g, pl.ds(k0, W), :] += _tn(dz, qbs[g])
                out.append((dq, run_l + rowsum, run_d + jnp.sum(dlws[g], axis=1, keepdims=True)))
            return tuple(out)

        def qblock(i, carry0):
            q0 = pl.multiple_of(i * T, T)
            qbs = [q_ref[g, pl.ds(q0, T), :] for g in range(G)]
            dobs = [do_ref[g, pl.ds(q0, T), :] for g in range(G)]
            tots = [_row_to_col(jnp.sum(jnp.where(blk == i, l_ref[g], 0.0), axis=0, keepdims=True), eye)
                    for g in range(G)]
            z1 = jnp.zeros((T, 1), F32)
            st = tuple((jnp.zeros((T, dh), F32), z1, z1) for _ in range(G))

            def kblock(j, st):
                return step(qbs, dobs, tots, pl.multiple_of(j * W, W), st, None)

            half = jnp.right_shift(i, 1)
            st = lax.fori_loop(0, half, kblock, st)
            k_last = pl.multiple_of(half * W, W)
            st = step(qbs, dobs, tots, k_last, st, (k_last + kcol) < (q0 + qrow))
            for g in range(G):
                dq_ref[g, pl.ds(q0, T), :] = st[g][0].astype(dq_ref.dtype)
            return carry0

        lax.fori_loop(0, nq, qblock, 0)
        dk_ref[...] = dk_acc[...].astype(dk_ref.dtype)
        dv_ref[...] = dv_acc[...].astype(dv_ref.dtype)

    spec = pl.BlockSpec((None, G, S, dh), lambda b, h: (b, h, 0, 0))
    lspec = pl.BlockSpec((None, G, nq, T), lambda b, h: (b, h, 0, 0))
    shp = jax.ShapeDtypeStruct((B, H, S, dh), BF16)
    return pl.pallas_call(
        body, name=name, out_shape=[shp, shp, shp], grid=(B, H // G),
        in_specs=[spec, spec, spec, lspec, spec], out_specs=[spec] * 3,
        scratch_shapes=[pltpu.VMEM((G, S, dh), F32), pltpu.VMEM((G, S, dh), F32)],
        compiler_params=_params(("parallel", "parallel")),
    )(q, k, v, ltot, do)


def _cmul(ar, ai, br, bi):
    return ar * br - ai * bi, ar * bi + ai * br


def _cpow(lr, li, n):
    rr, ri = None, None
    br, bi = lr, li
    while n:
        if n & 1:
            rr, ri = (br, bi) if rr is None else _cmul(rr, ri, br, bi)
        n >>= 1
        if n:
            br, bi = _cmul(br, bi, br, bi)
    return rr, ri


def _ssm_scan(sr, si, lr, li, n_steps, reverse):
    W = sr.shape[1]
    R = SEGMENTS
    lim = -li if reverse else li
    zero = jnp.zeros((R, W), F32)

    def row(k):
        i = (n_steps - 1 - k) if reverse else k
        return pl.multiple_of(i * R, R)

    def local(k, st):
        cr, ci = st
        r0 = row(k)
        pr, pi = _cmul(lr, lim, cr, ci)
        nr = pr + sr[pl.ds(r0, R), :]
        ni = pi + si[pl.ds(r0, R), :]
        sr[pl.ds(r0, R), :] = nr
        si[pl.ds(r0, R), :] = ni
        return nr, ni

    er, ei = lax.fori_loop(0, n_steps, local, (zero, zero), unroll=SCAN_UNROLL)
    lnr, lni = _cpow(lr, lim, n_steps)
    rows = lax.broadcasted_iota(jnp.int32, (R, W), 0)
    cr, ci = zero, zero
    for step in range(1, R):
        tr, ti = _cmul(lnr, lni, cr, ci)
        tr, ti = tr + er, ti + ei
        if reverse:
            seg = R - 1 - step
            tr, ti = pltpu.roll(tr, R - 1, 0), pltpu.roll(ti, R - 1, 0)
        else:
            seg = step
            tr, ti = pltpu.roll(tr, 1, 0), pltpu.roll(ti, 1, 0)
        cr = jnp.where(rows == seg, tr, cr)
        ci = jnp.where(rows == seg, ti, ci)

    def fix(k, st):
        pr, pi = st
        r0 = row(k)
        ar, ai = _cmul(pr, pi, cr, ci)
        sr[pl.ds(r0, R), :] += ar
        si[pl.ds(r0, R), :] += ai
        return _cmul(lr, lim, pr, pi)

    lax.fori_loop(0, n_steps, fix, (lr, lim), unroll=SCAN_UNROLL)
    return cr, ci


def _ssm_specs(S, W):
    CH = GROUPS_PER_BLOCK * SSM_GROUP
    return dict(
        rows=pl.BlockSpec((S, CH), lambda b, j: (b, j)),
        b=pl.BlockSpec((None, CH, W), lambda b, j: (j, 0, 0)),
        c=pl.BlockSpec((None, W, CH), lambda b, j: (j, 0, 0)),
        lam=pl.BlockSpec((None, SEGMENTS, W), lambda b, j: (j, 0, 0)),
        vec=pl.BlockSpec((1, CH), lambda b, j: (0, j)),
    )


def ssm_fwd(u, bre, bim, cre, cim, lr8, li8, dsk, *, B, S, name):
    D = u.shape[1]
    J, CH, W = bre.shape
    n_steps = S // SEGMENTS
    sp = _ssm_specs(S, W)

    def body(u_ref, bre_ref, bim_ref, cre_ref, cim_ref, lr_ref, li_ref, dsk_ref, y_ref, sr, si):
        u = u_ref[...]
        ub = u.astype(BF16)
        sr[...] = jnp.dot(ub, bre_ref[...], preferred_element_type=F32)
        si[...] = jnp.dot(ub, bim_ref[...], preferred_element_type=F32)
        _ssm_scan(sr, si, lr_ref[...], li_ref[...], n_steps, False)
        y = jnp.dot(sr[...].astype(BF16), cre_ref[...], preferred_element_type=F32)
        y = y - jnp.dot(si[...].astype(BF16), cim_ref[...], preferred_element_type=F32)
        y_ref[...] = y + dsk_ref[...] * u

    return pl.pallas_call(
        body, name=name, out_shape=jax.ShapeDtypeStruct((B * S, D), F32), grid=(B, J),
        in_specs=[sp["rows"], sp["b"], sp["b"], sp["c"], sp["c"], sp["lam"], sp["lam"], sp["vec"]],
        out_specs=sp["rows"],
        scratch_shapes=[pltpu.VMEM((S, W), F32), pltpu.VMEM((S, W), F32)],
        compiler_params=_params(("parallel", "parallel")),
    )(u, bre, bim, cre, cim, lr8, li8, dsk)


def ssm_bwd(u, dy, bre, bim, cre, cim, lr8, li8, dsk, *, B, S, name):
    D = u.shape[1]
    J, CH, W = bre.shape
    n_steps = S // SEGMENTS
    sp = _ssm_specs(S, W)

    def body(u_ref, dy_ref, bre_ref, bim_ref, cre_ref, cim_ref, lr_ref, li_ref, dsk_ref,
             du_ref, dbre_ref, dbim_ref, dcre_ref, dcim_ref, dlr_ref, dli_ref, ddsk_ref, sr, si, ar, ai):
        u = u_ref[...]
        dy = dy_ref[...]
        ub = u.astype(BF16)
        dyb = dy.astype(BF16)
        lr, li = lr_ref[...], li_ref[...]
        sr[...] = jnp.dot(ub, bre_ref[...], preferred_element_type=F32)
        si[...] = jnp.dot(ub, bim_ref[...], preferred_element_type=F32)
        cr, ci = _ssm_scan(sr, si, lr, li, n_steps, False)
        ar[...] = _nt(dyb, cre_ref[...])
        ai[...] = -_nt(dyb, cim_ref[...])
        _ssm_scan(ar, ai, lr, li, n_steps, True)

        def dlam(k, st):
            dr, di = st
            r0 = pl.multiple_of((k + 1) * SEGMENTS, SEGMENTS)
            p0 = pl.multiple_of(k * SEGMENTS, SEGMENTS)
            pr, pi = sr[pl.ds(p0, SEGMENTS), :], si[pl.ds(p0, SEGMENTS), :]
            xr, xi = ar[pl.ds(r0, SEGMENTS), :], ai[pl.ds(r0, SEGMENTS), :]
            return dr + pr * xr + pi * xi, di + pr * xi - pi * xr

        xr, xi = ar[0:SEGMENTS, :], ai[0:SEGMENTS, :]
        dr, di = lax.fori_loop(0, n_steps - 1, dlam, (cr * xr + ci * xi, cr * xi - ci * xr), unroll=SCAN_UNROLL)
        dlr_ref[...] = dr
        dli_ref[...] = di
        arb = ar[...].astype(BF16)
        aib = ai[...].astype(BF16)
        du_ref[...] = _nt(arb, bre_ref[...]) + _nt(aib, bim_ref[...]) + dsk_ref[...] * dy
        dbre_ref[...] = _tn(ub, arb)
        dbim_ref[...] = _tn(ub, aib)
        dcre_ref[...] = _tn(sr[...].astype(BF16), dyb)
        dcim_ref[...] = -_tn(si[...].astype(BF16), dyb)
        ddsk_ref[...] = jnp.sum(dy * u, axis=0, keepdims=True)

    def per(shape):
        return pl.BlockSpec((None, None) + shape, lambda b, j: (b, j, 0, 0))

    return pl.pallas_call(
        body, name=name,
        out_shape=[jax.ShapeDtypeStruct((B * S, D), F32),
                   jax.ShapeDtypeStruct((B, J, CH, W), F32), jax.ShapeDtypeStruct((B, J, CH, W), F32),
                   jax.ShapeDtypeStruct((B, J, W, CH), F32), jax.ShapeDtypeStruct((B, J, W, CH), F32),
                   jax.ShapeDtypeStruct((B, J, SEGMENTS, W), F32), jax.ShapeDtypeStruct((B, J, SEGMENTS, W), F32),
                   jax.ShapeDtypeStruct((B, J, 1, CH), F32)],
        grid=(B, J),
        in_specs=[sp["rows"], sp["rows"], sp["b"], sp["b"], sp["c"], sp["c"], sp["lam"], sp["lam"], sp["vec"]],
        out_specs=[sp["rows"], per((CH, W)), per((CH, W)), per((W, CH)), per((W, CH)), per((SEGMENTS, W)),
                   per((SEGMENTS, W)),
                   per((1, CH))],
        scratch_shapes=[pltpu.VMEM((S, W), F32)] * 4,
        compiler_params=_params(("parallel", "parallel")),
    )(u, dy, bre, bim, cre, cim, lr8, li8, dsk)


def _ssm_discretize(a_re, a_im, log_dt, b_re, b_im):
    dt = jnp.exp(log_dt)[:, None]
    er = jnp.exp(a_re * dt)
    lr = er * jnp.cos(a_im * dt)
    li = er * jnp.sin(a_im * dt)
    den = a_re * a_re + a_im * a_im
    fr = ((lr - 1.0) * a_re + li * a_im) / den
    fi = (li * a_re - (lr - 1.0) * a_im) / den
    bbr = fr[..., None] * b_re - fi[..., None] * b_im
    bbi = fr[..., None] * b_im + fi[..., None] * b_re
    return lr, li, bbr, bbi


def _block_diag_in(m):
    G, P, H = m.shape
    J = G // GROUPS_PER_BLOCK
    m = m.reshape(J, GROUPS_PER_BLOCK, P, H).transpose(0, 1, 3, 2)
    eye = jnp.eye(GROUPS_PER_BLOCK, dtype=m.dtype)
    out = m[:, :, :, None, :] * eye[None, :, None, :, None]
    return out.reshape(J, GROUPS_PER_BLOCK * H, GROUPS_PER_BLOCK * P)


def _block_diag_in_grad(d, G, P, H):
    J = G // GROUPS_PER_BLOCK
    d = d.reshape(J, GROUPS_PER_BLOCK, H, GROUPS_PER_BLOCK, P)
    idx = jnp.arange(GROUPS_PER_BLOCK)
    d = d[:, idx, :, idx, :]
    return d.transpose(1, 0, 3, 2).reshape(G, P, H)


def _block_diag_out(m):
    G, H, P = m.shape
    J = G // GROUPS_PER_BLOCK
    m = m.reshape(J, GROUPS_PER_BLOCK, H, P).transpose(0, 1, 3, 2)
    eye = jnp.eye(GROUPS_PER_BLOCK, dtype=m.dtype)
    out = m[:, :, :, None, :] * eye[None, :, None, :, None]
    return out.reshape(J, GROUPS_PER_BLOCK * P, GROUPS_PER_BLOCK * H)


def _block_diag_out_grad(d, G, H, P):
    J = G // GROUPS_PER_BLOCK
    d = d.reshape(J, GROUPS_PER_BLOCK, P, GROUPS_PER_BLOCK, H)
    idx = jnp.arange(GROUPS_PER_BLOCK)
    d = d[:, idx, :, idx, :]
    return d.transpose(1, 0, 3, 2).reshape(G, H, P)


def _interleave(a, B, S):
    L = S // SEGMENTS
    return a.reshape(B, SEGMENTS, L, a.shape[-1]).transpose(0, 2, 1, 3).reshape(B * S, a.shape[-1])


def _deinterleave(a, B, S):
    L = S // SEGMENTS
    return a.reshape(B, L, SEGMENTS, a.shape[-1]).transpose(0, 2, 1, 3).reshape(B * S, a.shape[-1])


def _adamw_math(w, g, m, v):
    m = ADAM_B1 * m + (1.0 - ADAM_B1) * g
    v = ADAM_B2 * v + (1.0 - ADAM_B2) * (g * g)
    m_hat = m / (1.0 - ADAM_B1 ** ADAM_STEP)
    v_hat = v / (1.0 - ADAM_B2 ** ADAM_STEP)
    delta = -ADAM_LR * (m_hat / (jnp.sqrt(v_hat) + ADAM_EPS) + ADAM_WD * w)
    return delta, m, v


def adamw(w, g, m, v, *, name):
    R, C = w.shape
    tr = _tile(R, (max(8, (1 << 18) // C // 8 * 8), 256, 128, 64, 32, 16, 8))

    def body(w_ref, g_ref, m_ref, v_ref, d_ref, nm_ref, nv_ref):
        d, nm, nv = _adamw_math(w_ref[...], g_ref[...], m_ref[...], v_ref[...])
        d_ref[...] = d
        nm_ref[...] = nm
        nv_ref[...] = nv

    spec = pl.BlockSpec((tr, C), lambda i: (i, 0))
    shp = jax.ShapeDtypeStruct((R, C), F32)
    return pl.pallas_call(
        body, name=name, out_shape=[shp, shp, shp], grid=(R // tr,), in_specs=[spec] * 4, out_specs=[spec] * 3,
        compiler_params=_params(("parallel",)),
    )(w, g, m, v)


def sum_leading(a, *, name, out_dtype=F32):
    n, R, C = a.shape
    tr = _tile(R, (256, 128, 64, 32, 16, 8))

    def body(a_ref, o_ref):
        acc = a_ref[0].astype(F32)
        for i in range(1, n):
            acc = acc + a_ref[i].astype(F32)
        o_ref[...] = acc.astype(o_ref.dtype)

    return pl.pallas_call(
        body, name=name, out_shape=jax.ShapeDtypeStruct((R, C), out_dtype), grid=(R // tr,),
        in_specs=[pl.BlockSpec((n, tr, C), lambda i: (0, i, 0))], out_specs=pl.BlockSpec((tr, C), lambda i: (i, 0)),
        compiler_params=_params(("parallel",)),
    )(a)


def _any_specs(n):
    return [pl.BlockSpec(memory_space=pl.ANY) for _ in range(n)]


def _coords():
    return lax.axis_index("x"), lax.axis_index("y"), lax.axis_index("c")


def _flip(v, bit):
    return (v + bit) % 2


def all_gather8(a, *, name):
    shape = a.shape

    def body(a_ref, o_ref, send_sems, recv_sems, local_sem):
        x, y, c = _coords()
        me = 4 * x + 2 * y + c
        mine = pltpu.make_async_copy(a_ref, o_ref.at[me], local_sem)
        mine.start()
        sends = []
        for k in range(1, N_DEV):
            peer = (_flip(x, (k >> 2) & 1), _flip(y, (k >> 1) & 1), _flip(c, k & 1))
            cp = pltpu.make_async_remote_copy(a_ref, o_ref.at[me], send_sems.at[k - 1], recv_sems.at[k - 1],
                                              device_id=peer, device_id_type=MESH)
            cp.start()
            sends.append(cp)
        for k in range(1, N_DEV):
            px, py, pc = _flip(x, (k >> 2) & 1), _flip(y, (k >> 1) & 1), _flip(c, k & 1)
            src = 4 * px + 2 * py + pc
            pltpu.make_async_remote_copy(a_ref, o_ref.at[src], send_sems.at[k - 1], recv_sems.at[k - 1],
                                         device_id=(px, py, pc), device_id_type=MESH).wait_recv()
        for cp in sends:
            cp.wait_send()
        mine.wait()

    return pl.pallas_call(
        body, name=name, out_shape=jax.ShapeDtypeStruct((N_DEV,) + shape, a.dtype),
        in_specs=_any_specs(1), out_specs=pl.BlockSpec(memory_space=pl.ANY),
        scratch_shapes=[pltpu.SemaphoreType.DMA((N_DEV - 1,)), pltpu.SemaphoreType.DMA((N_DEV - 1,)),
                        pltpu.SemaphoreType.DMA(())],
    )(a)


def _chip_of(x, y, p):
    px, py = _flip(x, (p >> 1) & 1), _flip(y, p & 1)
    return 2 * px + py, px, py


def gather_chip_shards(arrs, *, name):
    n = len(arrs)

    def body(*refs):
        ins, outs = refs[:n], refs[n:2 * n]
        ici_send, ici_recv, d2d_send, d2d_recv = refs[2 * n:]
        x, y, c = _coords()
        me = 2 * x + y
        sends = []
        for i in range(n):
            half = ins[i].shape[0] // 2
            rows = pl.ds(c * half, half)
            for p in range(1, N_CHIPS):
                _, px, py = _chip_of(x, y, p)
                s = i * 3 + p - 1
                cp = pltpu.make_async_remote_copy(ins[i].at[rows], outs[i].at[me, rows], ici_send.at[s], ici_recv.at[s],
                                                  device_id=(px, py, c), device_id_type=MESH)
                cp.start()
                sends.append(cp)
        for i in range(n):
            half = ins[i].shape[0] // 2
            rows = pl.ds(c * half, half)
            for p in range(1, N_CHIPS):
                src, px, py = _chip_of(x, y, p)
                s = i * 3 + p - 1
                pltpu.make_async_remote_copy(ins[i].at[rows], outs[i].at[src, rows], ici_send.at[s], ici_recv.at[s],
                                             device_id=(px, py, c), device_id_type=MESH).wait_recv()
                cp = pltpu.make_async_remote_copy(outs[i].at[src, rows], outs[i].at[src, rows], d2d_send.at[s],
                                                  d2d_recv.at[s], device_id=(x, y, 1 - c), device_id_type=MESH)
                cp.start()
                sends.append(cp)
        for i in range(n):
            half = ins[i].shape[0] // 2
            theirs = pl.ds((1 - c) * half, half)
            for p in range(1, N_CHIPS):
                src, _, _ = _chip_of(x, y, p)
                s = i * 3 + p - 1
                pltpu.make_async_remote_copy(outs[i].at[src, theirs], outs[i].at[src, theirs], d2d_send.at[s],
                                             d2d_recv.at[s], device_id=(x, y, 1 - c), device_id_type=MESH).wait_recv()
        for cp in sends:
            cp.wait_send()

    dma = pltpu.SemaphoreType.DMA
    got = pl.pallas_call(
        body, name=name,
        out_shape=[jax.ShapeDtypeStruct((N_CHIPS,) + a.shape, a.dtype) for a in arrs],
        in_specs=_any_specs(n), out_specs=_any_specs(n),
        scratch_shapes=[dma((3 * n,)), dma((3 * n,)), dma((3 * n,)), dma((3 * n,))],
    )(*arrs)
    chip = 2 * lax.axis_index("x") + lax.axis_index("y")
    own = lax.broadcasted_iota(jnp.int32, (N_CHIPS, 1, 1), 0) == chip
    return [jnp.where(own, a[None], g) for g, a in zip(got, arrs)]


def swap_halves(arrs, *, name):
    n = len(arrs)

    def body(*refs):
        ins, outs = refs[:n], refs[n:2 * n]
        send_sems, recv_sems = refs[2 * n:]
        x, y, c = _coords()
        cps = []
        for i in range(n):
            half = ins[i].shape[1] // 2
            cp = pltpu.make_async_remote_copy(ins[i].at[:, pl.ds((1 - c) * half, half)], outs[i], send_sems.at[i],
                                              recv_sems.at[i], device_id=(x, y, 1 - c), device_id_type=MESH)
            cp.start()
            cps.append(cp)
        for cp in cps:
            cp.wait()

    dma = pltpu.SemaphoreType.DMA
    return pl.pallas_call(
        body, name=name,
        out_shape=[jax.ShapeDtypeStruct((N_CHIPS, a.shape[1] // 2, a.shape[2]), a.dtype) for a in arrs],
        in_specs=_any_specs(n), out_specs=_any_specs(n), scratch_shapes=[dma((n,)), dma((n,))],
    )(*arrs)


def add_half(g, other, c_idx, *, name, out_dtype):
    _, R, C = g.shape
    half = R // 2
    tr = _tile(half, (256, 128, 64, 32, 16, 8))
    nt = half // tr

    def body(c_ref, g_ref, o_ref, out_ref):
        out_ref[...] = (g_ref[...] + o_ref[...]).astype(out_ref.dtype)

    return pl.pallas_call(
        body, name=name, out_shape=jax.ShapeDtypeStruct((N_CHIPS, half, C), out_dtype),
        grid_spec=pltpu.PrefetchScalarGridSpec(
            num_scalar_prefetch=1, grid=(N_CHIPS, nt),
            in_specs=[pl.BlockSpec((None, tr, C), lambda r, t, c_ref: (r, c_ref[0] * nt + t, 0)),
                      pl.BlockSpec((None, tr, C), lambda r, t, c_ref: (r, t, 0))],
            out_specs=pl.BlockSpec((None, tr, C), lambda r, t, c_ref: (r, t, 0))),
        compiler_params=_params(("parallel", "parallel")),
    )(c_idx, g, other)


def scatter_to_chips(arrs, *, name):
    n = len(arrs)

    def body(*refs):
        ins, outs = refs[:n], refs[n:2 * n]
        send_sems, recv_sems = refs[2 * n:]
        x, y, c = _coords()
        cps = []
        for i in range(n):
            for p in range(1, N_CHIPS):
                dst, px, py = _chip_of(x, y, p)
                s = i * 3 + p - 1
                cp = pltpu.make_async_remote_copy(ins[i].at[dst], outs[i].at[p - 1], send_sems.at[s], recv_sems.at[s],
                                                  device_id=(px, py, c), device_id_type=MESH)
                cp.start()
                cps.append(cp)
        for cp in cps:
            cp.wait()

    dma = pltpu.SemaphoreType.DMA
    return pl.pallas_call(
        body, name=name,
        out_shape=[jax.ShapeDtypeStruct((N_CHIPS - 1,) + a.shape[1:], a.dtype) for a in arrs],
        in_specs=_any_specs(n), out_specs=_any_specs(n), scratch_shapes=[dma((3 * n,)), dma((3 * n,))],
    )(*arrs)


def add_chips(h, got, r_idx, *, name):
    _, R, C = h.shape
    tr = _tile(R, (256, 128, 64, 32, 16, 8))

    def body(r_ref, h_ref, g_ref, out_ref):
        acc = h_ref[...].astype(F32)
        for p in range(N_CHIPS - 1):
            acc = acc + g_ref[p].astype(F32)
        out_ref[...] = acc

    return pl.pallas_call(
        body, name=name, out_shape=jax.ShapeDtypeStruct((R, C), F32),
        grid_spec=pltpu.PrefetchScalarGridSpec(
            num_scalar_prefetch=1, grid=(R // tr,),
            in_specs=[pl.BlockSpec((None, tr, C), lambda t, r_ref: (r_ref[0], t, 0)),
                      pl.BlockSpec((N_CHIPS - 1, tr, C), lambda t, r_ref: (0, t, 0))],
            out_specs=pl.BlockSpec((tr, C), lambda t, r_ref: (t, 0))),
        compiler_params=_params(("parallel",)),
    )(r_idx, h, got)


def swap_with_sibling(arrs, *, name):
    n = len(arrs)

    def body(*refs):
        ins, outs = refs[:n], refs[n:2 * n]
        send_sems, recv_sems = refs[2 * n:]
        x, y, c = _coords()
        cps = []
        for i in range(n):
            cp = pltpu.make_async_remote_copy(ins[i], outs[i], send_sems.at[i], recv_sems.at[i],
                                              device_id=(x, y, 1 - c), device_id_type=MESH)
            cp.start()
            cps.append(cp)
        for cp in cps:
            cp.wait()

    dma = pltpu.SemaphoreType.DMA
    return pl.pallas_call(
        body, name=name, out_shape=[jax.ShapeDtypeStruct(a.shape, a.dtype) for a in arrs],
        in_specs=_any_specs(n), out_specs=_any_specs(n), scratch_shapes=[dma((n,)), dma((n,))],
    )(*arrs)


def join_halves(arrs, *, name):
    theirs = swap_with_sibling(arrs, name=name)
    first = lax.axis_index("c") == 0
    return [jnp.concatenate([jnp.where(first, a, b), jnp.where(first, b, a)], axis=0) for a, b in zip(arrs, theirs)]


def reduce_scatter_chips(grads):
    x, y, c = _coords()
    c_idx = jnp.reshape(c, (1,)).astype(jnp.int32)
    r_idx = jnp.reshape(2 * x + y, (1,)).astype(jnp.int32)
    theirs = swap_halves(grads, name="rs_swap_halves")
    pair = [add_half(g, o, c_idx, name=f"rs_add_half_{i}", out_dtype=BF16) for i, (g, o) in enumerate(zip(grads, theirs))]
    got = scatter_to_chips(pair, name="rs_scatter_to_chips")
    mine = [add_chips(h, g, r_idx, name=f"rs_add_chips_{i}") for i, (h, g) in enumerate(zip(pair, got))]
    return join_halves(mine, name="rs_join_halves")


def _to_heads(t, B, S):
    return t.reshape(B, S, -1, HEAD_DIM).transpose(0, 2, 1, 3)


def _from_heads(t, B, S):
    return t.transpose(0, 2, 1, 3).reshape(B * S, -1)


def _chip_major(w, axis):
    n = w.shape[axis] // N_CHIPS
    parts = w.reshape(w.shape[:axis] + (N_CHIPS, n) + w.shape[axis + 1:])
    return jnp.moveaxis(parts, axis, 0)


def _from_chip_major(g, axis):
    g = jnp.moveaxis(g, 0, axis)
    return g.reshape(g.shape[:axis] + (g.shape[axis] * g.shape[axis + 1],) + g.shape[axis + 2:])


def kernel(x, c, norm_mix, norm_ffn, w_mod, b_mod, w_qkv, w_o_attn, w_in_ssm, a_re, a_im, log_dt, b_re, b_im, c_re, c_im, d_skip, w_glu, b_glu, w_o_ssm, w_up, conv_w, conv_b, w_down, norm_out, w_fin, b_fin, loss_target, m_norm_mix, m_norm_ffn, m_w_mod, m_b_mod, m_w_qkv, m_w_o_attn, m_w_in_ssm, m_a_re, m_a_im, m_log_dt, m_b_re, m_b_im, m_c_re, m_c_im, m_d_skip, m_w_glu, m_b_glu, m_w_o_ssm, m_w_up, m_conv_w, m_conv_b, m_w_down, m_norm_out, m_w_fin, m_b_fin, v_norm_mix, v_norm_ffn, v_w_mod, v_b_mod, v_w_qkv, v_w_o_attn, v_w_in_ssm, v_a_re, v_a_im, v_log_dt, v_b_re, v_b_im, v_c_re, v_c_im, v_d_skip, v_w_glu, v_b_glu, v_w_o_ssm, v_w_up, v_conv_w, v_conv_b, v_w_down, v_norm_out, v_w_fin, v_b_fin):
    B, S, D = x.shape
    T = B * S
    F2 = conv_b.shape[1]
    F = F2 // 2
    G, P = a_re.shape[1], a_re.shape[2]
    H = b_re.shape[3]
    mx, my, mc = _coords()
    chip = 2 * mx + my
    dev = 4 * mx + 2 * my + mc
    BG = N_DEV * B
    mod_w = w_mod.shape[2]
    fin_w = w_fin.shape[1]

    c_all = all_gather8(c, name="gather_c").reshape(BG, D)
    c_act = silu_rows(c_all, name="silu_c")
    b_mod_mine = lax.dynamic_slice(b_mod, (0, chip * mod_w), (2, mod_w))
    b_fin_mine = lax.dynamic_slice(b_fin, (chip * fin_w,), (fin_w,))
    cond = [matmul(c_act, w_mod[i], bias=b_mod_mine[i], name=f"mod_proj_{i}") for i in range(2)]
    cond.append(matmul(c_act, w_fin, bias=b_fin_mine, name="fin_proj"))
    cond_all = all_gather8(jnp.concatenate(cond, axis=1), name="gather_cond")
    cond_all = cond_all[::2]
    cond_rows = lax.dynamic_slice(cond_all, (0, dev * B, 0), (N_CHIPS, B, cond_all.shape[2]))
    mods = []
    for i in range(2):
        full = cond_rows[:, :, i * mod_w:(i + 1) * mod_w].transpose(1, 0, 2).reshape(B, N_CHIPS * mod_w)
        mods.append([full[:, k * D:(k + 1) * D] for k in range(6)])
    fin = cond_rows[:, :, 2 * mod_w:].transpose(1, 0, 2).reshape(B, N_CHIPS * fin_w)
    sh_f, sc_f = fin[:, :D], fin[:, D:]

    rows1024 = jnp.concatenate([w_o_attn[0], w_in_ssm[0], w_glu[0], w_o_ssm[0], w_down.reshape(-1, D)], axis=0)
    g_qkv, g_rows, g_up = gather_chip_shards(
        [w_qkv[0].astype(BF16), rows1024.astype(BF16), w_up.reshape(2 * D, -1).astype(BF16)], name="gather_weights")
    Dq = D // N_CHIPS
    Fq = F // N_CHIPS
    W_qkv = _from_chip_major(g_qkv, 1)
    W_o_attn = g_rows[:, 0 * Dq:1 * Dq].reshape(D, D)
    W_in = g_rows[:, 1 * Dq:2 * Dq].reshape(D, D)
    W_glu = g_rows[:, 2 * Dq:3 * Dq].reshape(D, D)
    W_o_ssm = g_rows[:, 3 * Dq:4 * Dq].reshape(D, D)
    W_down = [g_rows[:, 4 * Dq + i * Fq:4 * Dq + (i + 1) * Fq].reshape(F, D) for i in range(2)]
    W_up = [_from_chip_major(g_up[:, i * D:(i + 1) * D], 1) for i in range(2)]
    small = jnp.concatenate([conv_w.reshape(6, -1), jnp.pad(d_skip, ((0, 0), (0, conv_w.shape[2] - Dq))),
                             jnp.pad(b_glu, ((0, 0), (0, conv_w.shape[2] - Dq)))], axis=0)
    small_all = all_gather8(small, name="gather_small")[::2]
    conv_w_full = _from_chip_major(small_all[:, :6].reshape(N_CHIPS, 2, 3, -1), 2)
    d_skip_full = small_all[:, 6, :Dq].reshape(1, D)
    b_glu_full = small_all[:, 7, :Dq].reshape(D)

    x0 = x.reshape(T, D)
    tgt = loss_target.reshape(T, D)

    def ffn_fwd(xin, i):
        sh2, sc2, g2 = mods[i][3], mods[i][4], mods[i][5]
        h2 = norm_mod_fwd(xin, norm_ffn[i], sh2, sc2, B=B, S=S, name=f"ffn_norm_{i}")
        up = matmul(h2, W_up[i], name=f"ffn_up_{i}")
        act = conv_gate_fwd(up, conv_w_full[i], conv_b[i:i + 1], B=B, S=S, name=f"ffn_conv_{i}")
        yf = matmul(act, W_down[i], name=f"ffn_down_{i}")
        xout = gate_res_fwd(xin, yf, g2, B=B, S=S, name=f"ffn_res_{i}")
        return xout, (xin, h2, up, act, yf)

    sh1, sc1, g1 = mods[0][0], mods[0][1], mods[0][2]
    h1a = norm_mod_fwd(x0, norm_mix[0], sh1, sc1, B=B, S=S, name="att_norm")
    qkv = matmul(h1a, W_qkv, out_dtype=BF16, name="att_qkv")
    q, k, v = [_to_heads(qkv[:, i * D:(i + 1) * D], B, S) for i in range(3)]
    o, ltot = attn_fwd_wide(q, k, v, name="att_fwd")
    o2 = _from_heads(o, B, S).astype(BF16)
    ya = matmul(o2, W_o_attn, name="att_out")
    x1 = gate_res_fwd(x0, ya, g1, B=B, S=S, name="att_res")
    x2, ffn0 = ffn_fwd(x1, 0)

    lr, li, bbr, bbi = _ssm_discretize(a_re[0], a_im[0], log_dt[0], b_re[0], b_im[0])
    J = G // GROUPS_PER_BLOCK
    Wst = GROUPS_PER_BLOCK * P
    bre_blk = _block_diag_in(bbr).astype(BF16)
    bim_blk = _block_diag_in(bbi).astype(BF16)
    cre_blk = _block_diag_out(c_re[0]).astype(BF16)
    cim_blk = _block_diag_out(c_im[0]).astype(BF16)
    lr8 = jnp.broadcast_to(lr.reshape(J, 1, Wst), (J, SEGMENTS, Wst))
    li8 = jnp.broadcast_to(li.reshape(J, 1, Wst), (J, SEGMENTS, Wst))
    sh1s, sc1s, g1s = mods[1][0], mods[1][1], mods[1][2]
    h1s = norm_mod_fwd(x2, norm_mix[1], sh1s, sc1s, B=B, S=S, name="ssm_norm")
    h1p = _interleave(h1s, B, S)
    u = matmul(h1p, W_in, name="ssm_in")
    y_ssm = ssm_fwd(u, bre_blk, bim_blk, cre_blk, cim_blk, lr8, li8, d_skip_full, B=B, S=S, name="ssm_scan_fwd")
    zb = gelu_fwd(y_ssm, B=B, S=S, name="ssm_gelu")
    s_glu = matmul(zb, W_glu, bias=b_glu_full, name="ssm_glu_proj")
    gb = glu_fwd(y_ssm, s_glu, B=B, S=S, name="ssm_glu")
    ys_p = matmul(gb, W_o_ssm, name="ssm_out")
    ys = _deinterleave(ys_p, B, S)
    x3 = gate_res_fwd(x2, ys, g1s, B=B, S=S, name="ssm_res")
    x4, ffn1 = ffn_fwd(x3, 1)

    dx4, loss_p, dsh_f, dsc_f, dnorm_out = final_loss(x4, tgt, norm_out, sh_f, sc_f, B=B, S=S, name="loss_head")
    loss = lax.psum(jnp.sum(loss_p), ("x", "y", "c"))

    def ffn_bwd(dxo, i, saved):
        xin, h2, up, act, yf = saved
        sc2, g2 = mods[i][4], mods[i][5]
        dyf, dg2 = gate_res_bwd(dxo, yf, g2, B=B, S=S, name=f"ffn_res_bwd_{i}")
        dact = matmul(dyf, W_down[i], tb=True, name=f"ffn_down_dx_{i}")
        dW_down = matmul(act, dyf, ta=True, name=f"ffn_down_dw_{i}")
        d3, dcb = conv_gate_bwd1(up, dact, conv_w_full[i], conv_b[i:i + 1], B=B, S=S, name=f"ffn_conv_bwd1_{i}")
        dup, dcw = conv_bwd2(d3, up, conv_w_full[i], B=B, S=S, name=f"ffn_conv_bwd2_{i}")
        dh2 = matmul(dup, W_up[i], tb=True, name=f"ffn_up_dx_{i}")
        dW_up = matmul(h2, dup, ta=True, name=f"ffn_up_dw_{i}")
        dxin, dsh2, dsc2, dnf = norm_mod_bwd(dh2, xin, dxo, norm_ffn[i], sc2, B=B, S=S, name=f"ffn_norm_bwd_{i}")
        return dxin, dict(dW_down=dW_down, dW_up=dW_up, dconv_b=jnp.sum(dcb, axis=0).reshape(F2),
                          dconv_w=jnp.sum(dcw, axis=0), dnorm_ffn=jnp.sum(dnf, axis=0), dsh2=dsh2, dsc2=dsc2, dg2=dg2)

    dx3, gf1 = ffn_bwd(dx4, 1, ffn1)

    dys_p, dg1s = gate_res_bwd(_interleave(dx3, B, S), ys_p, g1s, B=B, S=S, name="ssm_res_bwd")
    dgb = matmul(dys_p, W_o_ssm, tb=True, name="ssm_out_dx")
    dW_o_ssm = matmul(gb, dys_p, ta=True, name="ssm_out_dw")
    ds_glu, dz1, db_glu = glu_bwd1(y_ssm, s_glu, dgb, B=B, S=S, name="ssm_glu_bwd1")
    dz2 = matmul(ds_glu, W_glu, tb=True, name="ssm_glu_dx")
    dW_glu = matmul(zb, ds_glu, ta=True, name="ssm_glu_dw")
    dy_ssm = glu_bwd2(y_ssm, dz1, dz2, B=B, S=S, name="ssm_glu_bwd2")
    du, dbre, dbim, dcre, dcim, dlr8, dli8, ddsk = ssm_bwd(u, dy_ssm, bre_blk, bim_blk, cre_blk, cim_blk, lr8, li8,
                                                           d_skip_full, B=B, S=S, name="ssm_scan_bwd")
    dub = du.astype(BF16)
    dh1p = matmul(dub, W_in, tb=True, name="ssm_in_dx")
    dW_in = matmul(h1p, dub, ta=True, name="ssm_in_dw")
    dx2, dsh1s, dsc1s, dnm1 = norm_mod_bwd(_deinterleave(dh1p, B, S), x2, dx3, norm_mix[1], sc1s, B=B, S=S,
                                           name="ssm_norm_bwd")
    dlr = jnp.sum(dlr8, axis=(0, 2)).reshape(G, P)
    dli = jnp.sum(dli8, axis=(0, 2)).reshape(G, P)
    dbbr = _block_diag_in_grad(jnp.sum(dbre, axis=0), G, P, H)
    dbbi = _block_diag_in_grad(jnp.sum(dbim, axis=0), G, P, H)
    dc_re = _block_diag_out_grad(jnp.sum(dcre, axis=0), G, H, P)
    dc_im = _block_diag_out_grad(jnp.sum(dcim, axis=0), G, H, P)
    dd_skip = jnp.sum(ddsk, axis=0).reshape(D)

    dx1, gf0 = ffn_bwd(dx2, 0, ffn0)

    dya, dg1 = gate_res_bwd(dx1, ya, g1, B=B, S=S, name="att_res_bwd")
    do2 = matmul(dya, W_o_attn, tb=True, out_dtype=BF16, name="att_out_dx")
    dW_o_attn = matmul(o2, dya, ta=True, name="att_out_dw")
    dq, dk, dv = attn_bwd_wide(q, k, v, ltot, _to_heads(do2, B, S), name="att_bwd")
    dqkv = jnp.concatenate([_from_heads(t, B, S) for t in (dq, dk, dv)], axis=1)
    dh1a = matmul(dqkv, W_qkv, tb=True, name="att_qkv_dx")
    dW_qkv = matmul(h1a, dqkv, ta=True, name="att_qkv_dw")
    grad_x, dsh1, dsc1, dnm0 = norm_mod_bwd(dh1a, x0, dx1, norm_mix[0], sc1, B=B, S=S, name="att_norm_bwd")

    g_rows_cm = jnp.concatenate([dW_o_attn.reshape(N_CHIPS, Dq, D), dW_in.reshape(N_CHIPS, Dq, D),
                                 dW_glu.reshape(N_CHIPS, Dq, D), dW_o_ssm.reshape(N_CHIPS, Dq, D),
                                 gf0["dW_down"].reshape(N_CHIPS, Fq, D), gf1["dW_down"].reshape(N_CHIPS, Fq, D)], axis=1)
    g_up_cm = jnp.concatenate([_chip_major(gf0["dW_up"], 1), _chip_major(gf1["dW_up"], 1)], axis=1)
    r_qkv, r_rows, r_up = reduce_scatter_chips([_chip_major(dW_qkv, 1), g_rows_cm, g_up_cm])
    grad_w_qkv = r_qkv[None]
    grad_w_o_attn = r_rows[0 * Dq:1 * Dq][None]
    grad_w_in_ssm = r_rows[1 * Dq:2 * Dq][None]
    grad_w_glu = r_rows[2 * Dq:3 * Dq][None]
    grad_w_o_ssm = r_rows[3 * Dq:4 * Dq][None]
    grad_w_down = r_rows[4 * Dq:].reshape(2, Fq, D)
    grad_w_up = r_up.reshape(2, D, -1)

    dmod_rows = jnp.concatenate([dsh1, dsc1, dg1, gf0["dsh2"], gf0["dsc2"], gf0["dg2"],
                                 dsh1s, dsc1s, dg1s, gf1["dsh2"], gf1["dsc2"], gf1["dg2"], dsh_f, dsc_f], axis=1)
    dmod_all = all_gather8(dmod_rows, name="gather_dmod").reshape(BG, 14 * D)
    grad_w_mod = jnp.stack([
        matmul(c_act, lax.dynamic_slice(dmod_all, (0, i * 6 * D + chip * mod_w), (BG, mod_w)), ta=True,
               name=f"mod_dw_{i}") for i in range(2)])
    grad_w_fin = matmul(c_act, lax.dynamic_slice(dmod_all, (0, 12 * D + chip * fin_w), (BG, fin_w)), ta=True,
                        name="fin_dw")

    parts = [jnp.concatenate([jnp.sum(dnm0, axis=0), jnp.sum(dnm1, axis=0)]),
             jnp.concatenate([gf0["dnorm_ffn"], gf1["dnorm_ffn"]]),
             jnp.sum(dmod_rows[:, :12 * D], axis=0),
             dlr.reshape(-1), dli.reshape(-1), dbbr.reshape(-1), dbbi.reshape(-1), dc_re.reshape(-1), dc_im.reshape(-1),
             dd_skip, jnp.sum(db_glu, axis=0),
             gf0["dconv_w"].reshape(-1), gf1["dconv_w"].reshape(-1), gf0["dconv_b"], gf1["dconv_b"],
             jnp.sum(dnorm_out, axis=0), jnp.sum(dmod_rows[:, 12 * D:], axis=0)]
    sizes = [int(p.shape[0]) for p in parts]
    flat = jnp.concatenate(parts)
    width = 1024
    padded = -(-flat.shape[0] // (8 * width)) * (8 * width)
    flat = jnp.pad(flat, (0, padded - flat.shape[0])).reshape(-1, width)
    summed = sum_leading(all_gather8(flat, name="gather_small_grads"), name="sum_small_grads").reshape(-1)
    offs = [0]
    for s_ in sizes:
        offs.append(offs[-1] + s_)
    (s_nm, s_nf, s_bmod, s_lr, s_li, s_bbr, s_bbi, s_cre, s_cim, s_dsk, s_bglu, s_cw0, s_cw1, s_cb0, s_cb1, s_no,
     s_bfin) = [summed[offs[i]:offs[i + 1]] for i in range(len(sizes))]
    _, disc_vjp = jax.vjp(_ssm_discretize, a_re[0], a_im[0], log_dt[0], b_re[0], b_im[0])
    ga_re, ga_im, glog_dt, gb_re, gb_im = disc_vjp((s_lr.reshape(G, P), s_li.reshape(G, P), s_bbr.reshape(G, P, H),
                                                    s_bbi.reshape(G, P, H)))
    grad_norm_mix = s_nm.reshape(2, D)
    grad_norm_ffn = s_nf.reshape(2, D)
    grad_b_mod = s_bmod.reshape(2, 6 * D)
    grad_c_re = s_cre.reshape(1, G, H, P)
    grad_c_im = s_cim.reshape(1, G, H, P)
    grad_d_skip = lax.dynamic_slice(s_dsk, (chip * Dq,), (Dq,)).reshape(1, Dq)
    grad_b_glu = lax.dynamic_slice(s_bglu, (chip * Dq,), (Dq,)).reshape(1, Dq)
    cw_full = jnp.stack([s_cw0.reshape(3, F2), s_cw1.reshape(3, F2)])
    grad_conv_w = lax.dynamic_slice(cw_full, (0, 0, chip * (F2 // N_CHIPS)), (2, 3, F2 // N_CHIPS))
    grad_conv_b = jnp.stack([s_cb0, s_cb1])
    grad_norm_out = s_no
    grad_b_fin = s_bfin

    grads = dict(
        norm_mix=grad_norm_mix, norm_ffn=grad_norm_ffn, w_mod=grad_w_mod, b_mod=grad_b_mod, w_qkv=grad_w_qkv,
        w_o_attn=grad_w_o_attn, w_in_ssm=grad_w_in_ssm, a_re=ga_re[None], a_im=ga_im[None], log_dt=glog_dt[None],
        b_re=gb_re[None], b_im=gb_im[None], c_re=grad_c_re, c_im=grad_c_im, d_skip=grad_d_skip, w_glu=grad_w_glu,
        b_glu=grad_b_glu, w_o_ssm=grad_w_o_ssm, w_up=grad_w_up, conv_w=grad_conv_w, conv_b=grad_conv_b,
        w_down=grad_w_down, norm_out=grad_norm_out, w_fin=grad_w_fin, b_fin=grad_b_fin)
    weights = dict(
        norm_mix=norm_mix, norm_ffn=norm_ffn, w_mod=w_mod, b_mod=b_mod, w_qkv=w_qkv, w_o_attn=w_o_attn,
        w_in_ssm=w_in_ssm, a_re=a_re, a_im=a_im, log_dt=log_dt, b_re=b_re, b_im=b_im, c_re=c_re, c_im=c_im,
        d_skip=d_skip, w_glu=w_glu, b_glu=b_glu, w_o_ssm=w_o_ssm, w_up=w_up, conv_w=conv_w, conv_b=conv_b,
        w_down=w_down, norm_out=norm_out, w_fin=w_fin, b_fin=b_fin)
    m_in = dict(
        norm_mix=m_norm_mix, norm_ffn=m_norm_ffn, w_mod=m_w_mod, b_mod=m_b_mod, w_qkv=m_w_qkv, w_o_attn=m_w_o_attn,
        w_in_ssm=m_w_in_ssm, a_re=m_a_re, a_im=m_a_im, log_dt=m_log_dt, b_re=m_b_re, b_im=m_b_im, c_re=m_c_re,
        c_im=m_c_im, d_skip=m_d_skip, w_glu=m_w_glu, b_glu=m_b_glu, w_o_ssm=m_w_o_ssm, w_up=m_w_up, conv_w=m_conv_w,
        conv_b=m_conv_b, w_down=m_w_down, norm_out=m_norm_out, w_fin=m_w_fin, b_fin=m_b_fin)
    v_in = dict(
        norm_mix=v_norm_mix, norm_ffn=v_norm_ffn, w_mod=v_w_mod, b_mod=v_b_mod, w_qkv=v_w_qkv, w_o_attn=v_w_o_attn,
        w_in_ssm=v_w_in_ssm, a_re=v_a_re, a_im=v_a_im, log_dt=v_log_dt, b_re=v_b_re, b_im=v_b_im, c_re=v_c_re,
        c_im=v_c_im, d_skip=v_d_skip, w_glu=v_w_glu, b_glu=v_b_glu, w_o_ssm=v_w_o_ssm, w_up=v_w_up, conv_w=v_conv_w,
        conv_b=v_conv_b, w_down=v_w_down, norm_out=v_norm_out, w_fin=v_w_fin, b_fin=v_b_fin)
    names = list(weights)
    for n_ in names:
        grads[n_] = grads[n_].reshape(weights[n_].shape)

    big = ("w_mod", "w_qkv", "w_o_attn", "w_in_ssm", "w_glu", "w_o_ssm", "w_up", "w_down", "w_fin")
    delta, new_m, new_v = {}, {}, {}
    for n_ in big:
        shp = weights[n_].shape
        two_d = lambda a: a.reshape(-1, shp[-1])
        d_, m_, v_ = adamw(two_d(weights[n_]), two_d(grads[n_]), two_d(m_in[n_]), two_d(v_in[n_]), name=f"adamw_{n_}")
        delta[n_], new_m[n_], new_v[n_] = d_.reshape(shp), m_.reshape(shp), v_.reshape(shp)
    rest = [n_ for n_ in names if n_ not in big]

    def pack(tree):
        f = jnp.concatenate([tree[n_].reshape(-1) for n_ in rest])
        pad_to = -(-f.shape[0] // (8 * width)) * (8 * width)
        return jnp.pad(f, (0, pad_to - f.shape[0]), constant_values=1.0).reshape(-1, width)

    d_, m_, v_ = adamw(pack(weights), pack(grads), pack(m_in), pack(v_in), name="adamw_small")
    off = 0
    for n_ in rest:
        sz = int(math.prod(weights[n_].shape))
        shp = weights[n_].shape
        delta[n_] = d_.reshape(-1)[off:off + sz].reshape(shp)
        new_m[n_] = m_.reshape(-1)[off:off + sz].reshape(shp)
        new_v[n_] = v_.reshape(-1)[off:off + sz].reshape(shp)
        off += sz

    return (loss, grad_x.reshape(B, S, D), *[grads[n_] for n_ in names], *[delta[n_] for n_ in names],
            *[new_m[n_] for n_ in names], *[new_v[n_] for n_ in names])
```

```python
import functools
import math

import jax
import jax.numpy as jnp
from jax import lax
from jax.experimental import pallas as pl
from jax.experimental.pallas import tpu as pltpu

F32 = jnp.float32
BF16 = jnp.bfloat16
MESH = pl.DeviceIdType.MESH

HEAD_DIM = 64
SSM_GROUP = 16
STATE = 64
GROUPS_PER_BLOCK = 8
SEGMENTS = 16
SCAN_UNROLL = 4
EPS = 1e-6
ADAM_LR = 0.001
ADAM_B1 = 0.9
ADAM_B2 = 0.999
ADAM_EPS = 1e-08
ADAM_WD = 0.01
ADAM_STEP = 10
N_CHIPS = 4
N_DEV = 8
V7X_VMEM_LIMIT = 56 * 1024 * 1024
ATT_BLOCK = 128
ATT_HEADS = 8
ATT_HEADS_BWD = 4


def _tile(n, prefs):
    for p in prefs:
        if n % p == 0:
            return p
    return n


def _params(sem, vmem=V7X_VMEM_LIMIT):
    return pltpu.CompilerParams(dimension_semantics=sem, vmem_limit_bytes=vmem)


def matmul(a, b, *, ta=False, tb=False, bias=None, out_dtype=F32, b_chips=False, out_chips=False, name):
    if ta:
        K, M = a.shape
    else:
        M, K = a.shape
    b_rows, b_cols = (b.shape[1], N_CHIPS * b.shape[2]) if b_chips else b.shape
    if tb:
        N, Kb = b_rows, b_cols
    else:
        Kb, N = b_rows, b_cols
    assert K == Kb, (a.shape, b.shape, ta, tb)
    n_cut = N // N_CHIPS if (out_chips or (b_chips and not tb)) else N
    k_cut = K // N_CHIPS if (b_chips and tb) else K
    tm = _tile(M, (1024, 512, 256, 128))
    tn = _tile(n_cut, (1024, 1408, 768, 512, 256, 128))
    tk = k_cut if k_cut <= 2816 else _tile(k_cut, (1024, 512, 256, 128))
    nk = K // tk
    npc = n_cut // tn
    kpc = k_cut // tk
    dims = (((0,) if ta else (1,), (1,) if tb else (0,)), ((), ()))

    def body(*refs):
        a_ref, b_ref = refs[:2]
        bias_ref = refs[2] if bias is not None else None
        o_ref = refs[-2] if nk > 1 else refs[-1]

        def finish(r):
            if bias_ref is not None:
                r = r + bias_ref[...]
            o_ref[...] = r.astype(o_ref.dtype)

        prod = lax.dot_general(a_ref[...].astype(BF16), b_ref[...].astype(BF16), dims, preferred_element_type=F32)
        if nk == 1:
            finish(prod)
            return
        acc_ref = refs[-1]
        k = pl.program_id(2)

        @pl.when(k == 0)
        def _():
            acc_ref[...] = prod

        @pl.when(k > 0)
        def _():
            acc_ref[...] += prod

        @pl.when(k == nk - 1)
        def _():
            finish(acc_ref[...])

    a_spec = pl.BlockSpec((tk, tm), lambda i, j, k: (k, i)) if ta else pl.BlockSpec((tm, tk), lambda i, j, k: (i, k))
    if not b_chips:
        b_spec = pl.BlockSpec((tn, tk), lambda i, j, k: (j, k)) if tb else pl.BlockSpec((tk, tn), lambda i, j, k: (k, j))
    elif tb:
        b_spec = pl.BlockSpec((None, tn, tk), lambda i, j, k: (lax.div(k, kpc), j, lax.rem(k, kpc)))
    else:
        b_spec = pl.BlockSpec((None, tk, tn), lambda i, j, k: (lax.div(j, npc), k, lax.rem(j, npc)))
    in_specs = [a_spec, b_spec]
    args = [a, b]
    if bias is not None:
        in_specs.append(pl.BlockSpec((1, tn), lambda i, j, k: (0, j)))
        args.append(bias.reshape(1, N).astype(F32))
    if out_chips:
        out_shape = jax.ShapeDtypeStruct((N_CHIPS, M, n_cut), out_dtype)
        out_spec = pl.BlockSpec((None, tm, tn), lambda i, j, k: (lax.div(j, npc), i, lax.rem(j, npc)))
    else:
        out_shape = jax.ShapeDtypeStruct((M, N), out_dtype)
        out_spec = pl.BlockSpec((tm, tn), lambda i, j, k: (i, j))
    return pl.pallas_call(
        body, name=name,
        out_shape=out_shape,
        grid=(M // tm, N // tn, nk),
        in_specs=in_specs,
        out_specs=out_spec,
        scratch_shapes=[pltpu.VMEM((tm, tn), F32)] if nk > 1 else [],
        compiler_params=_params(("parallel", "parallel", "arbitrary")),
    )(*args)


def rowwise(fn, tiled, per_seq, glob, out_tiled, out_seq, *, B, S, name, rows=512):
    tm = _tile(S, (rows, 128, 64, 32, 16, 8))
    nt = S // tm
    n_in = len(tiled) + len(per_seq) + len(glob)
    n_ot = len(out_tiled)

    def body(*refs):
        ins = refs[:n_in]
        outs = refs[n_in:]
        vals = fn(*[r[...] for r in ins])
        if not isinstance(vals, (tuple, list)):
            vals = (vals,)
        assert len(vals) == len(outs), (name, len(vals), len(outs))
        for o_ref, v in zip(outs[:n_ot], vals[:n_ot]):
            o_ref[...] = v.astype(o_ref.dtype)
        t = pl.program_id(1)
        for o_ref, v in zip(outs[n_ot:], vals[n_ot:]):
            def first(o_ref=o_ref, v=v):
                o_ref[...] = v.astype(F32)

            def later(o_ref=o_ref, v=v):
                o_ref[...] += v.astype(F32)

            pl.when(t == 0)(first)
            pl.when(t > 0)(later)

    in_specs = [pl.BlockSpec((tm, a.shape[1]), lambda b, t: (b * nt + t, 0)) for a in tiled]
    in_specs += [pl.BlockSpec((None, 1, a.shape[1]), lambda b, t: (b, 0, 0)) for a in per_seq]
    in_specs += [pl.BlockSpec(a.shape, lambda b, t: (0,) * a.ndim) for a in glob]
    out_shape = [jax.ShapeDtypeStruct((B * S, w), dt) for w, dt in out_tiled]
    out_shape += [jax.ShapeDtypeStruct((B, 1, w), F32) for w in out_seq]
    out_specs = [pl.BlockSpec((tm, w), lambda b, t: (b * nt + t, 0)) for w, _ in out_tiled]
    out_specs += [pl.BlockSpec((None, 1, w), lambda b, t: (b, 0, 0)) for w in out_seq]
    res = pl.pallas_call(
        body, name=name, out_shape=out_shape, grid=(B, nt), in_specs=in_specs, out_specs=out_specs,
        compiler_params=_params(("parallel", "arbitrary")),
    )(*tiled, *[a.reshape(B, 1, a.shape[1]) for a in per_seq], *glob)
    res = list(res)
    for i in range(n_ot, len(res)):
        res[i] = res[i].reshape(B, res[i].shape[-1])
    return res


def _rms(x):
    r = lax.rsqrt(jnp.mean(x * x, axis=-1, keepdims=True) + EPS)
    return x * r, r


def norm_mod_fwd(x, g, sh, sc, *, B, S, name):
    def fn(x, sh, sc, g):
        xn, _ = _rms(x)
        return (xn * g) * (1.0 + sc) + sh

    return rowwise(fn, [x], [sh, sc], [g.reshape(1, -1)], [(x.shape[1], BF16)], [], B=B, S=S, name=name)[0]


def _norm_mod_bwd_math(dh, x, sc, g):
    xn, r = _rms(x)
    y = xn * g
    dy = dh * (1.0 + sc)
    dxn = dy * g
    dx = r * (dxn - xn * jnp.mean(dxn * xn, axis=-1, keepdims=True))
    dsh = jnp.sum(dh, axis=0, keepdims=True)
    dsc = jnp.sum(dh * y, axis=0, keepdims=True)
    dg = jnp.sum(dy * xn, axis=0, keepdims=True)
    return dx, dsh, dsc, dg


def norm_mod_bwd(dh, x, dres, g, sc, *, B, S, name):
    D = x.shape[1]

    def fn(dh, x, dres, sc, g):
        dx, dsh, dsc, dg = _norm_mod_bwd_math(dh.astype(F32), x, sc, g)
        return dres + dx, dsh, dsc, dg

    return rowwise(fn, [dh, x, dres], [sc], [g.reshape(1, -1)], [(D, F32)], [D, D, D], B=B, S=S, name=name)


def gate_res_fwd(x, y, gate, *, B, S, name):
    return rowwise(lambda x, y, g: x + g * y, [x, y], [gate], [], [(x.shape[1], F32)], [], B=B, S=S, name=name)[0]


def gate_res_bwd(dx, y, gate, *, B, S, name):
    D = dx.shape[1]

    def fn(dx, y, g):
        return g * dx, jnp.sum(dx * y, axis=0, keepdims=True)

    return rowwise(fn, [dx, y], [gate], [], [(D, BF16)], [D], B=B, S=S, name=name)


def final_loss(x, tgt, g, sh, sc, *, B, S, name):
    D = x.shape[1]

    def fn(x, tgt, sh, sc, g):
        xn, _ = _rms(x)
        y = (xn * g) * (1.0 + sc) + sh
        err = y - tgt
        loss = 0.5 * jnp.sum(err * err, axis=0, keepdims=True) * (1.0 / D)
        dx, dsh, dsc, dg = _norm_mod_bwd_math(err * (1.0 / D), x, sc, g)
        return dx, loss, dsh, dsc, dg

    return rowwise(fn, [x, tgt], [sh, sc], [g.reshape(1, -1)], [(D, F32)], [D, D, D, D], B=B, S=S, name=name)


def _gelu(y):
    c0 = math.sqrt(2.0 / math.pi)
    t = jnp.tanh(c0 * (y + 0.044715 * (y * y * y)))
    return 0.5 * y * (1.0 + t), t


def _sigmoid(s):
    return 1.0 / (1.0 + jnp.exp(-s))


def gelu_fwd(y, *, B, S, name):
    return rowwise(lambda y: _gelu(y)[0], [y], [], [], [(y.shape[1], BF16)], [], B=B, S=S, name=name)[0]


def glu_fwd(y, s, *, B, S, name):
    return rowwise(lambda y, s: _gelu(y)[0] * _sigmoid(s), [y, s], [], [], [(y.shape[1], BF16)], [], B=B, S=S,
                   name=name)[0]


def glu_bwd1(y, s, dg, *, B, S, name):
    D = y.shape[1]

    def fn(y, s, dg):
        z = _gelu(y)[0]
        sig = _sigmoid(s)
        ds = dg * z * sig * (1.0 - sig)
        return ds, dg * sig, jnp.sum(ds, axis=0, keepdims=True)

    return rowwise(fn, [y, s, dg], [], [], [(D, BF16), (D, F32)], [D], B=B, S=S, name=name)


def glu_bwd2(y, dz1, dz2, *, B, S, name):
    D = y.shape[1]
    c0 = math.sqrt(2.0 / math.pi)

    def fn(y, dz1, dz2):
        _, t = _gelu(y)
        dgelu = 0.5 * (1.0 + t) + 0.5 * y * (1.0 - t * t) * c0 * (1.0 + 3.0 * 0.044715 * y * y)
        return (dz1 + dz2) * dgelu

    return rowwise(fn, [y, dz1, dz2], [], [], [(D, F32)], [], B=B, S=S, name=name)[0]


def silu_rows(c, *, name):
    R, W = c.shape
    return rowwise(lambda c: c * _sigmoid(c), [c], [], [], [(W, F32)], [], B=1, S=R, name=name)[0]


def _shift_down(cur, h6, h7):
    rows = lax.broadcasted_iota(jnp.int32, cur.shape, 0)
    m1 = jnp.where(rows == 0, h7, pltpu.roll(cur, 1, 0))
    m2 = jnp.where(rows == 0, h6, jnp.where(rows == 1, h7, pltpu.roll(cur, 2, 0)))
    return m1, m2


def _conv3(cur, halo_ref, w_ref, has_prev):
    h6 = jnp.where(has_prev, halo_ref[6:7, :], 0.0)
    h7 = jnp.where(has_prev, halo_ref[7:8, :], 0.0)
    m1, m2 = _shift_down(cur, h6, h7)
    return w_ref[2:3, :] * cur + w_ref[1:2, :] * m1 + w_ref[0:1, :] * m2, m1, m2


def _conv_tiles(S, F):
    ts = _tile(S, (1024, 512, 256, 128, 64, 32, 16, 8))
    tn = _tile(F, (256, 128))
    return ts, tn, S // ts, F // tn


def conv_gate_fwd(up, cw, cb, *, B, S, name):
    F = up.shape[1] // 2
    ts, tn, nts, nF = _conv_tiles(S, F)
    hb = ts // 8

    def body(g_ref, gh_ref, v_ref, vh_ref, wg_ref, wv_ref, bg_ref, bv_ref, o_ref):
        has_prev = pl.program_id(2) > 0
        gc = _conv3(g_ref[...], gh_ref, wg_ref, has_prev)[0] + bg_ref[...]
        vc = _conv3(v_ref[...], vh_ref, wv_ref, has_prev)[0] + bv_ref[...]
        o_ref[...] = (gc * _sigmoid(gc) * vc).astype(o_ref.dtype)

    def cur(off):
        return pl.BlockSpec((ts, tn), lambda b, j, t: (b * nts + t, j + off))

    def halo(off):
        return pl.BlockSpec((8, tn), lambda b, j, t: (jnp.maximum((b * nts + t) * hb - 1, 0), j + off))

    def vec(rows, off):
        return pl.BlockSpec((rows, tn), lambda b, j, t: (0, j + off))

    return pl.pallas_call(
        body, name=name, out_shape=jax.ShapeDtypeStruct((B * S, F), BF16), grid=(B, nF, nts),
        in_specs=[cur(0), halo(0), cur(nF), halo(nF), vec(3, 0), vec(3, nF), vec(1, 0), vec(1, nF)],
        out_specs=pl.BlockSpec((ts, tn), lambda b, j, t: (b * nts + t, j)),
        compiler_params=_params(("parallel", "parallel", "arbitrary")),
    )(up, up, up, up, cw, cw, cb, cb)


def conv_gate_bwd1(up, dact, cw, cb, *, B, S, name):
    F = up.shape[1] // 2
    ts, tn, nts, nF = _conv_tiles(S, F)
    hb = ts // 8

    def body(g_ref, gh_ref, v_ref, vh_ref, da_ref, wg_ref, wv_ref, bg_ref, bv_ref, d_ref, db_ref):
        t = pl.program_id(2)
        has_prev = t > 0
        gc = _conv3(g_ref[...], gh_ref, wg_ref, has_prev)[0] + bg_ref[...]
        vc = _conv3(v_ref[...], vh_ref, wv_ref, has_prev)[0] + bv_ref[...]
        sig = _sigmoid(gc)
        da = da_ref[...]
        dg = da * vc * (sig * (1.0 + gc * (1.0 - sig)))
        dv = da * (gc * sig)
        d_ref[0] = dg
        d_ref[1] = dv
        part = jnp.concatenate([jnp.sum(dg, axis=0, keepdims=True), jnp.sum(dv, axis=0, keepdims=True)], axis=0)

        @pl.when(t == 0)
        def _():
            db_ref[...] = part

        @pl.when(t > 0)
        def _():
            db_ref[...] += part

    def cur(off):
        return pl.BlockSpec((ts, tn), lambda b, j, t: (b * nts + t, j + off))

    def halo(off):
        return pl.BlockSpec((8, tn), lambda b, j, t: (jnp.maximum((b * nts + t) * hb - 1, 0), j + off))

    def vec(rows, off):
        return pl.BlockSpec((rows, tn), lambda b, j, t: (0, j + off))

    return pl.pallas_call(
        body, name=name,
        out_shape=[jax.ShapeDtypeStruct((2, B * S, F), F32), jax.ShapeDtypeStruct((B, 2, F), F32)],
        grid=(B, nF, nts),
        in_specs=[cur(0), halo(0), cur(nF), halo(nF), cur(0), vec(3, 0), vec(3, nF), vec(1, 0), vec(1, nF)],
        out_specs=[pl.BlockSpec((2, ts, tn), lambda b, j, t: (0, b * nts + t, j)),
                   pl.BlockSpec((None, 2, tn), lambda b, j, t: (b, 0, j))],
        compiler_params=_params(("parallel", "parallel", "arbitrary")),
    )(up, up, up, up, dact, cw, cw, cb, cb)


def conv_bwd2(d3, up, cw, *, B, S, name):
    F = up.shape[1] // 2
    ts, tn, nts, nF = _conv_tiles(S, F)
    hb = ts // 8
    last8 = B * S // 8 - 1

    def body(d_ref, da_ref, u_ref, uh_ref, w_ref, o_ref, dw_ref):
        t = pl.program_id(3)
        d = d_ref[...]
        has_next = t < nts - 1
        a0 = jnp.where(has_next, da_ref[0:1, :], 0.0)
        a1 = jnp.where(has_next, da_ref[1:2, :], 0.0)
        rows = lax.broadcasted_iota(jnp.int32, d.shape, 0)
        p1 = jnp.where(rows == ts - 1, a0, pltpu.roll(d, ts - 1, 0))
        p2 = jnp.where(rows == ts - 1, a1, jnp.where(rows == ts - 2, a0, pltpu.roll(d, ts - 2, 0)))
        o_ref[...] = (w_ref[2:3, :] * d + w_ref[1:2, :] * p1 + w_ref[0:1, :] * p2).astype(o_ref.dtype)
        u = u_ref[...]
        has_prev = t > 0
        h6 = jnp.where(has_prev, uh_ref[6:7, :], 0.0)
        h7 = jnp.where(has_prev, uh_ref[7:8, :], 0.0)
        m1, m2 = _shift_down(u, h6, h7)
        part = jnp.concatenate([jnp.sum(d * m2, axis=0, keepdims=True), jnp.sum(d * m1, axis=0, keepdims=True),
                                jnp.sum(d * u, axis=0, keepdims=True)], axis=0)

        @pl.when(t == 0)
        def _():
            dw_ref[...] = part

        @pl.when(t > 0)
        def _():
            dw_ref[...] += part

    return pl.pallas_call(
        body, name=name,
        out_shape=[jax.ShapeDtypeStruct((B * S, 2 * F), BF16), jax.ShapeDtypeStruct((B, 3, 2 * F), F32)],
        grid=(B, 2, nF, nts),
        in_specs=[
            pl.BlockSpec((None, ts, tn), lambda b, g, j, t: (g, b * nts + t, j)),
            pl.BlockSpec((None, 8, tn), lambda b, g, j, t: (g, jnp.minimum((b * nts + t + 1) * hb, last8), j)),
            pl.BlockSpec((ts, tn), lambda b, g, j, t: (b * nts + t, g * nF + j)),
            pl.BlockSpec((8, tn), lambda b, g, j, t: (jnp.maximum((b * nts + t) * hb - 1, 0), g * nF + j)),
            pl.BlockSpec((3, tn), lambda b, g, j, t: (0, g * nF + j)),
        ],
        out_specs=[pl.BlockSpec((ts, tn), lambda b, g, j, t: (b * nts + t, g * nF + j)),
                   pl.BlockSpec((None, 3, tn), lambda b, g, j, t: (b, 0, g * nF + j))],
        compiler_params=_params(("parallel", "parallel", "parallel", "arbitrary")),
    )(d3, d3, up, up, cw)


MASKED_LOG = -1e30


def _split2(x):
    bits = lax.bitcast_convert_type(x, jnp.uint32) & jnp.uint32(0xFFFF0000)
    hi = lax.bitcast_convert_type(bits, F32)
    return hi.astype(BF16), (x - hi).astype(BF16)


def _split_dot(x, m):
    hi, lo = _split2(x)
    return jnp.dot(hi, m, preferred_element_type=F32) + jnp.dot(lo, m, preferred_element_type=F32)


def _nt(a, b):
    return lax.dot_general(a, b, (((1,), (1,)), ((), ())), preferred_element_type=F32)


def _tn(a, b):
    return lax.dot_general(a, b, (((0,), (0,)), ((), ())), preferred_element_type=F32)


def _att_scores(q, k, mask, prescaled=False):
    z = _nt(q, k)
    if not prescaled:
        z = z * (HEAD_DIM ** -0.5)
    e = jnp.exp(-jnp.abs(z))
    sp = jnp.log(1.0 + e)
    lb = jnp.minimum(z, 0.0) - sp
    l1 = lb - z
    if mask is not None:
        lb = jnp.where(mask, lb, MASKED_LOG)
        l1 = jnp.where(mask, l1, 0.0)
    return z, lb, l1, e


def _col_to_row(col, eye):
    return jnp.sum(jnp.where(eye, col, 0.0), axis=0, keepdims=True)


def _row_to_col(row, eye):
    return jnp.sum(jnp.where(eye, row, 0.0), axis=1, keepdims=True)


def attn_fwd(q, k, v, *, name):
    B, H, S, dh = q.shape
    T = ATT_BLOCK
    nq = S // T

    G = _tile(H, (ATT_HEADS, 2))

    def body(q_ref, k_ref, v_ref, o_ref, l_ref):
        r = lax.broadcasted_iota(jnp.int32, (T, T), 0)
        c = lax.broadcasted_iota(jnp.int32, (T, T), 1)
        later = (r > c).astype(BF16)
        eye = r == c
        diag = c < r
        blk = lax.broadcasted_iota(jnp.int32, (nq, T), 0)

        later2 = jnp.concatenate([later, later], axis=0)

        def scores(g, qb, k0, mask):
            _, lb, l1, _ = _att_scores(qb, k_ref[g, pl.ds(k0, T), :], mask)
            return lb, jnp.concatenate(_split2(l1), axis=1), jnp.sum(l1, axis=1, keepdims=True)

        def weigh_all(k0, sc, st):
            suf = jnp.dot(jnp.concatenate([s[1] for s in sc], axis=0), later2, preferred_element_type=F32)
            out = []
            for g in range(G):
                lb, _, rowsum = sc[g]
                acc, run = st[g]
                w = jnp.exp(lb + suf[g * T:(g + 1) * T] + run)
                acc = acc + jnp.dot(w.astype(BF16), v_ref[g, pl.ds(k0, T), :], preferred_element_type=F32)
                out.append((acc, run + rowsum))
            return tuple(out)

        def qblock(i, totals):
            q0 = pl.multiple_of(i * T, T)
            qbs = [q_ref[g, pl.ds(q0, T), :] for g in range(G)]
            sc0 = tuple(scores(g, qbs[g], q0, diag) for g in range(G))
            st0 = tuple((jnp.zeros((T, dh), F32), jnp.zeros((T, 1), F32)) for _ in range(G))

            def kblock(jj, carry):
                sc, st = carry
                k_next = pl.multiple_of((i - jj) * T, T)
                k_cur = pl.multiple_of((i - jj + 1) * T, T)
                st = weigh_all(k_cur, sc, st)
                sc_next = tuple(scores(g, qbs[g], k_next, None) for g in range(G))
                return sc_next, st

            sc, st = lax.fori_loop(1, i + 1, kblock, (sc0, st0))
            st = weigh_all(0, sc, st)
            for g in range(G):
                o_ref[g, pl.ds(q0, T), :] = st[g][0]
            return tuple(jnp.where(blk == i, _col_to_row(st[g][1], eye), totals[g]) for g in range(G))

        totals = lax.fori_loop(0, nq, qblock, tuple(jnp.zeros((nq, T), F32) for _ in range(G)))
        for g in range(G):
            l_ref[g] = totals[g]

    spec = pl.BlockSpec((None, G, S, dh), lambda b, h: (b, h, 0, 0))
    lspec = pl.BlockSpec((None, G, nq, T), lambda b, h: (b, h, 0, 0))
    return pl.pallas_call(
        body, name=name,
        out_shape=[jax.ShapeDtypeStruct((B, H, S, dh), F32), jax.ShapeDtypeStruct((B, H, nq, T), F32)],
        grid=(B, H // G), in_specs=[spec, spec, spec], out_specs=[spec, lspec],
        compiler_params=_params(("parallel", "parallel")),
    )(q, k, v)


def attn_bwd(q, k, v, ltot, do, *, name):
    B, H, S, dh = q.shape
    T = ATT_BLOCK
    nq = S // T
    scale = HEAD_DIM ** -0.5

    G = _tile(H, (ATT_HEADS_BWD, 2))

    def body(q_ref, k_ref, v_ref, l_ref, do_ref, dq_ref, dk_ref, dv_ref, dk_acc, dv_acc):
        r = lax.broadcasted_iota(jnp.int32, (T, T), 0)
        c = lax.broadcasted_iota(jnp.int32, (T, T), 1)
        upto = (r <= c).astype(BF16)
        before = (r < c).astype(BF16)
        upto2 = jnp.concatenate([upto, upto], axis=0)
        before2 = jnp.concatenate([before, before], axis=0)
        eye = r == c
        diag = c < r
        blk = lax.broadcasted_iota(jnp.int32, (nq, T), 0)
        dk_acc[...] = jnp.zeros_like(dk_acc)
        dv_acc[...] = jnp.zeros_like(dv_acc)

        def scores(g, qb, dob, k0, mask):
            z, lb, l1, e = _att_scores(qb, k_ref[g, pl.ds(k0, T), :], mask)
            inv = 1.0 / (1.0 + e)
            small = e * inv
            pos = z >= 0.0
            beta = jnp.where(pos, inv, small)
            omb = jnp.where(pos, small, inv)
            if mask is not None:
                beta = jnp.where(mask, beta, 0.0)
            dw = _nt(dob, v_ref[g, pl.ds(k0, T), :])
            return lb, jnp.concatenate(_split2(l1), axis=1), jnp.sum(l1, axis=1, keepdims=True), dw, beta, omb

        def grads_all(qbs, dobs, tots, k0, sc, st):
            pre = jnp.dot(jnp.concatenate([s[1] for s in sc], axis=0), upto2, preferred_element_type=F32)
            dlws = []
            for g in range(G):
                lb = sc[g][0]
                w = jnp.exp(lb + (tots[g] - (pre[g * T:(g + 1) * T] + st[g][1])))
                dv_acc[g, pl.ds(k0, T), :] += _tn(w.astype(BF16), dobs[g])
                dlws.append(sc[g][3] * w)
            pre_d = jnp.dot(jnp.concatenate([jnp.concatenate(_split2(d), axis=1) for d in dlws], axis=0), before2,
                            preferred_element_type=F32)
            out = []
            for g in range(G):
                _, _, rowsum, _, beta, omb = sc[g]
                dq, run_l, run_d = st[g]
                dl1 = pre_d[g * T:(g + 1) * T] + run_d
                dz = ((dlws[g] * omb - dl1 * beta) * scale).astype(BF16)
                dq = dq + jnp.dot(dz, k_ref[g, pl.ds(k0, T), :], preferred_element_type=F32)
                dk_acc[g, pl.ds(k0, T), :] += _tn(dz, qbs[g])
                out.append((dq, run_l + rowsum, run_d + jnp.sum(dlws[g], axis=1, keepdims=True)))
            return tuple(out)

        def block_inputs(i, q0):
            qbs = [q_ref[g, pl.ds(q0, T), :] for g in range(G)]
            dobs = [do_ref[g, pl.ds(q0, T), :] for g in range(G)]
            tots = [_row_to_col(jnp.sum(jnp.where(blk == i, l_ref[g], 0.0), axis=0, keepdims=True), eye)
                    for g in range(G)]
            z1 = jnp.zeros((T, 1), F32)
            return qbs, dobs, tots, tuple((jnp.zeros((T, dh), F32), z1, z1) for _ in range(G))

        qbs, dobs, tots, st = block_inputs(0, 0)
        st = grads_all(qbs, dobs, tots, 0, tuple(scores(g, qbs[g], dobs[g], 0, diag) for g in range(G)), st)
        for g in range(G):
            dq_ref[g, 0:T, :] = st[g][0].astype(dq_ref.dtype)

        def qblock(i, carry0):
            q0 = pl.multiple_of(i * T, T)
            qbs, dobs, tots, st0 = block_inputs(i, q0)
            sc0 = tuple(scores(g, qbs[g], dobs[g], 0, None) for g in range(G))

            def kblock(j, carry):
                sc, st = carry
                k_cur = pl.multiple_of(j * T, T)
                k_next = pl.multiple_of((j + 1) * T, T)
                sc_next = tuple(scores(g, qbs[g], dobs[g], k_next, None) for g in range(G))
                st = grads_all(qbs, dobs, tots, k_cur, sc, st)
                return sc_next, st

            sc, st = lax.fori_loop(0, i - 1, kblock, (sc0, st0))
            k_last = pl.multiple_of((i - 1) * T, T)
            st = grads_all(qbs, dobs, tots, k_last, sc, st)
            sc_diag = tuple(scores(g, qbs[g], dobs[g], q0, diag) for g in range(G))
            st = grads_all(qbs, dobs, tots, q0, sc_diag, st)
            for g in range(G):
                dq_ref[g, pl.ds(q0, T), :] = st[g][0].astype(dq_ref.dtype)
            return carry0

        lax.fori_loop(1, nq, qblock, 0)
        dk_ref[...] = dk_acc[...].astype(dk_ref.dtype)
        dv_ref[...] = dv_acc[...].astype(dv_ref.dtype)

    spec = pl.BlockSpec((None, G, S, dh), lambda b, h: (b, h, 0, 0))
    lspec = pl.BlockSpec((None, G, nq, T), lambda b, h: (b, h, 0, 0))
    shp = jax.ShapeDtypeStruct((B, H, S, dh), BF16)
    return pl.pallas_call(
        body, name=name, out_shape=[shp, shp, shp], grid=(B, H // G),
        in_specs=[spec, spec, spec, lspec, spec], out_specs=[spec] * 3,
        scratch_shapes=[pltpu.VMEM((G, S, dh), F32), pltpu.VMEM((G, S, dh), F32)],
        compiler_params=_params(("parallel", "parallel")),
    )(q, k, v, ltot, do)


def _wide_consts(T, W):
    r = lax.broadcasted_iota(jnp.int32, (W, W), 0)
    c = lax.broadcasted_iota(jnp.int32, (W, W), 1)
    two = lambda m: jnp.concatenate([m.astype(BF16)] * 2, axis=0)
    qrow = lax.broadcasted_iota(jnp.int32, (T, W), 0)
    kcol = lax.broadcasted_iota(jnp.int32, (T, W), 1)
    er = lax.broadcasted_iota(jnp.int32, (T, T), 0)
    ec = lax.broadcasted_iota(jnp.int32, (T, T), 1)
    return two(r > c), two(r <= c), two(r < c), qrow, kcol, er == ec


def attn_fwd_wide(q, k, v, *, name):
    B, H, S, dh = q.shape
    T = ATT_BLOCK
    W = 2 * T
    nq = S // T
    G = _tile(H, (ATT_HEADS, 2))

    def body(q_ref, k_ref, v_ref, o_ref, l_ref):
        later2, _, _, qrow, kcol, eye = _wide_consts(T, W)
        blk = lax.broadcasted_iota(jnp.int32, (nq, T), 0)

        def step(qbs, k0, st, mask):
            parts, lbs, sums = [], [], []
            for g in range(G):
                _, lb, l1, _ = _att_scores(qbs[g], k_ref[g, pl.ds(k0, W), :], mask, prescaled=True)
                parts.append(jnp.concatenate(_split2(l1), axis=1))
                lbs.append(lb)
                sums.append(jnp.sum(l1, axis=1, keepdims=True))
            suf = jnp.dot(jnp.concatenate(parts, axis=0), later2, preferred_element_type=F32)
            out = []
            for g in range(G):
                acc, run = st[g]
                w = jnp.exp(lbs[g] + suf[g * T:(g + 1) * T] + run)
                acc = acc + jnp.dot(w.astype(BF16), v_ref[g, pl.ds(k0, W), :], preferred_element_type=F32)
                out.append((acc, run + sums[g]))
            return tuple(out)

        def qblock(i, totals):
            q0 = pl.multiple_of(i * T, T)
            qbs = [q_ref[g, pl.ds(q0, T), :] * (HEAD_DIM ** -0.5) for g in range(G)]
            half = jnp.right_shift(i, 1)
            last = half * W
            k_last = pl.multiple_of(last, W)
            mask = (k_last + kcol) < (q0 + qrow)
            st = tuple((jnp.zeros((T, dh), F32), jnp.zeros((T, 1), F32)) for _ in range(G))
            st = step(qbs, k_last, st, mask)

            def kblock(jj, st):
                return step(qbs, pl.multiple_of(last - jj * W, W), st, None)

            st = lax.fori_loop(1, half + 1, kblock, st)
            for g in range(G):
                o_ref[g, pl.ds(q0, T), :] = st[g][0]
            return tuple(jnp.where(blk == i, _col_to_row(st[g][1], eye), totals[g]) for g in range(G))

        totals = lax.fori_loop(0, nq, qblock, tuple(jnp.zeros((nq, T), F32) for _ in range(G)))
        for g in range(G):
            l_ref[g] = totals[g]

    spec = pl.BlockSpec((None, G, S, dh), lambda b, h: (b, h, 0, 0))
    lspec = pl.BlockSpec((None, G, nq, T), lambda b, h: (b, h, 0, 0))
    return pl.pallas_call(
        body, name=name,
        out_shape=[jax.ShapeDtypeStruct((B, H, S, dh), F32), jax.ShapeDtypeStruct((B, H, nq, T), F32)],
        grid=(B, H // G), in_specs=[spec, spec, spec], out_specs=[spec, lspec],
        compiler_params=_params(("parallel", "parallel")),
    )(q, k, v)


def attn_bwd_wide(q, k, v, ltot, do, *, name):
    B, H, S, dh = q.shape
    T = ATT_BLOCK
    W = 2 * T
    nq = S // T
    scale = HEAD_DIM ** -0.5
    G = _tile(H, (ATT_HEADS_BWD, 2))

    def body(q_ref, k_ref, v_ref, l_ref, do_ref, dq_ref, dk_ref, dv_ref, dk_acc, dv_acc):
        _, upto2, before2, qrow, kcol, eye = _wide_consts(T, W)
        blk = lax.broadcasted_iota(jnp.int32, (nq, T), 0)
        dk_acc[...] = jnp.zeros_like(dk_acc)
        dv_acc[...] = jnp.zeros_like(dv_acc)

        def step(qbs, dobs, tots, k0, st, mask):
            sc = []
            for g in range(G):
                z, lb, l1, e = _att_scores(qbs[g], k_ref[g, pl.ds(k0, W), :], mask, prescaled=True)
                inv = 1.0 / (1.0 + e)
                small = e * inv
                pos = z >= 0.0
                beta = jnp.where(pos, inv, small)
                omb = jnp.where(pos, small, inv)
                if mask is not None:
                    beta = jnp.where(mask, beta, 0.0)
                dw = _nt(dobs[g], v_ref[g, pl.ds(k0, W), :])
                sc.append((lb, jnp.concatenate(_split2(l1), axis=1), jnp.sum(l1, axis=1, keepdims=True), dw, beta, omb))
            pre = jnp.dot(jnp.concatenate([s[1] for s in sc], axis=0), upto2, preferred_element_type=F32)
            dlws = []
            for g in range(G):
                w = jnp.exp(sc[g][0] + (tots[g] - (pre[g * T:(g + 1) * T] + st[g][1])))
                dv_acc[g, pl.ds(k0, W), :] += _tn(w.astype(BF16), dobs[g])
                dlws.append(sc[g][3] * w)
            pre_d = jnp.dot(jnp.concatenate([jnp.concatenate(_split2(d), axis=1) for d in dlws], axis=0), before2,
                            preferred_element_type=F32)
            out = []
            for g in range(G):
                _, _, rowsum, _, beta, omb = sc[g]
                dq, run_l, run_d = st[g]
                dl1 = pre_d[g * T:(g + 1) * T] + run_d
                dz = (dlws[g] * omb - dl1 * beta).astype(BF16)
                dq = dq + jnp.dot(dz, k_ref[g, pl.ds(k0, W), :], preferred_element_type=F32)
                dk_acc[g, pl.ds(k0, W), :] += _tn(dz, qbs[g])
                out.append((dq, run_l + rowsum, run_d + jnp.sum(dlws[g], axis=1, keepdims=True)))
            return tuple(out)

        def qblock(i, carry0):
            q0 = pl.multiple_of(i * T, T)
            qbs = [q_ref[g, pl.ds(q0, T), :] * scale for g in range(G)]
            dobs = [do_ref[g, pl.ds(q0, T), :] for g in range(G)]
            tots = [_row_to_col(jnp.sum(jnp.where(blk == i, l_ref[g], 0.0), axis=0, keepdims=True), eye)
                    for g in range(G)]
            z1 = jnp.zeros((T, 1), F32)
            st = tuple((jnp.zeros((T, dh), F32), z1, z1) for _ in range(G))

            def kblock(j, st):
                return step(qbs, dobs, tots, pl.multiple_of(j * W, W), st, None)

            half = jnp.right_shift(i, 1)
            st = lax.fori_loop(0, half, kblock, st)
            k_last = pl.multiple_of(half * W, W)
            st = step(qbs, dobs, tots, k_last, st, (k_last + kcol) < (q0 + qrow))
            for g in range(G):
                dq_ref[g, pl.ds(q0, T), :] = (st[g][0] * scale).astype(dq_ref.dtype)
            return carry0

        lax.fori_loop(0, nq, qblock, 0)
        dk_ref[...] = dk_acc[...].astype(dk_ref.dtype)
        dv_ref[...] = dv_acc[...].astype(dv_ref.dtype)

    spec = pl.BlockSpec((None, G, S, dh), lambda b, h: (b, h, 0, 0))
    lspec = pl.BlockSpec((None, G, nq, T), lambda b, h: (b, h, 0, 0))
    shp = jax.ShapeDtypeStruct((B, H, S, dh), BF16)
    return pl.pallas_call(
        body, name=name, out_shape=[shp, shp, shp], grid=(B, H // G),
        in_specs=[spec, spec, spec, lspec, spec], out_specs=[spec] * 3,
        scratch_shapes=[pltpu.VMEM((G, S, dh), F32), pltpu.VMEM((G, S, dh), F32)],
        compiler_params=_params(("parallel", "parallel")),
    )(q, k, v, ltot, do)


def _cmul(ar, ai, br, bi):
    return ar * br - ai * bi, ar * bi + ai * br


def _cpow(lr, li, n):
    rr, ri = None, None
    br, bi = lr, li
    while n:
        if n & 1:
            rr, ri = (br, bi) if rr is None else _cmul(rr, ri, br, bi)
        n >>= 1
        if n:
            br, bi = _cmul(br, bi, br, bi)
    return rr, ri


def _ssm_scan(sr, si, lr, li, n_steps, reverse):
    W = sr.shape[1]
    R = SEGMENTS
    lim = -li if reverse else li
    zero = jnp.zeros((R, W), F32)

    def row(k):
        i = (n_steps - 1 - k) if reverse else k
        return pl.multiple_of(i * R, R)

    def local(k, st):
        cr, ci = st
        r0 = row(k)
        pr, pi = _cmul(lr, lim, cr, ci)
        nr = pr + sr[pl.ds(r0, R), :]
        ni = pi + si[pl.ds(r0, R), :]
        sr[pl.ds(r0, R), :] = nr
        si[pl.ds(r0, R), :] = ni
        return nr, ni

    er, ei = lax.fori_loop(0, n_steps, local, (zero, zero), unroll=SCAN_UNROLL)
    lnr, lni = _cpow(lr, lim, n_steps)
    rows = lax.broadcasted_iota(jnp.int32, (R, W), 0)
    cr, ci = zero, zero
    for step in range(1, R):
        tr, ti = _cmul(lnr, lni, cr, ci)
        tr, ti = tr + er, ti + ei
        if reverse:
            seg = R - 1 - step
            tr, ti = pltpu.roll(tr, R - 1, 0), pltpu.roll(ti, R - 1, 0)
        else:
            seg = step
            tr, ti = pltpu.roll(tr, 1, 0), pltpu.roll(ti, 1, 0)
        cr = jnp.where(rows == seg, tr, cr)
        ci = jnp.where(rows == seg, ti, ci)

    def fix(k, st):
        pr, pi = st
        r0 = row(k)
        ar, ai = _cmul(pr, pi, cr, ci)
        sr[pl.ds(r0, R), :] += ar
        si[pl.ds(r0, R), :] += ai
        return _cmul(lr, lim, pr, pi)

    lax.fori_loop(0, n_steps, fix, (lr, lim), unroll=SCAN_UNROLL)
    return cr, ci


def _ssm_specs(S, W):
    CH = GROUPS_PER_BLOCK * SSM_GROUP
    return dict(
        rows=pl.BlockSpec((S, CH), lambda b, j: (b, j)),
        b=pl.BlockSpec((None, CH, W), lambda b, j: (j, 0, 0)),
        c=pl.BlockSpec((None, W, CH), lambda b, j: (j, 0, 0)),
        lam=pl.BlockSpec((None, SEGMENTS, W), lambda b, j: (j, 0, 0)),
        vec=pl.BlockSpec((1, CH), lambda b, j: (0, j)),
    )


def ssm_fwd(u, bre, bim, cre, cim, lr8, li8, dsk, *, B, S, name):
    D = u.shape[1]
    J, CH, W = bre.shape
    n_steps = S // SEGMENTS
    sp = _ssm_specs(S, W)

    def body(u_ref, bre_ref, bim_ref, cre_ref, cim_ref, lr_ref, li_ref, dsk_ref, y_ref, sr, si):
        u = u_ref[...]
        ub = u.astype(BF16)
        sr[...] = jnp.dot(ub, bre_ref[...], preferred_element_type=F32)
        si[...] = jnp.dot(ub, bim_ref[...], preferred_element_type=F32)
        _ssm_scan(sr, si, lr_ref[...], li_ref[...], n_steps, False)
        y = jnp.dot(sr[...].astype(BF16), cre_ref[...], preferred_element_type=F32)
        y = y - jnp.dot(si[...].astype(BF16), cim_ref[...], preferred_element_type=F32)
        y_ref[...] = y + dsk_ref[...] * u

    return pl.pallas_call(
        body, name=name, out_shape=jax.ShapeDtypeStruct((B * S, D), F32), grid=(B, J),
        in_specs=[sp["rows"], sp["b"], sp["b"], sp["c"], sp["c"], sp["lam"], sp["lam"], sp["vec"]],
        out_specs=sp["rows"],
        scratch_shapes=[pltpu.VMEM((S, W), F32), pltpu.VMEM((S, W), F32)],
        compiler_params=_params(("parallel", "parallel")),
    )(u, bre, bim, cre, cim, lr8, li8, dsk)


def ssm_bwd(u, dy, bre, bim, cre, cim, lr8, li8, dsk, *, B, S, name):
    D = u.shape[1]
    J, CH, W = bre.shape
    n_steps = S // SEGMENTS
    sp = _ssm_specs(S, W)

    def body(u_ref, dy_ref, bre_ref, bim_ref, cre_ref, cim_ref, lr_ref, li_ref, dsk_ref,
             du_ref, dbre_ref, dbim_ref, dcre_ref, dcim_ref, dlr_ref, dli_ref, ddsk_ref, sr, si, ar, ai):
        u = u_ref[...]
        dy = dy_ref[...]
        ub = u.astype(BF16)
        dyb = dy.astype(BF16)
        lr, li = lr_ref[...], li_ref[...]
        sr[...] = jnp.dot(ub, bre_ref[...], preferred_element_type=F32)
        si[...] = jnp.dot(ub, bim_ref[...], preferred_element_type=F32)
        cr, ci = _ssm_scan(sr, si, lr, li, n_steps, False)
        ar[...] = _nt(dyb, cre_ref[...])
        ai[...] = -_nt(dyb, cim_ref[...])
        _ssm_scan(ar, ai, lr, li, n_steps, True)

        def dlam(k, st):
            dr, di = st
            r0 = pl.multiple_of((k + 1) * SEGMENTS, SEGMENTS)
            p0 = pl.multiple_of(k * SEGMENTS, SEGMENTS)
            pr, pi = sr[pl.ds(p0, SEGMENTS), :], si[pl.ds(p0, SEGMENTS), :]
            xr, xi = ar[pl.ds(r0, SEGMENTS), :], ai[pl.ds(r0, SEGMENTS), :]
            return dr + pr * xr + pi * xi, di + pr * xi - pi * xr

        xr, xi = ar[0:SEGMENTS, :], ai[0:SEGMENTS, :]
        dr, di = lax.fori_loop(0, n_steps - 1, dlam, (cr * xr + ci * xi, cr * xi - ci * xr), unroll=SCAN_UNROLL)
        dlr_ref[...] = dr
        dli_ref[...] = di
        arb = ar[...].astype(BF16)
        aib = ai[...].astype(BF16)
        du_ref[...] = _nt(arb, bre_ref[...]) + _nt(aib, bim_ref[...]) + dsk_ref[...] * dy
        dbre_ref[...] = _tn(ub, arb)
        dbim_ref[...] = _tn(ub, aib)
        dcre_ref[...] = _tn(sr[...].astype(BF16), dyb)
        dcim_ref[...] = -_tn(si[...].astype(BF16), dyb)
        ddsk_ref[...] = jnp.sum(dy * u, axis=0, keepdims=True)

    def per(shape):
        return pl.BlockSpec((None, None) + shape, lambda b, j: (b, j, 0, 0))

    return pl.pallas_call(
        body, name=name,
        out_shape=[jax.ShapeDtypeStruct((B * S, D), F32),
                   jax.ShapeDtypeStruct((B, J, CH, W), F32), jax.ShapeDtypeStruct((B, J, CH, W), F32),
                   jax.ShapeDtypeStruct((B, J, W, CH), F32), jax.ShapeDtypeStruct((B, J, W, CH), F32),
                   jax.ShapeDtypeStruct((B, J, SEGMENTS, W), F32), jax.ShapeDtypeStruct((B, J, SEGMENTS, W), F32),
                   jax.ShapeDtypeStruct((B, J, 1, CH), F32)],
        grid=(B, J),
        in_specs=[sp["rows"], sp["rows"], sp["b"], sp["b"], sp["c"], sp["c"], sp["lam"], sp["lam"], sp["vec"]],
        out_specs=[sp["rows"], per((CH, W)), per((CH, W)), per((W, CH)), per((W, CH)), per((SEGMENTS, W)),
                   per((SEGMENTS, W)),
                   per((1, CH))],
        scratch_shapes=[pltpu.VMEM((S, W), F32)] * 4,
        compiler_params=_params(("parallel", "parallel")),
    )(u, dy, bre, bim, cre, cim, lr8, li8, dsk)


def _ssm_discretize(a_re, a_im, log_dt, b_re, b_im):
    dt = jnp.exp(log_dt)[:, None]
    er = jnp.exp(a_re * dt)
    lr = er * jnp.cos(a_im * dt)
    li = er * jnp.sin(a_im * dt)
    den = a_re * a_re + a_im * a_im
    fr = ((lr - 1.0) * a_re + li * a_im) / den
    fi = (li * a_re - (lr - 1.0) * a_im) / den
    bbr = fr[..., None] * b_re - fi[..., None] * b_im
    bbi = fr[..., None] * b_im + fi[..., None] * b_re
    return lr, li, bbr, bbi


def _block_diag_in(m):
    G, P, H = m.shape
    J = G // GROUPS_PER_BLOCK
    m = m.reshape(J, GROUPS_PER_BLOCK, P, H).transpose(0, 1, 3, 2)
    eye = jnp.eye(GROUPS_PER_BLOCK, dtype=m.dtype)
    out = m[:, :, :, None, :] * eye[None, :, None, :, None]
    return out.reshape(J, GROUPS_PER_BLOCK * H, GROUPS_PER_BLOCK * P)


def _block_diag_in_grad(d, G, P, H):
    J = G // GROUPS_PER_BLOCK
    d = d.reshape(J, GROUPS_PER_BLOCK, H, GROUPS_PER_BLOCK, P)
    idx = jnp.arange(GROUPS_PER_BLOCK)
    d = d[:, idx, :, idx, :]
    return d.transpose(1, 0, 3, 2).reshape(G, P, H)


def _block_diag_out(m):
    G, H, P = m.shape
    J = G // GROUPS_PER_BLOCK
    m = m.reshape(J, GROUPS_PER_BLOCK, H, P).transpose(0, 1, 3, 2)
    eye = jnp.eye(GROUPS_PER_BLOCK, dtype=m.dtype)
    out = m[:, :, :, None, :] * eye[None, :, None, :, None]
    return out.reshape(J, GROUPS_PER_BLOCK * P, GROUPS_PER_BLOCK * H)


def _block_diag_out_grad(d, G, H, P):
    J = G // GROUPS_PER_BLOCK
    d = d.reshape(J, GROUPS_PER_BLOCK, P, GROUPS_PER_BLOCK, H)
    idx = jnp.arange(GROUPS_PER_BLOCK)
    d = d[:, idx, :, idx, :]
    return d.transpose(1, 0, 3, 2).reshape(G, H, P)


def _interleave(a, B, S):
    L = S // SEGMENTS
    return a.reshape(B, SEGMENTS, L, a.shape[-1]).transpose(0, 2, 1, 3).reshape(B * S, a.shape[-1])


def _deinterleave(a, B, S):
    L = S // SEGMENTS
    return a.reshape(B, L, SEGMENTS, a.shape[-1]).transpose(0, 2, 1, 3).reshape(B * S, a.shape[-1])


def _adamw_math(w, g, m, v):
    m = ADAM_B1 * m + (1.0 - ADAM_B1) * g
    v = ADAM_B2 * v + (1.0 - ADAM_B2) * (g * g)
    m_hat = m / (1.0 - ADAM_B1 ** ADAM_STEP)
    v_hat = v / (1.0 - ADAM_B2 ** ADAM_STEP)
    delta = -ADAM_LR * (m_hat / (jnp.sqrt(v_hat) + ADAM_EPS) + ADAM_WD * w)
    return delta, m, v


def adamw(w, g, m, v, *, name):
    R, C = w.shape
    tr = _tile(R, (max(8, (1 << 18) // C // 8 * 8), 256, 128, 64, 32, 16, 8))

    def body(w_ref, g_ref, m_ref, v_ref, d_ref, nm_ref, nv_ref):
        d, nm, nv = _adamw_math(w_ref[...], g_ref[...], m_ref[...], v_ref[...])
        d_ref[...] = d
        nm_ref[...] = nm
        nv_ref[...] = nv

    spec = pl.BlockSpec((tr, C), lambda i: (i, 0))
    shp = jax.ShapeDtypeStruct((R, C), F32)
    return pl.pallas_call(
        body, name=name, out_shape=[shp, shp, shp], grid=(R // tr,), in_specs=[spec] * 4, out_specs=[spec] * 3,
        compiler_params=_params(("parallel",)),
    )(w, g, m, v)


def sum_leading(a, *, name, out_dtype=F32):
    n, R, C = a.shape
    tr = _tile(R, (256, 128, 64, 32, 16, 8))

    def body(a_ref, o_ref):
        acc = a_ref[0].astype(F32)
        for i in range(1, n):
            acc = acc + a_ref[i].astype(F32)
        o_ref[...] = acc.astype(o_ref.dtype)

    return pl.pallas_call(
        body, name=name, out_shape=jax.ShapeDtypeStruct((R, C), out_dtype), grid=(R // tr,),
        in_specs=[pl.BlockSpec((n, tr, C), lambda i: (0, i, 0))], out_specs=pl.BlockSpec((tr, C), lambda i: (i, 0)),
        compiler_params=_params(("parallel",)),
    )(a)


def _any_specs(n):
    return [pl.BlockSpec(memory_space=pl.ANY) for _ in range(n)]


def _coords():
    return lax.axis_index("x"), lax.axis_index("y"), lax.axis_index("c")


def _flip(v, bit):
    return (v + bit) % 2


def all_gather8(a, *, name):
    shape = a.shape

    def body(a_ref, o_ref, send_sems, recv_sems, local_sem):
        x, y, c = _coords()
        me = 4 * x + 2 * y + c
        mine = pltpu.make_async_copy(a_ref, o_ref.at[me], local_sem)
        mine.start()
        sends = []
        for k in range(1, N_DEV):
            peer = (_flip(x, (k >> 2) & 1), _flip(y, (k >> 1) & 1), _flip(c, k & 1))
            cp = pltpu.make_async_remote_copy(a_ref, o_ref.at[me], send_sems.at[k - 1], recv_sems.at[k - 1],
                                              device_id=peer, device_id_type=MESH)
            cp.start()
            sends.append(cp)
        for k in range(1, N_DEV):
            px, py, pc = _flip(x, (k >> 2) & 1), _flip(y, (k >> 1) & 1), _flip(c, k & 1)
            src = 4 * px + 2 * py + pc
            pltpu.make_async_remote_copy(a_ref, o_ref.at[src], send_sems.at[k - 1], recv_sems.at[k - 1],
                                         device_id=(px, py, pc), device_id_type=MESH).wait_recv()
        for cp in sends:
            cp.wait_send()
        mine.wait()

    return pl.pallas_call(
        body, name=name, out_shape=jax.ShapeDtypeStruct((N_DEV,) + shape, a.dtype),
        in_specs=_any_specs(1), out_specs=pl.BlockSpec(memory_space=pl.ANY),
        scratch_shapes=[pltpu.SemaphoreType.DMA((N_DEV - 1,)), pltpu.SemaphoreType.DMA((N_DEV - 1,)),
                        pltpu.SemaphoreType.DMA(())],
    )(a)


def _chip_of(x, y, p):
    px, py = _flip(x, (p >> 1) & 1), _flip(y, p & 1)
    return 2 * px + py, px, py


def gather_chip_shards(arrs, *, name):
    n = len(arrs)

    def body(*refs):
        ins, outs = refs[:n], refs[n:2 * n]
        ici_send, ici_recv, d2d_send, d2d_recv, local_sems = refs[2 * n:2 * n + 5]
        bufs = refs[2 * n + 5:]
        x, y, c = _coords()
        me = 2 * x + y
        loads = []
        for i in range(n):
            cp = pltpu.make_async_copy(ins[i], bufs[i], local_sems.at[i])
            cp.start()
            loads.append(cp)
        sends = []
        for i in range(n):
            half = ins[i].shape[0] // 2
            rows = pl.ds(c * half, half)
            for p in range(1, N_CHIPS):
                _, px, py = _chip_of(x, y, p)
                s = i * 3 + p - 1
                cp = pltpu.make_async_remote_copy(ins[i].at[rows], outs[i].at[me, rows], ici_send.at[s], ici_recv.at[s],
                                                  device_id=(px, py, c), device_id_type=MESH)
                cp.start()
                sends.append(cp)
        stores = []
        for i in range(n):
            loads[i].wait()
            cp = pltpu.make_async_copy(bufs[i], outs[i].at[me], local_sems.at[i])
            cp.start()
            stores.append(cp)
        for i in range(n):
            half = ins[i].shape[0] // 2
            rows = pl.ds(c * half, half)
            for p in range(1, N_CHIPS):
                src, px, py = _chip_of(x, y, p)
                s = i * 3 + p - 1
                pltpu.make_async_remote_copy(ins[i].at[rows], outs[i].at[src, rows], ici_send.at[s], ici_recv.at[s],
                                             device_id=(px, py, c), device_id_type=MESH).wait_recv()
                cp = pltpu.make_async_remote_copy(outs[i].at[src, rows], outs[i].at[src, rows], d2d_send.at[s],
                                                  d2d_recv.at[s], device_id=(x, y, 1 - c), device_id_type=MESH)
                cp.start()
                sends.append(cp)
        for i in range(n):
            half = ins[i].shape[0] // 2
            theirs = pl.ds((1 - c) * half, half)
            for p in range(1, N_CHIPS):
                src, _, _ = _chip_of(x, y, p)
                s = i * 3 + p - 1
                pltpu.make_async_remote_copy(outs[i].at[src, theirs], outs[i].at[src, theirs], d2d_send.at[s],
                                             d2d_recv.at[s], device_id=(x, y, 1 - c), device_id_type=MESH).wait_recv()
        for cp in sends:
            cp.wait_send()
        for cp in stores:
            cp.wait()

    dma = pltpu.SemaphoreType.DMA
    return pl.pallas_call(
        body, name=name,
        out_shape=[jax.ShapeDtypeStruct((N_CHIPS,) + a.shape, a.dtype) for a in arrs],
        in_specs=_any_specs(n), out_specs=_any_specs(n),
        scratch_shapes=[dma((3 * n,)), dma((3 * n,)), dma((3 * n,)), dma((3 * n,)), dma((n,))]
        + [pltpu.VMEM(a.shape, a.dtype) for a in arrs],
        compiler_params=pltpu.CompilerParams(vmem_limit_bytes=V7X_VMEM_LIMIT),
    )(*arrs)


def swap_halves(arrs, *, name):
    n = len(arrs)

    def body(*refs):
        ins, outs = refs[:n], refs[n:2 * n]
        send_sems, recv_sems = refs[2 * n:]
        x, y, c = _coords()
        cps = []
        for i in range(n):
            half = ins[i].shape[1] // 2
            cp = pltpu.make_async_remote_copy(ins[i].at[:, pl.ds((1 - c) * half, half)], outs[i], send_sems.at[i],
                                              recv_sems.at[i], device_id=(x, y, 1 - c), device_id_type=MESH)
            cp.start()
            cps.append(cp)
        for cp in cps:
            cp.wait()

    dma = pltpu.SemaphoreType.DMA
    return pl.pallas_call(
        body, name=name,
        out_shape=[jax.ShapeDtypeStruct((N_CHIPS, a.shape[1] // 2, a.shape[2]), a.dtype) for a in arrs],
        in_specs=_any_specs(n), out_specs=_any_specs(n), scratch_shapes=[dma((n,)), dma((n,))],
    )(*arrs)


def add_half(g, other, c_idx, *, name, out_dtype):
    _, R, C = g.shape
    half = R // 2
    tr = _tile(half, (256, 128, 64, 32, 16, 8))
    nt = half // tr

    def body(c_ref, g_ref, o_ref, out_ref):
        out_ref[...] = (g_ref[...] + o_ref[...]).astype(out_ref.dtype)

    return pl.pallas_call(
        body, name=name, out_shape=jax.ShapeDtypeStruct((N_CHIPS, half, C), out_dtype),
        grid_spec=pltpu.PrefetchScalarGridSpec(
            num_scalar_prefetch=1, grid=(N_CHIPS, nt),
            in_specs=[pl.BlockSpec((None, tr, C), lambda r, t, c_ref: (r, c_ref[0] * nt + t, 0)),
                      pl.BlockSpec((None, tr, C), lambda r, t, c_ref: (r, t, 0))],
            out_specs=pl.BlockSpec((None, tr, C), lambda r, t, c_ref: (r, t, 0))),
        compiler_params=_params(("parallel", "parallel")),
    )(c_idx, g, other)


def scatter_to_chips(arrs, *, name):
    n = len(arrs)

    def body(*refs):
        ins, outs = refs[:n], refs[n:2 * n]
        send_sems, recv_sems = refs[2 * n:]
        x, y, c = _coords()
        cps = []
        for i in range(n):
            for p in range(1, N_CHIPS):
                dst, px, py = _chip_of(x, y, p)
                s = i * 3 + p - 1
                cp = pltpu.make_async_remote_copy(ins[i].at[dst], outs[i].at[p - 1], send_sems.at[s], recv_sems.at[s],
                                                  device_id=(px, py, c), device_id_type=MESH)
                cp.start()
                cps.append(cp)
        for cp in cps:
            cp.wait()

    dma = pltpu.SemaphoreType.DMA
    return pl.pallas_call(
        body, name=name,
        out_shape=[jax.ShapeDtypeStruct((N_CHIPS - 1,) + a.shape[1:], a.dtype) for a in arrs],
        in_specs=_any_specs(n), out_specs=_any_specs(n), scratch_shapes=[dma((3 * n,)), dma((3 * n,))],
    )(*arrs)


def add_chips(h, got, r_idx, *, name):
    _, R, C = h.shape
    tr = _tile(R, (256, 128, 64, 32, 16, 8))

    def body(r_ref, h_ref, g_ref, out_ref):
        acc = h_ref[...].astype(F32)
        for p in range(N_CHIPS - 1):
            acc = acc + g_ref[p].astype(F32)
        out_ref[...] = acc

    return pl.pallas_call(
        body, name=name, out_shape=jax.ShapeDtypeStruct((R, C), F32),
        grid_spec=pltpu.PrefetchScalarGridSpec(
            num_scalar_prefetch=1, grid=(R // tr,),
            in_specs=[pl.BlockSpec((None, tr, C), lambda t, r_ref: (r_ref[0], t, 0)),
                      pl.BlockSpec((N_CHIPS - 1, tr, C), lambda t, r_ref: (0, t, 0))],
            out_specs=pl.BlockSpec((tr, C), lambda t, r_ref: (t, 0))),
        compiler_params=_params(("parallel",)),
    )(r_idx, h, got)


def join_halves(arrs, *, name):
    n = len(arrs)

    def body(*refs):
        ins, outs = refs[:n], refs[n:2 * n]
        send_sems, recv_sems, local_sems = refs[2 * n:2 * n + 3]
        bufs = refs[2 * n + 3:]
        x, y, c = _coords()
        loads, sends, stores = [], [], []
        for i in range(n):
            cp = pltpu.make_async_copy(ins[i], bufs[i], local_sems.at[i])
            cp.start()
            loads.append(cp)
        for i in range(n):
            half = ins[i].shape[0]
            cp = pltpu.make_async_remote_copy(ins[i], outs[i].at[pl.ds(c * half, half)], send_sems.at[i], recv_sems.at[i],
                                              device_id=(x, y, 1 - c), device_id_type=MESH)
            cp.start()
            sends.append(cp)
        for i in range(n):
            half = ins[i].shape[0]
            loads[i].wait()
            cp = pltpu.make_async_copy(bufs[i], outs[i].at[pl.ds(c * half, half)], local_sems.at[i])
            cp.start()
            stores.append(cp)
        for i in range(n):
            half = ins[i].shape[0]
            pltpu.make_async_remote_copy(ins[i], outs[i].at[pl.ds((1 - c) * half, half)], send_sems.at[i],
                                         recv_sems.at[i], device_id=(x, y, 1 - c), device_id_type=MESH).wait_recv()
        for i in range(n):
            sends[i].wait_send()
            stores[i].wait()

    dma = pltpu.SemaphoreType.DMA
    return pl.pallas_call(
        body, name=name,
        out_shape=[jax.ShapeDtypeStruct((2 * a.shape[0], a.shape[1]), a.dtype) for a in arrs],
        in_specs=_any_specs(n), out_specs=_any_specs(n),
        scratch_shapes=[dma((n,)), dma((n,)), dma((n,))] + [pltpu.VMEM(a.shape, a.dtype) for a in arrs],
        compiler_params=pltpu.CompilerParams(vmem_limit_bytes=V7X_VMEM_LIMIT),
    )(*arrs)


def reduce_scatter_chips(grads, wire_dtypes):
    x, y, c = _coords()
    c_idx = jnp.reshape(c, (1,)).astype(jnp.int32)
    r_idx = jnp.reshape(2 * x + y, (1,)).astype(jnp.int32)
    theirs = swap_halves(grads, name="rs_swap_halves")
    pair = [add_half(g, o, c_idx, name=f"rs_add_half_{i}", out_dtype=wire_dtypes[i])
            for i, (g, o) in enumerate(zip(grads, theirs))]
    got = scatter_to_chips(pair, name="rs_scatter_to_chips")
    mine = [add_chips(h, g, r_idx, name=f"rs_add_chips_{i}") for i, (h, g) in enumerate(zip(pair, got))]
    return join_halves(mine, name="rs_join_halves")


def _to_heads(t, B, S):
    return t.reshape(B, S, -1, HEAD_DIM).transpose(0, 2, 1, 3)


def _from_heads(t, B, S):
    return t.transpose(0, 2, 1, 3).reshape(B * S, -1)


def _chip_major(w, axis):
    n = w.shape[axis] // N_CHIPS
    parts = w.reshape(w.shape[:axis] + (N_CHIPS, n) + w.shape[axis + 1:])
    return jnp.moveaxis(parts, axis, 0)


def _from_chip_major(g, axis):
    g = jnp.moveaxis(g, 0, axis)
    return g.reshape(g.shape[:axis] + (g.shape[axis] * g.shape[axis + 1],) + g.shape[axis + 2:])


def kernel(x, c, norm_mix, norm_ffn, w_mod, b_mod, w_qkv, w_o_attn, w_in_ssm, a_re, a_im, log_dt, b_re, b_im, c_re, c_im, d_skip, w_glu, b_glu, w_o_ssm, w_up, conv_w, conv_b, w_down, norm_out, w_fin, b_fin, loss_target, m_norm_mix, m_norm_ffn, m_w_mod, m_b_mod, m_w_qkv, m_w_o_attn, m_w_in_ssm, m_a_re, m_a_im, m_log_dt, m_b_re, m_b_im, m_c_re, m_c_im, m_d_skip, m_w_glu, m_b_glu, m_w_o_ssm, m_w_up, m_conv_w, m_conv_b, m_w_down, m_norm_out, m_w_fin, m_b_fin, v_norm_mix, v_norm_ffn, v_w_mod, v_b_mod, v_w_qkv, v_w_o_attn, v_w_in_ssm, v_a_re, v_a_im, v_log_dt, v_b_re, v_b_im, v_c_re, v_c_im, v_d_skip, v_w_glu, v_b_glu, v_w_o_ssm, v_w_up, v_conv_w, v_conv_b, v_w_down, v_norm_out, v_w_fin, v_b_fin):
    B, S, D = x.shape
    T = B * S
    F2 = conv_b.shape[1]
    F = F2 // 2
    G, P = a_re.shape[1], a_re.shape[2]
    H = b_re.shape[3]
    mx, my, mc = _coords()
    chip = 2 * mx + my
    dev = 4 * mx + 2 * my + mc
    BG = N_DEV * B
    mod_w = w_mod.shape[2]
    fin_w = w_fin.shape[1]

    c_all = all_gather8(c, name="gather_c").reshape(BG, D)
    c_act = silu_rows(c_all, name="silu_c")
    b_mod_mine = lax.dynamic_slice(b_mod, (0, chip * mod_w), (2, mod_w))
    b_fin_mine = lax.dynamic_slice(b_fin, (chip * fin_w,), (fin_w,))
    cond = [matmul(c_act, w_mod[i], bias=b_mod_mine[i], name=f"mod_proj_{i}") for i in range(2)]
    cond.append(matmul(c_act, w_fin, bias=b_fin_mine, name="fin_proj"))
    cond_all = all_gather8(jnp.concatenate(cond, axis=1), name="gather_cond")
    cond_all = cond_all[::2]
    cond_rows = lax.dynamic_slice(cond_all, (0, dev * B, 0), (N_CHIPS, B, cond_all.shape[2]))
    mods = []
    for i in range(2):
        full = cond_rows[:, :, i * mod_w:(i + 1) * mod_w].transpose(1, 0, 2).reshape(B, N_CHIPS * mod_w)
        mods.append([full[:, k * D:(k + 1) * D] for k in range(6)])
    fin = cond_rows[:, :, 2 * mod_w:].transpose(1, 0, 2).reshape(B, N_CHIPS * fin_w)
    sh_f, sc_f = fin[:, :D], fin[:, D:]

    rows1024 = jnp.concatenate([w_o_attn[0], w_in_ssm[0], w_glu[0], w_o_ssm[0], w_down.reshape(-1, D)], axis=0)
    W_qkv, g_rows, W_up0, W_up1 = gather_chip_shards(
        [w_qkv[0].astype(BF16), rows1024.astype(BF16), w_up[0].astype(BF16), w_up[1].astype(BF16)],
        name="gather_weights")
    W_up = [W_up0, W_up1]
    Dq = D // N_CHIPS
    Fq = F // N_CHIPS
    W_o_attn = g_rows[:, 0 * Dq:1 * Dq].reshape(D, D)
    W_in = g_rows[:, 1 * Dq:2 * Dq].reshape(D, D)
    W_glu = g_rows[:, 2 * Dq:3 * Dq].reshape(D, D)
    W_o_ssm = g_rows[:, 3 * Dq:4 * Dq].reshape(D, D)
    W_down = [g_rows[:, 4 * Dq + i * Fq:4 * Dq + (i + 1) * Fq].reshape(F, D) for i in range(2)]
    small = jnp.concatenate([conv_w.reshape(6, -1), jnp.pad(d_skip, ((0, 0), (0, conv_w.shape[2] - Dq))),
                             jnp.pad(b_glu, ((0, 0), (0, conv_w.shape[2] - Dq)))], axis=0)
    small_all = all_gather8(small, name="gather_small")[::2]
    conv_w_full = _from_chip_major(small_all[:, :6].reshape(N_CHIPS, 2, 3, -1), 2)
    d_skip_full = small_all[:, 6, :Dq].reshape(1, D)
    b_glu_full = small_all[:, 7, :Dq].reshape(D)

    x0 = x.reshape(T, D)
    tgt = loss_target.reshape(T, D)

    def ffn_fwd(xin, i):
        sh2, sc2, g2 = mods[i][3], mods[i][4], mods[i][5]
        h2 = norm_mod_fwd(xin, norm_ffn[i], sh2, sc2, B=B, S=S, name=f"ffn_norm_{i}")
        up = matmul(h2, W_up[i], b_chips=True, name=f"ffn_up_{i}")
        act = conv_gate_fwd(up, conv_w_full[i], conv_b[i:i + 1], B=B, S=S, name=f"ffn_conv_{i}")
        yf = matmul(act, W_down[i], name=f"ffn_down_{i}")
        xout = gate_res_fwd(xin, yf, g2, B=B, S=S, name=f"ffn_res_{i}")
        return xout, (xin, h2, up, act, yf)

    sh1, sc1, g1 = mods[0][0], mods[0][1], mods[0][2]
    h1a = norm_mod_fwd(x0, norm_mix[0], sh1, sc1, B=B, S=S, name="att_norm")
    qkv = matmul(h1a, W_qkv, out_dtype=BF16, b_chips=True, name="att_qkv")
    q, k, v = [_to_heads(qkv[:, i * D:(i + 1) * D], B, S) for i in range(3)]
    o, ltot = attn_fwd_wide(q, k, v, name="att_fwd")
    o2 = _from_heads(o, B, S).astype(BF16)
    ya = matmul(o2, W_o_attn, name="att_out")
    x1 = gate_res_fwd(x0, ya, g1, B=B, S=S, name="att_res")
    x2, ffn0 = ffn_fwd(x1, 0)

    lr, li, bbr, bbi = _ssm_discretize(a_re[0], a_im[0], log_dt[0], b_re[0], b_im[0])
    J = G // GROUPS_PER_BLOCK
    Wst = GROUPS_PER_BLOCK * P
    bre_blk = _block_diag_in(bbr).astype(BF16)
    bim_blk = _block_diag_in(bbi).astype(BF16)
    cre_blk = _block_diag_out(c_re[0]).astype(BF16)
    cim_blk = _block_diag_out(c_im[0]).astype(BF16)
    lr8 = jnp.broadcast_to(lr.reshape(J, 1, Wst), (J, SEGMENTS, Wst))
    li8 = jnp.broadcast_to(li.reshape(J, 1, Wst), (J, SEGMENTS, Wst))
    sh1s, sc1s, g1s = mods[1][0], mods[1][1], mods[1][2]
    h1s = norm_mod_fwd(x2, norm_mix[1], sh1s, sc1s, B=B, S=S, name="ssm_norm")
    h1p = _interleave(h1s, B, S)
    u = matmul(h1p, W_in, name="ssm_in")
    y_ssm = ssm_fwd(u, bre_blk, bim_blk, cre_blk, cim_blk, lr8, li8, d_skip_full, B=B, S=S, name="ssm_scan_fwd")
    zb = gelu_fwd(y_ssm, B=B, S=S, name="ssm_gelu")
    s_glu = matmul(zb, W_glu, bias=b_glu_full, name="ssm_glu_proj")
    gb = glu_fwd(y_ssm, s_glu, B=B, S=S, name="ssm_glu")
    ys_p = matmul(gb, W_o_ssm, name="ssm_out")
    ys = _deinterleave(ys_p, B, S)
    x3 = gate_res_fwd(x2, ys, g1s, B=B, S=S, name="ssm_res")
    x4, ffn1 = ffn_fwd(x3, 1)

    dx4, loss_p, dsh_f, dsc_f, dnorm_out = final_loss(x4, tgt, norm_out, sh_f, sc_f, B=B, S=S, name="loss_head")
    loss = lax.psum(jnp.sum(loss_p), ("x", "y", "c"))

    def ffn_bwd(dxo, i, saved):
        xin, h2, up, act, yf = saved
        sc2, g2 = mods[i][4], mods[i][5]
        dyf, dg2 = gate_res_bwd(dxo, yf, g2, B=B, S=S, name=f"ffn_res_bwd_{i}")
        dact = matmul(dyf, W_down[i], tb=True, name=f"ffn_down_dx_{i}")
        dW_down = matmul(act, dyf, ta=True, name=f"ffn_down_dw_{i}")
        d3, dcb = conv_gate_bwd1(up, dact, conv_w_full[i], conv_b[i:i + 1], B=B, S=S, name=f"ffn_conv_bwd1_{i}")
        dup, dcw = conv_bwd2(d3, up, conv_w_full[i], B=B, S=S, name=f"ffn_conv_bwd2_{i}")
        dh2 = matmul(dup, W_up[i], tb=True, b_chips=True, name=f"ffn_up_dx_{i}")
        dW_up = matmul(h2, dup, ta=True, out_chips=True, name=f"ffn_up_dw_{i}")
        dxin, dsh2, dsc2, dnf = norm_mod_bwd(dh2, xin, dxo, norm_ffn[i], sc2, B=B, S=S, name=f"ffn_norm_bwd_{i}")
        return dxin, dict(dW_down=dW_down, dW_up=dW_up, dconv_b=jnp.sum(dcb, axis=0).reshape(F2),
                          dconv_w=jnp.sum(dcw, axis=0), dnorm_ffn=jnp.sum(dnf, axis=0), dsh2=dsh2, dsc2=dsc2, dg2=dg2)

    dx3, gf1 = ffn_bwd(dx4, 1, ffn1)

    dys_p, dg1s = gate_res_bwd(_interleave(dx3, B, S), ys_p, g1s, B=B, S=S, name="ssm_res_bwd")
    dgb = matmul(dys_p, W_o_ssm, tb=True, name="ssm_out_dx")
    dW_o_ssm = matmul(gb, dys_p, ta=True, name="ssm_out_dw")
    ds_glu, dz1, db_glu = glu_bwd1(y_ssm, s_glu, dgb, B=B, S=S, name="ssm_glu_bwd1")
    dz2 = matmul(ds_glu, W_glu, tb=True, name="ssm_glu_dx")
    dW_glu = matmul(zb, ds_glu, ta=True, name="ssm_glu_dw")
    dy_ssm = glu_bwd2(y_ssm, dz1, dz2, B=B, S=S, name="ssm_glu_bwd2")
    du, dbre, dbim, dcre, dcim, dlr8, dli8, ddsk = ssm_bwd(u, dy_ssm, bre_blk, bim_blk, cre_blk, cim_blk, lr8, li8,
                                                           d_skip_full, B=B, S=S, name="ssm_scan_bwd")
    dub = du.astype(BF16)
    dh1p = matmul(dub, W_in, tb=True, name="ssm_in_dx")
    dW_in = matmul(h1p, dub, ta=True, name="ssm_in_dw")
    dx2, dsh1s, dsc1s, dnm1 = norm_mod_bwd(_deinterleave(dh1p, B, S), x2, dx3, norm_mix[1], sc1s, B=B, S=S,
                                           name="ssm_norm_bwd")
    dlr = jnp.sum(dlr8, axis=(0, 2)).reshape(G, P)
    dli = jnp.sum(dli8, axis=(0, 2)).reshape(G, P)
    dbbr = _block_diag_in_grad(jnp.sum(dbre, axis=0), G, P, H)
    dbbi = _block_diag_in_grad(jnp.sum(dbim, axis=0), G, P, H)
    dc_re = _block_diag_out_grad(jnp.sum(dcre, axis=0), G, H, P)
    dc_im = _block_diag_out_grad(jnp.sum(dcim, axis=0), G, H, P)
    dd_skip = jnp.sum(ddsk, axis=0).reshape(D)

    dx1, gf0 = ffn_bwd(dx2, 0, ffn0)

    dya, dg1 = gate_res_bwd(dx1, ya, g1, B=B, S=S, name="att_res_bwd")
    do2 = matmul(dya, W_o_attn, tb=True, out_dtype=BF16, name="att_out_dx")
    dW_o_attn = matmul(o2, dya, ta=True, name="att_out_dw")
    dq, dk, dv = attn_bwd_wide(q, k, v, ltot, _to_heads(do2, B, S), name="att_bwd")
    dqkv = jnp.concatenate([_from_heads(t, B, S) for t in (dq, dk, dv)], axis=1)
    dh1a = matmul(dqkv, W_qkv, tb=True, b_chips=True, name="att_qkv_dx")
    dW_qkv = matmul(h1a, dqkv, ta=True, out_chips=True, name="att_qkv_dw")
    grad_x, dsh1, dsc1, dnm0 = norm_mod_bwd(dh1a, x0, dx1, norm_mix[0], sc1, B=B, S=S, name="att_norm_bwd")

    dmod_rows = jnp.concatenate([dsh1, dsc1, dg1, gf0["dsh2"], gf0["dsc2"], gf0["dg2"],
                                 dsh1s, dsc1s, dg1s, gf1["dsh2"], gf1["dsc2"], gf1["dg2"], dsh_f, dsc_f], axis=1)
    dmod_all = all_gather8(dmod_rows, name="gather_dmod").reshape(BG, 14 * D)
    grad_w_mod = jnp.stack([
        matmul(c_act, lax.dynamic_slice(dmod_all, (0, i * 6 * D + chip * mod_w), (BG, mod_w)), ta=True,
               name=f"mod_dw_{i}") for i in range(2)])
    grad_w_fin = matmul(c_act, lax.dynamic_slice(dmod_all, (0, 12 * D + chip * fin_w), (BG, fin_w)), ta=True,
                        name="fin_dw")

    parts = [jnp.concatenate([jnp.sum(dnm0, axis=0), jnp.sum(dnm1, axis=0)]),
             jnp.concatenate([gf0["dnorm_ffn"], gf1["dnorm_ffn"]]),
             jnp.sum(dmod_rows[:, :12 * D], axis=0),
             dlr.reshape(-1), dli.reshape(-1), dbbr.reshape(-1), dbbi.reshape(-1), dc_re.reshape(-1), dc_im.reshape(-1),
             dd_skip, jnp.sum(db_glu, axis=0),
             gf0["dconv_w"].reshape(-1), gf1["dconv_w"].reshape(-1), gf0["dconv_b"], gf1["dconv_b"],
             jnp.sum(dnorm_out, axis=0), jnp.sum(dmod_rows[:, 12 * D:], axis=0)]
    sizes = [int(p.shape[0]) for p in parts]
    flat = jnp.concatenate(parts)
    width = 1024
    quantum = N_CHIPS * 16 * width
    padded = -(-flat.shape[0] // quantum) * quantum
    small_cm = jnp.pad(flat, (0, padded - flat.shape[0])).reshape(N_CHIPS, -1, width)

    g_rows_cm = jnp.concatenate([dW_o_attn.reshape(N_CHIPS, Dq, D), dW_in.reshape(N_CHIPS, Dq, D),
                                 dW_glu.reshape(N_CHIPS, Dq, D), dW_o_ssm.reshape(N_CHIPS, Dq, D),
                                 gf0["dW_down"].reshape(N_CHIPS, Fq, D), gf1["dW_down"].reshape(N_CHIPS, Fq, D)], axis=1)
    r_qkv, r_rows, r_up0, r_up1, r_small = reduce_scatter_chips(
        [dW_qkv, g_rows_cm, gf0["dW_up"], gf1["dW_up"], small_cm], [BF16, BF16, BF16, BF16, F32])
    grad_w_qkv = r_qkv[None]
    grad_w_o_attn = r_rows[0 * Dq:1 * Dq][None]
    grad_w_in_ssm = r_rows[1 * Dq:2 * Dq][None]
    grad_w_glu = r_rows[2 * Dq:3 * Dq][None]
    grad_w_o_ssm = r_rows[3 * Dq:4 * Dq][None]
    grad_w_down = r_rows[4 * Dq:].reshape(2, Fq, D)
    grad_w_up = jnp.stack([r_up0, r_up1])
    summed = all_gather8(r_small, name="gather_small_grads")[::2].reshape(-1)
    offs = [0]
    for s_ in sizes:
        offs.append(offs[-1] + s_)
    (s_nm, s_nf, s_bmod, s_lr, s_li, s_bbr, s_bbi, s_cre, s_cim, s_dsk, s_bglu, s_cw0, s_cw1, s_cb0, s_cb1, s_no,
     s_bfin) = [summed[offs[i]:offs[i + 1]] for i in range(len(sizes))]
    _, disc_vjp = jax.vjp(_ssm_discretize, a_re[0], a_im[0], log_dt[0], b_re[0], b_im[0])
    ga_re, ga_im, glog_dt, gb_re, gb_im = disc_vjp((s_lr.reshape(G, P), s_li.reshape(G, P), s_bbr.reshape(G, P, H),
                                                    s_bbi.reshape(G, P, H)))
    grad_norm_mix = s_nm.reshape(2, D)
    grad_norm_ffn = s_nf.reshape(2, D)
    grad_b_mod = s_bmod.reshape(2, 6 * D)
    grad_c_re = s_cre.reshape(1, G, H, P)
    grad_c_im = s_cim.reshape(1, G, H, P)
    grad_d_skip = lax.dynamic_slice(s_dsk, (chip * Dq,), (Dq,)).reshape(1, Dq)
    grad_b_glu = lax.dynamic_slice(s_bglu, (chip * Dq,), (Dq,)).reshape(1, Dq)
    cw_full = jnp.stack([s_cw0.reshape(3, F2), s_cw1.reshape(3, F2)])
    grad_conv_w = lax.dynamic_slice(cw_full, (0, 0, chip * (F2 // N_CHIPS)), (2, 3, F2 // N_CHIPS))
    grad_conv_b = jnp.stack([s_cb0, s_cb1])
    grad_norm_out = s_no
    grad_b_fin = s_bfin

    grads = dict(
        norm_mix=grad_norm_mix, norm_ffn=grad_norm_ffn, w_mod=grad_w_mod, b_mod=grad_b_mod, w_qkv=grad_w_qkv,
        w_o_attn=grad_w_o_attn, w_in_ssm=grad_w_in_ssm, a_re=ga_re[None], a_im=ga_im[None], log_dt=glog_dt[None],
        b_re=gb_re[None], b_im=gb_im[None], c_re=grad_c_re, c_im=grad_c_im, d_skip=grad_d_skip, w_glu=grad_w_glu,
        b_glu=grad_b_glu, w_o_ssm=grad_w_o_ssm, w_up=grad_w_up, conv_w=grad_conv_w, conv_b=grad_conv_b,
        w_down=grad_w_down, norm_out=grad_norm_out, w_fin=grad_w_fin, b_fin=grad_b_fin)
    weights = dict(
        norm_mix=norm_mix, norm_ffn=norm_ffn, w_mod=w_mod, b_mod=b_mod, w_qkv=w_qkv, w_o_attn=w_o_attn,
        w_in_ssm=w_in_ssm, a_re=a_re, a_im=a_im, log_dt=log_dt, b_re=b_re, b_im=b_im, c_re=c_re, c_im=c_im,
        d_skip=d_skip, w_glu=w_glu, b_glu=b_glu, w_o_ssm=w_o_ssm, w_up=w_up, conv_w=conv_w, conv_b=conv_b,
        w_down=w_down, norm_out=norm_out, w_fin=w_fin, b_fin=b_fin)
    m_in = dict(
        norm_mix=m_norm_mix, norm_ffn=m_norm_ffn, w_mod=m_w_mod, b_mod=m_b_mod, w_qkv=m_w_qkv, w_o_attn=m_w_o_attn,
        w_in_ssm=m_w_in_ssm, a_re=m_a_re, a_im=m_a_im, log_dt=m_log_dt, b_re=m_b_re, b_im=m_b_im, c_re=m_c_re,
        c_im=m_c_im, d_skip=m_d_skip, w_glu=m_w_glu, b_glu=m_b_glu, w_o_ssm=m_w_o_ssm, w_up=m_w_up, conv_w=m_conv_w,
        conv_b=m_conv_b, w_down=m_w_down, norm_out=m_norm_out, w_fin=m_w_fin, b_fin=m_b_fin)
    v_in = dict(
        norm_mix=v_norm_mix, norm_ffn=v_norm_ffn, w_mod=v_w_mod, b_mod=v_b_mod, w_qkv=v_w_qkv, w_o_attn=v_w_o_attn,
        w_in_ssm=v_w_in_ssm, a_re=v_a_re, a_im=v_a_im, log_dt=v_log_dt, b_re=v_b_re, b_im=v_b_im, c_re=v_c_re,
        c_im=v_c_im, d_skip=v_d_skip, w_glu=v_w_glu, b_glu=v_b_glu, w_o_ssm=v_w_o_ssm, w_up=v_w_up, conv_w=v_conv_w,
        conv_b=v_conv_b, w_down=v_w_down, norm_out=v_norm_out, w_fin=v_w_fin, b_fin=v_b_fin)
    names = list(weights)
    for n_ in names:
        grads[n_] = grads[n_].reshape(weights[n_].shape)

    big = ("w_mod", "w_qkv", "w_o_attn", "w_in_ssm", "w_glu", "w_o_ssm", "w_up", "w_down", "w_fin")
    delta, new_m, new_v = {}, {}, {}
    for n_ in big:
        shp = weights[n_].shape
        two_d = lambda a: a.reshape(-1, shp[-1])
        d_, m_, v_ = adamw(two_d(weights[n_]), two_d(grads[n_]), two_d(m_in[n_]), two_d(v_in[n_]), name=f"adamw_{n_}")
        delta[n_], new_m[n_], new_v[n_] = d_.reshape(shp), m_.reshape(shp), v_.reshape(shp)
    rest = [n_ for n_ in names if n_ not in big]

    def pack(tree):
        f = jnp.concatenate([tree[n_].reshape(-1) for n_ in rest])
        pad_to = -(-f.shape[0] // (8 * width)) * (8 * width)
        return jnp.pad(f, (0, pad_to - f.shape[0]), constant_values=1.0).reshape(-1, width)

    d_, m_, v_ = adamw(pack(weights), pack(grads), pack(m_in), pack(v_in), name="adamw_small")
    off = 0
    for n_ in rest:
        sz = int(math.prod(weights[n_].shape))
        shp = weights[n_].shape
        delta[n_] = d_.reshape(-1)[off:off + sz].reshape(shp)
        new_m[n_] = m_.reshape(-1)[off:off + sz].reshape(shp)
        new_v[n_] = v_.reshape(-1)[off:off + sz].reshape(shp)
        off += sz

    return (loss, grad_x.reshape(B, S, D), *[grads[n_] for n_ in names], *[delta[n_] for n_ in names],
            *[new_m[n_] for n_ in names], *[new_v[n_] for n_ in names])
```

```python
import functools
import math

import jax
import jax.numpy as jnp
from jax import lax
from jax.experimental import pallas as pl
from jax.experimental.pallas import tpu as pltpu

F32 = jnp.float32
BF16 = jnp.bfloat16
MESH = pl.DeviceIdType.MESH

HEAD_DIM = 64
SSM_GROUP = 16
STATE = 64
GROUPS_PER_BLOCK = 8
SEGMENTS = 16
SCAN_UNROLL = 4
EPS = 1e-6
ADAM_LR = 0.001
ADAM_B1 = 0.9
ADAM_B2 = 0.999
ADAM_EPS = 1e-08
ADAM_WD = 0.01
ADAM_STEP = 10
N_CHIPS = 4
N_DEV = 8
V7X_VMEM_LIMIT = 56 * 1024 * 1024
ATT_BLOCK = 128
ATT_HEADS = 8
ATT_HEADS_BWD = 4


def _tile(n, prefs):
    for p in prefs:
        if n % p == 0:
            return p
    return n


def _params(sem, vmem=V7X_VMEM_LIMIT):
    return pltpu.CompilerParams(dimension_semantics=sem, vmem_limit_bytes=vmem)


def matmul(a, b, *, ta=False, tb=False, bias=None, out_dtype=F32, b_chips=False, out_chips=False, name):
    a_parts = a.shape[0] if a.ndim == 3 else 1
    if a_parts > 1:
        assert not ta
        M, K = a.shape[1], a_parts * a.shape[2]
    elif ta:
        K, M = a.shape
    else:
        M, K = a.shape
    b_parts = b.shape[0] if b_chips else 1
    b_rows, b_cols = (b.shape[1], b_parts * b.shape[2]) if b_chips else b.shape
    if tb:
        N, Kb = b_rows, b_cols
    else:
        Kb, N = b_rows, b_cols
    assert K == Kb, (a.shape, b.shape, ta, tb)
    n_cut = N // max(N_CHIPS if out_chips else 1, b_parts if not tb else 1)
    k_cut = K // max(b_parts if tb else 1, a_parts)
    tm = _tile(M, (1024, 512, 256, 128))
    tn = _tile(n_cut, (1024, 1408, 768, 512, 256, 128))
    tk = k_cut if k_cut <= 2816 else _tile(k_cut, (1024, 512, 256, 128))
    nk = K // tk
    npc = N // N_CHIPS // tn
    npb = N // b_parts // tn
    kpb = K // b_parts // tk
    kpa = K // a_parts // tk
    dims = (((0,) if ta else (1,), (1,) if tb else (0,)), ((), ()))

    def body(*refs):
        a_ref, b_ref = refs[:2]
        bias_ref = refs[2] if bias is not None else None
        o_ref = refs[-2] if nk > 1 else refs[-1]

        def finish(r):
            if bias_ref is not None:
                r = r + bias_ref[...]
            o_ref[...] = r.astype(o_ref.dtype)

        prod = lax.dot_general(a_ref[...].astype(BF16), b_ref[...].astype(BF16), dims, preferred_element_type=F32)
        if nk == 1:
            finish(prod)
            return
        acc_ref = refs[-1]
        k = pl.program_id(2)

        @pl.when(k == 0)
        def _():
            acc_ref[...] = prod

        @pl.when(k > 0)
        def _():
            acc_ref[...] += prod

        @pl.when(k == nk - 1)
        def _():
            finish(acc_ref[...])

    if a_parts > 1:
        a_spec = pl.BlockSpec((None, tm, tk), lambda i, j, k: (lax.div(k, kpa), i, lax.rem(k, kpa)))
    else:
        a_spec = pl.BlockSpec((tk, tm), lambda i, j, k: (k, i)) if ta else pl.BlockSpec((tm, tk), lambda i, j, k: (i, k))
    if not b_chips:
        b_spec = pl.BlockSpec((tn, tk), lambda i, j, k: (j, k)) if tb else pl.BlockSpec((tk, tn), lambda i, j, k: (k, j))
    elif tb:
        b_spec = pl.BlockSpec((None, tn, tk), lambda i, j, k: (lax.div(k, kpb), j, lax.rem(k, kpb)))
    else:
        b_spec = pl.BlockSpec((None, tk, tn), lambda i, j, k: (lax.div(j, npb), k, lax.rem(j, npb)))
    in_specs = [a_spec, b_spec]
    args = [a, b]
    if bias is not None:
        in_specs.append(pl.BlockSpec((1, tn), lambda i, j, k: (0, j)))
        args.append(bias.reshape(1, N).astype(F32))
    if out_chips:
        out_shape = jax.ShapeDtypeStruct((N_CHIPS, M, N // N_CHIPS), out_dtype)
        out_spec = pl.BlockSpec((None, tm, tn), lambda i, j, k: (lax.div(j, npc), i, lax.rem(j, npc)))
    else:
        out_shape = jax.ShapeDtypeStruct((M, N), out_dtype)
        out_spec = pl.BlockSpec((tm, tn), lambda i, j, k: (i, j))
    return pl.pallas_call(
        body, name=name,
        out_shape=out_shape,
        grid=(M // tm, N // tn, nk),
        in_specs=in_specs,
        out_specs=out_spec,
        scratch_shapes=[pltpu.VMEM((tm, tn), F32)] if nk > 1 else [],
        compiler_params=_params(("parallel", "parallel", "arbitrary")),
    )(*args)


def rowwise(fn, tiled, per_seq, glob, out_tiled, out_seq, *, B, S, name, rows=512):
    tm = _tile(S, (rows, 128, 64, 32, 16, 8))
    nt = S // tm
    n_in = len(tiled) + len(per_seq) + len(glob)
    n_ot = len(out_tiled)

    def body(*refs):
        ins = refs[:n_in]
        outs = refs[n_in:]
        vals = fn(*[r[...] for r in ins])
        if not isinstance(vals, (tuple, list)):
            vals = (vals,)
        assert len(vals) == len(outs), (name, len(vals), len(outs))
        for o_ref, v in zip(outs[:n_ot], vals[:n_ot]):
            o_ref[...] = v.astype(o_ref.dtype)
        t = pl.program_id(1)
        for o_ref, v in zip(outs[n_ot:], vals[n_ot:]):
            def first(o_ref=o_ref, v=v):
                o_ref[...] = v.astype(F32)

            def later(o_ref=o_ref, v=v):
                o_ref[...] += v.astype(F32)

            pl.when(t == 0)(first)
            pl.when(t > 0)(later)

    in_specs = [pl.BlockSpec((tm, a.shape[1]), lambda b, t: (b * nt + t, 0)) for a in tiled]
    in_specs += [pl.BlockSpec((None, 1, a.shape[1]), lambda b, t: (b, 0, 0)) for a in per_seq]
    in_specs += [pl.BlockSpec(a.shape, lambda b, t: (0,) * a.ndim) for a in glob]
    out_shape = [jax.ShapeDtypeStruct((B * S, w), dt) for w, dt in out_tiled]
    out_shape += [jax.ShapeDtypeStruct((B, 1, w), F32) for w in out_seq]
    out_specs = [pl.BlockSpec((tm, w), lambda b, t: (b * nt + t, 0)) for w, _ in out_tiled]
    out_specs += [pl.BlockSpec((None, 1, w), lambda b, t: (b, 0, 0)) for w in out_seq]
    res = pl.pallas_call(
        body, name=name, out_shape=out_shape, grid=(B, nt), in_specs=in_specs, out_specs=out_specs,
        compiler_params=_params(("parallel", "arbitrary")),
    )(*tiled, *[a.reshape(B, 1, a.shape[1]) for a in per_seq], *glob)
    res = list(res)
    for i in range(n_ot, len(res)):
        res[i] = res[i].reshape(B, res[i].shape[-1])
    return res


def _rms(x):
    r = lax.rsqrt(jnp.mean(x * x, axis=-1, keepdims=True) + EPS)
    return x * r, r


def norm_mod_fwd(x, g, sh, sc, *, B, S, name):
    def fn(x, sh, sc, g):
        xn, _ = _rms(x)
        return (xn * g) * (1.0 + sc) + sh

    return rowwise(fn, [x], [sh, sc], [g.reshape(1, -1)], [(x.shape[1], BF16)], [], B=B, S=S, name=name)[0]


def _norm_mod_bwd_math(dh, x, sc, g):
    xn, r = _rms(x)
    y = xn * g
    dy = dh * (1.0 + sc)
    dxn = dy * g
    dx = r * (dxn - xn * jnp.mean(dxn * xn, axis=-1, keepdims=True))
    dsh = jnp.sum(dh, axis=0, keepdims=True)
    dsc = jnp.sum(dh * y, axis=0, keepdims=True)
    dg = jnp.sum(dy * xn, axis=0, keepdims=True)
    return dx, dsh, dsc, dg


def norm_mod_bwd(dh, x, dres, g, sc, *, B, S, name):
    D = x.shape[1]

    def fn(dh, x, dres, sc, g):
        dx, dsh, dsc, dg = _norm_mod_bwd_math(dh.astype(F32), x, sc, g)
        return dres + dx, dsh, dsc, dg

    return rowwise(fn, [dh, x, dres], [sc], [g.reshape(1, -1)], [(D, F32)], [D, D, D], B=B, S=S, name=name)


def gate_res_fwd(x, y, gate, *, B, S, name):
    return rowwise(lambda x, y, g: x + g * y, [x, y], [gate], [], [(x.shape[1], F32)], [], B=B, S=S, name=name)[0]


def gate_res_bwd(dx, y, gate, *, B, S, name):
    D = dx.shape[1]

    def fn(dx, y, g):
        return g * dx, jnp.sum(dx * y, axis=0, keepdims=True)

    return rowwise(fn, [dx, y], [gate], [], [(D, BF16)], [D], B=B, S=S, name=name)


def final_loss(x, tgt, g, sh, sc, *, B, S, name):
    D = x.shape[1]

    def fn(x, tgt, sh, sc, g):
        xn, _ = _rms(x)
        y = (xn * g) * (1.0 + sc) + sh
        err = y - tgt
        loss = 0.5 * jnp.sum(err * err, axis=0, keepdims=True) * (1.0 / D)
        dx, dsh, dsc, dg = _norm_mod_bwd_math(err * (1.0 / D), x, sc, g)
        return dx, loss, dsh, dsc, dg

    return rowwise(fn, [x, tgt], [sh, sc], [g.reshape(1, -1)], [(D, F32)], [D, D, D, D], B=B, S=S, name=name)


def _gelu(y):
    c0 = math.sqrt(2.0 / math.pi)
    t = jnp.tanh(c0 * (y + 0.044715 * (y * y * y)))
    return 0.5 * y * (1.0 + t), t


def _sigmoid(s):
    return 1.0 / (1.0 + jnp.exp(-s))


def gelu_fwd(y, *, B, S, name):
    return rowwise(lambda y: _gelu(y)[0], [y], [], [], [(y.shape[1], BF16)], [], B=B, S=S, name=name)[0]


def glu_fwd(y, s, *, B, S, name):
    return rowwise(lambda y, s: _gelu(y)[0] * _sigmoid(s), [y, s], [], [], [(y.shape[1], BF16)], [], B=B, S=S,
                   name=name)[0]


def glu_bwd1(y, s, dg, *, B, S, name):
    D = y.shape[1]

    def fn(y, s, dg):
        z = _gelu(y)[0]
        sig = _sigmoid(s)
        ds = dg * z * sig * (1.0 - sig)
        return ds, dg * sig, jnp.sum(ds, axis=0, keepdims=True)

    return rowwise(fn, [y, s, dg], [], [], [(D, BF16), (D, F32)], [D], B=B, S=S, name=name)


def glu_bwd2(y, dz1, dz2, *, B, S, name):
    D = y.shape[1]
    c0 = math.sqrt(2.0 / math.pi)

    def fn(y, dz1, dz2):
        _, t = _gelu(y)
        dgelu = 0.5 * (1.0 + t) + 0.5 * y * (1.0 - t * t) * c0 * (1.0 + 3.0 * 0.044715 * y * y)
        return (dz1 + dz2) * dgelu

    return rowwise(fn, [y, dz1, dz2], [], [], [(D, F32)], [], B=B, S=S, name=name)[0]


def silu_rows(c, *, name):
    R, W = c.shape
    return rowwise(lambda c: c * _sigmoid(c), [c], [], [], [(W, F32)], [], B=1, S=R, name=name)[0]


def _shift_down(cur, h6, h7):
    rows = lax.broadcasted_iota(jnp.int32, cur.shape, 0)
    m1 = jnp.where(rows == 0, h7, pltpu.roll(cur, 1, 0))
    m2 = jnp.where(rows == 0, h6, jnp.where(rows == 1, h7, pltpu.roll(cur, 2, 0)))
    return m1, m2


def _conv3(cur, halo_ref, w_ref, has_prev):
    h6 = jnp.where(has_prev, halo_ref[6:7, :], 0.0)
    h7 = jnp.where(has_prev, halo_ref[7:8, :], 0.0)
    m1, m2 = _shift_down(cur, h6, h7)
    return w_ref[2:3, :] * cur + w_ref[1:2, :] * m1 + w_ref[0:1, :] * m2, m1, m2


def _conv_tiles(S, F):
    ts = _tile(S, (1024, 512, 256, 128, 64, 32, 16, 8))
    tn = _tile(F, (256, 128))
    return ts, tn, S // ts, F // tn


def conv_gate_fwd(up, cw, cb, *, B, S, name):
    F = up.shape[1] // 2
    ts, tn, nts, nF = _conv_tiles(S, F)
    hb = ts // 8

    def body(g_ref, gh_ref, v_ref, vh_ref, wg_ref, wv_ref, bg_ref, bv_ref, o_ref):
        has_prev = pl.program_id(2) > 0
        gc = _conv3(g_ref[...], gh_ref, wg_ref, has_prev)[0] + bg_ref[...]
        vc = _conv3(v_ref[...], vh_ref, wv_ref, has_prev)[0] + bv_ref[...]
        o_ref[...] = (gc * _sigmoid(gc) * vc).astype(o_ref.dtype)

    def cur(off):
        return pl.BlockSpec((ts, tn), lambda b, j, t: (b * nts + t, j + off))

    def halo(off):
        return pl.BlockSpec((8, tn), lambda b, j, t: (jnp.maximum((b * nts + t) * hb - 1, 0), j + off))

    def vec(rows, off):
        return pl.BlockSpec((rows, tn), lambda b, j, t: (0, j + off))

    return pl.pallas_call(
        body, name=name, out_shape=jax.ShapeDtypeStruct((B * S, F), BF16), grid=(B, nF, nts),
        in_specs=[cur(0), halo(0), cur(nF), halo(nF), vec(3, 0), vec(3, nF), vec(1, 0), vec(1, nF)],
        out_specs=pl.BlockSpec((ts, tn), lambda b, j, t: (b * nts + t, j)),
        compiler_params=_params(("parallel", "parallel", "arbitrary")),
    )(up, up, up, up, cw, cw, cb, cb)


def conv_gate_bwd1(up, dact, cw, cb, *, B, S, name):
    F = up.shape[1] // 2
    ts, tn, nts, nF = _conv_tiles(S, F)
    hb = ts // 8

    def body(g_ref, gh_ref, v_ref, vh_ref, da_ref, wg_ref, wv_ref, bg_ref, bv_ref, d_ref, db_ref):
        t = pl.program_id(2)
        has_prev = t > 0
        gc = _conv3(g_ref[...], gh_ref, wg_ref, has_prev)[0] + bg_ref[...]
        vc = _conv3(v_ref[...], vh_ref, wv_ref, has_prev)[0] + bv_ref[...]
        sig = _sigmoid(gc)
        da = da_ref[...]
        dg = da * vc * (sig * (1.0 + gc * (1.0 - sig)))
        dv = da * (gc * sig)
        d_ref[0] = dg
        d_ref[1] = dv
        part = jnp.concatenate([jnp.sum(dg, axis=0, keepdims=True), jnp.sum(dv, axis=0, keepdims=True)], axis=0)

        @pl.when(t == 0)
        def _():
            db_ref[...] = part

        @pl.when(t > 0)
        def _():
            db_ref[...] += part

    def cur(off):
        return pl.BlockSpec((ts, tn), lambda b, j, t: (b * nts + t, j + off))

    def halo(off):
        return pl.BlockSpec((8, tn), lambda b, j, t: (jnp.maximum((b * nts + t) * hb - 1, 0), j + off))

    def vec(rows, off):
        return pl.BlockSpec((rows, tn), lambda b, j, t: (0, j + off))

    return pl.pallas_call(
        body, name=name,
        out_shape=[jax.ShapeDtypeStruct((2, B * S, F), F32), jax.ShapeDtypeStruct((B, 2, F), F32)],
        grid=(B, nF, nts),
        in_specs=[cur(0), halo(0), cur(nF), halo(nF), cur(0), vec(3, 0), vec(3, nF), vec(1, 0), vec(1, nF)],
        out_specs=[pl.BlockSpec((2, ts, tn), lambda b, j, t: (0, b * nts + t, j)),
                   pl.BlockSpec((None, 2, tn), lambda b, j, t: (b, 0, j))],
        compiler_params=_params(("parallel", "parallel", "arbitrary")),
    )(up, up, up, up, dact, cw, cw, cb, cb)


def conv_bwd2(d3, up, cw, *, B, S, name):
    F = up.shape[1] // 2
    ts, tn, nts, nF = _conv_tiles(S, F)
    hb = ts // 8
    last8 = B * S // 8 - 1

    def body(d_ref, da_ref, u_ref, uh_ref, w_ref, o_ref, dw_ref):
        t = pl.program_id(3)
        d = d_ref[...]
        has_next = t < nts - 1
        a0 = jnp.where(has_next, da_ref[0:1, :], 0.0)
        a1 = jnp.where(has_next, da_ref[1:2, :], 0.0)
        rows = lax.broadcasted_iota(jnp.int32, d.shape, 0)
        p1 = jnp.where(rows == ts - 1, a0, pltpu.roll(d, ts - 1, 0))
        p2 = jnp.where(rows == ts - 1, a1, jnp.where(rows == ts - 2, a0, pltpu.roll(d, ts - 2, 0)))
        o_ref[...] = (w_ref[2:3, :] * d + w_ref[1:2, :] * p1 + w_ref[0:1, :] * p2).astype(o_ref.dtype)
        u = u_ref[...]
        has_prev = t > 0
        h6 = jnp.where(has_prev, uh_ref[6:7, :], 0.0)
        h7 = jnp.where(has_prev, uh_ref[7:8, :], 0.0)
        m1, m2 = _shift_down(u, h6, h7)
        part = jnp.concatenate([jnp.sum(d * m2, axis=0, keepdims=True), jnp.sum(d * m1, axis=0, keepdims=True),
                                jnp.sum(d * u, axis=0, keepdims=True)], axis=0)

        @pl.when(t == 0)
        def _():
            dw_ref[...] = part

        @pl.when(t > 0)
        def _():
            dw_ref[...] += part

    return pl.pallas_call(
        body, name=name,
        out_shape=[jax.ShapeDtypeStruct((B * S, 2 * F), BF16), jax.ShapeDtypeStruct((B, 3, 2 * F), F32)],
        grid=(B, 2, nF, nts),
        in_specs=[
            pl.BlockSpec((None, ts, tn), lambda b, g, j, t: (g, b * nts + t, j)),
            pl.BlockSpec((None, 8, tn), lambda b, g, j, t: (g, jnp.minimum((b * nts + t + 1) * hb, last8), j)),
            pl.BlockSpec((ts, tn), lambda b, g, j, t: (b * nts + t, g * nF + j)),
            pl.BlockSpec((8, tn), lambda b, g, j, t: (jnp.maximum((b * nts + t) * hb - 1, 0), g * nF + j)),
            pl.BlockSpec((3, tn), lambda b, g, j, t: (0, g * nF + j)),
        ],
        out_specs=[pl.BlockSpec((ts, tn), lambda b, g, j, t: (b * nts + t, g * nF + j)),
                   pl.BlockSpec((None, 3, tn), lambda b, g, j, t: (b, 0, g * nF + j))],
        compiler_params=_params(("parallel", "parallel", "parallel", "arbitrary")),
    )(d3, d3, up, up, cw)


def conv_gate_bwd(up, dact, cw, cb, *, B, S, name):
    F = up.shape[1] // 2
    tn = _tile(F, (256, 128))
    nF = F // tn

    def body(g_ref, v_ref, da_ref, wg_ref, wv_ref, bg_ref, bv_ref, o_ref, dw_ref, db_ref):
        rows = lax.broadcasted_iota(jnp.int32, (S, tn), 0)

        def earlier(x, k):
            return jnp.where(rows >= k, pltpu.roll(x, k, 0), 0.0)

        def later(x, k):
            return jnp.where(rows < S - k, pltpu.roll(x, S - k, 0), 0.0)

        def conv(x, w_ref):
            x1, x2 = earlier(x, 1), earlier(x, 2)
            return w_ref[2:3, :] * x + w_ref[1:2, :] * x1 + w_ref[0:1, :] * x2, x1, x2

        def back(d, x, x1, x2, w_ref, half):
            o_ref[half] = (w_ref[2:3, :] * d + w_ref[1:2, :] * later(d, 1) + w_ref[0:1, :] * later(d, 2)
                           ).astype(o_ref.dtype)
            dw_ref[half] = jnp.concatenate([jnp.sum(d * x2, axis=0, keepdims=True),
                                            jnp.sum(d * x1, axis=0, keepdims=True),
                                            jnp.sum(d * x, axis=0, keepdims=True)], axis=0)
            return jnp.sum(d, axis=0, keepdims=True)

        g, v, da = g_ref[...], v_ref[...], da_ref[...]
        gc, g1, g2 = conv(g, wg_ref)
        vc, v1, v2 = conv(v, wv_ref)
        gc = gc + bg_ref[...]
        vc = vc + bv_ref[...]
        sig = _sigmoid(gc)
        dg = da * vc * (sig * (1.0 + gc * (1.0 - sig)))
        dv = da * (gc * sig)
        db_ref[...] = jnp.concatenate([back(dg, g, g1, g2, wg_ref, 0), back(dv, v, v1, v2, wv_ref, 1)], axis=0)

    def cols(off):
        return pl.BlockSpec((S, tn), lambda b, j: (b, j + off))

    def vec(rows, off):
        return pl.BlockSpec((rows, tn), lambda b, j: (0, j + off))

    return pl.pallas_call(
        body, name=name,
        out_shape=[jax.ShapeDtypeStruct((2, B * S, F), BF16), jax.ShapeDtypeStruct((B, 2, 3, F), F32),
                   jax.ShapeDtypeStruct((B, 2, F), F32)],
        grid=(B, nF),
        in_specs=[cols(0), cols(nF), cols(0), vec(3, 0), vec(3, nF), vec(1, 0), vec(1, nF)],
        out_specs=[pl.BlockSpec((2, S, tn), lambda b, j: (0, b, j)),
                   pl.BlockSpec((None, 2, 3, tn), lambda b, j: (b, 0, 0, j)),
                   pl.BlockSpec((None, 2, tn), lambda b, j: (b, 0, j))],
        compiler_params=_params(("parallel", "parallel")),
    )(up, up, dact, cw, cw, cb, cb)


MASKED_LOG = -1e30


def _split2(x):
    bits = lax.bitcast_convert_type(x, jnp.uint32) & jnp.uint32(0xFFFF0000)
    hi = lax.bitcast_convert_type(bits, F32)
    return hi.astype(BF16), (x - hi).astype(BF16)


def _split_dot(x, m):
    hi, lo = _split2(x)
    return jnp.dot(hi, m, preferred_element_type=F32) + jnp.dot(lo, m, preferred_element_type=F32)


def _nt(a, b):
    return lax.dot_general(a, b, (((1,), (1,)), ((), ())), preferred_element_type=F32)


def _tn(a, b):
    return lax.dot_general(a, b, (((0,), (0,)), ((), ())), preferred_element_type=F32)


def _att_scores(q, k, mask, prescaled=False):
    z = _nt(q, k)
    if not prescaled:
        z = z * (HEAD_DIM ** -0.5)
    e = jnp.exp(-jnp.abs(z))
    sp = jnp.log(1.0 + e)
    lb = jnp.minimum(z, 0.0) - sp
    l1 = lb - z
    if mask is not None:
        lb = jnp.where(mask, lb, MASKED_LOG)
        l1 = jnp.where(mask, l1, 0.0)
    return z, lb, l1, e


def _col_to_row(col, eye):
    return jnp.sum(jnp.where(eye, col, 0.0), axis=0, keepdims=True)


def _row_to_col(row, eye):
    return jnp.sum(jnp.where(eye, row, 0.0), axis=1, keepdims=True)


def attn_fwd(q, k, v, *, name):
    B, H, S, dh = q.shape
    T = ATT_BLOCK
    nq = S // T

    G = _tile(H, (ATT_HEADS, 2))

    def body(q_ref, k_ref, v_ref, o_ref, l_ref):
        r = lax.broadcasted_iota(jnp.int32, (T, T), 0)
        c = lax.broadcasted_iota(jnp.int32, (T, T), 1)
        later = (r > c).astype(BF16)
        eye = r == c
        diag = c < r
        blk = lax.broadcasted_iota(jnp.int32, (nq, T), 0)

        later2 = jnp.concatenate([later, later], axis=0)

        def scores(g, qb, k0, mask):
            _, lb, l1, _ = _att_scores(qb, k_ref[g, pl.ds(k0, T), :], mask)
            return lb, jnp.concatenate(_split2(l1), axis=1), jnp.sum(l1, axis=1, keepdims=True)

        def weigh_all(k0, sc, st):
            suf = jnp.dot(jnp.concatenate([s[1] for s in sc], axis=0), later2, preferred_element_type=F32)
            out = []
            for g in range(G):
                lb, _, rowsum = sc[g]
                acc, run = st[g]
                w = jnp.exp(lb + suf[g * T:(g + 1) * T] + run)
                acc = acc + jnp.dot(w.astype(BF16), v_ref[g, pl.ds(k0, T), :], preferred_element_type=F32)
                out.append((acc, run + rowsum))
            return tuple(out)

        def qblock(i, totals):
            q0 = pl.multiple_of(i * T, T)
            qbs = [q_ref[g, pl.ds(q0, T), :] for g in range(G)]
            sc0 = tuple(scores(g, qbs[g], q0, diag) for g in range(G))
            st0 = tuple((jnp.zeros((T, dh), F32), jnp.zeros((T, 1), F32)) for _ in range(G))

            def kblock(jj, carry):
                sc, st = carry
                k_next = pl.multiple_of((i - jj) * T, T)
                k_cur = pl.multiple_of((i - jj + 1) * T, T)
                st = weigh_all(k_cur, sc, st)
                sc_next = tuple(scores(g, qbs[g], k_next, None) for g in range(G))
                return sc_next, st

            sc, st = lax.fori_loop(1, i + 1, kblock, (sc0, st0))
            st = weigh_all(0, sc, st)
            for g in range(G):
                o_ref[g, pl.ds(q0, T), :] = st[g][0]
            return tuple(jnp.where(blk == i, _col_to_row(st[g][1], eye), totals[g]) for g in range(G))

        totals = lax.fori_loop(0, nq, qblock, tuple(jnp.zeros((nq, T), F32) for _ in range(G)))
        for g in range(G):
            l_ref[g] = totals[g]

    spec = pl.BlockSpec((None, G, S, dh), lambda b, h: (b, h, 0, 0))
    lspec = pl.BlockSpec((None, G, nq, T), lambda b, h: (b, h, 0, 0))
    return pl.pallas_call(
        body, name=name,
        out_shape=[jax.ShapeDtypeStruct((B, H, S, dh), F32), jax.ShapeDtypeStruct((B, H, nq, T), F32)],
        grid=(B, H // G), in_specs=[spec, spec, spec], out_specs=[spec, lspec],
        compiler_params=_params(("parallel", "parallel")),
    )(q, k, v)


def attn_bwd(q, k, v, ltot, do, *, name):
    B, H, S, dh = q.shape
    T = ATT_BLOCK
    nq = S // T
    scale = HEAD_DIM ** -0.5

    G = _tile(H, (ATT_HEADS_BWD, 2))

    def body(q_ref, k_ref, v_ref, l_ref, do_ref, dq_ref, dk_ref, dv_ref, dk_acc, dv_acc):
        r = lax.broadcasted_iota(jnp.int32, (T, T), 0)
        c = lax.broadcasted_iota(jnp.int32, (T, T), 1)
        upto = (r <= c).astype(BF16)
        before = (r < c).astype(BF16)
        upto2 = jnp.concatenate([upto, upto], axis=0)
        before2 = jnp.concatenate([before, before], axis=0)
        eye = r == c
        diag = c < r
        blk = lax.broadcasted_iota(jnp.int32, (nq, T), 0)
        dk_acc[...] = jnp.zeros_like(dk_acc)
        dv_acc[...] = jnp.zeros_like(dv_acc)

        def scores(g, qb, dob, k0, mask):
            z, lb, l1, e = _att_scores(qb, k_ref[g, pl.ds(k0, T), :], mask)
            inv = 1.0 / (1.0 + e)
            small = e * inv
            pos = z >= 0.0
            beta = jnp.where(pos, inv, small)
            omb = jnp.where(pos, small, inv)
            if mask is not None:
                beta = jnp.where(mask, beta, 0.0)
            dw = _nt(dob, v_ref[g, pl.ds(k0, T), :])
            return lb, jnp.concatenate(_split2(l1), axis=1), jnp.sum(l1, axis=1, keepdims=True), dw, beta, omb

        def grads_all(qbs, dobs, tots, k0, sc, st):
            pre = jnp.dot(jnp.concatenate([s[1] for s in sc], axis=0), upto2, preferred_element_type=F32)
            dlws = []
            for g in range(G):
                lb = sc[g][0]
                w = jnp.exp(lb + (tots[g] - (pre[g * T:(g + 1) * T] + st[g][1])))
                dv_acc[g, pl.ds(k0, T), :] += _tn(w.astype(BF16), dobs[g])
                dlws.append(sc[g][3] * w)
            pre_d = jnp.dot(jnp.concatenate([jnp.concatenate(_split2(d), axis=1) for d in dlws], axis=0), before2,
                            preferred_element_type=F32)
            out = []
            for g in range(G):
                _, _, rowsum, _, beta, omb = sc[g]
                dq, run_l, run_d = st[g]
                dl1 = pre_d[g * T:(g + 1) * T] + run_d
                dz = ((dlws[g] * omb - dl1 * beta) * scale).astype(BF16)
                dq = dq + jnp.dot(dz, k_ref[g, pl.ds(k0, T), :], preferred_element_type=F32)
                dk_acc[g, pl.ds(k0, T), :] += _tn(dz, qbs[g])
                out.append((dq, run_l + rowsum, run_d + jnp.sum(dlws[g], axis=1, keepdims=True)))
            return tuple(out)

        def block_inputs(i, q0):
            qbs = [q_ref[g, pl.ds(q0, T), :] for g in range(G)]
            dobs = [do_ref[g, pl.ds(q0, T), :] for g in range(G)]
            tots = [_row_to_col(jnp.sum(jnp.where(blk == i, l_ref[g], 0.0), axis=0, keepdims=True), eye)
                    for g in range(G)]
            z1 = jnp.zeros((T, 1), F32)
            return qbs, dobs, tots, tuple((jnp.zeros((T, dh), F32), z1, z1) for _ in range(G))

        qbs, dobs, tots, st = block_inputs(0, 0)
        st = grads_all(qbs, dobs, tots, 0, tuple(scores(g, qbs[g], dobs[g], 0, diag) for g in range(G)), st)
        for g in range(G):
            dq_ref[g, 0:T, :] = st[g][0].astype(dq_ref.dtype)

        def qblock(i, carry0):
            q0 = pl.multiple_of(i * T, T)
            qbs, dobs, tots, st0 = block_inputs(i, q0)
            sc0 = tuple(scores(g, qbs[g], dobs[g], 0, None) for g in range(G))

            def kblock(j, carry):
                sc, st = carry
                k_cur = pl.multiple_of(j * T, T)
                k_next = pl.multiple_of((j + 1) * T, T)
                sc_next = tuple(scores(g, qbs[g], dobs[g], k_next, None) for g in range(G))
                st = grads_all(qbs, dobs, tots, k_cur, sc, st)
                return sc_next, st

            sc, st = lax.fori_loop(0, i - 1, kblock, (sc0, st0))
            k_last = pl.multiple_of((i - 1) * T, T)
            st = grads_all(qbs, dobs, tots, k_last, sc, st)
            sc_diag = tuple(scores(g, qbs[g], dobs[g], q0, diag) for g in range(G))
            st = grads_all(qbs, dobs, tots, q0, sc_diag, st)
            for g in range(G):
                dq_ref[g, pl.ds(q0, T), :] = st[g][0].astype(dq_ref.dtype)
            return carry0

        lax.fori_loop(1, nq, qblock, 0)
        dk_ref[...] = dk_acc[...].astype(dk_ref.dtype)
        dv_ref[...] = dv_acc[...].astype(dv_ref.dtype)

    spec = pl.BlockSpec((None, G, S, dh), lambda b, h: (b, h, 0, 0))
    lspec = pl.BlockSpec((None, G, nq, T), lambda b, h: (b, h, 0, 0))
    shp = jax.ShapeDtypeStruct((B, H, S, dh), BF16)
    return pl.pallas_call(
        body, name=name, out_shape=[shp, shp, shp], grid=(B, H // G),
        in_specs=[spec, spec, spec, lspec, spec], out_specs=[spec] * 3,
        scratch_shapes=[pltpu.VMEM((G, S, dh), F32), pltpu.VMEM((G, S, dh), F32)],
        compiler_params=_params(("parallel", "parallel")),
    )(q, k, v, ltot, do)


def _wide_consts(T, W):
    r = lax.broadcasted_iota(jnp.int32, (W, W), 0)
    c = lax.broadcasted_iota(jnp.int32, (W, W), 1)
    two = lambda m: jnp.concatenate([m.astype(BF16)] * 2, axis=0)
    qrow = lax.broadcasted_iota(jnp.int32, (T, W), 0)
    kcol = lax.broadcasted_iota(jnp.int32, (T, W), 1)
    er = lax.broadcasted_iota(jnp.int32, (T, T), 0)
    ec = lax.broadcasted_iota(jnp.int32, (T, T), 1)
    return two(r > c), two(r <= c), two(r < c), qrow, kcol, er == ec


def attn_fwd_wide(q, k, v, *, name):
    B, H, S, dh = q.shape
    T = ATT_BLOCK
    W = 2 * T
    nq = S // T
    G = _tile(H, (ATT_HEADS, 2))

    def body(q_ref, k_ref, v_ref, o_ref, l_ref):
        later2, _, _, qrow, kcol, eye = _wide_consts(T, W)
        blk = lax.broadcasted_iota(jnp.int32, (nq, T), 0)

        def step(qbs, k0, st, mask):
            parts, lbs, sums = [], [], []
            for g in range(G):
                _, lb, l1, _ = _att_scores(qbs[g], k_ref[g, pl.ds(k0, W), :], mask, prescaled=True)
                parts.append(jnp.concatenate(_split2(l1), axis=1))
                lbs.append(lb)
                sums.append(jnp.sum(l1, axis=1, keepdims=True))
            suf = jnp.dot(jnp.concatenate(parts, axis=0), later2, preferred_element_type=F32)
            out = []
            for g in range(G):
                acc, run = st[g]
                w = jnp.exp(lbs[g] + suf[g * T:(g + 1) * T] + run)
                acc = acc + jnp.dot(w.astype(BF16), v_ref[g, pl.ds(k0, W), :], preferred_element_type=F32)
                out.append((acc, run + sums[g]))
            return tuple(out)

        def qblock(i, totals):
            q0 = pl.multiple_of(i * T, T)
            qbs = [q_ref[g, pl.ds(q0, T), :] * (HEAD_DIM ** -0.5) for g in range(G)]
            half = jnp.right_shift(i, 1)
            last = half * W
            k_last = pl.multiple_of(last, W)
            mask = (k_last + kcol) < (q0 + qrow)
            st = tuple((jnp.zeros((T, dh), F32), jnp.zeros((T, 1), F32)) for _ in range(G))
            st = step(qbs, k_last, st, mask)

            def kblock(jj, st):
                return step(qbs, pl.multiple_of(last - jj * W, W), st, None)

            st = lax.fori_loop(1, half + 1, kblock, st)
            for g in range(G):
                o_ref[g, pl.ds(q0, T), :] = st[g][0]
            return tuple(jnp.where(blk == i, _col_to_row(st[g][1], eye), totals[g]) for g in range(G))

        totals = lax.fori_loop(0, nq, qblock, tuple(jnp.zeros((nq, T), F32) for _ in range(G)))
        for g in range(G):
            l_ref[g] = totals[g]

    spec = pl.BlockSpec((None, G, S, dh), lambda b, h: (b, h, 0, 0))
    lspec = pl.BlockSpec((None, G, nq, T), lambda b, h: (b, h, 0, 0))
    return pl.pallas_call(
        body, name=name,
        out_shape=[jax.ShapeDtypeStruct((B, H, S, dh), F32), jax.ShapeDtypeStruct((B, H, nq, T), F32)],
        grid=(B, H // G), in_specs=[spec, spec, spec], out_specs=[spec, lspec],
        compiler_params=_params(("parallel", "parallel")),
    )(q, k, v)


def attn_bwd_wide(q, k, v, ltot, do, *, name):
    B, H, S, dh = q.shape
    T = ATT_BLOCK
    W = 2 * T
    nq = S // T
    scale = HEAD_DIM ** -0.5
    G = _tile(H, (ATT_HEADS_BWD, 2))

    def body(q_ref, k_ref, v_ref, l_ref, do_ref, dq_ref, dk_ref, dv_ref, dk_acc, dv_acc):
        _, upto2, before2, qrow, kcol, eye = _wide_consts(T, W)
        blk = lax.broadcasted_iota(jnp.int32, (nq, T), 0)
        dk_acc[...] = jnp.zeros_like(dk_acc)
        dv_acc[...] = jnp.zeros_like(dv_acc)

        def step(qbs, dobs, tots, k0, st, mask):
            sc = []
            for g in range(G):
                z, lb, l1, e = _att_scores(qbs[g], k_ref[g, pl.ds(k0, W), :], mask, prescaled=True)
                inv = 1.0 / (1.0 + e)
                small = e * inv
                pos = z >= 0.0
                beta = jnp.where(pos, inv, small)
                omb = jnp.where(pos, small, inv)
                if mask is not None:
                    beta = jnp.where(mask, beta, 0.0)
                dw = _nt(dobs[g], v_ref[g, pl.ds(k0, W), :])
                sc.append((lb, jnp.concatenate(_split2(l1), axis=1), jnp.sum(l1, axis=1, keepdims=True), dw, beta, omb))
            pre = jnp.dot(jnp.concatenate([s[1] for s in sc], axis=0), upto2, preferred_element_type=F32)
            dlws = []
            for g in range(G):
                w = jnp.exp(sc[g][0] + (tots[g] - (pre[g * T:(g + 1) * T] + st[g][1])))
                dv_acc[g, pl.ds(k0, W), :] += _tn(w.astype(BF16), dobs[g])
                dlws.append(sc[g][3] * w)
            pre_d = jnp.dot(jnp.concatenate([jnp.concatenate(_split2(d), axis=1) for d in dlws], axis=0), before2,
                            preferred_element_type=F32)
            out = []
            for g in range(G):
                _, _, rowsum, _, beta, omb = sc[g]
                dq, run_l, run_d = st[g]
                dl1 = pre_d[g * T:(g + 1) * T] + run_d
                dz = (dlws[g] * omb - dl1 * beta).astype(BF16)
                dq = dq + jnp.dot(dz, k_ref[g, pl.ds(k0, W), :], preferred_element_type=F32)
                dk_acc[g, pl.ds(k0, W), :] += _tn(dz, qbs[g])
                out.append((dq, run_l + rowsum, run_d + jnp.sum(dlws[g], axis=1, keepdims=True)))
            return tuple(out)

        def qblock(i, carry0):
            q0 = pl.multiple_of(i * T, T)
            qbs = [q_ref[g, pl.ds(q0, T), :] * scale for g in range(G)]
            dobs = [do_ref[g, pl.ds(q0, T), :] for g in range(G)]
            tots = [_row_to_col(jnp.sum(jnp.where(blk == i, l_ref[g], 0.0), axis=0, keepdims=True), eye)
                    for g in range(G)]
            z1 = jnp.zeros((T, 1), F32)
            st = tuple((jnp.zeros((T, dh), F32), z1, z1) for _ in range(G))

            def kblock(j, st):
                return step(qbs, dobs, tots, pl.multiple_of(j * W, W), st, None)

            half = jnp.right_shift(i, 1)
            st = lax.fori_loop(0, half, kblock, st)
            k_last = pl.multiple_of(half * W, W)
            st = step(qbs, dobs, tots, k_last, st, (k_last + kcol) < (q0 + qrow))
            for g in range(G):
                dq_ref[g, pl.ds(q0, T), :] = (st[g][0] * scale).astype(dq_ref.dtype)
            return carry0

        lax.fori_loop(0, nq, qblock, 0)
        dk_ref[...] = dk_acc[...].astype(dk_ref.dtype)
        dv_ref[...] = dv_acc[...].astype(dv_ref.dtype)

    spec = pl.BlockSpec((None, G, S, dh), lambda b, h: (b, h, 0, 0))
    lspec = pl.BlockSpec((None, G, nq, T), lambda b, h: (b, h, 0, 0))
    shp = jax.ShapeDtypeStruct((B, H, S, dh), BF16)
    return pl.pallas_call(
        body, name=name, out_shape=[shp, shp, shp], grid=(B, H // G),
        in_specs=[spec, spec, spec, lspec, spec], out_specs=[spec] * 3,
        scratch_shapes=[pltpu.VMEM((G, S, dh), F32), pltpu.VMEM((G, S, dh), F32)],
        compiler_params=_params(("parallel", "parallel")),
    )(q, k, v, ltot, do)


def _cmul(ar, ai, br, bi):
    return ar * br - ai * bi, ar * bi + ai * br


def _cpow(lr, li, n):
    rr, ri = None, None
    br, bi = lr, li
    while n:
        if n & 1:
            rr, ri = (br, bi) if rr is None else _cmul(rr, ri, br, bi)
        n >>= 1
        if n:
            br, bi = _cmul(br, bi, br, bi)
    return rr, ri


def _ssm_scan(sr, si, lr, li, n_steps, reverse):
    W = sr.shape[1]
    R = SEGMENTS
    lim = -li if reverse else li
    zero = jnp.zeros((R, W), F32)

    def row(k):
        i = (n_steps - 1 - k) if reverse else k
        return pl.multiple_of(i * R, R)

    def local(k, st):
        cr, ci = st
        r0 = row(k)
        pr, pi = _cmul(lr, lim, cr, ci)
        nr = pr + sr[pl.ds(r0, R), :]
        ni = pi + si[pl.ds(r0, R), :]
        sr[pl.ds(r0, R), :] = nr
        si[pl.ds(r0, R), :] = ni
        return nr, ni

    er, ei = lax.fori_loop(0, n_steps, local, (zero, zero), unroll=SCAN_UNROLL)
    lnr, lni = _cpow(lr, lim, n_steps)
    rows = lax.broadcasted_iota(jnp.int32, (R, W), 0)
    cr, ci = zero, zero
    for step in range(1, R):
        tr, ti = _cmul(lnr, lni, cr, ci)
        tr, ti = tr + er, ti + ei
        if reverse:
            seg = R - 1 - step
            tr, ti = pltpu.roll(tr, R - 1, 0), pltpu.roll(ti, R - 1, 0)
        else:
            seg = step
            tr, ti = pltpu.roll(tr, 1, 0), pltpu.roll(ti, 1, 0)
        cr = jnp.where(rows == seg, tr, cr)
        ci = jnp.where(rows == seg, ti, ci)

    def fix(k, st):
        pr, pi = st
        r0 = row(k)
        ar, ai = _cmul(pr, pi, cr, ci)
        sr[pl.ds(r0, R), :] += ar
        si[pl.ds(r0, R), :] += ai
        return _cmul(lr, lim, pr, pi)

    lax.fori_loop(0, n_steps, fix, (lr, lim), unroll=SCAN_UNROLL)
    return cr, ci


def _ssm_specs(S, W):
    CH = GROUPS_PER_BLOCK * SSM_GROUP
    return dict(
        rows=pl.BlockSpec((S, CH), lambda b, j: (b, j)),
        b=pl.BlockSpec((None, CH, W), lambda b, j: (j, 0, 0)),
        c=pl.BlockSpec((None, W, CH), lambda b, j: (j, 0, 0)),
        lam=pl.BlockSpec((None, SEGMENTS, W), lambda b, j: (j, 0, 0)),
        vec=pl.BlockSpec((1, CH), lambda b, j: (0, j)),
    )


def ssm_fwd(u, bre, bim, cre, cim, lr8, li8, dsk, *, B, S, name):
    D = u.shape[1]
    J, CH, W = bre.shape
    n_steps = S // SEGMENTS
    sp = _ssm_specs(S, W)

    def body(u_ref, bre_ref, bim_ref, cre_ref, cim_ref, lr_ref, li_ref, dsk_ref, y_ref, sr, si):
        u = u_ref[...]
        ub = u.astype(BF16)
        sr[...] = jnp.dot(ub, bre_ref[...], preferred_element_type=F32)
        si[...] = jnp.dot(ub, bim_ref[...], preferred_element_type=F32)
        _ssm_scan(sr, si, lr_ref[...], li_ref[...], n_steps, False)
        y = jnp.dot(sr[...].astype(BF16), cre_ref[...], preferred_element_type=F32)
        y = y - jnp.dot(si[...].astype(BF16), cim_ref[...], preferred_element_type=F32)
        y_ref[...] = y + dsk_ref[...] * u

    return pl.pallas_call(
        body, name=name, out_shape=jax.ShapeDtypeStruct((B * S, D), F32), grid=(B, J),
        in_specs=[sp["rows"], sp["b"], sp["b"], sp["c"], sp["c"], sp["lam"], sp["lam"], sp["vec"]],
        out_specs=sp["rows"],
        scratch_shapes=[pltpu.VMEM((S, W), F32), pltpu.VMEM((S, W), F32)],
        compiler_params=_params(("parallel", "parallel")),
    )(u, bre, bim, cre, cim, lr8, li8, dsk)


def ssm_bwd(u, dy, bre, bim, cre, cim, lr8, li8, dsk, *, B, S, name):
    D = u.shape[1]
    J, CH, W = bre.shape
    n_steps = S // SEGMENTS
    sp = _ssm_specs(S, W)

    def body(u_ref, dy_ref, bre_ref, bim_ref, cre_ref, cim_ref, lr_ref, li_ref, dsk_ref,
             du_ref, dbre_ref, dbim_ref, dcre_ref, dcim_ref, dlr_ref, dli_ref, ddsk_ref, sr, si, ar, ai):
        u = u_ref[...]
        dy = dy_ref[...]
        ub = u.astype(BF16)
        dyb = dy.astype(BF16)
        lr, li = lr_ref[...], li_ref[...]
        sr[...] = jnp.dot(ub, bre_ref[...], preferred_element_type=F32)
        si[...] = jnp.dot(ub, bim_ref[...], preferred_element_type=F32)
        cr, ci = _ssm_scan(sr, si, lr, li, n_steps, False)
        ar[...] = _nt(dyb, cre_ref[...])
        ai[...] = -_nt(dyb, cim_ref[...])
        _ssm_scan(ar, ai, lr, li, n_steps, True)

        def dlam(k, st):
            dr, di = st
            r0 = pl.multiple_of((k + 1) * SEGMENTS, SEGMENTS)
            p0 = pl.multiple_of(k * SEGMENTS, SEGMENTS)
            pr, pi = sr[pl.ds(p0, SEGMENTS), :], si[pl.ds(p0, SEGMENTS), :]
            xr, xi = ar[pl.ds(r0, SEGMENTS), :], ai[pl.ds(r0, SEGMENTS), :]
            return dr + pr * xr + pi * xi, di + pr * xi - pi * xr

        xr, xi = ar[0:SEGMENTS, :], ai[0:SEGMENTS, :]
        dr, di = lax.fori_loop(0, n_steps - 1, dlam, (cr * xr + ci * xi, cr * xi - ci * xr), unroll=SCAN_UNROLL)
        dlr_ref[...] = dr
        dli_ref[...] = di
        arb = ar[...].astype(BF16)
        aib = ai[...].astype(BF16)
        du_ref[...] = _nt(arb, bre_ref[...]) + _nt(aib, bim_ref[...]) + dsk_ref[...] * dy
        dbre_ref[...] = _tn(ub, arb)
        dbim_ref[...] = _tn(ub, aib)
        dcre_ref[...] = _tn(sr[...].astype(BF16), dyb)
        dcim_ref[...] = -_tn(si[...].astype(BF16), dyb)
        ddsk_ref[...] = jnp.sum(dy * u, axis=0, keepdims=True)

    def per(shape):
        return pl.BlockSpec((None, None) + shape, lambda b, j: (b, j, 0, 0))

    return pl.pallas_call(
        body, name=name,
        out_shape=[jax.ShapeDtypeStruct((B * S, D), F32),
                   jax.ShapeDtypeStruct((B, J, CH, W), F32), jax.ShapeDtypeStruct((B, J, CH, W), F32),
                   jax.ShapeDtypeStruct((B, J, W, CH), F32), jax.ShapeDtypeStruct((B, J, W, CH), F32),
                   jax.ShapeDtypeStruct((B, J, SEGMENTS, W), F32), jax.ShapeDtypeStruct((B, J, SEGMENTS, W), F32),
                   jax.ShapeDtypeStruct((B, J, 1, CH), F32)],
        grid=(B, J),
        in_specs=[sp["rows"], sp["rows"], sp["b"], sp["b"], sp["c"], sp["c"], sp["lam"], sp["lam"], sp["vec"]],
        out_specs=[sp["rows"], per((CH, W)), per((CH, W)), per((W, CH)), per((W, CH)), per((SEGMENTS, W)),
                   per((SEGMENTS, W)),
                   per((1, CH))],
        scratch_shapes=[pltpu.VMEM((S, W), F32)] * 4,
        compiler_params=_params(("parallel", "parallel")),
    )(u, dy, bre, bim, cre, cim, lr8, li8, dsk)


def _ssm_discretize(a_re, a_im, log_dt, b_re, b_im):
    dt = jnp.exp(log_dt)[:, None]
    er = jnp.exp(a_re * dt)
    lr = er * jnp.cos(a_im * dt)
    li = er * jnp.sin(a_im * dt)
    den = a_re * a_re + a_im * a_im
    fr = ((lr - 1.0) * a_re + li * a_im) / den
    fi = (li * a_re - (lr - 1.0) * a_im) / den
    bbr = fr[..., None] * b_re - fi[..., None] * b_im
    bbi = fr[..., None] * b_im + fi[..., None] * b_re
    return lr, li, bbr, bbi


def _block_diag_in(m):
    G, P, H = m.shape
    J = G // GROUPS_PER_BLOCK
    m = m.reshape(J, GROUPS_PER_BLOCK, P, H).transpose(0, 1, 3, 2)
    eye = jnp.eye(GROUPS_PER_BLOCK, dtype=m.dtype)
    out = m[:, :, :, None, :] * eye[None, :, None, :, None]
    return out.reshape(J, GROUPS_PER_BLOCK * H, GROUPS_PER_BLOCK * P)


def _block_diag_in_grad(d, G, P, H):
    J = G // GROUPS_PER_BLOCK
    d = d.reshape(J, GROUPS_PER_BLOCK, H, GROUPS_PER_BLOCK, P)
    idx = jnp.arange(GROUPS_PER_BLOCK)
    d = d[:, idx, :, idx, :]
    return d.transpose(1, 0, 3, 2).reshape(G, P, H)


def _block_diag_out(m):
    G, H, P = m.shape
    J = G // GROUPS_PER_BLOCK
    m = m.reshape(J, GROUPS_PER_BLOCK, H, P).transpose(0, 1, 3, 2)
    eye = jnp.eye(GROUPS_PER_BLOCK, dtype=m.dtype)
    out = m[:, :, :, None, :] * eye[None, :, None, :, None]
    return out.reshape(J, GROUPS_PER_BLOCK * P, GROUPS_PER_BLOCK * H)


def _block_diag_out_grad(d, G, H, P):
    J = G // GROUPS_PER_BLOCK
    d = d.reshape(J, GROUPS_PER_BLOCK, P, GROUPS_PER_BLOCK, H)
    idx = jnp.arange(GROUPS_PER_BLOCK)
    d = d[:, idx, :, idx, :]
    return d.transpose(1, 0, 3, 2).reshape(G, H, P)


def _interleave(a, B, S):
    L = S // SEGMENTS
    return a.reshape(B, SEGMENTS, L, a.shape[-1]).transpose(0, 2, 1, 3).reshape(B * S, a.shape[-1])


def _deinterleave(a, B, S):
    L = S // SEGMENTS
    return a.reshape(B, L, SEGMENTS, a.shape[-1]).transpose(0, 2, 1, 3).reshape(B * S, a.shape[-1])


def _adamw_math(w, g, m, v):
    m = ADAM_B1 * m + (1.0 - ADAM_B1) * g
    v = ADAM_B2 * v + (1.0 - ADAM_B2) * (g * g)
    m_hat = m / (1.0 - ADAM_B1 ** ADAM_STEP)
    v_hat = v / (1.0 - ADAM_B2 ** ADAM_STEP)
    delta = -ADAM_LR * (m_hat / (jnp.sqrt(v_hat) + ADAM_EPS) + ADAM_WD * w)
    return delta, m, v


def adamw(w, g, m, v, *, name):
    R, C = w.shape
    tr = _tile(R, (max(8, (1 << 18) // C // 8 * 8), 256, 128, 64, 32, 16, 8))

    def body(w_ref, g_ref, m_ref, v_ref, d_ref, nm_ref, nv_ref):
        d, nm, nv = _adamw_math(w_ref[...], g_ref[...], m_ref[...], v_ref[...])
        d_ref[...] = d
        nm_ref[...] = nm
        nv_ref[...] = nv

    spec = pl.BlockSpec((tr, C), lambda i: (i, 0))
    shp = jax.ShapeDtypeStruct((R, C), F32)
    return pl.pallas_call(
        body, name=name, out_shape=[shp, shp, shp], grid=(R // tr,), in_specs=[spec] * 4, out_specs=[spec] * 3,
        compiler_params=_params(("parallel",)),
    )(w, g, m, v)


def sum_leading(a, *, name, out_dtype=F32):
    n, R, C = a.shape
    tr = _tile(R, (256, 128, 64, 32, 16, 8))

    def body(a_ref, o_ref):
        acc = a_ref[0].astype(F32)
        for i in range(1, n):
            acc = acc + a_ref[i].astype(F32)
        o_ref[...] = acc.astype(o_ref.dtype)

    return pl.pallas_call(
        body, name=name, out_shape=jax.ShapeDtypeStruct((R, C), out_dtype), grid=(R // tr,),
        in_specs=[pl.BlockSpec((n, tr, C), lambda i: (0, i, 0))], out_specs=pl.BlockSpec((tr, C), lambda i: (i, 0)),
        compiler_params=_params(("parallel",)),
    )(a)


def _any_specs(n):
    return [pl.BlockSpec(memory_space=pl.ANY) for _ in range(n)]


def _coords():
    return lax.axis_index("x"), lax.axis_index("y"), lax.axis_index("c")


def _flip(v, bit):
    return (v + bit) % 2


def all_gather8(a, *, name):
    shape = a.shape

    def body(a_ref, o_ref, send_sems, recv_sems, local_sem):
        x, y, c = _coords()
        me = 4 * x + 2 * y + c
        mine = pltpu.make_async_copy(a_ref, o_ref.at[me], local_sem)
        mine.start()
        sends = []
        for k in range(1, N_DEV):
            peer = (_flip(x, (k >> 2) & 1), _flip(y, (k >> 1) & 1), _flip(c, k & 1))
            cp = pltpu.make_async_remote_copy(a_ref, o_ref.at[me], send_sems.at[k - 1], recv_sems.at[k - 1],
                                              device_id=peer, device_id_type=MESH)
            cp.start()
            sends.append(cp)
        for k in range(1, N_DEV):
            px, py, pc = _flip(x, (k >> 2) & 1), _flip(y, (k >> 1) & 1), _flip(c, k & 1)
            src = 4 * px + 2 * py + pc
            pltpu.make_async_remote_copy(a_ref, o_ref.at[src], send_sems.at[k - 1], recv_sems.at[k - 1],
                                         device_id=(px, py, pc), device_id_type=MESH).wait_recv()
        for cp in sends:
            cp.wait_send()
        mine.wait()

    return pl.pallas_call(
        body, name=name, out_shape=jax.ShapeDtypeStruct((N_DEV,) + shape, a.dtype),
        in_specs=_any_specs(1), out_specs=pl.BlockSpec(memory_space=pl.ANY),
        scratch_shapes=[pltpu.SemaphoreType.DMA((N_DEV - 1,)), pltpu.SemaphoreType.DMA((N_DEV - 1,)),
                        pltpu.SemaphoreType.DMA(())],
    )(a)


def _chip_of(x, y, p):
    px, py = _flip(x, (p >> 1) & 1), _flip(y, p & 1)
    return 2 * px + py, px, py


def gather_chip_shards(arrs, *, name):
    n = len(arrs)

    def body(*refs):
        ins, outs = refs[:n], refs[n:2 * n]
        ici_send, ici_recv, d2d_send, d2d_recv, local_sems = refs[2 * n:2 * n + 5]
        bufs = refs[2 * n + 5:]
        x, y, c = _coords()
        me = 2 * x + y
        loads = []
        for i in range(n):
            cp = pltpu.make_async_copy(ins[i], bufs[i], local_sems.at[i])
            cp.start()
            loads.append(cp)
        sends = []
        for i in range(n):
            half = ins[i].shape[0] // 2
            rows = pl.ds(c * half, half)
            for p in range(1, N_CHIPS):
                _, px, py = _chip_of(x, y, p)
                s = i * 3 + p - 1
                cp = pltpu.make_async_remote_copy(ins[i].at[rows], outs[i].at[me, rows], ici_send.at[s], ici_recv.at[s],
                                                  device_id=(px, py, c), device_id_type=MESH)
                cp.start()
                sends.append(cp)
        stores = []
        for i in range(n):
            loads[i].wait()
            cp = pltpu.make_async_copy(bufs[i], outs[i].at[me], local_sems.at[i])
            cp.start()
            stores.append(cp)
        for i in range(n):
            half = ins[i].shape[0] // 2
            rows = pl.ds(c * half, half)
            for p in range(1, N_CHIPS):
                src, px, py = _chip_of(x, y, p)
                s = i * 3 + p - 1
                pltpu.make_async_remote_copy(ins[i].at[rows], outs[i].at[src, rows], ici_send.at[s], ici_recv.at[s],
                                             device_id=(px, py, c), device_id_type=MESH).wait_recv()
                cp = pltpu.make_async_remote_copy(outs[i].at[src, rows], outs[i].at[src, rows], d2d_send.at[s],
                                                  d2d_recv.at[s], device_id=(x, y, 1 - c), device_id_type=MESH)
                cp.start()
                sends.append(cp)
        for i in range(n):
            half = ins[i].shape[0] // 2
            theirs = pl.ds((1 - c) * half, half)
            for p in range(1, N_CHIPS):
                src, _, _ = _chip_of(x, y, p)
                s = i * 3 + p - 1
                pltpu.make_async_remote_copy(outs[i].at[src, theirs], outs[i].at[src, theirs], d2d_send.at[s],
                                             d2d_recv.at[s], device_id=(x, y, 1 - c), device_id_type=MESH).wait_recv()
        for cp in sends:
            cp.wait_send()
        for cp in stores:
            cp.wait()

    dma = pltpu.SemaphoreType.DMA
    return pl.pallas_call(
        body, name=name,
        out_shape=[jax.ShapeDtypeStruct((N_CHIPS,) + a.shape, a.dtype) for a in arrs],
        in_specs=_any_specs(n), out_specs=_any_specs(n),
        scratch_shapes=[dma((3 * n,)), dma((3 * n,)), dma((3 * n,)), dma((3 * n,)), dma((n,))]
        + [pltpu.VMEM(a.shape, a.dtype) for a in arrs],
        compiler_params=pltpu.CompilerParams(vmem_limit_bytes=V7X_VMEM_LIMIT),
    )(*arrs)


def swap_halves(arrs, *, name):
    n = len(arrs)

    def body(*refs):
        ins, outs = refs[:n], refs[n:2 * n]
        send_sems, recv_sems = refs[2 * n:]
        x, y, c = _coords()
        cps = []
        for i in range(n):
            half = ins[i].shape[1] // 2
            cp = pltpu.make_async_remote_copy(ins[i].at[:, pl.ds((1 - c) * half, half)], outs[i], send_sems.at[i],
                                              recv_sems.at[i], device_id=(x, y, 1 - c), device_id_type=MESH)
            cp.start()
            cps.append(cp)
        for cp in cps:
            cp.wait()

    dma = pltpu.SemaphoreType.DMA
    return pl.pallas_call(
        body, name=name,
        out_shape=[jax.ShapeDtypeStruct((N_CHIPS, a.shape[1] // 2, a.shape[2]), a.dtype) for a in arrs],
        in_specs=_any_specs(n), out_specs=_any_specs(n), scratch_shapes=[dma((n,)), dma((n,))],
    )(*arrs)


def add_half(g, other, c_idx, *, name, out_dtype):
    _, R, C = g.shape
    half = R // 2
    tr = _tile(half, (256, 128, 64, 32, 16, 8))
    nt = half // tr

    def body(c_ref, g_ref, o_ref, out_ref):
        out_ref[...] = (g_ref[...] + o_ref[...]).astype(out_ref.dtype)

    return pl.pallas_call(
        body, name=name, out_shape=jax.ShapeDtypeStruct((N_CHIPS, half, C), out_dtype),
        grid_spec=pltpu.PrefetchScalarGridSpec(
            num_scalar_prefetch=1, grid=(N_CHIPS, nt),
            in_specs=[pl.BlockSpec((None, tr, C), lambda r, t, c_ref: (r, c_ref[0] * nt + t, 0)),
                      pl.BlockSpec((None, tr, C), lambda r, t, c_ref: (r, t, 0))],
            out_specs=pl.BlockSpec((None, tr, C), lambda r, t, c_ref: (r, t, 0))),
        compiler_params=_params(("parallel", "parallel")),
    )(c_idx, g, other)


def scatter_to_chips(arrs, *, name):
    n = len(arrs)

    def body(*refs):
        ins, outs = refs[:n], refs[n:2 * n]
        send_sems, recv_sems = refs[2 * n:]
        x, y, c = _coords()
        cps = []
        for i in range(n):
            for p in range(1, N_CHIPS):
                dst, px, py = _chip_of(x, y, p)
                s = i * 3 + p - 1
                cp = pltpu.make_async_remote_copy(ins[i].at[dst], outs[i].at[p - 1], send_sems.at[s], recv_sems.at[s],
                                                  device_id=(px, py, c), device_id_type=MESH)
                cp.start()
                cps.append(cp)
        for cp in cps:
            cp.wait()

    dma = pltpu.SemaphoreType.DMA
    return pl.pallas_call(
        body, name=name,
        out_shape=[jax.ShapeDtypeStruct((N_CHIPS - 1,) + a.shape[1:], a.dtype) for a in arrs],
        in_specs=_any_specs(n), out_specs=_any_specs(n), scratch_shapes=[dma((3 * n,)), dma((3 * n,))],
    )(*arrs)


def add_chips(h, got, r_idx, *, name):
    _, R, C = h.shape
    tr = _tile(R, (256, 128, 64, 32, 16, 8))

    def body(r_ref, h_ref, g_ref, out_ref):
        acc = h_ref[...].astype(F32)
        for p in range(N_CHIPS - 1):
            acc = acc + g_ref[p].astype(F32)
        out_ref[...] = acc

    return pl.pallas_call(
        body, name=name, out_shape=jax.ShapeDtypeStruct((R, C), F32),
        grid_spec=pltpu.PrefetchScalarGridSpec(
            num_scalar_prefetch=1, grid=(R // tr,),
            in_specs=[pl.BlockSpec((None, tr, C), lambda t, r_ref: (r_ref[0], t, 0)),
                      pl.BlockSpec((N_CHIPS - 1, tr, C), lambda t, r_ref: (0, t, 0))],
            out_specs=pl.BlockSpec((tr, C), lambda t, r_ref: (t, 0))),
        compiler_params=_params(("parallel",)),
    )(r_idx, h, got)


def join_halves(arrs, *, name):
    n = len(arrs)

    def body(*refs):
        ins, outs = refs[:n], refs[n:2 * n]
        send_sems, recv_sems, local_sems = refs[2 * n:2 * n + 3]
        bufs = refs[2 * n + 3:]
        x, y, c = _coords()
        loads, sends, stores = [], [], []
        for i in range(n):
            cp = pltpu.make_async_copy(ins[i], bufs[i], local_sems.at[i])
            cp.start()
            loads.append(cp)
        for i in range(n):
            half = ins[i].shape[0]
            cp = pltpu.make_async_remote_copy(ins[i], outs[i].at[pl.ds(c * half, half)], send_sems.at[i], recv_sems.at[i],
                                              device_id=(x, y, 1 - c), device_id_type=MESH)
            cp.start()
            sends.append(cp)
        for i in range(n):
            half = ins[i].shape[0]
            loads[i].wait()
            cp = pltpu.make_async_copy(bufs[i], outs[i].at[pl.ds(c * half, half)], local_sems.at[i])
            cp.start()
            stores.append(cp)
        for i in range(n):
            half = ins[i].shape[0]
            pltpu.make_async_remote_copy(ins[i], outs[i].at[pl.ds((1 - c) * half, half)], send_sems.at[i],
                                         recv_sems.at[i], device_id=(x, y, 1 - c), device_id_type=MESH).wait_recv()
        for i in range(n):
            sends[i].wait_send()
            stores[i].wait()

    dma = pltpu.SemaphoreType.DMA
    return pl.pallas_call(
        body, name=name,
        out_shape=[jax.ShapeDtypeStruct((2 * a.shape[0], a.shape[1]), a.dtype) for a in arrs],
        in_specs=_any_specs(n), out_specs=_any_specs(n),
        scratch_shapes=[dma((n,)), dma((n,)), dma((n,))] + [pltpu.VMEM(a.shape, a.dtype) for a in arrs],
        compiler_params=pltpu.CompilerParams(vmem_limit_bytes=V7X_VMEM_LIMIT),
    )(*arrs)


def reduce_scatter_chips(grads, wire_dtypes):
    x, y, c = _coords()
    c_idx = jnp.reshape(c, (1,)).astype(jnp.int32)
    r_idx = jnp.reshape(2 * x + y, (1,)).astype(jnp.int32)
    theirs = swap_halves(grads, name="rs_swap_halves")
    pair = [add_half(g, o, c_idx, name=f"rs_add_half_{i}", out_dtype=wire_dtypes[i])
            for i, (g, o) in enumerate(zip(grads, theirs))]
    got = scatter_to_chips(pair, name="rs_scatter_to_chips")
    mine = [add_chips(h, g, r_idx, name=f"rs_add_chips_{i}") for i, (h, g) in enumerate(zip(pair, got))]
    return join_halves(mine, name="rs_join_halves")


def _to_heads(t, B, S):
    return t.reshape(B, S, -1, HEAD_DIM).transpose(0, 2, 1, 3)


def _from_heads(t, B, S):
    return t.transpose(0, 2, 1, 3).reshape(B * S, -1)


def _chip_major(w, axis):
    n = w.shape[axis] // N_CHIPS
    parts = w.reshape(w.shape[:axis] + (N_CHIPS, n) + w.shape[axis + 1:])
    return jnp.moveaxis(parts, axis, 0)


def _from_chip_major(g, axis):
    g = jnp.moveaxis(g, 0, axis)
    return g.reshape(g.shape[:axis] + (g.shape[axis] * g.shape[axis + 1],) + g.shape[axis + 2:])


def kernel(x, c, norm_mix, norm_ffn, w_mod, b_mod, w_qkv, w_o_attn, w_in_ssm, a_re, a_im, log_dt, b_re, b_im, c_re, c_im, d_skip, w_glu, b_glu, w_o_ssm, w_up, conv_w, conv_b, w_down, norm_out, w_fin, b_fin, loss_target, m_norm_mix, m_norm_ffn, m_w_mod, m_b_mod, m_w_qkv, m_w_o_attn, m_w_in_ssm, m_a_re, m_a_im, m_log_dt, m_b_re, m_b_im, m_c_re, m_c_im, m_d_skip, m_w_glu, m_b_glu, m_w_o_ssm, m_w_up, m_conv_w, m_conv_b, m_w_down, m_norm_out, m_w_fin, m_b_fin, v_norm_mix, v_norm_ffn, v_w_mod, v_b_mod, v_w_qkv, v_w_o_attn, v_w_in_ssm, v_a_re, v_a_im, v_log_dt, v_b_re, v_b_im, v_c_re, v_c_im, v_d_skip, v_w_glu, v_b_glu, v_w_o_ssm, v_w_up, v_conv_w, v_conv_b, v_w_down, v_norm_out, v_w_fin, v_b_fin):
    B, S, D = x.shape
    T = B * S
    F2 = conv_b.shape[1]
    F = F2 // 2
    G, P = a_re.shape[1], a_re.shape[2]
    H = b_re.shape[3]
    mx, my, mc = _coords()
    chip = 2 * mx + my
    dev = 4 * mx + 2 * my + mc
    BG = N_DEV * B
    mod_w = w_mod.shape[2]
    fin_w = w_fin.shape[1]

    c_all = all_gather8(c, name="gather_c").reshape(BG, D)
    c_act = silu_rows(c_all, name="silu_c")
    b_mod_mine = lax.dynamic_slice(b_mod, (0, chip * mod_w), (2, mod_w))
    b_fin_mine = lax.dynamic_slice(b_fin, (chip * fin_w,), (fin_w,))
    cond = [matmul(c_act, w_mod[i], bias=b_mod_mine[i], name=f"mod_proj_{i}") for i in range(2)]
    cond.append(matmul(c_act, w_fin, bias=b_fin_mine, name="fin_proj"))
    cond_all = all_gather8(jnp.concatenate(cond, axis=1), name="gather_cond")
    cond_all = cond_all[::2]
    cond_rows = lax.dynamic_slice(cond_all, (0, dev * B, 0), (N_CHIPS, B, cond_all.shape[2]))
    mods = []
    for i in range(2):
        full = cond_rows[:, :, i * mod_w:(i + 1) * mod_w].transpose(1, 0, 2).reshape(B, N_CHIPS * mod_w)
        mods.append([full[:, k * D:(k + 1) * D] for k in range(6)])
    fin = cond_rows[:, :, 2 * mod_w:].transpose(1, 0, 2).reshape(B, N_CHIPS * fin_w)
    sh_f, sc_f = fin[:, :D], fin[:, D:]

    rows1024 = jnp.concatenate([w_o_attn[0], w_in_ssm[0], w_glu[0], w_o_ssm[0], w_down.reshape(-1, D)], axis=0)
    W_qkv, g_rows, W_up0, W_up1 = gather_chip_shards(
        [w_qkv[0].astype(BF16), rows1024.astype(BF16), w_up[0].astype(BF16), w_up[1].astype(BF16)],
        name="gather_weights")
    W_up = [W_up0, W_up1]
    Dq = D // N_CHIPS
    Fq = F // N_CHIPS
    W_o_attn = g_rows[:, 0 * Dq:1 * Dq].reshape(D, D)
    W_in = g_rows[:, 1 * Dq:2 * Dq].reshape(D, D)
    W_glu = g_rows[:, 2 * Dq:3 * Dq].reshape(D, D)
    W_o_ssm = g_rows[:, 3 * Dq:4 * Dq].reshape(D, D)
    W_down = [g_rows[:, 4 * Dq + i * Fq:4 * Dq + (i + 1) * Fq].reshape(F, D) for i in range(2)]
    small = jnp.concatenate([conv_w.reshape(6, -1), jnp.pad(d_skip, ((0, 0), (0, conv_w.shape[2] - Dq))),
                             jnp.pad(b_glu, ((0, 0), (0, conv_w.shape[2] - Dq)))], axis=0)
    small_all = all_gather8(small, name="gather_small")[::2]
    conv_w_full = _from_chip_major(small_all[:, :6].reshape(N_CHIPS, 2, 3, -1), 2)
    d_skip_full = small_all[:, 6, :Dq].reshape(1, D)
    b_glu_full = small_all[:, 7, :Dq].reshape(D)

    x0 = x.reshape(T, D)
    tgt = loss_target.reshape(T, D)

    def ffn_fwd(xin, i):
        sh2, sc2, g2 = mods[i][3], mods[i][4], mods[i][5]
        h2 = norm_mod_fwd(xin, norm_ffn[i], sh2, sc2, B=B, S=S, name=f"ffn_norm_{i}")
        up = matmul(h2, W_up[i], b_chips=True, name=f"ffn_up_{i}")
        act = conv_gate_fwd(up, conv_w_full[i], conv_b[i:i + 1], B=B, S=S, name=f"ffn_conv_{i}")
        yf = matmul(act, W_down[i], name=f"ffn_down_{i}")
        xout = gate_res_fwd(xin, yf, g2, B=B, S=S, name=f"ffn_res_{i}")
        return xout, (xin, h2, up, act, yf)

    sh1, sc1, g1 = mods[0][0], mods[0][1], mods[0][2]
    h1a = norm_mod_fwd(x0, norm_mix[0], sh1, sc1, B=B, S=S, name="att_norm")
    qkv = matmul(h1a, W_qkv, out_dtype=BF16, b_chips=True, name="att_qkv")
    q, k, v = [_to_heads(qkv[:, i * D:(i + 1) * D], B, S) for i in range(3)]
    o, ltot = attn_fwd_wide(q, k, v, name="att_fwd")
    o2 = _from_heads(o, B, S).astype(BF16)
    ya = matmul(o2, W_o_attn, name="att_out")
    x1 = gate_res_fwd(x0, ya, g1, B=B, S=S, name="att_res")
    x2, ffn0 = ffn_fwd(x1, 0)

    lr, li, bbr, bbi = _ssm_discretize(a_re[0], a_im[0], log_dt[0], b_re[0], b_im[0])
    J = G // GROUPS_PER_BLOCK
    Wst = GROUPS_PER_BLOCK * P
    bre_blk = _block_diag_in(bbr).astype(BF16)
    bim_blk = _block_diag_in(bbi).astype(BF16)
    cre_blk = _block_diag_out(c_re[0]).astype(BF16)
    cim_blk = _block_diag_out(c_im[0]).astype(BF16)
    lr8 = jnp.broadcast_to(lr.reshape(J, 1, Wst), (J, SEGMENTS, Wst))
    li8 = jnp.broadcast_to(li.reshape(J, 1, Wst), (J, SEGMENTS, Wst))
    sh1s, sc1s, g1s = mods[1][0], mods[1][1], mods[1][2]
    h1s = norm_mod_fwd(x2, norm_mix[1], sh1s, sc1s, B=B, S=S, name="ssm_norm")
    h1p = _interleave(h1s, B, S)
    u = matmul(h1p, W_in, name="ssm_in")
    y_ssm = ssm_fwd(u, bre_blk, bim_blk, cre_blk, cim_blk, lr8, li8, d_skip_full, B=B, S=S, name="ssm_scan_fwd")
    zb = gelu_fwd(y_ssm, B=B, S=S, name="ssm_gelu")
    s_glu = matmul(zb, W_glu, bias=b_glu_full, name="ssm_glu_proj")
    gb = glu_fwd(y_ssm, s_glu, B=B, S=S, name="ssm_glu")
    ys_p = matmul(gb, W_o_ssm, name="ssm_out")
    ys = _deinterleave(ys_p, B, S)
    x3 = gate_res_fwd(x2, ys, g1s, B=B, S=S, name="ssm_res")
    x4, ffn1 = ffn_fwd(x3, 1)

    dx4, loss_p, dsh_f, dsc_f, dnorm_out = final_loss(x4, tgt, norm_out, sh_f, sc_f, B=B, S=S, name="loss_head")
    loss = lax.psum(jnp.sum(loss_p), ("x", "y", "c"))

    def ffn_bwd(dxo, i, saved):
        xin, h2, up, act, yf = saved
        sc2, g2 = mods[i][4], mods[i][5]
        dyf, dg2 = gate_res_bwd(dxo, yf, g2, B=B, S=S, name=f"ffn_res_bwd_{i}")
        dact = matmul(dyf, W_down[i], tb=True, name=f"ffn_down_dx_{i}")
        dW_down = matmul(act, dyf, ta=True, name=f"ffn_down_dw_{i}")
        dup, dcw, dcb = conv_gate_bwd(up, dact, conv_w_full[i], conv_b[i:i + 1], B=B, S=S, name=f"ffn_conv_bwd_{i}")
        dh2 = matmul(dup, W_up[i], tb=True, b_chips=True, name=f"ffn_up_dx_{i}")
        dW_up = matmul(h2, dup, ta=True, b_chips=True, out_chips=True, name=f"ffn_up_dw_{i}")
        dxin, dsh2, dsc2, dnf = norm_mod_bwd(dh2, xin, dxo, norm_ffn[i], sc2, B=B, S=S, name=f"ffn_norm_bwd_{i}")
        dconv_w = jnp.sum(dcw, axis=0).transpose(1, 0, 2).reshape(3, F2)
        return dxin, dict(dW_down=dW_down, dW_up=dW_up, dconv_b=jnp.sum(dcb, axis=0).reshape(F2),
                          dconv_w=dconv_w, dnorm_ffn=jnp.sum(dnf, axis=0), dsh2=dsh2, dsc2=dsc2, dg2=dg2)

    dx3, gf1 = ffn_bwd(dx4, 1, ffn1)

    dys_p, dg1s = gate_res_bwd(_interleave(dx3, B, S), ys_p, g1s, B=B, S=S, name="ssm_res_bwd")
    dgb = matmul(dys_p, W_o_ssm, tb=True, name="ssm_out_dx")
    dW_o_ssm = matmul(gb, dys_p, ta=True, name="ssm_out_dw")
    ds_glu, dz1, db_glu = glu_bwd1(y_ssm, s_glu, dgb, B=B, S=S, name="ssm_glu_bwd1")
    dz2 = matmul(ds_glu, W_glu, tb=True, name="ssm_glu_dx")
    dW_glu = matmul(zb, ds_glu, ta=True, name="ssm_glu_dw")
    dy_ssm = glu_bwd2(y_ssm, dz1, dz2, B=B, S=S, name="ssm_glu_bwd2")
    du, dbre, dbim, dcre, dcim, dlr8, dli8, ddsk = ssm_bwd(u, dy_ssm, bre_blk, bim_blk, cre_blk, cim_blk, lr8, li8,
                                                           d_skip_full, B=B, S=S, name="ssm_scan_bwd")
    dub = du.astype(BF16)
    dh1p = matmul(dub, W_in, tb=True, name="ssm_in_dx")
    dW_in = matmul(h1p, dub, ta=True, name="ssm_in_dw")
    dx2, dsh1s, dsc1s, dnm1 = norm_mod_bwd(_deinterleave(dh1p, B, S), x2, dx3, norm_mix[1], sc1s, B=B, S=S,
                                           name="ssm_norm_bwd")
    dlr = jnp.sum(dlr8, axis=(0, 2)).reshape(G, P)
    dli = jnp.sum(dli8, axis=(0, 2)).reshape(G, P)
    dbbr = _block_diag_in_grad(jnp.sum(dbre, axis=0), G, P, H)
    dbbi = _block_diag_in_grad(jnp.sum(dbim, axis=0), G, P, H)
    dc_re = _block_diag_out_grad(jnp.sum(dcre, axis=0), G, H, P)
    dc_im = _block_diag_out_grad(jnp.sum(dcim, axis=0), G, H, P)
    dd_skip = jnp.sum(ddsk, axis=0).reshape(D)

    dx1, gf0 = ffn_bwd(dx2, 0, ffn0)

    dya, dg1 = gate_res_bwd(dx1, ya, g1, B=B, S=S, name="att_res_bwd")
    do2 = matmul(dya, W_o_attn, tb=True, out_dtype=BF16, name="att_out_dx")
    dW_o_attn = matmul(o2, dya, ta=True, name="att_out_dw")
    dq, dk, dv = attn_bwd_wide(q, k, v, ltot, _to_heads(do2, B, S), name="att_bwd")
    dqkv = jnp.concatenate([_from_heads(t, B, S) for t in (dq, dk, dv)], axis=1)
    dh1a = matmul(dqkv, W_qkv, tb=True, b_chips=True, name="att_qkv_dx")
    dW_qkv = matmul(h1a, dqkv, ta=True, out_chips=True, name="att_qkv_dw")
    grad_x, dsh1, dsc1, dnm0 = norm_mod_bwd(dh1a, x0, dx1, norm_mix[0], sc1, B=B, S=S, name="att_norm_bwd")

    dmod_rows = jnp.concatenate([dsh1, dsc1, dg1, gf0["dsh2"], gf0["dsc2"], gf0["dg2"],
                                 dsh1s, dsc1s, dg1s, gf1["dsh2"], gf1["dsc2"], gf1["dg2"], dsh_f, dsc_f], axis=1)
    dmod_all = all_gather8(dmod_rows, name="gather_dmod").reshape(BG, 14 * D)
    grad_w_mod = jnp.stack([
        matmul(c_act, lax.dynamic_slice(dmod_all, (0, i * 6 * D + chip * mod_w), (BG, mod_w)), ta=True,
               name=f"mod_dw_{i}") for i in range(2)])
    grad_w_fin = matmul(c_act, lax.dynamic_slice(dmod_all, (0, 12 * D + chip * fin_w), (BG, fin_w)), ta=True,
                        name="fin_dw")

    parts = [jnp.concatenate([jnp.sum(dnm0, axis=0), jnp.sum(dnm1, axis=0)]),
             jnp.concatenate([gf0["dnorm_ffn"], gf1["dnorm_ffn"]]),
             jnp.sum(dmod_rows[:, :12 * D], axis=0),
             dlr.reshape(-1), dli.reshape(-1), dbbr.reshape(-1), dbbi.reshape(-1), dc_re.reshape(-1), dc_im.reshape(-1),
             dd_skip, jnp.sum(db_glu, axis=0),
             gf0["dconv_w"].reshape(-1), gf1["dconv_w"].reshape(-1), gf0["dconv_b"], gf1["dconv_b"],
             jnp.sum(dnorm_out, axis=0), jnp.sum(dmod_rows[:, 12 * D:], axis=0)]
    sizes = [int(p.shape[0]) for p in parts]
    flat = jnp.concatenate(parts)
    width = 1024
    quantum = N_CHIPS * 16 * width
    padded = -(-flat.shape[0] // quantum) * quantum
    small_cm = jnp.pad(flat, (0, padded - flat.shape[0])).reshape(N_CHIPS, -1, width)

    g_rows_cm = jnp.concatenate([dW_o_attn.reshape(N_CHIPS, Dq, D), dW_in.reshape(N_CHIPS, Dq, D),
                                 dW_glu.reshape(N_CHIPS, Dq, D), dW_o_ssm.reshape(N_CHIPS, Dq, D),
                                 gf0["dW_down"].reshape(N_CHIPS, Fq, D), gf1["dW_down"].reshape(N_CHIPS, Fq, D)], axis=1)
    r_qkv, r_rows, r_up0, r_up1, r_small = reduce_scatter_chips(
        [dW_qkv, g_rows_cm, gf0["dW_up"], gf1["dW_up"], small_cm], [BF16, BF16, BF16, BF16, F32])
    grad_w_qkv = r_qkv[None]
    grad_w_o_attn = r_rows[0 * Dq:1 * Dq][None]
    grad_w_in_ssm = r_rows[1 * Dq:2 * Dq][None]
    grad_w_glu = r_rows[2 * Dq:3 * Dq][None]
    grad_w_o_ssm = r_rows[3 * Dq:4 * Dq][None]
    grad_w_down = r_rows[4 * Dq:].reshape(2, Fq, D)
    grad_w_up = jnp.stack([r_up0, r_up1])
    summed = all_gather8(r_small, name="gather_small_grads")[::2].reshape(-1)
    offs = [0]
    for s_ in sizes:
        offs.append(offs[-1] + s_)
    (s_nm, s_nf, s_bmod, s_lr, s_li, s_bbr, s_bbi, s_cre, s_cim, s_dsk, s_bglu, s_cw0, s_cw1, s_cb0, s_cb1, s_no,
     s_bfin) = [summed[offs[i]:offs[i + 1]] for i in range(len(sizes))]
    _, disc_vjp = jax.vjp(_ssm_discretize, a_re[0], a_im[0], log_dt[0], b_re[0], b_im[0])
    ga_re, ga_im, glog_dt, gb_re, gb_im = disc_vjp((s_lr.reshape(G, P), s_li.reshape(G, P), s_bbr.reshape(G, P, H),
                                                    s_bbi.reshape(G, P, H)))
    grad_norm_mix = s_nm.reshape(2, D)
    grad_norm_ffn = s_nf.reshape(2, D)
    grad_b_mod = s_bmod.reshape(2, 6 * D)
    grad_c_re = s_cre.reshape(1, G, H, P)
    grad_c_im = s_cim.reshape(1, G, H, P)
    grad_d_skip = lax.dynamic_slice(s_dsk, (chip * Dq,), (Dq,)).reshape(1, Dq)
    grad_b_glu = lax.dynamic_slice(s_bglu, (chip * Dq,), (Dq,)).reshape(1, Dq)
    cw_full = jnp.stack([s_cw0.reshape(3, F2), s_cw1.reshape(3, F2)])
    grad_conv_w = lax.dynamic_slice(cw_full, (0, 0, chip * (F2 // N_CHIPS)), (2, 3, F2 // N_CHIPS))
    grad_conv_b = jnp.stack([s_cb0, s_cb1])
    grad_norm_out = s_no
    grad_b_fin = s_bfin

    grads = dict(
        norm_mix=grad_norm_mix, norm_ffn=grad_norm_ffn, w_mod=grad_w_mod, b_mod=grad_b_mod, w_qkv=grad_w_qkv,
        w_o_attn=grad_w_o_attn, w_in_ssm=grad_w_in_ssm, a_re=ga_re[None], a_im=ga_im[None], log_dt=glog_dt[None],
        b_re=gb_re[None], b_im=gb_im[None], c_re=grad_c_re, c_im=grad_c_im, d_skip=grad_d_skip, w_glu=grad_w_glu,
        b_glu=grad_b_glu, w_o_ssm=grad_w_o_ssm, w_up=grad_w_up, conv_w=grad_conv_w, conv_b=grad_conv_b,
        w_down=grad_w_down, norm_out=grad_norm_out, w_fin=grad_w_fin, b_fin=grad_b_fin)
    weights = dict(
        norm_mix=norm_mix, norm_ffn=norm_ffn, w_mod=w_mod, b_mod=b_mod, w_qkv=w_qkv, w_o_attn=w_o_attn,
        w_in_ssm=w_in_ssm, a_re=a_re, a_im=a_im, log_dt=log_dt, b_re=b_re, b_im=b_im, c_re=c_re, c_im=c_im,
        d_skip=d_skip, w_glu=w_glu, b_glu=b_glu, w_o_ssm=w_o_ssm, w_up=w_up, conv_w=conv_w, conv_b=conv_b,
        w_down=w_down, norm_out=norm_out, w_fin=w_fin, b_fin=b_fin)
    m_in = dict(
        norm_mix=m_norm_mix, norm_ffn=m_norm_ffn, w_mod=m_w_mod, b_mod=m_b_mod, w_qkv=m_w_qkv, w_o_attn=m_w_o_attn,
        w_in_ssm=m_w_in_ssm, a_re=m_a_re, a_im=m_a_im, log_dt=m_log_dt, b_re=m_b_re, b_im=m_b_im, c_re=m_c_re,
        c_im=m_c_im, d_skip=m_d_skip, w_glu=m_w_glu, b_glu=m_b_glu, w_o_ssm=m_w_o_ssm, w_up=m_w_up, conv_w=m_conv_w,
        conv_b=m_conv_b, w_down=m_w_down, norm_out=m_norm_out, w_fin=m_w_fin, b_fin=m_b_fin)
    v_in = dict(
        norm_mix=v_norm_mix, norm_ffn=v_norm_ffn, w_mod=v_w_mod, b_mod=v_b_mod, w_qkv=v_w_qkv, w_o_attn=v_w_o_attn,
        w_in_ssm=v_w_in_ssm, a_re=v_a_re, a_im=v_a_im, log_dt=v_log_dt, b_re=v_b_re, b_im=v_b_im, c_re=v_c_re,
        c_im=v_c_im, d_skip=v_d_skip, w_glu=v_w_glu, b_glu=v_b_glu, w_o_ssm=v_w_o_ssm, w_up=v_w_up, conv_w=v_conv_w,
        conv_b=v_conv_b, w_down=v_w_down, norm_out=v_norm_out, w_fin=v_w_fin, b_fin=v_b_fin)
    names = list(weights)
    for n_ in names:
        grads[n_] = grads[n_].reshape(weights[n_].shape)

    big = ("w_mod", "w_qkv", "w_o_attn", "w_in_ssm", "w_glu", "w_o_ssm", "w_up", "w_down", "w_fin")
    delta, new_m, new_v = {}, {}, {}
    for n_ in big:
        shp = weights[n_].shape
        two_d = lambda a: a.reshape(-1, shp[-1])
        d_, m_, v_ = adamw(two_d(weights[n_]), two_d(grads[n_]), two_d(m_in[n_]), two_d(v_in[n_]), name=f"adamw_{n_}")
        delta[n_], new_m[n_], new_v[n_] = d_.reshape(shp), m_.reshape(shp), v_.reshape(shp)
    rest = [n_ for n_ in names if n_ not in big]

    def pack(tree):
        f = jnp.concatenate([tree[n_].reshape(-1) for n_ in rest])
        pad_to = -(-f.shape[0] // (8 * width)) * (8 * width)
        return jnp.pad(f, (0, pad_to - f.shape[0]), constant_values=1.0).reshape(-1, width)

    d_, m_, v_ = adamw(pack(weights), pack(grads), pack(m_in), pack(v_in), name="adamw_small")
    off = 0
    for n_ in rest:
        sz = int(math.prod(weights[n_].shape))
        shp = weights[n_].shape
        delta[n_] = d_.reshape(-1)[off:off + sz].reshape(shp)
        new_m[n_] = m_.reshape(-1)[off:off + sz].reshape(shp)
        new_v[n_] = v_.reshape(-1)[off:off + sz].reshape(shp)
        off += sz

    return (loss, grad_x.reshape(B, S, D), *[grads[n_] for n_ in names], *[delta[n_] for n_ in names],
            *[new_m[n_] for n_ in names], *[new_v[n_] for n_ in names])
```

```python
import functools
import math

import jax
import jax.numpy as jnp
from jax import lax
from jax.experimental import pallas as pl
from jax.experimental.pallas import tpu as pltpu

F32 = jnp.float32
BF16 = jnp.bfloat16
MESH = pl.DeviceIdType.MESH

HEAD_DIM = 64
SSM_GROUP = 16
STATE = 64
GROUPS_PER_BLOCK = 8
SEGMENTS = 16
SCAN_UNROLL = 4
EPS = 1e-6
ADAM_LR = 0.001
ADAM_B1 = 0.9
ADAM_B2 = 0.999
ADAM_EPS = 1e-08
ADAM_WD = 0.01
ADAM_STEP = 10
N_CHIPS = 4
N_DEV = 8
V7X_VMEM_LIMIT = 56 * 1024 * 1024
ATT_BLOCK = 128
ATT_HEADS = 8
ATT_HEADS_BWD = 4


def _tile(n, prefs):
    for p in prefs:
        if n % p == 0:
            return p
    return n


def _params(sem, vmem=V7X_VMEM_LIMIT):
    return pltpu.CompilerParams(dimension_semantics=sem, vmem_limit_bytes=vmem)


def matmul(a, b, *, ta=False, tb=False, bias=None, out_dtype=F32, b_chips=False, out_chips=False, name):
    a_parts = a.shape[0] if a.ndim == 3 else 1
    if a_parts > 1:
        assert not ta
        M, K = a.shape[1], a_parts * a.shape[2]
    elif ta:
        K, M = a.shape
    else:
        M, K = a.shape
    b_parts = b.shape[0] if b_chips else 1
    b_rows, b_cols = (b.shape[1], b_parts * b.shape[2]) if b_chips else b.shape
    if tb:
        N, Kb = b_rows, b_cols
    else:
        Kb, N = b_rows, b_cols
    assert K == Kb, (a.shape, b.shape, ta, tb)
    n_cut = N // max(N_CHIPS if out_chips else 1, b_parts if not tb else 1)
    k_cut = K // max(b_parts if tb else 1, a_parts)
    tm = _tile(M, (1024, 512, 256, 128))
    tn = _tile(n_cut, (1024, 1408, 768, 512, 256, 128))
    tk = k_cut if k_cut <= 2816 else _tile(k_cut, (1024, 512, 256, 128))
    nk = K // tk
    npc = N // N_CHIPS // tn
    npb = N // b_parts // tn
    kpb = K // b_parts // tk
    kpa = K // a_parts // tk
    dims = (((0,) if ta else (1,), (1,) if tb else (0,)), ((), ()))

    def body(*refs):
        a_ref, b_ref = refs[:2]
        bias_ref = refs[2] if bias is not None else None
        o_ref = refs[-2] if nk > 1 else refs[-1]

        def finish(r):
            if bias_ref is not None:
                r = r + bias_ref[...]
            o_ref[...] = r.astype(o_ref.dtype)

        prod = lax.dot_general(a_ref[...].astype(BF16), b_ref[...].astype(BF16), dims, preferred_element_type=F32)
        if nk == 1:
            finish(prod)
            return
        acc_ref = refs[-1]
        k = pl.program_id(2)

        @pl.when(k == 0)
        def _():
            acc_ref[...] = prod

        @pl.when(k > 0)
        def _():
            acc_ref[...] += prod

        @pl.when(k == nk - 1)
        def _():
            finish(acc_ref[...])

    if a_parts > 1:
        a_spec = pl.BlockSpec((None, tm, tk), lambda i, j, k: (lax.div(k, kpa), i, lax.rem(k, kpa)))
    else:
        a_spec = pl.BlockSpec((tk, tm), lambda i, j, k: (k, i)) if ta else pl.BlockSpec((tm, tk), lambda i, j, k: (i, k))
    if not b_chips:
        b_spec = pl.BlockSpec((tn, tk), lambda i, j, k: (j, k)) if tb else pl.BlockSpec((tk, tn), lambda i, j, k: (k, j))
    elif tb:
        b_spec = pl.BlockSpec((None, tn, tk), lambda i, j, k: (lax.div(k, kpb), j, lax.rem(k, kpb)))
    else:
        b_spec = pl.BlockSpec((None, tk, tn), lambda i, j, k: (lax.div(j, npb), k, lax.rem(j, npb)))
    in_specs = [a_spec, b_spec]
    args = [a, b]
    if bias is not None:
        in_specs.append(pl.BlockSpec((1, tn), lambda i, j, k: (0, j)))
        args.append(bias.reshape(1, N).astype(F32))
    if out_chips:
        out_shape = jax.ShapeDtypeStruct((N_CHIPS, M, N // N_CHIPS), out_dtype)
        out_spec = pl.BlockSpec((None, tm, tn), lambda i, j, k: (lax.div(j, npc), i, lax.rem(j, npc)))
    else:
        out_shape = jax.ShapeDtypeStruct((M, N), out_dtype)
        out_spec = pl.BlockSpec((tm, tn), lambda i, j, k: (i, j))
    return pl.pallas_call(
        body, name=name,
        out_shape=out_shape,
        grid=(M // tm, N // tn, nk),
        in_specs=in_specs,
        out_specs=out_spec,
        scratch_shapes=[pltpu.VMEM((tm, tn), F32)] if nk > 1 else [],
        compiler_params=_params(("parallel", "parallel", "arbitrary")),
    )(*args)


def rowwise(fn, tiled, per_seq, glob, out_tiled, out_seq, *, B, S, name, rows=512):
    tm = _tile(S, (rows, 128, 64, 32, 16, 8))
    nt = S // tm
    n_in = len(tiled) + len(per_seq) + len(glob)
    n_ot = len(out_tiled)

    def body(*refs):
        ins = refs[:n_in]
        outs = refs[n_in:]
        vals = fn(*[r[...] for r in ins])
        if not isinstance(vals, (tuple, list)):
            vals = (vals,)
        assert len(vals) == len(outs), (name, len(vals), len(outs))
        for o_ref, v in zip(outs[:n_ot], vals[:n_ot]):
            o_ref[...] = v.astype(o_ref.dtype)
        t = pl.program_id(1)
        for o_ref, v in zip(outs[n_ot:], vals[n_ot:]):
            def first(o_ref=o_ref, v=v):
                o_ref[...] = v.astype(F32)

            def later(o_ref=o_ref, v=v):
                o_ref[...] += v.astype(F32)

            pl.when(t == 0)(first)
            pl.when(t > 0)(later)

    in_specs = [pl.BlockSpec((tm, a.shape[1]), lambda b, t: (b * nt + t, 0)) for a in tiled]
    in_specs += [pl.BlockSpec((None, 1, a.shape[1]), lambda b, t: (b, 0, 0)) for a in per_seq]
    in_specs += [pl.BlockSpec(a.shape, lambda b, t: (0,) * a.ndim) for a in glob]
    out_shape = [jax.ShapeDtypeStruct((B * S, w), dt) for w, dt in out_tiled]
    out_shape += [jax.ShapeDtypeStruct((B, 1, w), F32) for w in out_seq]
    out_specs = [pl.BlockSpec((tm, w), lambda b, t: (b * nt + t, 0)) for w, _ in out_tiled]
    out_specs += [pl.BlockSpec((None, 1, w), lambda b, t: (b, 0, 0)) for w in out_seq]
    res = pl.pallas_call(
        body, name=name, out_shape=out_shape, grid=(B, nt), in_specs=in_specs, out_specs=out_specs,
        compiler_params=_params(("parallel", "arbitrary")),
    )(*tiled, *[a.reshape(B, 1, a.shape[1]) for a in per_seq], *glob)
    res = list(res)
    for i in range(n_ot, len(res)):
        res[i] = res[i].reshape(B, res[i].shape[-1])
    return res


def _rms(x):
    r = lax.rsqrt(jnp.mean(x * x, axis=-1, keepdims=True) + EPS)
    return x * r, r


def norm_mod_fwd(x, g, sh, sc, *, B, S, name):
    def fn(x, sh, sc, g):
        xn, _ = _rms(x)
        return (xn * g) * (1.0 + sc) + sh

    return rowwise(fn, [x], [sh, sc], [g.reshape(1, -1)], [(x.shape[1], BF16)], [], B=B, S=S, name=name)[0]


def _norm_mod_bwd_math(dh, x, sc, g):
    xn, r = _rms(x)
    y = xn * g
    dy = dh * (1.0 + sc)
    dxn = dy * g
    dx = r * (dxn - xn * jnp.mean(dxn * xn, axis=-1, keepdims=True))
    dsh = jnp.sum(dh, axis=0, keepdims=True)
    dsc = jnp.sum(dh * y, axis=0, keepdims=True)
    dg = jnp.sum(dy * xn, axis=0, keepdims=True)
    return dx, dsh, dsc, dg


def norm_mod_bwd(dh, x, dres, g, sc, *, B, S, name):
    D = x.shape[1]

    def fn(dh, x, dres, sc, g):
        dx, dsh, dsc, dg = _norm_mod_bwd_math(dh.astype(F32), x, sc, g)
        return dres + dx, dsh, dsc, dg

    return rowwise(fn, [dh, x, dres], [sc], [g.reshape(1, -1)], [(D, F32)], [D, D, D], B=B, S=S, name=name)


def gate_res_fwd(x, y, gate, *, B, S, name):
    return rowwise(lambda x, y, g: x + g * y, [x, y], [gate], [], [(x.shape[1], F32)], [], B=B, S=S, name=name)[0]


def gate_res_bwd(dx, y, gate, *, B, S, name):
    D = dx.shape[1]

    def fn(dx, y, g):
        return g * dx, jnp.sum(dx * y, axis=0, keepdims=True)

    return rowwise(fn, [dx, y], [gate], [], [(D, BF16)], [D], B=B, S=S, name=name)


def final_loss(x, tgt, g, sh, sc, *, B, S, name):
    D = x.shape[1]

    def fn(x, tgt, sh, sc, g):
        xn, _ = _rms(x)
        y = (xn * g) * (1.0 + sc) + sh
        err = y - tgt
        loss = 0.5 * jnp.sum(err * err, axis=0, keepdims=True) * (1.0 / D)
        dx, dsh, dsc, dg = _norm_mod_bwd_math(err * (1.0 / D), x, sc, g)
        return dx, loss, dsh, dsc, dg

    return rowwise(fn, [x, tgt], [sh, sc], [g.reshape(1, -1)], [(D, F32)], [D, D, D, D], B=B, S=S, name=name)


def _gelu(y):
    c0 = math.sqrt(2.0 / math.pi)
    t = jnp.tanh(c0 * (y + 0.044715 * (y * y * y)))
    return 0.5 * y * (1.0 + t), t


def _sigmoid(s):
    return 1.0 / (1.0 + jnp.exp(-s))


def gelu_fwd(y, *, B, S, name):
    return rowwise(lambda y: _gelu(y)[0], [y], [], [], [(y.shape[1], BF16)], [], B=B, S=S, name=name)[0]


def glu_fwd(y, s, *, B, S, name):
    return rowwise(lambda y, s: _gelu(y)[0] * _sigmoid(s), [y, s], [], [], [(y.shape[1], BF16)], [], B=B, S=S,
                   name=name)[0]


def glu_bwd1(y, s, dg, *, B, S, name):
    D = y.shape[1]

    def fn(y, s, dg):
        z = _gelu(y)[0]
        sig = _sigmoid(s)
        ds = dg * z * sig * (1.0 - sig)
        return ds, dg * sig, jnp.sum(ds, axis=0, keepdims=True)

    return rowwise(fn, [y, s, dg], [], [], [(D, BF16), (D, F32)], [D], B=B, S=S, name=name)


def glu_bwd2(y, dz1, dz2, *, B, S, name):
    D = y.shape[1]
    c0 = math.sqrt(2.0 / math.pi)

    def fn(y, dz1, dz2):
        _, t = _gelu(y)
        dgelu = 0.5 * (1.0 + t) + 0.5 * y * (1.0 - t * t) * c0 * (1.0 + 3.0 * 0.044715 * y * y)
        return (dz1 + dz2) * dgelu

    return rowwise(fn, [y, dz1, dz2], [], [], [(D, F32)], [], B=B, S=S, name=name)[0]


def silu_rows(c, *, name):
    R, W = c.shape
    return rowwise(lambda c: c * _sigmoid(c), [c], [], [], [(W, F32)], [], B=1, S=R, name=name)[0]


def _shift_down(cur, h6, h7):
    rows = lax.broadcasted_iota(jnp.int32, cur.shape, 0)
    m1 = jnp.where(rows == 0, h7, pltpu.roll(cur, 1, 0))
    m2 = jnp.where(rows == 0, h6, jnp.where(rows == 1, h7, pltpu.roll(cur, 2, 0)))
    return m1, m2


def _conv3(cur, halo_ref, w_ref, has_prev):
    h6 = jnp.where(has_prev, halo_ref[6:7, :], 0.0)
    h7 = jnp.where(has_prev, halo_ref[7:8, :], 0.0)
    m1, m2 = _shift_down(cur, h6, h7)
    return w_ref[2:3, :] * cur + w_ref[1:2, :] * m1 + w_ref[0:1, :] * m2, m1, m2


def _conv_tiles(S, F):
    ts = _tile(S, (1024, 512, 256, 128, 64, 32, 16, 8))
    tn = _tile(F, (256, 128))
    return ts, tn, S // ts, F // tn


def conv_gate_fwd(up, cw, cb, *, B, S, name):
    F = up.shape[1] // 2
    ts, tn, nts, nF = _conv_tiles(S, F)
    hb = ts // 8

    def body(g_ref, gh_ref, v_ref, vh_ref, wg_ref, wv_ref, bg_ref, bv_ref, o_ref):
        has_prev = pl.program_id(2) > 0
        gc = _conv3(g_ref[...], gh_ref, wg_ref, has_prev)[0] + bg_ref[...]
        vc = _conv3(v_ref[...], vh_ref, wv_ref, has_prev)[0] + bv_ref[...]
        o_ref[...] = (gc * _sigmoid(gc) * vc).astype(o_ref.dtype)

    def cur(off):
        return pl.BlockSpec((ts, tn), lambda b, j, t: (b * nts + t, j + off))

    def halo(off):
        return pl.BlockSpec((8, tn), lambda b, j, t: (jnp.maximum((b * nts + t) * hb - 1, 0), j + off))

    def vec(rows, off):
        return pl.BlockSpec((rows, tn), lambda b, j, t: (0, j + off))

    return pl.pallas_call(
        body, name=name, out_shape=jax.ShapeDtypeStruct((B * S, F), BF16), grid=(B, nF, nts),
        in_specs=[cur(0), halo(0), cur(nF), halo(nF), vec(3, 0), vec(3, nF), vec(1, 0), vec(1, nF)],
        out_specs=pl.BlockSpec((ts, tn), lambda b, j, t: (b * nts + t, j)),
        compiler_params=_params(("parallel", "parallel", "arbitrary")),
    )(up, up, up, up, cw, cw, cb, cb)


def conv_gate_bwd1(up, dact, cw, cb, *, B, S, name):
    F = up.shape[1] // 2
    ts, tn, nts, nF = _conv_tiles(S, F)
    hb = ts // 8

    def body(g_ref, gh_ref, v_ref, vh_ref, da_ref, wg_ref, wv_ref, bg_ref, bv_ref, d_ref, db_ref):
        t = pl.program_id(2)
        has_prev = t > 0
        gc = _conv3(g_ref[...], gh_ref, wg_ref, has_prev)[0] + bg_ref[...]
        vc = _conv3(v_ref[...], vh_ref, wv_ref, has_prev)[0] + bv_ref[...]
        sig = _sigmoid(gc)
        da = da_ref[...]
        dg = da * vc * (sig * (1.0 + gc * (1.0 - sig)))
        dv = da * (gc * sig)
        d_ref[0] = dg
        d_ref[1] = dv
        part = jnp.concatenate([jnp.sum(dg, axis=0, keepdims=True), jnp.sum(dv, axis=0, keepdims=True)], axis=0)

        @pl.when(t == 0)
        def _():
            db_ref[...] = part

        @pl.when(t > 0)
        def _():
            db_ref[...] += part

    def cur(off):
        return pl.BlockSpec((ts, tn), lambda b, j, t: (b * nts + t, j + off))

    def halo(off):
        return pl.BlockSpec((8, tn), lambda b, j, t: (jnp.maximum((b * nts + t) * hb - 1, 0), j + off))

    def vec(rows, off):
        return pl.BlockSpec((rows, tn), lambda b, j, t: (0, j + off))

    return pl.pallas_call(
        body, name=name,
        out_shape=[jax.ShapeDtypeStruct((2, B * S, F), F32), jax.ShapeDtypeStruct((B, 2, F), F32)],
        grid=(B, nF, nts),
        in_specs=[cur(0), halo(0), cur(nF), halo(nF), cur(0), vec(3, 0), vec(3, nF), vec(1, 0), vec(1, nF)],
        out_specs=[pl.BlockSpec((2, ts, tn), lambda b, j, t: (0, b * nts + t, j)),
                   pl.BlockSpec((None, 2, tn), lambda b, j, t: (b, 0, j))],
        compiler_params=_params(("parallel", "parallel", "arbitrary")),
    )(up, up, up, up, dact, cw, cw, cb, cb)


def conv_bwd2(d3, up, cw, *, B, S, name):
    F = up.shape[1] // 2
    ts, tn, nts, nF = _conv_tiles(S, F)
    hb = ts // 8
    last8 = B * S // 8 - 1

    def body(d_ref, da_ref, u_ref, uh_ref, w_ref, o_ref, dw_ref):
        t = pl.program_id(3)
        d = d_ref[...]
        has_next = t < nts - 1
        a0 = jnp.where(has_next, da_ref[0:1, :], 0.0)
        a1 = jnp.where(has_next, da_ref[1:2, :], 0.0)
        rows = lax.broadcasted_iota(jnp.int32, d.shape, 0)
        p1 = jnp.where(rows == ts - 1, a0, pltpu.roll(d, ts - 1, 0))
        p2 = jnp.where(rows == ts - 1, a1, jnp.where(rows == ts - 2, a0, pltpu.roll(d, ts - 2, 0)))
        o_ref[...] = (w_ref[2:3, :] * d + w_ref[1:2, :] * p1 + w_ref[0:1, :] * p2).astype(o_ref.dtype)
        u = u_ref[...]
        has_prev = t > 0
        h6 = jnp.where(has_prev, uh_ref[6:7, :], 0.0)
        h7 = jnp.where(has_prev, uh_ref[7:8, :], 0.0)
        m1, m2 = _shift_down(u, h6, h7)
        part = jnp.concatenate([jnp.sum(d * m2, axis=0, keepdims=True), jnp.sum(d * m1, axis=0, keepdims=True),
                                jnp.sum(d * u, axis=0, keepdims=True)], axis=0)

        @pl.when(t == 0)
        def _():
            dw_ref[...] = part

        @pl.when(t > 0)
        def _():
            dw_ref[...] += part

    return pl.pallas_call(
        body, name=name,
        out_shape=[jax.ShapeDtypeStruct((B * S, 2 * F), BF16), jax.ShapeDtypeStruct((B, 3, 2 * F), F32)],
        grid=(B, 2, nF, nts),
        in_specs=[
            pl.BlockSpec((None, ts, tn), lambda b, g, j, t: (g, b * nts + t, j)),
            pl.BlockSpec((None, 8, tn), lambda b, g, j, t: (g, jnp.minimum((b * nts + t + 1) * hb, last8), j)),
            pl.BlockSpec((ts, tn), lambda b, g, j, t: (b * nts + t, g * nF + j)),
            pl.BlockSpec((8, tn), lambda b, g, j, t: (jnp.maximum((b * nts + t) * hb - 1, 0), g * nF + j)),
            pl.BlockSpec((3, tn), lambda b, g, j, t: (0, g * nF + j)),
        ],
        out_specs=[pl.BlockSpec((ts, tn), lambda b, g, j, t: (b * nts + t, g * nF + j)),
                   pl.BlockSpec((None, 3, tn), lambda b, g, j, t: (b, 0, g * nF + j))],
        compiler_params=_params(("parallel", "parallel", "parallel", "arbitrary")),
    )(d3, d3, up, up, cw)


def conv_gate_bwd(up, dact, cw, cb, *, B, S, name):
    F = up.shape[1] // 2
    tn = _tile(F, (256, 128))
    nF = F // tn

    def body(g_ref, v_ref, da_ref, wg_ref, wv_ref, bg_ref, bv_ref, o_ref, dw_ref, db_ref):
        rows = lax.broadcasted_iota(jnp.int32, (S, tn), 0)

        def earlier(x, k):
            return jnp.where(rows >= k, pltpu.roll(x, k, 0), 0.0)

        def later(x, k):
            return jnp.where(rows < S - k, pltpu.roll(x, S - k, 0), 0.0)

        def conv(x, w_ref):
            x1, x2 = earlier(x, 1), earlier(x, 2)
            return w_ref[2:3, :] * x + w_ref[1:2, :] * x1 + w_ref[0:1, :] * x2, x1, x2

        def back(d, x, x1, x2, w_ref, half):
            o_ref[half] = (w_ref[2:3, :] * d + w_ref[1:2, :] * later(d, 1) + w_ref[0:1, :] * later(d, 2)
                           ).astype(o_ref.dtype)
            dw_ref[half] = jnp.concatenate([jnp.sum(d * x2, axis=0, keepdims=True),
                                            jnp.sum(d * x1, axis=0, keepdims=True),
                                            jnp.sum(d * x, axis=0, keepdims=True)], axis=0)
            return jnp.sum(d, axis=0, keepdims=True)

        g, v, da = g_ref[...], v_ref[...], da_ref[...]
        gc, g1, g2 = conv(g, wg_ref)
        vc, v1, v2 = conv(v, wv_ref)
        gc = gc + bg_ref[...]
        vc = vc + bv_ref[...]
        sig = _sigmoid(gc)
        dg = da * vc * (sig * (1.0 + gc * (1.0 - sig)))
        dv = da * (gc * sig)
        db_ref[...] = jnp.concatenate([back(dg, g, g1, g2, wg_ref, 0), back(dv, v, v1, v2, wv_ref, 1)], axis=0)

    def cols(off):
        return pl.BlockSpec((S, tn), lambda b, j: (b, j + off))

    def vec(rows, off):
        return pl.BlockSpec((rows, tn), lambda b, j: (0, j + off))

    return pl.pallas_call(
        body, name=name,
        out_shape=[jax.ShapeDtypeStruct((2, B * S, F), BF16), jax.ShapeDtypeStruct((B, 2, 3, F), F32),
                   jax.ShapeDtypeStruct((B, 2, F), F32)],
        grid=(B, nF),
        in_specs=[cols(0), cols(nF), cols(0), vec(3, 0), vec(3, nF), vec(1, 0), vec(1, nF)],
        out_specs=[pl.BlockSpec((2, S, tn), lambda b, j: (0, b, j)),
                   pl.BlockSpec((None, 2, 3, tn), lambda b, j: (b, 0, 0, j)),
                   pl.BlockSpec((None, 2, tn), lambda b, j: (b, 0, j))],
        compiler_params=_params(("parallel", "parallel")),
    )(up, up, dact, cw, cw, cb, cb)


MASKED_LOG = -1e30


def _split2(x):
    bits = lax.bitcast_convert_type(x, jnp.uint32) & jnp.uint32(0xFFFF0000)
    hi = lax.bitcast_convert_type(bits, F32)
    return hi.astype(BF16), (x - hi).astype(BF16)


def _split_dot(x, m):
    hi, lo = _split2(x)
    return jnp.dot(hi, m, preferred_element_type=F32) + jnp.dot(lo, m, preferred_element_type=F32)


def _nt(a, b):
    return lax.dot_general(a, b, (((1,), (1,)), ((), ())), preferred_element_type=F32)


def _tn(a, b):
    return lax.dot_general(a, b, (((0,), (0,)), ((), ())), preferred_element_type=F32)


def _att_scores(q, k, mask, prescaled=False):
    z = _nt(q, k)
    if not prescaled:
        z = z * (HEAD_DIM ** -0.5)
    e = jnp.exp(-jnp.abs(z))
    sp = jnp.log(1.0 + e)
    lb = jnp.minimum(z, 0.0) - sp
    l1 = lb - z
    if mask is not None:
        lb = jnp.where(mask, lb, MASKED_LOG)
        l1 = jnp.where(mask, l1, 0.0)
    return z, lb, l1, e


def _col_to_row(col, eye):
    return jnp.sum(jnp.where(eye, col, 0.0), axis=0, keepdims=True)


def _row_to_col(row, eye):
    return jnp.sum(jnp.where(eye, row, 0.0), axis=1, keepdims=True)


def attn_fwd(q, k, v, *, name):
    B, H, S, dh = q.shape
    T = ATT_BLOCK
    nq = S // T

    G = _tile(H, (ATT_HEADS, 2))

    def body(q_ref, k_ref, v_ref, o_ref, l_ref):
        r = lax.broadcasted_iota(jnp.int32, (T, T), 0)
        c = lax.broadcasted_iota(jnp.int32, (T, T), 1)
        later = (r > c).astype(BF16)
        eye = r == c
        diag = c < r
        blk = lax.broadcasted_iota(jnp.int32, (nq, T), 0)

        later2 = jnp.concatenate([later, later], axis=0)

        def scores(g, qb, k0, mask):
            _, lb, l1, _ = _att_scores(qb, k_ref[g, pl.ds(k0, T), :], mask)
            return lb, jnp.concatenate(_split2(l1), axis=1), jnp.sum(l1, axis=1, keepdims=True)

        def weigh_all(k0, sc, st):
            suf = jnp.dot(jnp.concatenate([s[1] for s in sc], axis=0), later2, preferred_element_type=F32)
            out = []
            for g in range(G):
                lb, _, rowsum = sc[g]
                acc, run = st[g]
                w = jnp.exp(lb + suf[g * T:(g + 1) * T] + run)
                acc = acc + jnp.dot(w.astype(BF16), v_ref[g, pl.ds(k0, T), :], preferred_element_type=F32)
                out.append((acc, run + rowsum))
            return tuple(out)

        def qblock(i, totals):
            q0 = pl.multiple_of(i * T, T)
            qbs = [q_ref[g, pl.ds(q0, T), :] for g in range(G)]
            sc0 = tuple(scores(g, qbs[g], q0, diag) for g in range(G))
            st0 = tuple((jnp.zeros((T, dh), F32), jnp.zeros((T, 1), F32)) for _ in range(G))

            def kblock(jj, carry):
                sc, st = carry
                k_next = pl.multiple_of((i - jj) * T, T)
                k_cur = pl.multiple_of((i - jj + 1) * T, T)
                st = weigh_all(k_cur, sc, st)
                sc_next = tuple(scores(g, qbs[g], k_next, None) for g in range(G))
                return sc_next, st

            sc, st = lax.fori_loop(1, i + 1, kblock, (sc0, st0))
            st = weigh_all(0, sc, st)
            for g in range(G):
                o_ref[g, pl.ds(q0, T), :] = st[g][0]
            return tuple(jnp.where(blk == i, _col_to_row(st[g][1], eye), totals[g]) for g in range(G))

        totals = lax.fori_loop(0, nq, qblock, tuple(jnp.zeros((nq, T), F32) for _ in range(G)))
        for g in range(G):
            l_ref[g] = totals[g]

    spec = pl.BlockSpec((None, G, S, dh), lambda b, h: (b, h, 0, 0))
    lspec = pl.BlockSpec((None, G, nq, T), lambda b, h: (b, h, 0, 0))
    return pl.pallas_call(
        body, name=name,
        out_shape=[jax.ShapeDtypeStruct((B, H, S, dh), F32), jax.ShapeDtypeStruct((B, H, nq, T), F32)],
        grid=(B, H // G), in_specs=[spec, spec, spec], out_specs=[spec, lspec],
        compiler_params=_params(("parallel", "parallel")),
    )(q, k, v)


def attn_bwd(q, k, v, ltot, do, *, name):
    B, H, S, dh = q.shape
    T = ATT_BLOCK
    nq = S // T
    scale = HEAD_DIM ** -0.5

    G = _tile(H, (ATT_HEADS_BWD, 2))

    def body(q_ref, k_ref, v_ref, l_ref, do_ref, dq_ref, dk_ref, dv_ref, dk_acc, dv_acc):
        r = lax.broadcasted_iota(jnp.int32, (T, T), 0)
        c = lax.broadcasted_iota(jnp.int32, (T, T), 1)
        upto = (r <= c).astype(BF16)
        before = (r < c).astype(BF16)
        upto2 = jnp.concatenate([upto, upto], axis=0)
        before2 = jnp.concatenate([before, before], axis=0)
        eye = r == c
        diag = c < r
        blk = lax.broadcasted_iota(jnp.int32, (nq, T), 0)
        dk_acc[...] = jnp.zeros_like(dk_acc)
        dv_acc[...] = jnp.zeros_like(dv_acc)

        def scores(g, qb, dob, k0, mask):
            z, lb, l1, e = _att_scores(qb, k_ref[g, pl.ds(k0, T), :], mask)
            inv = 1.0 / (1.0 + e)
            small = e * inv
            pos = z >= 0.0
            beta = jnp.where(pos, inv, small)
            omb = jnp.where(pos, small, inv)
            if mask is not None:
                beta = jnp.where(mask, beta, 0.0)
            dw = _nt(dob, v_ref[g, pl.ds(k0, T), :])
            return lb, jnp.concatenate(_split2(l1), axis=1), jnp.sum(l1, axis=1, keepdims=True), dw, beta, omb

        def grads_all(qbs, dobs, tots, k0, sc, st):
            pre = jnp.dot(jnp.concatenate([s[1] for s in sc], axis=0), upto2, preferred_element_type=F32)
            dlws = []
            for g in range(G):
                lb = sc[g][0]
                w = jnp.exp(lb + (tots[g] - (pre[g * T:(g + 1) * T] + st[g][1])))
                dv_acc[g, pl.ds(k0, T), :] += _tn(w.astype(BF16), dobs[g])
                dlws.append(sc[g][3] * w)
            pre_d = jnp.dot(jnp.concatenate([jnp.concatenate(_split2(d), axis=1) for d in dlws], axis=0), before2,
                            preferred_element_type=F32)
            out = []
            for g in range(G):
                _, _, rowsum, _, beta, omb = sc[g]
                dq, run_l, run_d = st[g]
                dl1 = pre_d[g * T:(g + 1) * T] + run_d
                dz = ((dlws[g] * omb - dl1 * beta) * scale).astype(BF16)
                dq = dq + jnp.dot(dz, k_ref[g, pl.ds(k0, T), :], preferred_element_type=F32)
                dk_acc[g, pl.ds(k0, T), :] += _tn(dz, qbs[g])
                out.append((dq, run_l + rowsum, run_d + jnp.sum(dlws[g], axis=1, keepdims=True)))
            return tuple(out)

        def block_inputs(i, q0):
            qbs = [q_ref[g, pl.ds(q0, T), :] for g in range(G)]
            dobs = [do_ref[g, pl.ds(q0, T), :] for g in range(G)]
            tots = [_row_to_col(jnp.sum(jnp.where(blk == i, l_ref[g], 0.0), axis=0, keepdims=True), eye)
                    for g in range(G)]
            z1 = jnp.zeros((T, 1), F32)
            return qbs, dobs, tots, tuple((jnp.zeros((T, dh), F32), z1, z1) for _ in range(G))

        qbs, dobs, tots, st = block_inputs(0, 0)
        st = grads_all(qbs, dobs, tots, 0, tuple(scores(g, qbs[g], dobs[g], 0, diag) for g in range(G)), st)
        for g in range(G):
            dq_ref[g, 0:T, :] = st[g][0].astype(dq_ref.dtype)

        def qblock(i, carry0):
            q0 = pl.multiple_of(i * T, T)
            qbs, dobs, tots, st0 = block_inputs(i, q0)
            sc0 = tuple(scores(g, qbs[g], dobs[g], 0, None) for g in range(G))

            def kblock(j, carry):
                sc, st = carry
                k_cur = pl.multiple_of(j * T, T)
                k_next = pl.multiple_of((j + 1) * T, T)
                sc_next = tuple(scores(g, qbs[g], dobs[g], k_next, None) for g in range(G))
                st = grads_all(qbs, dobs, tots, k_cur, sc, st)
                return sc_next, st

            sc, st = lax.fori_loop(0, i - 1, kblock, (sc0, st0))
            k_last = pl.multiple_of((i - 1) * T, T)
            st = grads_all(qbs, dobs, tots, k_last, sc, st)
            sc_diag = tuple(scores(g, qbs[g], dobs[g], q0, diag) for g in range(G))
            st = grads_all(qbs, dobs, tots, q0, sc_diag, st)
            for g in range(G):
                dq_ref[g, pl.ds(q0, T), :] = st[g][0].astype(dq_ref.dtype)
            return carry0

        lax.fori_loop(1, nq, qblock, 0)
        dk_ref[...] = dk_acc[...].astype(dk_ref.dtype)
        dv_ref[...] = dv_acc[...].astype(dv_ref.dtype)

    spec = pl.BlockSpec((None, G, S, dh), lambda b, h: (b, h, 0, 0))
    lspec = pl.BlockSpec((None, G, nq, T), lambda b, h: (b, h, 0, 0))
    shp = jax.ShapeDtypeStruct((B, H, S, dh), BF16)
    return pl.pallas_call(
        body, name=name, out_shape=[shp, shp, shp], grid=(B, H // G),
        in_specs=[spec, spec, spec, lspec, spec], out_specs=[spec] * 3,
        scratch_shapes=[pltpu.VMEM((G, S, dh), F32), pltpu.VMEM((G, S, dh), F32)],
        compiler_params=_params(("parallel", "parallel")),
    )(q, k, v, ltot, do)


def _wide_consts(T, W):
    r = lax.broadcasted_iota(jnp.int32, (W, W), 0)
    c = lax.broadcasted_iota(jnp.int32, (W, W), 1)
    two = lambda m: jnp.concatenate([m.astype(BF16)] * 2, axis=0)
    qrow = lax.broadcasted_iota(jnp.int32, (T, W), 0)
    kcol = lax.broadcasted_iota(jnp.int32, (T, W), 1)
    er = lax.broadcasted_iota(jnp.int32, (T, T), 0)
    ec = lax.broadcasted_iota(jnp.int32, (T, T), 1)
    return two(r > c), two(r <= c), two(r < c), qrow, kcol, er == ec


def attn_fwd_wide(q, k, v, shards=(), slots=(), *, name):
    B, H, S, dh = q.shape
    T = ATT_BLOCK
    W = 2 * T
    nq = S // T
    G = _tile(H, (ATT_HEADS, 2))
    n = len(shards)
    n_steps = B * (H // G)

    def body(*refs):
        q_ref, k_ref, v_ref = refs[:3]
        o_ref, l_ref = refs[3 + 2 * n:5 + 2 * n]
        step_id = pl.program_id(0) * (H // G) + pl.program_id(1)
        if n:
            gather = _ShardGather(refs[3:3 + n], refs[5 + 2 * n:5 + 3 * n], *refs[5 + 3 * n:])
            pl.when(step_id == 0)(gather.send)
            pl.when(step_id == n_steps - 1)(gather.forward)
        later2, _, _, qrow, kcol, eye = _wide_consts(T, W)
        blk = lax.broadcasted_iota(jnp.int32, (nq, T), 0)

        def step(qbs, k0, st, mask):
            parts, lbs, sums = [], [], []
            for g in range(G):
                _, lb, l1, _ = _att_scores(qbs[g], k_ref[g, pl.ds(k0, W), :], mask, prescaled=True)
                parts.append(jnp.concatenate(_split2(l1), axis=1))
                lbs.append(lb)
                sums.append(jnp.sum(l1, axis=1, keepdims=True))
            suf = jnp.dot(jnp.concatenate(parts, axis=0), later2, preferred_element_type=F32)
            out = []
            for g in range(G):
                acc, run = st[g]
                w = jnp.exp(lbs[g] + suf[g * T:(g + 1) * T] + run)
                acc = acc + jnp.dot(w.astype(BF16), v_ref[g, pl.ds(k0, W), :], preferred_element_type=F32)
                out.append((acc, run + sums[g]))
            return tuple(out)

        def qblock(i, totals):
            q0 = pl.multiple_of(i * T, T)
            qbs = [q_ref[g, pl.ds(q0, T), :] * (HEAD_DIM ** -0.5) for g in range(G)]
            half = jnp.right_shift(i, 1)
            last = half * W
            k_last = pl.multiple_of(last, W)
            mask = (k_last + kcol) < (q0 + qrow)
            st = tuple((jnp.zeros((T, dh), F32), jnp.zeros((T, 1), F32)) for _ in range(G))
            st = step(qbs, k_last, st, mask)

            def kblock(jj, st):
                return step(qbs, pl.multiple_of(last - jj * W, W), st, None)

            st = lax.fori_loop(1, half + 1, kblock, st)
            for g in range(G):
                o_ref[g, pl.ds(q0, T), :] = st[g][0]
            return tuple(jnp.where(blk == i, _col_to_row(st[g][1], eye), totals[g]) for g in range(G))

        totals = lax.fori_loop(0, nq, qblock, tuple(jnp.zeros((nq, T), F32) for _ in range(G)))
        for g in range(G):
            l_ref[g] = totals[g]
        if n:
            pl.when(step_id == n_steps - 1)(gather.finish)

    spec = pl.BlockSpec((None, G, S, dh), lambda b, h: (b, h, 0, 0))
    lspec = pl.BlockSpec((None, G, nq, T), lambda b, h: (b, h, 0, 0))
    dma = pltpu.SemaphoreType.DMA
    return pl.pallas_call(
        body, name=name,
        out_shape=[jax.ShapeDtypeStruct((B, H, S, dh), F32), jax.ShapeDtypeStruct((B, H, nq, T), F32)]
        + [jax.ShapeDtypeStruct(s.shape, s.dtype) for s in slots],
        grid=(B, H // G), in_specs=[spec, spec, spec] + _any_specs(2 * n), out_specs=[spec, lspec] + _any_specs(n),
        input_output_aliases={3 + n + i: 2 + i for i in range(n)},
        scratch_shapes=[dma((3 * n,))] * 4 if n else [],
        compiler_params=_params(("arbitrary", "arbitrary")),
    )(q, k, v, *shards, *slots)


def attn_bwd_wide(q, k, v, ltot, do, *, name):
    B, H, S, dh = q.shape
    T = ATT_BLOCK
    W = 2 * T
    nq = S // T
    scale = HEAD_DIM ** -0.5
    G = _tile(H, (ATT_HEADS_BWD, 2))

    def body(q_ref, k_ref, v_ref, l_ref, do_ref, dq_ref, dk_ref, dv_ref, dk_acc, dv_acc):
        _, upto2, before2, qrow, kcol, eye = _wide_consts(T, W)
        blk = lax.broadcasted_iota(jnp.int32, (nq, T), 0)
        dk_acc[...] = jnp.zeros_like(dk_acc)
        dv_acc[...] = jnp.zeros_like(dv_acc)

        def step(qbs, dobs, tots, k0, st, mask):
            sc = []
            for g in range(G):
                z, lb, l1, e = _att_scores(qbs[g], k_ref[g, pl.ds(k0, W), :], mask, prescaled=True)
                inv = 1.0 / (1.0 + e)
                small = e * inv
                pos = z >= 0.0
                beta = jnp.where(pos, inv, small)
                omb = jnp.where(pos, small, inv)
                if mask is not None:
                    beta = jnp.where(mask, beta, 0.0)
                dw = _nt(dobs[g], v_ref[g, pl.ds(k0, W), :])
                sc.append((lb, jnp.concatenate(_split2(l1), axis=1), jnp.sum(l1, axis=1, keepdims=True), dw, beta, omb))
            pre = jnp.dot(jnp.concatenate([s[1] for s in sc], axis=0), upto2, preferred_element_type=F32)
            dlws = []
            for g in range(G):
                w = jnp.exp(sc[g][0] + (tots[g] - (pre[g * T:(g + 1) * T] + st[g][1])))
                dv_acc[g, pl.ds(k0, W), :] += _tn(w.astype(BF16), dobs[g])
                dlws.append(sc[g][3] * w)
            pre_d = jnp.dot(jnp.concatenate([jnp.concatenate(_split2(d), axis=1) for d in dlws], axis=0), before2,
                            preferred_element_type=F32)
            out = []
            for g in range(G):
                _, _, rowsum, _, beta, omb = sc[g]
                dq, run_l, run_d = st[g]
                dl1 = pre_d[g * T:(g + 1) * T] + run_d
                dz = (dlws[g] * omb - dl1 * beta).astype(BF16)
                dq = dq + jnp.dot(dz, k_ref[g, pl.ds(k0, W), :], preferred_element_type=F32)
                dk_acc[g, pl.ds(k0, W), :] += _tn(dz, qbs[g])
                out.append((dq, run_l + rowsum, run_d + jnp.sum(dlws[g], axis=1, keepdims=True)))
            return tuple(out)

        def qblock(i, carry0):
            q0 = pl.multiple_of(i * T, T)
            qbs = [q_ref[g, pl.ds(q0, T), :] * scale for g in range(G)]
            dobs = [do_ref[g, pl.ds(q0, T), :] for g in range(G)]
            tots = [_row_to_col(jnp.sum(jnp.where(blk == i, l_ref[g], 0.0), axis=0, keepdims=True), eye)
                    for g in range(G)]
            z1 = jnp.zeros((T, 1), F32)
            st = tuple((jnp.zeros((T, dh), F32), z1, z1) for _ in range(G))

            def kblock(j, st):
                return step(qbs, dobs, tots, pl.multiple_of(j * W, W), st, None)

            half = jnp.right_shift(i, 1)
            st = lax.fori_loop(0, half, kblock, st)
            k_last = pl.multiple_of(half * W, W)
            st = step(qbs, dobs, tots, k_last, st, (k_last + kcol) < (q0 + qrow))
            for g in range(G):
                dq_ref[g, pl.ds(q0, T), :] = (st[g][0] * scale).astype(dq_ref.dtype)
            return carry0

        lax.fori_loop(0, nq, qblock, 0)
        dk_ref[...] = dk_acc[...].astype(dk_ref.dtype)
        dv_ref[...] = dv_acc[...].astype(dv_ref.dtype)

    spec = pl.BlockSpec((None, G, S, dh), lambda b, h: (b, h, 0, 0))
    lspec = pl.BlockSpec((None, G, nq, T), lambda b, h: (b, h, 0, 0))
    shp = jax.ShapeDtypeStruct((B, H, S, dh), BF16)
    return pl.pallas_call(
        body, name=name, out_shape=[shp, shp, shp], grid=(B, H // G),
        in_specs=[spec, spec, spec, lspec, spec], out_specs=[spec] * 3,
        scratch_shapes=[pltpu.VMEM((G, S, dh), F32), pltpu.VMEM((G, S, dh), F32)],
        compiler_params=_params(("parallel", "parallel")),
    )(q, k, v, ltot, do)


def _cmul(ar, ai, br, bi):
    return ar * br - ai * bi, ar * bi + ai * br


def _cpow(lr, li, n):
    rr, ri = None, None
    br, bi = lr, li
    while n:
        if n & 1:
            rr, ri = (br, bi) if rr is None else _cmul(rr, ri, br, bi)
        n >>= 1
        if n:
            br, bi = _cmul(br, bi, br, bi)
    return rr, ri


def _ssm_scan(sr, si, lr, li, n_steps, reverse):
    W = sr.shape[1]
    R = SEGMENTS
    lim = -li if reverse else li
    zero = jnp.zeros((R, W), F32)

    def row(k):
        i = (n_steps - 1 - k) if reverse else k
        return pl.multiple_of(i * R, R)

    def local(k, st):
        cr, ci = st
        r0 = row(k)
        pr, pi = _cmul(lr, lim, cr, ci)
        nr = pr + sr[pl.ds(r0, R), :]
        ni = pi + si[pl.ds(r0, R), :]
        sr[pl.ds(r0, R), :] = nr
        si[pl.ds(r0, R), :] = ni
        return nr, ni

    er, ei = lax.fori_loop(0, n_steps, local, (zero, zero), unroll=SCAN_UNROLL)
    lnr, lni = _cpow(lr, lim, n_steps)
    rows = lax.broadcasted_iota(jnp.int32, (R, W), 0)
    cr, ci = zero, zero
    for step in range(1, R):
        tr, ti = _cmul(lnr, lni, cr, ci)
        tr, ti = tr + er, ti + ei
        if reverse:
            seg = R - 1 - step
            tr, ti = pltpu.roll(tr, R - 1, 0), pltpu.roll(ti, R - 1, 0)
        else:
            seg = step
            tr, ti = pltpu.roll(tr, 1, 0), pltpu.roll(ti, 1, 0)
        cr = jnp.where(rows == seg, tr, cr)
        ci = jnp.where(rows == seg, ti, ci)

    def fix(k, st):
        pr, pi = st
        r0 = row(k)
        ar, ai = _cmul(pr, pi, cr, ci)
        sr[pl.ds(r0, R), :] += ar
        si[pl.ds(r0, R), :] += ai
        return _cmul(lr, lim, pr, pi)

    lax.fori_loop(0, n_steps, fix, (lr, lim), unroll=SCAN_UNROLL)
    return cr, ci


def _ssm_specs(S, W):
    CH = GROUPS_PER_BLOCK * SSM_GROUP
    return dict(
        rows=pl.BlockSpec((S, CH), lambda b, j: (b, j)),
        b=pl.BlockSpec((None, CH, W), lambda b, j: (j, 0, 0)),
        c=pl.BlockSpec((None, W, CH), lambda b, j: (j, 0, 0)),
        lam=pl.BlockSpec((None, SEGMENTS, W), lambda b, j: (j, 0, 0)),
        vec=pl.BlockSpec((1, CH), lambda b, j: (0, j)),
    )


def ssm_fwd(u, bre, bim, cre, cim, lr8, li8, dsk, *, B, S, name):
    D = u.shape[1]
    J, CH, W = bre.shape
    n_steps = S // SEGMENTS
    sp = _ssm_specs(S, W)

    def body(u_ref, bre_ref, bim_ref, cre_ref, cim_ref, lr_ref, li_ref, dsk_ref, y_ref, sr, si):
        u = u_ref[...]
        ub = u.astype(BF16)
        sr[...] = jnp.dot(ub, bre_ref[...], preferred_element_type=F32)
        si[...] = jnp.dot(ub, bim_ref[...], preferred_element_type=F32)
        _ssm_scan(sr, si, lr_ref[...], li_ref[...], n_steps, False)
        y = jnp.dot(sr[...].astype(BF16), cre_ref[...], preferred_element_type=F32)
        y = y - jnp.dot(si[...].astype(BF16), cim_ref[...], preferred_element_type=F32)
        y_ref[...] = y + dsk_ref[...] * u

    return pl.pallas_call(
        body, name=name, out_shape=jax.ShapeDtypeStruct((B * S, D), F32), grid=(B, J),
        in_specs=[sp["rows"], sp["b"], sp["b"], sp["c"], sp["c"], sp["lam"], sp["lam"], sp["vec"]],
        out_specs=sp["rows"],
        scratch_shapes=[pltpu.VMEM((S, W), F32), pltpu.VMEM((S, W), F32)],
        compiler_params=_params(("parallel", "parallel")),
    )(u, bre, bim, cre, cim, lr8, li8, dsk)


def ssm_bwd(u, dy, bre, bim, cre, cim, lr8, li8, dsk, *, B, S, name):
    D = u.shape[1]
    J, CH, W = bre.shape
    n_steps = S // SEGMENTS
    sp = _ssm_specs(S, W)

    def body(u_ref, dy_ref, bre_ref, bim_ref, cre_ref, cim_ref, lr_ref, li_ref, dsk_ref,
             du_ref, dbre_ref, dbim_ref, dcre_ref, dcim_ref, dlr_ref, dli_ref, ddsk_ref, sr, si, ar, ai):
        u = u_ref[...]
        dy = dy_ref[...]
        ub = u.astype(BF16)
        dyb = dy.astype(BF16)
        lr, li = lr_ref[...], li_ref[...]
        sr[...] = jnp.dot(ub, bre_ref[...], preferred_element_type=F32)
        si[...] = jnp.dot(ub, bim_ref[...], preferred_element_type=F32)
        cr, ci = _ssm_scan(sr, si, lr, li, n_steps, False)
        ar[...] = _nt(dyb, cre_ref[...])
        ai[...] = -_nt(dyb, cim_ref[...])
        _ssm_scan(ar, ai, lr, li, n_steps, True)

        def dlam(k, st):
            dr, di = st
            r0 = pl.multiple_of((k + 1) * SEGMENTS, SEGMENTS)
            p0 = pl.multiple_of(k * SEGMENTS, SEGMENTS)
            pr, pi = sr[pl.ds(p0, SEGMENTS), :], si[pl.ds(p0, SEGMENTS), :]
            xr, xi = ar[pl.ds(r0, SEGMENTS), :], ai[pl.ds(r0, SEGMENTS), :]
            return dr + pr * xr + pi * xi, di + pr * xi - pi * xr

        xr, xi = ar[0:SEGMENTS, :], ai[0:SEGMENTS, :]
        dr, di = lax.fori_loop(0, n_steps - 1, dlam, (cr * xr + ci * xi, cr * xi - ci * xr), unroll=SCAN_UNROLL)
        dlr_ref[...] = dr
        dli_ref[...] = di
        arb = ar[...].astype(BF16)
        aib = ai[...].astype(BF16)
        du_ref[...] = _nt(arb, bre_ref[...]) + _nt(aib, bim_ref[...]) + dsk_ref[...] * dy
        dbre_ref[...] = _tn(ub, arb)
        dbim_ref[...] = _tn(ub, aib)
        dcre_ref[...] = _tn(sr[...].astype(BF16), dyb)
        dcim_ref[...] = -_tn(si[...].astype(BF16), dyb)
        ddsk_ref[...] = jnp.sum(dy * u, axis=0, keepdims=True)

    def per(shape):
        return pl.BlockSpec((None, None) + shape, lambda b, j: (b, j, 0, 0))

    return pl.pallas_call(
        body, name=name,
        out_shape=[jax.ShapeDtypeStruct((B * S, D), F32),
                   jax.ShapeDtypeStruct((B, J, CH, W), F32), jax.ShapeDtypeStruct((B, J, CH, W), F32),
                   jax.ShapeDtypeStruct((B, J, W, CH), F32), jax.ShapeDtypeStruct((B, J, W, CH), F32),
                   jax.ShapeDtypeStruct((B, J, SEGMENTS, W), F32), jax.ShapeDtypeStruct((B, J, SEGMENTS, W), F32),
                   jax.ShapeDtypeStruct((B, J, 1, CH), F32)],
        grid=(B, J),
        in_specs=[sp["rows"], sp["rows"], sp["b"], sp["b"], sp["c"], sp["c"], sp["lam"], sp["lam"], sp["vec"]],
        out_specs=[sp["rows"], per((CH, W)), per((CH, W)), per((W, CH)), per((W, CH)), per((SEGMENTS, W)),
                   per((SEGMENTS, W)),
                   per((1, CH))],
        scratch_shapes=[pltpu.VMEM((S, W), F32)] * 4,
        compiler_params=_params(("parallel", "parallel")),
    )(u, dy, bre, bim, cre, cim, lr8, li8, dsk)


def _ssm_discretize(a_re, a_im, log_dt, b_re, b_im):
    dt = jnp.exp(log_dt)[:, None]
    er = jnp.exp(a_re * dt)
    lr = er * jnp.cos(a_im * dt)
    li = er * jnp.sin(a_im * dt)
    den = a_re * a_re + a_im * a_im
    fr = ((lr - 1.0) * a_re + li * a_im) / den
    fi = (li * a_re - (lr - 1.0) * a_im) / den
    bbr = fr[..., None] * b_re - fi[..., None] * b_im
    bbi = fr[..., None] * b_im + fi[..., None] * b_re
    return lr, li, bbr, bbi


def _block_diag_in(m):
    G, P, H = m.shape
    J = G // GROUPS_PER_BLOCK
    m = m.reshape(J, GROUPS_PER_BLOCK, P, H).transpose(0, 1, 3, 2)
    eye = jnp.eye(GROUPS_PER_BLOCK, dtype=m.dtype)
    out = m[:, :, :, None, :] * eye[None, :, None, :, None]
    return out.reshape(J, GROUPS_PER_BLOCK * H, GROUPS_PER_BLOCK * P)


def _block_diag_in_grad(d, G, P, H):
    J = G // GROUPS_PER_BLOCK
    d = d.reshape(J, GROUPS_PER_BLOCK, H, GROUPS_PER_BLOCK, P)
    idx = jnp.arange(GROUPS_PER_BLOCK)
    d = d[:, idx, :, idx, :]
    return d.transpose(1, 0, 3, 2).reshape(G, P, H)


def _block_diag_out(m):
    G, H, P = m.shape
    J = G // GROUPS_PER_BLOCK
    m = m.reshape(J, GROUPS_PER_BLOCK, H, P).transpose(0, 1, 3, 2)
    eye = jnp.eye(GROUPS_PER_BLOCK, dtype=m.dtype)
    out = m[:, :, :, None, :] * eye[None, :, None, :, None]
    return out.reshape(J, GROUPS_PER_BLOCK * P, GROUPS_PER_BLOCK * H)


def _block_diag_out_grad(d, G, H, P):
    J = G // GROUPS_PER_BLOCK
    d = d.reshape(J, GROUPS_PER_BLOCK, P, GROUPS_PER_BLOCK, H)
    idx = jnp.arange(GROUPS_PER_BLOCK)
    d = d[:, idx, :, idx, :]
    return d.transpose(1, 0, 3, 2).reshape(G, H, P)


def _interleave(a, B, S):
    L = S // SEGMENTS
    return a.reshape(B, SEGMENTS, L, a.shape[-1]).transpose(0, 2, 1, 3).reshape(B * S, a.shape[-1])


def _deinterleave(a, B, S):
    L = S // SEGMENTS
    return a.reshape(B, L, SEGMENTS, a.shape[-1]).transpose(0, 2, 1, 3).reshape(B * S, a.shape[-1])


def _adamw_math(w, g, m, v):
    m = ADAM_B1 * m + (1.0 - ADAM_B1) * g
    v = ADAM_B2 * v + (1.0 - ADAM_B2) * (g * g)
    m_hat = m / (1.0 - ADAM_B1 ** ADAM_STEP)
    v_hat = v / (1.0 - ADAM_B2 ** ADAM_STEP)
    delta = -ADAM_LR * (m_hat / (jnp.sqrt(v_hat) + ADAM_EPS) + ADAM_WD * w)
    return delta, m, v


def adamw(w, g, m, v, *, name):
    R, C = w.shape
    tr = _tile(R, (max(8, (1 << 18) // C // 8 * 8), 256, 128, 64, 32, 16, 8))

    def body(w_ref, g_ref, m_ref, v_ref, d_ref, nm_ref, nv_ref):
        d, nm, nv = _adamw_math(w_ref[...], g_ref[...], m_ref[...], v_ref[...])
        d_ref[...] = d
        nm_ref[...] = nm
        nv_ref[...] = nv

    spec = pl.BlockSpec((tr, C), lambda i: (i, 0))
    shp = jax.ShapeDtypeStruct((R, C), F32)
    return pl.pallas_call(
        body, name=name, out_shape=[shp, shp, shp], grid=(R // tr,), in_specs=[spec] * 4, out_specs=[spec] * 3,
        compiler_params=_params(("parallel",)),
    )(w, g, m, v)


def sum_leading(a, *, name, out_dtype=F32):
    n, R, C = a.shape
    tr = _tile(R, (256, 128, 64, 32, 16, 8))

    def body(a_ref, o_ref):
        acc = a_ref[0].astype(F32)
        for i in range(1, n):
            acc = acc + a_ref[i].astype(F32)
        o_ref[...] = acc.astype(o_ref.dtype)

    return pl.pallas_call(
        body, name=name, out_shape=jax.ShapeDtypeStruct((R, C), out_dtype), grid=(R // tr,),
        in_specs=[pl.BlockSpec((n, tr, C), lambda i: (0, i, 0))], out_specs=pl.BlockSpec((tr, C), lambda i: (i, 0)),
        compiler_params=_params(("parallel",)),
    )(a)


def _any_specs(n):
    return [pl.BlockSpec(memory_space=pl.ANY) for _ in range(n)]


def _coords():
    return lax.axis_index("x"), lax.axis_index("y"), lax.axis_index("c")


def _flip(v, bit):
    return (v + bit) % 2


def all_gather8(a, *, name):
    shape = a.shape

    def body(a_ref, o_ref, send_sems, recv_sems, local_sem):
        x, y, c = _coords()
        me = 4 * x + 2 * y + c
        mine = pltpu.make_async_copy(a_ref, o_ref.at[me], local_sem)
        mine.start()
        sends = []
        for k in range(1, N_DEV):
            peer = (_flip(x, (k >> 2) & 1), _flip(y, (k >> 1) & 1), _flip(c, k & 1))
            cp = pltpu.make_async_remote_copy(a_ref, o_ref.at[me], send_sems.at[k - 1], recv_sems.at[k - 1],
                                              device_id=peer, device_id_type=MESH)
            cp.start()
            sends.append(cp)
        for k in range(1, N_DEV):
            px, py, pc = _flip(x, (k >> 2) & 1), _flip(y, (k >> 1) & 1), _flip(c, k & 1)
            src = 4 * px + 2 * py + pc
            pltpu.make_async_remote_copy(a_ref, o_ref.at[src], send_sems.at[k - 1], recv_sems.at[k - 1],
                                         device_id=(px, py, pc), device_id_type=MESH).wait_recv()
        for cp in sends:
            cp.wait_send()
        mine.wait()

    return pl.pallas_call(
        body, name=name, out_shape=jax.ShapeDtypeStruct((N_DEV,) + shape, a.dtype),
        in_specs=_any_specs(1), out_specs=pl.BlockSpec(memory_space=pl.ANY),
        scratch_shapes=[pltpu.SemaphoreType.DMA((N_DEV - 1,)), pltpu.SemaphoreType.DMA((N_DEV - 1,)),
                        pltpu.SemaphoreType.DMA(())],
    )(a)


def _chip_of(x, y, p):
    px, py = _flip(x, (p >> 1) & 1), _flip(y, p & 1)
    return 2 * px + py, px, py


class _ShardGather:
    def __init__(self, ins, outs, ici_send, ici_recv, d2d_send, d2d_recv):
        self.ins, self.outs = ins, outs
        self.sems = ici_send, ici_recv, d2d_send, d2d_recv
        self.x, self.y, self.c = _coords()
        self.me = 2 * self.x + self.y

    def _ici(self, i, p, slot):
        half = self.ins[i].shape[0] // 2
        rows = pl.ds(self.c * half, half)
        _, px, py = _chip_of(self.x, self.y, p)
        s = i * 3 + p - 1
        return pltpu.make_async_remote_copy(self.ins[i].at[rows], self.outs[i].at[slot, rows], self.sems[0].at[s],
                                            self.sems[1].at[s], device_id=(px, py, self.c), device_id_type=MESH)

    def _d2d(self, i, p, mine):
        half = self.ins[i].shape[0] // 2
        rows = pl.ds((self.c if mine else 1 - self.c) * half, half)
        src, _, _ = _chip_of(self.x, self.y, p)
        s = i * 3 + p - 1
        part = self.outs[i].at[src, rows]
        return pltpu.make_async_remote_copy(part, part, self.sems[2].at[s], self.sems[3].at[s],
                                            device_id=(self.x, self.y, 1 - self.c), device_id_type=MESH)

    def _each(self):
        return [(i, p) for i in range(len(self.ins)) for p in range(1, N_CHIPS)]

    def send(self):
        for i, p in self._each():
            self._ici(i, p, self.me).start()

    def forward(self):
        for i, p in self._each():
            self._ici(i, p, _chip_of(self.x, self.y, p)[0]).wait_recv()
            self._d2d(i, p, True).start()

    def finish(self):
        for i, p in self._each():
            self._d2d(i, p, False).wait_recv()
        for i, p in self._each():
            self._ici(i, p, self.me).wait_send()
            self._d2d(i, p, True).wait_send()


def gather_chip_shards(arrs, remote, *, name):
    n = len(arrs)
    far = [i for i in range(n) if remote[i]]

    def body(*refs):
        ins, outs = refs[:n], refs[n:2 * n]
        ici_send, ici_recv, d2d_send, d2d_recv, local_sems = refs[2 * n:2 * n + 5]
        bufs = refs[2 * n + 5:]
        me = 2 * lax.axis_index("x") + lax.axis_index("y")
        loads = []
        for i in range(n):
            cp = pltpu.make_async_copy(ins[i], bufs[i], local_sems.at[i])
            cp.start()
            loads.append(cp)
        gather = _ShardGather([ins[i] for i in far], [outs[i] for i in far], ici_send, ici_recv, d2d_send, d2d_recv)
        gather.send()
        stores = []
        for i in range(n):
            loads[i].wait()
            cp = pltpu.make_async_copy(bufs[i], outs[i].at[me], local_sems.at[i])
            cp.start()
            stores.append(cp)
        gather.forward()
        gather.finish()
        for cp in stores:
            cp.wait()

    dma = pltpu.SemaphoreType.DMA
    m = 3 * len(far)
    return pl.pallas_call(
        body, name=name,
        out_shape=[jax.ShapeDtypeStruct((N_CHIPS,) + a.shape, a.dtype) for a in arrs],
        in_specs=_any_specs(n), out_specs=_any_specs(n),
        scratch_shapes=[dma((m,)), dma((m,)), dma((m,)), dma((m,)), dma((n,))]
        + [pltpu.VMEM(a.shape, a.dtype) for a in arrs],
        compiler_params=pltpu.CompilerParams(vmem_limit_bytes=V7X_VMEM_LIMIT),
    )(*arrs)


def swap_halves(arrs, *, name):
    n = len(arrs)

    def body(*refs):
        ins, outs = refs[:n], refs[n:2 * n]
        send_sems, recv_sems = refs[2 * n:]
        x, y, c = _coords()
        cps = []
        for i in range(n):
            half = ins[i].shape[1] // 2
            cp = pltpu.make_async_remote_copy(ins[i].at[:, pl.ds((1 - c) * half, half)], outs[i], send_sems.at[i],
                                              recv_sems.at[i], device_id=(x, y, 1 - c), device_id_type=MESH)
            cp.start()
            cps.append(cp)
        for cp in cps:
            cp.wait()

    dma = pltpu.SemaphoreType.DMA
    return pl.pallas_call(
        body, name=name,
        out_shape=[jax.ShapeDtypeStruct((N_CHIPS, a.shape[1] // 2, a.shape[2]), a.dtype) for a in arrs],
        in_specs=_any_specs(n), out_specs=_any_specs(n), scratch_shapes=[dma((n,)), dma((n,))],
    )(*arrs)


def add_half(g, other, c_idx, *, name, out_dtype):
    _, R, C = g.shape
    half = R // 2
    tr = _tile(half, (256, 128, 64, 32, 16, 8))
    nt = half // tr

    def body(c_ref, g_ref, o_ref, out_ref):
        out_ref[...] = (g_ref[...] + o_ref[...]).astype(out_ref.dtype)

    return pl.pallas_call(
        body, name=name, out_shape=jax.ShapeDtypeStruct((N_CHIPS, half, C), out_dtype),
        grid_spec=pltpu.PrefetchScalarGridSpec(
            num_scalar_prefetch=1, grid=(N_CHIPS, nt),
            in_specs=[pl.BlockSpec((None, tr, C), lambda r, t, c_ref: (r, c_ref[0] * nt + t, 0)),
                      pl.BlockSpec((None, tr, C), lambda r, t, c_ref: (r, t, 0))],
            out_specs=pl.BlockSpec((None, tr, C), lambda r, t, c_ref: (r, t, 0))),
        compiler_params=_params(("parallel", "parallel")),
    )(c_idx, g, other)


def scatter_to_chips(arrs, *, name):
    n = len(arrs)

    def body(*refs):
        ins, outs = refs[:n], refs[n:2 * n]
        send_sems, recv_sems = refs[2 * n:]
        x, y, c = _coords()
        cps = []
        for i in range(n):
            for p in range(1, N_CHIPS):
                dst, px, py = _chip_of(x, y, p)
                s = i * 3 + p - 1
                cp = pltpu.make_async_remote_copy(ins[i].at[dst], outs[i].at[p - 1], send_sems.at[s], recv_sems.at[s],
                                                  device_id=(px, py, c), device_id_type=MESH)
                cp.start()
                cps.append(cp)
        for cp in cps:
            cp.wait()

    dma = pltpu.SemaphoreType.DMA
    return pl.pallas_call(
        body, name=name,
        out_shape=[jax.ShapeDtypeStruct((N_CHIPS - 1,) + a.shape[1:], a.dtype) for a in arrs],
        in_specs=_any_specs(n), out_specs=_any_specs(n), scratch_shapes=[dma((3 * n,)), dma((3 * n,))],
    )(*arrs)


def add_chips(h, got, r_idx, *, name):
    _, R, C = h.shape
    tr = _tile(R, (256, 128, 64, 32, 16, 8))

    def body(r_ref, h_ref, g_ref, out_ref):
        acc = h_ref[...].astype(F32)
        for p in range(N_CHIPS - 1):
            acc = acc + g_ref[p].astype(F32)
        out_ref[...] = acc

    return pl.pallas_call(
        body, name=name, out_shape=jax.ShapeDtypeStruct((R, C), F32),
        grid_spec=pltpu.PrefetchScalarGridSpec(
            num_scalar_prefetch=1, grid=(R // tr,),
            in_specs=[pl.BlockSpec((None, tr, C), lambda t, r_ref: (r_ref[0], t, 0)),
                      pl.BlockSpec((N_CHIPS - 1, tr, C), lambda t, r_ref: (0, t, 0))],
            out_specs=pl.BlockSpec((tr, C), lambda t, r_ref: (t, 0))),
        compiler_params=_params(("parallel",)),
    )(r_idx, h, got)


def join_halves(arrs, *, name):
    n = len(arrs)

    def body(*refs):
        ins, outs = refs[:n], refs[n:2 * n]
        send_sems, recv_sems, local_sems = refs[2 * n:2 * n + 3]
        bufs = refs[2 * n + 3:]
        x, y, c = _coords()
        loads, sends, stores = [], [], []
        for i in range(n):
            cp = pltpu.make_async_copy(ins[i], bufs[i], local_sems.at[i])
            cp.start()
            loads.append(cp)
        for i in range(n):
            half = ins[i].shape[0]
            cp = pltpu.make_async_remote_copy(ins[i], outs[i].at[pl.ds(c * half, half)], send_sems.at[i], recv_sems.at[i],
                                              device_id=(x, y, 1 - c), device_id_type=MESH)
            cp.start()
            sends.append(cp)
        for i in range(n):
            half = ins[i].shape[0]
            loads[i].wait()
            cp = pltpu.make_async_copy(bufs[i], outs[i].at[pl.ds(c * half, half)], local_sems.at[i])
            cp.start()
            stores.append(cp)
        for i in range(n):
            half = ins[i].shape[0]
            pltpu.make_async_remote_copy(ins[i], outs[i].at[pl.ds((1 - c) * half, half)], send_sems.at[i],
                                         recv_sems.at[i], device_id=(x, y, 1 - c), device_id_type=MESH).wait_recv()
        for i in range(n):
            sends[i].wait_send()
            stores[i].wait()

    dma = pltpu.SemaphoreType.DMA
    return pl.pallas_call(
        body, name=name,
        out_shape=[jax.ShapeDtypeStruct((2 * a.shape[0], a.shape[1]), a.dtype) for a in arrs],
        in_specs=_any_specs(n), out_specs=_any_specs(n),
        scratch_shapes=[dma((n,)), dma((n,)), dma((n,))] + [pltpu.VMEM(a.shape, a.dtype) for a in arrs],
        compiler_params=pltpu.CompilerParams(vmem_limit_bytes=V7X_VMEM_LIMIT),
    )(*arrs)


def reduce_scatter_chips(grads, wire_dtypes):
    x, y, c = _coords()
    c_idx = jnp.reshape(c, (1,)).astype(jnp.int32)
    r_idx = jnp.reshape(2 * x + y, (1,)).astype(jnp.int32)
    theirs = swap_halves(grads, name="rs_swap_halves")
    pair = [add_half(g, o, c_idx, name=f"rs_add_half_{i}", out_dtype=wire_dtypes[i])
            for i, (g, o) in enumerate(zip(grads, theirs))]
    got = scatter_to_chips(pair, name="rs_scatter_to_chips")
    mine = [add_chips(h, g, r_idx, name=f"rs_add_chips_{i}") for i, (h, g) in enumerate(zip(pair, got))]
    return join_halves(mine, name="rs_join_halves")


def _to_heads(t, B, S):
    return t.reshape(B, S, -1, HEAD_DIM).transpose(0, 2, 1, 3)


def _from_heads(t, B, S):
    return t.transpose(0, 2, 1, 3).reshape(B * S, -1)


def _chip_major(w, axis):
    n = w.shape[axis] // N_CHIPS
    parts = w.reshape(w.shape[:axis] + (N_CHIPS, n) + w.shape[axis + 1:])
    return jnp.moveaxis(parts, axis, 0)


def _from_chip_major(g, axis):
    g = jnp.moveaxis(g, 0, axis)
    return g.reshape(g.shape[:axis] + (g.shape[axis] * g.shape[axis + 1],) + g.shape[axis + 2:])


def kernel(x, c, norm_mix, norm_ffn, w_mod, b_mod, w_qkv, w_o_attn, w_in_ssm, a_re, a_im, log_dt, b_re, b_im, c_re, c_im, d_skip, w_glu, b_glu, w_o_ssm, w_up, conv_w, conv_b, w_down, norm_out, w_fin, b_fin, loss_target, m_norm_mix, m_norm_ffn, m_w_mod, m_b_mod, m_w_qkv, m_w_o_attn, m_w_in_ssm, m_a_re, m_a_im, m_log_dt, m_b_re, m_b_im, m_c_re, m_c_im, m_d_skip, m_w_glu, m_b_glu, m_w_o_ssm, m_w_up, m_conv_w, m_conv_b, m_w_down, m_norm_out, m_w_fin, m_b_fin, v_norm_mix, v_norm_ffn, v_w_mod, v_b_mod, v_w_qkv, v_w_o_attn, v_w_in_ssm, v_a_re, v_a_im, v_log_dt, v_b_re, v_b_im, v_c_re, v_c_im, v_d_skip, v_w_glu, v_b_glu, v_w_o_ssm, v_w_up, v_conv_w, v_conv_b, v_w_down, v_norm_out, v_w_fin, v_b_fin):
    B, S, D = x.shape
    T = B * S
    F2 = conv_b.shape[1]
    F = F2 // 2
    G, P = a_re.shape[1], a_re.shape[2]
    H = b_re.shape[3]
    mx, my, mc = _coords()
    chip = 2 * mx + my
    dev = 4 * mx + 2 * my + mc
    BG = N_DEV * B
    mod_w = w_mod.shape[2]
    fin_w = w_fin.shape[1]

    c_all = all_gather8(c, name="gather_c").reshape(BG, D)
    c_act = silu_rows(c_all, name="silu_c")
    b_mod_mine = lax.dynamic_slice(b_mod, (0, chip * mod_w), (2, mod_w))
    b_fin_mine = lax.dynamic_slice(b_fin, (chip * fin_w,), (fin_w,))
    cond = [matmul(c_act, w_mod[i], bias=b_mod_mine[i], name=f"mod_proj_{i}") for i in range(2)]
    cond.append(matmul(c_act, w_fin, bias=b_fin_mine, name="fin_proj"))
    cond_all = all_gather8(jnp.concatenate(cond, axis=1), name="gather_cond")
    cond_all = cond_all[::2]
    cond_rows = lax.dynamic_slice(cond_all, (0, dev * B, 0), (N_CHIPS, B, cond_all.shape[2]))
    mods = []
    for i in range(2):
        full = cond_rows[:, :, i * mod_w:(i + 1) * mod_w].transpose(1, 0, 2).reshape(B, N_CHIPS * mod_w)
        mods.append([full[:, k * D:(k + 1) * D] for k in range(6)])
    fin = cond_rows[:, :, 2 * mod_w:].transpose(1, 0, 2).reshape(B, N_CHIPS * fin_w)
    sh_f, sc_f = fin[:, :D], fin[:, D:]

    rows1024 = jnp.concatenate([w_o_attn[0], w_in_ssm[0], w_glu[0], w_o_ssm[0], w_down.reshape(-1, D)], axis=0)
    shards = [w_qkv[0].astype(BF16), rows1024.astype(BF16), w_up[0].astype(BF16), w_up[1].astype(BF16)]
    W_qkv, *own_slots = gather_chip_shards(shards, [True, False, False, False], name="gather_weights")
    Dq = D // N_CHIPS
    Fq = F // N_CHIPS
    small =jnp.concatenate([conv_w.reshape(6, -1), jnp.pad(d_skip, ((0, 0), (0, conv_w.shape[2] - Dq))),
                             jnp.pad(b_glu, ((0, 0), (0, conv_w.shape[2] - Dq)))], axis=0)
    small_all = all_gather8(small, name="gather_small")[::2]
    conv_w_full = _from_chip_major(small_all[:, :6].reshape(N_CHIPS, 2, 3, -1), 2)
    d_skip_full = small_all[:, 6, :Dq].reshape(1, D)
    b_glu_full = small_all[:, 7, :Dq].reshape(D)

    x0 = x.reshape(T, D)
    tgt = loss_target.reshape(T, D)

    def ffn_fwd(xin, i):
        sh2, sc2, g2 = mods[i][3], mods[i][4], mods[i][5]
        h2 = norm_mod_fwd(xin, norm_ffn[i], sh2, sc2, B=B, S=S, name=f"ffn_norm_{i}")
        up = matmul(h2, W_up[i], b_chips=True, name=f"ffn_up_{i}")
        act = conv_gate_fwd(up, conv_w_full[i], conv_b[i:i + 1], B=B, S=S, name=f"ffn_conv_{i}")
        yf = matmul(act, W_down[i], name=f"ffn_down_{i}")
        xout = gate_res_fwd(xin, yf, g2, B=B, S=S, name=f"ffn_res_{i}")
        return xout, (xin, h2, up, act, yf)

    sh1, sc1, g1 = mods[0][0], mods[0][1], mods[0][2]
    h1a = norm_mod_fwd(x0, norm_mix[0], sh1, sc1, B=B, S=S, name="att_norm")
    qkv = matmul(h1a, W_qkv, out_dtype=BF16, b_chips=True, name="att_qkv")
    q, k, v = [_to_heads(qkv[:, i * D:(i + 1) * D], B, S) for i in range(3)]
    o, ltot, g_rows, W_up0, W_up1 = attn_fwd_wide(q, k, v, shards[1:], own_slots, name="att_fwd")
    W_up = [W_up0, W_up1]
    W_o_attn = g_rows[:, 0 * Dq:1 * Dq].reshape(D, D)
    W_in = g_rows[:, 1 * Dq:2 * Dq].reshape(D, D)
    W_glu = g_rows[:, 2 * Dq:3 * Dq].reshape(D, D)
    W_o_ssm = g_rows[:, 3 * Dq:4 * Dq].reshape(D, D)
    W_down = [g_rows[:, 4 * Dq + i * Fq:4 * Dq + (i + 1) * Fq].reshape(F, D) for i in range(2)]
    o2 = _from_heads(o, B, S).astype(BF16)
    ya = matmul(o2, W_o_attn, name="att_out")
    x1 = gate_res_fwd(x0, ya, g1, B=B, S=S, name="att_res")
    x2, ffn0 = ffn_fwd(x1, 0)

    lr, li, bbr, bbi = _ssm_discretize(a_re[0], a_im[0], log_dt[0], b_re[0], b_im[0])
    J = G // GROUPS_PER_BLOCK
    Wst = GROUPS_PER_BLOCK * P
    bre_blk = _block_diag_in(bbr).astype(BF16)
    bim_blk = _block_diag_in(bbi).astype(BF16)
    cre_blk = _block_diag_out(c_re[0]).astype(BF16)
    cim_blk = _block_diag_out(c_im[0]).astype(BF16)
    lr8 = jnp.broadcast_to(lr.reshape(J, 1, Wst), (J, SEGMENTS, Wst))
    li8 = jnp.broadcast_to(li.reshape(J, 1, Wst), (J, SEGMENTS, Wst))
    sh1s, sc1s, g1s = mods[1][0], mods[1][1], mods[1][2]
    h1s = norm_mod_fwd(x2, norm_mix[1], sh1s, sc1s, B=B, S=S, name="ssm_norm")
    h1p = _interleave(h1s, B, S)
    u = matmul(h1p, W_in, name="ssm_in")
    y_ssm = ssm_fwd(u, bre_blk, bim_blk, cre_blk, cim_blk, lr8, li8, d_skip_full, B=B, S=S, name="ssm_scan_fwd")
    zb = gelu_fwd(y_ssm, B=B, S=S, name="ssm_gelu")
    s_glu = matmul(zb, W_glu, bias=b_glu_full, name="ssm_glu_proj")
    gb = glu_fwd(y_ssm, s_glu, B=B, S=S, name="ssm_glu")
    ys_p = matmul(gb, W_o_ssm, name="ssm_out")
    ys = _deinterleave(ys_p, B, S)
    x3 = gate_res_fwd(x2, ys, g1s, B=B, S=S, name="ssm_res")
    x4, ffn1 = ffn_fwd(x3, 1)

    dx4, loss_p, dsh_f, dsc_f, dnorm_out = final_loss(x4, tgt, norm_out, sh_f, sc_f, B=B, S=S, name="loss_head")
    loss = lax.psum(jnp.sum(loss_p), ("x", "y", "c"))

    def ffn_bwd(dxo, i, saved):
        xin, h2, up, act, yf = saved
        sc2, g2 = mods[i][4], mods[i][5]
        dyf, dg2 = gate_res_bwd(dxo, yf, g2, B=B, S=S, name=f"ffn_res_bwd_{i}")
        dact = matmul(dyf, W_down[i], tb=True, name=f"ffn_down_dx_{i}")
        dW_down = matmul(act, dyf, ta=True, name=f"ffn_down_dw_{i}")
        dup, dcw, dcb = conv_gate_bwd(up, dact, conv_w_full[i], conv_b[i:i + 1], B=B, S=S, name=f"ffn_conv_bwd_{i}")
        dh2 = matmul(dup, W_up[i], tb=True, b_chips=True, name=f"ffn_up_dx_{i}")
        dW_up = matmul(h2, dup, ta=True, b_chips=True, out_chips=True, name=f"ffn_up_dw_{i}")
        dxin, dsh2, dsc2, dnf = norm_mod_bwd(dh2, xin, dxo, norm_ffn[i], sc2, B=B, S=S, name=f"ffn_norm_bwd_{i}")
        dconv_w = jnp.sum(dcw, axis=0).transpose(1, 0, 2).reshape(3, F2)
        return dxin, dict(dW_down=dW_down, dW_up=dW_up, dconv_b=jnp.sum(dcb, axis=0).reshape(F2),
                          dconv_w=dconv_w, dnorm_ffn=jnp.sum(dnf, axis=0), dsh2=dsh2, dsc2=dsc2, dg2=dg2)

    dx3, gf1 = ffn_bwd(dx4, 1, ffn1)

    dys_p, dg1s = gate_res_bwd(_interleave(dx3, B, S), ys_p, g1s, B=B, S=S, name="ssm_res_bwd")
    dgb = matmul(dys_p, W_o_ssm, tb=True, name="ssm_out_dx")
    dW_o_ssm = matmul(gb, dys_p, ta=True, name="ssm_out_dw")
    ds_glu, dz1, db_glu = glu_bwd1(y_ssm, s_glu, dgb, B=B, S=S, name="ssm_glu_bwd1")
    dz2 = matmul(ds_glu, W_glu, tb=True, name="ssm_glu_dx")
    dW_glu = matmul(zb, ds_glu, ta=True, name="ssm_glu_dw")
    dy_ssm = glu_bwd2(y_ssm, dz1, dz2, B=B, S=S, name="ssm_glu_bwd2")
    du, dbre, dbim, dcre, dcim, dlr8, dli8, ddsk = ssm_bwd(u, dy_ssm, bre_blk, bim_blk, cre_blk, cim_blk, lr8, li8,
                                                           d_skip_full, B=B, S=S, name="ssm_scan_bwd")
    dub = du.astype(BF16)
    dh1p = matmul(dub, W_in, tb=True, name="ssm_in_dx")
    dW_in = matmul(h1p, dub, ta=True, name="ssm_in_dw")
    dx2, dsh1s, dsc1s, dnm1 = norm_mod_bwd(_deinterleave(dh1p, B, S), x2, dx3, norm_mix[1], sc1s, B=B, S=S,
                                           name="ssm_norm_bwd")
    dlr = jnp.sum(dlr8, axis=(0, 2)).reshape(G, P)
    dli = jnp.sum(dli8, axis=(0, 2)).reshape(G, P)
    dbbr = _block_diag_in_grad(jnp.sum(dbre, axis=0), G, P, H)
    dbbi = _block_diag_in_grad(jnp.sum(dbim, axis=0), G, P, H)
    dc_re = _block_diag_out_grad(jnp.sum(dcre, axis=0), G, H, P)
    dc_im = _block_diag_out_grad(jnp.sum(dcim, axis=0), G, H, P)
    dd_skip = jnp.sum(ddsk, axis=0).reshape(D)

    dx1, gf0 = ffn_bwd(dx2, 0, ffn0)

    dya, dg1 = gate_res_bwd(dx1, ya, g1, B=B, S=S, name="att_res_bwd")
    do2 = matmul(dya, W_o_attn, tb=True, out_dtype=BF16, name="att_out_dx")
    dW_o_attn = matmul(o2, dya, ta=True, name="att_out_dw")
    dq, dk, dv = attn_bwd_wide(q, k, v, ltot, _to_heads(do2, B, S), name="att_bwd")
    dqkv = jnp.concatenate([_from_heads(t, B, S) for t in (dq, dk, dv)], axis=1)
    dh1a = matmul(dqkv, W_qkv, tb=True, b_chips=True, name="att_qkv_dx")
    dW_qkv = matmul(h1a, dqkv, ta=True, out_chips=True, name="att_qkv_dw")
    grad_x, dsh1, dsc1, dnm0 = norm_mod_bwd(dh1a, x0, dx1, norm_mix[0], sc1, B=B, S=S, name="att_norm_bwd")

    dmod_rows = jnp.concatenate([dsh1, dsc1, dg1, gf0["dsh2"], gf0["dsc2"], gf0["dg2"],
                                 dsh1s, dsc1s, dg1s, gf1["dsh2"], gf1["dsc2"], gf1["dg2"], dsh_f, dsc_f], axis=1)
    dmod_all = all_gather8(dmod_rows, name="gather_dmod").reshape(BG, 14 * D)
    grad_w_mod = jnp.stack([
        matmul(c_act, lax.dynamic_slice(dmod_all, (0, i * 6 * D + chip * mod_w), (BG, mod_w)), ta=True,
               name=f"mod_dw_{i}") for i in range(2)])
    grad_w_fin = matmul(c_act, lax.dynamic_slice(dmod_all, (0, 12 * D + chip * fin_w), (BG, fin_w)), ta=True,
                        name="fin_dw")

    parts = [jnp.concatenate([jnp.sum(dnm0, axis=0), jnp.sum(dnm1, axis=0)]),
             jnp.concatenate([gf0["dnorm_ffn"], gf1["dnorm_ffn"]]),
             jnp.sum(dmod_rows[:, :12 * D], axis=0),
             dlr.reshape(-1), dli.reshape(-1), dbbr.reshape(-1), dbbi.reshape(-1), dc_re.reshape(-1), dc_im.reshape(-1),
             dd_skip, jnp.sum(db_glu, axis=0),
             gf0["dconv_w"].reshape(-1), gf1["dconv_w"].reshape(-1), gf0["dconv_b"], gf1["dconv_b"],
             jnp.sum(dnorm_out, axis=0), jnp.sum(dmod_rows[:, 12 * D:], axis=0)]
    sizes = [int(p.shape[0]) for p in parts]
    flat = jnp.concatenate(parts)
    width = 1024
    quantum = N_CHIPS * 16 * width
    padded = -(-flat.shape[0] // quantum) * quantum
    small_cm = jnp.pad(flat, (0, padded - flat.shape[0])).reshape(N_CHIPS, -1, width)

    g_rows_cm = jnp.concatenate([dW_o_attn.reshape(N_CHIPS, Dq, D), dW_in.reshape(N_CHIPS, Dq, D),
                                 dW_glu.reshape(N_CHIPS, Dq, D), dW_o_ssm.reshape(N_CHIPS, Dq, D),
                                 gf0["dW_down"].reshape(N_CHIPS, Fq, D), gf1["dW_down"].reshape(N_CHIPS, Fq, D)], axis=1)
    r_qkv, r_rows, r_up0, r_up1, r_small = reduce_scatter_chips(
        [dW_qkv, g_rows_cm, gf0["dW_up"], gf1["dW_up"], small_cm], [BF16, BF16, BF16, BF16, F32])
    grad_w_qkv = r_qkv[None]
    grad_w_o_attn = r_rows[0 * Dq:1 * Dq][None]
    grad_w_in_ssm = r_rows[1 * Dq:2 * Dq][None]
    grad_w_glu = r_rows[2 * Dq:3 * Dq][None]
    grad_w_o_ssm = r_rows[3 * Dq:4 * Dq][None]
    grad_w_down = r_rows[4 * Dq:].reshape(2, Fq, D)
    grad_w_up = jnp.stack([r_up0, r_up1])
    summed = all_gather8(r_small, name="gather_small_grads")[::2].reshape(-1)
    offs = [0]
    for s_ in sizes:
        offs.append(offs[-1] + s_)
    (s_nm, s_nf, s_bmod, s_lr, s_li, s_bbr, s_bbi, s_cre, s_cim, s_dsk, s_bglu, s_cw0, s_cw1, s_cb0, s_cb1, s_no,
     s_bfin) = [summed[offs[i]:offs[i + 1]] for i in range(len(sizes))]
    _, disc_vjp = jax.vjp(_ssm_discretize, a_re[0], a_im[0], log_dt[0], b_re[0], b_im[0])
    ga_re, ga_im, glog_dt, gb_re, gb_im = disc_vjp((s_lr.reshape(G, P), s_li.reshape(G, P), s_bbr.reshape(G, P, H),
                                                    s_bbi.reshape(G, P, H)))
    grad_norm_mix = s_nm.reshape(2, D)
    grad_norm_ffn = s_nf.reshape(2, D)
    grad_b_mod = s_bmod.reshape(2, 6 * D)
    grad_c_re = s_cre.reshape(1, G, H, P)
    grad_c_im = s_cim.reshape(1, G, H, P)
    grad_d_skip = lax.dynamic_slice(s_dsk, (chip * Dq,), (Dq,)).reshape(1, Dq)
    grad_b_glu = lax.dynamic_slice(s_bglu, (chip * Dq,), (Dq,)).reshape(1, Dq)
    cw_full = jnp.stack([s_cw0.reshape(3, F2), s_cw1.reshape(3, F2)])
    grad_conv_w = lax.dynamic_slice(cw_full, (0, 0, chip * (F2 // N_CHIPS)), (2, 3, F2 // N_CHIPS))
    grad_conv_b = jnp.stack([s_cb0, s_cb1])
    grad_norm_out = s_no
    grad_b_fin = s_bfin

    grads = dict(
        norm_mix=grad_norm_mix, norm_ffn=grad_norm_ffn, w_mod=grad_w_mod, b_mod=grad_b_mod, w_qkv=grad_w_qkv,
        w_o_attn=grad_w_o_attn, w_in_ssm=grad_w_in_ssm, a_re=ga_re[None], a_im=ga_im[None], log_dt=glog_dt[None],
        b_re=gb_re[None], b_im=gb_im[None], c_re=grad_c_re, c_im=grad_c_im, d_skip=grad_d_skip, w_glu=grad_w_glu,
        b_glu=grad_b_glu, w_o_ssm=grad_w_o_ssm, w_up=grad_w_up, conv_w=grad_conv_w, conv_b=grad_conv_b,
        w_down=grad_w_down, norm_out=grad_norm_out, w_fin=grad_w_fin, b_fin=grad_b_fin)
    weights = dict(
        norm_mix=norm_mix, norm_ffn=norm_ffn, w_mod=w_mod, b_mod=b_mod, w_qkv=w_qkv, w_o_attn=w_o_attn,
        w_in_ssm=w_in_ssm, a_re=a_re, a_im=a_im, log_dt=log_dt, b_re=b_re, b_im=b_im, c_re=c_re, c_im=c_im,
        d_skip=d_skip, w_glu=w_glu, b_glu=b_glu, w_o_ssm=w_o_ssm, w_up=w_up, conv_w=conv_w, conv_b=conv_b,
        w_down=w_down, norm_out=norm_out, w_fin=w_fin, b_fin=b_fin)
    m_in = dict(
        norm_mix=m_norm_mix, norm_ffn=m_norm_ffn, w_mod=m_w_mod, b_mod=m_b_mod, w_qkv=m_w_qkv, w_o_attn=m_w_o_attn,
        w_in_ssm=m_w_in_ssm, a_re=m_a_re, a_im=m_a_im, log_dt=m_log_dt, b_re=m_b_re, b_im=m_b_im, c_re=m_c_re,
        c_im=m_c_im, d_skip=m_d_skip, w_glu=m_w_glu, b_glu=m_b_glu, w_o_ssm=m_w_o_ssm, w_up=m_w_up, conv_w=m_conv_w,
        conv_b=m_conv_b, w_down=m_w_down, norm_out=m_norm_out, w_fin=m_w_fin, b_fin=m_b_fin)
    v_in = dict(
        norm_mix=v_norm_mix, norm_ffn=v_norm_ffn, w_mod=v_w_mod, b_mod=v_b_mod, w_qkv=v_w_qkv, w_o_attn=v_w_o_attn,
        w_in_ssm=v_w_in_ssm, a_re=v_a_re, a_im=v_a_im, log_dt=v_log_dt, b_re=v_b_re, b_im=v_b_im, c_re=v_c_re,
        c_im=v_c_im, d_skip=v_d_skip, w_glu=v_w_glu, b_glu=v_b_glu, w_o_ssm=v_w_o_ssm, w_up=v_w_up, conv_w=v_conv_w,
        conv_b=v_conv_b, w_down=v_w_down, norm_out=v_norm_out, w_fin=v_w_fin, b_fin=v_b_fin)
    names = list(weights)
    for n_ in names:
        grads[n_] = grads[n_].reshape(weights[n_].shape)

    big = ("w_mod", "w_qkv", "w_o_attn", "w_in_ssm", "w_glu", "w_o_ssm", "w_up", "w_down", "w_fin")
    delta, new_m, new_v = {}, {}, {}
    for n_ in big:
        shp = weights[n_].shape
        two_d = lambda a: a.reshape(-1, shp[-1])
        d_, m_, v_ = adamw(two_d(weights[n_]), two_d(grads[n_]), two_d(m_in[n_]), two_d(v_in[n_]), name=f"adamw_{n_}")
        delta[n_], new_m[n_], new_v[n_] = d_.reshape(shp), m_.reshape(shp), v_.reshape(shp)
    rest = [n_ for n_ in names if n_ not in big]

    def pack(tree):
        f = jnp.concatenate([tree[n_].reshape(-1) for n_ in rest])
        pad_to = -(-f.shape[0] // (8 * width)) * (8 * width)
        return jnp.pad(f, (0, pad_to - f.shape[0]), constant_values=1.0).reshape(-1, width)

    d_, m_, v_ = adamw(pack(weights), pack(grads), pack(m_in), pack(v_in), name="adamw_small")
    off = 0
    for n_ in rest:
        sz = int(math.prod(weights[n_].shape))
        shp = weights[n_].shape
        delta[n_] = d_.reshape(-1)[off:off + sz].reshape(shp)
        new_m[n_] = m_.reshape(-1)[off:off + sz].reshape(shp)
        new_v[n_] = v_.reshape(-1)[off:off + sz].reshape(shp)
        off += sz

    return (loss, grad_x.reshape(B, S, D), *[grads[n_] for n_ in names], *[delta[n_] for n_ in names],
            *[new_m[n_] for n_ in names], *[new_v[n_] for n_ in names])
```

```python
import functools
import math

import jax
import jax.numpy as jnp
from jax import lax
from jax.experimental import pallas as pl
from jax.experimental.pallas import tpu as pltpu

F32 = jnp.float32
BF16 = jnp.bfloat16
MESH = pl.DeviceIdType.MESH

HEAD_DIM = 64
SSM_GROUP = 16
STATE = 64
GROUPS_PER_BLOCK = 8
SEGMENTS = 16
SCAN_UNROLL = 4
EPS = 1e-6
ADAM_LR = 0.001
ADAM_B1 = 0.9
ADAM_B2 = 0.999
ADAM_EPS = 1e-08
ADAM_WD = 0.01
ADAM_STEP = 10
N_CHIPS = 4
N_DEV = 8
V7X_VMEM_LIMIT = 56 * 1024 * 1024
ATT_BLOCK = 128
ATT_HEADS = 8
ATT_HEADS_BWD = 4


def _tile(n, prefs):
    for p in prefs:
        if n % p == 0:
            return p
    return n


def _params(sem, vmem=V7X_VMEM_LIMIT):
    return pltpu.CompilerParams(dimension_semantics=sem, vmem_limit_bytes=vmem)


def matmul(a, b, *, ta=False, tb=False, bias=None, out_dtype=F32, b_chips=False, out_chips=False, name):
    a_parts = a.shape[0] if a.ndim == 3 else 1
    if a_parts > 1:
        assert not ta
        M, K = a.shape[1], a_parts * a.shape[2]
    elif ta:
        K, M = a.shape
    else:
        M, K = a.shape
    b_parts = b.shape[0] if b_chips else 1
    b_rows, b_cols = (b.shape[1], b_parts * b.shape[2]) if b_chips else b.shape
    if tb:
        N, Kb = b_rows, b_cols
    else:
        Kb, N = b_rows, b_cols
    assert K == Kb, (a.shape, b.shape, ta, tb)
    n_cut = N // max(N_CHIPS if out_chips else 1, b_parts if not tb else 1)
    k_cut = K // max(b_parts if tb else 1, a_parts)
    tm = _tile(M, (1024, 512, 256, 128))
    tn = _tile(n_cut, (1024, 1408, 768, 512, 256, 128))
    tk = k_cut if k_cut <= 2816 else _tile(k_cut, (1024, 512, 256, 128))
    nk = K // tk
    npc = N // N_CHIPS // tn
    npb = N // b_parts // tn
    kpb = K // b_parts // tk
    kpa = K // a_parts // tk
    dims = (((0,) if ta else (1,), (1,) if tb else (0,)), ((), ()))

    def body(*refs):
        a_ref, b_ref = refs[:2]
        bias_ref = refs[2] if bias is not None else None
        o_ref = refs[-2] if nk > 1 else refs[-1]

        def finish(r):
            if bias_ref is not None:
                r = r + bias_ref[...]
            o_ref[...] = r.astype(o_ref.dtype)

        prod = lax.dot_general(a_ref[...].astype(BF16), b_ref[...].astype(BF16), dims, preferred_element_type=F32)
        if nk == 1:
            finish(prod)
            return
        acc_ref = refs[-1]
        k = pl.program_id(2)

        @pl.when(k == 0)
        def _():
            acc_ref[...] = prod

        @pl.when(k > 0)
        def _():
            acc_ref[...] += prod

        @pl.when(k == nk - 1)
        def _():
            finish(acc_ref[...])

    if a_parts > 1:
        a_spec = pl.BlockSpec((None, tm, tk), lambda i, j, k: (lax.div(k, kpa), i, lax.rem(k, kpa)))
    else:
        a_spec = pl.BlockSpec((tk, tm), lambda i, j, k: (k, i)) if ta else pl.BlockSpec((tm, tk), lambda i, j, k: (i, k))
    if not b_chips:
        b_spec = pl.BlockSpec((tn, tk), lambda i, j, k: (j, k)) if tb else pl.BlockSpec((tk, tn), lambda i, j, k: (k, j))
    elif tb:
        b_spec = pl.BlockSpec((None, tn, tk), lambda i, j, k: (lax.div(k, kpb), j, lax.rem(k, kpb)))
    else:
        b_spec = pl.BlockSpec((None, tk, tn), lambda i, j, k: (lax.div(j, npb), k, lax.rem(j, npb)))
    in_specs = [a_spec, b_spec]
    args = [a, b]
    if bias is not None:
        in_specs.append(pl.BlockSpec((1, tn), lambda i, j, k: (0, j)))
        args.append(bias.reshape(1, N).astype(F32))
    if out_chips:
        out_shape = jax.ShapeDtypeStruct((N_CHIPS, M, N // N_CHIPS), out_dtype)
        out_spec = pl.BlockSpec((None, tm, tn), lambda i, j, k: (lax.div(j, npc), i, lax.rem(j, npc)))
    else:
        out_shape = jax.ShapeDtypeStruct((M, N), out_dtype)
        out_spec = pl.BlockSpec((tm, tn), lambda i, j, k: (i, j))
    return pl.pallas_call(
        body, name=name,
        out_shape=out_shape,
        grid=(M // tm, N // tn, nk),
        in_specs=in_specs,
        out_specs=out_spec,
        scratch_shapes=[pltpu.VMEM((tm, tn), F32)] if nk > 1 else [],
        compiler_params=_params(("parallel", "parallel", "arbitrary")),
    )(*args)


def rowwise(fn, tiled, per_seq, glob, out_tiled, out_seq, *, B, S, name, rows=512):
    tm = _tile(S, (rows, 128, 64, 32, 16, 8))
    nt = S // tm
    n_in = len(tiled) + len(per_seq) + len(glob)
    n_ot = len(out_tiled)

    def body(*refs):
        ins = refs[:n_in]
        outs = refs[n_in:]
        vals = fn(*[r[...] for r in ins])
        if not isinstance(vals, (tuple, list)):
            vals = (vals,)
        assert len(vals) == len(outs), (name, len(vals), len(outs))
        for o_ref, v in zip(outs[:n_ot], vals[:n_ot]):
            o_ref[...] = v.astype(o_ref.dtype)
        t = pl.program_id(1)
        for o_ref, v in zip(outs[n_ot:], vals[n_ot:]):
            def first(o_ref=o_ref, v=v):
                o_ref[...] = v.astype(F32)

            def later(o_ref=o_ref, v=v):
                o_ref[...] += v.astype(F32)

            pl.when(t == 0)(first)
            pl.when(t > 0)(later)

    in_specs = [pl.BlockSpec((tm, a.shape[1]), lambda b, t: (b * nt + t, 0)) for a in tiled]
    in_specs += [pl.BlockSpec((None, 1, a.shape[1]), lambda b, t: (b, 0, 0)) for a in per_seq]
    in_specs += [pl.BlockSpec(a.shape, lambda b, t: (0,) * a.ndim) for a in glob]
    out_shape = [jax.ShapeDtypeStruct((B * S, w), dt) for w, dt in out_tiled]
    out_shape += [jax.ShapeDtypeStruct((B, 1, w), F32) for w in out_seq]
    out_specs = [pl.BlockSpec((tm, w), lambda b, t: (b * nt + t, 0)) for w, _ in out_tiled]
    out_specs += [pl.BlockSpec((None, 1, w), lambda b, t: (b, 0, 0)) for w in out_seq]
    res = pl.pallas_call(
        body, name=name, out_shape=out_shape, grid=(B, nt), in_specs=in_specs, out_specs=out_specs,
        compiler_params=_params(("parallel", "arbitrary")),
    )(*tiled, *[a.reshape(B, 1, a.shape[1]) for a in per_seq], *glob)
    res = list(res)
    for i in range(n_ot, len(res)):
        res[i] = res[i].reshape(B, res[i].shape[-1])
    return res


def _rms(x):
    r = lax.rsqrt(jnp.mean(x * x, axis=-1, keepdims=True) + EPS)
    return x * r, r


def norm_mod_fwd(x, g, sh, sc, *, B, S, name):
    def fn(x, sh, sc, g):
        xn, _ = _rms(x)
        return (xn * g) * (1.0 + sc) + sh

    return rowwise(fn, [x], [sh, sc], [g.reshape(1, -1)], [(x.shape[1], BF16)], [], B=B, S=S, name=name)[0]


def _norm_mod_bwd_math(dh, x, sc, g):
    xn, r = _rms(x)
    y = xn * g
    dy = dh * (1.0 + sc)
    dxn = dy * g
    dx = r * (dxn - xn * jnp.mean(dxn * xn, axis=-1, keepdims=True))
    dsh = jnp.sum(dh, axis=0, keepdims=True)
    dsc = jnp.sum(dh * y, axis=0, keepdims=True)
    dg = jnp.sum(dy * xn, axis=0, keepdims=True)
    return dx, dsh, dsc, dg


def norm_mod_bwd(dh, x, dres, g, sc, *, B, S, name):
    D = x.shape[1]

    def fn(dh, x, dres, sc, g):
        dx, dsh, dsc, dg = _norm_mod_bwd_math(dh.astype(F32), x, sc, g)
        return dres + dx, dsh, dsc, dg

    return rowwise(fn, [dh, x, dres], [sc], [g.reshape(1, -1)], [(D, F32)], [D, D, D], B=B, S=S, name=name)


def gate_res_fwd(x, y, gate, *, B, S, name):
    return rowwise(lambda x, y, g: x + g * y, [x, y], [gate], [], [(x.shape[1], F32)], [], B=B, S=S, name=name)[0]


def gate_res_bwd(dx, y, gate, *, B, S, name):
    D = dx.shape[1]

    def fn(dx, y, g):
        return g * dx, jnp.sum(dx * y, axis=0, keepdims=True)

    return rowwise(fn, [dx, y], [gate], [], [(D, BF16)], [D], B=B, S=S, name=name)


def final_loss(x, tgt, g, sh, sc, *, B, S, name):
    D = x.shape[1]

    def fn(x, tgt, sh, sc, g):
        xn, _ = _rms(x)
        y = (xn * g) * (1.0 + sc) + sh
        err = y - tgt
        loss = 0.5 * jnp.sum(err * err, axis=0, keepdims=True) * (1.0 / D)
        dx, dsh, dsc, dg = _norm_mod_bwd_math(err * (1.0 / D), x, sc, g)
        return dx, loss, dsh, dsc, dg

    return rowwise(fn, [x, tgt], [sh, sc], [g.reshape(1, -1)], [(D, F32)], [D, D, D, D], B=B, S=S, name=name)


def _gelu(y):
    c0 = math.sqrt(2.0 / math.pi)
    t = jnp.tanh(c0 * (y + 0.044715 * (y * y * y)))
    return 0.5 * y * (1.0 + t), t


def _sigmoid(s):
    return 1.0 / (1.0 + jnp.exp(-s))


def gelu_fwd(y, *, B, S, name):
    return rowwise(lambda y: _gelu(y)[0], [y], [], [], [(y.shape[1], BF16)], [], B=B, S=S, name=name)[0]


def glu_fwd(y, s, *, B, S, name):
    return rowwise(lambda y, s: _gelu(y)[0] * _sigmoid(s), [y, s], [], [], [(y.shape[1], BF16)], [], B=B, S=S,
                   name=name)[0]


def glu_bwd1(y, s, dg, *, B, S, name):
    D = y.shape[1]

    def fn(y, s, dg):
        z = _gelu(y)[0]
        sig = _sigmoid(s)
        ds = dg * z * sig * (1.0 - sig)
        return ds, dg * sig, jnp.sum(ds, axis=0, keepdims=True)

    return rowwise(fn, [y, s, dg], [], [], [(D, BF16), (D, F32)], [D], B=B, S=S, name=name)


def glu_bwd2(y, dz1, dz2, *, B, S, name):
    D = y.shape[1]
    c0 = math.sqrt(2.0 / math.pi)

    def fn(y, dz1, dz2):
        _, t = _gelu(y)
        dgelu = 0.5 * (1.0 + t) + 0.5 * y * (1.0 - t * t) * c0 * (1.0 + 3.0 * 0.044715 * y * y)
        return (dz1 + dz2) * dgelu

    return rowwise(fn, [y, dz1, dz2], [], [], [(D, F32)], [], B=B, S=S, name=name)[0]


def silu_rows(c, *, name):
    R, W = c.shape
    return rowwise(lambda c: c * _sigmoid(c), [c], [], [], [(W, F32)], [], B=1, S=R, name=name)[0]


def _shift_down(cur, h6, h7):
    rows = lax.broadcasted_iota(jnp.int32, cur.shape, 0)
    m1 = jnp.where(rows == 0, h7, pltpu.roll(cur, 1, 0))
    m2 = jnp.where(rows == 0, h6, jnp.where(rows == 1, h7, pltpu.roll(cur, 2, 0)))
    return m1, m2


def _conv3(cur, halo_ref, w_ref, has_prev):
    h6 = jnp.where(has_prev, halo_ref[6:7, :], 0.0)
    h7 = jnp.where(has_prev, halo_ref[7:8, :], 0.0)
    m1, m2 = _shift_down(cur, h6, h7)
    return w_ref[2:3, :] * cur + w_ref[1:2, :] * m1 + w_ref[0:1, :] * m2, m1, m2


def _conv_tiles(S, F):
    ts = _tile(S, (1024, 512, 256, 128, 64, 32, 16, 8))
    tn = _tile(F, (256, 128))
    return ts, tn, S // ts, F // tn


def conv_gate_fwd(up, cw, cb, *, B, S, name):
    F = up.shape[1] // 2
    ts, tn, nts, nF = _conv_tiles(S, F)
    hb = ts // 8

    def body(g_ref, gh_ref, v_ref, vh_ref, wg_ref, wv_ref, bg_ref, bv_ref, o_ref):
        has_prev = pl.program_id(2) > 0
        gc = _conv3(g_ref[...], gh_ref, wg_ref, has_prev)[0] + bg_ref[...]
        vc = _conv3(v_ref[...], vh_ref, wv_ref, has_prev)[0] + bv_ref[...]
        o_ref[...] = (gc * _sigmoid(gc) * vc).astype(o_ref.dtype)

    def cur(off):
        return pl.BlockSpec((ts, tn), lambda b, j, t: (b * nts + t, j + off))

    def halo(off):
        return pl.BlockSpec((8, tn), lambda b, j, t: (jnp.maximum((b * nts + t) * hb - 1, 0), j + off))

    def vec(rows, off):
        return pl.BlockSpec((rows, tn), lambda b, j, t: (0, j + off))

    return pl.pallas_call(
        body, name=name, out_shape=jax.ShapeDtypeStruct((B * S, F), BF16), grid=(B, nF, nts),
        in_specs=[cur(0), halo(0), cur(nF), halo(nF), vec(3, 0), vec(3, nF), vec(1, 0), vec(1, nF)],
        out_specs=pl.BlockSpec((ts, tn), lambda b, j, t: (b * nts + t, j)),
        compiler_params=_params(("parallel", "parallel", "arbitrary")),
    )(up, up, up, up, cw, cw, cb, cb)


def conv_gate_bwd1(up, dact, cw, cb, *, B, S, name):
    F = up.shape[1] // 2
    ts, tn, nts, nF = _conv_tiles(S, F)
    hb = ts // 8

    def body(g_ref, gh_ref, v_ref, vh_ref, da_ref, wg_ref, wv_ref, bg_ref, bv_ref, d_ref, db_ref):
        t = pl.program_id(2)
        has_prev = t > 0
        gc = _conv3(g_ref[...], gh_ref, wg_ref, has_prev)[0] + bg_ref[...]
        vc = _conv3(v_ref[...], vh_ref, wv_ref, has_prev)[0] + bv_ref[...]
        sig = _sigmoid(gc)
        da = da_ref[...]
        dg = da * vc * (sig * (1.0 + gc * (1.0 - sig)))
        dv = da * (gc * sig)
        d_ref[0] = dg
        d_ref[1] = dv
        part = jnp.concatenate([jnp.sum(dg, axis=0, keepdims=True), jnp.sum(dv, axis=0, keepdims=True)], axis=0)

        @pl.when(t == 0)
        def _():
            db_ref[...] = part

        @pl.when(t > 0)
        def _():
            db_ref[...] += part

    def cur(off):
        return pl.BlockSpec((ts, tn), lambda b, j, t: (b * nts + t, j + off))

    def halo(off):
        return pl.BlockSpec((8, tn), lambda b, j, t: (jnp.maximum((b * nts + t) * hb - 1, 0), j + off))

    def vec(rows, off):
        return pl.BlockSpec((rows, tn), lambda b, j, t: (0, j + off))

    return pl.pallas_call(
        body, name=name,
        out_shape=[jax.ShapeDtypeStruct((2, B * S, F), F32), jax.ShapeDtypeStruct((B, 2, F), F32)],
        grid=(B, nF, nts),
        in_specs=[cur(0), halo(0), cur(nF), halo(nF), cur(0), vec(3, 0), vec(3, nF), vec(1, 0), vec(1, nF)],
        out_specs=[pl.BlockSpec((2, ts, tn), lambda b, j, t: (0, b * nts + t, j)),
                   pl.BlockSpec((None, 2, tn), lambda b, j, t: (b, 0, j))],
        compiler_params=_params(("parallel", "parallel", "arbitrary")),
    )(up, up, up, up, dact, cw, cw, cb, cb)


def conv_bwd2(d3, up, cw, *, B, S, name):
    F = up.shape[1] // 2
    ts, tn, nts, nF = _conv_tiles(S, F)
    hb = ts // 8
    last8 = B * S // 8 - 1

    def body(d_ref, da_ref, u_ref, uh_ref, w_ref, o_ref, dw_ref):
        t = pl.program_id(3)
        d = d_ref[...]
        has_next = t < nts - 1
        a0 = jnp.where(has_next, da_ref[0:1, :], 0.0)
        a1 = jnp.where(has_next, da_ref[1:2, :], 0.0)
        rows = lax.broadcasted_iota(jnp.int32, d.shape, 0)
        p1 = jnp.where(rows == ts - 1, a0, pltpu.roll(d, ts - 1, 0))
        p2 = jnp.where(rows == ts - 1, a1, jnp.where(rows == ts - 2, a0, pltpu.roll(d, ts - 2, 0)))
        o_ref[...] = (w_ref[2:3, :] * d + w_ref[1:2, :] * p1 + w_ref[0:1, :] * p2).astype(o_ref.dtype)
        u = u_ref[...]
        has_prev = t > 0
        h6 = jnp.where(has_prev, uh_ref[6:7, :], 0.0)
        h7 = jnp.where(has_prev, uh_ref[7:8, :], 0.0)
        m1, m2 = _shift_down(u, h6, h7)
        part = jnp.concatenate([jnp.sum(d * m2, axis=0, keepdims=True), jnp.sum(d * m1, axis=0, keepdims=True),
                                jnp.sum(d * u, axis=0, keepdims=True)], axis=0)

        @pl.when(t == 0)
        def _():
            dw_ref[...] = part

        @pl.when(t > 0)
        def _():
            dw_ref[...] += part

    return pl.pallas_call(
        body, name=name,
        out_shape=[jax.ShapeDtypeStruct((B * S, 2 * F), BF16), jax.ShapeDtypeStruct((B, 3, 2 * F), F32)],
        grid=(B, 2, nF, nts),
        in_specs=[
            pl.BlockSpec((None, ts, tn), lambda b, g, j, t: (g, b * nts + t, j)),
            pl.BlockSpec((None, 8, tn), lambda b, g, j, t: (g, jnp.minimum((b * nts + t + 1) * hb, last8), j)),
            pl.BlockSpec((ts, tn), lambda b, g, j, t: (b * nts + t, g * nF + j)),
            pl.BlockSpec((8, tn), lambda b, g, j, t: (jnp.maximum((b * nts + t) * hb - 1, 0), g * nF + j)),
            pl.BlockSpec((3, tn), lambda b, g, j, t: (0, g * nF + j)),
        ],
        out_specs=[pl.BlockSpec((ts, tn), lambda b, g, j, t: (b * nts + t, g * nF + j)),
                   pl.BlockSpec((None, 3, tn), lambda b, g, j, t: (b, 0, g * nF + j))],
        compiler_params=_params(("parallel", "parallel", "parallel", "arbitrary")),
    )(d3, d3, up, up, cw)


def conv_gate_bwd(up, dact, cw, cb, *, B, S, name):
    F = up.shape[1] // 2
    tn = _tile(F, (256, 128))
    nF = F // tn

    def body(g_ref, v_ref, da_ref, wg_ref, wv_ref, bg_ref, bv_ref, o_ref, dw_ref, db_ref):
        rows = lax.broadcasted_iota(jnp.int32, (S, tn), 0)

        def earlier(x, k):
            return jnp.where(rows >= k, pltpu.roll(x, k, 0), 0.0)

        def later(x, k):
            return jnp.where(rows < S - k, pltpu.roll(x, S - k, 0), 0.0)

        def conv(x, w_ref):
            x1, x2 = earlier(x, 1), earlier(x, 2)
            return w_ref[2:3, :] * x + w_ref[1:2, :] * x1 + w_ref[0:1, :] * x2, x1, x2

        def back(d, x, x1, x2, w_ref, half):
            o_ref[half] = (w_ref[2:3, :] * d + w_ref[1:2, :] * later(d, 1) + w_ref[0:1, :] * later(d, 2)
                           ).astype(o_ref.dtype)
            dw_ref[half] = jnp.concatenate([jnp.sum(d * x2, axis=0, keepdims=True),
                                            jnp.sum(d * x1, axis=0, keepdims=True),
                                            jnp.sum(d * x, axis=0, keepdims=True)], axis=0)
            return jnp.sum(d, axis=0, keepdims=True)

        g, v, da = g_ref[...], v_ref[...], da_ref[...]
        gc, g1, g2 = conv(g, wg_ref)
        vc, v1, v2 = conv(v, wv_ref)
        gc = gc + bg_ref[...]
        vc = vc + bv_ref[...]
        sig = _sigmoid(gc)
        dg = da * vc * (sig * (1.0 + gc * (1.0 - sig)))
        dv = da * (gc * sig)
        db_ref[...] = jnp.concatenate([back(dg, g, g1, g2, wg_ref, 0), back(dv, v, v1, v2, wv_ref, 1)], axis=0)

    def cols(off):
        return pl.BlockSpec((S, tn), lambda b, j: (b, j + off))

    def vec(rows, off):
        return pl.BlockSpec((rows, tn), lambda b, j: (0, j + off))

    return pl.pallas_call(
        body, name=name,
        out_shape=[jax.ShapeDtypeStruct((2, B * S, F), BF16), jax.ShapeDtypeStruct((B, 2, 3, F), F32),
                   jax.ShapeDtypeStruct((B, 2, F), F32)],
        grid=(B, nF),
        in_specs=[cols(0), cols(nF), cols(0), vec(3, 0), vec(3, nF), vec(1, 0), vec(1, nF)],
        out_specs=[pl.BlockSpec((2, S, tn), lambda b, j: (0, b, j)),
                   pl.BlockSpec((None, 2, 3, tn), lambda b, j: (b, 0, 0, j)),
                   pl.BlockSpec((None, 2, tn), lambda b, j: (b, 0, j))],
        compiler_params=_params(("parallel", "parallel")),
    )(up, up, dact, cw, cw, cb, cb)


MASKED_LOG = -1e30


def _split2(x):
    bits = lax.bitcast_convert_type(x, jnp.uint32) & jnp.uint32(0xFFFF0000)
    hi = lax.bitcast_convert_type(bits, F32)
    return hi.astype(BF16), (x - hi).astype(BF16)


def _split_dot(x, m):
    hi, lo = _split2(x)
    return jnp.dot(hi, m, preferred_element_type=F32) + jnp.dot(lo, m, preferred_element_type=F32)


def _nt(a, b):
    return lax.dot_general(a, b, (((1,), (1,)), ((), ())), preferred_element_type=F32)


def _tn(a, b):
    return lax.dot_general(a, b, (((0,), (0,)), ((), ())), preferred_element_type=F32)


def _att_scores(q, k, mask, prescaled=False):
    z = _nt(q, k)
    if not prescaled:
        z = z * (HEAD_DIM ** -0.5)
    e = jnp.exp(-jnp.abs(z))
    sp = jnp.log(1.0 + e)
    lb = jnp.minimum(z, 0.0) - sp
    l1 = lb - z
    if mask is not None:
        lb = jnp.where(mask, lb, MASKED_LOG)
        l1 = jnp.where(mask, l1, 0.0)
    return z, lb, l1, e


def _col_to_row(col, eye):
    return jnp.sum(jnp.where(eye, col, 0.0), axis=0, keepdims=True)


def _row_to_col(row, eye):
    return jnp.sum(jnp.where(eye, row, 0.0), axis=1, keepdims=True)


def attn_fwd(q, k, v, *, name):
    B, H, S, dh = q.shape
    T = ATT_BLOCK
    nq = S // T

    G = _tile(H, (ATT_HEADS, 2))

    def body(q_ref, k_ref, v_ref, o_ref, l_ref):
        r = lax.broadcasted_iota(jnp.int32, (T, T), 0)
        c = lax.broadcasted_iota(jnp.int32, (T, T), 1)
        later = (r > c).astype(BF16)
        eye = r == c
        diag = c < r
        blk = lax.broadcasted_iota(jnp.int32, (nq, T), 0)

        later2 = jnp.concatenate([later, later], axis=0)

        def scores(g, qb, k0, mask):
            _, lb, l1, _ = _att_scores(qb, k_ref[g, pl.ds(k0, T), :], mask)
            return lb, jnp.concatenate(_split2(l1), axis=1), jnp.sum(l1, axis=1, keepdims=True)

        def weigh_all(k0, sc, st):
            suf = jnp.dot(jnp.concatenate([s[1] for s in sc], axis=0), later2, preferred_element_type=F32)
            out = []
            for g in range(G):
                lb, _, rowsum = sc[g]
                acc, run = st[g]
                w = jnp.exp(lb + suf[g * T:(g + 1) * T] + run)
                acc = acc + jnp.dot(w.astype(BF16), v_ref[g, pl.ds(k0, T), :], preferred_element_type=F32)
                out.append((acc, run + rowsum))
            return tuple(out)

        def qblock(i, totals):
            q0 = pl.multiple_of(i * T, T)
            qbs = [q_ref[g, pl.ds(q0, T), :] for g in range(G)]
            sc0 = tuple(scores(g, qbs[g], q0, diag) for g in range(G))
            st0 = tuple((jnp.zeros((T, dh), F32), jnp.zeros((T, 1), F32)) for _ in range(G))

            def kblock(jj, carry):
                sc, st = carry
                k_next = pl.multiple_of((i - jj) * T, T)
                k_cur = pl.multiple_of((i - jj + 1) * T, T)
                st = weigh_all(k_cur, sc, st)
                sc_next = tuple(scores(g, qbs[g], k_next, None) for g in range(G))
                return sc_next, st

            sc, st = lax.fori_loop(1, i + 1, kblock, (sc0, st0))
            st = weigh_all(0, sc, st)
            for g in range(G):
                o_ref[g, pl.ds(q0, T), :] = st[g][0]
            return tuple(jnp.where(blk == i, _col_to_row(st[g][1], eye), totals[g]) for g in range(G))

        totals = lax.fori_loop(0, nq, qblock, tuple(jnp.zeros((nq, T), F32) for _ in range(G)))
        for g in range(G):
            l_ref[g] = totals[g]

    spec = pl.BlockSpec((None, G, S, dh), lambda b, h: (b, h, 0, 0))
    lspec = pl.BlockSpec((None, G, nq, T), lambda b, h: (b, h, 0, 0))
    return pl.pallas_call(
        body, name=name,
        out_shape=[jax.ShapeDtypeStruct((B, H, S, dh), F32), jax.ShapeDtypeStruct((B, H, nq, T), F32)],
        grid=(B, H // G), in_specs=[spec, spec, spec], out_specs=[spec, lspec],
        compiler_params=_params(("parallel", "parallel")),
    )(q, k, v)


def attn_bwd(q, k, v, ltot, do, *, name):
    B, H, S, dh = q.shape
    T = ATT_BLOCK
    nq = S // T
    scale = HEAD_DIM ** -0.5

    G = _tile(H, (ATT_HEADS_BWD, 2))

    def body(q_ref, k_ref, v_ref, l_ref, do_ref, dq_ref, dk_ref, dv_ref, dk_acc, dv_acc):
        r = lax.broadcasted_iota(jnp.int32, (T, T), 0)
        c = lax.broadcasted_iota(jnp.int32, (T, T), 1)
        upto = (r <= c).astype(BF16)
        before = (r < c).astype(BF16)
        upto2 = jnp.concatenate([upto, upto], axis=0)
        before2 = jnp.concatenate([before, before], axis=0)
        eye = r == c
        diag = c < r
        blk = lax.broadcasted_iota(jnp.int32, (nq, T), 0)
        dk_acc[...] = jnp.zeros_like(dk_acc)
        dv_acc[...] = jnp.zeros_like(dv_acc)

        def scores(g, qb, dob, k0, mask):
            z, lb, l1, e = _att_scores(qb, k_ref[g, pl.ds(k0, T), :], mask)
            inv = 1.0 / (1.0 + e)
            small = e * inv
            pos = z >= 0.0
            beta = jnp.where(pos, inv, small)
            omb = jnp.where(pos, small, inv)
            if mask is not None:
                beta = jnp.where(mask, beta, 0.0)
            dw = _nt(dob, v_ref[g, pl.ds(k0, T), :])
            return lb, jnp.concatenate(_split2(l1), axis=1), jnp.sum(l1, axis=1, keepdims=True), dw, beta, omb

        def grads_all(qbs, dobs, tots, k0, sc, st):
            pre = jnp.dot(jnp.concatenate([s[1] for s in sc], axis=0), upto2, preferred_element_type=F32)
            dlws = []
            for g in range(G):
                lb = sc[g][0]
                w = jnp.exp(lb + (tots[g] - (pre[g * T:(g + 1) * T] + st[g][1])))
                dv_acc[g, pl.ds(k0, T), :] += _tn(w.astype(BF16), dobs[g])
                dlws.append(sc[g][3] * w)
            pre_d = jnp.dot(jnp.concatenate([jnp.concatenate(_split2(d), axis=1) for d in dlws], axis=0), before2,
                            preferred_element_type=F32)
            out = []
            for g in range(G):
                _, _, rowsum, _, beta, omb = sc[g]
                dq, run_l, run_d = st[g]
                dl1 = pre_d[g * T:(g + 1) * T] + run_d
                dz = ((dlws[g] * omb - dl1 * beta) * scale).astype(BF16)
                dq = dq + jnp.dot(dz, k_ref[g, pl.ds(k0, T), :], preferred_element_type=F32)
                dk_acc[g, pl.ds(k0, T), :] += _tn(dz, qbs[g])
                out.append((dq, run_l + rowsum, run_d + jnp.sum(dlws[g], axis=1, keepdims=True)))
            return tuple(out)

        def block_inputs(i, q0):
            qbs = [q_ref[g, pl.ds(q0, T), :] for g in range(G)]
            dobs = [do_ref[g, pl.ds(q0, T), :] for g in range(G)]
            tots = [_row_to_col(jnp.sum(jnp.where(blk == i, l_ref[g], 0.0), axis=0, keepdims=True), eye)
                    for g in range(G)]
            z1 = jnp.zeros((T, 1), F32)
            return qbs, dobs, tots, tuple((jnp.zeros((T, dh), F32), z1, z1) for _ in range(G))

        qbs, dobs, tots, st = block_inputs(0, 0)
        st = grads_all(qbs, dobs, tots, 0, tuple(scores(g, qbs[g], dobs[g], 0, diag) for g in range(G)), st)
        for g in range(G):
            dq_ref[g, 0:T, :] = st[g][0].astype(dq_ref.dtype)

        def qblock(i, carry0):
            q0 = pl.multiple_of(i * T, T)
            qbs, dobs, tots, st0 = block_inputs(i, q0)
            sc0 = tuple(scores(g, qbs[g], dobs[g], 0, None) for g in range(G))

            def kblock(j, carry):
                sc, st = carry
                k_cur = pl.multiple_of(j * T, T)
                k_next = pl.multiple_of((j + 1) * T, T)
                sc_next = tuple(scores(g, qbs[g], dobs[g], k_next, None) for g in range(G))
                st = grads_all(qbs, dobs, tots, k_cur, sc, st)
                return sc_next, st

            sc, st = lax.fori_loop(0, i - 1, kblock, (sc0, st0))
            k_last = pl.multiple_of((i - 1) * T, T)
            st = grads_all(qbs, dobs, tots, k_last, sc, st)
            sc_diag = tuple(scores(g, qbs[g], dobs[g], q0, diag) for g in range(G))
            st = grads_all(qbs, dobs, tots, q0, sc_diag, st)
            for g in range(G):
                dq_ref[g, pl.ds(q0, T), :] = st[g][0].astype(dq_ref.dtype)
            return carry0

        lax.fori_loop(1, nq, qblock, 0)
        dk_ref[...] = dk_acc[...].astype(dk_ref.dtype)
        dv_ref[...] = dv_acc[...].astype(dv_ref.dtype)

    spec = pl.BlockSpec((None, G, S, dh), lambda b, h: (b, h, 0, 0))
    lspec = pl.BlockSpec((None, G, nq, T), lambda b, h: (b, h, 0, 0))
    shp = jax.ShapeDtypeStruct((B, H, S, dh), BF16)
    return pl.pallas_call(
        body, name=name, out_shape=[shp, shp, shp], grid=(B, H // G),
        in_specs=[spec, spec, spec, lspec, spec], out_specs=[spec] * 3,
        scratch_shapes=[pltpu.VMEM((G, S, dh), F32), pltpu.VMEM((G, S, dh), F32)],
        compiler_params=_params(("parallel", "parallel")),
    )(q, k, v, ltot, do)


def _wide_consts(T, W):
    r = lax.broadcasted_iota(jnp.int32, (W, W), 0)
    c = lax.broadcasted_iota(jnp.int32, (W, W), 1)
    two = lambda m: jnp.concatenate([m.astype(BF16)] * 2, axis=0)
    qrow = lax.broadcasted_iota(jnp.int32, (T, W), 0)
    kcol = lax.broadcasted_iota(jnp.int32, (T, W), 1)
    er = lax.broadcasted_iota(jnp.int32, (T, T), 0)
    ec = lax.broadcasted_iota(jnp.int32, (T, T), 1)
    return two(r > c), two(r <= c), two(r < c), qrow, kcol, er == ec


def attn_fwd_wide(q, k, v, shards=(), slots=(), *, name):
    B, H, S, dh = q.shape
    T = ATT_BLOCK
    W = 2 * T
    nq = S // T
    G = _tile(H, (ATT_HEADS, 2))
    n = len(shards)
    n_steps = B * (H // G)

    def body(*refs):
        q_ref, k_ref, v_ref = refs[:3]
        o_ref, l_ref = refs[3 + 2 * n:5 + 2 * n]
        step_id = pl.program_id(0) * (H // G) + pl.program_id(1)
        if n:
            gather = _ShardGather(refs[3:3 + n], refs[5 + 2 * n:5 + 3 * n], *refs[5 + 3 * n:])
            pl.when(step_id == 0)(gather.send)
            pl.when(step_id == n_steps - 1)(gather.forward)
        later2, _, _, qrow, kcol, eye = _wide_consts(T, W)
        blk = lax.broadcasted_iota(jnp.int32, (nq, T), 0)

        def step(qbs, k0, st, mask):
            parts, lbs, sums = [], [], []
            for g in range(G):
                _, lb, l1, _ = _att_scores(qbs[g], k_ref[g, pl.ds(k0, W), :], mask, prescaled=True)
                parts.append(jnp.concatenate(_split2(l1), axis=1))
                lbs.append(lb)
                sums.append(jnp.sum(l1, axis=1, keepdims=True))
            suf = jnp.dot(jnp.concatenate(parts, axis=0), later2, preferred_element_type=F32)
            out = []
            for g in range(G):
                acc, run = st[g]
                w = jnp.exp(lbs[g] + suf[g * T:(g + 1) * T] + run)
                acc = acc + jnp.dot(w.astype(BF16), v_ref[g, pl.ds(k0, W), :], preferred_element_type=F32)
                out.append((acc, run + sums[g]))
            return tuple(out)

        def qblock(i, totals):
            q0 = pl.multiple_of(i * T, T)
            qbs = [q_ref[g, pl.ds(q0, T), :] * (HEAD_DIM ** -0.5) for g in range(G)]
            half = jnp.right_shift(i, 1)
            last = half * W
            k_last = pl.multiple_of(last, W)
            mask = (k_last + kcol) < (q0 + qrow)
            st = tuple((jnp.zeros((T, dh), F32), jnp.zeros((T, 1), F32)) for _ in range(G))
            st = step(qbs, k_last, st, mask)

            def kblock(jj, st):
                return step(qbs, pl.multiple_of(last - jj * W, W), st, None)

            st = lax.fori_loop(1, half + 1, kblock, st)
            for g in range(G):
                o_ref[g, pl.ds(q0, T), :] = st[g][0]
            return tuple(jnp.where(blk == i, _col_to_row(st[g][1], eye), totals[g]) for g in range(G))

        totals = lax.fori_loop(0, nq, qblock, tuple(jnp.zeros((nq, T), F32) for _ in range(G)))
        for g in range(G):
            l_ref[g] = totals[g]
        if n:
            pl.when(step_id == n_steps - 1)(gather.finish)

    spec = pl.BlockSpec((None, G, S, dh), lambda b, h: (b, h, 0, 0))
    lspec = pl.BlockSpec((None, G, nq, T), lambda b, h: (b, h, 0, 0))
    dma = pltpu.SemaphoreType.DMA
    return pl.pallas_call(
        body, name=name,
        out_shape=[jax.ShapeDtypeStruct((B, H, S, dh), F32), jax.ShapeDtypeStruct((B, H, nq, T), F32)]
        + [jax.ShapeDtypeStruct(s.shape, s.dtype) for s in slots],
        grid=(B, H // G), in_specs=[spec, spec, spec] + _any_specs(2 * n), out_specs=[spec, lspec] + _any_specs(n),
        input_output_aliases={3 + n + i: 2 + i for i in range(n)},
        scratch_shapes=[dma((3 * n,))] * 4 if n else [],
        compiler_params=_params(("arbitrary", "arbitrary")),
    )(q, k, v, *shards, *slots)


def attn_bwd_wide(q, k, v, ltot, do, partials=(), *, name):
    B, H, S, dh = q.shape
    T = ATT_BLOCK
    W = 2 * T
    nq = S // T
    scale = HEAD_DIM ** -0.5
    G = _tile(H, (ATT_HEADS_BWD, 2))
    n = len(partials)
    n_steps = B * (H // G)

    def body(*refs):
        q_ref, k_ref, v_ref, l_ref, do_ref = refs[:5]
        dq_ref, dk_ref, dv_ref = refs[5 + n:8 + n]
        dk_acc, dv_acc = refs[8 + 2 * n:10 + 2 * n]
        step_id = pl.program_id(0) * (H // G) + pl.program_id(1)
        if n:
            scatter = _ChipScatter(refs[5:5 + n], refs[8 + n:8 + 2 * n], *refs[10 + 2 * n:])
            pl.when(step_id == 0)(scatter.send)
        _, upto2, before2, qrow, kcol, eye = _wide_consts(T, W)
        blk = lax.broadcasted_iota(jnp.int32, (nq, T), 0)
        dk_acc[...] = jnp.zeros_like(dk_acc)
        dv_acc[...] = jnp.zeros_like(dv_acc)

        def step(qbs, dobs, tots, k0, st, mask):
            sc = []
            for g in range(G):
                z, lb, l1, e = _att_scores(qbs[g], k_ref[g, pl.ds(k0, W), :], mask, prescaled=True)
                inv = 1.0 / (1.0 + e)
                small = e * inv
                pos = z >= 0.0
                beta = jnp.where(pos, inv, small)
                omb = jnp.where(pos, small, inv)
                if mask is not None:
                    beta = jnp.where(mask, beta, 0.0)
                dw = _nt(dobs[g], v_ref[g, pl.ds(k0, W), :])
                sc.append((lb, jnp.concatenate(_split2(l1), axis=1), jnp.sum(l1, axis=1, keepdims=True), dw, beta, omb))
            pre = jnp.dot(jnp.concatenate([s[1] for s in sc], axis=0), upto2, preferred_element_type=F32)
            dlws = []
            for g in range(G):
                w = jnp.exp(sc[g][0] + (tots[g] - (pre[g * T:(g + 1) * T] + st[g][1])))
                dv_acc[g, pl.ds(k0, W), :] += _tn(w.astype(BF16), dobs[g])
                dlws.append(sc[g][3] * w)
            pre_d = jnp.dot(jnp.concatenate([jnp.concatenate(_split2(d), axis=1) for d in dlws], axis=0), before2,
                            preferred_element_type=F32)
            out = []
            for g in range(G):
                _, _, rowsum, _, beta, omb = sc[g]
                dq, run_l, run_d = st[g]
                dl1 = pre_d[g * T:(g + 1) * T] + run_d
                dz = (dlws[g] * omb - dl1 * beta).astype(BF16)
                dq = dq + jnp.dot(dz, k_ref[g, pl.ds(k0, W), :], preferred_element_type=F32)
                dk_acc[g, pl.ds(k0, W), :] += _tn(dz, qbs[g])
                out.append((dq, run_l + rowsum, run_d + jnp.sum(dlws[g], axis=1, keepdims=True)))
            return tuple(out)

        def qblock(i, carry0):
            q0 = pl.multiple_of(i * T, T)
            qbs = [q_ref[g, pl.ds(q0, T), :] * scale for g in range(G)]
            dobs = [do_ref[g, pl.ds(q0, T), :] for g in range(G)]
            tots = [_row_to_col(jnp.sum(jnp.where(blk == i, l_ref[g], 0.0), axis=0, keepdims=True), eye)
                    for g in range(G)]
            z1 = jnp.zeros((T, 1), F32)
            st = tuple((jnp.zeros((T, dh), F32), z1, z1) for _ in range(G))

            def kblock(j, st):
                return step(qbs, dobs, tots, pl.multiple_of(j * W, W), st, None)

            half = jnp.right_shift(i, 1)
            st = lax.fori_loop(0, half, kblock, st)
            k_last = pl.multiple_of(half * W, W)
            st = step(qbs, dobs, tots, k_last, st, (k_last + kcol) < (q0 + qrow))
            for g in range(G):
                dq_ref[g, pl.ds(q0, T), :] = (st[g][0] * scale).astype(dq_ref.dtype)
            return carry0

        lax.fori_loop(0, nq, qblock, 0)
        dk_ref[...] = dk_acc[...].astype(dk_ref.dtype)
        dv_ref[...] = dv_acc[...].astype(dv_ref.dtype)
        if n:
            pl.when(step_id == n_steps - 1)(scatter.finish)

    spec = pl.BlockSpec((None, G, S, dh), lambda b, h: (b, h, 0, 0))
    lspec = pl.BlockSpec((None, G, nq, T), lambda b, h: (b, h, 0, 0))
    shp = jax.ShapeDtypeStruct((B, H, S, dh), BF16)
    dma = pltpu.SemaphoreType.DMA
    return pl.pallas_call(
        body, name=name,
        out_shape=[shp, shp, shp] + [jax.ShapeDtypeStruct((N_CHIPS - 1,) + a.shape[1:], a.dtype) for a in partials],
        grid=(B, H // G),
        in_specs=[spec, spec, spec, lspec, spec] + _any_specs(n), out_specs=[spec] * 3 + _any_specs(n),
        scratch_shapes=[pltpu.VMEM((G, S, dh), F32), pltpu.VMEM((G, S, dh), F32)]
        + ([dma((3 * n,)), dma((3 * n,))] if n else []),
        compiler_params=_params(("arbitrary", "arbitrary")),
    )(q, k, v, ltot, do, *partials)


def _cmul(ar, ai, br, bi):
    return ar * br - ai * bi, ar * bi + ai * br


def _cpow(lr, li, n):
    rr, ri = None, None
    br, bi = lr, li
    while n:
        if n & 1:
            rr, ri = (br, bi) if rr is None else _cmul(rr, ri, br, bi)
        n >>= 1
        if n:
            br, bi = _cmul(br, bi, br, bi)
    return rr, ri


def _ssm_scan(sr, si, lr, li, n_steps, reverse):
    W = sr.shape[1]
    R = SEGMENTS
    lim = -li if reverse else li
    zero = jnp.zeros((R, W), F32)

    def row(k):
        i = (n_steps - 1 - k) if reverse else k
        return pl.multiple_of(i * R, R)

    def local(k, st):
        cr, ci = st
        r0 = row(k)
        pr, pi = _cmul(lr, lim, cr, ci)
        nr = pr + sr[pl.ds(r0, R), :]
        ni = pi + si[pl.ds(r0, R), :]
        sr[pl.ds(r0, R), :] = nr
        si[pl.ds(r0, R), :] = ni
        return nr, ni

    er, ei = lax.fori_loop(0, n_steps, local, (zero, zero), unroll=SCAN_UNROLL)
    lnr, lni = _cpow(lr, lim, n_steps)
    rows = lax.broadcasted_iota(jnp.int32, (R, W), 0)
    cr, ci = zero, zero
    for step in range(1, R):
        tr, ti = _cmul(lnr, lni, cr, ci)
        tr, ti = tr + er, ti + ei
        if reverse:
            seg = R - 1 - step
            tr, ti = pltpu.roll(tr, R - 1, 0), pltpu.roll(ti, R - 1, 0)
        else:
            seg = step
            tr, ti = pltpu.roll(tr, 1, 0), pltpu.roll(ti, 1, 0)
        cr = jnp.where(rows == seg, tr, cr)
        ci = jnp.where(rows == seg, ti, ci)

    def fix(k, st):
        pr, pi = st
        r0 = row(k)
        ar, ai = _cmul(pr, pi, cr, ci)
        sr[pl.ds(r0, R), :] += ar
        si[pl.ds(r0, R), :] += ai
        return _cmul(lr, lim, pr, pi)

    lax.fori_loop(0, n_steps, fix, (lr, lim), unroll=SCAN_UNROLL)
    return cr, ci


def _ssm_specs(S, W):
    CH = GROUPS_PER_BLOCK * SSM_GROUP
    return dict(
        rows=pl.BlockSpec((S, CH), lambda b, j: (b, j)),
        b=pl.BlockSpec((None, CH, W), lambda b, j: (j, 0, 0)),
        c=pl.BlockSpec((None, W, CH), lambda b, j: (j, 0, 0)),
        lam=pl.BlockSpec((None, SEGMENTS, W), lambda b, j: (j, 0, 0)),
        vec=pl.BlockSpec((1, CH), lambda b, j: (0, j)),
    )


def ssm_fwd(u, bre, bim, cre, cim, lr8, li8, dsk, *, B, S, name):
    D = u.shape[1]
    J, CH, W = bre.shape
    n_steps = S // SEGMENTS
    sp = _ssm_specs(S, W)

    def body(u_ref, bre_ref, bim_ref, cre_ref, cim_ref, lr_ref, li_ref, dsk_ref, y_ref, sr, si):
        u = u_ref[...]
        ub = u.astype(BF16)
        sr[...] = jnp.dot(ub, bre_ref[...], preferred_element_type=F32)
        si[...] = jnp.dot(ub, bim_ref[...], preferred_element_type=F32)
        _ssm_scan(sr, si, lr_ref[...], li_ref[...], n_steps, False)
        y = jnp.dot(sr[...].astype(BF16), cre_ref[...], preferred_element_type=F32)
        y = y - jnp.dot(si[...].astype(BF16), cim_ref[...], preferred_element_type=F32)
        y_ref[...] = y + dsk_ref[...] * u

    return pl.pallas_call(
        body, name=name, out_shape=jax.ShapeDtypeStruct((B * S, D), F32), grid=(B, J),
        in_specs=[sp["rows"], sp["b"], sp["b"], sp["c"], sp["c"], sp["lam"], sp["lam"], sp["vec"]],
        out_specs=sp["rows"],
        scratch_shapes=[pltpu.VMEM((S, W), F32), pltpu.VMEM((S, W), F32)],
        compiler_params=_params(("parallel", "parallel")),
    )(u, bre, bim, cre, cim, lr8, li8, dsk)


def ssm_bwd(u, dy, bre, bim, cre, cim, lr8, li8, dsk, *, B, S, name):
    D = u.shape[1]
    J, CH, W = bre.shape
    n_steps = S // SEGMENTS
    sp = _ssm_specs(S, W)

    def body(u_ref, dy_ref, bre_ref, bim_ref, cre_ref, cim_ref, lr_ref, li_ref, dsk_ref,
             du_ref, dbre_ref, dbim_ref, dcre_ref, dcim_ref, dlr_ref, dli_ref, ddsk_ref, sr, si, ar, ai):
        u = u_ref[...]
        dy = dy_ref[...]
        ub = u.astype(BF16)
        dyb = dy.astype(BF16)
        lr, li = lr_ref[...], li_ref[...]
        sr[...] = jnp.dot(ub, bre_ref[...], preferred_element_type=F32)
        si[...] = jnp.dot(ub, bim_ref[...], preferred_element_type=F32)
        cr, ci = _ssm_scan(sr, si, lr, li, n_steps, False)
        ar[...] = _nt(dyb, cre_ref[...])
        ai[...] = -_nt(dyb, cim_ref[...])
        _ssm_scan(ar, ai, lr, li, n_steps, True)

        def dlam(k, st):
            dr, di = st
            r0 = pl.multiple_of((k + 1) * SEGMENTS, SEGMENTS)
            p0 = pl.multiple_of(k * SEGMENTS, SEGMENTS)
            pr, pi = sr[pl.ds(p0, SEGMENTS), :], si[pl.ds(p0, SEGMENTS), :]
            xr, xi = ar[pl.ds(r0, SEGMENTS), :], ai[pl.ds(r0, SEGMENTS), :]
            return dr + pr * xr + pi * xi, di + pr * xi - pi * xr

        xr, xi = ar[0:SEGMENTS, :], ai[0:SEGMENTS, :]
        dr, di = lax.fori_loop(0, n_steps - 1, dlam, (cr * xr + ci * xi, cr * xi - ci * xr), unroll=SCAN_UNROLL)
        dlr_ref[...] = dr
        dli_ref[...] = di
        arb = ar[...].astype(BF16)
        aib = ai[...].astype(BF16)
        du_ref[...] = _nt(arb, bre_ref[...]) + _nt(aib, bim_ref[...]) + dsk_ref[...] * dy
        dbre_ref[...] = _tn(ub, arb)
        dbim_ref[...] = _tn(ub, aib)
        dcre_ref[...] = _tn(sr[...].astype(BF16), dyb)
        dcim_ref[...] = -_tn(si[...].astype(BF16), dyb)
        ddsk_ref[...] = jnp.sum(dy * u, axis=0, keepdims=True)

    def per(shape):
        return pl.BlockSpec((None, None) + shape, lambda b, j: (b, j, 0, 0))

    return pl.pallas_call(
        body, name=name,
        out_shape=[jax.ShapeDtypeStruct((B * S, D), F32),
                   jax.ShapeDtypeStruct((B, J, CH, W), F32), jax.ShapeDtypeStruct((B, J, CH, W), F32),
                   jax.ShapeDtypeStruct((B, J, W, CH), F32), jax.ShapeDtypeStruct((B, J, W, CH), F32),
                   jax.ShapeDtypeStruct((B, J, SEGMENTS, W), F32), jax.ShapeDtypeStruct((B, J, SEGMENTS, W), F32),
                   jax.ShapeDtypeStruct((B, J, 1, CH), F32)],
        grid=(B, J),
        in_specs=[sp["rows"], sp["rows"], sp["b"], sp["b"], sp["c"], sp["c"], sp["lam"], sp["lam"], sp["vec"]],
        out_specs=[sp["rows"], per((CH, W)), per((CH, W)), per((W, CH)), per((W, CH)), per((SEGMENTS, W)),
                   per((SEGMENTS, W)),
                   per((1, CH))],
        scratch_shapes=[pltpu.VMEM((S, W), F32)] * 4,
        compiler_params=_params(("parallel", "parallel")),
    )(u, dy, bre, bim, cre, cim, lr8, li8, dsk)


def _ssm_discretize(a_re, a_im, log_dt, b_re, b_im):
    dt = jnp.exp(log_dt)[:, None]
    er = jnp.exp(a_re * dt)
    lr = er * jnp.cos(a_im * dt)
    li = er * jnp.sin(a_im * dt)
    den = a_re * a_re + a_im * a_im
    fr = ((lr - 1.0) * a_re + li * a_im) / den
    fi = (li * a_re - (lr - 1.0) * a_im) / den
    bbr = fr[..., None] * b_re - fi[..., None] * b_im
    bbi = fr[..., None] * b_im + fi[..., None] * b_re
    return lr, li, bbr, bbi


def _block_diag_in(m):
    G, P, H = m.shape
    J = G // GROUPS_PER_BLOCK
    m = m.reshape(J, GROUPS_PER_BLOCK, P, H).transpose(0, 1, 3, 2)
    eye = jnp.eye(GROUPS_PER_BLOCK, dtype=m.dtype)
    out = m[:, :, :, None, :] * eye[None, :, None, :, None]
    return out.reshape(J, GROUPS_PER_BLOCK * H, GROUPS_PER_BLOCK * P)


def _block_diag_in_grad(d, G, P, H):
    J = G // GROUPS_PER_BLOCK
    d = d.reshape(J, GROUPS_PER_BLOCK, H, GROUPS_PER_BLOCK, P)
    idx = jnp.arange(GROUPS_PER_BLOCK)
    d = d[:, idx, :, idx, :]
    return d.transpose(1, 0, 3, 2).reshape(G, P, H)


def _block_diag_out(m):
    G, H, P = m.shape
    J = G // GROUPS_PER_BLOCK
    m = m.reshape(J, GROUPS_PER_BLOCK, H, P).transpose(0, 1, 3, 2)
    eye = jnp.eye(GROUPS_PER_BLOCK, dtype=m.dtype)
    out = m[:, :, :, None, :] * eye[None, :, None, :, None]
    return out.reshape(J, GROUPS_PER_BLOCK * P, GROUPS_PER_BLOCK * H)


def _block_diag_out_grad(d, G, H, P):
    J = G // GROUPS_PER_BLOCK
    d = d.reshape(J, GROUPS_PER_BLOCK, P, GROUPS_PER_BLOCK, H)
    idx = jnp.arange(GROUPS_PER_BLOCK)
    d = d[:, idx, :, idx, :]
    return d.transpose(1, 0, 3, 2).reshape(G, H, P)


def _interleave(a, B, S):
    L = S // SEGMENTS
    return a.reshape(B, SEGMENTS, L, a.shape[-1]).transpose(0, 2, 1, 3).reshape(B * S, a.shape[-1])


def _deinterleave(a, B, S):
    L = S // SEGMENTS
    return a.reshape(B, L, SEGMENTS, a.shape[-1]).transpose(0, 2, 1, 3).reshape(B * S, a.shape[-1])


def _adamw_math(w, g, m, v):
    m = ADAM_B1 * m + (1.0 - ADAM_B1) * g
    v = ADAM_B2 * v + (1.0 - ADAM_B2) * (g * g)
    m_hat = m / (1.0 - ADAM_B1 ** ADAM_STEP)
    v_hat = v / (1.0 - ADAM_B2 ** ADAM_STEP)
    delta = -ADAM_LR * (m_hat / (jnp.sqrt(v_hat) + ADAM_EPS) + ADAM_WD * w)
    return delta, m, v


def adamw(w, g, m, v, *, name):
    R, C = w.shape
    tr = _tile(R, (max(8, (1 << 18) // C // 8 * 8), 256, 128, 64, 32, 16, 8))

    def body(w_ref, g_ref, m_ref, v_ref, d_ref, nm_ref, nv_ref):
        d, nm, nv = _adamw_math(w_ref[...], g_ref[...], m_ref[...], v_ref[...])
        d_ref[...] = d
        nm_ref[...] = nm
        nv_ref[...] = nv

    spec = pl.BlockSpec((tr, C), lambda i: (i, 0))
    shp = jax.ShapeDtypeStruct((R, C), F32)
    return pl.pallas_call(
        body, name=name, out_shape=[shp, shp, shp], grid=(R // tr,), in_specs=[spec] * 4, out_specs=[spec] * 3,
        compiler_params=_params(("parallel",)),
    )(w, g, m, v)


def sum_leading(a, *, name, out_dtype=F32):
    n, R, C = a.shape
    tr = _tile(R, (256, 128, 64, 32, 16, 8))

    def body(a_ref, o_ref):
        acc = a_ref[0].astype(F32)
        for i in range(1, n):
            acc = acc + a_ref[i].astype(F32)
        o_ref[...] = acc.astype(o_ref.dtype)

    return pl.pallas_call(
        body, name=name, out_shape=jax.ShapeDtypeStruct((R, C), out_dtype), grid=(R // tr,),
        in_specs=[pl.BlockSpec((n, tr, C), lambda i: (0, i, 0))], out_specs=pl.BlockSpec((tr, C), lambda i: (i, 0)),
        compiler_params=_params(("parallel",)),
    )(a)


def _any_specs(n):
    return [pl.BlockSpec(memory_space=pl.ANY) for _ in range(n)]


def _coords():
    return lax.axis_index("x"), lax.axis_index("y"), lax.axis_index("c")


def _flip(v, bit):
    return (v + bit) % 2


def all_gather8(a, *, name):
    shape = a.shape

    def body(a_ref, o_ref, send_sems, recv_sems, local_sem):
        x, y, c = _coords()
        me = 4 * x + 2 * y + c
        mine = pltpu.make_async_copy(a_ref, o_ref.at[me], local_sem)
        mine.start()
        sends = []
        for k in range(1, N_DEV):
            peer = (_flip(x, (k >> 2) & 1), _flip(y, (k >> 1) & 1), _flip(c, k & 1))
            cp = pltpu.make_async_remote_copy(a_ref, o_ref.at[me], send_sems.at[k - 1], recv_sems.at[k - 1],
                                              device_id=peer, device_id_type=MESH)
            cp.start()
            sends.append(cp)
        for k in range(1, N_DEV):
            px, py, pc = _flip(x, (k >> 2) & 1), _flip(y, (k >> 1) & 1), _flip(c, k & 1)
            src = 4 * px + 2 * py + pc
            pltpu.make_async_remote_copy(a_ref, o_ref.at[src], send_sems.at[k - 1], recv_sems.at[k - 1],
                                         device_id=(px, py, pc), device_id_type=MESH).wait_recv()
        for cp in sends:
            cp.wait_send()
        mine.wait()

    return pl.pallas_call(
        body, name=name, out_shape=jax.ShapeDtypeStruct((N_DEV,) + shape, a.dtype),
        in_specs=_any_specs(1), out_specs=pl.BlockSpec(memory_space=pl.ANY),
        scratch_shapes=[pltpu.SemaphoreType.DMA((N_DEV - 1,)), pltpu.SemaphoreType.DMA((N_DEV - 1,)),
                        pltpu.SemaphoreType.DMA(())],
    )(a)


def _chip_of(x, y, p):
    px, py = _flip(x, (p >> 1) & 1), _flip(y, p & 1)
    return 2 * px + py, px, py


class _ShardGather:
    def __init__(self, ins, outs, ici_send, ici_recv, d2d_send, d2d_recv):
        self.ins, self.outs = ins, outs
        self.sems = ici_send, ici_recv, d2d_send, d2d_recv
        self.x, self.y, self.c = _coords()
        self.me = 2 * self.x + self.y

    def _ici(self, i, p, slot):
        half = self.ins[i].shape[0] // 2
        rows = pl.ds(self.c * half, half)
        _, px, py = _chip_of(self.x, self.y, p)
        s = i * 3 + p - 1
        return pltpu.make_async_remote_copy(self.ins[i].at[rows], self.outs[i].at[slot, rows], self.sems[0].at[s],
                                            self.sems[1].at[s], device_id=(px, py, self.c), device_id_type=MESH)

    def _d2d(self, i, p, mine):
        half = self.ins[i].shape[0] // 2
        rows = pl.ds((self.c if mine else 1 - self.c) * half, half)
        src, _, _ = _chip_of(self.x, self.y, p)
        s = i * 3 + p - 1
        part = self.outs[i].at[src, rows]
        return pltpu.make_async_remote_copy(part, part, self.sems[2].at[s], self.sems[3].at[s],
                                            device_id=(self.x, self.y, 1 - self.c), device_id_type=MESH)

    def _each(self):
        return [(i, p) for i in range(len(self.ins)) for p in range(1, N_CHIPS)]

    def send(self):
        for i, p in self._each():
            self._ici(i, p, self.me).start()

    def forward(self):
        for i, p in self._each():
            self._ici(i, p, _chip_of(self.x, self.y, p)[0]).wait_recv()
            self._d2d(i, p, True).start()

    def finish(self):
        for i, p in self._each():
            self._d2d(i, p, False).wait_recv()
        for i, p in self._each():
            self._ici(i, p, self.me).wait_send()
            self._d2d(i, p, True).wait_send()


def gather_chip_shards(arrs, remote, *, name):
    n = len(arrs)
    far = [i for i in range(n) if remote[i]]

    def body(*refs):
        ins, outs = refs[:n], refs[n:2 * n]
        ici_send, ici_recv, d2d_send, d2d_recv, local_sems = refs[2 * n:2 * n + 5]
        bufs = refs[2 * n + 5:]
        me = 2 * lax.axis_index("x") + lax.axis_index("y")
        loads = []
        for i in range(n):
            cp = pltpu.make_async_copy(ins[i], bufs[i], local_sems.at[i])
            cp.start()
            loads.append(cp)
        gather = _ShardGather([ins[i] for i in far], [outs[i] for i in far], ici_send, ici_recv, d2d_send, d2d_recv)
        gather.send()
        stores = []
        for i in range(n):
            loads[i].wait()
            cp = pltpu.make_async_copy(bufs[i], outs[i].at[me], local_sems.at[i])
            cp.start()
            stores.append(cp)
        gather.forward()
        gather.finish()
        for cp in stores:
            cp.wait()

    dma = pltpu.SemaphoreType.DMA
    m = 3 * len(far)
    return pl.pallas_call(
        body, name=name,
        out_shape=[jax.ShapeDtypeStruct((N_CHIPS,) + a.shape, a.dtype) for a in arrs],
        in_specs=_any_specs(n), out_specs=_any_specs(n),
        scratch_shapes=[dma((m,)), dma((m,)), dma((m,)), dma((m,)), dma((n,))]
        + [pltpu.VMEM(a.shape, a.dtype) for a in arrs],
        compiler_params=pltpu.CompilerParams(vmem_limit_bytes=V7X_VMEM_LIMIT),
    )(*arrs)


def swap_halves(arrs, *, name):
    n = len(arrs)

    def body(*refs):
        ins, outs = refs[:n], refs[n:2 * n]
        send_sems, recv_sems = refs[2 * n:]
        x, y, c = _coords()
        cps = []
        for i in range(n):
            half = ins[i].shape[1] // 2
            cp = pltpu.make_async_remote_copy(ins[i].at[:, pl.ds((1 - c) * half, half)], outs[i], send_sems.at[i],
                                              recv_sems.at[i], device_id=(x, y, 1 - c), device_id_type=MESH)
            cp.start()
            cps.append(cp)
        for cp in cps:
            cp.wait()

    dma = pltpu.SemaphoreType.DMA
    return pl.pallas_call(
        body, name=name,
        out_shape=[jax.ShapeDtypeStruct((N_CHIPS, a.shape[1] // 2, a.shape[2]), a.dtype) for a in arrs],
        in_specs=_any_specs(n), out_specs=_any_specs(n), scratch_shapes=[dma((n,)), dma((n,))],
    )(*arrs)


def add_half(g, other, c_idx, *, name, out_dtype):
    _, R, C = g.shape
    half = R // 2
    tr = _tile(half, (256, 128, 64, 32, 16, 8))
    nt = half // tr

    def body(c_ref, g_ref, o_ref, out_ref):
        out_ref[...] = (g_ref[...] + o_ref[...]).astype(out_ref.dtype)

    return pl.pallas_call(
        body, name=name, out_shape=jax.ShapeDtypeStruct((N_CHIPS, half, C), out_dtype),
        grid_spec=pltpu.PrefetchScalarGridSpec(
            num_scalar_prefetch=1, grid=(N_CHIPS, nt),
            in_specs=[pl.BlockSpec((None, tr, C), lambda r, t, c_ref: (r, c_ref[0] * nt + t, 0)),
                      pl.BlockSpec((None, tr, C), lambda r, t, c_ref: (r, t, 0))],
            out_specs=pl.BlockSpec((None, tr, C), lambda r, t, c_ref: (r, t, 0))),
        compiler_params=_params(("parallel", "parallel")),
    )(c_idx, g, other)


class _ChipScatter:
    def __init__(self, ins, outs, send_sems, recv_sems):
        self.ins, self.outs, self.send_sems, self.recv_sems = ins, outs, send_sems, recv_sems
        self.x, self.y, self.c = _coords()

    def _copies(self):
        for i in range(len(self.ins)):
            for p in range(1, N_CHIPS):
                dst, px, py = _chip_of(self.x, self.y, p)
                s = i * 3 + p - 1
                yield pltpu.make_async_remote_copy(self.ins[i].at[dst], self.outs[i].at[p - 1], self.send_sems.at[s],
                                                   self.recv_sems.at[s], device_id=(px, py, self.c), device_id_type=MESH)

    def send(self):
        for cp in self._copies():
            cp.start()

    def finish(self):
        for cp in self._copies():
            cp.wait()


def scatter_to_chips(arrs, *, name):
    n = len(arrs)

    def body(*refs):
        scatter = _ChipScatter(refs[:n], refs[n:2 * n], *refs[2 * n:])
        scatter.send()
        scatter.finish()

    dma = pltpu.SemaphoreType.DMA
    return pl.pallas_call(
        body, name=name,
        out_shape=[jax.ShapeDtypeStruct((N_CHIPS - 1,) + a.shape[1:], a.dtype) for a in arrs],
        in_specs=_any_specs(n), out_specs=_any_specs(n), scratch_shapes=[dma((3 * n,)), dma((3 * n,))],
    )(*arrs)


def add_chips(h, got, r_idx, *, name):
    _, R, C = h.shape
    tr = _tile(R, (256, 128, 64, 32, 16, 8))

    def body(r_ref, h_ref, g_ref, out_ref):
        acc = h_ref[...].astype(F32)
        for p in range(N_CHIPS - 1):
            acc = acc + g_ref[p].astype(F32)
        out_ref[...] = acc

    return pl.pallas_call(
        body, name=name, out_shape=jax.ShapeDtypeStruct((R, C), F32),
        grid_spec=pltpu.PrefetchScalarGridSpec(
            num_scalar_prefetch=1, grid=(R // tr,),
            in_specs=[pl.BlockSpec((None, tr, C), lambda t, r_ref: (r_ref[0], t, 0)),
                      pl.BlockSpec((N_CHIPS - 1, tr, C), lambda t, r_ref: (0, t, 0))],
            out_specs=pl.BlockSpec((tr, C), lambda t, r_ref: (t, 0))),
        compiler_params=_params(("parallel",)),
    )(r_idx, h, got)


def join_halves(arrs, *, name):
    n = len(arrs)

    def body(*refs):
        ins, outs = refs[:n], refs[n:2 * n]
        send_sems, recv_sems, local_sems = refs[2 * n:2 * n + 3]
        bufs = refs[2 * n + 3:]
        x, y, c = _coords()
        loads, sends, stores = [], [], []
        for i in range(n):
            cp = pltpu.make_async_copy(ins[i], bufs[i], local_sems.at[i])
            cp.start()
            loads.append(cp)
        for i in range(n):
            half = ins[i].shape[0]
            cp = pltpu.make_async_remote_copy(ins[i], outs[i].at[pl.ds(c * half, half)], send_sems.at[i], recv_sems.at[i],
                                              device_id=(x, y, 1 - c), device_id_type=MESH)
            cp.start()
            sends.append(cp)
        for i in range(n):
            half = ins[i].shape[0]
            loads[i].wait()
            cp = pltpu.make_async_copy(bufs[i], outs[i].at[pl.ds(c * half, half)], local_sems.at[i])
            cp.start()
            stores.append(cp)
        for i in range(n):
            half = ins[i].shape[0]
            pltpu.make_async_remote_copy(ins[i], outs[i].at[pl.ds((1 - c) * half, half)], send_sems.at[i],
                                         recv_sems.at[i], device_id=(x, y, 1 - c), device_id_type=MESH).wait_recv()
        for i in range(n):
            sends[i].wait_send()
            stores[i].wait()

    dma = pltpu.SemaphoreType.DMA
    return pl.pallas_call(
        body, name=name,
        out_shape=[jax.ShapeDtypeStruct((2 * a.shape[0], a.shape[1]), a.dtype) for a in arrs],
        in_specs=_any_specs(n), out_specs=_any_specs(n),
        scratch_shapes=[dma((n,)), dma((n,)), dma((n,))] + [pltpu.VMEM(a.shape, a.dtype) for a in arrs],
        compiler_params=pltpu.CompilerParams(vmem_limit_bytes=V7X_VMEM_LIMIT),
    )(*arrs)


def pair_sums(grads, wire_dtypes, tag):
    c_idx = jnp.reshape(lax.axis_index("c"), (1,)).astype(jnp.int32)
    theirs = swap_halves(grads, name=f"rs_swap_halves_{tag}")
    return [add_half(g, o, c_idx, name=f"rs_add_half_{tag}{i}", out_dtype=wire_dtypes[i])
            for i, (g, o) in enumerate(zip(grads, theirs))]


def chip_sums(pairs, gots, tag):
    r_idx = jnp.reshape(2 * lax.axis_index("x") + lax.axis_index("y"), (1,)).astype(jnp.int32)
    return [add_chips(h, g, r_idx, name=f"rs_add_chips_{tag}{i}") for i, (h, g) in enumerate(zip(pairs, gots))]


def _to_heads(t, B, S):
    return t.reshape(B, S, -1, HEAD_DIM).transpose(0, 2, 1, 3)


def _from_heads(t, B, S):
    return t.transpose(0, 2, 1, 3).reshape(B * S, -1)


def _chip_major(w, axis):
    n = w.shape[axis] // N_CHIPS
    parts = w.reshape(w.shape[:axis] + (N_CHIPS, n) + w.shape[axis + 1:])
    return jnp.moveaxis(parts, axis, 0)


def _from_chip_major(g, axis):
    g = jnp.moveaxis(g, 0, axis)
    return g.reshape(g.shape[:axis] + (g.shape[axis] * g.shape[axis + 1],) + g.shape[axis + 2:])


def kernel(x, c, norm_mix, norm_ffn, w_mod, b_mod, w_qkv, w_o_attn, w_in_ssm, a_re, a_im, log_dt, b_re, b_im, c_re, c_im, d_skip, w_glu, b_glu, w_o_ssm, w_up, conv_w, conv_b, w_down, norm_out, w_fin, b_fin, loss_target, m_norm_mix, m_norm_ffn, m_w_mod, m_b_mod, m_w_qkv, m_w_o_attn, m_w_in_ssm, m_a_re, m_a_im, m_log_dt, m_b_re, m_b_im, m_c_re, m_c_im, m_d_skip, m_w_glu, m_b_glu, m_w_o_ssm, m_w_up, m_conv_w, m_conv_b, m_w_down, m_norm_out, m_w_fin, m_b_fin, v_norm_mix, v_norm_ffn, v_w_mod, v_b_mod, v_w_qkv, v_w_o_attn, v_w_in_ssm, v_a_re, v_a_im, v_log_dt, v_b_re, v_b_im, v_c_re, v_c_im, v_d_skip, v_w_glu, v_b_glu, v_w_o_ssm, v_w_up, v_conv_w, v_conv_b, v_w_down, v_norm_out, v_w_fin, v_b_fin):
    B, S, D = x.shape
    T = B * S
    F2 = conv_b.shape[1]
    F = F2 // 2
    G, P = a_re.shape[1], a_re.shape[2]
    H = b_re.shape[3]
    mx, my, mc = _coords()
    chip = 2 * mx + my
    dev = 4 * mx + 2 * my + mc
    BG = N_DEV * B
    mod_w = w_mod.shape[2]
    fin_w = w_fin.shape[1]

    c_all = all_gather8(c, name="gather_c").reshape(BG, D)
    c_act = silu_rows(c_all, name="silu_c")
    b_mod_mine = lax.dynamic_slice(b_mod, (0, chip * mod_w), (2, mod_w))
    b_fin_mine = lax.dynamic_slice(b_fin, (chip * fin_w,), (fin_w,))
    cond = [matmul(c_act, w_mod[i], bias=b_mod_mine[i], name=f"mod_proj_{i}") for i in range(2)]
    cond.append(matmul(c_act, w_fin, bias=b_fin_mine, name="fin_proj"))
    cond_all = all_gather8(jnp.concatenate(cond, axis=1), name="gather_cond")
    cond_all = cond_all[::2]
    cond_rows = lax.dynamic_slice(cond_all, (0, dev * B, 0), (N_CHIPS, B, cond_all.shape[2]))
    mods = []
    for i in range(2):
        full = cond_rows[:, :, i * mod_w:(i + 1) * mod_w].transpose(1, 0, 2).reshape(B, N_CHIPS * mod_w)
        mods.append([full[:, k * D:(k + 1) * D] for k in range(6)])
    fin = cond_rows[:, :, 2 * mod_w:].transpose(1, 0, 2).reshape(B, N_CHIPS * fin_w)
    sh_f, sc_f = fin[:, :D], fin[:, D:]

    rows1024 = jnp.concatenate([w_o_attn[0], w_in_ssm[0], w_glu[0], w_o_ssm[0], w_down.reshape(-1, D)], axis=0)
    shards = [w_qkv[0].astype(BF16), rows1024.astype(BF16), w_up[0].astype(BF16), w_up[1].astype(BF16)]
    W_qkv, *own_slots = gather_chip_shards(shards, [True, False, False, False], name="gather_weights")
    Dq = D // N_CHIPS
    Fq = F // N_CHIPS
    small =jnp.concatenate([conv_w.reshape(6, -1), jnp.pad(d_skip, ((0, 0), (0, conv_w.shape[2] - Dq))),
                             jnp.pad(b_glu, ((0, 0), (0, conv_w.shape[2] - Dq)))], axis=0)
    small_all = all_gather8(small, name="gather_small")[::2]
    conv_w_full = _from_chip_major(small_all[:, :6].reshape(N_CHIPS, 2, 3, -1), 2)
    d_skip_full = small_all[:, 6, :Dq].reshape(1, D)
    b_glu_full = small_all[:, 7, :Dq].reshape(D)

    x0 = x.reshape(T, D)
    tgt = loss_target.reshape(T, D)

    def ffn_fwd(xin, i):
        sh2, sc2, g2 = mods[i][3], mods[i][4], mods[i][5]
        h2 = norm_mod_fwd(xin, norm_ffn[i], sh2, sc2, B=B, S=S, name=f"ffn_norm_{i}")
        up = matmul(h2, W_up[i], b_chips=True, name=f"ffn_up_{i}")
        act = conv_gate_fwd(up, conv_w_full[i], conv_b[i:i + 1], B=B, S=S, name=f"ffn_conv_{i}")
        yf = matmul(act, W_down[i], name=f"ffn_down_{i}")
        xout = gate_res_fwd(xin, yf, g2, B=B, S=S, name=f"ffn_res_{i}")
        return xout, (xin, h2, up, act, yf)

    sh1, sc1, g1 = mods[0][0], mods[0][1], mods[0][2]
    h1a = norm_mod_fwd(x0, norm_mix[0], sh1, sc1, B=B, S=S, name="att_norm")
    qkv = matmul(h1a, W_qkv, out_dtype=BF16, b_chips=True, name="att_qkv")
    q, k, v = [_to_heads(qkv[:, i * D:(i + 1) * D], B, S) for i in range(3)]
    o, ltot, g_rows, W_up0, W_up1 = attn_fwd_wide(q, k, v, shards[1:], own_slots, name="att_fwd")
    W_up = [W_up0, W_up1]
    W_o_attn = g_rows[:, 0 * Dq:1 * Dq].reshape(D, D)
    W_in = g_rows[:, 1 * Dq:2 * Dq].reshape(D, D)
    W_glu = g_rows[:, 2 * Dq:3 * Dq].reshape(D, D)
    W_o_ssm = g_rows[:, 3 * Dq:4 * Dq].reshape(D, D)
    W_down = [g_rows[:, 4 * Dq + i * Fq:4 * Dq + (i + 1) * Fq].reshape(F, D) for i in range(2)]
    o2 = _from_heads(o, B, S).astype(BF16)
    ya = matmul(o2, W_o_attn, name="att_out")
    x1 = gate_res_fwd(x0, ya, g1, B=B, S=S, name="att_res")
    x2, ffn0 = ffn_fwd(x1, 0)

    lr, li, bbr, bbi = _ssm_discretize(a_re[0], a_im[0], log_dt[0], b_re[0], b_im[0])
    J = G // GROUPS_PER_BLOCK
    Wst = GROUPS_PER_BLOCK * P
    bre_blk = _block_diag_in(bbr).astype(BF16)
    bim_blk = _block_diag_in(bbi).astype(BF16)
    cre_blk = _block_diag_out(c_re[0]).astype(BF16)
    cim_blk = _block_diag_out(c_im[0]).astype(BF16)
    lr8 = jnp.broadcast_to(lr.reshape(J, 1, Wst), (J, SEGMENTS, Wst))
    li8 = jnp.broadcast_to(li.reshape(J, 1, Wst), (J, SEGMENTS, Wst))
    sh1s, sc1s, g1s = mods[1][0], mods[1][1], mods[1][2]
    h1s = norm_mod_fwd(x2, norm_mix[1], sh1s, sc1s, B=B, S=S, name="ssm_norm")
    h1p = _interleave(h1s, B, S)
    u = matmul(h1p, W_in, name="ssm_in")
    y_ssm = ssm_fwd(u, bre_blk, bim_blk, cre_blk, cim_blk, lr8, li8, d_skip_full, B=B, S=S, name="ssm_scan_fwd")
    zb = gelu_fwd(y_ssm, B=B, S=S, name="ssm_gelu")
    s_glu = matmul(zb, W_glu, bias=b_glu_full, name="ssm_glu_proj")
    gb = glu_fwd(y_ssm, s_glu, B=B, S=S, name="ssm_glu")
    ys_p = matmul(gb, W_o_ssm, name="ssm_out")
    ys = _deinterleave(ys_p, B, S)
    x3 = gate_res_fwd(x2, ys, g1s, B=B, S=S, name="ssm_res")
    x4, ffn1 = ffn_fwd(x3, 1)

    dx4, loss_p, dsh_f, dsc_f, dnorm_out = final_loss(x4, tgt, norm_out, sh_f, sc_f, B=B, S=S, name="loss_head")
    loss = lax.psum(jnp.sum(loss_p), ("x", "y", "c"))

    def ffn_bwd(dxo, i, saved):
        xin, h2, up, act, yf = saved
        sc2, g2 = mods[i][4], mods[i][5]
        dyf, dg2 = gate_res_bwd(dxo, yf, g2, B=B, S=S, name=f"ffn_res_bwd_{i}")
        dact = matmul(dyf, W_down[i], tb=True, name=f"ffn_down_dx_{i}")
        dW_down = matmul(act, dyf, ta=True, name=f"ffn_down_dw_{i}")
        dup, dcw, dcb = conv_gate_bwd(up, dact, conv_w_full[i], conv_b[i:i + 1], B=B, S=S, name=f"ffn_conv_bwd_{i}")
        dh2 = matmul(dup, W_up[i], tb=True, b_chips=True, name=f"ffn_up_dx_{i}")
        dW_up = matmul(h2, dup, ta=True, b_chips=True, out_chips=True, name=f"ffn_up_dw_{i}")
        dxin, dsh2, dsc2, dnf = norm_mod_bwd(dh2, xin, dxo, norm_ffn[i], sc2, B=B, S=S, name=f"ffn_norm_bwd_{i}")
        dconv_w = jnp.sum(dcw, axis=0).transpose(1, 0, 2).reshape(3, F2)
        return dxin, dict(dW_down=dW_down, dW_up=dW_up, dconv_b=jnp.sum(dcb, axis=0).reshape(F2),
                          dconv_w=dconv_w, dnorm_ffn=jnp.sum(dnf, axis=0), dsh2=dsh2, dsc2=dsc2, dg2=dg2)

    dx3, gf1 = ffn_bwd(dx4, 1, ffn1)

    dys_p, dg1s = gate_res_bwd(_interleave(dx3, B, S), ys_p, g1s, B=B, S=S, name="ssm_res_bwd")
    dgb = matmul(dys_p, W_o_ssm, tb=True, name="ssm_out_dx")
    dW_o_ssm = matmul(gb, dys_p, ta=True, name="ssm_out_dw")
    ds_glu, dz1, db_glu = glu_bwd1(y_ssm, s_glu, dgb, B=B, S=S, name="ssm_glu_bwd1")
    dz2 = matmul(ds_glu, W_glu, tb=True, name="ssm_glu_dx")
    dW_glu = matmul(zb, ds_glu, ta=True, name="ssm_glu_dw")
    dy_ssm = glu_bwd2(y_ssm, dz1, dz2, B=B, S=S, name="ssm_glu_bwd2")
    du, dbre, dbim, dcre, dcim, dlr8, dli8, ddsk = ssm_bwd(u, dy_ssm, bre_blk, bim_blk, cre_blk, cim_blk, lr8, li8,
                                                           d_skip_full, B=B, S=S, name="ssm_scan_bwd")
    dub = du.astype(BF16)
    dh1p = matmul(dub, W_in, tb=True, name="ssm_in_dx")
    dW_in = matmul(h1p, dub, ta=True, name="ssm_in_dw")
    dx2, dsh1s, dsc1s, dnm1 = norm_mod_bwd(_deinterleave(dh1p, B, S), x2, dx3, norm_mix[1], sc1s, B=B, S=S,
                                           name="ssm_norm_bwd")
    dlr = jnp.sum(dlr8, axis=(0, 2)).reshape(G, P)
    dli = jnp.sum(dli8, axis=(0, 2)).reshape(G, P)
    dbbr = _block_diag_in_grad(jnp.sum(dbre, axis=0), G, P, H)
    dbbi = _block_diag_in_grad(jnp.sum(dbim, axis=0), G, P, H)
    dc_re = _block_diag_out_grad(jnp.sum(dcre, axis=0), G, H, P)
    dc_im = _block_diag_out_grad(jnp.sum(dcim, axis=0), G, H, P)
    dd_skip = jnp.sum(ddsk, axis=0).reshape(D)

    dx1, gf0 = ffn_bwd(dx2, 0, ffn0)

    dya, dg1 = gate_res_bwd(dx1, ya, g1, B=B, S=S, name="att_res_bwd")
    do2 = matmul(dya, W_o_attn, tb=True, out_dtype=BF16, name="att_out_dx")
    dW_o_attn = matmul(o2, dya, ta=True, name="att_out_dw")
    g_rows_cm = jnp.concatenate([dW_o_attn.reshape(N_CHIPS, Dq, D), dW_in.reshape(N_CHIPS, Dq, D),
                                 dW_glu.reshape(N_CHIPS, Dq, D), dW_o_ssm.reshape(N_CHIPS, Dq, D),
                                 gf0["dW_down"].reshape(N_CHIPS, Fq, D), gf1["dW_down"].reshape(N_CHIPS, Fq, D)], axis=1)
    pairs_a = pair_sums([g_rows_cm, gf0["dW_up"], gf1["dW_up"]], [BF16, BF16, BF16], "a")
    dq, dk, dv, *gots_a = attn_bwd_wide(q, k, v, ltot, _to_heads(do2, B, S), pairs_a, name="att_bwd")
    dqkv = jnp.concatenate([_from_heads(t, B, S) for t in (dq, dk, dv)], axis=1)
    dh1a = matmul(dqkv, W_qkv, tb=True, b_chips=True, name="att_qkv_dx")
    dW_qkv = matmul(h1a, dqkv, ta=True, out_chips=True, name="att_qkv_dw")
    grad_x, dsh1, dsc1, dnm0 = norm_mod_bwd(dh1a, x0, dx1, norm_mix[0], sc1, B=B, S=S, name="att_norm_bwd")

    dmod_rows = jnp.concatenate([dsh1, dsc1, dg1, gf0["dsh2"], gf0["dsc2"], gf0["dg2"],
                                 dsh1s, dsc1s, dg1s, gf1["dsh2"], gf1["dsc2"], gf1["dg2"], dsh_f, dsc_f], axis=1)
    dmod_all = all_gather8(dmod_rows, name="gather_dmod").reshape(BG, 14 * D)
    grad_w_mod = jnp.stack([
        matmul(c_act, lax.dynamic_slice(dmod_all, (0, i * 6 * D + chip * mod_w), (BG, mod_w)), ta=True,
               name=f"mod_dw_{i}") for i in range(2)])
    grad_w_fin = matmul(c_act, lax.dynamic_slice(dmod_all, (0, 12 * D + chip * fin_w), (BG, fin_w)), ta=True,
                        name="fin_dw")

    parts = [jnp.concatenate([jnp.sum(dnm0, axis=0), jnp.sum(dnm1, axis=0)]),
             jnp.concatenate([gf0["dnorm_ffn"], gf1["dnorm_ffn"]]),
             jnp.sum(dmod_rows[:, :12 * D], axis=0),
             dlr.reshape(-1), dli.reshape(-1), dbbr.reshape(-1), dbbi.reshape(-1), dc_re.reshape(-1), dc_im.reshape(-1),
             dd_skip, jnp.sum(db_glu, axis=0),
             gf0["dconv_w"].reshape(-1), gf1["dconv_w"].reshape(-1), gf0["dconv_b"], gf1["dconv_b"],
             jnp.sum(dnorm_out, axis=0), jnp.sum(dmod_rows[:, 12 * D:], axis=0)]
    sizes = [int(p.shape[0]) for p in parts]
    flat = jnp.concatenate(parts)
    width = 1024
    quantum = N_CHIPS * 16 * width
    padded = -(-flat.shape[0] // quantum) * quantum
    small_cm = jnp.pad(flat, (0, padded - flat.shape[0])).reshape(N_CHIPS, -1, width)

    pairs_b = pair_sums([dW_qkv, small_cm], [BF16, F32], "b")
    gots_b = scatter_to_chips(pairs_b, name="rs_scatter_to_chips")
    r_qkv, r_small, r_rows, r_up0, r_up1 = join_halves(
        chip_sums(pairs_b, gots_b, "b") + chip_sums(pairs_a, gots_a, "a"), name="rs_join_halves")
    grad_w_qkv = r_qkv[None]
    grad_w_o_attn = r_rows[0 * Dq:1 * Dq][None]
    grad_w_in_ssm = r_rows[1 * Dq:2 * Dq][None]
    grad_w_glu = r_rows[2 * Dq:3 * Dq][None]
    grad_w_o_ssm = r_rows[3 * Dq:4 * Dq][None]
    grad_w_down = r_rows[4 * Dq:].reshape(2, Fq, D)
    grad_w_up = jnp.stack([r_up0, r_up1])
    summed = all_gather8(r_small, name="gather_small_grads")[::2].reshape(-1)
    offs = [0]
    for s_ in sizes:
        offs.append(offs[-1] + s_)
    (s_nm, s_nf, s_bmod, s_lr, s_li, s_bbr, s_bbi, s_cre, s_cim, s_dsk, s_bglu, s_cw0, s_cw1, s_cb0, s_cb1, s_no,
     s_bfin) = [summed[offs[i]:offs[i + 1]] for i in range(len(sizes))]
    _, disc_vjp = jax.vjp(_ssm_discretize, a_re[0], a_im[0], log_dt[0], b_re[0], b_im[0])
    ga_re, ga_im, glog_dt, gb_re, gb_im = disc_vjp((s_lr.reshape(G, P), s_li.reshape(G, P), s_bbr.reshape(G, P, H),
                                                    s_bbi.reshape(G, P, H)))
    grad_norm_mix = s_nm.reshape(2, D)
    grad_norm_ffn = s_nf.reshape(2, D)
    grad_b_mod = s_bmod.reshape(2, 6 * D)
    grad_c_re = s_cre.reshape(1, G, H, P)
    grad_c_im = s_cim.reshape(1, G, H, P)
    grad_d_skip = lax.dynamic_slice(s_dsk, (chip * Dq,), (Dq,)).reshape(1, Dq)
    grad_b_glu = lax.dynamic_slice(s_bglu, (chip * Dq,), (Dq,)).reshape(1, Dq)
    cw_full = jnp.stack([s_cw0.reshape(3, F2), s_cw1.reshape(3, F2)])
    grad_conv_w = lax.dynamic_slice(cw_full, (0, 0, chip * (F2 // N_CHIPS)), (2, 3, F2 // N_CHIPS))
    grad_conv_b = jnp.stack([s_cb0, s_cb1])
    grad_norm_out = s_no
    grad_b_fin = s_bfin

    grads = dict(
        norm_mix=grad_norm_mix, norm_ffn=grad_norm_ffn, w_mod=grad_w_mod, b_mod=grad_b_mod, w_qkv=grad_w_qkv,
        w_o_attn=grad_w_o_attn, w_in_ssm=grad_w_in_ssm, a_re=ga_re[None], a_im=ga_im[None], log_dt=glog_dt[None],
        b_re=gb_re[None], b_im=gb_im[None], c_re=grad_c_re, c_im=grad_c_im, d_skip=grad_d_skip, w_glu=grad_w_glu,
        b_glu=grad_b_glu, w_o_ssm=grad_w_o_ssm, w_up=grad_w_up, conv_w=grad_conv_w, conv_b=grad_conv_b,
        w_down=grad_w_down, norm_out=grad_norm_out, w_fin=grad_w_fin, b_fin=grad_b_fin)
    weights = dict(
        norm_mix=norm_mix, norm_ffn=norm_ffn, w_mod=w_mod, b_mod=b_mod, w_qkv=w_qkv, w_o_attn=w_o_attn,
        w_in_ssm=w_in_ssm, a_re=a_re, a_im=a_im, log_dt=log_dt, b_re=b_re, b_im=b_im, c_re=c_re, c_im=c_im,
        d_skip=d_skip, w_glu=w_glu, b_glu=b_glu, w_o_ssm=w_o_ssm, w_up=w_up, conv_w=conv_w, conv_b=conv_b,
        w_down=w_down, norm_out=norm_out, w_fin=w_fin, b_fin=b_fin)
    m_in = dict(
        norm_mix=m_norm_mix, norm_ffn=m_norm_ffn, w_mod=m_w_mod, b_mod=m_b_mod, w_qkv=m_w_qkv, w_o_attn=m_w_o_attn,
        w_in_ssm=m_w_in_ssm, a_re=m_a_re, a_im=m_a_im, log_dt=m_log_dt, b_re=m_b_re, b_im=m_b_im, c_re=m_c_re,
        c_im=m_c_im, d_skip=m_d_skip, w_glu=m_w_glu, b_glu=m_b_glu, w_o_ssm=m_w_o_ssm, w_up=m_w_up, conv_w=m_conv_w,
        conv_b=m_conv_b, w_down=m_w_down, norm_out=m_norm_out, w_fin=m_w_fin, b_fin=m_b_fin)
    v_in = dict(
        norm_mix=v_norm_mix, norm_ffn=v_norm_ffn, w_mod=v_w_mod, b_mod=v_b_mod, w_qkv=v_w_qkv, w_o_attn=v_w_o_attn,
        w_in_ssm=v_w_in_ssm, a_re=v_a_re, a_im=v_a_im, log_dt=v_log_dt, b_re=v_b_re, b_im=v_b_im, c_re=v_c_re,
        c_im=v_c_im, d_skip=v_d_skip, w_glu=v_w_glu, b_glu=v_b_glu, w_o_ssm=v_w_o_ssm, w_up=v_w_up, conv_w=v_conv_w,
        conv_b=v_conv_b, w_down=v_w_down, norm_out=v_norm_out, w_fin=v_w_fin, b_fin=v_b_fin)
    names = list(weights)
    for n_ in names:
        grads[n_] = grads[n_].reshape(weights[n_].shape)

    big = ("w_mod", "w_qkv", "w_o_attn", "w_in_ssm", "w_glu", "w_o_ssm", "w_up", "w_down", "w_fin")
    delta, new_m, new_v = {}, {}, {}
    for n_ in big:
        shp = weights[n_].shape
        two_d = lambda a: a.reshape(-1, shp[-1])
        d_, m_, v_ = adamw(two_d(weights[n_]), two_d(grads[n_]), two_d(m_in[n_]), two_d(v_in[n_]), name=f"adamw_{n_}")
        delta[n_], new_m[n_], new_v[n_] = d_.reshape(shp), m_.reshape(shp), v_.reshape(shp)
    rest = [n_ for n_ in names if n_ not in big]

    def pack(tree):
        f = jnp.concatenate([tree[n_].reshape(-1) for n_ in rest])
        pad_to = -(-f.shape[0] // (8 * width)) * (8 * width)
        return jnp.pad(f, (0, pad_to - f.shape[0]), constant_values=1.0).reshape(-1, width)

    d_, m_, v_ = adamw(pack(weights), pack(grads), pack(m_in), pack(v_in), name="adamw_small")
    off = 0
    for n_ in rest:
        sz = int(math.prod(weights[n_].shape))
        shp = weights[n_].shape
        delta[n_] = d_.reshape(-1)[off:off + sz].reshape(shp)
        new_m[n_] = m_.reshape(-1)[off:off + sz].reshape(shp)
        new_v[n_] = v_.reshape(-1)[off:off + sz].reshape(shp)
        off += sz

    return (loss, grad_x.reshape(B, S, D), *[grads[n_] for n_ in names], *[delta[n_] for n_ in names],
            *[new_m[n_] for n_ in names], *[new_v[n_] for n_ in names])
```

```python
import functools
import math

import jax
import jax.numpy as jnp
from jax import lax
from jax.experimental import pallas as pl
from jax.experimental.pallas import tpu as pltpu

F32 = jnp.float32
BF16 = jnp.bfloat16
MESH = pl.DeviceIdType.MESH

HEAD_DIM = 64
SSM_GROUP = 16
STATE = 64
GROUPS_PER_BLOCK = 8
SEGMENTS = 16
SCAN_UNROLL = 4
EPS = 1e-6
ADAM_LR = 0.001
ADAM_B1 = 0.9
ADAM_B2 = 0.999
ADAM_EPS = 1e-08
ADAM_WD = 0.01
ADAM_STEP = 10
N_CHIPS = 4
N_DEV = 8
V7X_VMEM_LIMIT = 56 * 1024 * 1024
ATT_BLOCK = 128
ATT_HEADS = 8
ATT_HEADS_BWD = 8


def _tile(n, prefs):
    for p in prefs:
        if n % p == 0:
            return p
    return n


def _params(sem, vmem=V7X_VMEM_LIMIT):
    return pltpu.CompilerParams(dimension_semantics=sem, vmem_limit_bytes=vmem)


def matmul(a, b, *, ta=False, tb=False, bias=None, out_dtype=F32, b_chips=False, out_chips=False, name):
    a_parts = a.shape[0] if a.ndim == 3 else 1
    if a_parts > 1:
        assert not ta
        M, K = a.shape[1], a_parts * a.shape[2]
    elif ta:
        K, M = a.shape
    else:
        M, K = a.shape
    b_parts = b.shape[0] if b_chips else 1
    b_rows, b_cols = (b.shape[1], b_parts * b.shape[2]) if b_chips else b.shape
    if tb:
        N, Kb = b_rows, b_cols
    else:
        Kb, N = b_rows, b_cols
    assert K == Kb, (a.shape, b.shape, ta, tb)
    n_cut = math.gcd(N // (N_CHIPS if out_chips else 1), N // (b_parts if not tb else 1))
    k_cut = math.gcd(K // (b_parts if tb else 1), K // a_parts)
    tm = _tile(M, (1024, 512, 256, 128))
    tn = _tile(n_cut, (1024, 1408, 768, 512, 256, 128))
    tk = k_cut if k_cut <= 2816 else _tile(k_cut, (1024, 512, 256, 128))
    nk = K // tk
    npc = N // N_CHIPS // tn
    npb = N // b_parts // tn
    kpb = K // b_parts // tk
    kpa = K // a_parts // tk
    dims = (((0,) if ta else (1,), (1,) if tb else (0,)), ((), ()))

    def body(*refs):
        a_ref, b_ref = refs[:2]
        bias_ref = refs[2] if bias is not None else None
        o_ref = refs[-2] if nk > 1 else refs[-1]

        def finish(r):
            if bias_ref is not None:
                r = r + bias_ref[...]
            o_ref[...] = r.astype(o_ref.dtype)

        prod = lax.dot_general(a_ref[...].astype(BF16), b_ref[...].astype(BF16), dims, preferred_element_type=F32)
        if nk == 1:
            finish(prod)
            return
        acc_ref = refs[-1]
        k = pl.program_id(2)

        @pl.when(k == 0)
        def _():
            acc_ref[...] = prod

        @pl.when(k > 0)
        def _():
            acc_ref[...] += prod

        @pl.when(k == nk - 1)
        def _():
            finish(acc_ref[...])

    if a_parts > 1:
        a_spec = pl.BlockSpec((None, tm, tk), lambda i, j, k: (lax.div(k, kpa), i, lax.rem(k, kpa)))
    else:
        a_spec = pl.BlockSpec((tk, tm), lambda i, j, k: (k, i)) if ta else pl.BlockSpec((tm, tk), lambda i, j, k: (i, k))
    if not b_chips:
        b_spec = pl.BlockSpec((tn, tk), lambda i, j, k: (j, k)) if tb else pl.BlockSpec((tk, tn), lambda i, j, k: (k, j))
    elif tb:
        b_spec = pl.BlockSpec((None, tn, tk), lambda i, j, k: (lax.div(k, kpb), j, lax.rem(k, kpb)))
    else:
        b_spec = pl.BlockSpec((None, tk, tn), lambda i, j, k: (lax.div(j, npb), k, lax.rem(j, npb)))
    in_specs = [a_spec, b_spec]
    args = [a, b]
    if bias is not None:
        in_specs.append(pl.BlockSpec((1, tn), lambda i, j, k: (0, j)))
        args.append(bias.reshape(1, N).astype(F32))
    if out_chips:
        out_shape = jax.ShapeDtypeStruct((N_CHIPS, M, N // N_CHIPS), out_dtype)
        out_spec = pl.BlockSpec((None, tm, tn), lambda i, j, k: (lax.div(j, npc), i, lax.rem(j, npc)))
    else:
        out_shape = jax.ShapeDtypeStruct((M, N), out_dtype)
        out_spec = pl.BlockSpec((tm, tn), lambda i, j, k: (i, j))
    return pl.pallas_call(
        body, name=name,
        out_shape=out_shape,
        grid=(M // tm, N // tn, nk),
        in_specs=in_specs,
        out_specs=out_spec,
        scratch_shapes=[pltpu.VMEM((tm, tn), F32)] if nk > 1 else [],
        compiler_params=_params(("parallel", "parallel", "arbitrary")),
    )(*args)


def rowwise(fn, tiled, per_seq, glob, out_tiled, out_seq, *, B, S, name, rows=512):
    tm = _tile(S, (rows, 128, 64, 32, 16, 8))
    nt = S // tm
    n_in = len(tiled) + len(per_seq) + len(glob)
    n_ot = len(out_tiled)

    def body(*refs):
        ins = refs[:n_in]
        outs = refs[n_in:]
        vals = fn(*[r[...] for r in ins])
        if not isinstance(vals, (tuple, list)):
            vals = (vals,)
        assert len(vals) == len(outs), (name, len(vals), len(outs))
        for o_ref, v in zip(outs[:n_ot], vals[:n_ot]):
            o_ref[...] = v.astype(o_ref.dtype)
        t = pl.program_id(1)
        for o_ref, v in zip(outs[n_ot:], vals[n_ot:]):
            def first(o_ref=o_ref, v=v):
                o_ref[...] = v.astype(F32)

            def later(o_ref=o_ref, v=v):
                o_ref[...] += v.astype(F32)

            pl.when(t == 0)(first)
            pl.when(t > 0)(later)

    in_specs = [pl.BlockSpec((tm, a.shape[1]), lambda b, t: (b * nt + t, 0)) for a in tiled]
    in_specs += [pl.BlockSpec((None, 1, a.shape[1]), lambda b, t: (b, 0, 0)) for a in per_seq]
    in_specs += [pl.BlockSpec(a.shape, lambda b, t: (0,) * a.ndim) for a in glob]
    out_shape = [jax.ShapeDtypeStruct((B * S, w), dt) for w, dt in out_tiled]
    out_shape += [jax.ShapeDtypeStruct((B, 1, w), F32) for w in out_seq]
    out_specs = [pl.BlockSpec((tm, w), lambda b, t: (b * nt + t, 0)) for w, _ in out_tiled]
    out_specs += [pl.BlockSpec((None, 1, w), lambda b, t: (b, 0, 0)) for w in out_seq]
    res = pl.pallas_call(
        body, name=name, out_shape=out_shape, grid=(B, nt), in_specs=in_specs, out_specs=out_specs,
        compiler_params=_params(("parallel", "arbitrary")),
    )(*tiled, *[a.reshape(B, 1, a.shape[1]) for a in per_seq], *glob)
    res = list(res)
    for i in range(n_ot, len(res)):
        res[i] = res[i].reshape(B, res[i].shape[-1])
    return res


def _rms(x):
    r = lax.rsqrt(jnp.mean(x * x, axis=-1, keepdims=True) + EPS)
    return x * r, r


def norm_mod_fwd(x, g, sh, sc, *, B, S, name):
    def fn(x, sh, sc, g):
        xn, _ = _rms(x)
        return (xn * g) * (1.0 + sc) + sh

    return rowwise(fn, [x], [sh, sc], [g.reshape(1, -1)], [(x.shape[1], BF16)], [], B=B, S=S, name=name)[0]


def _norm_mod_bwd_math(dh, x, sc, g):
    xn, r = _rms(x)
    y = xn * g
    dy = dh * (1.0 + sc)
    dxn = dy * g
    dx = r * (dxn - xn * jnp.mean(dxn * xn, axis=-1, keepdims=True))
    dsh = jnp.sum(dh, axis=0, keepdims=True)
    dsc = jnp.sum(dh * y, axis=0, keepdims=True)
    dg = jnp.sum(dy * xn, axis=0, keepdims=True)
    return dx, dsh, dsc, dg


def norm_mod_bwd(dh, x, dres, g, sc, *, B, S, name):
    D = x.shape[1]

    def fn(dh, x, dres, sc, g):
        dx, dsh, dsc, dg = _norm_mod_bwd_math(dh.astype(F32), x, sc, g)
        return dres + dx, dsh, dsc, dg

    return rowwise(fn, [dh, x, dres], [sc], [g.reshape(1, -1)], [(D, F32)], [D, D, D], B=B, S=S, name=name)


def gate_res_fwd(x, y, gate, *, B, S, name):
    return rowwise(lambda x, y, g: x + g * y, [x, y], [gate], [], [(x.shape[1], F32)], [], B=B, S=S, name=name)[0]


def gate_res_bwd(dx, y, gate, *, B, S, name):
    D = dx.shape[1]

    def fn(dx, y, g):
        return g * dx, jnp.sum(dx * y, axis=0, keepdims=True)

    return rowwise(fn, [dx, y], [gate], [], [(D, BF16)], [D], B=B, S=S, name=name)


def final_loss(x, tgt, g, sh, sc, *, B, S, name):
    D = x.shape[1]

    def fn(x, tgt, sh, sc, g):
        xn, _ = _rms(x)
        y = (xn * g) * (1.0 + sc) + sh
        err = y - tgt
        loss = 0.5 * jnp.sum(err * err, axis=0, keepdims=True) * (1.0 / D)
        dx, dsh, dsc, dg = _norm_mod_bwd_math(err * (1.0 / D), x, sc, g)
        return dx, loss, dsh, dsc, dg

    return rowwise(fn, [x, tgt], [sh, sc], [g.reshape(1, -1)], [(D, F32)], [D, D, D, D], B=B, S=S, name=name)


def _gelu(y):
    c0 = math.sqrt(2.0 / math.pi)
    t = jnp.tanh(c0 * (y + 0.044715 * (y * y * y)))
    return 0.5 * y * (1.0 + t), t


def _sigmoid(s):
    return 1.0 / (1.0 + jnp.exp(-s))


def gelu_fwd(y, *, B, S, name):
    return rowwise(lambda y: _gelu(y)[0], [y], [], [], [(y.shape[1], BF16)], [], B=B, S=S, name=name)[0]


def glu_fwd(y, s, *, B, S, name):
    return rowwise(lambda y, s: _gelu(y)[0] * _sigmoid(s), [y, s], [], [], [(y.shape[1], BF16)], [], B=B, S=S,
                   name=name)[0]


def glu_bwd1(y, s, dg, *, B, S, name):
    D = y.shape[1]

    def fn(y, s, dg):
        z = _gelu(y)[0]
        sig = _sigmoid(s)
        ds = dg * z * sig * (1.0 - sig)
        return ds, dg * sig, jnp.sum(ds, axis=0, keepdims=True)

    return rowwise(fn, [y, s, dg], [], [], [(D, BF16), (D, F32)], [D], B=B, S=S, name=name)


def glu_bwd2(y, dz1, dz2, *, B, S, name):
    D = y.shape[1]
    c0 = math.sqrt(2.0 / math.pi)

    def fn(y, dz1, dz2):
        _, t = _gelu(y)
        dgelu = 0.5 * (1.0 + t) + 0.5 * y * (1.0 - t * t) * c0 * (1.0 + 3.0 * 0.044715 * y * y)
        return (dz1 + dz2) * dgelu

    return rowwise(fn, [y, dz1, dz2], [], [], [(D, F32)], [], B=B, S=S, name=name)[0]


def silu_rows(c, *, name):
    R, W = c.shape
    return rowwise(lambda c: c * _sigmoid(c), [c], [], [], [(W, F32)], [], B=1, S=R, name=name)[0]


def _shift_down(cur, h6, h7):
    rows = lax.broadcasted_iota(jnp.int32, cur.shape, 0)
    m1 = jnp.where(rows == 0, h7, pltpu.roll(cur, 1, 0))
    m2 = jnp.where(rows == 0, h6, jnp.where(rows == 1, h7, pltpu.roll(cur, 2, 0)))
    return m1, m2


def _conv3(cur, halo_ref, w_ref, has_prev):
    h6 = jnp.where(has_prev, halo_ref[6:7, :], 0.0)
    h7 = jnp.where(has_prev, halo_ref[7:8, :], 0.0)
    m1, m2 = _shift_down(cur, h6, h7)
    return w_ref[2:3, :] * cur + w_ref[1:2, :] * m1 + w_ref[0:1, :] * m2, m1, m2


def _conv_tiles(S, F):
    ts = _tile(S, (1024, 512, 256, 128, 64, 32, 16, 8))
    tn = _tile(F, (256, 128))
    return ts, tn, S // ts, F // tn


def conv_gate_fwd(up, cw, cb, *, B, S, name):
    F = up.shape[1] // 2
    ts, tn, nts, nF = _conv_tiles(S, F)
    hb = ts // 8

    def body(g_ref, gh_ref, v_ref, vh_ref, wg_ref, wv_ref, bg_ref, bv_ref, o_ref):
        has_prev = pl.program_id(2) > 0
        gc = _conv3(g_ref[...], gh_ref, wg_ref, has_prev)[0] + bg_ref[...]
        vc = _conv3(v_ref[...], vh_ref, wv_ref, has_prev)[0] + bv_ref[...]
        o_ref[...] = (gc * _sigmoid(gc) * vc).astype(o_ref.dtype)

    def cur(off):
        return pl.BlockSpec((ts, tn), lambda b, j, t: (b * nts + t, j + off))

    def halo(off):
        return pl.BlockSpec((8, tn), lambda b, j, t: (jnp.maximum((b * nts + t) * hb - 1, 0), j + off))

    def vec(rows, off):
        return pl.BlockSpec((rows, tn), lambda b, j, t: (0, j + off))

    return pl.pallas_call(
        body, name=name, out_shape=jax.ShapeDtypeStruct((B * S, F), BF16), grid=(B, nF, nts),
        in_specs=[cur(0), halo(0), cur(nF), halo(nF), vec(3, 0), vec(3, nF), vec(1, 0), vec(1, nF)],
        out_specs=pl.BlockSpec((ts, tn), lambda b, j, t: (b * nts + t, j)),
        compiler_params=_params(("parallel", "parallel", "arbitrary")),
    )(up, up, up, up, cw, cw, cb, cb)


def conv_gate_bwd1(up, dact, cw, cb, *, B, S, name):
    F = up.shape[1] // 2
    ts, tn, nts, nF = _conv_tiles(S, F)
    hb = ts // 8

    def body(g_ref, gh_ref, v_ref, vh_ref, da_ref, wg_ref, wv_ref, bg_ref, bv_ref, d_ref, db_ref):
        t = pl.program_id(2)
        has_prev = t > 0
        gc = _conv3(g_ref[...], gh_ref, wg_ref, has_prev)[0] + bg_ref[...]
        vc = _conv3(v_ref[...], vh_ref, wv_ref, has_prev)[0] + bv_ref[...]
        sig = _sigmoid(gc)
        da = da_ref[...]
        dg = da * vc * (sig * (1.0 + gc * (1.0 - sig)))
        dv = da * (gc * sig)
        d_ref[0] = dg
        d_ref[1] = dv
        part = jnp.concatenate([jnp.sum(dg, axis=0, keepdims=True), jnp.sum(dv, axis=0, keepdims=True)], axis=0)

        @pl.when(t == 0)
        def _():
            db_ref[...] = part

        @pl.when(t > 0)
        def _():
            db_ref[...] += part

    def cur(off):
        return pl.BlockSpec((ts, tn), lambda b, j, t: (b * nts + t, j + off))

    def halo(off):
        return pl.BlockSpec((8, tn), lambda b, j, t: (jnp.maximum((b * nts + t) * hb - 1, 0), j + off))

    def vec(rows, off):
        return pl.BlockSpec((rows, tn), lambda b, j, t: (0, j + off))

    return pl.pallas_call(
        body, name=name,
        out_shape=[jax.ShapeDtypeStruct((2, B * S, F), F32), jax.ShapeDtypeStruct((B, 2, F), F32)],
        grid=(B, nF, nts),
        in_specs=[cur(0), halo(0), cur(nF), halo(nF), cur(0), vec(3, 0), vec(3, nF), vec(1, 0), vec(1, nF)],
        out_specs=[pl.BlockSpec((2, ts, tn), lambda b, j, t: (0, b * nts + t, j)),
                   pl.BlockSpec((None, 2, tn), lambda b, j, t: (b, 0, j))],
        compiler_params=_params(("parallel", "parallel", "arbitrary")),
    )(up, up, up, up, dact, cw, cw, cb, cb)


def conv_bwd2(d3, up, cw, *, B, S, name):
    F = up.shape[1] // 2
    ts, tn, nts, nF = _conv_tiles(S, F)
    hb = ts // 8
    last8 = B * S // 8 - 1

    def body(d_ref, da_ref, u_ref, uh_ref, w_ref, o_ref, dw_ref):
        t = pl.program_id(3)
        d = d_ref[...]
        has_next = t < nts - 1
        a0 = jnp.where(has_next, da_ref[0:1, :], 0.0)
        a1 = jnp.where(has_next, da_ref[1:2, :], 0.0)
        rows = lax.broadcasted_iota(jnp.int32, d.shape, 0)
        p1 = jnp.where(rows == ts - 1, a0, pltpu.roll(d, ts - 1, 0))
        p2 = jnp.where(rows == ts - 1, a1, jnp.where(rows == ts - 2, a0, pltpu.roll(d, ts - 2, 0)))
        o_ref[...] = (w_ref[2:3, :] * d + w_ref[1:2, :] * p1 + w_ref[0:1, :] * p2).astype(o_ref.dtype)
        u = u_ref[...]
        has_prev = t > 0
        h6 = jnp.where(has_prev, uh_ref[6:7, :], 0.0)
        h7 = jnp.where(has_prev, uh_ref[7:8, :], 0.0)
        m1, m2 = _shift_down(u, h6, h7)
        part = jnp.concatenate([jnp.sum(d * m2, axis=0, keepdims=True), jnp.sum(d * m1, axis=0, keepdims=True),
                                jnp.sum(d * u, axis=0, keepdims=True)], axis=0)

        @pl.when(t == 0)
        def _():
            dw_ref[...] = part

        @pl.when(t > 0)
        def _():
            dw_ref[...] += part

    return pl.pallas_call(
        body, name=name,
        out_shape=[jax.ShapeDtypeStruct((B * S, 2 * F), BF16), jax.ShapeDtypeStruct((B, 3, 2 * F), F32)],
        grid=(B, 2, nF, nts),
        in_specs=[
            pl.BlockSpec((None, ts, tn), lambda b, g, j, t: (g, b * nts + t, j)),
            pl.BlockSpec((None, 8, tn), lambda b, g, j, t: (g, jnp.minimum((b * nts + t + 1) * hb, last8), j)),
            pl.BlockSpec((ts, tn), lambda b, g, j, t: (b * nts + t, g * nF + j)),
            pl.BlockSpec((8, tn), lambda b, g, j, t: (jnp.maximum((b * nts + t) * hb - 1, 0), g * nF + j)),
            pl.BlockSpec((3, tn), lambda b, g, j, t: (0, g * nF + j)),
        ],
        out_specs=[pl.BlockSpec((ts, tn), lambda b, g, j, t: (b * nts + t, g * nF + j)),
                   pl.BlockSpec((None, 3, tn), lambda b, g, j, t: (b, 0, g * nF + j))],
        compiler_params=_params(("parallel", "parallel", "parallel", "arbitrary")),
    )(d3, d3, up, up, cw)


def conv_gate_bwd(up, dact, cw, cb, *, B, S, name):
    F = up.shape[1] // 2
    tn = _tile(F, (256, 128))
    nF = F // tn

    def body(g_ref, v_ref, da_ref, wg_ref, wv_ref, bg_ref, bv_ref, o_ref, dw_ref, db_ref):
        rows = lax.broadcasted_iota(jnp.int32, (S, tn), 0)

        def earlier(x, k):
            return jnp.where(rows >= k, pltpu.roll(x, k, 0), 0.0)

        def later(x, k):
            return jnp.where(rows < S - k, pltpu.roll(x, S - k, 0), 0.0)

        def conv(x, w_ref):
            x1, x2 = earlier(x, 1), earlier(x, 2)
            return w_ref[2:3, :] * x + w_ref[1:2, :] * x1 + w_ref[0:1, :] * x2, x1, x2

        def back(d, x, x1, x2, w_ref, half):
            o_ref[half] = (w_ref[2:3, :] * d + w_ref[1:2, :] * later(d, 1) + w_ref[0:1, :] * later(d, 2)
                           ).astype(o_ref.dtype)
            dw_ref[half] = jnp.concatenate([jnp.sum(d * x2, axis=0, keepdims=True),
                                            jnp.sum(d * x1, axis=0, keepdims=True),
                                            jnp.sum(d * x, axis=0, keepdims=True)], axis=0)
            return jnp.sum(d, axis=0, keepdims=True)

        g, v, da = g_ref[...], v_ref[...], da_ref[...]
        gc, g1, g2 = conv(g, wg_ref)
        vc, v1, v2 = conv(v, wv_ref)
        gc = gc + bg_ref[...]
        vc = vc + bv_ref[...]
        sig = _sigmoid(gc)
        dg = da * vc * (sig * (1.0 + gc * (1.0 - sig)))
        dv = da * (gc * sig)
        db_ref[...] = jnp.concatenate([back(dg, g, g1, g2, wg_ref, 0), back(dv, v, v1, v2, wv_ref, 1)], axis=0)

    def cols(off):
        return pl.BlockSpec((S, tn), lambda b, j: (b, j + off))

    def vec(rows, off):
        return pl.BlockSpec((rows, tn), lambda b, j: (0, j + off))

    return pl.pallas_call(
        body, name=name,
        out_shape=[jax.ShapeDtypeStruct((2, B * S, F), BF16), jax.ShapeDtypeStruct((B, 2, 3, F), F32),
                   jax.ShapeDtypeStruct((B, 2, F), F32)],
        grid=(B, nF),
        in_specs=[cols(0), cols(nF), cols(0), vec(3, 0), vec(3, nF), vec(1, 0), vec(1, nF)],
        out_specs=[pl.BlockSpec((2, S, tn), lambda b, j: (0, b, j)),
                   pl.BlockSpec((None, 2, 3, tn), lambda b, j: (b, 0, 0, j)),
                   pl.BlockSpec((None, 2, tn), lambda b, j: (b, 0, j))],
        compiler_params=_params(("parallel", "parallel")),
    )(up, up, dact, cw, cw, cb, cb)


MASKED_LOG = -1e30


def _split2(x):
    bits = lax.bitcast_convert_type(x, jnp.uint32) & jnp.uint32(0xFFFF0000)
    hi = lax.bitcast_convert_type(bits, F32)
    return hi.astype(BF16), (x - hi).astype(BF16)


def _split_dot(x, m):
    hi, lo = _split2(x)
    return jnp.dot(hi, m, preferred_element_type=F32) + jnp.dot(lo, m, preferred_element_type=F32)


def _nt(a, b):
    return lax.dot_general(a, b, (((1,), (1,)), ((), ())), preferred_element_type=F32)


def _tn(a, b):
    return lax.dot_general(a, b, (((0,), (0,)), ((), ())), preferred_element_type=F32)


def _att_scores(q, k, mask, prescaled=False):
    z = _nt(q, k)
    if not prescaled:
        z = z * (HEAD_DIM ** -0.5)
    e = jnp.exp(-jnp.abs(z))
    sp = jnp.log(1.0 + e)
    lb = jnp.minimum(z, 0.0) - sp
    l1 = lb - z
    if mask is not None:
        lb = jnp.where(mask, lb, MASKED_LOG)
        l1 = jnp.where(mask, l1, 0.0)
    return z, lb, l1, e


def _col_to_row(col, eye):
    return jnp.sum(jnp.where(eye, col, 0.0), axis=0, keepdims=True)


def _row_to_col(row, eye):
    return jnp.sum(jnp.where(eye, row, 0.0), axis=1, keepdims=True)


def attn_fwd(q, k, v, *, name):
    B, H, S, dh = q.shape
    T = ATT_BLOCK
    nq = S // T

    G = _tile(H, (ATT_HEADS, 2))

    def body(q_ref, k_ref, v_ref, o_ref, l_ref):
        r = lax.broadcasted_iota(jnp.int32, (T, T), 0)
        c = lax.broadcasted_iota(jnp.int32, (T, T), 1)
        later = (r > c).astype(BF16)
        eye = r == c
        diag = c < r
        blk = lax.broadcasted_iota(jnp.int32, (nq, T), 0)

        later2 = jnp.concatenate([later, later], axis=0)

        def scores(g, qb, k0, mask):
            _, lb, l1, _ = _att_scores(qb, k_ref[g, pl.ds(k0, T), :], mask)
            return lb, jnp.concatenate(_split2(l1), axis=1), jnp.sum(l1, axis=1, keepdims=True)

        def weigh_all(k0, sc, st):
            suf = jnp.dot(jnp.concatenate([s[1] for s in sc], axis=0), later2, preferred_element_type=F32)
            out = []
            for g in range(G):
                lb, _, rowsum = sc[g]
                acc, run = st[g]
                w = jnp.exp(lb + suf[g * T:(g + 1) * T] + run)
                acc = acc + jnp.dot(w.astype(BF16), v_ref[g, pl.ds(k0, T), :], preferred_element_type=F32)
                out.append((acc, run + rowsum))
            return tuple(out)

        def qblock(i, totals):
            q0 = pl.multiple_of(i * T, T)
            qbs = [q_ref[g, pl.ds(q0, T), :] for g in range(G)]
            sc0 = tuple(scores(g, qbs[g], q0, diag) for g in range(G))
            st0 = tuple((jnp.zeros((T, dh), F32), jnp.zeros((T, 1), F32)) for _ in range(G))

            def kblock(jj, carry):
                sc, st = carry
                k_next = pl.multiple_of((i - jj) * T, T)
                k_cur = pl.multiple_of((i - jj + 1) * T, T)
                st = weigh_all(k_cur, sc, st)
                sc_next = tuple(scores(g, qbs[g], k_next, None) for g in range(G))
                return sc_next, st

            sc, st = lax.fori_loop(1, i + 1, kblock, (sc0, st0))
            st = weigh_all(0, sc, st)
            for g in range(G):
                o_ref[g, pl.ds(q0, T), :] = st[g][0]
            return tuple(jnp.where(blk == i, _col_to_row(st[g][1], eye), totals[g]) for g in range(G))

        totals = lax.fori_loop(0, nq, qblock, tuple(jnp.zeros((nq, T), F32) for _ in range(G)))
        for g in range(G):
            l_ref[g] = totals[g]

    spec = pl.BlockSpec((None, G, S, dh), lambda b, h: (b, h, 0, 0))
    lspec = pl.BlockSpec((None, G, nq, T), lambda b, h: (b, h, 0, 0))
    return pl.pallas_call(
        body, name=name,
        out_shape=[jax.ShapeDtypeStruct((B, H, S, dh), F32), jax.ShapeDtypeStruct((B, H, nq, T), F32)],
        grid=(B, H // G), in_specs=[spec, spec, spec], out_specs=[spec, lspec],
        compiler_params=_params(("parallel", "parallel")),
    )(q, k, v)


def attn_bwd(q, k, v, ltot, do, *, name):
    B, H, S, dh = q.shape
    T = ATT_BLOCK
    nq = S // T
    scale = HEAD_DIM ** -0.5

    G = _tile(H, (ATT_HEADS_BWD, 2))

    def body(q_ref, k_ref, v_ref, l_ref, do_ref, dq_ref, dk_ref, dv_ref, dk_acc, dv_acc):
        r = lax.broadcasted_iota(jnp.int32, (T, T), 0)
        c = lax.broadcasted_iota(jnp.int32, (T, T), 1)
        upto = (r <= c).astype(BF16)
        before = (r < c).astype(BF16)
        upto2 = jnp.concatenate([upto, upto], axis=0)
        before2 = jnp.concatenate([before, before], axis=0)
        eye = r == c
        diag = c < r
        blk = lax.broadcasted_iota(jnp.int32, (nq, T), 0)
        dk_acc[...] = jnp.zeros_like(dk_acc)
        dv_acc[...] = jnp.zeros_like(dv_acc)

        def scores(g, qb, dob, k0, mask):
            z, lb, l1, e = _att_scores(qb, k_ref[g, pl.ds(k0, T), :], mask)
            inv = 1.0 / (1.0 + e)
            small = e * inv
            pos = z >= 0.0
            beta = jnp.where(pos, inv, small)
            omb = jnp.where(pos, small, inv)
            if mask is not None:
                beta = jnp.where(mask, beta, 0.0)
            dw = _nt(dob, v_ref[g, pl.ds(k0, T), :])
            return lb, jnp.concatenate(_split2(l1), axis=1), jnp.sum(l1, axis=1, keepdims=True), dw, beta, omb

        def grads_all(qbs, dobs, tots, k0, sc, st):
            pre = jnp.dot(jnp.concatenate([s[1] for s in sc], axis=0), upto2, preferred_element_type=F32)
            dlws = []
            for g in range(G):
                lb = sc[g][0]
                w = jnp.exp(lb + (tots[g] - (pre[g * T:(g + 1) * T] + st[g][1])))
                dv_acc[g, pl.ds(k0, T), :] += _tn(w.astype(BF16), dobs[g])
                dlws.append(sc[g][3] * w)
            pre_d = jnp.dot(jnp.concatenate([jnp.concatenate(_split2(d), axis=1) for d in dlws], axis=0), before2,
                            preferred_element_type=F32)
            out = []
            for g in range(G):
                _, _, rowsum, _, beta, omb = sc[g]
                dq, run_l, run_d = st[g]
                dl1 = pre_d[g * T:(g + 1) * T] + run_d
                dz = ((dlws[g] * omb - dl1 * beta) * scale).astype(BF16)
                dq = dq + jnp.dot(dz, k_ref[g, pl.ds(k0, T), :], preferred_element_type=F32)
                dk_acc[g, pl.ds(k0, T), :] += _tn(dz, qbs[g])
                out.append((dq, run_l + rowsum, run_d + jnp.sum(dlws[g], axis=1, keepdims=True)))
            return tuple(out)

        def block_inputs(i, q0):
            qbs = [q_ref[g, pl.ds(q0, T), :] for g in range(G)]
            dobs = [do_ref[g, pl.ds(q0, T), :] for g in range(G)]
            tots = [_row_to_col(jnp.sum(jnp.where(blk == i, l_ref[g], 0.0), axis=0, keepdims=True), eye)
                    for g in range(G)]
            z1 = jnp.zeros((T, 1), F32)
            return qbs, dobs, tots, tuple((jnp.zeros((T, dh), F32), z1, z1) for _ in range(G))

        qbs, dobs, tots, st = block_inputs(0, 0)
        st = grads_all(qbs, dobs, tots, 0, tuple(scores(g, qbs[g], dobs[g], 0, diag) for g in range(G)), st)
        for g in range(G):
            dq_ref[g, 0:T, :] = st[g][0].astype(dq_ref.dtype)

        def qblock(i, carry0):
            q0 = pl.multiple_of(i * T, T)
            qbs, dobs, tots, st0 = block_inputs(i, q0)
            sc0 = tuple(scores(g, qbs[g], dobs[g], 0, None) for g in range(G))

            def kblock(j, carry):
                sc, st = carry
                k_cur = pl.multiple_of(j * T, T)
                k_next = pl.multiple_of((j + 1) * T, T)
                sc_next = tuple(scores(g, qbs[g], dobs[g], k_next, None) for g in range(G))
                st = grads_all(qbs, dobs, tots, k_cur, sc, st)
                return sc_next, st

            sc, st = lax.fori_loop(0, i - 1, kblock, (sc0, st0))
            k_last = pl.multiple_of((i - 1) * T, T)
            st = grads_all(qbs, dobs, tots, k_last, sc, st)
            sc_diag = tuple(scores(g, qbs[g], dobs[g], q0, diag) for g in range(G))
            st = grads_all(qbs, dobs, tots, q0, sc_diag, st)
            for g in range(G):
                dq_ref[g, pl.ds(q0, T), :] = st[g][0].astype(dq_ref.dtype)
            return carry0

        lax.fori_loop(1, nq, qblock, 0)
        dk_ref[...] = dk_acc[...].astype(dk_ref.dtype)
        dv_ref[...] = dv_acc[...].astype(dv_ref.dtype)

    spec = pl.BlockSpec((None, G, S, dh), lambda b, h: (b, h, 0, 0))
    lspec = pl.BlockSpec((None, G, nq, T), lambda b, h: (b, h, 0, 0))
    shp = jax.ShapeDtypeStruct((B, H, S, dh), BF16)
    return pl.pallas_call(
        body, name=name, out_shape=[shp, shp, shp], grid=(B, H // G),
        in_specs=[spec, spec, spec, lspec, spec], out_specs=[spec] * 3,
        scratch_shapes=[pltpu.VMEM((G, S, dh), F32), pltpu.VMEM((G, S, dh), F32)],
        compiler_params=_params(("parallel", "parallel")),
    )(q, k, v, ltot, do)


def _wide_consts(T, W):
    r = lax.broadcasted_iota(jnp.int32, (W, W), 0)
    c = lax.broadcasted_iota(jnp.int32, (W, W), 1)
    two = lambda m: jnp.concatenate([m.astype(BF16)] * 2, axis=0)
    qrow = lax.broadcasted_iota(jnp.int32, (T, W), 0)
    kcol = lax.broadcasted_iota(jnp.int32, (T, W), 1)
    er = lax.broadcasted_iota(jnp.int32, (T, T), 0)
    ec = lax.broadcasted_iota(jnp.int32, (T, T), 1)
    return two(r > c), two(r <= c), two(r < c), qrow, kcol, er == ec


def attn_fwd_wide(q, k, v, shards=(), slots=(), *, name):
    B, H, S, dh = q.shape
    T = ATT_BLOCK
    W = 2 * T
    nq = S // T
    G = _tile(H, (ATT_HEADS, 2))
    n = len(shards)
    n_steps = B * (H // G)

    def body(*refs):
        q_ref, k_ref, v_ref = refs[:3]
        o_ref, l_ref = refs[3 + 2 * n:5 + 2 * n]
        step_id = pl.program_id(0) * (H // G) + pl.program_id(1)
        if n:
            gather = _ShardGather(refs[3:3 + n], refs[5 + 2 * n:5 + 3 * n], *refs[5 + 3 * n:])
            pl.when(step_id == 0)(gather.send)
            pl.when(step_id == n_steps - 1)(gather.forward)
        later2, _, _, qrow, kcol, eye = _wide_consts(T, W)
        blk = lax.broadcasted_iota(jnp.int32, (nq, T), 0)

        def step(qbs, k0, st, mask):
            parts, lbs, sums = [], [], []
            for g in range(G):
                _, lb, l1, _ = _att_scores(qbs[g], k_ref[g, pl.ds(k0, W), :], mask, prescaled=True)
                parts.append(jnp.concatenate(_split2(l1), axis=1))
                lbs.append(lb)
                sums.append(jnp.sum(l1, axis=1, keepdims=True))
            suf = jnp.dot(jnp.concatenate(parts, axis=0), later2, preferred_element_type=F32)
            out = []
            for g in range(G):
                acc, run = st[g]
                w = jnp.exp(lbs[g] + suf[g * T:(g + 1) * T] + run)
                acc = acc + jnp.dot(w.astype(BF16), v_ref[g, pl.ds(k0, W), :], preferred_element_type=F32)
                out.append((acc, run + sums[g]))
            return tuple(out)

        def qblock(i, totals):
            q0 = pl.multiple_of(i * T, T)
            qbs = [q_ref[g, pl.ds(q0, T), :] * (HEAD_DIM ** -0.5) for g in range(G)]
            half = jnp.right_shift(i, 1)
            last = half * W
            k_last = pl.multiple_of(last, W)
            mask = (k_last + kcol) < (q0 + qrow)
            st = tuple((jnp.zeros((T, dh), F32), jnp.zeros((T, 1), F32)) for _ in range(G))
            st = step(qbs, k_last, st, mask)

            def kblock(jj, st):
                return step(qbs, pl.multiple_of(last - jj * W, W), st, None)

            st = lax.fori_loop(1, half + 1, kblock, st)
            for g in range(G):
                o_ref[g, pl.ds(q0, T), :] = st[g][0]
            return tuple(jnp.where(blk == i, _col_to_row(st[g][1], eye), totals[g]) for g in range(G))

        totals = lax.fori_loop(0, nq, qblock, tuple(jnp.zeros((nq, T), F32) for _ in range(G)))
        for g in range(G):
            l_ref[g] = totals[g]
        if n:
            pl.when(step_id == n_steps - 1)(gather.finish)

    spec = pl.BlockSpec((None, G, S, dh), lambda b, h: (b, h, 0, 0))
    lspec = pl.BlockSpec((None, G, nq, T), lambda b, h: (b, h, 0, 0))
    dma = pltpu.SemaphoreType.DMA
    return pl.pallas_call(
        body, name=name,
        out_shape=[jax.ShapeDtypeStruct((B, H, S, dh), F32), jax.ShapeDtypeStruct((B, H, nq, T), F32)]
        + [jax.ShapeDtypeStruct(s.shape, s.dtype) for s in slots],
        grid=(B, H // G), in_specs=[spec, spec, spec] + _any_specs(2 * n), out_specs=[spec, lspec] + _any_specs(n),
        input_output_aliases={3 + n + i: 2 + i for i in range(n)},
        scratch_shapes=[dma((3 * n,))] * 4 if n else [],
        compiler_params=_params(("arbitrary", "arbitrary")),
    )(q, k, v, *shards, *slots)


def attn_bwd_wide(q, k, v, ltot, do, partials=(), *, name):
    B, H, S, dh = q.shape
    T = ATT_BLOCK
    W = 2 * T
    nq = S // T
    scale = HEAD_DIM ** -0.5
    G = _tile(H, (ATT_HEADS_BWD, 2))
    n = len(partials)
    n_steps = B * (H // G)

    def body(*refs):
        q_ref, k_ref, v_ref, l_ref, do_ref = refs[:5]
        dq_ref, dk_ref, dv_ref = refs[5 + n:8 + n]
        dk_acc, dv_acc = refs[8 + 2 * n:10 + 2 * n]
        step_id = pl.program_id(0) * (H // G) + pl.program_id(1)
        if n:
            scatter = _ChipScatter(refs[5:5 + n], refs[8 + n:8 + 2 * n], *refs[10 + 2 * n:])
            pl.when(step_id == 0)(scatter.send)
        _, upto2, before2, qrow, kcol, eye = _wide_consts(T, W)
        blk = lax.broadcasted_iota(jnp.int32, (nq, T), 0)
        dk_acc[...] = jnp.zeros_like(dk_acc)
        dv_acc[...] = jnp.zeros_like(dv_acc)

        def step(qbs, dobs, tots, k0, st, mask):
            sc = []
            for g in range(G):
                z, lb, l1, e = _att_scores(qbs[g], k_ref[g, pl.ds(k0, W), :], mask, prescaled=True)
                inv = 1.0 / (1.0 + e)
                small = e * inv
                pos = z >= 0.0
                beta = jnp.where(pos, inv, small)
                omb = jnp.where(pos, small, inv)
                if mask is not None:
                    beta = jnp.where(mask, beta, 0.0)
                dw = _nt(dobs[g], v_ref[g, pl.ds(k0, W), :])
                sc.append((lb, jnp.concatenate(_split2(l1), axis=1), jnp.sum(l1, axis=1, keepdims=True), dw, beta, omb))
            pre = jnp.dot(jnp.concatenate([s[1] for s in sc], axis=0), upto2, preferred_element_type=F32)
            dlws = []
            for g in range(G):
                w = jnp.exp(sc[g][0] + (tots[g] - (pre[g * T:(g + 1) * T] + st[g][1])))
                dv_acc[g, pl.ds(k0, W), :] += _tn(w.astype(BF16), dobs[g])
                dlws.append(sc[g][3] * w)
            pre_d = jnp.dot(jnp.concatenate([jnp.concatenate(_split2(d), axis=1) for d in dlws], axis=0), before2,
                            preferred_element_type=F32)
            out = []
            for g in range(G):
                _, _, rowsum, _, beta, omb = sc[g]
                dq, run_l, run_d = st[g]
                dl1 = pre_d[g * T:(g + 1) * T] + run_d
                dz = (dlws[g] * omb - dl1 * beta).astype(BF16)
                dq = dq + jnp.dot(dz, k_ref[g, pl.ds(k0, W), :], preferred_element_type=F32)
                dk_acc[g, pl.ds(k0, W), :] += _tn(dz, qbs[g])
                out.append((dq, run_l + rowsum, run_d + jnp.sum(dlws[g], axis=1, keepdims=True)))
            return tuple(out)

        def qblock(i, carry0):
            q0 = pl.multiple_of(i * T, T)
            qbs = [q_ref[g, pl.ds(q0, T), :] * scale for g in range(G)]
            dobs = [do_ref[g, pl.ds(q0, T), :] for g in range(G)]
            tots = [_row_to_col(jnp.sum(jnp.where(blk == i, l_ref[g], 0.0), axis=0, keepdims=True), eye)
                    for g in range(G)]
            z1 = jnp.zeros((T, 1), F32)
            st = tuple((jnp.zeros((T, dh), F32), z1, z1) for _ in range(G))

            def kblock(j, st):
                return step(qbs, dobs, tots, pl.multiple_of(j * W, W), st, None)

            half = jnp.right_shift(i, 1)
            st = lax.fori_loop(0, half, kblock, st)
            k_last = pl.multiple_of(half * W, W)
            st = step(qbs, dobs, tots, k_last, st, (k_last + kcol) < (q0 + qrow))
            for g in range(G):
                dq_ref[g, pl.ds(q0, T), :] = (st[g][0] * scale).astype(dq_ref.dtype)
            return carry0

        lax.fori_loop(0, nq, qblock, 0)
        dk_ref[...] = dk_acc[...].astype(dk_ref.dtype)
        dv_ref[...] = dv_acc[...].astype(dv_ref.dtype)
        if n:
            pl.when(step_id == n_steps - 1)(scatter.finish)

    spec = pl.BlockSpec((None, G, S, dh), lambda b, h: (b, h, 0, 0))
    lspec = pl.BlockSpec((None, G, nq, T), lambda b, h: (b, h, 0, 0))
    shp = jax.ShapeDtypeStruct((B, H, S, dh), BF16)
    dma = pltpu.SemaphoreType.DMA
    return pl.pallas_call(
        body, name=name,
        out_shape=[shp, shp, shp] + [jax.ShapeDtypeStruct((N_CHIPS - 1,) + a.shape[1:], a.dtype) for a in partials],
        grid=(B, H // G),
        in_specs=[spec, spec, spec, lspec, spec] + _any_specs(n), out_specs=[spec] * 3 + _any_specs(n),
        scratch_shapes=[pltpu.VMEM((G, S, dh), F32), pltpu.VMEM((G, S, dh), F32)]
        + ([dma((3 * n,)), dma((3 * n,))] if n else []),
        compiler_params=_params(("arbitrary", "arbitrary")),
    )(q, k, v, ltot, do, *partials)


def _pair_masks(x, first):
    zero = jnp.zeros_like(x)
    return jnp.where(first, x, zero), jnp.where(first, zero, x)


def attn_fwd_pairs(qkv, shards=(), slots=(), *, B, S, name):
    D = qkv.shape[1] // 3
    H = D // HEAD_DIM
    T = ATT_BLOCK
    W = 2 * T
    nq = S // T
    G = _tile(H, (ATT_HEADS, 2))
    P = G // 2
    LW = 2 * HEAD_DIM * P
    nsec = D // LW
    n = len(shards)
    n_steps = B * nsec

    def body(*refs):
        q_ref, k_ref, v_ref = refs[:3]
        o_ref, l_ref = refs[3 + 2 * n:5 + 2 * n]
        step_id = pl.program_id(0) * nsec + pl.program_id(1)
        if n:
            gather = _ShardGather(refs[3:3 + n], refs[5 + 2 * n:5 + 3 * n], *refs[5 + 3 * n:])
            pl.when(step_id == 0)(gather.send)
            pl.when(step_id == n_steps - 1)(gather.forward)
        later2, _, _, qrow, kcol, eye = _wide_consts(T, W)
        blk = lax.broadcasted_iota(jnp.int32, (nq, T), 0)
        first_q = lax.broadcasted_iota(jnp.int32, (T, 2 * HEAD_DIM), 1) < HEAD_DIM
        first_k = lax.broadcasted_iota(jnp.int32, (W, 2 * HEAD_DIM), 1) < HEAD_DIM

        def lanes(j):
            return slice(j * 2 * HEAD_DIM, (j + 1) * 2 * HEAD_DIM)

        def step(qms, k0, st, mask):
            accs, runs = st
            parts, lbs, sums = [], [], []
            for g in range(G):
                _, lb, l1, _ = _att_scores(qms[g], k_ref[pl.ds(k0, W), lanes(g // 2)], mask, prescaled=True)
                parts.append(jnp.concatenate(_split2(l1), axis=1))
                lbs.append(lb)
                sums.append(jnp.sum(l1, axis=1, keepdims=True))
            suf = jnp.dot(jnp.concatenate(parts, axis=0), later2, preferred_element_type=F32)
            new_accs, new_runs = [], []
            for j in range(P):
                vms = _pair_masks(v_ref[pl.ds(k0, W), lanes(j)], first_k)
                acc = accs[j]
                for h in range(2):
                    g = 2 * j + h
                    w = jnp.exp(lbs[g] + suf[g * T:(g + 1) * T] + runs[g])
                    acc = acc + jnp.dot(w.astype(BF16), vms[h], preferred_element_type=F32)
                    new_runs.append(runs[g] + sums[g])
                new_accs.append(acc)
            return tuple(new_accs), tuple(new_runs)

        def qblock(i, totals):
            q0 = pl.multiple_of(i * T, T)
            qms = []
            for j in range(P):
                qms.extend(_pair_masks(q_ref[pl.ds(q0, T), lanes(j)] * (HEAD_DIM ** -0.5), first_q))
            half = jnp.right_shift(i, 1)
            last = half * W
            k_last = pl.multiple_of(last, W)
            mask = (k_last + kcol) < (q0 + qrow)
            st = (tuple(jnp.zeros((T, 2 * HEAD_DIM), F32) for _ in range(P)),
                  tuple(jnp.zeros((T, 1), F32) for _ in range(G)))
            st = step(qms, k_last, st, mask)

            def kblock(jj, st):
                return step(qms, pl.multiple_of(last - jj * W, W), st, None)

            accs, runs = lax.fori_loop(1, half + 1, kblock, st)
            for j in range(P):
                o_ref[pl.ds(q0, T), lanes(j)] = accs[j].astype(o_ref.dtype)
            return tuple(jnp.where(blk == i, _col_to_row(runs[g], eye), totals[g]) for g in range(G))

        totals = lax.fori_loop(0, nq, qblock, tuple(jnp.zeros((nq, T), F32) for _ in range(G)))
        for g in range(G):
            l_ref[g] = totals[g]
        if n:
            pl.when(step_id == n_steps - 1)(gather.finish)

    def cols(section):
        return pl.BlockSpec((S, LW), lambda b, h: (b, section * nsec + h))

    lspec = pl.BlockSpec((None, G, nq, T), lambda b, h: (b, h, 0, 0))
    dma = pltpu.SemaphoreType.DMA
    return pl.pallas_call(
        body, name=name,
        out_shape=[jax.ShapeDtypeStruct((B * S, D), BF16), jax.ShapeDtypeStruct((B, H, nq, T), F32)]
        + [jax.ShapeDtypeStruct(s.shape, s.dtype) for s in slots],
        grid=(B, nsec), in_specs=[cols(0), cols(1), cols(2)] + _any_specs(2 * n),
        out_specs=[cols(0), lspec] + _any_specs(n),
        input_output_aliases={3 + n + i: 2 + i for i in range(n)},
        scratch_shapes=[dma((3 * n,))] * 4 if n else [],
        compiler_params=_params(("arbitrary", "arbitrary")),
    )(qkv, qkv, qkv, *shards, *slots)


def attn_bwd_pairs(qkv, ltot, do, partials=(), *, B, S, name):
    D = qkv.shape[1] // 3
    H = D // HEAD_DIM
    T = ATT_BLOCK
    W = 2 * T
    nq = S // T
    scale = HEAD_DIM ** -0.5
    G = _tile(H, (ATT_HEADS_BWD, 2))
    P = G // 2
    LW = 2 * HEAD_DIM * P
    nsec = D // LW
    n = len(partials)
    n_steps = B * nsec

    def body(*refs):
        q_ref, k_ref, v_ref, l_ref, do_ref = refs[:5]
        d_ref = refs[5 + n]
        dk_acc, dv_acc = refs[6 + 2 * n:8 + 2 * n]
        step_id = pl.program_id(0) * nsec + pl.program_id(1)
        if n:
            scatter = _ChipScatter(refs[5:5 + n], refs[6 + n:6 + 2 * n], *refs[8 + 2 * n:])
            pl.when(step_id == 0)(scatter.send)
        _, upto2, before2, qrow, kcol, eye = _wide_consts(T, W)
        blk = lax.broadcasted_iota(jnp.int32, (nq, T), 0)
        first_q = lax.broadcasted_iota(jnp.int32, (T, 2 * HEAD_DIM), 1) < HEAD_DIM
        first_k = lax.broadcasted_iota(jnp.int32, (W, 2 * HEAD_DIM), 1) < HEAD_DIM
        dk_acc[...] = jnp.zeros_like(dk_acc)
        dv_acc[...] = jnp.zeros_like(dv_acc)

        def lanes(j):
            return slice(j * 2 * HEAD_DIM, (j + 1) * 2 * HEAD_DIM)

        def step(qms, doms, tots, k0, st, mask):
            dqs, runs_l, runs_d = st
            sc = []
            for g in range(G):
                z, lb, l1, e = _att_scores(qms[g], k_ref[pl.ds(k0, W), lanes(g // 2)], mask, prescaled=True)
                inv = 1.0 / (1.0 + e)
                small = e * inv
                pos = z >= 0.0
                beta = jnp.where(pos, inv, small)
                omb = jnp.where(pos, small, inv)
                if mask is not None:
                    beta = jnp.where(mask, beta, 0.0)
                dw = _nt(doms[g], v_ref[pl.ds(k0, W), lanes(g // 2)])
                sc.append((lb, jnp.concatenate(_split2(l1), axis=1), jnp.sum(l1, axis=1, keepdims=True), dw, beta, omb))
            pre = jnp.dot(jnp.concatenate([s[1] for s in sc], axis=0), upto2, preferred_element_type=F32)
            dlws = []
            for j in range(P):
                dv = None
                for h in range(2):
                    g = 2 * j + h
                    w = jnp.exp(sc[g][0] + (tots[g] - (pre[g * T:(g + 1) * T] + runs_l[g])))
                    t = _tn(w.astype(BF16), doms[g])
                    dv = t if dv is None else dv + t
                    dlws.append(sc[g][3] * w)
                dv_acc[j, pl.ds(k0, W), :] += dv
            pre_d = jnp.dot(jnp.concatenate([jnp.concatenate(_split2(d), axis=1) for d in dlws], axis=0), before2,
                            preferred_element_type=F32)
            new_dqs, new_l, new_d = [], [], []
            for j in range(P):
                kms = _pair_masks(k_ref[pl.ds(k0, W), lanes(j)], first_k)
                dq, dk = dqs[j], None
                for h in range(2):
                    g = 2 * j + h
                    _, _, rowsum, _, beta, omb = sc[g]
                    dl1 = pre_d[g * T:(g + 1) * T] + runs_d[g]
                    dz = (dlws[g] * omb - dl1 * beta).astype(BF16)
                    dq = dq + jnp.dot(dz, kms[h], preferred_element_type=F32)
                    t = _tn(dz, qms[g])
                    dk = t if dk is None else dk + t
                    new_l.append(runs_l[g] + rowsum)
                    new_d.append(runs_d[g] + jnp.sum(dlws[g], axis=1, keepdims=True))
                dk_acc[j, pl.ds(k0, W), :] += dk
                new_dqs.append(dq)
            return tuple(new_dqs), tuple(new_l), tuple(new_d)

        def qblock(i, carry0):
            q0 = pl.multiple_of(i * T, T)
            qms, doms = [], []
            for j in range(P):
                qms.extend(_pair_masks(q_ref[pl.ds(q0, T), lanes(j)] * scale, first_q))
                doms.extend(_pair_masks(do_ref[pl.ds(q0, T), lanes(j)], first_q))
            tots = [_row_to_col(jnp.sum(jnp.where(blk == i, l_ref[g], 0.0), axis=0, keepdims=True), eye)
                    for g in range(G)]
            z1 = tuple(jnp.zeros((T, 1), F32) for _ in range(G))
            st = (tuple(jnp.zeros((T, 2 * HEAD_DIM), F32) for _ in range(P)), z1, z1)

            def kblock(j, st):
                return step(qms, doms, tots, pl.multiple_of(j * W, W), st, None)

            half = jnp.right_shift(i, 1)
            st = lax.fori_loop(0, half, kblock, st)
            k_last = pl.multiple_of(half * W, W)
            dqs, _, _ = step(qms, doms, tots, k_last, st, (k_last + kcol) < (q0 + qrow))
            for j in range(P):
                d_ref[0, pl.ds(q0, T), lanes(j)] = (dqs[j] * scale).astype(d_ref.dtype)
            return carry0

        lax.fori_loop(0, nq, qblock, 0)
        for j in range(P):
            d_ref[1, :, lanes(j)] = dk_acc[j].astype(d_ref.dtype)
            d_ref[2, :, lanes(j)] = dv_acc[j].astype(d_ref.dtype)
        if n:
            pl.when(step_id == n_steps - 1)(scatter.finish)

    def cols(section):
        return pl.BlockSpec((S, LW), lambda b, h: (b, section * nsec + h))

    lspec = pl.BlockSpec((None, G, nq, T), lambda b, h: (b, h, 0, 0))
    dma = pltpu.SemaphoreType.DMA
    return pl.pallas_call(
        body, name=name,
        out_shape=[jax.ShapeDtypeStruct((3, B * S, D), BF16)]
        + [jax.ShapeDtypeStruct((N_CHIPS - 1,) + a.shape[1:], a.dtype) for a in partials],
        grid=(B, nsec),
        in_specs=[cols(0), cols(1), cols(2), lspec, cols(0)] + _any_specs(n),
        out_specs=[pl.BlockSpec((3, S, LW), lambda b, h: (0, b, h))] + _any_specs(n),
        scratch_shapes=[pltpu.VMEM((P, S, 2 * HEAD_DIM), F32), pltpu.VMEM((P, S, 2 * HEAD_DIM), F32)]
        + ([dma((3 * n,)), dma((3 * n,))] if n else []),
        compiler_params=_params(("arbitrary", "arbitrary")),
    )(qkv, qkv, qkv, ltot, do, *partials)


def _cmul(ar, ai, br, bi):
    return ar * br - ai * bi, ar * bi + ai * br


def _cpow(lr, li, n):
    rr, ri = None, None
    br, bi = lr, li
    while n:
        if n & 1:
            rr, ri = (br, bi) if rr is None else _cmul(rr, ri, br, bi)
        n >>= 1
        if n:
            br, bi = _cmul(br, bi, br, bi)
    return rr, ri


def _ssm_scan(sr, si, lr, li, n_steps, reverse):
    W = sr.shape[1]
    R = SEGMENTS
    lim = -li if reverse else li
    zero = jnp.zeros((R, W), F32)

    def row(k):
        i = (n_steps - 1 - k) if reverse else k
        return pl.multiple_of(i * R, R)

    def local(k, st):
        cr, ci = st
        r0 = row(k)
        pr, pi = _cmul(lr, lim, cr, ci)
        nr = pr + sr[pl.ds(r0, R), :]
        ni = pi + si[pl.ds(r0, R), :]
        sr[pl.ds(r0, R), :] = nr
        si[pl.ds(r0, R), :] = ni
        return nr, ni

    er, ei = lax.fori_loop(0, n_steps, local, (zero, zero), unroll=SCAN_UNROLL)
    lnr, lni = _cpow(lr, lim, n_steps)
    rows = lax.broadcasted_iota(jnp.int32, (R, W), 0)
    cr, ci = zero, zero
    for step in range(1, R):
        tr, ti = _cmul(lnr, lni, cr, ci)
        tr, ti = tr + er, ti + ei
        if reverse:
            seg = R - 1 - step
            tr, ti = pltpu.roll(tr, R - 1, 0), pltpu.roll(ti, R - 1, 0)
        else:
            seg = step
            tr, ti = pltpu.roll(tr, 1, 0), pltpu.roll(ti, 1, 0)
        cr = jnp.where(rows == seg, tr, cr)
        ci = jnp.where(rows == seg, ti, ci)

    def fix(k, st):
        pr, pi = st
        r0 = row(k)
        ar, ai = _cmul(pr, pi, cr, ci)
        sr[pl.ds(r0, R), :] += ar
        si[pl.ds(r0, R), :] += ai
        return _cmul(lr, lim, pr, pi)

    lax.fori_loop(0, n_steps, fix, (lr, lim), unroll=SCAN_UNROLL)
    return cr, ci


def _ssm_specs(S, W):
    CH = GROUPS_PER_BLOCK * SSM_GROUP
    return dict(
        rows=pl.BlockSpec((S, CH), lambda b, j: (b, j)),
        b=pl.BlockSpec((None, CH, W), lambda b, j: (j, 0, 0)),
        c=pl.BlockSpec((None, W, CH), lambda b, j: (j, 0, 0)),
        lam=pl.BlockSpec((None, SEGMENTS, W), lambda b, j: (j, 0, 0)),
        vec=pl.BlockSpec((1, CH), lambda b, j: (0, j)),
    )


def ssm_fwd(u, bre, bim, cre, cim, lr8, li8, dsk, *, B, S, name):
    D = u.shape[1]
    J, CH, W = bre.shape
    n_steps = S // SEGMENTS
    sp = _ssm_specs(S, W)

    def body(u_ref, bre_ref, bim_ref, cre_ref, cim_ref, lr_ref, li_ref, dsk_ref, y_ref, sr, si):
        u = u_ref[...]
        ub = u.astype(BF16)
        sr[...] = jnp.dot(ub, bre_ref[...], preferred_element_type=F32)
        si[...] = jnp.dot(ub, bim_ref[...], preferred_element_type=F32)
        _ssm_scan(sr, si, lr_ref[...], li_ref[...], n_steps, False)
        y = jnp.dot(sr[...].astype(BF16), cre_ref[...], preferred_element_type=F32)
        y = y - jnp.dot(si[...].astype(BF16), cim_ref[...], preferred_element_type=F32)
        y_ref[...] = y + dsk_ref[...] * u

    return pl.pallas_call(
        body, name=name, out_shape=jax.ShapeDtypeStruct((B * S, D), F32), grid=(B, J),
        in_specs=[sp["rows"], sp["b"], sp["b"], sp["c"], sp["c"], sp["lam"], sp["lam"], sp["vec"]],
        out_specs=sp["rows"],
        scratch_shapes=[pltpu.VMEM((S, W), F32), pltpu.VMEM((S, W), F32)],
        compiler_params=_params(("parallel", "parallel")),
    )(u, bre, bim, cre, cim, lr8, li8, dsk)


def ssm_bwd(u, dy, bre, bim, cre, cim, lr8, li8, dsk, *, B, S, name):
    D = u.shape[1]
    J, CH, W = bre.shape
    n_steps = S // SEGMENTS
    sp = _ssm_specs(S, W)

    def body(u_ref, dy_ref, bre_ref, bim_ref, cre_ref, cim_ref, lr_ref, li_ref, dsk_ref,
             du_ref, dbre_ref, dbim_ref, dcre_ref, dcim_ref, dlr_ref, dli_ref, ddsk_ref, sr, si, ar, ai):
        u = u_ref[...]
        dy = dy_ref[...]
        ub = u.astype(BF16)
        dyb = dy.astype(BF16)
        lr, li = lr_ref[...], li_ref[...]
        sr[...] = jnp.dot(ub, bre_ref[...], preferred_element_type=F32)
        si[...] = jnp.dot(ub, bim_ref[...], preferred_element_type=F32)
        cr, ci = _ssm_scan(sr, si, lr, li, n_steps, False)
        ar[...] = _nt(dyb, cre_ref[...])
        ai[...] = -_nt(dyb, cim_ref[...])
        _ssm_scan(ar, ai, lr, li, n_steps, True)

        def dlam(k, st):
            dr, di = st
            r0 = pl.multiple_of((k + 1) * SEGMENTS, SEGMENTS)
            p0 = pl.multiple_of(k * SEGMENTS, SEGMENTS)
            pr, pi = sr[pl.ds(p0, SEGMENTS), :], si[pl.ds(p0, SEGMENTS), :]
            xr, xi = ar[pl.ds(r0, SEGMENTS), :], ai[pl.ds(r0, SEGMENTS), :]
            return dr + pr * xr + pi * xi, di + pr * xi - pi * xr

        xr, xi = ar[0:SEGMENTS, :], ai[0:SEGMENTS, :]
        dr, di = lax.fori_loop(0, n_steps - 1, dlam, (cr * xr + ci * xi, cr * xi - ci * xr), unroll=SCAN_UNROLL)
        dlr_ref[...] = dr
        dli_ref[...] = di
        arb = ar[...].astype(BF16)
        aib = ai[...].astype(BF16)
        du_ref[...] = _nt(arb, bre_ref[...]) + _nt(aib, bim_ref[...]) + dsk_ref[...] * dy
        dbre_ref[...] = _tn(ub, arb)
        dbim_ref[...] = _tn(ub, aib)
        dcre_ref[...] = _tn(sr[...].astype(BF16), dyb)
        dcim_ref[...] = -_tn(si[...].astype(BF16), dyb)
        ddsk_ref[...] = jnp.sum(dy * u, axis=0, keepdims=True)

    def per(shape):
        return pl.BlockSpec((None, None) + shape, lambda b, j: (b, j, 0, 0))

    return pl.pallas_call(
        body, name=name,
        out_shape=[jax.ShapeDtypeStruct((B * S, D), F32),
                   jax.ShapeDtypeStruct((B, J, CH, W), F32), jax.ShapeDtypeStruct((B, J, CH, W), F32),
                   jax.ShapeDtypeStruct((B, J, W, CH), F32), jax.ShapeDtypeStruct((B, J, W, CH), F32),
                   jax.ShapeDtypeStruct((B, J, SEGMENTS, W), F32), jax.ShapeDtypeStruct((B, J, SEGMENTS, W), F32),
                   jax.ShapeDtypeStruct((B, J, 1, CH), F32)],
        grid=(B, J),
        in_specs=[sp["rows"], sp["rows"], sp["b"], sp["b"], sp["c"], sp["c"], sp["lam"], sp["lam"], sp["vec"]],
        out_specs=[sp["rows"], per((CH, W)), per((CH, W)), per((W, CH)), per((W, CH)), per((SEGMENTS, W)),
                   per((SEGMENTS, W)),
                   per((1, CH))],
        scratch_shapes=[pltpu.VMEM((S, W), F32)] * 4,
        compiler_params=_params(("parallel", "parallel")),
    )(u, dy, bre, bim, cre, cim, lr8, li8, dsk)


def _ssm_discretize(a_re, a_im, log_dt, b_re, b_im):
    dt = jnp.exp(log_dt)[:, None]
    er = jnp.exp(a_re * dt)
    lr = er * jnp.cos(a_im * dt)
    li = er * jnp.sin(a_im * dt)
    den = a_re * a_re + a_im * a_im
    fr = ((lr - 1.0) * a_re + li * a_im) / den
    fi = (li * a_re - (lr - 1.0) * a_im) / den
    bbr = fr[..., None] * b_re - fi[..., None] * b_im
    bbi = fr[..., None] * b_im + fi[..., None] * b_re
    return lr, li, bbr, bbi


def _block_diag_in(m):
    G, P, H = m.shape
    J = G // GROUPS_PER_BLOCK
    m = m.reshape(J, GROUPS_PER_BLOCK, P, H).transpose(0, 1, 3, 2)
    eye = jnp.eye(GROUPS_PER_BLOCK, dtype=m.dtype)
    out = m[:, :, :, None, :] * eye[None, :, None, :, None]
    return out.reshape(J, GROUPS_PER_BLOCK * H, GROUPS_PER_BLOCK * P)


def _block_diag_in_grad(d, G, P, H):
    J = G // GROUPS_PER_BLOCK
    d = d.reshape(J, GROUPS_PER_BLOCK, H, GROUPS_PER_BLOCK, P)
    idx = jnp.arange(GROUPS_PER_BLOCK)
    d = d[:, idx, :, idx, :]
    return d.transpose(1, 0, 3, 2).reshape(G, P, H)


def _block_diag_out(m):
    G, H, P = m.shape
    J = G // GROUPS_PER_BLOCK
    m = m.reshape(J, GROUPS_PER_BLOCK, H, P).transpose(0, 1, 3, 2)
    eye = jnp.eye(GROUPS_PER_BLOCK, dtype=m.dtype)
    out = m[:, :, :, None, :] * eye[None, :, None, :, None]
    return out.reshape(J, GROUPS_PER_BLOCK * P, GROUPS_PER_BLOCK * H)


def _block_diag_out_grad(d, G, H, P):
    J = G // GROUPS_PER_BLOCK
    d = d.reshape(J, GROUPS_PER_BLOCK, P, GROUPS_PER_BLOCK, H)
    idx = jnp.arange(GROUPS_PER_BLOCK)
    d = d[:, idx, :, idx, :]
    return d.transpose(1, 0, 3, 2).reshape(G, H, P)


def _interleave(a, B, S):
    L = S // SEGMENTS
    return a.reshape(B, SEGMENTS, L, a.shape[-1]).transpose(0, 2, 1, 3).reshape(B * S, a.shape[-1])


def _deinterleave(a, B, S):
    L = S // SEGMENTS
    return a.reshape(B, L, SEGMENTS, a.shape[-1]).transpose(0, 2, 1, 3).reshape(B * S, a.shape[-1])


def _adamw_math(w, g, m, v):
    m = ADAM_B1 * m + (1.0 - ADAM_B1) * g
    v = ADAM_B2 * v + (1.0 - ADAM_B2) * (g * g)
    m_hat = m / (1.0 - ADAM_B1 ** ADAM_STEP)
    v_hat = v / (1.0 - ADAM_B2 ** ADAM_STEP)
    delta = -ADAM_LR * (m_hat / (jnp.sqrt(v_hat) + ADAM_EPS) + ADAM_WD * w)
    return delta, m, v


def adamw(w, g, m, v, *, name):
    R, C = w.shape
    tr = _tile(R, (max(8, (1 << 18) // C // 8 * 8), 256, 128, 64, 32, 16, 8))

    def body(w_ref, g_ref, m_ref, v_ref, d_ref, nm_ref, nv_ref):
        d, nm, nv = _adamw_math(w_ref[...], g_ref[...], m_ref[...], v_ref[...])
        d_ref[...] = d
        nm_ref[...] = nm
        nv_ref[...] = nv

    spec = pl.BlockSpec((tr, C), lambda i: (i, 0))
    shp = jax.ShapeDtypeStruct((R, C), F32)
    return pl.pallas_call(
        body, name=name, out_shape=[shp, shp, shp], grid=(R // tr,), in_specs=[spec] * 4, out_specs=[spec] * 3,
        compiler_params=_params(("parallel",)),
    )(w, g, m, v)


def sum_leading(a, *, name, out_dtype=F32):
    n, R, C = a.shape
    tr = _tile(R, (256, 128, 64, 32, 16, 8))

    def body(a_ref, o_ref):
        acc = a_ref[0].astype(F32)
        for i in range(1, n):
            acc = acc + a_ref[i].astype(F32)
        o_ref[...] = acc.astype(o_ref.dtype)

    return pl.pallas_call(
        body, name=name, out_shape=jax.ShapeDtypeStruct((R, C), out_dtype), grid=(R // tr,),
        in_specs=[pl.BlockSpec((n, tr, C), lambda i: (0, i, 0))], out_specs=pl.BlockSpec((tr, C), lambda i: (i, 0)),
        compiler_params=_params(("parallel",)),
    )(a)


def _any_specs(n):
    return [pl.BlockSpec(memory_space=pl.ANY) for _ in range(n)]


def _coords():
    return lax.axis_index("x"), lax.axis_index("y"), lax.axis_index("c")


def _flip(v, bit):
    return (v + bit) % 2


def all_gather8(a, *, name):
    shape = a.shape

    def body(a_ref, o_ref, send_sems, recv_sems, local_sem):
        x, y, c = _coords()
        me = 4 * x + 2 * y + c
        mine = pltpu.make_async_copy(a_ref, o_ref.at[me], local_sem)
        mine.start()
        sends = []
        for k in range(1, N_DEV):
            peer = (_flip(x, (k >> 2) & 1), _flip(y, (k >> 1) & 1), _flip(c, k & 1))
            cp = pltpu.make_async_remote_copy(a_ref, o_ref.at[me], send_sems.at[k - 1], recv_sems.at[k - 1],
                                              device_id=peer, device_id_type=MESH)
            cp.start()
            sends.append(cp)
        for k in range(1, N_DEV):
            px, py, pc = _flip(x, (k >> 2) & 1), _flip(y, (k >> 1) & 1), _flip(c, k & 1)
            src = 4 * px + 2 * py + pc
            pltpu.make_async_remote_copy(a_ref, o_ref.at[src], send_sems.at[k - 1], recv_sems.at[k - 1],
                                         device_id=(px, py, pc), device_id_type=MESH).wait_recv()
        for cp in sends:
            cp.wait_send()
        mine.wait()

    return pl.pallas_call(
        body, name=name, out_shape=jax.ShapeDtypeStruct((N_DEV,) + shape, a.dtype),
        in_specs=_any_specs(1), out_specs=pl.BlockSpec(memory_space=pl.ANY),
        scratch_shapes=[pltpu.SemaphoreType.DMA((N_DEV - 1,)), pltpu.SemaphoreType.DMA((N_DEV - 1,)),
                        pltpu.SemaphoreType.DMA(())],
    )(a)


def _chip_of(x, y, p):
    px, py = _flip(x, (p >> 1) & 1), _flip(y, p & 1)
    return 2 * px + py, px, py


class _ShardGather:
    def __init__(self, ins, outs, ici_send, ici_recv, d2d_send, d2d_recv):
        self.ins, self.outs = ins, outs
        self.sems = ici_send, ici_recv, d2d_send, d2d_recv
        self.x, self.y, self.c = _coords()
        self.me = 2 * self.x + self.y

    def _ici(self, i, p, slot):
        half = self.ins[i].shape[0] // 2
        rows = pl.ds(self.c * half, half)
        _, px, py = _chip_of(self.x, self.y, p)
        s = i * 3 + p - 1
        return pltpu.make_async_remote_copy(self.ins[i].at[rows], self.outs[i].at[slot, rows], self.sems[0].at[s],
                                            self.sems[1].at[s], device_id=(px, py, self.c), device_id_type=MESH)

    def _d2d(self, i, p, mine):
        half = self.ins[i].shape[0] // 2
        rows = pl.ds((self.c if mine else 1 - self.c) * half, half)
        src, _, _ = _chip_of(self.x, self.y, p)
        s = i * 3 + p - 1
        part = self.outs[i].at[src, rows]
        return pltpu.make_async_remote_copy(part, part, self.sems[2].at[s], self.sems[3].at[s],
                                            device_id=(self.x, self.y, 1 - self.c), device_id_type=MESH)

    def _each(self):
        return [(i, p) for i in range(len(self.ins)) for p in range(1, N_CHIPS)]

    def send(self):
        for i, p in self._each():
            self._ici(i, p, self.me).start()

    def forward(self):
        for i, p in self._each():
            self._ici(i, p, _chip_of(self.x, self.y, p)[0]).wait_recv()
            self._d2d(i, p, True).start()

    def finish(self):
        for i, p in self._each():
            self._d2d(i, p, False).wait_recv()
        for i, p in self._each():
            self._ici(i, p, self.me).wait_send()
            self._d2d(i, p, True).wait_send()


def gather_chip_shards(arrs, remote, *, name):
    n = len(arrs)
    far = [i for i in range(n) if remote[i]]

    def body(*refs):
        ins, outs = refs[:n], refs[n:2 * n]
        ici_send, ici_recv, d2d_send, d2d_recv, local_sems = refs[2 * n:2 * n + 5]
        bufs = refs[2 * n + 5:]
        me = 2 * lax.axis_index("x") + lax.axis_index("y")
        loads = []
        for i in range(n):
            cp = pltpu.make_async_copy(ins[i], bufs[i], local_sems.at[i])
            cp.start()
            loads.append(cp)
        gather = _ShardGather([ins[i] for i in far], [outs[i] for i in far], ici_send, ici_recv, d2d_send, d2d_recv)
        gather.send()
        stores = []
        for i in range(n):
            loads[i].wait()
            cp = pltpu.make_async_copy(bufs[i], outs[i].at[me], local_sems.at[i])
            cp.start()
            stores.append(cp)
        gather.forward()
        gather.finish()
        for cp in stores:
            cp.wait()

    dma = pltpu.SemaphoreType.DMA
    m = 3 * len(far)
    return pl.pallas_call(
        body, name=name,
        out_shape=[jax.ShapeDtypeStruct((N_CHIPS,) + a.shape, a.dtype) for a in arrs],
        in_specs=_any_specs(n), out_specs=_any_specs(n),
        scratch_shapes=[dma((m,)), dma((m,)), dma((m,)), dma((m,)), dma((n,))]
        + [pltpu.VMEM(a.shape, a.dtype) for a in arrs],
        compiler_params=pltpu.CompilerParams(vmem_limit_bytes=V7X_VMEM_LIMIT),
    )(*arrs)


def swap_halves(arrs, *, name):
    n = len(arrs)

    def body(*refs):
        ins, outs = refs[:n], refs[n:2 * n]
        send_sems, recv_sems = refs[2 * n:]
        x, y, c = _coords()
        cps = []
        for i in range(n):
            half = ins[i].shape[1] // 2
            cp = pltpu.make_async_remote_copy(ins[i].at[:, pl.ds((1 - c) * half, half)], outs[i], send_sems.at[i],
                                              recv_sems.at[i], device_id=(x, y, 1 - c), device_id_type=MESH)
            cp.start()
            cps.append(cp)
        for cp in cps:
            cp.wait()

    dma = pltpu.SemaphoreType.DMA
    return pl.pallas_call(
        body, name=name,
        out_shape=[jax.ShapeDtypeStruct((N_CHIPS, a.shape[1] // 2, a.shape[2]), a.dtype) for a in arrs],
        in_specs=_any_specs(n), out_specs=_any_specs(n), scratch_shapes=[dma((n,)), dma((n,))],
    )(*arrs)


def add_half(g, other, c_idx, *, name, out_dtype):
    _, R, C = g.shape
    half = R // 2
    tr = _tile(half, (256, 128, 64, 32, 16, 8))
    nt = half // tr

    def body(c_ref, g_ref, o_ref, out_ref):
        out_ref[...] = (g_ref[...] + o_ref[...]).astype(out_ref.dtype)

    return pl.pallas_call(
        body, name=name, out_shape=jax.ShapeDtypeStruct((N_CHIPS, half, C), out_dtype),
        grid_spec=pltpu.PrefetchScalarGridSpec(
            num_scalar_prefetch=1, grid=(N_CHIPS, nt),
            in_specs=[pl.BlockSpec((None, tr, C), lambda r, t, c_ref: (r, c_ref[0] * nt + t, 0)),
                      pl.BlockSpec((None, tr, C), lambda r, t, c_ref: (r, t, 0))],
            out_specs=pl.BlockSpec((None, tr, C), lambda r, t, c_ref: (r, t, 0))),
        compiler_params=_params(("parallel", "parallel")),
    )(c_idx, g, other)


class _ChipScatter:
    def __init__(self, ins, outs, send_sems, recv_sems):
        self.ins, self.outs, self.send_sems, self.recv_sems = ins, outs, send_sems, recv_sems
        self.x, self.y, self.c = _coords()

    def _copies(self):
        for i in range(len(self.ins)):
            for p in range(1, N_CHIPS):
                dst, px, py = _chip_of(self.x, self.y, p)
                s = i * 3 + p - 1
                yield pltpu.make_async_remote_copy(self.ins[i].at[dst], self.outs[i].at[p - 1], self.send_sems.at[s],
                                                   self.recv_sems.at[s], device_id=(px, py, self.c), device_id_type=MESH)

    def send(self):
        for cp in self._copies():
            cp.start()

    def finish(self):
        for cp in self._copies():
            cp.wait()


def scatter_to_chips(arrs, *, name):
    n = len(arrs)

    def body(*refs):
        scatter = _ChipScatter(refs[:n], refs[n:2 * n], *refs[2 * n:])
        scatter.send()
        scatter.finish()

    dma = pltpu.SemaphoreType.DMA
    return pl.pallas_call(
        body, name=name,
        out_shape=[jax.ShapeDtypeStruct((N_CHIPS - 1,) + a.shape[1:], a.dtype) for a in arrs],
        in_specs=_any_specs(n), out_specs=_any_specs(n), scratch_shapes=[dma((3 * n,)), dma((3 * n,))],
    )(*arrs)


def add_chips(h, got, r_idx, *, name):
    _, R, C = h.shape
    tr = _tile(R, (256, 128, 64, 32, 16, 8))

    def body(r_ref, h_ref, g_ref, out_ref):
        acc = h_ref[...].astype(F32)
        for p in range(N_CHIPS - 1):
            acc = acc + g_ref[p].astype(F32)
        out_ref[...] = acc

    return pl.pallas_call(
        body, name=name, out_shape=jax.ShapeDtypeStruct((R, C), F32),
        grid_spec=pltpu.PrefetchScalarGridSpec(
            num_scalar_prefetch=1, grid=(R // tr,),
            in_specs=[pl.BlockSpec((None, tr, C), lambda t, r_ref: (r_ref[0], t, 0)),
                      pl.BlockSpec((N_CHIPS - 1, tr, C), lambda t, r_ref: (0, t, 0))],
            out_specs=pl.BlockSpec((tr, C), lambda t, r_ref: (t, 0))),
        compiler_params=_params(("parallel",)),
    )(r_idx, h, got)


def join_halves(arrs, *, name):
    n = len(arrs)

    def body(*refs):
        ins, outs = refs[:n], refs[n:2 * n]
        send_sems, recv_sems, local_sems = refs[2 * n:2 * n + 3]
        bufs = refs[2 * n + 3:]
        x, y, c = _coords()
        loads, sends, stores = [], [], []
        for i in range(n):
            cp = pltpu.make_async_copy(ins[i], bufs[i], local_sems.at[i])
            cp.start()
            loads.append(cp)
        for i in range(n):
            half = ins[i].shape[0]
            cp = pltpu.make_async_remote_copy(ins[i], outs[i].at[pl.ds(c * half, half)], send_sems.at[i], recv_sems.at[i],
                                              device_id=(x, y, 1 - c), device_id_type=MESH)
            cp.start()
            sends.append(cp)
        for i in range(n):
            half = ins[i].shape[0]
            loads[i].wait()
            cp = pltpu.make_async_copy(bufs[i], outs[i].at[pl.ds(c * half, half)], local_sems.at[i])
            cp.start()
            stores.append(cp)
        for i in range(n):
            half = ins[i].shape[0]
            pltpu.make_async_remote_copy(ins[i], outs[i].at[pl.ds((1 - c) * half, half)], send_sems.at[i],
                                         recv_sems.at[i], device_id=(x, y, 1 - c), device_id_type=MESH).wait_recv()
        for i in range(n):
            sends[i].wait_send()
            stores[i].wait()

    dma = pltpu.SemaphoreType.DMA
    return pl.pallas_call(
        body, name=name,
        out_shape=[jax.ShapeDtypeStruct((2 * a.shape[0], a.shape[1]), a.dtype) for a in arrs],
        in_specs=_any_specs(n), out_specs=_any_specs(n),
        scratch_shapes=[dma((n,)), dma((n,)), dma((n,))] + [pltpu.VMEM(a.shape, a.dtype) for a in arrs],
        compiler_params=pltpu.CompilerParams(vmem_limit_bytes=V7X_VMEM_LIMIT),
    )(*arrs)


def pair_sums(grads, wire_dtypes, tag):
    c_idx = jnp.reshape(lax.axis_index("c"), (1,)).astype(jnp.int32)
    theirs = swap_halves(grads, name=f"rs_swap_halves_{tag}")
    return [add_half(g, o, c_idx, name=f"rs_add_half_{tag}{i}", out_dtype=wire_dtypes[i])
            for i, (g, o) in enumerate(zip(grads, theirs))]


def chip_sums(pairs, gots, tag):
    r_idx = jnp.reshape(2 * lax.axis_index("x") + lax.axis_index("y"), (1,)).astype(jnp.int32)
    return [add_chips(h, g, r_idx, name=f"rs_add_chips_{tag}{i}") for i, (h, g) in enumerate(zip(pairs, gots))]


def _to_heads(t, B, S):
    return t.reshape(B, S, -1, HEAD_DIM).transpose(0, 2, 1, 3)


def _from_heads(t, B, S):
    return t.transpose(0, 2, 1, 3).reshape(B * S, -1)


def _chip_major(w, axis):
    n = w.shape[axis] // N_CHIPS
    parts = w.reshape(w.shape[:axis] + (N_CHIPS, n) + w.shape[axis + 1:])
    return jnp.moveaxis(parts, axis, 0)


def _from_chip_major(g, axis):
    g = jnp.moveaxis(g, 0, axis)
    return g.reshape(g.shape[:axis] + (g.shape[axis] * g.shape[axis + 1],) + g.shape[axis + 2:])


def kernel(x, c, norm_mix, norm_ffn, w_mod, b_mod, w_qkv, w_o_attn, w_in_ssm, a_re, a_im, log_dt, b_re, b_im, c_re, c_im, d_skip, w_glu, b_glu, w_o_ssm, w_up, conv_w, conv_b, w_down, norm_out, w_fin, b_fin, loss_target, m_norm_mix, m_norm_ffn, m_w_mod, m_b_mod, m_w_qkv, m_w_o_attn, m_w_in_ssm, m_a_re, m_a_im, m_log_dt, m_b_re, m_b_im, m_c_re, m_c_im, m_d_skip, m_w_glu, m_b_glu, m_w_o_ssm, m_w_up, m_conv_w, m_conv_b, m_w_down, m_norm_out, m_w_fin, m_b_fin, v_norm_mix, v_norm_ffn, v_w_mod, v_b_mod, v_w_qkv, v_w_o_attn, v_w_in_ssm, v_a_re, v_a_im, v_log_dt, v_b_re, v_b_im, v_c_re, v_c_im, v_d_skip, v_w_glu, v_b_glu, v_w_o_ssm, v_w_up, v_conv_w, v_conv_b, v_w_down, v_norm_out, v_w_fin, v_b_fin):
    B, S, D = x.shape
    T = B * S
    F2 = conv_b.shape[1]
    F = F2 // 2
    G, P = a_re.shape[1], a_re.shape[2]
    H = b_re.shape[3]
    mx, my, mc = _coords()
    chip = 2 * mx + my
    dev = 4 * mx + 2 * my + mc
    BG = N_DEV * B
    mod_w = w_mod.shape[2]
    fin_w = w_fin.shape[1]

    c_all = all_gather8(c, name="gather_c").reshape(BG, D)
    c_act = silu_rows(c_all, name="silu_c")
    b_mod_mine = lax.dynamic_slice(b_mod, (0, chip * mod_w), (2, mod_w))
    b_fin_mine = lax.dynamic_slice(b_fin, (chip * fin_w,), (fin_w,))
    cond = [matmul(c_act, w_mod[i], bias=b_mod_mine[i], name=f"mod_proj_{i}") for i in range(2)]
    cond.append(matmul(c_act, w_fin, bias=b_fin_mine, name="fin_proj"))
    cond_all = all_gather8(jnp.concatenate(cond, axis=1), name="gather_cond")
    cond_all = cond_all[::2]
    cond_rows = lax.dynamic_slice(cond_all, (0, dev * B, 0), (N_CHIPS, B, cond_all.shape[2]))
    mods = []
    for i in range(2):
        full = cond_rows[:, :, i * mod_w:(i + 1) * mod_w].transpose(1, 0, 2).reshape(B, N_CHIPS * mod_w)
        mods.append([full[:, k * D:(k + 1) * D] for k in range(6)])
    fin = cond_rows[:, :, 2 * mod_w:].transpose(1, 0, 2).reshape(B, N_CHIPS * fin_w)
    sh_f, sc_f = fin[:, :D], fin[:, D:]

    rows1024 = jnp.concatenate([w_o_attn[0], w_in_ssm[0], w_glu[0], w_o_ssm[0], w_down.reshape(-1, D)], axis=0)
    shards = [w_qkv[0].astype(BF16), rows1024.astype(BF16), w_up[0].astype(BF16), w_up[1].astype(BF16)]
    W_qkv, *own_slots = gather_chip_shards(shards, [True, False, False, False], name="gather_weights")
    Dq = D // N_CHIPS
    Fq = F // N_CHIPS
    small =jnp.concatenate([conv_w.reshape(6, -1), jnp.pad(d_skip, ((0, 0), (0, conv_w.shape[2] - Dq))),
                             jnp.pad(b_glu, ((0, 0), (0, conv_w.shape[2] - Dq)))], axis=0)
    small_all = all_gather8(small, name="gather_small")[::2]
    conv_w_full = _from_chip_major(small_all[:, :6].reshape(N_CHIPS, 2, 3, -1), 2)
    d_skip_full = small_all[:, 6, :Dq].reshape(1, D)
    b_glu_full = small_all[:, 7, :Dq].reshape(D)

    x0 = x.reshape(T, D)
    tgt = loss_target.reshape(T, D)

    def ffn_fwd(xin, i):
        sh2, sc2, g2 = mods[i][3], mods[i][4], mods[i][5]
        h2 = norm_mod_fwd(xin, norm_ffn[i], sh2, sc2, B=B, S=S, name=f"ffn_norm_{i}")
        up = matmul(h2, W_up[i], b_chips=True, name=f"ffn_up_{i}")
        act = conv_gate_fwd(up, conv_w_full[i], conv_b[i:i + 1], B=B, S=S, name=f"ffn_conv_{i}")
        yf = matmul(act, W_down[i], name=f"ffn_down_{i}")
        xout = gate_res_fwd(xin, yf, g2, B=B, S=S, name=f"ffn_res_{i}")
        return xout, (xin, h2, up, act, yf)

    sh1, sc1, g1 = mods[0][0], mods[0][1], mods[0][2]
    h1a = norm_mod_fwd(x0, norm_mix[0], sh1, sc1, B=B, S=S, name="att_norm")
    qkv = matmul(h1a, W_qkv, out_dtype=BF16, b_chips=True, name="att_qkv")
    o2, ltot, g_rows, W_up0, W_up1 = attn_fwd_pairs(qkv, shards[1:], own_slots, B=B, S=S, name="att_fwd")
    W_up = [W_up0, W_up1]
    W_o_attn = g_rows[:, 0 * Dq:1 * Dq].reshape(D, D)
    W_in = g_rows[:, 1 * Dq:2 * Dq].reshape(D, D)
    W_glu = g_rows[:, 2 * Dq:3 * Dq].reshape(D, D)
    W_o_ssm = g_rows[:, 3 * Dq:4 * Dq].reshape(D, D)
    W_down = [g_rows[:, 4 * Dq + i * Fq:4 * Dq + (i + 1) * Fq].reshape(F, D) for i in range(2)]
    ya = matmul(o2, W_o_attn, name="att_out")
    x1 = gate_res_fwd(x0, ya, g1, B=B, S=S, name="att_res")
    x2, ffn0 = ffn_fwd(x1, 0)

    lr, li, bbr, bbi = _ssm_discretize(a_re[0], a_im[0], log_dt[0], b_re[0], b_im[0])
    J = G // GROUPS_PER_BLOCK
    Wst = GROUPS_PER_BLOCK * P
    bre_blk = _block_diag_in(bbr).astype(BF16)
    bim_blk = _block_diag_in(bbi).astype(BF16)
    cre_blk = _block_diag_out(c_re[0]).astype(BF16)
    cim_blk = _block_diag_out(c_im[0]).astype(BF16)
    lr8 = jnp.broadcast_to(lr.reshape(J, 1, Wst), (J, SEGMENTS, Wst))
    li8 = jnp.broadcast_to(li.reshape(J, 1, Wst), (J, SEGMENTS, Wst))
    sh1s, sc1s, g1s = mods[1][0], mods[1][1], mods[1][2]
    h1s = norm_mod_fwd(x2, norm_mix[1], sh1s, sc1s, B=B, S=S, name="ssm_norm")
    h1p = _interleave(h1s, B, S)
    u = matmul(h1p, W_in, name="ssm_in")
    y_ssm = ssm_fwd(u, bre_blk, bim_blk, cre_blk, cim_blk, lr8, li8, d_skip_full, B=B, S=S, name="ssm_scan_fwd")
    zb = gelu_fwd(y_ssm, B=B, S=S, name="ssm_gelu")
    s_glu = matmul(zb, W_glu, bias=b_glu_full, name="ssm_glu_proj")
    gb = glu_fwd(y_ssm, s_glu, B=B, S=S, name="ssm_glu")
    ys_p = matmul(gb, W_o_ssm, name="ssm_out")
    ys = _deinterleave(ys_p, B, S)
    x3 = gate_res_fwd(x2, ys, g1s, B=B, S=S, name="ssm_res")
    x4, ffn1 = ffn_fwd(x3, 1)

    dx4, loss_p, dsh_f, dsc_f, dnorm_out = final_loss(x4, tgt, norm_out, sh_f, sc_f, B=B, S=S, name="loss_head")
    loss = lax.psum(jnp.sum(loss_p), ("x", "y", "c"))

    def ffn_bwd(dxo, i, saved):
        xin, h2, up, act, yf = saved
        sc2, g2 = mods[i][4], mods[i][5]
        dyf, dg2 = gate_res_bwd(dxo, yf, g2, B=B, S=S, name=f"ffn_res_bwd_{i}")
        dact = matmul(dyf, W_down[i], tb=True, name=f"ffn_down_dx_{i}")
        dW_down = matmul(act, dyf, ta=True, name=f"ffn_down_dw_{i}")
        dup, dcw, dcb = conv_gate_bwd(up, dact, conv_w_full[i], conv_b[i:i + 1], B=B, S=S, name=f"ffn_conv_bwd_{i}")
        dh2 = matmul(dup, W_up[i], tb=True, b_chips=True, name=f"ffn_up_dx_{i}")
        dW_up = matmul(h2, dup, ta=True, b_chips=True, out_chips=True, name=f"ffn_up_dw_{i}")
        dxin, dsh2, dsc2, dnf = norm_mod_bwd(dh2, xin, dxo, norm_ffn[i], sc2, B=B, S=S, name=f"ffn_norm_bwd_{i}")
        dconv_w = jnp.sum(dcw, axis=0).transpose(1, 0, 2).reshape(3, F2)
        return dxin, dict(dW_down=dW_down, dW_up=dW_up, dconv_b=jnp.sum(dcb, axis=0).reshape(F2),
                          dconv_w=dconv_w, dnorm_ffn=jnp.sum(dnf, axis=0), dsh2=dsh2, dsc2=dsc2, dg2=dg2)

    dx3, gf1 = ffn_bwd(dx4, 1, ffn1)

    dys_p, dg1s = gate_res_bwd(_interleave(dx3, B, S), ys_p, g1s, B=B, S=S, name="ssm_res_bwd")
    dgb = matmul(dys_p, W_o_ssm, tb=True, name="ssm_out_dx")
    dW_o_ssm = matmul(gb, dys_p, ta=True, name="ssm_out_dw")
    ds_glu, dz1, db_glu = glu_bwd1(y_ssm, s_glu, dgb, B=B, S=S, name="ssm_glu_bwd1")
    dz2 = matmul(ds_glu, W_glu, tb=True, name="ssm_glu_dx")
    dW_glu = matmul(zb, ds_glu, ta=True, name="ssm_glu_dw")
    dy_ssm = glu_bwd2(y_ssm, dz1, dz2, B=B, S=S, name="ssm_glu_bwd2")
    du, dbre, dbim, dcre, dcim, dlr8, dli8, ddsk = ssm_bwd(u, dy_ssm, bre_blk, bim_blk, cre_blk, cim_blk, lr8, li8,
                                                           d_skip_full, B=B, S=S, name="ssm_scan_bwd")
    dub = du.astype(BF16)
    dh1p = matmul(dub, W_in, tb=True, name="ssm_in_dx")
    dW_in = matmul(h1p, dub, ta=True, name="ssm_in_dw")
    dx2, dsh1s, dsc1s, dnm1 = norm_mod_bwd(_deinterleave(dh1p, B, S), x2, dx3, norm_mix[1], sc1s, B=B, S=S,
                                           name="ssm_norm_bwd")
    dlr = jnp.sum(dlr8, axis=(0, 2)).reshape(G, P)
    dli = jnp.sum(dli8, axis=(0, 2)).reshape(G, P)
    dbbr = _block_diag_in_grad(jnp.sum(dbre, axis=0), G, P, H)
    dbbi = _block_diag_in_grad(jnp.sum(dbim, axis=0), G, P, H)
    dc_re = _block_diag_out_grad(jnp.sum(dcre, axis=0), G, H, P)
    dc_im = _block_diag_out_grad(jnp.sum(dcim, axis=0), G, H, P)
    dd_skip = jnp.sum(ddsk, axis=0).reshape(D)

    dx1, gf0 = ffn_bwd(dx2, 0, ffn0)

    dya, dg1 = gate_res_bwd(dx1, ya, g1, B=B, S=S, name="att_res_bwd")
    do2 = matmul(dya, W_o_attn, tb=True, out_dtype=BF16, name="att_out_dx")
    dW_o_attn = matmul(o2, dya, ta=True, name="att_out_dw")
    g_rows_cm = jnp.concatenate([dW_o_attn.reshape(N_CHIPS, Dq, D), dW_in.reshape(N_CHIPS, Dq, D),
                                 dW_glu.reshape(N_CHIPS, Dq, D), dW_o_ssm.reshape(N_CHIPS, Dq, D),
                                 gf0["dW_down"].reshape(N_CHIPS, Fq, D), gf1["dW_down"].reshape(N_CHIPS, Fq, D)], axis=1)
    pairs_a = pair_sums([g_rows_cm, gf0["dW_up"], gf1["dW_up"]], [BF16, BF16, BF16], "a")
    dqkv, *gots_a = attn_bwd_pairs(qkv, ltot, do2, pairs_a, B=B, S=S, name="att_bwd")
    dh1a = matmul(dqkv, W_qkv, tb=True, b_chips=True, name="att_qkv_dx")
    dW_qkv = matmul(h1a, dqkv, ta=True, b_chips=True, out_chips=True, name="att_qkv_dw")
    grad_x, dsh1, dsc1, dnm0 = norm_mod_bwd(dh1a, x0, dx1, norm_mix[0], sc1, B=B, S=S, name="att_norm_bwd")

    dmod_rows = jnp.concatenate([dsh1, dsc1, dg1, gf0["dsh2"], gf0["dsc2"], gf0["dg2"],
                                 dsh1s, dsc1s, dg1s, gf1["dsh2"], gf1["dsc2"], gf1["dg2"], dsh_f, dsc_f], axis=1)
    dmod_all = all_gather8(dmod_rows, name="gather_dmod").reshape(BG, 14 * D)
    grad_w_mod = jnp.stack([
        matmul(c_act, lax.dynamic_slice(dmod_all, (0, i * 6 * D + chip * mod_w), (BG, mod_w)), ta=True,
               name=f"mod_dw_{i}") for i in range(2)])
    grad_w_fin = matmul(c_act, lax.dynamic_slice(dmod_all, (0, 12 * D + chip * fin_w), (BG, fin_w)), ta=True,
                        name="fin_dw")

    parts = [jnp.concatenate([jnp.sum(dnm0, axis=0), jnp.sum(dnm1, axis=0)]),
             jnp.concatenate([gf0["dnorm_ffn"], gf1["dnorm_ffn"]]),
             jnp.sum(dmod_rows[:, :12 * D], axis=0),
             dlr.reshape(-1), dli.reshape(-1), dbbr.reshape(-1), dbbi.reshape(-1), dc_re.reshape(-1), dc_im.reshape(-1),
             dd_skip, jnp.sum(db_glu, axis=0),
             gf0["dconv_w"].reshape(-1), gf1["dconv_w"].reshape(-1), gf0["dconv_b"], gf1["dconv_b"],
             jnp.sum(dnorm_out, axis=0), jnp.sum(dmod_rows[:, 12 * D:], axis=0)]
    sizes = [int(p.shape[0]) for p in parts]
    flat = jnp.concatenate(parts)
    width = 1024
    quantum = N_CHIPS * 16 * width
    padded = -(-flat.shape[0] // quantum) * quantum
    small_cm = jnp.pad(flat, (0, padded - flat.shape[0])).reshape(N_CHIPS, -1, width)

    pairs_b = pair_sums([dW_qkv, small_cm], [BF16, F32], "b")
    gots_b = scatter_to_chips(pairs_b, name="rs_scatter_to_chips")
    r_qkv, r_small, r_rows, r_up0, r_up1 = join_halves(
        chip_sums(pairs_b, gots_b, "b") + chip_sums(pairs_a, gots_a, "a"), name="rs_join_halves")
    grad_w_qkv = r_qkv[None]
    grad_w_o_attn = r_rows[0 * Dq:1 * Dq][None]
    grad_w_in_ssm = r_rows[1 * Dq:2 * Dq][None]
    grad_w_glu = r_rows[2 * Dq:3 * Dq][None]
    grad_w_o_ssm = r_rows[3 * Dq:4 * Dq][None]
    grad_w_down = r_rows[4 * Dq:].reshape(2, Fq, D)
    grad_w_up = jnp.stack([r_up0, r_up1])
    summed = all_gather8(r_small, name="gather_small_grads")[::2].reshape(-1)
    offs = [0]
    for s_ in sizes:
        offs.append(offs[-1] + s_)
    (s_nm, s_nf, s_bmod, s_lr, s_li, s_bbr, s_bbi, s_cre, s_cim, s_dsk, s_bglu, s_cw0, s_cw1, s_cb0, s_cb1, s_no,
     s_bfin) = [summed[offs[i]:offs[i + 1]] for i in range(len(sizes))]
    _, disc_vjp = jax.vjp(_ssm_discretize, a_re[0], a_im[0], log_dt[0], b_re[0], b_im[0])
    ga_re, ga_im, glog_dt, gb_re, gb_im = disc_vjp((s_lr.reshape(G, P), s_li.reshape(G, P), s_bbr.reshape(G, P, H),
                                                    s_bbi.reshape(G, P, H)))
    grad_norm_mix = s_nm.reshape(2, D)
    grad_norm_ffn = s_nf.reshape(2, D)
    grad_b_mod = s_bmod.reshape(2, 6 * D)
    grad_c_re = s_cre.reshape(1, G, H, P)
    grad_c_im = s_cim.reshape(1, G, H, P)
    grad_d_skip = lax.dynamic_slice(s_dsk, (chip * Dq,), (Dq,)).reshape(1, Dq)
    grad_b_glu = lax.dynamic_slice(s_bglu, (chip * Dq,), (Dq,)).reshape(1, Dq)
    cw_full = jnp.stack([s_cw0.reshape(3, F2), s_cw1.reshape(3, F2)])
    grad_conv_w = lax.dynamic_slice(cw_full, (0, 0, chip * (F2 // N_CHIPS)), (2, 3, F2 // N_CHIPS))
    grad_conv_b = jnp.stack([s_cb0, s_cb1])
    grad_norm_out = s_no
    grad_b_fin = s_bfin

    grads = dict(
        norm_mix=grad_norm_mix, norm_ffn=grad_norm_ffn, w_mod=grad_w_mod, b_mod=grad_b_mod, w_qkv=grad_w_qkv,
        w_o_attn=grad_w_o_attn, w_in_ssm=grad_w_in_ssm, a_re=ga_re[None], a_im=ga_im[None], log_dt=glog_dt[None],
        b_re=gb_re[None], b_im=gb_im[None], c_re=grad_c_re, c_im=grad_c_im, d_skip=grad_d_skip, w_glu=grad_w_glu,
        b_glu=grad_b_glu, w_o_ssm=grad_w_o_ssm, w_up=grad_w_up, conv_w=grad_conv_w, conv_b=grad_conv_b,
        w_down=grad_w_down, norm_out=grad_norm_out, w_fin=grad_w_fin, b_fin=grad_b_fin)
    weights = dict(
        norm_mix=norm_mix, norm_ffn=norm_ffn, w_mod=w_mod, b_mod=b_mod, w_qkv=w_qkv, w_o_attn=w_o_attn,
        w_in_ssm=w_in_ssm, a_re=a_re, a_im=a_im, log_dt=log_dt, b_re=b_re, b_im=b_im, c_re=c_re, c_im=c_im,
        d_skip=d_skip, w_glu=w_glu, b_glu=b_glu, w_o_ssm=w_o_ssm, w_up=w_up, conv_w=conv_w, conv_b=conv_b,
        w_down=w_down, norm_out=norm_out, w_fin=w_fin, b_fin=b_fin)
    m_in = dict(
        norm_mix=m_norm_mix, norm_ffn=m_norm_ffn, w_mod=m_w_mod, b_mod=m_b_mod, w_qkv=m_w_qkv, w_o_attn=m_w_o_attn,
        w_in_ssm=m_w_in_ssm, a_re=m_a_re, a_im=m_a_im, log_dt=m_log_dt, b_re=m_b_re, b_im=m_b_im, c_re=m_c_re,
        c_im=m_c_im, d_skip=m_d_skip, w_glu=m_w_glu, b_glu=m_b_glu, w_o_ssm=m_w_o_ssm, w_up=m_w_up, conv_w=m_conv_w,
        conv_b=m_conv_b, w_down=m_w_down, norm_out=m_norm_out, w_fin=m_w_fin, b_fin=m_b_fin)
    v_in = dict(
        norm_mix=v_norm_mix, norm_ffn=v_norm_ffn, w_mod=v_w_mod, b_mod=v_b_mod, w_qkv=v_w_qkv, w_o_attn=v_w_o_attn,
        w_in_ssm=v_w_in_ssm, a_re=v_a_re, a_im=v_a_im, log_dt=v_log_dt, b_re=v_b_re, b_im=v_b_im, c_re=v_c_re,
        c_im=v_c_im, d_skip=v_d_skip, w_glu=v_w_glu, b_glu=v_b_glu, w_o_ssm=v_w_o_ssm, w_up=v_w_up, conv_w=v_conv_w,
        conv_b=v_conv_b, w_down=v_w_down, norm_out=v_norm_out, w_fin=v_w_fin, b_fin=v_b_fin)
    names = list(weights)
    for n_ in names:
        grads[n_] = grads[n_].reshape(weights[n_].shape)

    big = ("w_mod", "w_qkv", "w_o_attn", "w_in_ssm", "w_glu", "w_o_ssm", "w_up", "w_down", "w_fin")
    delta, new_m, new_v = {}, {}, {}
    for n_ in big:
        shp = weights[n_].shape
        two_d = lambda a: a.reshape(-1, shp[-1])
        d_, m_, v_ = adamw(two_d(weights[n_]), two_d(grads[n_]), two_d(m_in[n_]), two_d(v_in[n_]), name=f"adamw_{n_}")
        delta[n_], new_m[n_], new_v[n_] = d_.reshape(shp), m_.reshape(shp), v_.reshape(shp)
    rest = [n_ for n_ in names if n_ not in big]

    def pack(tree):
        f = jnp.concatenate([tree[n_].reshape(-1) for n_ in rest])
        pad_to = -(-f.shape[0] // (8 * width)) * (8 * width)
        return jnp.pad(f, (0, pad_to - f.shape[0]), constant_values=1.0).reshape(-1, width)

    d_, m_, v_ = adamw(pack(weights), pack(grads), pack(m_in), pack(v_in), name="adamw_small")
    off = 0
    for n_ in rest:
        sz = int(math.prod(weights[n_].shape))
        shp = weights[n_].shape
        delta[n_] = d_.reshape(-1)[off:off + sz].reshape(shp)
        new_m[n_] = m_.reshape(-1)[off:off + sz].reshape(shp)
        new_v[n_] = v_.reshape(-1)[off:off + sz].reshape(shp)
        off += sz

    return (loss, grad_x.reshape(B, S, D), *[grads[n_] for n_ in names], *[delta[n_] for n_ in names],
            *[new_m[n_] for n_ in names], *[new_v[n_] for n_ in names])
```

```python
import functools
import math

import jax
import jax.numpy as jnp
from jax import lax
from jax.experimental import pallas as pl
from jax.experimental.pallas import tpu as pltpu

F32 = jnp.float32
BF16 = jnp.bfloat16
MESH = pl.DeviceIdType.MESH

HEAD_DIM = 64
SSM_GROUP = 16
STATE = 64
GROUPS_PER_BLOCK = 8
SEGMENTS = 16
SCAN_UNROLL = 4
EPS = 1e-6
ADAM_LR = 0.001
ADAM_B1 = 0.9
ADAM_B2 = 0.999
ADAM_EPS = 1e-08
ADAM_WD = 0.01
ADAM_STEP = 10
N_CHIPS = 4
N_DEV = 8
V7X_VMEM_LIMIT = 56 * 1024 * 1024
ATT_BLOCK = 128
ATT_HEADS = 8
ATT_HEADS_BWD = 8


def _tile(n, prefs):
    for p in prefs:
        if n % p == 0:
            return p
    return n


def _params(sem, vmem=V7X_VMEM_LIMIT):
    return pltpu.CompilerParams(dimension_semantics=sem, vmem_limit_bytes=vmem)


def matmul(a, b, *, ta=False, tb=False, bias=None, out_dtype=F32, b_chips=False, out_chips=False, name):
    a_parts = a.shape[0] if a.ndim == 3 else 1
    if a_parts > 1:
        assert not ta
        M, K = a.shape[1], a_parts * a.shape[2]
    elif ta:
        K, M = a.shape
    else:
        M, K = a.shape
    b_parts = b.shape[0] if b_chips else 1
    b_rows, b_cols = (b.shape[1], b_parts * b.shape[2]) if b_chips else b.shape
    if tb:
        N, Kb = b_rows, b_cols
    else:
        Kb, N = b_rows, b_cols
    assert K == Kb, (a.shape, b.shape, ta, tb)
    n_cut = math.gcd(N // (N_CHIPS if out_chips else 1), N // (b_parts if not tb else 1))
    k_cut = math.gcd(K // (b_parts if tb else 1), K // a_parts)
    tm = _tile(M, (1024, 1408, 512, 256, 128))
    tn = _tile(n_cut, (1024, 1408, 768, 512, 256, 128))
    tk = k_cut if k_cut <= 2816 else _tile(k_cut, (1024, 512, 256, 128))
    nk = K // tk
    npc = N // N_CHIPS // tn
    npb = N // b_parts // tn
    kpb = K // b_parts // tk
    kpa = K // a_parts // tk
    dims = (((0,) if ta else (1,), (1,) if tb else (0,)), ((), ()))

    def body(*refs):
        a_ref, b_ref = refs[:2]
        bias_ref = refs[2] if bias is not None else None
        o_ref = refs[-2] if nk > 1 else refs[-1]

        def finish(r):
            if bias_ref is not None:
                r = r + bias_ref[...]
            o_ref[...] = r.astype(o_ref.dtype)

        prod = lax.dot_general(a_ref[...].astype(BF16), b_ref[...].astype(BF16), dims, preferred_element_type=F32)
        if nk == 1:
            finish(prod)
            return
        acc_ref = refs[-1]
        k = pl.program_id(2)

        @pl.when(k == 0)
        def _():
            acc_ref[...] = prod

        @pl.when(k > 0)
        def _():
            acc_ref[...] += prod

        @pl.when(k == nk - 1)
        def _():
            finish(acc_ref[...])

    if a_parts > 1:
        a_spec = pl.BlockSpec((None, tm, tk), lambda i, j, k: (lax.div(k, kpa), i, lax.rem(k, kpa)))
    else:
        a_spec = pl.BlockSpec((tk, tm), lambda i, j, k: (k, i)) if ta else pl.BlockSpec((tm, tk), lambda i, j, k: (i, k))
    if not b_chips:
        b_spec = pl.BlockSpec((tn, tk), lambda i, j, k: (j, k)) if tb else pl.BlockSpec((tk, tn), lambda i, j, k: (k, j))
    elif tb:
        b_spec = pl.BlockSpec((None, tn, tk), lambda i, j, k: (lax.div(k, kpb), j, lax.rem(k, kpb)))
    else:
        b_spec = pl.BlockSpec((None, tk, tn), lambda i, j, k: (lax.div(j, npb), k, lax.rem(j, npb)))
    in_specs = [a_spec, b_spec]
    args = [a, b]
    if bias is not None:
        in_specs.append(pl.BlockSpec((1, tn), lambda i, j, k: (0, j)))
        args.append(bias.reshape(1, N).astype(F32))
    if out_chips:
        out_shape = jax.ShapeDtypeStruct((N_CHIPS, M, N // N_CHIPS), out_dtype)
        out_spec = pl.BlockSpec((None, tm, tn), lambda i, j, k: (lax.div(j, npc), i, lax.rem(j, npc)))
    else:
        out_shape = jax.ShapeDtypeStruct((M, N), out_dtype)
        out_spec = pl.BlockSpec((tm, tn), lambda i, j, k: (i, j))
    return pl.pallas_call(
        body, name=name,
        out_shape=out_shape,
        grid=(M // tm, N // tn, nk),
        in_specs=in_specs,
        out_specs=out_spec,
        scratch_shapes=[pltpu.VMEM((tm, tn), F32)] if nk > 1 else [],
        compiler_params=_params(("parallel", "parallel", "arbitrary")),
    )(*args)


def rowwise(fn, tiled, per_seq, glob, out_tiled, out_seq, *, B, S, name, rows=512):
    tm = _tile(S, (rows, 128, 64, 32, 16, 8))
    nt = S // tm
    n_in = len(tiled) + len(per_seq) + len(glob)
    n_ot = len(out_tiled)

    def body(*refs):
        ins = refs[:n_in]
        outs = refs[n_in:]
        vals = fn(*[r[...] for r in ins])
        if not isinstance(vals, (tuple, list)):
            vals = (vals,)
        assert len(vals) == len(outs), (name, len(vals), len(outs))
        for o_ref, v in zip(outs[:n_ot], vals[:n_ot]):
            o_ref[...] = v.astype(o_ref.dtype)
        t = pl.program_id(1)
        for o_ref, v in zip(outs[n_ot:], vals[n_ot:]):
            def first(o_ref=o_ref, v=v):
                o_ref[...] = v.astype(F32)

            def later(o_ref=o_ref, v=v):
                o_ref[...] += v.astype(F32)

            pl.when(t == 0)(first)
            pl.when(t > 0)(later)

    in_specs = [pl.BlockSpec((tm, a.shape[1]), lambda b, t: (b * nt + t, 0)) for a in tiled]
    in_specs += [pl.BlockSpec((None, 1, a.shape[1]), lambda b, t: (b, 0, 0)) for a in per_seq]
    in_specs += [pl.BlockSpec(a.shape, lambda b, t: (0,) * a.ndim) for a in glob]
    out_shape = [jax.ShapeDtypeStruct((B * S, w), dt) for w, dt in out_tiled]
    out_shape += [jax.ShapeDtypeStruct((B, 1, w), F32) for w in out_seq]
    out_specs = [pl.BlockSpec((tm, w), lambda b, t: (b * nt + t, 0)) for w, _ in out_tiled]
    out_specs += [pl.BlockSpec((None, 1, w), lambda b, t: (b, 0, 0)) for w in out_seq]
    res = pl.pallas_call(
        body, name=name, out_shape=out_shape, grid=(B, nt), in_specs=in_specs, out_specs=out_specs,
        compiler_params=_params(("parallel", "arbitrary")),
    )(*tiled, *[a.reshape(B, 1, a.shape[1]) for a in per_seq], *glob)
    res = list(res)
    for i in range(n_ot, len(res)):
        res[i] = res[i].reshape(B, res[i].shape[-1])
    return res


def _rms(x):
    r = lax.rsqrt(jnp.mean(x * x, axis=-1, keepdims=True) + EPS)
    return x * r, r


def norm_mod_fwd(x, g, sh, sc, *, B, S, name):
    def fn(x, sh, sc, g):
        xn, _ = _rms(x)
        return (xn * g) * (1.0 + sc) + sh

    return rowwise(fn, [x], [sh, sc], [g.reshape(1, -1)], [(x.shape[1], BF16)], [], B=B, S=S, name=name)[0]


def _norm_mod_bwd_math(dh, x, sc, g):
    xn, r = _rms(x)
    y = xn * g
    dy = dh * (1.0 + sc)
    dxn = dy * g
    dx = r * (dxn - xn * jnp.mean(dxn * xn, axis=-1, keepdims=True))
    dsh = jnp.sum(dh, axis=0, keepdims=True)
    dsc = jnp.sum(dh * y, axis=0, keepdims=True)
    dg = jnp.sum(dy * xn, axis=0, keepdims=True)
    return dx, dsh, dsc, dg


def norm_mod_bwd(dh, x, dres, g, sc, *, B, S, name):
    D = x.shape[1]

    def fn(dh, x, dres, sc, g):
        dx, dsh, dsc, dg = _norm_mod_bwd_math(dh.astype(F32), x, sc, g)
        return dres + dx, dsh, dsc, dg

    return rowwise(fn, [dh, x, dres], [sc], [g.reshape(1, -1)], [(D, F32)], [D, D, D], B=B, S=S, name=name)


def norm_mod_bwd_gate(dh, x, dres, g, sc, y_prev, gate_prev, *, B, S, name):
    D = x.shape[1]

    def fn(dh, x, dres, y, sc, gate, g):
        dx, dsh, dsc, dg = _norm_mod_bwd_math(dh.astype(F32), x, sc, g)
        dx = dres + dx
        return dx, gate * dx, dsh, dsc, dg, jnp.sum(dx * y, axis=0, keepdims=True)

    return rowwise(fn, [dh, x, dres, y_prev], [sc, gate_prev], [g.reshape(1, -1)], [(D, F32), (D, BF16)],
                   [D, D, D, D], B=B, S=S, name=name)


def gate_res_fwd(x, y, gate, *, B, S, name):
    return rowwise(lambda x, y, g: x + g * y, [x, y], [gate], [], [(x.shape[1], F32)], [], B=B, S=S, name=name)[0]


def res_norm_mod_fwd(x, y, gate, g, sh, sc, *, B, S, name):
    D = x.shape[1]

    def fn(x, y, gate, sh, sc, g):
        x = x + gate * y
        xn, _ = _rms(x)
        return x, (xn * g) * (1.0 + sc) + sh

    return rowwise(fn, [x, y], [gate, sh, sc], [g.reshape(1, -1)], [(D, F32), (D, BF16)], [], B=B, S=S, name=name)


def gate_res_bwd(dx, y, gate, *, B, S, name):
    D = dx.shape[1]

    def fn(dx, y, g):
        return g * dx, jnp.sum(dx * y, axis=0, keepdims=True)

    return rowwise(fn, [dx, y], [gate], [], [(D, BF16)], [D], B=B, S=S, name=name)


def final_loss(x, tgt, g, sh, sc, y_prev, gate_prev, *, B, S, name):
    D = x.shape[1]

    def fn(x, tgt, y_prev, sh, sc, gate, g):
        xn, _ = _rms(x)
        y = (xn * g) * (1.0 + sc) + sh
        err = y - tgt
        loss = 0.5 * jnp.sum(err * err, axis=0, keepdims=True) * (1.0 / D)
        dx, dsh, dsc, dg = _norm_mod_bwd_math(err * (1.0 / D), x, sc, g)
        return dx, gate * dx, loss, dsh, dsc, dg, jnp.sum(dx * y_prev, axis=0, keepdims=True)

    return rowwise(fn, [x, tgt, y_prev], [sh, sc, gate_prev], [g.reshape(1, -1)], [(D, F32), (D, BF16)],
                   [D, D, D, D, D], B=B, S=S, name=name)


def _gelu(y):
    c0 = math.sqrt(2.0 / math.pi)
    t = jnp.tanh(c0 * (y + 0.044715 * (y * y * y)))
    return 0.5 * y * (1.0 + t), t


def _sigmoid(s):
    return 1.0 / (1.0 + jnp.exp(-s))


def gelu_fwd(y, *, B, S, name):
    return rowwise(lambda y: _gelu(y)[0], [y], [], [], [(y.shape[1], BF16)], [], B=B, S=S, name=name)[0]


def glu_fwd(y, s, *, B, S, name):
    return rowwise(lambda y, s: _gelu(y)[0] * _sigmoid(s), [y, s], [], [], [(y.shape[1], BF16)], [], B=B, S=S,
                   name=name)[0]


def glu_bwd1(y, s, dg, *, B, S, name):
    D = y.shape[1]

    def fn(y, s, dg):
        z = _gelu(y)[0]
        sig = _sigmoid(s)
        ds = dg * z * sig * (1.0 - sig)
        return ds, dg * sig, jnp.sum(ds, axis=0, keepdims=True)

    return rowwise(fn, [y, s, dg], [], [], [(D, BF16), (D, F32)], [D], B=B, S=S, name=name)


def glu_bwd2(y, dz1, dz2, *, B, S, name):
    D = y.shape[1]
    c0 = math.sqrt(2.0 / math.pi)

    def fn(y, dz1, dz2):
        _, t = _gelu(y)
        dgelu = 0.5 * (1.0 + t) + 0.5 * y * (1.0 - t * t) * c0 * (1.0 + 3.0 * 0.044715 * y * y)
        return (dz1 + dz2) * dgelu

    return rowwise(fn, [y, dz1, dz2], [], [], [(D, F32)], [], B=B, S=S, name=name)[0]


def silu_rows(c, *, name):
    R, W = c.shape
    return rowwise(lambda c: c * _sigmoid(c), [c], [], [], [(W, F32)], [], B=1, S=R, name=name)[0]


def _shift_down(cur, h6, h7):
    rows = lax.broadcasted_iota(jnp.int32, cur.shape, 0)
    m1 = jnp.where(rows == 0, h7, pltpu.roll(cur, 1, 0))
    m2 = jnp.where(rows == 0, h6, jnp.where(rows == 1, h7, pltpu.roll(cur, 2, 0)))
    return m1, m2


def _conv3(cur, halo_ref, w_ref, has_prev):
    h6 = jnp.where(has_prev, halo_ref[6:7, :], 0.0)
    h7 = jnp.where(has_prev, halo_ref[7:8, :], 0.0)
    m1, m2 = _shift_down(cur, h6, h7)
    return w_ref[2:3, :] * cur + w_ref[1:2, :] * m1 + w_ref[0:1, :] * m2, m1, m2


def _conv_tiles(S, F):
    ts = _tile(S, (1024, 512, 256, 128, 64, 32, 16, 8))
    tn = _tile(F, (256, 128))
    return ts, tn, S // ts, F // tn


def conv_gate_fwd(up, cw, cb, *, B, S, name):
    F = up.shape[1] // 2
    ts, tn, nts, nF = _conv_tiles(S, F)
    hb = ts // 8

    def body(g_ref, gh_ref, v_ref, vh_ref, wg_ref, wv_ref, bg_ref, bv_ref, o_ref):
        has_prev = pl.program_id(2) > 0
        gc = _conv3(g_ref[...], gh_ref, wg_ref, has_prev)[0] + bg_ref[...]
        vc = _conv3(v_ref[...], vh_ref, wv_ref, has_prev)[0] + bv_ref[...]
        o_ref[...] = (gc * _sigmoid(gc) * vc).astype(o_ref.dtype)

    def cur(off):
        return pl.BlockSpec((ts, tn), lambda b, j, t: (b * nts + t, j + off))

    def halo(off):
        return pl.BlockSpec((8, tn), lambda b, j, t: (jnp.maximum((b * nts + t) * hb - 1, 0), j + off))

    def vec(rows, off):
        return pl.BlockSpec((rows, tn), lambda b, j, t: (0, j + off))

    return pl.pallas_call(
        body, name=name, out_shape=jax.ShapeDtypeStruct((B * S, F), BF16), grid=(B, nF, nts),
        in_specs=[cur(0), halo(0), cur(nF), halo(nF), vec(3, 0), vec(3, nF), vec(1, 0), vec(1, nF)],
        out_specs=pl.BlockSpec((ts, tn), lambda b, j, t: (b * nts + t, j)),
        compiler_params=_params(("parallel", "parallel", "arbitrary")),
    )(up, up, up, up, cw, cw, cb, cb)


def conv_gate_bwd1(up, dact, cw, cb, *, B, S, name):
    F = up.shape[1] // 2
    ts, tn, nts, nF = _conv_tiles(S, F)
    hb = ts // 8

    def body(g_ref, gh_ref, v_ref, vh_ref, da_ref, wg_ref, wv_ref, bg_ref, bv_ref, d_ref, db_ref):
        t = pl.program_id(2)
        has_prev = t > 0
        gc = _conv3(g_ref[...], gh_ref, wg_ref, has_prev)[0] + bg_ref[...]
        vc = _conv3(v_ref[...], vh_ref, wv_ref, has_prev)[0] + bv_ref[...]
        sig = _sigmoid(gc)
        da = da_ref[...]
        dg = da * vc * (sig * (1.0 + gc * (1.0 - sig)))
        dv = da * (gc * sig)
        d_ref[0] = dg
        d_ref[1] = dv
        part = jnp.concatenate([jnp.sum(dg, axis=0, keepdims=True), jnp.sum(dv, axis=0, keepdims=True)], axis=0)

        @pl.when(t == 0)
        def _():
            db_ref[...] = part

        @pl.when(t > 0)
        def _():
            db_ref[...] += part

    def cur(off):
        return pl.BlockSpec((ts, tn), lambda b, j, t: (b * nts + t, j + off))

    def halo(off):
        return pl.BlockSpec((8, tn), lambda b, j, t: (jnp.maximum((b * nts + t) * hb - 1, 0), j + off))

    def vec(rows, off):
        return pl.BlockSpec((rows, tn), lambda b, j, t: (0, j + off))

    return pl.pallas_call(
        body, name=name,
        out_shape=[jax.ShapeDtypeStruct((2, B * S, F), F32), jax.ShapeDtypeStruct((B, 2, F), F32)],
        grid=(B, nF, nts),
        in_specs=[cur(0), halo(0), cur(nF), halo(nF), cur(0), vec(3, 0), vec(3, nF), vec(1, 0), vec(1, nF)],
        out_specs=[pl.BlockSpec((2, ts, tn), lambda b, j, t: (0, b * nts + t, j)),
                   pl.BlockSpec((None, 2, tn), lambda b, j, t: (b, 0, j))],
        compiler_params=_params(("parallel", "parallel", "arbitrary")),
    )(up, up, up, up, dact, cw, cw, cb, cb)


def conv_bwd2(d3, up, cw, *, B, S, name):
    F = up.shape[1] // 2
    ts, tn, nts, nF = _conv_tiles(S, F)
    hb = ts // 8
    last8 = B * S // 8 - 1

    def body(d_ref, da_ref, u_ref, uh_ref, w_ref, o_ref, dw_ref):
        t = pl.program_id(3)
        d = d_ref[...]
        has_next = t < nts - 1
        a0 = jnp.where(has_next, da_ref[0:1, :], 0.0)
        a1 = jnp.where(has_next, da_ref[1:2, :], 0.0)
        rows = lax.broadcasted_iota(jnp.int32, d.shape, 0)
        p1 = jnp.where(rows == ts - 1, a0, pltpu.roll(d, ts - 1, 0))
        p2 = jnp.where(rows == ts - 1, a1, jnp.where(rows == ts - 2, a0, pltpu.roll(d, ts - 2, 0)))
        o_ref[...] = (w_ref[2:3, :] * d + w_ref[1:2, :] * p1 + w_ref[0:1, :] * p2).astype(o_ref.dtype)
        u = u_ref[...]
        has_prev = t > 0
        h6 = jnp.where(has_prev, uh_ref[6:7, :], 0.0)
        h7 = jnp.where(has_prev, uh_ref[7:8, :], 0.0)
        m1, m2 = _shift_down(u, h6, h7)
        part = jnp.concatenate([jnp.sum(d * m2, axis=0, keepdims=True), jnp.sum(d * m1, axis=0, keepdims=True),
                                jnp.sum(d * u, axis=0, keepdims=True)], axis=0)

        @pl.when(t == 0)
        def _():
            dw_ref[...] = part

        @pl.when(t > 0)
        def _():
            dw_ref[...] += part

    return pl.pallas_call(
        body, name=name,
        out_shape=[jax.ShapeDtypeStruct((B * S, 2 * F), BF16), jax.ShapeDtypeStruct((B, 3, 2 * F), F32)],
        grid=(B, 2, nF, nts),
        in_specs=[
            pl.BlockSpec((None, ts, tn), lambda b, g, j, t: (g, b * nts + t, j)),
            pl.BlockSpec((None, 8, tn), lambda b, g, j, t: (g, jnp.minimum((b * nts + t + 1) * hb, last8), j)),
            pl.BlockSpec((ts, tn), lambda b, g, j, t: (b * nts + t, g * nF + j)),
            pl.BlockSpec((8, tn), lambda b, g, j, t: (jnp.maximum((b * nts + t) * hb - 1, 0), g * nF + j)),
            pl.BlockSpec((3, tn), lambda b, g, j, t: (0, g * nF + j)),
        ],
        out_specs=[pl.BlockSpec((ts, tn), lambda b, g, j, t: (b * nts + t, g * nF + j)),
                   pl.BlockSpec((None, 3, tn), lambda b, g, j, t: (b, 0, g * nF + j))],
        compiler_params=_params(("parallel", "parallel", "parallel", "arbitrary")),
    )(d3, d3, up, up, cw)


def conv_gate_bwd(up, dact, cw, cb, *, B, S, name):
    F = up.shape[1] // 2
    tn = _tile(F, (256, 128))
    nF = F // tn

    def body(g_ref, v_ref, da_ref, wg_ref, wv_ref, bg_ref, bv_ref, o_ref, dw_ref, db_ref):
        rows = lax.broadcasted_iota(jnp.int32, (S, tn), 0)

        def earlier(x, k):
            return jnp.where(rows >= k, pltpu.roll(x, k, 0), 0.0)

        def later(x, k):
            return jnp.where(rows < S - k, pltpu.roll(x, S - k, 0), 0.0)

        def conv(x, w_ref):
            x1, x2 = earlier(x, 1), earlier(x, 2)
            return w_ref[2:3, :] * x + w_ref[1:2, :] * x1 + w_ref[0:1, :] * x2, x1, x2

        def back(d, x, x1, x2, w_ref, half):
            o_ref[half] = (w_ref[2:3, :] * d + w_ref[1:2, :] * later(d, 1) + w_ref[0:1, :] * later(d, 2)
                           ).astype(o_ref.dtype)
            dw_ref[half] = jnp.concatenate([jnp.sum(d * x2, axis=0, keepdims=True),
                                            jnp.sum(d * x1, axis=0, keepdims=True),
                                            jnp.sum(d * x, axis=0, keepdims=True)], axis=0)
            return jnp.sum(d, axis=0, keepdims=True)

        g, v, da = g_ref[...], v_ref[...], da_ref[...]
        gc, g1, g2 = conv(g, wg_ref)
        vc, v1, v2 = conv(v, wv_ref)
        gc = gc + bg_ref[...]
        vc = vc + bv_ref[...]
        sig = _sigmoid(gc)
        dg = da * vc * (sig * (1.0 + gc * (1.0 - sig)))
        dv = da * (gc * sig)
        db_ref[...] = jnp.concatenate([back(dg, g, g1, g2, wg_ref, 0), back(dv, v, v1, v2, wv_ref, 1)], axis=0)

    def cols(off):
        return pl.BlockSpec((S, tn), lambda b, j: (b, j + off))

    def vec(rows, off):
        return pl.BlockSpec((rows, tn), lambda b, j: (0, j + off))

    return pl.pallas_call(
        body, name=name,
        out_shape=[jax.ShapeDtypeStruct((2, B * S, F), BF16), jax.ShapeDtypeStruct((B, 2, 3, F), F32),
                   jax.ShapeDtypeStruct((B, 2, F), F32)],
        grid=(B, nF),
        in_specs=[cols(0), cols(nF), cols(0), vec(3, 0), vec(3, nF), vec(1, 0), vec(1, nF)],
        out_specs=[pl.BlockSpec((2, S, tn), lambda b, j: (0, b, j)),
                   pl.BlockSpec((None, 2, 3, tn), lambda b, j: (b, 0, 0, j)),
                   pl.BlockSpec((None, 2, tn), lambda b, j: (b, 0, j))],
        compiler_params=_params(("parallel", "parallel")),
    )(up, up, dact, cw, cw, cb, cb)


MASKED_LOG = -1e30


def _split2(x):
    bits = lax.bitcast_convert_type(x, jnp.uint32) & jnp.uint32(0xFFFF0000)
    hi = lax.bitcast_convert_type(bits, F32)
    return hi.astype(BF16), (x - hi).astype(BF16)


def _split_dot(x, m):
    hi, lo = _split2(x)
    return jnp.dot(hi, m, preferred_element_type=F32) + jnp.dot(lo, m, preferred_element_type=F32)


def _nt(a, b):
    return lax.dot_general(a, b, (((1,), (1,)), ((), ())), preferred_element_type=F32)


def _tn(a, b):
    return lax.dot_general(a, b, (((0,), (0,)), ((), ())), preferred_element_type=F32)


def _att_scores(q, k, mask, prescaled=False):
    z = _nt(q, k)
    if not prescaled:
        z = z * (HEAD_DIM ** -0.5)
    e = jnp.exp(-jnp.abs(z))
    sp = jnp.log(1.0 + e)
    lb = jnp.minimum(z, 0.0) - sp
    l1 = lb - z
    if mask is not None:
        lb = jnp.where(mask, lb, MASKED_LOG)
        l1 = jnp.where(mask, l1, 0.0)
    return z, lb, l1, e


def _col_to_row(col, eye):
    return jnp.sum(jnp.where(eye, col, 0.0), axis=0, keepdims=True)


def _row_to_col(row, eye):
    return jnp.sum(jnp.where(eye, row, 0.0), axis=1, keepdims=True)


def attn_fwd(q, k, v, *, name):
    B, H, S, dh = q.shape
    T = ATT_BLOCK
    nq = S // T

    G = _tile(H, (ATT_HEADS, 2))

    def body(q_ref, k_ref, v_ref, o_ref, l_ref):
        r = lax.broadcasted_iota(jnp.int32, (T, T), 0)
        c = lax.broadcasted_iota(jnp.int32, (T, T), 1)
        later = (r > c).astype(BF16)
        eye = r == c
        diag = c < r
        blk = lax.broadcasted_iota(jnp.int32, (nq, T), 0)

        later2 = jnp.concatenate([later, later], axis=0)

        def scores(g, qb, k0, mask):
            _, lb, l1, _ = _att_scores(qb, k_ref[g, pl.ds(k0, T), :], mask)
            return lb, jnp.concatenate(_split2(l1), axis=1), jnp.sum(l1, axis=1, keepdims=True)

        def weigh_all(k0, sc, st):
            suf = jnp.dot(jnp.concatenate([s[1] for s in sc], axis=0), later2, preferred_element_type=F32)
            out = []
            for g in range(G):
                lb, _, rowsum = sc[g]
                acc, run = st[g]
                w = jnp.exp(lb + suf[g * T:(g + 1) * T] + run)
                acc = acc + jnp.dot(w.astype(BF16), v_ref[g, pl.ds(k0, T), :], preferred_element_type=F32)
                out.append((acc, run + rowsum))
            return tuple(out)

        def qblock(i, totals):
            q0 = pl.multiple_of(i * T, T)
            qbs = [q_ref[g, pl.ds(q0, T), :] for g in range(G)]
            sc0 = tuple(scores(g, qbs[g], q0, diag) for g in range(G))
            st0 = tuple((jnp.zeros((T, dh), F32), jnp.zeros((T, 1), F32)) for _ in range(G))

            def kblock(jj, carry):
                sc, st = carry
                k_next = pl.multiple_of((i - jj) * T, T)
                k_cur = pl.multiple_of((i - jj + 1) * T, T)
                st = weigh_all(k_cur, sc, st)
                sc_next = tuple(scores(g, qbs[g], k_next, None) for g in range(G))
                return sc_next, st

            sc, st = lax.fori_loop(1, i + 1, kblock, (sc0, st0))
            st = weigh_all(0, sc, st)
            for g in range(G):
                o_ref[g, pl.ds(q0, T), :] = st[g][0]
            return tuple(jnp.where(blk == i, _col_to_row(st[g][1], eye), totals[g]) for g in range(G))

        totals = lax.fori_loop(0, nq, qblock, tuple(jnp.zeros((nq, T), F32) for _ in range(G)))
        for g in range(G):
            l_ref[g] = totals[g]

    spec = pl.BlockSpec((None, G, S, dh), lambda b, h: (b, h, 0, 0))
    lspec = pl.BlockSpec((None, G, nq, T), lambda b, h: (b, h, 0, 0))
    return pl.pallas_call(
        body, name=name,
        out_shape=[jax.ShapeDtypeStruct((B, H, S, dh), F32), jax.ShapeDtypeStruct((B, H, nq, T), F32)],
        grid=(B, H // G), in_specs=[spec, spec, spec], out_specs=[spec, lspec],
        compiler_params=_params(("parallel", "parallel")),
    )(q, k, v)


def attn_bwd(q, k, v, ltot, do, *, name):
    B, H, S, dh = q.shape
    T = ATT_BLOCK
    nq = S // T
    scale = HEAD_DIM ** -0.5

    G = _tile(H, (ATT_HEADS_BWD, 2))

    def body(q_ref, k_ref, v_ref, l_ref, do_ref, dq_ref, dk_ref, dv_ref, dk_acc, dv_acc):
        r = lax.broadcasted_iota(jnp.int32, (T, T), 0)
        c = lax.broadcasted_iota(jnp.int32, (T, T), 1)
        upto = (r <= c).astype(BF16)
        before = (r < c).astype(BF16)
        upto2 = jnp.concatenate([upto, upto], axis=0)
        before2 = jnp.concatenate([before, before], axis=0)
        eye = r == c
        diag = c < r
        blk = lax.broadcasted_iota(jnp.int32, (nq, T), 0)
        dk_acc[...] = jnp.zeros_like(dk_acc)
        dv_acc[...] = jnp.zeros_like(dv_acc)

        def scores(g, qb, dob, k0, mask):
            z, lb, l1, e = _att_scores(qb, k_ref[g, pl.ds(k0, T), :], mask)
            inv = 1.0 / (1.0 + e)
            small = e * inv
            pos = z >= 0.0
            beta = jnp.where(pos, inv, small)
            omb = jnp.where(pos, small, inv)
            if mask is not None:
                beta = jnp.where(mask, beta, 0.0)
            dw = _nt(dob, v_ref[g, pl.ds(k0, T), :])
            return lb, jnp.concatenate(_split2(l1), axis=1), jnp.sum(l1, axis=1, keepdims=True), dw, beta, omb

        def grads_all(qbs, dobs, tots, k0, sc, st):
            pre = jnp.dot(jnp.concatenate([s[1] for s in sc], axis=0), upto2, preferred_element_type=F32)
            dlws = []
            for g in range(G):
                lb = sc[g][0]
                w = jnp.exp(lb + (tots[g] - (pre[g * T:(g + 1) * T] + st[g][1])))
                dv_acc[g, pl.ds(k0, T), :] += _tn(w.astype(BF16), dobs[g])
                dlws.append(sc[g][3] * w)
            pre_d = jnp.dot(jnp.concatenate([jnp.concatenate(_split2(d), axis=1) for d in dlws], axis=0), before2,
                            preferred_element_type=F32)
            out = []
            for g in range(G):
                _, _, rowsum, _, beta, omb = sc[g]
                dq, run_l, run_d = st[g]
                dl1 = pre_d[g * T:(g + 1) * T] + run_d
                dz = ((dlws[g] * omb - dl1 * beta) * scale).astype(BF16)
                dq = dq + jnp.dot(dz, k_ref[g, pl.ds(k0, T), :], preferred_element_type=F32)
                dk_acc[g, pl.ds(k0, T), :] += _tn(dz, qbs[g])
                out.append((dq, run_l + rowsum, run_d + jnp.sum(dlws[g], axis=1, keepdims=True)))
            return tuple(out)

        def block_inputs(i, q0):
            qbs = [q_ref[g, pl.ds(q0, T), :] for g in range(G)]
            dobs = [do_ref[g, pl.ds(q0, T), :] for g in range(G)]
            tots = [_row_to_col(jnp.sum(jnp.where(blk == i, l_ref[g], 0.0), axis=0, keepdims=True), eye)
                    for g in range(G)]
            z1 = jnp.zeros((T, 1), F32)
            return qbs, dobs, tots, tuple((jnp.zeros((T, dh), F32), z1, z1) for _ in range(G))

        qbs, dobs, tots, st = block_inputs(0, 0)
        st = grads_all(qbs, dobs, tots, 0, tuple(scores(g, qbs[g], dobs[g], 0, diag) for g in range(G)), st)
        for g in range(G):
            dq_ref[g, 0:T, :] = st[g][0].astype(dq_ref.dtype)

        def qblock(i, carry0):
            q0 = pl.multiple_of(i * T, T)
            qbs, dobs, tots, st0 = block_inputs(i, q0)
            sc0 = tuple(scores(g, qbs[g], dobs[g], 0, None) for g in range(G))

            def kblock(j, carry):
                sc, st = carry
                k_cur = pl.multiple_of(j * T, T)
                k_next = pl.multiple_of((j + 1) * T, T)
                sc_next = tuple(scores(g, qbs[g], dobs[g], k_next, None) for g in range(G))
                st = grads_all(qbs, dobs, tots, k_cur, sc, st)
                return sc_next, st

            sc, st = lax.fori_loop(0, i - 1, kblock, (sc0, st0))
            k_last = pl.multiple_of((i - 1) * T, T)
            st = grads_all(qbs, dobs, tots, k_last, sc, st)
            sc_diag = tuple(scores(g, qbs[g], dobs[g], q0, diag) for g in range(G))
            st = grads_all(qbs, dobs, tots, q0, sc_diag, st)
            for g in range(G):
                dq_ref[g, pl.ds(q0, T), :] = st[g][0].astype(dq_ref.dtype)
            return carry0

        lax.fori_loop(1, nq, qblock, 0)
        dk_ref[...] = dk_acc[...].astype(dk_ref.dtype)
        dv_ref[...] = dv_acc[...].astype(dv_ref.dtype)

    spec = pl.BlockSpec((None, G, S, dh), lambda b, h: (b, h, 0, 0))
    lspec = pl.BlockSpec((None, G, nq, T), lambda b, h: (b, h, 0, 0))
    shp = jax.ShapeDtypeStruct((B, H, S, dh), BF16)
    return pl.pallas_call(
        body, name=name, out_shape=[shp, shp, shp], grid=(B, H // G),
        in_specs=[spec, spec, spec, lspec, spec], out_specs=[spec] * 3,
        scratch_shapes=[pltpu.VMEM((G, S, dh), F32), pltpu.VMEM((G, S, dh), F32)],
        compiler_params=_params(("parallel", "parallel")),
    )(q, k, v, ltot, do)


def _wide_consts(T, W):
    r = lax.broadcasted_iota(jnp.int32, (W, W), 0)
    c = lax.broadcasted_iota(jnp.int32, (W, W), 1)
    two = lambda m: jnp.concatenate([m.astype(BF16)] * 2, axis=0)
    qrow = lax.broadcasted_iota(jnp.int32, (T, W), 0)
    kcol = lax.broadcasted_iota(jnp.int32, (T, W), 1)
    er = lax.broadcasted_iota(jnp.int32, (T, T), 0)
    ec = lax.broadcasted_iota(jnp.int32, (T, T), 1)
    return two(r > c), two(r <= c), two(r < c), qrow, kcol, er == ec


def attn_fwd_wide(q, k, v, shards=(), slots=(), *, name):
    B, H, S, dh = q.shape
    T = ATT_BLOCK
    W = 2 * T
    nq = S // T
    G = _tile(H, (ATT_HEADS, 2))
    n = len(shards)
    n_steps = B * (H // G)

    def body(*refs):
        q_ref, k_ref, v_ref = refs[:3]
        o_ref, l_ref = refs[3 + 2 * n:5 + 2 * n]
        step_id = pl.program_id(0) * (H // G) + pl.program_id(1)
        if n:
            gather = _ShardGather(refs[3:3 + n], refs[5 + 2 * n:5 + 3 * n], *refs[5 + 3 * n:])
            pl.when(step_id == 0)(gather.send)
            pl.when(step_id == n_steps - 1)(gather.forward)
        later2, _, _, qrow, kcol, eye = _wide_consts(T, W)
        blk = lax.broadcasted_iota(jnp.int32, (nq, T), 0)

        def step(qbs, k0, st, mask):
            parts, lbs, sums = [], [], []
            for g in range(G):
                _, lb, l1, _ = _att_scores(qbs[g], k_ref[g, pl.ds(k0, W), :], mask, prescaled=True)
                parts.append(jnp.concatenate(_split2(l1), axis=1))
                lbs.append(lb)
                sums.append(jnp.sum(l1, axis=1, keepdims=True))
            suf = jnp.dot(jnp.concatenate(parts, axis=0), later2, preferred_element_type=F32)
            out = []
            for g in range(G):
                acc, run = st[g]
                w = jnp.exp(lbs[g] + suf[g * T:(g + 1) * T] + run)
                acc = acc + jnp.dot(w.astype(BF16), v_ref[g, pl.ds(k0, W), :], preferred_element_type=F32)
                out.append((acc, run + sums[g]))
            return tuple(out)

        def qblock(i, totals):
            q0 = pl.multiple_of(i * T, T)
            qbs = [q_ref[g, pl.ds(q0, T), :] * (HEAD_DIM ** -0.5) for g in range(G)]
            half = jnp.right_shift(i, 1)
            last = half * W
            k_last = pl.multiple_of(last, W)
            mask = (k_last + kcol) < (q0 + qrow)
            st = tuple((jnp.zeros((T, dh), F32), jnp.zeros((T, 1), F32)) for _ in range(G))
            st = step(qbs, k_last, st, mask)

            def kblock(jj, st):
                return step(qbs, pl.multiple_of(last - jj * W, W), st, None)

            st = lax.fori_loop(1, half + 1, kblock, st)
            for g in range(G):
                o_ref[g, pl.ds(q0, T), :] = st[g][0]
            return tuple(jnp.where(blk == i, _col_to_row(st[g][1], eye), totals[g]) for g in range(G))

        totals = lax.fori_loop(0, nq, qblock, tuple(jnp.zeros((nq, T), F32) for _ in range(G)))
        for g in range(G):
            l_ref[g] = totals[g]
        if n:
            pl.when(step_id == n_steps - 1)(gather.finish)

    spec = pl.BlockSpec((None, G, S, dh), lambda b, h: (b, h, 0, 0))
    lspec = pl.BlockSpec((None, G, nq, T), lambda b, h: (b, h, 0, 0))
    dma = pltpu.SemaphoreType.DMA
    return pl.pallas_call(
        body, name=name,
        out_shape=[jax.ShapeDtypeStruct((B, H, S, dh), F32), jax.ShapeDtypeStruct((B, H, nq, T), F32)]
        + [jax.ShapeDtypeStruct(s.shape, s.dtype) for s in slots],
        grid=(B, H // G), in_specs=[spec, spec, spec] + _any_specs(2 * n), out_specs=[spec, lspec] + _any_specs(n),
        input_output_aliases={3 + n + i: 2 + i for i in range(n)},
        scratch_shapes=[dma((3 * n,))] * 4 if n else [],
        compiler_params=_params(("arbitrary", "arbitrary")),
    )(q, k, v, *shards, *slots)


def attn_bwd_wide(q, k, v, ltot, do, partials=(), *, name):
    B, H, S, dh = q.shape
    T = ATT_BLOCK
    W = 2 * T
    nq = S // T
    scale = HEAD_DIM ** -0.5
    G = _tile(H, (ATT_HEADS_BWD, 2))
    n = len(partials)
    n_steps = B * (H // G)

    def body(*refs):
        q_ref, k_ref, v_ref, l_ref, do_ref = refs[:5]
        dq_ref, dk_ref, dv_ref = refs[5 + n:8 + n]
        dk_acc, dv_acc = refs[8 + 2 * n:10 + 2 * n]
        step_id = pl.program_id(0) * (H // G) + pl.program_id(1)
        if n:
            scatter = _ChipScatter(refs[5:5 + n], refs[8 + n:8 + 2 * n], *refs[10 + 2 * n:])
            pl.when(step_id == 0)(scatter.send)
        _, upto2, before2, qrow, kcol, eye = _wide_consts(T, W)
        blk = lax.broadcasted_iota(jnp.int32, (nq, T), 0)
        dk_acc[...] = jnp.zeros_like(dk_acc)
        dv_acc[...] = jnp.zeros_like(dv_acc)

        def step(qbs, dobs, tots, k0, st, mask):
            sc = []
            for g in range(G):
                z, lb, l1, e = _att_scores(qbs[g], k_ref[g, pl.ds(k0, W), :], mask, prescaled=True)
                inv = 1.0 / (1.0 + e)
                small = e * inv
                pos = z >= 0.0
                beta = jnp.where(pos, inv, small)
                omb = jnp.where(pos, small, inv)
                if mask is not None:
                    beta = jnp.where(mask, beta, 0.0)
                dw = _nt(dobs[g], v_ref[g, pl.ds(k0, W), :])
                sc.append((lb, jnp.concatenate(_split2(l1), axis=1), jnp.sum(l1, axis=1, keepdims=True), dw, beta, omb))
            pre = jnp.dot(jnp.concatenate([s[1] for s in sc], axis=0), upto2, preferred_element_type=F32)
            dlws = []
            for g in range(G):
                w = jnp.exp(sc[g][0] + (tots[g] - (pre[g * T:(g + 1) * T] + st[g][1])))
                dv_acc[g, pl.ds(k0, W), :] += _tn(w.astype(BF16), dobs[g])
                dlws.append(sc[g][3] * w)
            pre_d = jnp.dot(jnp.concatenate([jnp.concatenate(_split2(d), axis=1) for d in dlws], axis=0), before2,
                            preferred_element_type=F32)
            out = []
            for g in range(G):
                _, _, rowsum, _, beta, omb = sc[g]
                dq, run_l, run_d = st[g]
                dl1 = pre_d[g * T:(g + 1) * T] + run_d
                dz = (dlws[g] * omb - dl1 * beta).astype(BF16)
                dq = dq + jnp.dot(dz, k_ref[g, pl.ds(k0, W), :], preferred_element_type=F32)
                dk_acc[g, pl.ds(k0, W), :] += _tn(dz, qbs[g])
                out.append((dq, run_l + rowsum, run_d + jnp.sum(dlws[g], axis=1, keepdims=True)))
            return tuple(out)

        def qblock(i, carry0):
            q0 = pl.multiple_of(i * T, T)
            qbs = [q_ref[g, pl.ds(q0, T), :] * scale for g in range(G)]
            dobs = [do_ref[g, pl.ds(q0, T), :] for g in range(G)]
            tots = [_row_to_col(jnp.sum(jnp.where(blk == i, l_ref[g], 0.0), axis=0, keepdims=True), eye)
                    for g in range(G)]
            z1 = jnp.zeros((T, 1), F32)
            st = tuple((jnp.zeros((T, dh), F32), z1, z1) for _ in range(G))

            def kblock(j, st):
                return step(qbs, dobs, tots, pl.multiple_of(j * W, W), st, None)

            half = jnp.right_shift(i, 1)
            st = lax.fori_loop(0, half, kblock, st)
            k_last = pl.multiple_of(half * W, W)
            st = step(qbs, dobs, tots, k_last, st, (k_last + kcol) < (q0 + qrow))
            for g in range(G):
                dq_ref[g, pl.ds(q0, T), :] = (st[g][0] * scale).astype(dq_ref.dtype)
            return carry0

        lax.fori_loop(0, nq, qblock, 0)
        dk_ref[...] = dk_acc[...].astype(dk_ref.dtype)
        dv_ref[...] = dv_acc[...].astype(dv_ref.dtype)
        if n:
            pl.when(step_id == n_steps - 1)(scatter.finish)

    spec = pl.BlockSpec((None, G, S, dh), lambda b, h: (b, h, 0, 0))
    lspec = pl.BlockSpec((None, G, nq, T), lambda b, h: (b, h, 0, 0))
    shp = jax.ShapeDtypeStruct((B, H, S, dh), BF16)
    dma = pltpu.SemaphoreType.DMA
    return pl.pallas_call(
        body, name=name,
        out_shape=[shp, shp, shp] + [jax.ShapeDtypeStruct((N_CHIPS - 1,) + a.shape[1:], a.dtype) for a in partials],
        grid=(B, H // G),
        in_specs=[spec, spec, spec, lspec, spec] + _any_specs(n), out_specs=[spec] * 3 + _any_specs(n),
        scratch_shapes=[pltpu.VMEM((G, S, dh), F32), pltpu.VMEM((G, S, dh), F32)]
        + ([dma((3 * n,)), dma((3 * n,))] if n else []),
        compiler_params=_params(("arbitrary", "arbitrary")),
    )(q, k, v, ltot, do, *partials)


def _pair_masks(x, first):
    zero = jnp.zeros_like(x)
    return jnp.where(first, x, zero), jnp.where(first, zero, x)


def attn_fwd_pairs(qkv, shards=(), slots=(), *, B, S, name):
    D = qkv.shape[1] // 3
    H = D // HEAD_DIM
    T = ATT_BLOCK
    W = 2 * T
    nq = S // T
    G = _tile(H, (ATT_HEADS, 2))
    P = G // 2
    LW = 2 * HEAD_DIM * P
    nsec = D // LW
    n = len(shards)
    n_steps = B * nsec

    def body(*refs):
        q_ref, k_ref, v_ref = refs[:3]
        o_ref, l_ref = refs[3 + 2 * n:5 + 2 * n]
        step_id = pl.program_id(0) * nsec + pl.program_id(1)
        if n:
            gather = _ShardGather(refs[3:3 + n], refs[5 + 2 * n:5 + 3 * n], *refs[5 + 3 * n:])
            pl.when(step_id == 0)(gather.send)
            pl.when(step_id == n_steps - 1)(gather.forward)
        later2, _, _, qrow, kcol, eye = _wide_consts(T, W)
        blk = lax.broadcasted_iota(jnp.int32, (nq, T), 0)
        first_q = lax.broadcasted_iota(jnp.int32, (T, 2 * HEAD_DIM), 1) < HEAD_DIM
        first_k = lax.broadcasted_iota(jnp.int32, (W, 2 * HEAD_DIM), 1) < HEAD_DIM

        def lanes(j):
            return slice(j * 2 * HEAD_DIM, (j + 1) * 2 * HEAD_DIM)

        def step(qms, k0, st, mask):
            accs, runs = st
            parts, lbs, sums = [], [], []
            for g in range(G):
                _, lb, l1, _ = _att_scores(qms[g], k_ref[pl.ds(k0, W), lanes(g // 2)], mask, prescaled=True)
                parts.append(jnp.concatenate(_split2(l1), axis=1))
                lbs.append(lb)
                sums.append(jnp.sum(l1, axis=1, keepdims=True))
            suf = jnp.dot(jnp.concatenate(parts, axis=0), later2, preferred_element_type=F32)
            new_accs, new_runs = [], []
            for j in range(P):
                vms = _pair_masks(v_ref[pl.ds(k0, W), lanes(j)], first_k)
                acc = accs[j]
                for h in range(2):
                    g = 2 * j + h
                    w = jnp.exp(lbs[g] + suf[g * T:(g + 1) * T] + runs[g])
                    acc = acc + jnp.dot(w.astype(BF16), vms[h], preferred_element_type=F32)
                    new_runs.append(runs[g] + sums[g])
                new_accs.append(acc)
            return tuple(new_accs), tuple(new_runs)

        def qblock(i, totals):
            q0 = pl.multiple_of(i * T, T)
            qms = []
            for j in range(P):
                qms.extend(_pair_masks(q_ref[pl.ds(q0, T), lanes(j)] * (HEAD_DIM ** -0.5), first_q))
            half = jnp.right_shift(i, 1)
            last = half * W
            k_last = pl.multiple_of(last, W)
            mask = (k_last + kcol) < (q0 + qrow)
            st = (tuple(jnp.zeros((T, 2 * HEAD_DIM), F32) for _ in range(P)),
                  tuple(jnp.zeros((T, 1), F32) for _ in range(G)))
            st = step(qms, k_last, st, mask)

            def kblock(jj, st):
                return step(qms, pl.multiple_of(last - jj * W, W), st, None)

            accs, runs = lax.fori_loop(1, half + 1, kblock, st)
            for j in range(P):
                o_ref[pl.ds(q0, T), lanes(j)] = accs[j].astype(o_ref.dtype)
            return tuple(jnp.where(blk == i, _col_to_row(runs[g], eye), totals[g]) for g in range(G))

        totals = lax.fori_loop(0, nq, qblock, tuple(jnp.zeros((nq, T), F32) for _ in range(G)))
        for g in range(G):
            l_ref[g] = totals[g]
        if n:
            pl.when(step_id == n_steps - 1)(gather.finish)

    def cols(section):
        return pl.BlockSpec((S, LW), lambda b, h: (b, section * nsec + h))

    lspec = pl.BlockSpec((None, G, nq, T), lambda b, h: (b, h, 0, 0))
    dma = pltpu.SemaphoreType.DMA
    return pl.pallas_call(
        body, name=name,
        out_shape=[jax.ShapeDtypeStruct((B * S, D), BF16), jax.ShapeDtypeStruct((B, H, nq, T), F32)]
        + [jax.ShapeDtypeStruct(s.shape, s.dtype) for s in slots],
        grid=(B, nsec), in_specs=[cols(0), cols(1), cols(2)] + _any_specs(2 * n),
        out_specs=[cols(0), lspec] + _any_specs(n),
        input_output_aliases={3 + n + i: 2 + i for i in range(n)},
        scratch_shapes=[dma((3 * n,))] * 4 if n else [],
        compiler_params=_params(("arbitrary", "arbitrary")),
    )(qkv, qkv, qkv, *shards, *slots)


def attn_bwd_pairs(qkv, ltot, do, partials=(), *, B, S, name):
    D = qkv.shape[1] // 3
    H = D // HEAD_DIM
    T = ATT_BLOCK
    W = 2 * T
    nq = S // T
    scale = HEAD_DIM ** -0.5
    G = _tile(H, (ATT_HEADS_BWD, 2))
    P = G // 2
    LW = 2 * HEAD_DIM * P
    nsec = D // LW
    n = len(partials)
    n_steps = B * nsec

    def body(*refs):
        q_ref, k_ref, v_ref, l_ref, do_ref = refs[:5]
        d_ref = refs[5 + n]
        dk_acc, dv_acc = refs[6 + 2 * n:8 + 2 * n]
        step_id = pl.program_id(0) * nsec + pl.program_id(1)
        if n:
            scatter = _ChipScatter(refs[5:5 + n], refs[6 + n:6 + 2 * n], *refs[8 + 2 * n:])
            pl.when(step_id == 0)(scatter.send)
        _, upto2, before2, qrow, kcol, eye = _wide_consts(T, W)
        blk = lax.broadcasted_iota(jnp.int32, (nq, T), 0)
        first_q = lax.broadcasted_iota(jnp.int32, (T, 2 * HEAD_DIM), 1) < HEAD_DIM
        first_k = lax.broadcasted_iota(jnp.int32, (W, 2 * HEAD_DIM), 1) < HEAD_DIM
        dk_acc[...] = jnp.zeros_like(dk_acc)
        dv_acc[...] = jnp.zeros_like(dv_acc)

        def lanes(j):
            return slice(j * 2 * HEAD_DIM, (j + 1) * 2 * HEAD_DIM)

        def step(qms, doms, tots, k0, st, mask):
            dqs, runs_l, runs_d = st
            sc = []
            for g in range(G):
                z, lb, l1, e = _att_scores(qms[g], k_ref[pl.ds(k0, W), lanes(g // 2)], mask, prescaled=True)
                inv = 1.0 / (1.0 + e)
                small = e * inv
                pos = z >= 0.0
                beta = jnp.where(pos, inv, small)
                omb = jnp.where(pos, small, inv)
                if mask is not None:
                    beta = jnp.where(mask, beta, 0.0)
                dw = _nt(doms[g], v_ref[pl.ds(k0, W), lanes(g // 2)])
                sc.append((lb, jnp.concatenate(_split2(l1), axis=1), jnp.sum(l1, axis=1, keepdims=True), dw, beta, omb))
            pre = jnp.dot(jnp.concatenate([s[1] for s in sc], axis=0), upto2, preferred_element_type=F32)
            dlws = []
            for j in range(P):
                dv = None
                for h in range(2):
                    g = 2 * j + h
                    w = jnp.exp(sc[g][0] + (tots[g] - (pre[g * T:(g + 1) * T] + runs_l[g])))
                    t = _tn(w.astype(BF16), doms[g])
                    dv = t if dv is None else dv + t
                    dlws.append(sc[g][3] * w)
                dv_acc[j, pl.ds(k0, W), :] += dv
            pre_d = jnp.dot(jnp.concatenate([jnp.concatenate(_split2(d), axis=1) for d in dlws], axis=0), before2,
                            preferred_element_type=F32)
            new_dqs, new_l, new_d = [], [], []
            for j in range(P):
                kms = _pair_masks(k_ref[pl.ds(k0, W), lanes(j)], first_k)
                dq, dk = dqs[j], None
                for h in range(2):
                    g = 2 * j + h
                    _, _, rowsum, _, beta, omb = sc[g]
                    dl1 = pre_d[g * T:(g + 1) * T] + runs_d[g]
                    dz = (dlws[g] * omb - dl1 * beta).astype(BF16)
                    dq = dq + jnp.dot(dz, kms[h], preferred_element_type=F32)
                    t = _tn(dz, qms[g])
                    dk = t if dk is None else dk + t
                    new_l.append(runs_l[g] + rowsum)
                    new_d.append(runs_d[g] + jnp.sum(dlws[g], axis=1, keepdims=True))
                dk_acc[j, pl.ds(k0, W), :] += dk
                new_dqs.append(dq)
            return tuple(new_dqs), tuple(new_l), tuple(new_d)

        def qblock(i, carry0):
            q0 = pl.multiple_of(i * T, T)
            qms, doms = [], []
            for j in range(P):
                qms.extend(_pair_masks(q_ref[pl.ds(q0, T), lanes(j)] * scale, first_q))
                doms.extend(_pair_masks(do_ref[pl.ds(q0, T), lanes(j)], first_q))
            tots = [_row_to_col(jnp.sum(jnp.where(blk == i, l_ref[g], 0.0), axis=0, keepdims=True), eye)
                    for g in range(G)]
            z1 = tuple(jnp.zeros((T, 1), F32) for _ in range(G))
            st = (tuple(jnp.zeros((T, 2 * HEAD_DIM), F32) for _ in range(P)), z1, z1)

            def kblock(j, st):
                return step(qms, doms, tots, pl.multiple_of(j * W, W), st, None)

            half = jnp.right_shift(i, 1)
            st = lax.fori_loop(0, half, kblock, st)
            k_last = pl.multiple_of(half * W, W)
            dqs, _, _ = step(qms, doms, tots, k_last, st, (k_last + kcol) < (q0 + qrow))
            for j in range(P):
                d_ref[0, pl.ds(q0, T), lanes(j)] = (dqs[j] * scale).astype(d_ref.dtype)
            return carry0

        lax.fori_loop(0, nq, qblock, 0)
        for j in range(P):
            d_ref[1, :, lanes(j)] = dk_acc[j].astype(d_ref.dtype)
            d_ref[2, :, lanes(j)] = dv_acc[j].astype(d_ref.dtype)
        if n:
            pl.when(step_id == n_steps - 1)(scatter.finish)

    def cols(section):
        return pl.BlockSpec((S, LW), lambda b, h: (b, section * nsec + h))

    lspec = pl.BlockSpec((None, G, nq, T), lambda b, h: (b, h, 0, 0))
    dma = pltpu.SemaphoreType.DMA
    return pl.pallas_call(
        body, name=name,
        out_shape=[jax.ShapeDtypeStruct((3, B * S, D), BF16)]
        + [jax.ShapeDtypeStruct((N_CHIPS - 1,) + a.shape[1:], a.dtype) for a in partials],
        grid=(B, nsec),
        in_specs=[cols(0), cols(1), cols(2), lspec, cols(0)] + _any_specs(n),
        out_specs=[pl.BlockSpec((3, S, LW), lambda b, h: (0, b, h))] + _any_specs(n),
        scratch_shapes=[pltpu.VMEM((P, S, 2 * HEAD_DIM), F32), pltpu.VMEM((P, S, 2 * HEAD_DIM), F32)]
        + ([dma((3 * n,)), dma((3 * n,))] if n else []),
        compiler_params=_params(("arbitrary", "arbitrary")),
    )(qkv, qkv, qkv, ltot, do, *partials)


def _cmul(ar, ai, br, bi):
    return ar * br - ai * bi, ar * bi + ai * br


def _cpow(lr, li, n):
    rr, ri = None, None
    br, bi = lr, li
    while n:
        if n & 1:
            rr, ri = (br, bi) if rr is None else _cmul(rr, ri, br, bi)
        n >>= 1
        if n:
            br, bi = _cmul(br, bi, br, bi)
    return rr, ri


def _ssm_scan(sr, si, lr, li, n_steps, reverse):
    W = sr.shape[1]
    R = SEGMENTS
    lim = -li if reverse else li
    zero = jnp.zeros((R, W), F32)

    def row(k):
        i = (n_steps - 1 - k) if reverse else k
        return pl.multiple_of(i * R, R)

    def local(k, st):
        cr, ci = st
        r0 = row(k)
        pr, pi = _cmul(lr, lim, cr, ci)
        nr = pr + sr[pl.ds(r0, R), :]
        ni = pi + si[pl.ds(r0, R), :]
        sr[pl.ds(r0, R), :] = nr
        si[pl.ds(r0, R), :] = ni
        return nr, ni

    er, ei = lax.fori_loop(0, n_steps, local, (zero, zero), unroll=SCAN_UNROLL)
    lnr, lni = _cpow(lr, lim, n_steps)
    rows = lax.broadcasted_iota(jnp.int32, (R, W), 0)
    cr, ci = zero, zero
    for step in range(1, R):
        tr, ti = _cmul(lnr, lni, cr, ci)
        tr, ti = tr + er, ti + ei
        if reverse:
            seg = R - 1 - step
            tr, ti = pltpu.roll(tr, R - 1, 0), pltpu.roll(ti, R - 1, 0)
        else:
            seg = step
            tr, ti = pltpu.roll(tr, 1, 0), pltpu.roll(ti, 1, 0)
        cr = jnp.where(rows == seg, tr, cr)
        ci = jnp.where(rows == seg, ti, ci)

    def fix(k, st):
        pr, pi = st
        r0 = row(k)
        ar, ai = _cmul(pr, pi, cr, ci)
        sr[pl.ds(r0, R), :] += ar
        si[pl.ds(r0, R), :] += ai
        return _cmul(lr, lim, pr, pi)

    lax.fori_loop(0, n_steps, fix, (lr, lim), unroll=SCAN_UNROLL)
    return cr, ci


def _ssm_specs(S, W):
    CH = GROUPS_PER_BLOCK * SSM_GROUP
    return dict(
        rows=pl.BlockSpec((S, CH), lambda b, j: (b, j)),
        b=pl.BlockSpec((None, CH, W), lambda b, j: (j, 0, 0)),
        c=pl.BlockSpec((None, W, CH), lambda b, j: (j, 0, 0)),
        lam=pl.BlockSpec((None, SEGMENTS, W), lambda b, j: (j, 0, 0)),
        vec=pl.BlockSpec((1, CH), lambda b, j: (0, j)),
    )


def ssm_fwd(u, bre, bim, cre, cim, lr8, li8, dsk, *, B, S, name):
    D = u.shape[1]
    J, CH, W = bre.shape
    n_steps = S // SEGMENTS
    sp = _ssm_specs(S, W)

    def body(u_ref, bre_ref, bim_ref, cre_ref, cim_ref, lr_ref, li_ref, dsk_ref, y_ref, sr, si):
        u = u_ref[...]
        ub = u.astype(BF16)
        sr[...] = jnp.dot(ub, bre_ref[...], preferred_element_type=F32)
        si[...] = jnp.dot(ub, bim_ref[...], preferred_element_type=F32)
        _ssm_scan(sr, si, lr_ref[...], li_ref[...], n_steps, False)
        y = jnp.dot(sr[...].astype(BF16), cre_ref[...], preferred_element_type=F32)
        y = y - jnp.dot(si[...].astype(BF16), cim_ref[...], preferred_element_type=F32)
        y_ref[...] = y + dsk_ref[...] * u

    return pl.pallas_call(
        body, name=name, out_shape=jax.ShapeDtypeStruct((B * S, D), F32), grid=(B, J),
        in_specs=[sp["rows"], sp["b"], sp["b"], sp["c"], sp["c"], sp["lam"], sp["lam"], sp["vec"]],
        out_specs=sp["rows"],
        scratch_shapes=[pltpu.VMEM((S, W), F32), pltpu.VMEM((S, W), F32)],
        compiler_params=_params(("parallel", "parallel")),
    )(u, bre, bim, cre, cim, lr8, li8, dsk)


def ssm_bwd(u, dy, bre, bim, cre, cim, lr8, li8, dsk, *, B, S, name):
    D = u.shape[1]
    J, CH, W = bre.shape
    n_steps = S // SEGMENTS
    sp = _ssm_specs(S, W)

    def body(u_ref, dy_ref, bre_ref, bim_ref, cre_ref, cim_ref, lr_ref, li_ref, dsk_ref,
             du_ref, dbre_ref, dbim_ref, dcre_ref, dcim_ref, dlr_ref, dli_ref, ddsk_ref, sr, si, ar, ai):
        u = u_ref[...]
        dy = dy_ref[...]
        ub = u.astype(BF16)
        dyb = dy.astype(BF16)
        lr, li = lr_ref[...], li_ref[...]
        sr[...] = jnp.dot(ub, bre_ref[...], preferred_element_type=F32)
        si[...] = jnp.dot(ub, bim_ref[...], preferred_element_type=F32)
        cr, ci = _ssm_scan(sr, si, lr, li, n_steps, False)
        ar[...] = _nt(dyb, cre_ref[...])
        ai[...] = -_nt(dyb, cim_ref[...])
        _ssm_scan(ar, ai, lr, li, n_steps, True)

        def dlam(k, st):
            dr, di = st
            r0 = pl.multiple_of((k + 1) * SEGMENTS, SEGMENTS)
            p0 = pl.multiple_of(k * SEGMENTS, SEGMENTS)
            pr, pi = sr[pl.ds(p0, SEGMENTS), :], si[pl.ds(p0, SEGMENTS), :]
            xr, xi = ar[pl.ds(r0, SEGMENTS), :], ai[pl.ds(r0, SEGMENTS), :]
            return dr + pr * xr + pi * xi, di + pr * xi - pi * xr

        xr, xi = ar[0:SEGMENTS, :], ai[0:SEGMENTS, :]
        dr, di = lax.fori_loop(0, n_steps - 1, dlam, (cr * xr + ci * xi, cr * xi - ci * xr), unroll=SCAN_UNROLL)
        dlr_ref[...] = dr
        dli_ref[...] = di
        arb = ar[...].astype(BF16)
        aib = ai[...].astype(BF16)
        du_ref[...] = _nt(arb, bre_ref[...]) + _nt(aib, bim_ref[...]) + dsk_ref[...] * dy
        dbre_ref[...] = _tn(ub, arb)
        dbim_ref[...] = _tn(ub, aib)
        dcre_ref[...] = _tn(sr[...].astype(BF16), dyb)
        dcim_ref[...] = -_tn(si[...].astype(BF16), dyb)
        ddsk_ref[...] = jnp.sum(dy * u, axis=0, keepdims=True)

    def per(shape):
        return pl.BlockSpec((None, None) + shape, lambda b, j: (b, j, 0, 0))

    return pl.pallas_call(
        body, name=name,
        out_shape=[jax.ShapeDtypeStruct((B * S, D), F32),
                   jax.ShapeDtypeStruct((B, J, CH, W), F32), jax.ShapeDtypeStruct((B, J, CH, W), F32),
                   jax.ShapeDtypeStruct((B, J, W, CH), F32), jax.ShapeDtypeStruct((B, J, W, CH), F32),
                   jax.ShapeDtypeStruct((B, J, SEGMENTS, W), F32), jax.ShapeDtypeStruct((B, J, SEGMENTS, W), F32),
                   jax.ShapeDtypeStruct((B, J, 1, CH), F32)],
        grid=(B, J),
        in_specs=[sp["rows"], sp["rows"], sp["b"], sp["b"], sp["c"], sp["c"], sp["lam"], sp["lam"], sp["vec"]],
        out_specs=[sp["rows"], per((CH, W)), per((CH, W)), per((W, CH)), per((W, CH)), per((SEGMENTS, W)),
                   per((SEGMENTS, W)),
                   per((1, CH))],
        scratch_shapes=[pltpu.VMEM((S, W), F32)] * 4,
        compiler_params=_params(("parallel", "parallel")),
    )(u, dy, bre, bim, cre, cim, lr8, li8, dsk)


def _ssm_discretize(a_re, a_im, log_dt, b_re, b_im):
    dt = jnp.exp(log_dt)[:, None]
    er = jnp.exp(a_re * dt)
    lr = er * jnp.cos(a_im * dt)
    li = er * jnp.sin(a_im * dt)
    den = a_re * a_re + a_im * a_im
    fr = ((lr - 1.0) * a_re + li * a_im) / den
    fi = (li * a_re - (lr - 1.0) * a_im) / den
    bbr = fr[..., None] * b_re - fi[..., None] * b_im
    bbi = fr[..., None] * b_im + fi[..., None] * b_re
    return lr, li, bbr, bbi


def _block_diag_in(m):
    G, P, H = m.shape
    J = G // GROUPS_PER_BLOCK
    m = m.reshape(J, GROUPS_PER_BLOCK, P, H).transpose(0, 1, 3, 2)
    eye = jnp.eye(GROUPS_PER_BLOCK, dtype=m.dtype)
    out = m[:, :, :, None, :] * eye[None, :, None, :, None]
    return out.reshape(J, GROUPS_PER_BLOCK * H, GROUPS_PER_BLOCK * P)


def _block_diag_in_grad(d, G, P, H):
    J = G // GROUPS_PER_BLOCK
    d = d.reshape(J, GROUPS_PER_BLOCK, H, GROUPS_PER_BLOCK, P)
    idx = jnp.arange(GROUPS_PER_BLOCK)
    d = d[:, idx, :, idx, :]
    return d.transpose(1, 0, 3, 2).reshape(G, P, H)


def _block_diag_out(m):
    G, H, P = m.shape
    J = G // GROUPS_PER_BLOCK
    m = m.reshape(J, GROUPS_PER_BLOCK, H, P).transpose(0, 1, 3, 2)
    eye = jnp.eye(GROUPS_PER_BLOCK, dtype=m.dtype)
    out = m[:, :, :, None, :] * eye[None, :, None, :, None]
    return out.reshape(J, GROUPS_PER_BLOCK * P, GROUPS_PER_BLOCK * H)


def _block_diag_out_grad(d, G, H, P):
    J = G // GROUPS_PER_BLOCK
    d = d.reshape(J, GROUPS_PER_BLOCK, P, GROUPS_PER_BLOCK, H)
    idx = jnp.arange(GROUPS_PER_BLOCK)
    d = d[:, idx, :, idx, :]
    return d.transpose(1, 0, 3, 2).reshape(G, H, P)


def _interleave(a, B, S):
    L = S // SEGMENTS
    return a.reshape(B, SEGMENTS, L, a.shape[-1]).transpose(0, 2, 1, 3).reshape(B * S, a.shape[-1])


def _deinterleave(a, B, S):
    L = S // SEGMENTS
    return a.reshape(B, L, SEGMENTS, a.shape[-1]).transpose(0, 2, 1, 3).reshape(B * S, a.shape[-1])


def _adamw_math(w, g, m, v):
    m = ADAM_B1 * m + (1.0 - ADAM_B1) * g
    v = ADAM_B2 * v + (1.0 - ADAM_B2) * (g * g)
    m_hat = m / (1.0 - ADAM_B1 ** ADAM_STEP)
    v_hat = v / (1.0 - ADAM_B2 ** ADAM_STEP)
    delta = -ADAM_LR * (m_hat / (jnp.sqrt(v_hat) + ADAM_EPS) + ADAM_WD * w)
    return delta, m, v


def adamw(w, g, m, v, *, name):
    R, C = w.shape
    tr = _tile(R, (max(8, (1 << 18) // C // 8 * 8), 256, 128, 64, 32, 16, 8))

    def body(w_ref, g_ref, m_ref, v_ref, d_ref, nm_ref, nv_ref):
        d, nm, nv = _adamw_math(w_ref[...], g_ref[...], m_ref[...], v_ref[...])
        d_ref[...] = d
        nm_ref[...] = nm
        nv_ref[...] = nv

    spec = pl.BlockSpec((tr, C), lambda i: (i, 0))
    shp = jax.ShapeDtypeStruct((R, C), F32)
    return pl.pallas_call(
        body, name=name, out_shape=[shp, shp, shp], grid=(R // tr,), in_specs=[spec] * 4, out_specs=[spec] * 3,
        compiler_params=_params(("parallel",)),
    )(w, g, m, v)


def sum_leading(a, *, name, out_dtype=F32):
    n, R, C = a.shape
    tr = _tile(R, (256, 128, 64, 32, 16, 8))

    def body(a_ref, o_ref):
        acc = a_ref[0].astype(F32)
        for i in range(1, n):
            acc = acc + a_ref[i].astype(F32)
        o_ref[...] = acc.astype(o_ref.dtype)

    return pl.pallas_call(
        body, name=name, out_shape=jax.ShapeDtypeStruct((R, C), out_dtype), grid=(R // tr,),
        in_specs=[pl.BlockSpec((n, tr, C), lambda i: (0, i, 0))], out_specs=pl.BlockSpec((tr, C), lambda i: (i, 0)),
        compiler_params=_params(("parallel",)),
    )(a)


def _any_specs(n):
    return [pl.BlockSpec(memory_space=pl.ANY) for _ in range(n)]


def _coords():
    return lax.axis_index("x"), lax.axis_index("y"), lax.axis_index("c")


def _flip(v, bit):
    return (v + bit) % 2


def all_gather8(a, *, name):
    shape = a.shape

    def body(a_ref, o_ref, send_sems, recv_sems, local_sem):
        x, y, c = _coords()
        me = 4 * x + 2 * y + c
        mine = pltpu.make_async_copy(a_ref, o_ref.at[me], local_sem)
        mine.start()
        sends = []
        for k in range(1, N_DEV):
            peer = (_flip(x, (k >> 2) & 1), _flip(y, (k >> 1) & 1), _flip(c, k & 1))
            cp = pltpu.make_async_remote_copy(a_ref, o_ref.at[me], send_sems.at[k - 1], recv_sems.at[k - 1],
                                              device_id=peer, device_id_type=MESH)
            cp.start()
            sends.append(cp)
        for k in range(1, N_DEV):
            px, py, pc = _flip(x, (k >> 2) & 1), _flip(y, (k >> 1) & 1), _flip(c, k & 1)
            src = 4 * px + 2 * py + pc
            pltpu.make_async_remote_copy(a_ref, o_ref.at[src], send_sems.at[k - 1], recv_sems.at[k - 1],
                                         device_id=(px, py, pc), device_id_type=MESH).wait_recv()
        for cp in sends:
            cp.wait_send()
        mine.wait()

    return pl.pallas_call(
        body, name=name, out_shape=jax.ShapeDtypeStruct((N_DEV,) + shape, a.dtype),
        in_specs=_any_specs(1), out_specs=pl.BlockSpec(memory_space=pl.ANY),
        scratch_shapes=[pltpu.SemaphoreType.DMA((N_DEV - 1,)), pltpu.SemaphoreType.DMA((N_DEV - 1,)),
                        pltpu.SemaphoreType.DMA(())],
    )(a)


def _chip_of(x, y, p):
    px, py = _flip(x, (p >> 1) & 1), _flip(y, p & 1)
    return 2 * px + py, px, py


class _ShardGather:
    def __init__(self, ins, outs, ici_send, ici_recv, d2d_send, d2d_recv):
        self.ins, self.outs = ins, outs
        self.sems = ici_send, ici_recv, d2d_send, d2d_recv
        self.x, self.y, self.c = _coords()
        self.me = 2 * self.x + self.y

    def _ici(self, i, p, slot):
        half = self.ins[i].shape[0] // 2
        rows = pl.ds(self.c * half, half)
        _, px, py = _chip_of(self.x, self.y, p)
        s = i * 3 + p - 1
        return pltpu.make_async_remote_copy(self.ins[i].at[rows], self.outs[i].at[slot, rows], self.sems[0].at[s],
                                            self.sems[1].at[s], device_id=(px, py, self.c), device_id_type=MESH)

    def _d2d(self, i, p, mine):
        half = self.ins[i].shape[0] // 2
        rows = pl.ds((self.c if mine else 1 - self.c) * half, half)
        src, _, _ = _chip_of(self.x, self.y, p)
        s = i * 3 + p - 1
        part = self.outs[i].at[src, rows]
        return pltpu.make_async_remote_copy(part, part, self.sems[2].at[s], self.sems[3].at[s],
                                            device_id=(self.x, self.y, 1 - self.c), device_id_type=MESH)

    def _each(self):
        return [(i, p) for i in range(len(self.ins)) for p in range(1, N_CHIPS)]

    def send(self):
        for i, p in self._each():
            self._ici(i, p, self.me).start()

    def forward(self):
        for i, p in self._each():
            self._ici(i, p, _chip_of(self.x, self.y, p)[0]).wait_recv()
            self._d2d(i, p, True).start()

    def finish(self):
        for i, p in self._each():
            self._d2d(i, p, False).wait_recv()
        for i, p in self._each():
            self._ici(i, p, self.me).wait_send()
            self._d2d(i, p, True).wait_send()


def gather_chip_shards(arrs, remote, *, name):
    n = len(arrs)
    far = [i for i in range(n) if remote[i]]

    def body(*refs):
        ins, outs = refs[:n], refs[n:2 * n]
        ici_send, ici_recv, d2d_send, d2d_recv, local_sems = refs[2 * n:2 * n + 5]
        bufs = refs[2 * n + 5:]
        me = 2 * lax.axis_index("x") + lax.axis_index("y")
        loads = []
        for i in range(n):
            cp = pltpu.make_async_copy(ins[i], bufs[i], local_sems.at[i])
            cp.start()
            loads.append(cp)
        gather = _ShardGather([ins[i] for i in far], [outs[i] for i in far], ici_send, ici_recv, d2d_send, d2d_recv)
        gather.send()
        stores = []
        for i in range(n):
            loads[i].wait()
            cp = pltpu.make_async_copy(bufs[i], outs[i].at[me], local_sems.at[i])
            cp.start()
            stores.append(cp)
        gather.forward()
        gather.finish()
        for cp in stores:
            cp.wait()

    dma = pltpu.SemaphoreType.DMA
    m = 3 * len(far)
    return pl.pallas_call(
        body, name=name,
        out_shape=[jax.ShapeDtypeStruct((N_CHIPS,) + a.shape, a.dtype) for a in arrs],
        in_specs=_any_specs(n), out_specs=_any_specs(n),
        scratch_shapes=[dma((m,)), dma((m,)), dma((m,)), dma((m,)), dma((n,))]
        + [pltpu.VMEM(a.shape, a.dtype) for a in arrs],
        compiler_params=pltpu.CompilerParams(vmem_limit_bytes=V7X_VMEM_LIMIT),
    )(*arrs)


def swap_halves(arrs, *, name):
    n = len(arrs)

    def body(*refs):
        ins, outs = refs[:n], refs[n:2 * n]
        send_sems, recv_sems = refs[2 * n:]
        x, y, c = _coords()
        cps = []
        for i in range(n):
            half = ins[i].shape[1] // 2
            cp = pltpu.make_async_remote_copy(ins[i].at[:, pl.ds((1 - c) * half, half)], outs[i], send_sems.at[i],
                                              recv_sems.at[i], device_id=(x, y, 1 - c), device_id_type=MESH)
            cp.start()
            cps.append(cp)
        for cp in cps:
            cp.wait()

    dma = pltpu.SemaphoreType.DMA
    return pl.pallas_call(
        body, name=name,
        out_shape=[jax.ShapeDtypeStruct((N_CHIPS, a.shape[1] // 2, a.shape[2]), a.dtype) for a in arrs],
        in_specs=_any_specs(n), out_specs=_any_specs(n), scratch_shapes=[dma((n,)), dma((n,))],
    )(*arrs)


def add_half(g, other, c_idx, *, name, out_dtype):
    _, R, C = g.shape
    half = R // 2
    tr = _tile(half, (256, 128, 64, 32, 16, 8))
    nt = half // tr

    def body(c_ref, g_ref, o_ref, out_ref):
        out_ref[...] = (g_ref[...].astype(F32) + o_ref[...].astype(F32)).astype(out_ref.dtype)

    return pl.pallas_call(
        body, name=name, out_shape=jax.ShapeDtypeStruct((N_CHIPS, half, C), out_dtype),
        grid_spec=pltpu.PrefetchScalarGridSpec(
            num_scalar_prefetch=1, grid=(N_CHIPS, nt),
            in_specs=[pl.BlockSpec((None, tr, C), lambda r, t, c_ref: (r, c_ref[0] * nt + t, 0)),
                      pl.BlockSpec((None, tr, C), lambda r, t, c_ref: (r, t, 0))],
            out_specs=pl.BlockSpec((None, tr, C), lambda r, t, c_ref: (r, t, 0))),
        compiler_params=_params(("parallel", "parallel")),
    )(c_idx, g, other)


class _ChipScatter:
    def __init__(self, ins, outs, send_sems, recv_sems):
        self.ins, self.outs, self.send_sems, self.recv_sems = ins, outs, send_sems, recv_sems
        self.x, self.y, self.c = _coords()

    def _copies(self):
        for i in range(len(self.ins)):
            for p in range(1, N_CHIPS):
                dst, px, py = _chip_of(self.x, self.y, p)
                s = i * 3 + p - 1
                yield pltpu.make_async_remote_copy(self.ins[i].at[dst], self.outs[i].at[p - 1], self.send_sems.at[s],
                                                   self.recv_sems.at[s], device_id=(px, py, self.c), device_id_type=MESH)

    def send(self):
        for cp in self._copies():
            cp.start()

    def finish(self):
        for cp in self._copies():
            cp.wait()


def scatter_to_chips(arrs, *, name):
    n = len(arrs)

    def body(*refs):
        scatter = _ChipScatter(refs[:n], refs[n:2 * n], *refs[2 * n:])
        scatter.send()
        scatter.finish()

    dma = pltpu.SemaphoreType.DMA
    return pl.pallas_call(
        body, name=name,
        out_shape=[jax.ShapeDtypeStruct((N_CHIPS - 1,) + a.shape[1:], a.dtype) for a in arrs],
        in_specs=_any_specs(n), out_specs=_any_specs(n), scratch_shapes=[dma((3 * n,)), dma((3 * n,))],
    )(*arrs)


def add_chips(h, got, r_idx, *, name):
    _, R, C = h.shape
    tr = _tile(R, (256, 128, 64, 32, 16, 8))

    def body(r_ref, h_ref, g_ref, out_ref):
        acc = h_ref[...].astype(F32)
        for p in range(N_CHIPS - 1):
            acc = acc + g_ref[p].astype(F32)
        out_ref[...] = acc

    return pl.pallas_call(
        body, name=name, out_shape=jax.ShapeDtypeStruct((R, C), F32),
        grid_spec=pltpu.PrefetchScalarGridSpec(
            num_scalar_prefetch=1, grid=(R // tr,),
            in_specs=[pl.BlockSpec((None, tr, C), lambda t, r_ref: (r_ref[0], t, 0)),
                      pl.BlockSpec((N_CHIPS - 1, tr, C), lambda t, r_ref: (0, t, 0))],
            out_specs=pl.BlockSpec((tr, C), lambda t, r_ref: (t, 0))),
        compiler_params=_params(("parallel",)),
    )(r_idx, h, got)


def join_halves(arrs, *, name):
    n = len(arrs)

    def body(*refs):
        ins, outs = refs[:n], refs[n:2 * n]
        send_sems, recv_sems, local_sems = refs[2 * n:2 * n + 3]
        bufs = refs[2 * n + 3:]
        x, y, c = _coords()
        loads, sends, stores = [], [], []
        for i in range(n):
            cp = pltpu.make_async_copy(ins[i], bufs[i], local_sems.at[i])
            cp.start()
            loads.append(cp)
        for i in range(n):
            half = ins[i].shape[0]
            cp = pltpu.make_async_remote_copy(ins[i], outs[i].at[pl.ds(c * half, half)], send_sems.at[i], recv_sems.at[i],
                                              device_id=(x, y, 1 - c), device_id_type=MESH)
            cp.start()
            sends.append(cp)
        for i in range(n):
            half = ins[i].shape[0]
            loads[i].wait()
            cp = pltpu.make_async_copy(bufs[i], outs[i].at[pl.ds(c * half, half)], local_sems.at[i])
            cp.start()
            stores.append(cp)
        for i in range(n):
            half = ins[i].shape[0]
            pltpu.make_async_remote_copy(ins[i], outs[i].at[pl.ds((1 - c) * half, half)], send_sems.at[i],
                                         recv_sems.at[i], device_id=(x, y, 1 - c), device_id_type=MESH).wait_recv()
        for i in range(n):
            sends[i].wait_send()
            stores[i].wait()

    dma = pltpu.SemaphoreType.DMA
    return pl.pallas_call(
        body, name=name,
        out_shape=[jax.ShapeDtypeStruct((2 * a.shape[0], a.shape[1]), a.dtype) for a in arrs],
        in_specs=_any_specs(n), out_specs=_any_specs(n),
        scratch_shapes=[dma((n,)), dma((n,)), dma((n,))] + [pltpu.VMEM(a.shape, a.dtype) for a in arrs],
        compiler_params=pltpu.CompilerParams(vmem_limit_bytes=V7X_VMEM_LIMIT),
    )(*arrs)


def pair_sums(grads, wire_dtypes, tag):
    c_idx = jnp.reshape(lax.axis_index("c"), (1,)).astype(jnp.int32)
    theirs = swap_halves(grads, name=f"rs_swap_halves_{tag}")
    return [add_half(g, o, c_idx, name=f"rs_add_half_{tag}{i}", out_dtype=wire_dtypes[i])
            for i, (g, o) in enumerate(zip(grads, theirs))]


def chip_sums(pairs, gots, tag):
    r_idx = jnp.reshape(2 * lax.axis_index("x") + lax.axis_index("y"), (1,)).astype(jnp.int32)
    return [add_chips(h, g, r_idx, name=f"rs_add_chips_{tag}{i}") for i, (h, g) in enumerate(zip(pairs, gots))]


def _to_heads(t, B, S):
    return t.reshape(B, S, -1, HEAD_DIM).transpose(0, 2, 1, 3)


def _from_heads(t, B, S):
    return t.transpose(0, 2, 1, 3).reshape(B * S, -1)


def _chip_major(w, axis):
    n = w.shape[axis] // N_CHIPS
    parts = w.reshape(w.shape[:axis] + (N_CHIPS, n) + w.shape[axis + 1:])
    return jnp.moveaxis(parts, axis, 0)


def _from_chip_major(g, axis):
    g = jnp.moveaxis(g, 0, axis)
    return g.reshape(g.shape[:axis] + (g.shape[axis] * g.shape[axis + 1],) + g.shape[axis + 2:])


def kernel(x, c, norm_mix, norm_ffn, w_mod, b_mod, w_qkv, w_o_attn, w_in_ssm, a_re, a_im, log_dt, b_re, b_im, c_re, c_im, d_skip, w_glu, b_glu, w_o_ssm, w_up, conv_w, conv_b, w_down, norm_out, w_fin, b_fin, loss_target, m_norm_mix, m_norm_ffn, m_w_mod, m_b_mod, m_w_qkv, m_w_o_attn, m_w_in_ssm, m_a_re, m_a_im, m_log_dt, m_b_re, m_b_im, m_c_re, m_c_im, m_d_skip, m_w_glu, m_b_glu, m_w_o_ssm, m_w_up, m_conv_w, m_conv_b, m_w_down, m_norm_out, m_w_fin, m_b_fin, v_norm_mix, v_norm_ffn, v_w_mod, v_b_mod, v_w_qkv, v_w_o_attn, v_w_in_ssm, v_a_re, v_a_im, v_log_dt, v_b_re, v_b_im, v_c_re, v_c_im, v_d_skip, v_w_glu, v_b_glu, v_w_o_ssm, v_w_up, v_conv_w, v_conv_b, v_w_down, v_norm_out, v_w_fin, v_b_fin):
    B, S, D = x.shape
    T = B * S
    F2 = conv_b.shape[1]
    F = F2 // 2
    G, P = a_re.shape[1], a_re.shape[2]
    H = b_re.shape[3]
    mx, my, mc = _coords()
    chip = 2 * mx + my
    dev = 4 * mx + 2 * my + mc
    BG = N_DEV * B
    mod_w = w_mod.shape[2]
    fin_w = w_fin.shape[1]

    c_all = all_gather8(c, name="gather_c").reshape(BG, D)
    c_act = silu_rows(c_all, name="silu_c")
    b_mod_mine = lax.dynamic_slice(b_mod, (0, chip * mod_w), (2, mod_w))
    b_fin_mine = lax.dynamic_slice(b_fin, (chip * fin_w,), (fin_w,))
    cond = [matmul(c_act, w_mod[i], bias=b_mod_mine[i], name=f"mod_proj_{i}") for i in range(2)]
    cond.append(matmul(c_act, w_fin, bias=b_fin_mine, name="fin_proj"))
    cond_all = all_gather8(jnp.concatenate(cond, axis=1), name="gather_cond")
    cond_all = cond_all[::2]
    cond_rows = lax.dynamic_slice(cond_all, (0, dev * B, 0), (N_CHIPS, B, cond_all.shape[2]))
    mods = []
    for i in range(2):
        full = cond_rows[:, :, i * mod_w:(i + 1) * mod_w].transpose(1, 0, 2).reshape(B, N_CHIPS * mod_w)
        mods.append([full[:, k * D:(k + 1) * D] for k in range(6)])
    fin = cond_rows[:, :, 2 * mod_w:].transpose(1, 0, 2).reshape(B, N_CHIPS * fin_w)
    sh_f, sc_f = fin[:, :D], fin[:, D:]

    rows1024 = jnp.concatenate([w_o_attn[0], w_in_ssm[0], w_glu[0], w_o_ssm[0], w_down.reshape(-1, D)], axis=0)
    shards = [w_qkv[0].astype(BF16), rows1024.astype(BF16), w_up[0].astype(BF16), w_up[1].astype(BF16)]
    W_qkv, *own_slots = gather_chip_shards(shards, [True, False, False, False], name="gather_weights")
    Dq = D // N_CHIPS
    Fq = F // N_CHIPS
    small =jnp.concatenate([conv_w.reshape(6, -1), jnp.pad(d_skip, ((0, 0), (0, conv_w.shape[2] - Dq))),
                             jnp.pad(b_glu, ((0, 0), (0, conv_w.shape[2] - Dq)))], axis=0)
    small_all = all_gather8(small, name="gather_small")[::2]
    conv_w_full = _from_chip_major(small_all[:, :6].reshape(N_CHIPS, 2, 3, -1), 2)
    d_skip_full = small_all[:, 6, :Dq].reshape(1, D)
    b_glu_full = small_all[:, 7, :Dq].reshape(D)

    x0 = x.reshape(T, D)
    tgt = loss_target.reshape(T, D)

    def ffn_fwd(xprev, y, gate, i):
        sh2, sc2 = mods[i][3], mods[i][4]
        xin, h2 = res_norm_mod_fwd(xprev, y, gate, norm_ffn[i], sh2, sc2, B=B, S=S, name=f"ffn_norm_{i}")
        up = matmul(h2, W_up[i], b_chips=True, name=f"ffn_up_{i}")
        act = conv_gate_fwd(up, conv_w_full[i], conv_b[i:i + 1], B=B, S=S, name=f"ffn_conv_{i}")
        yf = matmul(act, W_down[i], name=f"ffn_down_{i}")
        return xin, yf, (xin, h2, up, act, yf)

    sh1, sc1, g1 = mods[0][0], mods[0][1], mods[0][2]
    h1a = norm_mod_fwd(x0, norm_mix[0], sh1, sc1, B=B, S=S, name="att_norm")
    qkv = matmul(h1a, W_qkv, out_dtype=BF16, b_chips=True, name="att_qkv")
    o2, ltot, g_rows, W_up0, W_up1 = attn_fwd_pairs(qkv, shards[1:], own_slots, B=B, S=S, name="att_fwd")
    W_up = [W_up0, W_up1]
    W_o_attn = g_rows[:, 0 * Dq:1 * Dq].reshape(D, D)
    W_in = g_rows[:, 1 * Dq:2 * Dq].reshape(D, D)
    W_glu = g_rows[:, 2 * Dq:3 * Dq].reshape(D, D)
    W_o_ssm = g_rows[:, 3 * Dq:4 * Dq].reshape(D, D)
    W_down = [g_rows[:, 4 * Dq + i * Fq:4 * Dq + (i + 1) * Fq].reshape(F, D) for i in range(2)]
    ya = matmul(o2, W_o_attn, name="att_out")
    x1, yf0, ffn0 = ffn_fwd(x0, ya, g1, 0)

    lr, li, bbr, bbi = _ssm_discretize(a_re[0], a_im[0], log_dt[0], b_re[0], b_im[0])
    J = G // GROUPS_PER_BLOCK
    Wst = GROUPS_PER_BLOCK * P
    bre_blk = _block_diag_in(bbr).astype(BF16)
    bim_blk = _block_diag_in(bbi).astype(BF16)
    cre_blk = _block_diag_out(c_re[0]).astype(BF16)
    cim_blk = _block_diag_out(c_im[0]).astype(BF16)
    lr8 = jnp.broadcast_to(lr.reshape(J, 1, Wst), (J, SEGMENTS, Wst))
    li8 = jnp.broadcast_to(li.reshape(J, 1, Wst), (J, SEGMENTS, Wst))
    sh1s, sc1s, g1s = mods[1][0], mods[1][1], mods[1][2]
    x2, h1s = res_norm_mod_fwd(x1, yf0, mods[0][5], norm_mix[1], sh1s, sc1s, B=B, S=S, name="ssm_norm")
    h1p = _interleave(h1s, B, S)
    u = matmul(h1p, W_in, name="ssm_in")
    y_ssm = ssm_fwd(u, bre_blk, bim_blk, cre_blk, cim_blk, lr8, li8, d_skip_full, B=B, S=S, name="ssm_scan_fwd")
    zb = gelu_fwd(y_ssm, B=B, S=S, name="ssm_gelu")
    s_glu = matmul(zb, W_glu, bias=b_glu_full, name="ssm_glu_proj")
    gb = glu_fwd(y_ssm, s_glu, B=B, S=S, name="ssm_glu")
    ys_p = matmul(gb, W_o_ssm, name="ssm_out")
    ys = _deinterleave(ys_p, B, S)
    x3, yf1, ffn1 = ffn_fwd(x2, ys, g1s, 1)
    x4 = gate_res_fwd(x3, yf1, mods[1][5], B=B, S=S, name="ffn_res_1")

    dx4, dyf1, loss_p, dsh_f, dsc_f, dnorm_out, dg2_1 = final_loss(x4, tgt, norm_out, sh_f, sc_f, yf1, mods[1][5],
                                                                   B=B, S=S, name="loss_head")
    loss = lax.psum(jnp.sum(loss_p), ("x", "y", "c"))

    def ffn_bwd(dxo, dyf, i, saved, y_prev, gate_prev):
        xin, h2, up, act, yf = saved
        sc2 = mods[i][4]
        dact = matmul(dyf, W_down[i], tb=True, name=f"ffn_down_dx_{i}")
        dW_down = matmul(act, dyf, ta=True, out_dtype=BF16, name=f"ffn_down_dw_{i}")
        dup, dcw, dcb = conv_gate_bwd(up, dact, conv_w_full[i], conv_b[i:i + 1], B=B, S=S, name=f"ffn_conv_bwd_{i}")
        dh2 = matmul(dup, W_up[i], tb=True, b_chips=True, name=f"ffn_up_dx_{i}")
        dW_up = matmul(h2, dup, ta=True, b_chips=True, out_chips=True, out_dtype=BF16, name=f"ffn_up_dw_{i}")
        dxin, dy_prev, dsh2, dsc2, dnf, dgate_prev = norm_mod_bwd_gate(
            dh2, xin, dxo, norm_ffn[i], sc2, y_prev, gate_prev, B=B, S=S, name=f"ffn_norm_bwd_{i}")
        dconv_w = jnp.sum(dcw, axis=0).transpose(1, 0, 2).reshape(3, F2)
        return dxin, dy_prev, dgate_prev, dict(dW_down=dW_down, dW_up=dW_up, dconv_b=jnp.sum(dcb, axis=0).reshape(F2),
                                               dconv_w=dconv_w, dnorm_ffn=jnp.sum(dnf, axis=0), dsh2=dsh2, dsc2=dsc2)

    dx3, dys, dg1s, gf1 = ffn_bwd(dx4, dyf1, 1, ffn1, ys, g1s)
    gf1["dg2"] = dg2_1

    dys_p = _interleave(dys, B, S)
    dgb = matmul(dys_p, W_o_ssm, tb=True, name="ssm_out_dx")
    dW_o_ssm = matmul(gb, dys_p, ta=True, out_dtype=BF16, name="ssm_out_dw")
    ds_glu, dz1, db_glu = glu_bwd1(y_ssm, s_glu, dgb, B=B, S=S, name="ssm_glu_bwd1")
    dz2 = matmul(ds_glu, W_glu, tb=True, name="ssm_glu_dx")
    dW_glu = matmul(zb, ds_glu, ta=True, out_dtype=BF16, name="ssm_glu_dw")
    dy_ssm = glu_bwd2(y_ssm, dz1, dz2, B=B, S=S, name="ssm_glu_bwd2")
    du, dbre, dbim, dcre, dcim, dlr8, dli8, ddsk = ssm_bwd(u, dy_ssm, bre_blk, bim_blk, cre_blk, cim_blk, lr8, li8,
                                                           d_skip_full, B=B, S=S, name="ssm_scan_bwd")
    dub = du.astype(BF16)
    dh1p = matmul(dub, W_in, tb=True, name="ssm_in_dx")
    dW_in = matmul(h1p, dub, ta=True, out_dtype=BF16, name="ssm_in_dw")
    dx2, dyf0, dsh1s, dsc1s, dnm1, dg2_0 = norm_mod_bwd_gate(_deinterleave(dh1p, B, S), x2, dx3, norm_mix[1], sc1s,
                                                             yf0, mods[0][5], B=B, S=S, name="ssm_norm_bwd")
    dlr = jnp.sum(dlr8, axis=(0, 2)).reshape(G, P)
    dli = jnp.sum(dli8, axis=(0, 2)).reshape(G, P)
    dbbr = _block_diag_in_grad(jnp.sum(dbre, axis=0), G, P, H)
    dbbi = _block_diag_in_grad(jnp.sum(dbim, axis=0), G, P, H)
    dc_re = _block_diag_out_grad(jnp.sum(dcre, axis=0), G, H, P)
    dc_im = _block_diag_out_grad(jnp.sum(dcim, axis=0), G, H, P)
    dd_skip = jnp.sum(ddsk, axis=0).reshape(D)

    dx1, dya, dg1, gf0 = ffn_bwd(dx2, dyf0, 0, ffn0, ya, g1)
    gf0["dg2"] = dg2_0

    do2 = matmul(dya, W_o_attn, tb=True, out_dtype=BF16, name="att_out_dx")
    dW_o_attn = matmul(o2, dya, ta=True, out_dtype=BF16, name="att_out_dw")
    g_rows_cm = jnp.concatenate([dW_o_attn.reshape(N_CHIPS, Dq, D), dW_in.reshape(N_CHIPS, Dq, D),
                                 dW_glu.reshape(N_CHIPS, Dq, D), dW_o_ssm.reshape(N_CHIPS, Dq, D),
                                 gf0["dW_down"].reshape(N_CHIPS, Fq, D), gf1["dW_down"].reshape(N_CHIPS, Fq, D)], axis=1)
    pairs_a = pair_sums([g_rows_cm, gf0["dW_up"], gf1["dW_up"]], [BF16, BF16, BF16], "a")
    dqkv, *gots_a = attn_bwd_pairs(qkv, ltot, do2, pairs_a, B=B, S=S, name="att_bwd")
    dh1a = matmul(dqkv, _from_chip_major(W_qkv, 1), tb=True, name="att_qkv_dx")
    dW_qkv = _chip_major(matmul(h1a, dqkv, ta=True, b_chips=True, out_dtype=BF16, name="att_qkv_dw"), 1)
    grad_x, dsh1, dsc1, dnm0 = norm_mod_bwd(dh1a, x0, dx1, norm_mix[0], sc1, B=B, S=S, name="att_norm_bwd")

    dmod_rows = jnp.concatenate([dsh1, dsc1, dg1, gf0["dsh2"], gf0["dsc2"], gf0["dg2"],
                                 dsh1s, dsc1s, dg1s, gf1["dsh2"], gf1["dsc2"], gf1["dg2"], dsh_f, dsc_f], axis=1)
    dmod_all = all_gather8(dmod_rows, name="gather_dmod").reshape(BG, 14 * D)
    grad_w_mod = jnp.stack([
        matmul(c_act, lax.dynamic_slice(dmod_all, (0, i * 6 * D + chip * mod_w), (BG, mod_w)), ta=True,
               name=f"mod_dw_{i}") for i in range(2)])
    grad_w_fin = matmul(c_act, lax.dynamic_slice(dmod_all, (0, 12 * D + chip * fin_w), (BG, fin_w)), ta=True,
                        name="fin_dw")

    parts = [jnp.concatenate([jnp.sum(dnm0, axis=0), jnp.sum(dnm1, axis=0)]),
             jnp.concatenate([gf0["dnorm_ffn"], gf1["dnorm_ffn"]]),
             jnp.sum(dmod_rows[:, :12 * D], axis=0),
             dlr.reshape(-1), dli.reshape(-1), dbbr.reshape(-1), dbbi.reshape(-1), dc_re.reshape(-1), dc_im.reshape(-1),
             dd_skip, jnp.sum(db_glu, axis=0),
             gf0["dconv_w"].reshape(-1), gf1["dconv_w"].reshape(-1), gf0["dconv_b"], gf1["dconv_b"],
             jnp.sum(dnorm_out, axis=0), jnp.sum(dmod_rows[:, 12 * D:], axis=0)]
    sizes = [int(p.shape[0]) for p in parts]
    flat = jnp.concatenate(parts)
    width = 1024
    quantum = N_CHIPS * 16 * width
    padded = -(-flat.shape[0] // quantum) * quantum
    small_cm = jnp.pad(flat, (0, padded - flat.shape[0])).reshape(N_CHIPS, -1, width)

    pairs_b = pair_sums([dW_qkv, small_cm], [BF16, F32], "b")
    gots_b = scatter_to_chips(pairs_b, name="rs_scatter_to_chips")
    r_qkv, r_small, r_rows, r_up0, r_up1 = join_halves(
        chip_sums(pairs_b, gots_b, "b") + chip_sums(pairs_a, gots_a, "a"), name="rs_join_halves")
    grad_w_qkv = r_qkv[None]
    grad_w_o_attn = r_rows[0 * Dq:1 * Dq][None]
    grad_w_in_ssm = r_rows[1 * Dq:2 * Dq][None]
    grad_w_glu = r_rows[2 * Dq:3 * Dq][None]
    grad_w_o_ssm = r_rows[3 * Dq:4 * Dq][None]
    grad_w_down = r_rows[4 * Dq:].reshape(2, Fq, D)
    grad_w_up = jnp.stack([r_up0, r_up1])
    summed = all_gather8(r_small, name="gather_small_grads")[::2].reshape(-1)
    offs = [0]
    for s_ in sizes:
        offs.append(offs[-1] + s_)
    (s_nm, s_nf, s_bmod, s_lr, s_li, s_bbr, s_bbi, s_cre, s_cim, s_dsk, s_bglu, s_cw0, s_cw1, s_cb0, s_cb1, s_no,
     s_bfin) = [summed[offs[i]:offs[i + 1]] for i in range(len(sizes))]
    _, disc_vjp = jax.vjp(_ssm_discretize, a_re[0], a_im[0], log_dt[0], b_re[0], b_im[0])
    ga_re, ga_im, glog_dt, gb_re, gb_im = disc_vjp((s_lr.reshape(G, P), s_li.reshape(G, P), s_bbr.reshape(G, P, H),
                                                    s_bbi.reshape(G, P, H)))
    grad_norm_mix = s_nm.reshape(2, D)
    grad_norm_ffn = s_nf.reshape(2, D)
    grad_b_mod = s_bmod.reshape(2, 6 * D)
    grad_c_re = s_cre.reshape(1, G, H, P)
    grad_c_im = s_cim.reshape(1, G, H, P)
    grad_d_skip = lax.dynamic_slice(s_dsk, (chip * Dq,), (Dq,)).reshape(1, Dq)
    grad_b_glu = lax.dynamic_slice(s_bglu, (chip * Dq,), (Dq,)).reshape(1, Dq)
    cw_full = jnp.stack([s_cw0.reshape(3, F2), s_cw1.reshape(3, F2)])
    grad_conv_w = lax.dynamic_slice(cw_full, (0, 0, chip * (F2 // N_CHIPS)), (2, 3, F2 // N_CHIPS))
    grad_conv_b = jnp.stack([s_cb0, s_cb1])
    grad_norm_out = s_no
    grad_b_fin = s_bfin

    grads = dict(
        norm_mix=grad_norm_mix, norm_ffn=grad_norm_ffn, w_mod=grad_w_mod, b_mod=grad_b_mod, w_qkv=grad_w_qkv,
        w_o_attn=grad_w_o_attn, w_in_ssm=grad_w_in_ssm, a_re=ga_re[None], a_im=ga_im[None], log_dt=glog_dt[None],
        b_re=gb_re[None], b_im=gb_im[None], c_re=grad_c_re, c_im=grad_c_im, d_skip=grad_d_skip, w_glu=grad_w_glu,
        b_glu=grad_b_glu, w_o_ssm=grad_w_o_ssm, w_up=grad_w_up, conv_w=grad_conv_w, conv_b=grad_conv_b,
        w_down=grad_w_down, norm_out=grad_norm_out, w_fin=grad_w_fin, b_fin=grad_b_fin)
    weights = dict(
        norm_mix=norm_mix, norm_ffn=norm_ffn, w_mod=w_mod, b_mod=b_mod, w_qkv=w_qkv, w_o_attn=w_o_attn,
        w_in_ssm=w_in_ssm, a_re=a_re, a_im=a_im, log_dt=log_dt, b_re=b_re, b_im=b_im, c_re=c_re, c_im=c_im,
        d_skip=d_skip, w_glu=w_glu, b_glu=b_glu, w_o_ssm=w_o_ssm, w_up=w_up, conv_w=conv_w, conv_b=conv_b,
        w_down=w_down, norm_out=norm_out, w_fin=w_fin, b_fin=b_fin)
    m_in = dict(
        norm_mix=m_norm_mix, norm_ffn=m_norm_ffn, w_mod=m_w_mod, b_mod=m_b_mod, w_qkv=m_w_qkv, w_o_attn=m_w_o_attn,
        w_in_ssm=m_w_in_ssm, a_re=m_a_re, a_im=m_a_im, log_dt=m_log_dt, b_re=m_b_re, b_im=m_b_im, c_re=m_c_re,
        c_im=m_c_im, d_skip=m_d_skip, w_glu=m_w_glu, b_glu=m_b_glu, w_o_ssm=m_w_o_ssm, w_up=m_w_up, conv_w=m_conv_w,
        conv_b=m_conv_b, w_down=m_w_down, norm_out=m_norm_out, w_fin=m_w_fin, b_fin=m_b_fin)
    v_in = dict(
        norm_mix=v_norm_mix, norm_ffn=v_norm_ffn, w_mod=v_w_mod, b_mod=v_b_mod, w_qkv=v_w_qkv, w_o_attn=v_w_o_attn,
        w_in_ssm=v_w_in_ssm, a_re=v_a_re, a_im=v_a_im, log_dt=v_log_dt, b_re=v_b_re, b_im=v_b_im, c_re=v_c_re,
        c_im=v_c_im, d_skip=v_d_skip, w_glu=v_w_glu, b_glu=v_b_glu, w_o_ssm=v_w_o_ssm, w_up=v_w_up, conv_w=v_conv_w,
        conv_b=v_conv_b, w_down=v_w_down, norm_out=v_norm_out, w_fin=v_w_fin, b_fin=v_b_fin)
    names = list(weights)
    for n_ in names:
        grads[n_] = grads[n_].reshape(weights[n_].shape)

    big = ("w_mod", "w_qkv", "w_o_attn", "w_in_ssm", "w_glu", "w_o_ssm", "w_up", "w_down", "w_fin")
    delta, new_m, new_v = {}, {}, {}
    for n_ in big:
        shp = weights[n_].shape
        two_d = lambda a: a.reshape(-1, shp[-1])
        d_, m_, v_ = adamw(two_d(weights[n_]), two_d(grads[n_]), two_d(m_in[n_]), two_d(v_in[n_]), name=f"adamw_{n_}")
        delta[n_], new_m[n_], new_v[n_] = d_.reshape(shp), m_.reshape(shp), v_.reshape(shp)
    rest = [n_ for n_ in names if n_ not in big]

    def pack(tree):
        f = jnp.concatenate([tree[n_].reshape(-1) for n_ in rest])
        pad_to = -(-f.shape[0] // (8 * width)) * (8 * width)
        return jnp.pad(f, (0, pad_to - f.shape[0]), constant_values=1.0).reshape(-1, width)

    d_, m_, v_ = adamw(pack(weights), pack(grads), pack(m_in), pack(v_in), name="adamw_small")
    off = 0
    for n_ in rest:
        sz = int(math.prod(weights[n_].shape))
        shp = weights[n_].shape
        delta[n_] = d_.reshape(-1)[off:off + sz].reshape(shp)
        new_m[n_] = m_.reshape(-1)[off:off + sz].reshape(shp)
        new_v[n_] = v_.reshape(-1)[off:off + sz].reshape(shp)
        off += sz

    return (loss, grad_x.reshape(B, S, D), *[grads[n_] for n_ in names], *[delta[n_] for n_ in names],
            *[new_m[n_] for n_ in names], *[new_v[n_] for n_ in names])
```

```python
import functools
import math

import jax
import jax.numpy as jnp
from jax import lax
from jax.experimental import pallas as pl
from jax.experimental.pallas import tpu as pltpu

F32 = jnp.float32
BF16 = jnp.bfloat16
MESH = pl.DeviceIdType.MESH

HEAD_DIM = 64
SSM_GROUP = 16
STATE = 64
GROUPS_PER_BLOCK = 8
SEGMENTS = 16
SCAN_UNROLL = 4
EPS = 1e-6
ADAM_LR = 0.001
ADAM_B1 = 0.9
ADAM_B2 = 0.999
ADAM_EPS = 1e-08
ADAM_WD = 0.01
ADAM_STEP = 10
N_CHIPS = 4
N_DEV = 8
V7X_VMEM_LIMIT = 56 * 1024 * 1024
ATT_BLOCK = 128
ATT_HEADS = 8
ATT_HEADS_BWD = 8


def _tile(n, prefs):
    for p in prefs:
        if n % p == 0:
            return p
    return n


def _params(sem, vmem=V7X_VMEM_LIMIT):
    return pltpu.CompilerParams(dimension_semantics=sem, vmem_limit_bytes=vmem)


def matmul(a, b, *, ta=False, tb=False, bias=None, out_dtype=F32, b_chips=False, out_chips=False, name):
    a_parts = a.shape[0] if a.ndim == 3 else 1
    if a_parts > 1:
        assert not ta
        M, K = a.shape[1], a_parts * a.shape[2]
    elif ta:
        K, M = a.shape
    else:
        M, K = a.shape
    b_parts = b.shape[0] if b_chips else 1
    b_rows, b_cols = (b.shape[1], b_parts * b.shape[2]) if b_chips else b.shape
    if tb:
        N, Kb = b_rows, b_cols
    else:
        Kb, N = b_rows, b_cols
    assert K == Kb, (a.shape, b.shape, ta, tb)
    n_cut = math.gcd(N // (N_CHIPS if out_chips else 1), N // (b_parts if not tb else 1))
    k_cut = math.gcd(K // (b_parts if tb else 1), K // a_parts)
    tm = _tile(M, (1024, 1408, 512, 256, 128))
    tn = _tile(n_cut, (1024, 1408, 768, 512, 256, 128))
    tk = k_cut if k_cut <= 2816 else _tile(k_cut, (1024, 512, 256, 128))
    nk = K // tk
    npc = N // N_CHIPS // tn
    npb = N // b_parts // tn
    kpb = K // b_parts // tk
    kpa = K // a_parts // tk
    dims = (((0,) if ta else (1,), (1,) if tb else (0,)), ((), ()))

    def body(*refs):
        a_ref, b_ref = refs[:2]
        bias_ref = refs[2] if bias is not None else None
        o_ref = refs[-2] if nk > 1 else refs[-1]

        def finish(r):
            if bias_ref is not None:
                r = r + bias_ref[...]
            o_ref[...] = r.astype(o_ref.dtype)

        prod = lax.dot_general(a_ref[...].astype(BF16), b_ref[...].astype(BF16), dims, preferred_element_type=F32)
        if nk == 1:
            finish(prod)
            return
        acc_ref = refs[-1]
        k = pl.program_id(2)

        @pl.when(k == 0)
        def _():
            acc_ref[...] = prod

        @pl.when(k > 0)
        def _():
            acc_ref[...] += prod

        @pl.when(k == nk - 1)
        def _():
            finish(acc_ref[...])

    if a_parts > 1:
        a_spec = pl.BlockSpec((None, tm, tk), lambda i, j, k: (lax.div(k, kpa), i, lax.rem(k, kpa)))
    else:
        a_spec = pl.BlockSpec((tk, tm), lambda i, j, k: (k, i)) if ta else pl.BlockSpec((tm, tk), lambda i, j, k: (i, k))
    if not b_chips:
        b_spec = pl.BlockSpec((tn, tk), lambda i, j, k: (j, k)) if tb else pl.BlockSpec((tk, tn), lambda i, j, k: (k, j))
    elif tb:
        b_spec = pl.BlockSpec((None, tn, tk), lambda i, j, k: (lax.div(k, kpb), j, lax.rem(k, kpb)))
    else:
        b_spec = pl.BlockSpec((None, tk, tn), lambda i, j, k: (lax.div(j, npb), k, lax.rem(j, npb)))
    in_specs = [a_spec, b_spec]
    args = [a, b]
    if bias is not None:
        in_specs.append(pl.BlockSpec((1, tn), lambda i, j, k: (0, j)))
        args.append(bias.reshape(1, N).astype(F32))
    if out_chips:
        out_shape = jax.ShapeDtypeStruct((N_CHIPS, M, N // N_CHIPS), out_dtype)
        out_spec = pl.BlockSpec((None, tm, tn), lambda i, j, k: (lax.div(j, npc), i, lax.rem(j, npc)))
    else:
        out_shape = jax.ShapeDtypeStruct((M, N), out_dtype)
        out_spec = pl.BlockSpec((tm, tn), lambda i, j, k: (i, j))
    return pl.pallas_call(
        body, name=name,
        out_shape=out_shape,
        grid=(M // tm, N // tn, nk),
        in_specs=in_specs,
        out_specs=out_spec,
        scratch_shapes=[pltpu.VMEM((tm, tn), F32)] if nk > 1 else [],
        compiler_params=_params(("parallel", "parallel", "arbitrary")),
    )(*args)


def rowwise(fn, tiled, per_seq, glob, out_tiled, out_seq, *, B, S, name, rows=512):
    tm = _tile(S, (rows, 128, 64, 32, 16, 8))
    nt = S // tm
    n_in = len(tiled) + len(per_seq) + len(glob)
    n_ot = len(out_tiled)

    def body(*refs):
        ins = refs[:n_in]
        outs = refs[n_in:]
        vals = fn(*[r[...] for r in ins])
        if not isinstance(vals, (tuple, list)):
            vals = (vals,)
        assert len(vals) == len(outs), (name, len(vals), len(outs))
        for o_ref, v in zip(outs[:n_ot], vals[:n_ot]):
            o_ref[...] = v.astype(o_ref.dtype)
        t = pl.program_id(1)
        for o_ref, v in zip(outs[n_ot:], vals[n_ot:]):
            def first(o_ref=o_ref, v=v):
                o_ref[...] = v.astype(F32)

            def later(o_ref=o_ref, v=v):
                o_ref[...] += v.astype(F32)

            pl.when(t == 0)(first)
            pl.when(t > 0)(later)

    in_specs = [pl.BlockSpec((tm, a.shape[1]), lambda b, t: (b * nt + t, 0)) for a in tiled]
    in_specs += [pl.BlockSpec((None, 1, a.shape[1]), lambda b, t: (b, 0, 0)) for a in per_seq]
    in_specs += [pl.BlockSpec(a.shape, lambda b, t: (0,) * a.ndim) for a in glob]
    out_shape = [jax.ShapeDtypeStruct((B * S, w), dt) for w, dt in out_tiled]
    out_shape += [jax.ShapeDtypeStruct((B, 1, w), F32) for w in out_seq]
    out_specs = [pl.BlockSpec((tm, w), lambda b, t: (b * nt + t, 0)) for w, _ in out_tiled]
    out_specs += [pl.BlockSpec((None, 1, w), lambda b, t: (b, 0, 0)) for w in out_seq]
    res = pl.pallas_call(
        body, name=name, out_shape=out_shape, grid=(B, nt), in_specs=in_specs, out_specs=out_specs,
        compiler_params=_params(("parallel", "arbitrary")),
    )(*tiled, *[a.reshape(B, 1, a.shape[1]) for a in per_seq], *glob)
    res = list(res)
    for i in range(n_ot, len(res)):
        res[i] = res[i].reshape(B, res[i].shape[-1])
    return res


def _rms(x):
    r = lax.rsqrt(jnp.mean(x * x, axis=-1, keepdims=True) + EPS)
    return x * r, r


def norm_mod_fwd(x, g, sh, sc, *, B, S, name):
    def fn(x, sh, sc, g):
        xn, _ = _rms(x)
        return (xn * g) * (1.0 + sc) + sh

    return rowwise(fn, [x], [sh, sc], [g.reshape(1, -1)], [(x.shape[1], BF16)], [], B=B, S=S, name=name)[0]


def _norm_mod_bwd_math(dh, x, sc, g):
    xn, r = _rms(x)
    y = xn * g
    dy = dh * (1.0 + sc)
    dxn = dy * g
    dx = r * (dxn - xn * jnp.mean(dxn * xn, axis=-1, keepdims=True))
    dsh = jnp.sum(dh, axis=0, keepdims=True)
    dsc = jnp.sum(dh * y, axis=0, keepdims=True)
    dg = jnp.sum(dy * xn, axis=0, keepdims=True)
    return dx, dsh, dsc, dg


def norm_mod_bwd(dh, x, dres, g, sc, *, B, S, name):
    D = x.shape[1]

    def fn(dh, x, dres, sc, g):
        dx, dsh, dsc, dg = _norm_mod_bwd_math(dh.astype(F32), x, sc, g)
        return dres + dx, dsh, dsc, dg

    return rowwise(fn, [dh, x, dres], [sc], [g.reshape(1, -1)], [(D, F32)], [D, D, D], B=B, S=S, name=name)


def norm_mod_bwd_gate(dh, x, dres, g, sc, y_prev, gate_prev, *, B, S, name):
    D = x.shape[1]

    def fn(dh, x, dres, y, sc, gate, g):
        dx, dsh, dsc, dg = _norm_mod_bwd_math(dh.astype(F32), x, sc, g)
        dx = dres + dx
        return dx, gate * dx, dsh, dsc, dg, jnp.sum(dx * y, axis=0, keepdims=True)

    return rowwise(fn, [dh, x, dres, y_prev], [sc, gate_prev], [g.reshape(1, -1)], [(D, F32), (D, BF16)],
                   [D, D, D, D], B=B, S=S, name=name)


def gate_res_fwd(x, y, gate, *, B, S, name):
    return rowwise(lambda x, y, g: x + g * y, [x, y], [gate], [], [(x.shape[1], F32)], [], B=B, S=S, name=name)[0]


def res_norm_mod_fwd(x, y, gate, g, sh, sc, *, B, S, name):
    D = x.shape[1]

    def fn(x, y, gate, sh, sc, g):
        x = x + gate * y
        xn, _ = _rms(x)
        return x, (xn * g) * (1.0 + sc) + sh

    return rowwise(fn, [x, y], [gate, sh, sc], [g.reshape(1, -1)], [(D, F32), (D, BF16)], [], B=B, S=S, name=name)


def gate_res_bwd(dx, y, gate, *, B, S, name):
    D = dx.shape[1]

    def fn(dx, y, g):
        return g * dx, jnp.sum(dx * y, axis=0, keepdims=True)

    return rowwise(fn, [dx, y], [gate], [], [(D, BF16)], [D], B=B, S=S, name=name)


def final_loss(x, tgt, g, sh, sc, y_prev, gate_prev, *, B, S, name):
    D = x.shape[1]

    def fn(x, tgt, y_prev, sh, sc, gate, g):
        xn, _ = _rms(x)
        y = (xn * g) * (1.0 + sc) + sh
        err = y - tgt
        loss = 0.5 * jnp.sum(err * err, axis=0, keepdims=True) * (1.0 / D)
        dx, dsh, dsc, dg = _norm_mod_bwd_math(err * (1.0 / D), x, sc, g)
        return dx, gate * dx, loss, dsh, dsc, dg, jnp.sum(dx * y_prev, axis=0, keepdims=True)

    return rowwise(fn, [x, tgt, y_prev], [sh, sc, gate_prev], [g.reshape(1, -1)], [(D, F32), (D, BF16)],
                   [D, D, D, D, D], B=B, S=S, name=name)


def _gelu(y):
    c0 = math.sqrt(2.0 / math.pi)
    t = jnp.tanh(c0 * (y + 0.044715 * (y * y * y)))
    return 0.5 * y * (1.0 + t), t


def _sigmoid(s):
    return 1.0 / (1.0 + jnp.exp(-s))


def gelu_fwd(y, *, B, S, name):
    return rowwise(lambda y: _gelu(y)[0], [y], [], [], [(y.shape[1], BF16)], [], B=B, S=S, name=name)[0]


def glu_fwd(y, s, *, B, S, name):
    return rowwise(lambda y, s: _gelu(y)[0] * _sigmoid(s), [y, s], [], [], [(y.shape[1], BF16)], [], B=B, S=S,
                   name=name)[0]


def glu_bwd1(y, s, dg, *, B, S, name):
    D = y.shape[1]

    def fn(y, s, dg):
        z = _gelu(y)[0]
        sig = _sigmoid(s)
        ds = dg * z * sig * (1.0 - sig)
        return ds, dg * sig, jnp.sum(ds, axis=0, keepdims=True)

    return rowwise(fn, [y, s, dg], [], [], [(D, BF16), (D, F32)], [D], B=B, S=S, name=name)


def glu_bwd2(y, dz1, dz2, *, B, S, name):
    D = y.shape[1]
    c0 = math.sqrt(2.0 / math.pi)

    def fn(y, dz1, dz2):
        _, t = _gelu(y)
        dgelu = 0.5 * (1.0 + t) + 0.5 * y * (1.0 - t * t) * c0 * (1.0 + 3.0 * 0.044715 * y * y)
        return (dz1 + dz2) * dgelu

    return rowwise(fn, [y, dz1, dz2], [], [], [(D, F32)], [], B=B, S=S, name=name)[0]


def silu_rows(c, *, name):
    R, W = c.shape
    return rowwise(lambda c: c * _sigmoid(c), [c], [], [], [(W, F32)], [], B=1, S=R, name=name)[0]


def _shift_down(cur, h6, h7):
    rows = lax.broadcasted_iota(jnp.int32, cur.shape, 0)
    m1 = jnp.where(rows == 0, h7, pltpu.roll(cur, 1, 0))
    m2 = jnp.where(rows == 0, h6, jnp.where(rows == 1, h7, pltpu.roll(cur, 2, 0)))
    return m1, m2


def _conv3(cur, halo_ref, w_ref, has_prev):
    h6 = jnp.where(has_prev, halo_ref[6:7, :], 0.0)
    h7 = jnp.where(has_prev, halo_ref[7:8, :], 0.0)
    m1, m2 = _shift_down(cur, h6, h7)
    return w_ref[2:3, :] * cur + w_ref[1:2, :] * m1 + w_ref[0:1, :] * m2, m1, m2


def _conv_tiles(S, F):
    ts = _tile(S, (1024, 512, 256, 128, 64, 32, 16, 8))
    tn = _tile(F, (256, 128))
    return ts, tn, S // ts, F // tn


def conv_gate_fwd(up, cw, cb, *, B, S, name):
    F = up.shape[1] // 2
    tn = _tile(F, (256, 128))
    nF = F // tn

    def body(g_ref, v_ref, wg_ref, wv_ref, bg_ref, bv_ref, o_ref):
        rows = lax.broadcasted_iota(jnp.int32, (S, tn), 0)

        def conv(x, w_ref):
            x1 = jnp.where(rows >= 1, pltpu.roll(x, 1, 0), 0.0)
            x2 = jnp.where(rows >= 2, pltpu.roll(x, 2, 0), 0.0)
            return w_ref[2:3, :] * x + w_ref[1:2, :] * x1 + w_ref[0:1, :] * x2

        gc = conv(g_ref[...], wg_ref) + bg_ref[...]
        vc = conv(v_ref[...], wv_ref) + bv_ref[...]
        o_ref[...] = (gc * _sigmoid(gc) * vc).astype(o_ref.dtype)

    def cols(off):
        return pl.BlockSpec((S, tn), lambda b, j: (b, j + off))

    def vec(rows, off):
        return pl.BlockSpec((rows, tn), lambda b, j: (0, j + off))

    return pl.pallas_call(
        body, name=name, out_shape=jax.ShapeDtypeStruct((B * S, F), BF16), grid=(B, nF),
        in_specs=[cols(0), cols(nF), vec(3, 0), vec(3, nF), vec(1, 0), vec(1, nF)],
        out_specs=pl.BlockSpec((S, tn), lambda b, j: (b, j)),
        compiler_params=_params(("parallel", "parallel")),
    )(up, up, cw, cw, cb, cb)


def conv_gate_bwd1(up, dact, cw, cb, *, B, S, name):
    F = up.shape[1] // 2
    ts, tn, nts, nF = _conv_tiles(S, F)
    hb = ts // 8

    def body(g_ref, gh_ref, v_ref, vh_ref, da_ref, wg_ref, wv_ref, bg_ref, bv_ref, d_ref, db_ref):
        t = pl.program_id(2)
        has_prev = t > 0
        gc = _conv3(g_ref[...], gh_ref, wg_ref, has_prev)[0] + bg_ref[...]
        vc = _conv3(v_ref[...], vh_ref, wv_ref, has_prev)[0] + bv_ref[...]
        sig = _sigmoid(gc)
        da = da_ref[...]
        dg = da * vc * (sig * (1.0 + gc * (1.0 - sig)))
        dv = da * (gc * sig)
        d_ref[0] = dg
        d_ref[1] = dv
        part = jnp.concatenate([jnp.sum(dg, axis=0, keepdims=True), jnp.sum(dv, axis=0, keepdims=True)], axis=0)

        @pl.when(t == 0)
        def _():
            db_ref[...] = part

        @pl.when(t > 0)
        def _():
            db_ref[...] += part

    def cur(off):
        return pl.BlockSpec((ts, tn), lambda b, j, t: (b * nts + t, j + off))

    def halo(off):
        return pl.BlockSpec((8, tn), lambda b, j, t: (jnp.maximum((b * nts + t) * hb - 1, 0), j + off))

    def vec(rows, off):
        return pl.BlockSpec((rows, tn), lambda b, j, t: (0, j + off))

    return pl.pallas_call(
        body, name=name,
        out_shape=[jax.ShapeDtypeStruct((2, B * S, F), F32), jax.ShapeDtypeStruct((B, 2, F), F32)],
        grid=(B, nF, nts),
        in_specs=[cur(0), halo(0), cur(nF), halo(nF), cur(0), vec(3, 0), vec(3, nF), vec(1, 0), vec(1, nF)],
        out_specs=[pl.BlockSpec((2, ts, tn), lambda b, j, t: (0, b * nts + t, j)),
                   pl.BlockSpec((None, 2, tn), lambda b, j, t: (b, 0, j))],
        compiler_params=_params(("parallel", "parallel", "arbitrary")),
    )(up, up, up, up, dact, cw, cw, cb, cb)


def conv_bwd2(d3, up, cw, *, B, S, name):
    F = up.shape[1] // 2
    ts, tn, nts, nF = _conv_tiles(S, F)
    hb = ts // 8
    last8 = B * S // 8 - 1

    def body(d_ref, da_ref, u_ref, uh_ref, w_ref, o_ref, dw_ref):
        t = pl.program_id(3)
        d = d_ref[...]
        has_next = t < nts - 1
        a0 = jnp.where(has_next, da_ref[0:1, :], 0.0)
        a1 = jnp.where(has_next, da_ref[1:2, :], 0.0)
        rows = lax.broadcasted_iota(jnp.int32, d.shape, 0)
        p1 = jnp.where(rows == ts - 1, a0, pltpu.roll(d, ts - 1, 0))
        p2 = jnp.where(rows == ts - 1, a1, jnp.where(rows == ts - 2, a0, pltpu.roll(d, ts - 2, 0)))
        o_ref[...] = (w_ref[2:3, :] * d + w_ref[1:2, :] * p1 + w_ref[0:1, :] * p2).astype(o_ref.dtype)
        u = u_ref[...]
        has_prev = t > 0
        h6 = jnp.where(has_prev, uh_ref[6:7, :], 0.0)
        h7 = jnp.where(has_prev, uh_ref[7:8, :], 0.0)
        m1, m2 = _shift_down(u, h6, h7)
        part = jnp.concatenate([jnp.sum(d * m2, axis=0, keepdims=True), jnp.sum(d * m1, axis=0, keepdims=True),
                                jnp.sum(d * u, axis=0, keepdims=True)], axis=0)

        @pl.when(t == 0)
        def _():
            dw_ref[...] = part

        @pl.when(t > 0)
        def _():
            dw_ref[...] += part

    return pl.pallas_call(
        body, name=name,
        out_shape=[jax.ShapeDtypeStruct((B * S, 2 * F), BF16), jax.ShapeDtypeStruct((B, 3, 2 * F), F32)],
        grid=(B, 2, nF, nts),
        in_specs=[
            pl.BlockSpec((None, ts, tn), lambda b, g, j, t: (g, b * nts + t, j)),
            pl.BlockSpec((None, 8, tn), lambda b, g, j, t: (g, jnp.minimum((b * nts + t + 1) * hb, last8), j)),
            pl.BlockSpec((ts, tn), lambda b, g, j, t: (b * nts + t, g * nF + j)),
            pl.BlockSpec((8, tn), lambda b, g, j, t: (jnp.maximum((b * nts + t) * hb - 1, 0), g * nF + j)),
            pl.BlockSpec((3, tn), lambda b, g, j, t: (0, g * nF + j)),
        ],
        out_specs=[pl.BlockSpec((ts, tn), lambda b, g, j, t: (b * nts + t, g * nF + j)),
                   pl.BlockSpec((None, 3, tn), lambda b, g, j, t: (b, 0, g * nF + j))],
        compiler_params=_params(("parallel", "parallel", "parallel", "arbitrary")),
    )(d3, d3, up, up, cw)


def conv_gate_bwd(up, dact, cw, cb, *, B, S, name):
    F = up.shape[1] // 2
    tn = _tile(F, (256, 128))
    nF = F // tn

    def body(g_ref, v_ref, da_ref, wg_ref, wv_ref, bg_ref, bv_ref, o_ref, dw_ref, db_ref):
        rows = lax.broadcasted_iota(jnp.int32, (S, tn), 0)

        def earlier(x, k):
            return jnp.where(rows >= k, pltpu.roll(x, k, 0), 0.0)

        def later(x, k):
            return jnp.where(rows < S - k, pltpu.roll(x, S - k, 0), 0.0)

        def conv(x, w_ref):
            x1, x2 = earlier(x, 1), earlier(x, 2)
            return w_ref[2:3, :] * x + w_ref[1:2, :] * x1 + w_ref[0:1, :] * x2, x1, x2

        def back(d, x, x1, x2, w_ref, half):
            o_ref[half] = (w_ref[2:3, :] * d + w_ref[1:2, :] * later(d, 1) + w_ref[0:1, :] * later(d, 2)
                           ).astype(o_ref.dtype)
            dw_ref[half] = jnp.concatenate([jnp.sum(d * x2, axis=0, keepdims=True),
                                            jnp.sum(d * x1, axis=0, keepdims=True),
                                            jnp.sum(d * x, axis=0, keepdims=True)], axis=0)
            return jnp.sum(d, axis=0, keepdims=True)

        g, v, da = g_ref[...], v_ref[...], da_ref[...]
        gc, g1, g2 = conv(g, wg_ref)
        vc, v1, v2 = conv(v, wv_ref)
        gc = gc + bg_ref[...]
        vc = vc + bv_ref[...]
        sig = _sigmoid(gc)
        dg = da * vc * (sig * (1.0 + gc * (1.0 - sig)))
        dv = da * (gc * sig)
        db_ref[...] = jnp.concatenate([back(dg, g, g1, g2, wg_ref, 0), back(dv, v, v1, v2, wv_ref, 1)], axis=0)

    def cols(off):
        return pl.BlockSpec((S, tn), lambda b, j: (b, j + off))

    def vec(rows, off):
        return pl.BlockSpec((rows, tn), lambda b, j: (0, j + off))

    return pl.pallas_call(
        body, name=name,
        out_shape=[jax.ShapeDtypeStruct((2, B * S, F), BF16), jax.ShapeDtypeStruct((B, 2, 3, F), F32),
                   jax.ShapeDtypeStruct((B, 2, F), F32)],
        grid=(B, nF),
        in_specs=[cols(0), cols(nF), cols(0), vec(3, 0), vec(3, nF), vec(1, 0), vec(1, nF)],
        out_specs=[pl.BlockSpec((2, S, tn), lambda b, j: (0, b, j)),
                   pl.BlockSpec((None, 2, 3, tn), lambda b, j: (b, 0, 0, j)),
                   pl.BlockSpec((None, 2, tn), lambda b, j: (b, 0, j))],
        compiler_params=_params(("parallel", "parallel")),
    )(up, up, dact, cw, cw, cb, cb)


MASKED_LOG = -1e30


def _split2(x):
    bits = lax.bitcast_convert_type(x, jnp.uint32) & jnp.uint32(0xFFFF0000)
    hi = lax.bitcast_convert_type(bits, F32)
    return hi.astype(BF16), (x - hi).astype(BF16)


def _split_dot(x, m):
    hi, lo = _split2(x)
    return jnp.dot(hi, m, preferred_element_type=F32) + jnp.dot(lo, m, preferred_element_type=F32)


def _nt(a, b):
    return lax.dot_general(a, b, (((1,), (1,)), ((), ())), preferred_element_type=F32)


def _tn(a, b):
    return lax.dot_general(a, b, (((0,), (0,)), ((), ())), preferred_element_type=F32)


def _att_scores(q, k, mask, prescaled=False):
    z = _nt(q, k)
    if not prescaled:
        z = z * (HEAD_DIM ** -0.5)
    e = jnp.exp(-jnp.abs(z))
    sp = jnp.log(1.0 + e)
    lb = jnp.minimum(z, 0.0) - sp
    l1 = lb - z
    if mask is not None:
        lb = jnp.where(mask, lb, MASKED_LOG)
        l1 = jnp.where(mask, l1, 0.0)
    return z, lb, l1, e


def _col_to_row(col, eye):
    return jnp.sum(jnp.where(eye, col, 0.0), axis=0, keepdims=True)


def _row_to_col(row, eye):
    return jnp.sum(jnp.where(eye, row, 0.0), axis=1, keepdims=True)


def attn_fwd(q, k, v, *, name):
    B, H, S, dh = q.shape
    T = ATT_BLOCK
    nq = S // T

    G = _tile(H, (ATT_HEADS, 2))

    def body(q_ref, k_ref, v_ref, o_ref, l_ref):
        r = lax.broadcasted_iota(jnp.int32, (T, T), 0)
        c = lax.broadcasted_iota(jnp.int32, (T, T), 1)
        later = (r > c).astype(BF16)
        eye = r == c
        diag = c < r
        blk = lax.broadcasted_iota(jnp.int32, (nq, T), 0)

        later2 = jnp.concatenate([later, later], axis=0)

        def scores(g, qb, k0, mask):
            _, lb, l1, _ = _att_scores(qb, k_ref[g, pl.ds(k0, T), :], mask)
            return lb, jnp.concatenate(_split2(l1), axis=1), jnp.sum(l1, axis=1, keepdims=True)

        def weigh_all(k0, sc, st):
            suf = jnp.dot(jnp.concatenate([s[1] for s in sc], axis=0), later2, preferred_element_type=F32)
            out = []
            for g in range(G):
                lb, _, rowsum = sc[g]
                acc, run = st[g]
                w = jnp.exp(lb + suf[g * T:(g + 1) * T] + run)
                acc = acc + jnp.dot(w.astype(BF16), v_ref[g, pl.ds(k0, T), :], preferred_element_type=F32)
                out.append((acc, run + rowsum))
            return tuple(out)

        def qblock(i, totals):
            q0 = pl.multiple_of(i * T, T)
            qbs = [q_ref[g, pl.ds(q0, T), :] for g in range(G)]
            sc0 = tuple(scores(g, qbs[g], q0, diag) for g in range(G))
            st0 = tuple((jnp.zeros((T, dh), F32), jnp.zeros((T, 1), F32)) for _ in range(G))

            def kblock(jj, carry):
                sc, st = carry
                k_next = pl.multiple_of((i - jj) * T, T)
                k_cur = pl.multiple_of((i - jj + 1) * T, T)
                st = weigh_all(k_cur, sc, st)
                sc_next = tuple(scores(g, qbs[g], k_next, None) for g in range(G))
                return sc_next, st

            sc, st = lax.fori_loop(1, i + 1, kblock, (sc0, st0))
            st = weigh_all(0, sc, st)
            for g in range(G):
                o_ref[g, pl.ds(q0, T), :] = st[g][0]
            return tuple(jnp.where(blk == i, _col_to_row(st[g][1], eye), totals[g]) for g in range(G))

        totals = lax.fori_loop(0, nq, qblock, tuple(jnp.zeros((nq, T), F32) for _ in range(G)))
        for g in range(G):
            l_ref[g] = totals[g]

    spec = pl.BlockSpec((None, G, S, dh), lambda b, h: (b, h, 0, 0))
    lspec = pl.BlockSpec((None, G, nq, T), lambda b, h: (b, h, 0, 0))
    return pl.pallas_call(
        body, name=name,
        out_shape=[jax.ShapeDtypeStruct((B, H, S, dh), F32), jax.ShapeDtypeStruct((B, H, nq, T), F32)],
        grid=(B, H // G), in_specs=[spec, spec, spec], out_specs=[spec, lspec],
        compiler_params=_params(("parallel", "parallel")),
    )(q, k, v)


def attn_bwd(q, k, v, ltot, do, *, name):
    B, H, S, dh = q.shape
    T = ATT_BLOCK
    nq = S // T
    scale = HEAD_DIM ** -0.5

    G = _tile(H, (ATT_HEADS_BWD, 2))

    def body(q_ref, k_ref, v_ref, l_ref, do_ref, dq_ref, dk_ref, dv_ref, dk_acc, dv_acc):
        r = lax.broadcasted_iota(jnp.int32, (T, T), 0)
        c = lax.broadcasted_iota(jnp.int32, (T, T), 1)
        upto = (r <= c).astype(BF16)
        before = (r < c).astype(BF16)
        upto2 = jnp.concatenate([upto, upto], axis=0)
        before2 = jnp.concatenate([before, before], axis=0)
        eye = r == c
        diag = c < r
        blk = lax.broadcasted_iota(jnp.int32, (nq, T), 0)
        dk_acc[...] = jnp.zeros_like(dk_acc)
        dv_acc[...] = jnp.zeros_like(dv_acc)

        def scores(g, qb, dob, k0, mask):
            z, lb, l1, e = _att_scores(qb, k_ref[g, pl.ds(k0, T), :], mask)
            inv = 1.0 / (1.0 + e)
            small = e * inv
            pos = z >= 0.0
            beta = jnp.where(pos, inv, small)
            omb = jnp.where(pos, small, inv)
            if mask is not None:
                beta = jnp.where(mask, beta, 0.0)
            dw = _nt(dob, v_ref[g, pl.ds(k0, T), :])
            return lb, jnp.concatenate(_split2(l1), axis=1), jnp.sum(l1, axis=1, keepdims=True), dw, beta, omb

        def grads_all(qbs, dobs, tots, k0, sc, st):
            pre = jnp.dot(jnp.concatenate([s[1] for s in sc], axis=0), upto2, preferred_element_type=F32)
            dlws = []
            for g in range(G):
                lb = sc[g][0]
                w = jnp.exp(lb + (tots[g] - (pre[g * T:(g + 1) * T] + st[g][1])))
                dv_acc[g, pl.ds(k0, T), :] += _tn(w.astype(BF16), dobs[g])
                dlws.append(sc[g][3] * w)
            pre_d = jnp.dot(jnp.concatenate([jnp.concatenate(_split2(d), axis=1) for d in dlws], axis=0), before2,
                            preferred_element_type=F32)
            out = []
            for g in range(G):
                _, _, rowsum, _, beta, omb = sc[g]
                dq, run_l, run_d = st[g]
                dl1 = pre_d[g * T:(g + 1) * T] + run_d
                dz = ((dlws[g] * omb - dl1 * beta) * scale).astype(BF16)
                dq = dq + jnp.dot(dz, k_ref[g, pl.ds(k0, T), :], preferred_element_type=F32)
                dk_acc[g, pl.ds(k0, T), :] += _tn(dz, qbs[g])
                out.append((dq, run_l + rowsum, run_d + jnp.sum(dlws[g], axis=1, keepdims=True)))
            return tuple(out)

        def block_inputs(i, q0):
            qbs = [q_ref[g, pl.ds(q0, T), :] for g in range(G)]
            dobs = [do_ref[g, pl.ds(q0, T), :] for g in range(G)]
            tots = [_row_to_col(jnp.sum(jnp.where(blk == i, l_ref[g], 0.0), axis=0, keepdims=True), eye)
                    for g in range(G)]
            z1 = jnp.zeros((T, 1), F32)
            return qbs, dobs, tots, tuple((jnp.zeros((T, dh), F32), z1, z1) for _ in range(G))

        qbs, dobs, tots, st = block_inputs(0, 0)
        st = grads_all(qbs, dobs, tots, 0, tuple(scores(g, qbs[g], dobs[g], 0, diag) for g in range(G)), st)
        for g in range(G):
            dq_ref[g, 0:T, :] = st[g][0].astype(dq_ref.dtype)

        def qblock(i, carry0):
            q0 = pl.multiple_of(i * T, T)
            qbs, dobs, tots, st0 = block_inputs(i, q0)
            sc0 = tuple(scores(g, qbs[g], dobs[g], 0, None) for g in range(G))

            def kblock(j, carry):
                sc, st = carry
                k_cur = pl.multiple_of(j * T, T)
                k_next = pl.multiple_of((j + 1) * T, T)
                sc_next = tuple(scores(g, qbs[g], dobs[g], k_next, None) for g in range(G))
                st = grads_all(qbs, dobs, tots, k_cur, sc, st)
                return sc_next, st

            sc, st = lax.fori_loop(0, i - 1, kblock, (sc0, st0))
            k_last = pl.multiple_of((i - 1) * T, T)
            st = grads_all(qbs, dobs, tots, k_last, sc, st)
            sc_diag = tuple(scores(g, qbs[g], dobs[g], q0, diag) for g in range(G))
            st = grads_all(qbs, dobs, tots, q0, sc_diag, st)
            for g in range(G):
                dq_ref[g, pl.ds(q0, T), :] = st[g][0].astype(dq_ref.dtype)
            return carry0

        lax.fori_loop(1, nq, qblock, 0)
        dk_ref[...] = dk_acc[...].astype(dk_ref.dtype)
        dv_ref[...] = dv_acc[...].astype(dv_ref.dtype)

    spec = pl.BlockSpec((None, G, S, dh), lambda b, h: (b, h, 0, 0))
    lspec = pl.BlockSpec((None, G, nq, T), lambda b, h: (b, h, 0, 0))
    shp = jax.ShapeDtypeStruct((B, H, S, dh), BF16)
    return pl.pallas_call(
        body, name=name, out_shape=[shp, shp, shp], grid=(B, H // G),
        in_specs=[spec, spec, spec, lspec, spec], out_specs=[spec] * 3,
        scratch_shapes=[pltpu.VMEM((G, S, dh), F32), pltpu.VMEM((G, S, dh), F32)],
        compiler_params=_params(("parallel", "parallel")),
    )(q, k, v, ltot, do)


def _wide_consts(T, W):
    r = lax.broadcasted_iota(jnp.int32, (W, W), 0)
    c = lax.broadcasted_iota(jnp.int32, (W, W), 1)
    two = lambda m: jnp.concatenate([m.astype(BF16)] * 2, axis=0)
    qrow = lax.broadcasted_iota(jnp.int32, (T, W), 0)
    kcol = lax.broadcasted_iota(jnp.int32, (T, W), 1)
    er = lax.broadcasted_iota(jnp.int32, (T, T), 0)
    ec = lax.broadcasted_iota(jnp.int32, (T, T), 1)
    return two(r > c), two(r <= c), two(r < c), qrow, kcol, er == ec


def attn_fwd_wide(q, k, v, shards=(), slots=(), *, name):
    B, H, S, dh = q.shape
    T = ATT_BLOCK
    W = 2 * T
    nq = S // T
    G = _tile(H, (ATT_HEADS, 2))
    n = len(shards)
    n_steps = B * (H // G)

    def body(*refs):
        q_ref, k_ref, v_ref = refs[:3]
        o_ref, l_ref = refs[3 + 2 * n:5 + 2 * n]
        step_id = pl.program_id(0) * (H // G) + pl.program_id(1)
        if n:
            gather = _ShardGather(refs[3:3 + n], refs[5 + 2 * n:5 + 3 * n], *refs[5 + 3 * n:])
            pl.when(step_id == 0)(gather.send)
            pl.when(step_id == n_steps - 1)(gather.forward)
        later2, _, _, qrow, kcol, eye = _wide_consts(T, W)
        blk = lax.broadcasted_iota(jnp.int32, (nq, T), 0)

        def step(qbs, k0, st, mask):
            parts, lbs, sums = [], [], []
            for g in range(G):
                _, lb, l1, _ = _att_scores(qbs[g], k_ref[g, pl.ds(k0, W), :], mask, prescaled=True)
                parts.append(jnp.concatenate(_split2(l1), axis=1))
                lbs.append(lb)
                sums.append(jnp.sum(l1, axis=1, keepdims=True))
            suf = jnp.dot(jnp.concatenate(parts, axis=0), later2, preferred_element_type=F32)
            out = []
            for g in range(G):
                acc, run = st[g]
                w = jnp.exp(lbs[g] + suf[g * T:(g + 1) * T] + run)
                acc = acc + jnp.dot(w.astype(BF16), v_ref[g, pl.ds(k0, W), :], preferred_element_type=F32)
                out.append((acc, run + sums[g]))
            return tuple(out)

        def qblock(i, totals):
            q0 = pl.multiple_of(i * T, T)
            qbs = [q_ref[g, pl.ds(q0, T), :] * (HEAD_DIM ** -0.5) for g in range(G)]
            half = jnp.right_shift(i, 1)
            last = half * W
            k_last = pl.multiple_of(last, W)
            mask = (k_last + kcol) < (q0 + qrow)
            st = tuple((jnp.zeros((T, dh), F32), jnp.zeros((T, 1), F32)) for _ in range(G))
            st = step(qbs, k_last, st, mask)

            def kblock(jj, st):
                return step(qbs, pl.multiple_of(last - jj * W, W), st, None)

            st = lax.fori_loop(1, half + 1, kblock, st)
            for g in range(G):
                o_ref[g, pl.ds(q0, T), :] = st[g][0]
            return tuple(jnp.where(blk == i, _col_to_row(st[g][1], eye), totals[g]) for g in range(G))

        totals = lax.fori_loop(0, nq, qblock, tuple(jnp.zeros((nq, T), F32) for _ in range(G)))
        for g in range(G):
            l_ref[g] = totals[g]
        if n:
            pl.when(step_id == n_steps - 1)(gather.finish)

    spec = pl.BlockSpec((None, G, S, dh), lambda b, h: (b, h, 0, 0))
    lspec = pl.BlockSpec((None, G, nq, T), lambda b, h: (b, h, 0, 0))
    dma = pltpu.SemaphoreType.DMA
    return pl.pallas_call(
        body, name=name,
        out_shape=[jax.ShapeDtypeStruct((B, H, S, dh), F32), jax.ShapeDtypeStruct((B, H, nq, T), F32)]
        + [jax.ShapeDtypeStruct(s.shape, s.dtype) for s in slots],
        grid=(B, H // G), in_specs=[spec, spec, spec] + _any_specs(2 * n), out_specs=[spec, lspec] + _any_specs(n),
        input_output_aliases={3 + n + i: 2 + i for i in range(n)},
        scratch_shapes=[dma((3 * n,))] * 4 if n else [],
        compiler_params=_params(("arbitrary", "arbitrary")),
    )(q, k, v, *shards, *slots)


def attn_bwd_wide(q, k, v, ltot, do, partials=(), *, name):
    B, H, S, dh = q.shape
    T = ATT_BLOCK
    W = 2 * T
    nq = S // T
    scale = HEAD_DIM ** -0.5
    G = _tile(H, (ATT_HEADS_BWD, 2))
    n = len(partials)
    n_steps = B * (H // G)

    def body(*refs):
        q_ref, k_ref, v_ref, l_ref, do_ref = refs[:5]
        dq_ref, dk_ref, dv_ref = refs[5 + n:8 + n]
        dk_acc, dv_acc = refs[8 + 2 * n:10 + 2 * n]
        step_id = pl.program_id(0) * (H // G) + pl.program_id(1)
        if n:
            scatter = _ChipScatter(refs[5:5 + n], refs[8 + n:8 + 2 * n], *refs[10 + 2 * n:])
            pl.when(step_id == 0)(scatter.send)
        _, upto2, before2, qrow, kcol, eye = _wide_consts(T, W)
        blk = lax.broadcasted_iota(jnp.int32, (nq, T), 0)
        dk_acc[...] = jnp.zeros_like(dk_acc)
        dv_acc[...] = jnp.zeros_like(dv_acc)

        def step(qbs, dobs, tots, k0, st, mask):
            sc = []
            for g in range(G):
                z, lb, l1, e = _att_scores(qbs[g], k_ref[g, pl.ds(k0, W), :], mask, prescaled=True)
                inv = 1.0 / (1.0 + e)
                small = e * inv
                pos = z >= 0.0
                beta = jnp.where(pos, inv, small)
                omb = jnp.where(pos, small, inv)
                if mask is not None:
                    beta = jnp.where(mask, beta, 0.0)
                dw = _nt(dobs[g], v_ref[g, pl.ds(k0, W), :])
                sc.append((lb, jnp.concatenate(_split2(l1), axis=1), jnp.sum(l1, axis=1, keepdims=True), dw, beta, omb))
            pre = jnp.dot(jnp.concatenate([s[1] for s in sc], axis=0), upto2, preferred_element_type=F32)
            dlws = []
            for g in range(G):
                w = jnp.exp(sc[g][0] + (tots[g] - (pre[g * T:(g + 1) * T] + st[g][1])))
                dv_acc[g, pl.ds(k0, W), :] += _tn(w.astype(BF16), dobs[g])
                dlws.append(sc[g][3] * w)
            pre_d = jnp.dot(jnp.concatenate([jnp.concatenate(_split2(d), axis=1) for d in dlws], axis=0), before2,
                            preferred_element_type=F32)
            out = []
            for g in range(G):
                _, _, rowsum, _, beta, omb = sc[g]
                dq, run_l, run_d = st[g]
                dl1 = pre_d[g * T:(g + 1) * T] + run_d
                dz = (dlws[g] * omb - dl1 * beta).astype(BF16)
                dq = dq + jnp.dot(dz, k_ref[g, pl.ds(k0, W), :], preferred_element_type=F32)
                dk_acc[g, pl.ds(k0, W), :] += _tn(dz, qbs[g])
                out.append((dq, run_l + rowsum, run_d + jnp.sum(dlws[g], axis=1, keepdims=True)))
            return tuple(out)

        def qblock(i, carry0):
            q0 = pl.multiple_of(i * T, T)
            qbs = [q_ref[g, pl.ds(q0, T), :] * scale for g in range(G)]
            dobs = [do_ref[g, pl.ds(q0, T), :] for g in range(G)]
            tots = [_row_to_col(jnp.sum(jnp.where(blk == i, l_ref[g], 0.0), axis=0, keepdims=True), eye)
                    for g in range(G)]
            z1 = jnp.zeros((T, 1), F32)
            st = tuple((jnp.zeros((T, dh), F32), z1, z1) for _ in range(G))

            def kblock(j, st):
                return step(qbs, dobs, tots, pl.multiple_of(j * W, W), st, None)

            half = jnp.right_shift(i, 1)
            st = lax.fori_loop(0, half, kblock, st)
            k_last = pl.multiple_of(half * W, W)
            st = step(qbs, dobs, tots, k_last, st, (k_last + kcol) < (q0 + qrow))
            for g in range(G):
                dq_ref[g, pl.ds(q0, T), :] = (st[g][0] * scale).astype(dq_ref.dtype)
            return carry0

        lax.fori_loop(0, nq, qblock, 0)
        dk_ref[...] = dk_acc[...].astype(dk_ref.dtype)
        dv_ref[...] = dv_acc[...].astype(dv_ref.dtype)
        if n:
            pl.when(step_id == n_steps - 1)(scatter.finish)

    spec = pl.BlockSpec((None, G, S, dh), lambda b, h: (b, h, 0, 0))
    lspec = pl.BlockSpec((None, G, nq, T), lambda b, h: (b, h, 0, 0))
    shp = jax.ShapeDtypeStruct((B, H, S, dh), BF16)
    dma = pltpu.SemaphoreType.DMA
    return pl.pallas_call(
        body, name=name,
        out_shape=[shp, shp, shp] + [jax.ShapeDtypeStruct((N_CHIPS - 1,) + a.shape[1:], a.dtype) for a in partials],
        grid=(B, H // G),
        in_specs=[spec, spec, spec, lspec, spec] + _any_specs(n), out_specs=[spec] * 3 + _any_specs(n),
        scratch_shapes=[pltpu.VMEM((G, S, dh), F32), pltpu.VMEM((G, S, dh), F32)]
        + ([dma((3 * n,)), dma((3 * n,))] if n else []),
        compiler_params=_params(("arbitrary", "arbitrary")),
    )(q, k, v, ltot, do, *partials)


def _pair_masks(x, first):
    zero = jnp.zeros_like(x)
    return jnp.where(first, x, zero), jnp.where(first, zero, x)


def attn_fwd_pairs(qkv, shards=(), slots=(), *, B, S, name):
    D = qkv.shape[1] // 3
    H = D // HEAD_DIM
    T = ATT_BLOCK
    W = 2 * T
    nq = S // T
    G = _tile(H, (ATT_HEADS, 2))
    P = G // 2
    LW = 2 * HEAD_DIM * P
    nsec = D // LW
    n = len(shards)
    n_steps = B * nsec

    def body(*refs):
        q_ref, k_ref, v_ref = refs[:3]
        o_ref, l_ref = refs[3 + 2 * n:5 + 2 * n]
        step_id = pl.program_id(0) * nsec + pl.program_id(1)
        if n:
            gather = _ShardGather(refs[3:3 + n], refs[5 + 2 * n:5 + 3 * n], *refs[5 + 3 * n:])
            pl.when(step_id == 0)(gather.send)
            pl.when(step_id == n_steps - 1)(gather.forward)
        later2, _, _, qrow, kcol, eye = _wide_consts(T, W)
        blk = lax.broadcasted_iota(jnp.int32, (nq, T), 0)
        first_q = lax.broadcasted_iota(jnp.int32, (T, 2 * HEAD_DIM), 1) < HEAD_DIM
        first_k = lax.broadcasted_iota(jnp.int32, (W, 2 * HEAD_DIM), 1) < HEAD_DIM

        def lanes(j):
            return slice(j * 2 * HEAD_DIM, (j + 1) * 2 * HEAD_DIM)

        def step(qms, k0, st, mask):
            accs, runs = st
            parts, lbs, sums = [], [], []
            for g in range(G):
                _, lb, l1, _ = _att_scores(qms[g], k_ref[pl.ds(k0, W), lanes(g // 2)], mask, prescaled=True)
                parts.append(jnp.concatenate(_split2(l1), axis=1))
                lbs.append(lb)
                sums.append(jnp.sum(l1, axis=1, keepdims=True))
            suf = jnp.dot(jnp.concatenate(parts, axis=0), later2, preferred_element_type=F32)
            new_accs, new_runs = [], []
            for j in range(P):
                vms = _pair_masks(v_ref[pl.ds(k0, W), lanes(j)], first_k)
                acc = accs[j]
                for h in range(2):
                    g = 2 * j + h
                    w = jnp.exp(lbs[g] + suf[g * T:(g + 1) * T] + runs[g])
                    acc = acc + jnp.dot(w.astype(BF16), vms[h], preferred_element_type=F32)
                    new_runs.append(runs[g] + sums[g])
                new_accs.append(acc)
            return tuple(new_accs), tuple(new_runs)

        def qblock(i, totals):
            q0 = pl.multiple_of(i * T, T)
            qms = []
            for j in range(P):
                qms.extend(_pair_masks(q_ref[pl.ds(q0, T), lanes(j)] * (HEAD_DIM ** -0.5), first_q))
            half = jnp.right_shift(i, 1)
            last = half * W
            k_last = pl.multiple_of(last, W)
            mask = (k_last + kcol) < (q0 + qrow)
            st = (tuple(jnp.zeros((T, 2 * HEAD_DIM), F32) for _ in range(P)),
                  tuple(jnp.zeros((T, 1), F32) for _ in range(G)))
            st = step(qms, k_last, st, mask)

            def kblock(jj, st):
                return step(qms, pl.multiple_of(last - jj * W, W), st, None)

            accs, runs = lax.fori_loop(1, half + 1, kblock, st)
            for j in range(P):
                o_ref[pl.ds(q0, T), lanes(j)] = accs[j].astype(o_ref.dtype)
            return tuple(jnp.where(blk == i, _col_to_row(runs[g], eye), totals[g]) for g in range(G))

        totals = lax.fori_loop(0, nq, qblock, tuple(jnp.zeros((nq, T), F32) for _ in range(G)))
        for g in range(G):
            l_ref[g] = totals[g]
        if n:
            pl.when(step_id == n_steps - 1)(gather.finish)

    def cols(section):
        return pl.BlockSpec((S, LW), lambda b, h: (b, section * nsec + h))

    lspec = pl.BlockSpec((None, G, nq, T), lambda b, h: (b, h, 0, 0))
    dma = pltpu.SemaphoreType.DMA
    return pl.pallas_call(
        body, name=name,
        out_shape=[jax.ShapeDtypeStruct((B * S, D), BF16), jax.ShapeDtypeStruct((B, H, nq, T), F32)]
        + [jax.ShapeDtypeStruct(s.shape, s.dtype) for s in slots],
        grid=(B, nsec), in_specs=[cols(0), cols(1), cols(2)] + _any_specs(2 * n),
        out_specs=[cols(0), lspec] + _any_specs(n),
        input_output_aliases={3 + n + i: 2 + i for i in range(n)},
        scratch_shapes=[dma((3 * n,))] * 4 if n else [],
        compiler_params=_params(("arbitrary", "arbitrary")),
    )(qkv, qkv, qkv, *shards, *slots)


def attn_bwd_pairs(qkv, ltot, do, partials=(), *, B, S, name):
    D = qkv.shape[1] // 3
    H = D // HEAD_DIM
    T = ATT_BLOCK
    W = 2 * T
    nq = S // T
    scale = HEAD_DIM ** -0.5
    G = _tile(H, (ATT_HEADS_BWD, 2))
    P = G // 2
    LW = 2 * HEAD_DIM * P
    nsec = D // LW
    n = len(partials)
    n_steps = B * nsec

    def body(*refs):
        q_ref, k_ref, v_ref, l_ref, do_ref = refs[:5]
        d_ref = refs[5 + n]
        dk_acc, dv_acc = refs[6 + 2 * n:8 + 2 * n]
        step_id = pl.program_id(0) * nsec + pl.program_id(1)
        if n:
            scatter = _ChipScatter(refs[5:5 + n], refs[6 + n:6 + 2 * n], *refs[8 + 2 * n:])
            pl.when(step_id == 0)(scatter.send)
        _, upto2, before2, qrow, kcol, eye = _wide_consts(T, W)
        blk = lax.broadcasted_iota(jnp.int32, (nq, T), 0)
        first_q = lax.broadcasted_iota(jnp.int32, (T, 2 * HEAD_DIM), 1) < HEAD_DIM
        first_k = lax.broadcasted_iota(jnp.int32, (W, 2 * HEAD_DIM), 1) < HEAD_DIM
        dk_acc[...] = jnp.zeros_like(dk_acc)
        dv_acc[...] = jnp.zeros_like(dv_acc)

        def lanes(j):
            return slice(j * 2 * HEAD_DIM, (j + 1) * 2 * HEAD_DIM)

        def step(qms, doms, tots, k0, st, mask):
            dqs, runs_l, runs_d = st
            sc = []
            for g in range(G):
                z, lb, l1, _ = _att_scores(qms[g], k_ref[pl.ds(k0, W), lanes(g // 2)], mask, prescaled=True)
                beta = 0.5 * jnp.tanh(0.5 * z) + 0.5
                omb = 1.0 - beta
                if mask is not None:
                    beta = jnp.where(mask, beta, 0.0)
                dw = _nt(doms[g], v_ref[pl.ds(k0, W), lanes(g // 2)])
                sc.append((lb, jnp.concatenate(_split2(l1), axis=1), jnp.sum(l1, axis=1, keepdims=True), dw, beta, omb))
            pre = jnp.dot(jnp.concatenate([s[1] for s in sc], axis=0), upto2, preferred_element_type=F32)
            dlws = []
            for j in range(P):
                dv = None
                for h in range(2):
                    g = 2 * j + h
                    w = jnp.exp(sc[g][0] + (tots[g] - (pre[g * T:(g + 1) * T] + runs_l[g])))
                    t = _tn(w.astype(BF16), doms[g])
                    dv = t if dv is None else dv + t
                    dlws.append(sc[g][3] * w)
                dv_acc[j, pl.ds(k0, W), :] += dv
            pre_d = jnp.dot(jnp.concatenate([jnp.concatenate(_split2(d), axis=1) for d in dlws], axis=0), before2,
                            preferred_element_type=F32)
            new_dqs, new_l, new_d = [], [], []
            for j in range(P):
                kms = _pair_masks(k_ref[pl.ds(k0, W), lanes(j)], first_k)
                dq, dk = dqs[j], None
                for h in range(2):
                    g = 2 * j + h
                    _, _, rowsum, _, beta, omb = sc[g]
                    dl1 = pre_d[g * T:(g + 1) * T] + runs_d[g]
                    dz = (dlws[g] * omb - dl1 * beta).astype(BF16)
                    dq = dq + jnp.dot(dz, kms[h], preferred_element_type=F32)
                    t = _tn(dz, qms[g])
                    dk = t if dk is None else dk + t
                    new_l.append(runs_l[g] + rowsum)
                    new_d.append(runs_d[g] + jnp.sum(dlws[g], axis=1, keepdims=True))
                dk_acc[j, pl.ds(k0, W), :] += dk
                new_dqs.append(dq)
            return tuple(new_dqs), tuple(new_l), tuple(new_d)

        def qblock(i, carry0):
            q0 = pl.multiple_of(i * T, T)
            qms, doms = [], []
            for j in range(P):
                qms.extend(_pair_masks(q_ref[pl.ds(q0, T), lanes(j)] * scale, first_q))
                doms.extend(_pair_masks(do_ref[pl.ds(q0, T), lanes(j)], first_q))
            tots = [_row_to_col(jnp.sum(jnp.where(blk == i, l_ref[g], 0.0), axis=0, keepdims=True), eye)
                    for g in range(G)]
            z1 = tuple(jnp.zeros((T, 1), F32) for _ in range(G))
            st = (tuple(jnp.zeros((T, 2 * HEAD_DIM), F32) for _ in range(P)), z1, z1)

            def kblock(j, st):
                return step(qms, doms, tots, pl.multiple_of(j * W, W), st, None)

            half = jnp.right_shift(i, 1)
            st = lax.fori_loop(0, half, kblock, st)
            k_last = pl.multiple_of(half * W, W)
            dqs, _, _ = step(qms, doms, tots, k_last, st, (k_last + kcol) < (q0 + qrow))
            for j in range(P):
                d_ref[0, pl.ds(q0, T), lanes(j)] = (dqs[j] * scale).astype(d_ref.dtype)
            return carry0

        lax.fori_loop(0, nq, qblock, 0)
        for j in range(P):
            d_ref[1, :, lanes(j)] = dk_acc[j].astype(d_ref.dtype)
            d_ref[2, :, lanes(j)] = dv_acc[j].astype(d_ref.dtype)
        if n:
            pl.when(step_id == n_steps - 1)(scatter.finish)

    def cols(section):
        return pl.BlockSpec((S, LW), lambda b, h: (b, section * nsec + h))

    lspec = pl.BlockSpec((None, G, nq, T), lambda b, h: (b, h, 0, 0))
    dma = pltpu.SemaphoreType.DMA
    return pl.pallas_call(
        body, name=name,
        out_shape=[jax.ShapeDtypeStruct((3, B * S, D), BF16)]
        + [jax.ShapeDtypeStruct((N_CHIPS - 1,) + a.shape[1:], a.dtype) for a in partials],
        grid=(B, nsec),
        in_specs=[cols(0), cols(1), cols(2), lspec, cols(0)] + _any_specs(n),
        out_specs=[pl.BlockSpec((3, S, LW), lambda b, h: (0, b, h))] + _any_specs(n),
        scratch_shapes=[pltpu.VMEM((P, S, 2 * HEAD_DIM), F32), pltpu.VMEM((P, S, 2 * HEAD_DIM), F32)]
        + ([dma((3 * n,)), dma((3 * n,))] if n else []),
        compiler_params=_params(("arbitrary", "arbitrary")),
    )(qkv, qkv, qkv, ltot, do, *partials)


def _cmul(ar, ai, br, bi):
    return ar * br - ai * bi, ar * bi + ai * br


def _cpow(lr, li, n):
    rr, ri = None, None
    br, bi = lr, li
    while n:
        if n & 1:
            rr, ri = (br, bi) if rr is None else _cmul(rr, ri, br, bi)
        n >>= 1
        if n:
            br, bi = _cmul(br, bi, br, bi)
    return rr, ri


def _ssm_scan(sr, si, lr, li, n_steps, reverse):
    W = sr.shape[1]
    R = SEGMENTS
    lim = -li if reverse else li
    zero = jnp.zeros((R, W), F32)

    def row(k):
        i = (n_steps - 1 - k) if reverse else k
        return pl.multiple_of(i * R, R)

    def local(k, st):
        cr, ci = st
        r0 = row(k)
        pr, pi = _cmul(lr, lim, cr, ci)
        nr = pr + sr[pl.ds(r0, R), :]
        ni = pi + si[pl.ds(r0, R), :]
        sr[pl.ds(r0, R), :] = nr
        si[pl.ds(r0, R), :] = ni
        return nr, ni

    er, ei = lax.fori_loop(0, n_steps, local, (zero, zero), unroll=SCAN_UNROLL)
    lnr, lni = _cpow(lr, lim, n_steps)
    rows = lax.broadcasted_iota(jnp.int32, (R, W), 0)
    cr, ci = zero, zero
    for step in range(1, R):
        tr, ti = _cmul(lnr, lni, cr, ci)
        tr, ti = tr + er, ti + ei
        if reverse:
            seg = R - 1 - step
            tr, ti = pltpu.roll(tr, R - 1, 0), pltpu.roll(ti, R - 1, 0)
        else:
            seg = step
            tr, ti = pltpu.roll(tr, 1, 0), pltpu.roll(ti, 1, 0)
        cr = jnp.where(rows == seg, tr, cr)
        ci = jnp.where(rows == seg, ti, ci)

    def fix(k, st):
        pr, pi = st
        r0 = row(k)
        ar, ai = _cmul(pr, pi, cr, ci)
        sr[pl.ds(r0, R), :] += ar
        si[pl.ds(r0, R), :] += ai
        return _cmul(lr, lim, pr, pi)

    lax.fori_loop(0, n_steps, fix, (lr, lim), unroll=SCAN_UNROLL)
    return cr, ci


def _ssm_specs(S, W):
    CH = GROUPS_PER_BLOCK * SSM_GROUP
    return dict(
        rows=pl.BlockSpec((S, CH), lambda b, j: (b, j)),
        b=pl.BlockSpec((None, CH, W), lambda b, j: (j, 0, 0)),
        c=pl.BlockSpec((None, W, CH), lambda b, j: (j, 0, 0)),
        lam=pl.BlockSpec((None, SEGMENTS, W), lambda b, j: (j, 0, 0)),
        vec=pl.BlockSpec((1, CH), lambda b, j: (0, j)),
    )


def ssm_fwd(u, bre, bim, cre, cim, lr8, li8, dsk, *, B, S, name):
    D = u.shape[1]
    J, CH, W = bre.shape
    n_steps = S // SEGMENTS
    sp = _ssm_specs(S, W)

    def body(u_ref, bre_ref, bim_ref, cre_ref, cim_ref, lr_ref, li_ref, dsk_ref, y_ref, sr, si):
        u = u_ref[...]
        ub = u.astype(BF16)
        sr[...] = jnp.dot(ub, bre_ref[...], preferred_element_type=F32)
        si[...] = jnp.dot(ub, bim_ref[...], preferred_element_type=F32)
        _ssm_scan(sr, si, lr_ref[...], li_ref[...], n_steps, False)
        y = jnp.dot(sr[...].astype(BF16), cre_ref[...], preferred_element_type=F32)
        y = y - jnp.dot(si[...].astype(BF16), cim_ref[...], preferred_element_type=F32)
        y_ref[...] = y + dsk_ref[...] * u

    return pl.pallas_call(
        body, name=name, out_shape=jax.ShapeDtypeStruct((B * S, D), F32), grid=(B, J),
        in_specs=[sp["rows"], sp["b"], sp["b"], sp["c"], sp["c"], sp["lam"], sp["lam"], sp["vec"]],
        out_specs=sp["rows"],
        scratch_shapes=[pltpu.VMEM((S, W), F32), pltpu.VMEM((S, W), F32)],
        compiler_params=_params(("parallel", "parallel")),
    )(u, bre, bim, cre, cim, lr8, li8, dsk)


def ssm_bwd(u, dy, bre, bim, cre, cim, lr8, li8, dsk, *, B, S, name):
    D = u.shape[1]
    J, CH, W = bre.shape
    n_steps = S // SEGMENTS
    sp = _ssm_specs(S, W)

    def body(u_ref, dy_ref, bre_ref, bim_ref, cre_ref, cim_ref, lr_ref, li_ref, dsk_ref,
             du_ref, dbre_ref, dbim_ref, dcre_ref, dcim_ref, dlr_ref, dli_ref, ddsk_ref, sr, si, ar, ai):
        u = u_ref[...]
        dy = dy_ref[...]
        ub = u.astype(BF16)
        dyb = dy.astype(BF16)
        lr, li = lr_ref[...], li_ref[...]
        sr[...] = jnp.dot(ub, bre_ref[...], preferred_element_type=F32)
        si[...] = jnp.dot(ub, bim_ref[...], preferred_element_type=F32)
        cr, ci = _ssm_scan(sr, si, lr, li, n_steps, False)
        ar[...] = _nt(dyb, cre_ref[...])
        ai[...] = -_nt(dyb, cim_ref[...])
        _ssm_scan(ar, ai, lr, li, n_steps, True)

        def dlam(k, st):
            dr, di = st
            r0 = pl.multiple_of((k + 1) * SEGMENTS, SEGMENTS)
            p0 = pl.multiple_of(k * SEGMENTS, SEGMENTS)
            pr, pi = sr[pl.ds(p0, SEGMENTS), :], si[pl.ds(p0, SEGMENTS), :]
            xr, xi = ar[pl.ds(r0, SEGMENTS), :], ai[pl.ds(r0, SEGMENTS), :]
            return dr + pr * xr + pi * xi, di + pr * xi - pi * xr

        xr, xi = ar[0:SEGMENTS, :], ai[0:SEGMENTS, :]
        dr, di = lax.fori_loop(0, n_steps - 1, dlam, (cr * xr + ci * xi, cr * xi - ci * xr), unroll=SCAN_UNROLL)
        dlr_ref[...] = dr
        dli_ref[...] = di
        arb = ar[...].astype(BF16)
        aib = ai[...].astype(BF16)
        du_ref[...] = _nt(arb, bre_ref[...]) + _nt(aib, bim_ref[...]) + dsk_ref[...] * dy
        dbre_ref[...] = _tn(ub, arb)
        dbim_ref[...] = _tn(ub, aib)
        dcre_ref[...] = _tn(sr[...].astype(BF16), dyb)
        dcim_ref[...] = -_tn(si[...].astype(BF16), dyb)
        ddsk_ref[...] = jnp.sum(dy * u, axis=0, keepdims=True)

    def per(shape):
        return pl.BlockSpec((None, None) + shape, lambda b, j: (b, j, 0, 0))

    return pl.pallas_call(
        body, name=name,
        out_shape=[jax.ShapeDtypeStruct((B * S, D), F32),
                   jax.ShapeDtypeStruct((B, J, CH, W), F32), jax.ShapeDtypeStruct((B, J, CH, W), F32),
                   jax.ShapeDtypeStruct((B, J, W, CH), F32), jax.ShapeDtypeStruct((B, J, W, CH), F32),
                   jax.ShapeDtypeStruct((B, J, SEGMENTS, W), F32), jax.ShapeDtypeStruct((B, J, SEGMENTS, W), F32),
                   jax.ShapeDtypeStruct((B, J, 1, CH), F32)],
        grid=(B, J),
        in_specs=[sp["rows"], sp["rows"], sp["b"], sp["b"], sp["c"], sp["c"], sp["lam"], sp["lam"], sp["vec"]],
        out_specs=[sp["rows"], per((CH, W)), per((CH, W)), per((W, CH)), per((W, CH)), per((SEGMENTS, W)),
                   per((SEGMENTS, W)),
                   per((1, CH))],
        scratch_shapes=[pltpu.VMEM((S, W), F32)] * 4,
        compiler_params=_params(("parallel", "parallel")),
    )(u, dy, bre, bim, cre, cim, lr8, li8, dsk)


def _ssm_discretize(a_re, a_im, log_dt, b_re, b_im):
    dt = jnp.exp(log_dt)[:, None]
    er = jnp.exp(a_re * dt)
    lr = er * jnp.cos(a_im * dt)
    li = er * jnp.sin(a_im * dt)
    den = a_re * a_re + a_im * a_im
    fr = ((lr - 1.0) * a_re + li * a_im) / den
    fi = (li * a_re - (lr - 1.0) * a_im) / den
    bbr = fr[..., None] * b_re - fi[..., None] * b_im
    bbi = fr[..., None] * b_im + fi[..., None] * b_re
    return lr, li, bbr, bbi


def _block_diag_in(m):
    G, P, H = m.shape
    J = G // GROUPS_PER_BLOCK
    m = m.reshape(J, GROUPS_PER_BLOCK, P, H).transpose(0, 1, 3, 2)
    eye = jnp.eye(GROUPS_PER_BLOCK, dtype=m.dtype)
    out = m[:, :, :, None, :] * eye[None, :, None, :, None]
    return out.reshape(J, GROUPS_PER_BLOCK * H, GROUPS_PER_BLOCK * P)


def _block_diag_in_grad(d, G, P, H):
    J = G // GROUPS_PER_BLOCK
    d = d.reshape(J, GROUPS_PER_BLOCK, H, GROUPS_PER_BLOCK, P)
    idx = jnp.arange(GROUPS_PER_BLOCK)
    d = d[:, idx, :, idx, :]
    return d.transpose(1, 0, 3, 2).reshape(G, P, H)


def _block_diag_out(m):
    G, H, P = m.shape
    J = G // GROUPS_PER_BLOCK
    m = m.reshape(J, GROUPS_PER_BLOCK, H, P).transpose(0, 1, 3, 2)
    eye = jnp.eye(GROUPS_PER_BLOCK, dtype=m.dtype)
    out = m[:, :, :, None, :] * eye[None, :, None, :, None]
    return out.reshape(J, GROUPS_PER_BLOCK * P, GROUPS_PER_BLOCK * H)


def _block_diag_out_grad(d, G, H, P):
    J = G // GROUPS_PER_BLOCK
    d = d.reshape(J, GROUPS_PER_BLOCK, P, GROUPS_PER_BLOCK, H)
    idx = jnp.arange(GROUPS_PER_BLOCK)
    d = d[:, idx, :, idx, :]
    return d.transpose(1, 0, 3, 2).reshape(G, H, P)


def _interleave(a, B, S):
    L = S // SEGMENTS
    return a.reshape(B, SEGMENTS, L, a.shape[-1]).transpose(0, 2, 1, 3).reshape(B * S, a.shape[-1])


def _deinterleave(a, B, S):
    L = S // SEGMENTS
    return a.reshape(B, L, SEGMENTS, a.shape[-1]).transpose(0, 2, 1, 3).reshape(B * S, a.shape[-1])


def _adamw_math(w, g, m, v):
    m = ADAM_B1 * m + (1.0 - ADAM_B1) * g
    v = ADAM_B2 * v + (1.0 - ADAM_B2) * (g * g)
    m_hat = m / (1.0 - ADAM_B1 ** ADAM_STEP)
    v_hat = v / (1.0 - ADAM_B2 ** ADAM_STEP)
    delta = -ADAM_LR * (m_hat / (jnp.sqrt(v_hat) + ADAM_EPS) + ADAM_WD * w)
    return delta, m, v


def adamw(w, g, m, v, *, name):
    R, C = w.shape
    tr = _tile(R, (max(8, (1 << 18) // C // 8 * 8), 256, 128, 64, 32, 16, 8))

    def body(w_ref, g_ref, m_ref, v_ref, d_ref, nm_ref, nv_ref):
        d, nm, nv = _adamw_math(w_ref[...], g_ref[...], m_ref[...], v_ref[...])
        d_ref[...] = d
        nm_ref[...] = nm
        nv_ref[...] = nv

    spec = pl.BlockSpec((tr, C), lambda i: (i, 0))
    shp = jax.ShapeDtypeStruct((R, C), F32)
    return pl.pallas_call(
        body, name=name, out_shape=[shp, shp, shp], grid=(R // tr,), in_specs=[spec] * 4, out_specs=[spec] * 3,
        compiler_params=_params(("parallel",)),
    )(w, g, m, v)


def sum_leading(a, *, name, out_dtype=F32):
    n, R, C = a.shape
    tr = _tile(R, (256, 128, 64, 32, 16, 8))

    def body(a_ref, o_ref):
        acc = a_ref[0].astype(F32)
        for i in range(1, n):
            acc = acc + a_ref[i].astype(F32)
        o_ref[...] = acc.astype(o_ref.dtype)

    return pl.pallas_call(
        body, name=name, out_shape=jax.ShapeDtypeStruct((R, C), out_dtype), grid=(R // tr,),
        in_specs=[pl.BlockSpec((n, tr, C), lambda i: (0, i, 0))], out_specs=pl.BlockSpec((tr, C), lambda i: (i, 0)),
        compiler_params=_params(("parallel",)),
    )(a)


def _any_specs(n):
    return [pl.BlockSpec(memory_space=pl.ANY) for _ in range(n)]


def _coords():
    return lax.axis_index("x"), lax.axis_index("y"), lax.axis_index("c")


def _flip(v, bit):
    return (v + bit) % 2


def all_gather8(a, *, name):
    shape = a.shape

    def body(a_ref, o_ref, send_sems, recv_sems, local_sem):
        x, y, c = _coords()
        me = 4 * x + 2 * y + c
        mine = pltpu.make_async_copy(a_ref, o_ref.at[me], local_sem)
        mine.start()
        sends = []
        for k in range(1, N_DEV):
            peer = (_flip(x, (k >> 2) & 1), _flip(y, (k >> 1) & 1), _flip(c, k & 1))
            cp = pltpu.make_async_remote_copy(a_ref, o_ref.at[me], send_sems.at[k - 1], recv_sems.at[k - 1],
                                              device_id=peer, device_id_type=MESH)
            cp.start()
            sends.append(cp)
        for k in range(1, N_DEV):
            px, py, pc = _flip(x, (k >> 2) & 1), _flip(y, (k >> 1) & 1), _flip(c, k & 1)
            src = 4 * px + 2 * py + pc
            pltpu.make_async_remote_copy(a_ref, o_ref.at[src], send_sems.at[k - 1], recv_sems.at[k - 1],
                                         device_id=(px, py, pc), device_id_type=MESH).wait_recv()
        for cp in sends:
            cp.wait_send()
        mine.wait()

    return pl.pallas_call(
        body, name=name, out_shape=jax.ShapeDtypeStruct((N_DEV,) + shape, a.dtype),
        in_specs=_any_specs(1), out_specs=pl.BlockSpec(memory_space=pl.ANY),
        scratch_shapes=[pltpu.SemaphoreType.DMA((N_DEV - 1,)), pltpu.SemaphoreType.DMA((N_DEV - 1,)),
                        pltpu.SemaphoreType.DMA(())],
    )(a)


def _chip_of(x, y, p):
    px, py = _flip(x, (p >> 1) & 1), _flip(y, p & 1)
    return 2 * px + py, px, py


class _ShardGather:
    def __init__(self, ins, outs, ici_send, ici_recv, d2d_send, d2d_recv):
        self.ins, self.outs = ins, outs
        self.sems = ici_send, ici_recv, d2d_send, d2d_recv
        self.x, self.y, self.c = _coords()
        self.me = 2 * self.x + self.y

    def _ici(self, i, p, slot):
        half = self.ins[i].shape[0] // 2
        rows = pl.ds(self.c * half, half)
        _, px, py = _chip_of(self.x, self.y, p)
        s = i * 3 + p - 1
        return pltpu.make_async_remote_copy(self.ins[i].at[rows], self.outs[i].at[slot, rows], self.sems[0].at[s],
                                            self.sems[1].at[s], device_id=(px, py, self.c), device_id_type=MESH)

    def _d2d(self, i, p, mine):
        half = self.ins[i].shape[0] // 2
        rows = pl.ds((self.c if mine else 1 - self.c) * half, half)
        src, _, _ = _chip_of(self.x, self.y, p)
        s = i * 3 + p - 1
        part = self.outs[i].at[src, rows]
        return pltpu.make_async_remote_copy(part, part, self.sems[2].at[s], self.sems[3].at[s],
                                            device_id=(self.x, self.y, 1 - self.c), device_id_type=MESH)

    def _each(self):
        return [(i, p) for i in range(len(self.ins)) for p in range(1, N_CHIPS)]

    def send(self):
        for i, p in self._each():
            self._ici(i, p, self.me).start()

    def forward(self):
        for i, p in self._each():
            self._ici(i, p, _chip_of(self.x, self.y, p)[0]).wait_recv()
            self._d2d(i, p, True).start()

    def finish(self):
        for i, p in self._each():
            self._d2d(i, p, False).wait_recv()
        for i, p in self._each():
            self._ici(i, p, self.me).wait_send()
            self._d2d(i, p, True).wait_send()


def gather_chip_shards(arrs, remote, *, name):
    n = len(arrs)
    far = [i for i in range(n) if remote[i]]

    def body(*refs):
        ins, outs = refs[:n], refs[n:2 * n]
        ici_send, ici_recv, d2d_send, d2d_recv, local_sems = refs[2 * n:2 * n + 5]
        bufs = refs[2 * n + 5:]
        me = 2 * lax.axis_index("x") + lax.axis_index("y")
        loads = []
        for i in range(n):
            cp = pltpu.make_async_copy(ins[i], bufs[i], local_sems.at[i])
            cp.start()
            loads.append(cp)
        gather = _ShardGather([ins[i] for i in far], [outs[i] for i in far], ici_send, ici_recv, d2d_send, d2d_recv)
        gather.send()
        stores = []
        for i in range(n):
            loads[i].wait()
            cp = pltpu.make_async_copy(bufs[i], outs[i].at[me], local_sems.at[i])
            cp.start()
            stores.append(cp)
        gather.forward()
        gather.finish()
        for cp in stores:
            cp.wait()

    dma = pltpu.SemaphoreType.DMA
    m = 3 * len(far)
    return pl.pallas_call(
        body, name=name,
        out_shape=[jax.ShapeDtypeStruct((N_CHIPS,) + a.shape, a.dtype) for a in arrs],
        in_specs=_any_specs(n), out_specs=_any_specs(n),
        scratch_shapes=[dma((m,)), dma((m,)), dma((m,)), dma((m,)), dma((n,))]
        + [pltpu.VMEM(a.shape, a.dtype) for a in arrs],
        compiler_params=pltpu.CompilerParams(vmem_limit_bytes=V7X_VMEM_LIMIT),
    )(*arrs)


def swap_halves(arrs, *, name):
    n = len(arrs)

    def body(*refs):
        ins, outs = refs[:n], refs[n:2 * n]
        send_sems, recv_sems = refs[2 * n:]
        x, y, c = _coords()
        cps = []
        for i in range(n):
            half = ins[i].shape[1] // 2
            cp = pltpu.make_async_remote_copy(ins[i].at[:, pl.ds((1 - c) * half, half)], outs[i], send_sems.at[i],
                                              recv_sems.at[i], device_id=(x, y, 1 - c), device_id_type=MESH)
            cp.start()
            cps.append(cp)
        for cp in cps:
            cp.wait()

    dma = pltpu.SemaphoreType.DMA
    return pl.pallas_call(
        body, name=name,
        out_shape=[jax.ShapeDtypeStruct((N_CHIPS, a.shape[1] // 2, a.shape[2]), a.dtype) for a in arrs],
        in_specs=_any_specs(n), out_specs=_any_specs(n), scratch_shapes=[dma((n,)), dma((n,))],
    )(*arrs)


def add_half(g, other, c_idx, *, name, out_dtype):
    _, R, C = g.shape
    half = R // 2
    tr = _tile(half, (256, 128, 64, 32, 16, 8))
    nt = half // tr

    def body(c_ref, g_ref, o_ref, out_ref):
        out_ref[...] = (g_ref[...].astype(F32) + o_ref[...].astype(F32)).astype(out_ref.dtype)

    return pl.pallas_call(
        body, name=name, out_shape=jax.ShapeDtypeStruct((N_CHIPS, half, C), out_dtype),
        grid_spec=pltpu.PrefetchScalarGridSpec(
            num_scalar_prefetch=1, grid=(N_CHIPS, nt),
            in_specs=[pl.BlockSpec((None, tr, C), lambda r, t, c_ref: (r, c_ref[0] * nt + t, 0)),
                      pl.BlockSpec((None, tr, C), lambda r, t, c_ref: (r, t, 0))],
            out_specs=pl.BlockSpec((None, tr, C), lambda r, t, c_ref: (r, t, 0))),
        compiler_params=_params(("parallel", "parallel")),
    )(c_idx, g, other)


class _ChipScatter:
    def __init__(self, ins, outs, send_sems, recv_sems):
        self.ins, self.outs, self.send_sems, self.recv_sems = ins, outs, send_sems, recv_sems
        self.x, self.y, self.c = _coords()

    def _copies(self):
        for i in range(len(self.ins)):
            for p in range(1, N_CHIPS):
                dst, px, py = _chip_of(self.x, self.y, p)
                s = i * 3 + p - 1
                yield pltpu.make_async_remote_copy(self.ins[i].at[dst], self.outs[i].at[p - 1], self.send_sems.at[s],
                                                   self.recv_sems.at[s], device_id=(px, py, self.c), device_id_type=MESH)

    def send(self):
        for cp in self._copies():
            cp.start()

    def finish(self):
        for cp in self._copies():
            cp.wait()


def scatter_to_chips(arrs, *, name):
    n = len(arrs)

    def body(*refs):
        scatter = _ChipScatter(refs[:n], refs[n:2 * n], *refs[2 * n:])
        scatter.send()
        scatter.finish()

    dma = pltpu.SemaphoreType.DMA
    return pl.pallas_call(
        body, name=name,
        out_shape=[jax.ShapeDtypeStruct((N_CHIPS - 1,) + a.shape[1:], a.dtype) for a in arrs],
        in_specs=_any_specs(n), out_specs=_any_specs(n), scratch_shapes=[dma((3 * n,)), dma((3 * n,))],
    )(*arrs)


def add_chips(h, got, r_idx, *, name):
    _, R, C = h.shape
    tr = _tile(R, (256, 128, 64, 32, 16, 8))

    def body(r_ref, h_ref, g_ref, out_ref):
        acc = h_ref[...].astype(F32)
        for p in range(N_CHIPS - 1):
            acc = acc + g_ref[p].astype(F32)
        out_ref[...] = acc

    return pl.pallas_call(
        body, name=name, out_shape=jax.ShapeDtypeStruct((R, C), F32),
        grid_spec=pltpu.PrefetchScalarGridSpec(
            num_scalar_prefetch=1, grid=(R // tr,),
            in_specs=[pl.BlockSpec((None, tr, C), lambda t, r_ref: (r_ref[0], t, 0)),
                      pl.BlockSpec((N_CHIPS - 1, tr, C), lambda t, r_ref: (0, t, 0))],
            out_specs=pl.BlockSpec((tr, C), lambda t, r_ref: (t, 0))),
        compiler_params=_params(("parallel",)),
    )(r_idx, h, got)


def join_halves(arrs, *, name):
    n = len(arrs)

    def body(*refs):
        ins, outs = refs[:n], refs[n:2 * n]
        send_sems, recv_sems, local_sems = refs[2 * n:2 * n + 3]
        bufs = refs[2 * n + 3:]
        x, y, c = _coords()
        loads, sends, stores = [], [], []
        for i in range(n):
            cp = pltpu.make_async_copy(ins[i], bufs[i], local_sems.at[i])
            cp.start()
            loads.append(cp)
        for i in range(n):
            half = ins[i].shape[0]
            cp = pltpu.make_async_remote_copy(ins[i], outs[i].at[pl.ds(c * half, half)], send_sems.at[i], recv_sems.at[i],
                                              device_id=(x, y, 1 - c), device_id_type=MESH)
            cp.start()
            sends.append(cp)
        for i in range(n):
            half = ins[i].shape[0]
            loads[i].wait()
            cp = pltpu.make_async_copy(bufs[i], outs[i].at[pl.ds(c * half, half)], local_sems.at[i])
            cp.start()
            stores.append(cp)
        for i in range(n):
            half = ins[i].shape[0]
            pltpu.make_async_remote_copy(ins[i], outs[i].at[pl.ds((1 - c) * half, half)], send_sems.at[i],
                                         recv_sems.at[i], device_id=(x, y, 1 - c), device_id_type=MESH).wait_recv()
        for i in range(n):
            sends[i].wait_send()
            stores[i].wait()

    dma = pltpu.SemaphoreType.DMA
    return pl.pallas_call(
        body, name=name,
        out_shape=[jax.ShapeDtypeStruct((2 * a.shape[0], a.shape[1]), a.dtype) for a in arrs],
        in_specs=_any_specs(n), out_specs=_any_specs(n),
        scratch_shapes=[dma((n,)), dma((n,)), dma((n,))] + [pltpu.VMEM(a.shape, a.dtype) for a in arrs],
        compiler_params=pltpu.CompilerParams(vmem_limit_bytes=V7X_VMEM_LIMIT),
    )(*arrs)


def pair_sums(grads, wire_dtypes, tag):
    c_idx = jnp.reshape(lax.axis_index("c"), (1,)).astype(jnp.int32)
    theirs = swap_halves(grads, name=f"rs_swap_halves_{tag}")
    return [add_half(g, o, c_idx, name=f"rs_add_half_{tag}{i}", out_dtype=wire_dtypes[i])
            for i, (g, o) in enumerate(zip(grads, theirs))]


def chip_sums(pairs, gots, tag):
    r_idx = jnp.reshape(2 * lax.axis_index("x") + lax.axis_index("y"), (1,)).astype(jnp.int32)
    return [add_chips(h, g, r_idx, name=f"rs_add_chips_{tag}{i}") for i, (h, g) in enumerate(zip(pairs, gots))]


def _to_heads(t, B, S):
    return t.reshape(B, S, -1, HEAD_DIM).transpose(0, 2, 1, 3)


def _from_heads(t, B, S):
    return t.transpose(0, 2, 1, 3).reshape(B * S, -1)


def _chip_major(w, axis):
    n = w.shape[axis] // N_CHIPS
    parts = w.reshape(w.shape[:axis] + (N_CHIPS, n) + w.shape[axis + 1:])
    return jnp.moveaxis(parts, axis, 0)


def _from_chip_major(g, axis):
    g = jnp.moveaxis(g, 0, axis)
    return g.reshape(g.shape[:axis] + (g.shape[axis] * g.shape[axis + 1],) + g.shape[axis + 2:])


def kernel(x, c, norm_mix, norm_ffn, w_mod, b_mod, w_qkv, w_o_attn, w_in_ssm, a_re, a_im, log_dt, b_re, b_im, c_re, c_im, d_skip, w_glu, b_glu, w_o_ssm, w_up, conv_w, conv_b, w_down, norm_out, w_fin, b_fin, loss_target, m_norm_mix, m_norm_ffn, m_w_mod, m_b_mod, m_w_qkv, m_w_o_attn, m_w_in_ssm, m_a_re, m_a_im, m_log_dt, m_b_re, m_b_im, m_c_re, m_c_im, m_d_skip, m_w_glu, m_b_glu, m_w_o_ssm, m_w_up, m_conv_w, m_conv_b, m_w_down, m_norm_out, m_w_fin, m_b_fin, v_norm_mix, v_norm_ffn, v_w_mod, v_b_mod, v_w_qkv, v_w_o_attn, v_w_in_ssm, v_a_re, v_a_im, v_log_dt, v_b_re, v_b_im, v_c_re, v_c_im, v_d_skip, v_w_glu, v_b_glu, v_w_o_ssm, v_w_up, v_conv_w, v_conv_b, v_w_down, v_norm_out, v_w_fin, v_b_fin):
    B, S, D = x.shape
    T = B * S
    F2 = conv_b.shape[1]
    F = F2 // 2
    G, P = a_re.shape[1], a_re.shape[2]
    H = b_re.shape[3]
    mx, my, mc = _coords()
    chip = 2 * mx + my
    dev = 4 * mx + 2 * my + mc
    BG = N_DEV * B
    mod_w = w_mod.shape[2]
    fin_w = w_fin.shape[1]

    c_all = all_gather8(c, name="gather_c").reshape(BG, D)
    c_act = silu_rows(c_all, name="silu_c")
    b_mod_mine = lax.dynamic_slice(b_mod, (0, chip * mod_w), (2, mod_w))
    b_fin_mine = lax.dynamic_slice(b_fin, (chip * fin_w,), (fin_w,))
    cond = [matmul(c_act, w_mod[i], bias=b_mod_mine[i], name=f"mod_proj_{i}") for i in range(2)]
    cond.append(matmul(c_act, w_fin, bias=b_fin_mine, name="fin_proj"))
    cond_all = all_gather8(jnp.concatenate(cond, axis=1), name="gather_cond")
    cond_all = cond_all[::2]
    cond_rows = lax.dynamic_slice(cond_all, (0, dev * B, 0), (N_CHIPS, B, cond_all.shape[2]))
    mods = []
    for i in range(2):
        full = cond_rows[:, :, i * mod_w:(i + 1) * mod_w].transpose(1, 0, 2).reshape(B, N_CHIPS * mod_w)
        mods.append([full[:, k * D:(k + 1) * D] for k in range(6)])
    fin = cond_rows[:, :, 2 * mod_w:].transpose(1, 0, 2).reshape(B, N_CHIPS * fin_w)
    sh_f, sc_f = fin[:, :D], fin[:, D:]

    rows1024 = jnp.concatenate([w_o_attn[0], w_in_ssm[0], w_glu[0], w_o_ssm[0], w_down.reshape(-1, D)], axis=0)
    shards = [w_qkv[0].astype(BF16), rows1024.astype(BF16), w_up[0].astype(BF16), w_up[1].astype(BF16)]
    W_qkv, *own_slots = gather_chip_shards(shards, [True, False, False, False], name="gather_weights")
    Dq = D // N_CHIPS
    Fq = F // N_CHIPS
    small =jnp.concatenate([conv_w.reshape(6, -1), jnp.pad(d_skip, ((0, 0), (0, conv_w.shape[2] - Dq))),
                             jnp.pad(b_glu, ((0, 0), (0, conv_w.shape[2] - Dq)))], axis=0)
    small_all = all_gather8(small, name="gather_small")[::2]
    conv_w_full = _from_chip_major(small_all[:, :6].reshape(N_CHIPS, 2, 3, -1), 2)
    d_skip_full = small_all[:, 6, :Dq].reshape(1, D)
    b_glu_full = small_all[:, 7, :Dq].reshape(D)

    x0 = x.reshape(T, D)
    tgt = loss_target.reshape(T, D)

    def ffn_fwd(xprev, y, gate, i):
        sh2, sc2 = mods[i][3], mods[i][4]
        xin, h2 = res_norm_mod_fwd(xprev, y, gate, norm_ffn[i], sh2, sc2, B=B, S=S, name=f"ffn_norm_{i}")
        up = matmul(h2, W_up[i], b_chips=True, name=f"ffn_up_{i}")
        act = conv_gate_fwd(up, conv_w_full[i], conv_b[i:i + 1], B=B, S=S, name=f"ffn_conv_{i}")
        yf = matmul(act, W_down[i], name=f"ffn_down_{i}")
        return xin, yf, (xin, h2, up, act, yf)

    sh1, sc1, g1 = mods[0][0], mods[0][1], mods[0][2]
    h1a = norm_mod_fwd(x0, norm_mix[0], sh1, sc1, B=B, S=S, name="att_norm")
    qkv = matmul(h1a, W_qkv, out_dtype=BF16, b_chips=True, name="att_qkv")
    o2, ltot, g_rows, W_up0, W_up1 = attn_fwd_pairs(qkv, shards[1:], own_slots, B=B, S=S, name="att_fwd")
    W_up = [W_up0, W_up1]
    W_o_attn = g_rows[:, 0 * Dq:1 * Dq].reshape(D, D)
    W_in = g_rows[:, 1 * Dq:2 * Dq].reshape(D, D)
    W_glu = g_rows[:, 2 * Dq:3 * Dq].reshape(D, D)
    W_o_ssm = g_rows[:, 3 * Dq:4 * Dq].reshape(D, D)
    W_down = [g_rows[:, 4 * Dq + i * Fq:4 * Dq + (i + 1) * Fq].reshape(F, D) for i in range(2)]
    ya = matmul(o2, W_o_attn, name="att_out")
    x1, yf0, ffn0 = ffn_fwd(x0, ya, g1, 0)

    lr, li, bbr, bbi = _ssm_discretize(a_re[0], a_im[0], log_dt[0], b_re[0], b_im[0])
    J = G // GROUPS_PER_BLOCK
    Wst = GROUPS_PER_BLOCK * P
    bre_blk = _block_diag_in(bbr).astype(BF16)
    bim_blk = _block_diag_in(bbi).astype(BF16)
    cre_blk = _block_diag_out(c_re[0]).astype(BF16)
    cim_blk = _block_diag_out(c_im[0]).astype(BF16)
    lr8 = jnp.broadcast_to(lr.reshape(J, 1, Wst), (J, SEGMENTS, Wst))
    li8 = jnp.broadcast_to(li.reshape(J, 1, Wst), (J, SEGMENTS, Wst))
    sh1s, sc1s, g1s = mods[1][0], mods[1][1], mods[1][2]
    x2, h1s = res_norm_mod_fwd(x1, yf0, mods[0][5], norm_mix[1], sh1s, sc1s, B=B, S=S, name="ssm_norm")
    h1p = _interleave(h1s, B, S)
    u = matmul(h1p, W_in, name="ssm_in")
    y_ssm = ssm_fwd(u, bre_blk, bim_blk, cre_blk, cim_blk, lr8, li8, d_skip_full, B=B, S=S, name="ssm_scan_fwd")
    zb = gelu_fwd(y_ssm, B=B, S=S, name="ssm_gelu")
    s_glu = matmul(zb, W_glu, bias=b_glu_full, name="ssm_glu_proj")
    gb = glu_fwd(y_ssm, s_glu, B=B, S=S, name="ssm_glu")
    ys_p = matmul(gb, W_o_ssm, name="ssm_out")
    ys = _deinterleave(ys_p, B, S)
    x3, yf1, ffn1 = ffn_fwd(x2, ys, g1s, 1)
    x4 = gate_res_fwd(x3, yf1, mods[1][5], B=B, S=S, name="ffn_res_1")

    dx4, dyf1, loss_p, dsh_f, dsc_f, dnorm_out, dg2_1 = final_loss(x4, tgt, norm_out, sh_f, sc_f, yf1, mods[1][5],
                                                                   B=B, S=S, name="loss_head")
    loss = lax.psum(jnp.sum(loss_p), ("x", "y", "c"))

    def ffn_bwd(dxo, dyf, i, saved, y_prev, gate_prev):
        xin, h2, up, act, yf = saved
        sc2 = mods[i][4]
        dact = matmul(dyf, W_down[i], tb=True, name=f"ffn_down_dx_{i}")
        dW_down = matmul(act, dyf, ta=True, out_dtype=BF16, name=f"ffn_down_dw_{i}")
        dup, dcw, dcb = conv_gate_bwd(up, dact, conv_w_full[i], conv_b[i:i + 1], B=B, S=S, name=f"ffn_conv_bwd_{i}")
        dh2 = matmul(dup, W_up[i], tb=True, b_chips=True, name=f"ffn_up_dx_{i}")
        dW_up = matmul(h2, dup, ta=True, b_chips=True, out_chips=True, out_dtype=BF16, name=f"ffn_up_dw_{i}")
        dxin, dy_prev, dsh2, dsc2, dnf, dgate_prev = norm_mod_bwd_gate(
            dh2, xin, dxo, norm_ffn[i], sc2, y_prev, gate_prev, B=B, S=S, name=f"ffn_norm_bwd_{i}")
        dconv_w = jnp.sum(dcw, axis=0).transpose(1, 0, 2).reshape(3, F2)
        return dxin, dy_prev, dgate_prev, dict(dW_down=dW_down, dW_up=dW_up, dconv_b=jnp.sum(dcb, axis=0).reshape(F2),
                                               dconv_w=dconv_w, dnorm_ffn=jnp.sum(dnf, axis=0), dsh2=dsh2, dsc2=dsc2)

    dx3, dys, dg1s, gf1 = ffn_bwd(dx4, dyf1, 1, ffn1, ys, g1s)
    gf1["dg2"] = dg2_1

    dys_p = _interleave(dys, B, S)
    dgb = matmul(dys_p, W_o_ssm, tb=True, name="ssm_out_dx")
    dW_o_ssm = matmul(gb, dys_p, ta=True, out_dtype=BF16, name="ssm_out_dw")
    ds_glu, dz1, db_glu = glu_bwd1(y_ssm, s_glu, dgb, B=B, S=S, name="ssm_glu_bwd1")
    dz2 = matmul(ds_glu, W_glu, tb=True, name="ssm_glu_dx")
    dW_glu = matmul(zb, ds_glu, ta=True, out_dtype=BF16, name="ssm_glu_dw")
    dy_ssm = glu_bwd2(y_ssm, dz1, dz2, B=B, S=S, name="ssm_glu_bwd2")
    du, dbre, dbim, dcre, dcim, dlr8, dli8, ddsk = ssm_bwd(u, dy_ssm, bre_blk, bim_blk, cre_blk, cim_blk, lr8, li8,
                                                           d_skip_full, B=B, S=S, name="ssm_scan_bwd")
    dub = du.astype(BF16)
    dh1p = matmul(dub, W_in, tb=True, name="ssm_in_dx")
    dW_in = matmul(h1p, dub, ta=True, out_dtype=BF16, name="ssm_in_dw")
    dx2, dyf0, dsh1s, dsc1s, dnm1, dg2_0 = norm_mod_bwd_gate(_deinterleave(dh1p, B, S), x2, dx3, norm_mix[1], sc1s,
                                                             yf0, mods[0][5], B=B, S=S, name="ssm_norm_bwd")
    dlr = jnp.sum(dlr8, axis=(0, 2)).reshape(G, P)
    dli = jnp.sum(dli8, axis=(0, 2)).reshape(G, P)
    dbbr = _block_diag_in_grad(jnp.sum(dbre, axis=0), G, P, H)
    dbbi = _block_diag_in_grad(jnp.sum(dbim, axis=0), G, P, H)
    dc_re = _block_diag_out_grad(jnp.sum(dcre, axis=0), G, H, P)
    dc_im = _block_diag_out_grad(jnp.sum(dcim, axis=0), G, H, P)
    dd_skip = jnp.sum(ddsk, axis=0).reshape(D)

    dx1, dya, dg1, gf0 = ffn_bwd(dx2, dyf0, 0, ffn0, ya, g1)
    gf0["dg2"] = dg2_0

    do2 = matmul(dya, W_o_attn, tb=True, out_dtype=BF16, name="att_out_dx")
    dW_o_attn = matmul(o2, dya, ta=True, out_dtype=BF16, name="att_out_dw")
    g_rows_cm = jnp.concatenate([dW_o_attn.reshape(N_CHIPS, Dq, D), dW_in.reshape(N_CHIPS, Dq, D),
                                 dW_glu.reshape(N_CHIPS, Dq, D), dW_o_ssm.reshape(N_CHIPS, Dq, D),
                                 gf0["dW_down"].reshape(N_CHIPS, Fq, D), gf1["dW_down"].reshape(N_CHIPS, Fq, D)], axis=1)
    pairs_a = pair_sums([g_rows_cm, gf0["dW_up"], gf1["dW_up"]], [BF16, BF16, BF16], "a")
    dqkv, *gots_a = attn_bwd_pairs(qkv, ltot, do2, pairs_a, B=B, S=S, name="att_bwd")
    dh1a = matmul(dqkv, _from_chip_major(W_qkv, 1), tb=True, name="att_qkv_dx")
    dW_qkv = _chip_major(matmul(h1a, dqkv, ta=True, b_chips=True, out_dtype=BF16, name="att_qkv_dw"), 1)
    grad_x, dsh1, dsc1, dnm0 = norm_mod_bwd(dh1a, x0, dx1, norm_mix[0], sc1, B=B, S=S, name="att_norm_bwd")

    dmod_rows = jnp.concatenate([dsh1, dsc1, dg1, gf0["dsh2"], gf0["dsc2"], gf0["dg2"],
                                 dsh1s, dsc1s, dg1s, gf1["dsh2"], gf1["dsc2"], gf1["dg2"], dsh_f, dsc_f], axis=1)
    dmod_all = all_gather8(dmod_rows, name="gather_dmod").reshape(BG, 14 * D)
    grad_w_mod = jnp.stack([
        matmul(c_act, lax.dynamic_slice(dmod_all, (0, i * 6 * D + chip * mod_w), (BG, mod_w)), ta=True,
               name=f"mod_dw_{i}") for i in range(2)])
    grad_w_fin = matmul(c_act, lax.dynamic_slice(dmod_all, (0, 12 * D + chip * fin_w), (BG, fin_w)), ta=True,
                        name="fin_dw")

    parts = [jnp.concatenate([jnp.sum(dnm0, axis=0), jnp.sum(dnm1, axis=0)]),
             jnp.concatenate([gf0["dnorm_ffn"], gf1["dnorm_ffn"]]),
             jnp.sum(dmod_rows[:, :12 * D], axis=0),
             dlr.reshape(-1), dli.reshape(-1), dbbr.reshape(-1), dbbi.reshape(-1), dc_re.reshape(-1), dc_im.reshape(-1),
             dd_skip, jnp.sum(db_glu, axis=0),
             jnp.sum(dnorm_out, axis=0), jnp.sum(dmod_rows[:, 12 * D:], axis=0),
             gf0["dconv_w"].reshape(-1), gf1["dconv_w"].reshape(-1), gf0["dconv_b"], gf1["dconv_b"]]
    sizes = [int(p.shape[0]) for p in parts]
    flat = jnp.concatenate(parts)
    width = 1024
    quantum = N_CHIPS * 16 * width
    padded = -(-flat.shape[0] // quantum) * quantum
    small_cm = jnp.pad(flat, (0, padded - flat.shape[0])).reshape(N_CHIPS, -1, width)

    pairs_b = pair_sums([dW_qkv, small_cm], [BF16, F32], "b")
    gots_b = scatter_to_chips(pairs_b, name="rs_scatter_to_chips")
    r_qkv, r_small, r_rows, r_up0, r_up1 = join_halves(
        chip_sums(pairs_b, gots_b, "b") + chip_sums(pairs_a, gots_a, "a"), name="rs_join_halves")
    grad_w_qkv = r_qkv[None]
    grad_w_o_attn = r_rows[0 * Dq:1 * Dq][None]
    grad_w_in_ssm = r_rows[1 * Dq:2 * Dq][None]
    grad_w_glu = r_rows[2 * Dq:3 * Dq][None]
    grad_w_o_ssm = r_rows[3 * Dq:4 * Dq][None]
    grad_w_down = r_rows[4 * Dq:].reshape(2, Fq, D)
    grad_w_up = jnp.stack([r_up0, r_up1])
    summed = all_gather8(r_small, name="gather_small_grads")[::2].reshape(-1)
    offs = [0]
    for s_ in sizes:
        offs.append(offs[-1] + s_)
    (s_nm, s_nf, s_bmod, s_lr, s_li, s_bbr, s_bbi, s_cre, s_cim, s_dsk, s_bglu, s_no, s_bfin, s_cw0, s_cw1, s_cb0,
     s_cb1) = [summed[offs[i]:offs[i + 1]] for i in range(len(sizes))]
    _, disc_vjp = jax.vjp(_ssm_discretize, a_re[0], a_im[0], log_dt[0], b_re[0], b_im[0])
    ga_re, ga_im, glog_dt, gb_re, gb_im = disc_vjp((s_lr.reshape(G, P), s_li.reshape(G, P), s_bbr.reshape(G, P, H),
                                                    s_bbi.reshape(G, P, H)))
    grad_norm_mix = s_nm.reshape(2, D)
    grad_norm_ffn = s_nf.reshape(2, D)
    grad_b_mod = s_bmod.reshape(2, 6 * D)
    grad_c_re = s_cre.reshape(1, G, H, P)
    grad_c_im = s_cim.reshape(1, G, H, P)
    grad_d_skip = lax.dynamic_slice(s_dsk, (chip * Dq,), (Dq,)).reshape(1, Dq)
    grad_b_glu = lax.dynamic_slice(s_bglu, (chip * Dq,), (Dq,)).reshape(1, Dq)
    cw_full = jnp.stack([s_cw0.reshape(3, F2), s_cw1.reshape(3, F2)])
    grad_conv_w = lax.dynamic_slice(cw_full, (0, 0, chip * (F2 // N_CHIPS)), (2, 3, F2 // N_CHIPS))
    grad_conv_b = jnp.stack([s_cb0, s_cb1])
    grad_norm_out = s_no
    grad_b_fin = s_bfin

    grads = dict(
        norm_mix=grad_norm_mix, norm_ffn=grad_norm_ffn, w_mod=grad_w_mod, b_mod=grad_b_mod, w_qkv=grad_w_qkv,
        w_o_attn=grad_w_o_attn, w_in_ssm=grad_w_in_ssm, a_re=ga_re[None], a_im=ga_im[None], log_dt=glog_dt[None],
        b_re=gb_re[None], b_im=gb_im[None], c_re=grad_c_re, c_im=grad_c_im, d_skip=grad_d_skip, w_glu=grad_w_glu,
        b_glu=grad_b_glu, w_o_ssm=grad_w_o_ssm, w_up=grad_w_up, conv_w=grad_conv_w, conv_b=grad_conv_b,
        w_down=grad_w_down, norm_out=grad_norm_out, w_fin=grad_w_fin, b_fin=grad_b_fin)
    weights = dict(
        norm_mix=norm_mix, norm_ffn=norm_ffn, w_mod=w_mod, b_mod=b_mod, w_qkv=w_qkv, w_o_attn=w_o_attn,
        w_in_ssm=w_in_ssm, a_re=a_re, a_im=a_im, log_dt=log_dt, b_re=b_re, b_im=b_im, c_re=c_re, c_im=c_im,
        d_skip=d_skip, w_glu=w_glu, b_glu=b_glu, w_o_ssm=w_o_ssm, w_up=w_up, conv_w=conv_w, conv_b=conv_b,
        w_down=w_down, norm_out=norm_out, w_fin=w_fin, b_fin=b_fin)
    m_in = dict(
        norm_mix=m_norm_mix, norm_ffn=m_norm_ffn, w_mod=m_w_mod, b_mod=m_b_mod, w_qkv=m_w_qkv, w_o_attn=m_w_o_attn,
        w_in_ssm=m_w_in_ssm, a_re=m_a_re, a_im=m_a_im, log_dt=m_log_dt, b_re=m_b_re, b_im=m_b_im, c_re=m_c_re,
        c_im=m_c_im, d_skip=m_d_skip, w_glu=m_w_glu, b_glu=m_b_glu, w_o_ssm=m_w_o_ssm, w_up=m_w_up, conv_w=m_conv_w,
        conv_b=m_conv_b, w_down=m_w_down, norm_out=m_norm_out, w_fin=m_w_fin, b_fin=m_b_fin)
    v_in = dict(
        norm_mix=v_norm_mix, norm_ffn=v_norm_ffn, w_mod=v_w_mod, b_mod=v_b_mod, w_qkv=v_w_qkv, w_o_attn=v_w_o_attn,
        w_in_ssm=v_w_in_ssm, a_re=v_a_re, a_im=v_a_im, log_dt=v_log_dt, b_re=v_b_re, b_im=v_b_im, c_re=v_c_re,
        c_im=v_c_im, d_skip=v_d_skip, w_glu=v_w_glu, b_glu=v_b_glu, w_o_ssm=v_w_o_ssm, w_up=v_w_up, conv_w=v_conv_w,
        conv_b=v_conv_b, w_down=v_w_down, norm_out=v_norm_out, w_fin=v_w_fin, b_fin=v_b_fin)
    names = list(weights)
    for n_ in names:
        grads[n_] = grads[n_].reshape(weights[n_].shape)

    big = ("w_mod", "w_qkv", "w_o_attn", "w_in_ssm", "w_glu", "w_o_ssm", "w_up", "w_down", "w_fin")
    delta, new_m, new_v = {}, {}, {}
    for n_ in big:
        shp = weights[n_].shape
        two_d = lambda a: a.reshape(-1, shp[-1])
        d_, m_, v_ = adamw(two_d(weights[n_]), two_d(grads[n_]), two_d(m_in[n_]), two_d(v_in[n_]), name=f"adamw_{n_}")
        delta[n_], new_m[n_], new_v[n_] = d_.reshape(shp), m_.reshape(shp), v_.reshape(shp)
    rest = sorted([n_ for n_ in names if n_ not in big], key=lambda n_: math.prod(weights[n_].shape) % 1024 != 0)

    def pack(tree):
        f = jnp.concatenate([tree[n_].reshape(-1) for n_ in rest])
        pad_to = -(-f.shape[0] // (8 * width)) * (8 * width)
        return jnp.pad(f, (0, pad_to - f.shape[0]), constant_values=1.0).reshape(-1, width)

    d_, m_, v_ = adamw(pack(weights), pack(grads), pack(m_in), pack(v_in), name="adamw_small")
    off = 0
    for n_ in rest:
        sz = int(math.prod(weights[n_].shape))
        shp = weights[n_].shape
        delta[n_] = d_.reshape(-1)[off:off + sz].reshape(shp)
        new_m[n_] = m_.reshape(-1)[off:off + sz].reshape(shp)
        new_v[n_] = v_.reshape(-1)[off:off + sz].reshape(shp)
        off += sz

    return (loss, grad_x.reshape(B, S, D), *[grads[n_] for n_ in names], *[delta[n_] for n_ in names],
            *[new_m[n_] for n_ in names], *[new_v[n_] for n_ in names])
```

```python
import math

import jax
import jax.numpy as jnp
from jax import lax
from jax.experimental import pallas as pl
from jax.experimental.pallas import tpu as pltpu

F32 = jnp.float32
BF16 = jnp.bfloat16
MESH = pl.DeviceIdType.MESH

HEAD_DIM = 64
SSM_GROUP = 16
STATE = 64
GROUPS_PER_BLOCK = 8
SEGMENTS = 16
SCAN_UNROLL = 4
EPS = 1e-6
ADAM_LR = 0.001
ADAM_B1 = 0.9
ADAM_B2 = 0.999
ADAM_EPS = 1e-08
ADAM_WD = 0.01
ADAM_STEP = 10
N_CHIPS = 4
N_DEV = 8
V7X_VMEM_LIMIT = 56 * 1024 * 1024
ATT_BLOCK = 128
ATT_HEADS = 8
ATT_HEADS_BWD = 8


def _tile(n, prefs):
    for p in prefs:
        if n % p == 0:
            return p
    return n


def _params(sem, vmem=V7X_VMEM_LIMIT):
    return pltpu.CompilerParams(dimension_semantics=sem, vmem_limit_bytes=vmem)


def matmul(a, b, *, ta=False, tb=False, bias=None, out_dtype=F32, b_chips=False, out_chips=False, name):
    a_parts = a.shape[0] if a.ndim == 3 else 1
    if a_parts > 1:
        assert not ta
        M, K = a.shape[1], a_parts * a.shape[2]
    elif ta:
        K, M = a.shape
    else:
        M, K = a.shape
    b_parts = b.shape[0] if b_chips else 1
    b_rows, b_cols = (b.shape[1], b_parts * b.shape[2]) if b_chips else b.shape
    if tb:
        N, Kb = b_rows, b_cols
    else:
        Kb, N = b_rows, b_cols
    assert K == Kb, (a.shape, b.shape, ta, tb)
    n_cut = math.gcd(N // (N_CHIPS if out_chips else 1), N // (b_parts if not tb else 1))
    k_cut = math.gcd(K // (b_parts if tb else 1), K // a_parts)
    tm = _tile(M, (1024, 1408, 512, 256, 128))
    tn = _tile(n_cut, (1024, 1408, 768, 512, 256, 128))
    tk = k_cut if k_cut <= 2816 else _tile(k_cut, (1024, 512, 256, 128))
    nk = K // tk
    npc = N // N_CHIPS // tn
    npb = N // b_parts // tn
    kpb = K // b_parts // tk
    kpa = K // a_parts // tk
    dims = (((0,) if ta else (1,), (1,) if tb else (0,)), ((), ()))

    def body(*refs):
        a_ref, b_ref = refs[:2]
        bias_ref = refs[2] if bias is not None else None
        o_ref = refs[-2] if nk > 1 else refs[-1]

        def finish(r):
            if bias_ref is not None:
                r = r + bias_ref[...]
            o_ref[...] = r.astype(o_ref.dtype)

        prod = lax.dot_general(a_ref[...].astype(BF16), b_ref[...].astype(BF16), dims, preferred_element_type=F32)
        if nk == 1:
            finish(prod)
            return
        acc_ref = refs[-1]
        k = pl.program_id(2)

        @pl.when(k == 0)
        def _():
            acc_ref[...] = prod

        @pl.when(k > 0)
        def _():
            acc_ref[...] += prod

        @pl.when(k == nk - 1)
        def _():
            finish(acc_ref[...])

    if a_parts > 1:
        a_spec = pl.BlockSpec((None, tm, tk), lambda i, j, k: (lax.div(k, kpa), i, lax.rem(k, kpa)))
    else:
        a_spec = pl.BlockSpec((tk, tm), lambda i, j, k: (k, i)) if ta else pl.BlockSpec((tm, tk), lambda i, j, k: (i, k))
    if not b_chips:
        b_spec = pl.BlockSpec((tn, tk), lambda i, j, k: (j, k)) if tb else pl.BlockSpec((tk, tn), lambda i, j, k: (k, j))
    elif tb:
        b_spec = pl.BlockSpec((None, tn, tk), lambda i, j, k: (lax.div(k, kpb), j, lax.rem(k, kpb)))
    else:
        b_spec = pl.BlockSpec((None, tk, tn), lambda i, j, k: (lax.div(j, npb), k, lax.rem(j, npb)))
    in_specs = [a_spec, b_spec]
    args = [a, b]
    if bias is not None:
        in_specs.append(pl.BlockSpec((1, tn), lambda i, j, k: (0, j)))
        args.append(bias.reshape(1, N).astype(F32))
    if out_chips:
        out_shape = jax.ShapeDtypeStruct((N_CHIPS, M, N // N_CHIPS), out_dtype)
        out_spec = pl.BlockSpec((None, tm, tn), lambda i, j, k: (lax.div(j, npc), i, lax.rem(j, npc)))
    else:
        out_shape = jax.ShapeDtypeStruct((M, N), out_dtype)
        out_spec = pl.BlockSpec((tm, tn), lambda i, j, k: (i, j))
    return pl.pallas_call(
        body, name=name,
        out_shape=out_shape,
        grid=(M // tm, N // tn, nk),
        in_specs=in_specs,
        out_specs=out_spec,
        scratch_shapes=[pltpu.VMEM((tm, tn), F32)] if nk > 1 else [],
        compiler_params=_params(("parallel", "parallel", "arbitrary")),
    )(*args)


def rowwise(fn, tiled, per_seq, glob, out_tiled, out_seq, *, B, S, name, rows=512):
    tm = _tile(S, (rows, 128, 64, 32, 16, 8))
    nt = S // tm
    n_in = len(tiled) + len(per_seq) + len(glob)
    n_ot = len(out_tiled)

    def body(*refs):
        ins = refs[:n_in]
        outs = refs[n_in:]
        vals = fn(*[r[...] for r in ins])
        if not isinstance(vals, (tuple, list)):
            vals = (vals,)
        assert len(vals) == len(outs), (name, len(vals), len(outs))
        for o_ref, v in zip(outs[:n_ot], vals[:n_ot]):
            o_ref[...] = v.astype(o_ref.dtype)
        t = pl.program_id(1)
        for o_ref, v in zip(outs[n_ot:], vals[n_ot:]):
            def first(o_ref=o_ref, v=v):
                o_ref[...] = v.astype(F32)

            def later(o_ref=o_ref, v=v):
                o_ref[...] += v.astype(F32)

            pl.when(t == 0)(first)
            pl.when(t > 0)(later)

    in_specs = [pl.BlockSpec((tm, a.shape[1]), lambda b, t: (b * nt + t, 0)) for a in tiled]
    in_specs += [pl.BlockSpec((None, 1, a.shape[1]), lambda b, t: (b, 0, 0)) for a in per_seq]
    in_specs += [pl.BlockSpec(a.shape, lambda b, t: (0,) * a.ndim) for a in glob]
    out_shape = [jax.ShapeDtypeStruct((B * S, w), dt) for w, dt in out_tiled]
    out_shape += [jax.ShapeDtypeStruct((B, 1, w), F32) for w in out_seq]
    out_specs = [pl.BlockSpec((tm, w), lambda b, t: (b * nt + t, 0)) for w, _ in out_tiled]
    out_specs += [pl.BlockSpec((None, 1, w), lambda b, t: (b, 0, 0)) for w in out_seq]
    res = pl.pallas_call(
        body, name=name, out_shape=out_shape, grid=(B, nt), in_specs=in_specs, out_specs=out_specs,
        compiler_params=_params(("parallel", "arbitrary")),
    )(*tiled, *[a.reshape(B, 1, a.shape[1]) for a in per_seq], *glob)
    res = list(res)
    for i in range(n_ot, len(res)):
        res[i] = res[i].reshape(B, res[i].shape[-1])
    return res


def _rms(x):
    r = lax.rsqrt(jnp.mean(x * x, axis=-1, keepdims=True) + EPS)
    return x * r, r


def norm_mod_fwd(x, g, sh, sc, *, B, S, name):
    def fn(x, sh, sc, g):
        xn, _ = _rms(x)
        return (xn * g) * (1.0 + sc) + sh

    return rowwise(fn, [x], [sh, sc], [g.reshape(1, -1)], [(x.shape[1], BF16)], [], B=B, S=S, name=name)[0]


def _norm_mod_bwd_math(dh, x, sc, g):
    xn, r = _rms(x)
    y = xn * g
    dy = dh * (1.0 + sc)
    dxn = dy * g
    dx = r * (dxn - xn * jnp.mean(dxn * xn, axis=-1, keepdims=True))
    dsh = jnp.sum(dh, axis=0, keepdims=True)
    dsc = jnp.sum(dh * y, axis=0, keepdims=True)
    dg = jnp.sum(dy * xn, axis=0, keepdims=True)
    return dx, dsh, dsc, dg


def norm_mod_bwd(dh, x, dres, g, sc, *, B, S, name):
    D = x.shape[1]

    def fn(dh, x, dres, sc, g):
        dx, dsh, dsc, dg = _norm_mod_bwd_math(dh.astype(F32), x, sc, g)
        return dres + dx, dsh, dsc, dg

    return rowwise(fn, [dh, x, dres], [sc], [g.reshape(1, -1)], [(D, F32)], [D, D, D], B=B, S=S, name=name)


def norm_mod_bwd_gate(dh, x, dres, g, sc, y_prev, gate_prev, *, B, S, name):
    D = x.shape[1]

    def fn(dh, x, dres, y, sc, gate, g):
        dx, dsh, dsc, dg = _norm_mod_bwd_math(dh.astype(F32), x, sc, g)
        dx = dres + dx
        return dx, gate * dx, dsh, dsc, dg, jnp.sum(dx * y, axis=0, keepdims=True)

    return rowwise(fn, [dh, x, dres, y_prev], [sc, gate_prev], [g.reshape(1, -1)], [(D, F32), (D, BF16)],
                   [D, D, D, D], B=B, S=S, name=name)


def gate_res_fwd(x, y, gate, *, B, S, name):
    return rowwise(lambda x, y, g: x + g * y, [x, y], [gate], [], [(x.shape[1], F32)], [], B=B, S=S, name=name)[0]


def res_norm_mod_fwd(x, y, gate, g, sh, sc, *, B, S, name):
    D = x.shape[1]

    def fn(x, y, gate, sh, sc, g):
        x = x + gate * y
        xn, _ = _rms(x)
        return x, (xn * g) * (1.0 + sc) + sh

    return rowwise(fn, [x, y], [gate, sh, sc], [g.reshape(1, -1)], [(D, F32), (D, BF16)], [], B=B, S=S, name=name)


def final_loss(x, tgt, g, sh, sc, y_prev, gate_prev, *, B, S, name):
    D = x.shape[1]

    def fn(x, tgt, y_prev, sh, sc, gate, g):
        xn, _ = _rms(x)
        y = (xn * g) * (1.0 + sc) + sh
        err = y - tgt
        loss = 0.5 * jnp.sum(err * err, axis=0, keepdims=True) * (1.0 / D)
        dx, dsh, dsc, dg = _norm_mod_bwd_math(err * (1.0 / D), x, sc, g)
        return dx, gate * dx, loss, dsh, dsc, dg, jnp.sum(dx * y_prev, axis=0, keepdims=True)

    return rowwise(fn, [x, tgt, y_prev], [sh, sc, gate_prev], [g.reshape(1, -1)], [(D, F32), (D, BF16)],
                   [D, D, D, D, D], B=B, S=S, name=name)


def _gelu(y):
    c0 = math.sqrt(2.0 / math.pi)
    t = jnp.tanh(c0 * (y + 0.044715 * (y * y * y)))
    return 0.5 * y * (1.0 + t), t


def _sigmoid(s):
    return 1.0 / (1.0 + jnp.exp(-s))


def gelu_fwd(y, *, B, S, name):
    return rowwise(lambda y: _gelu(y)[0], [y], [], [], [(y.shape[1], BF16)], [], B=B, S=S, name=name)[0]


def glu_fwd(y, s, *, B, S, name):
    return rowwise(lambda y, s: _gelu(y)[0] * _sigmoid(s), [y, s], [], [], [(y.shape[1], BF16)], [], B=B, S=S,
                   name=name)[0]


def glu_bwd1(y, s, dg, *, B, S, name):
    D = y.shape[1]

    def fn(y, s, dg):
        z = _gelu(y)[0]
        sig = _sigmoid(s)
        ds = dg * z * sig * (1.0 - sig)
        return ds, dg * sig, jnp.sum(ds, axis=0, keepdims=True)

    return rowwise(fn, [y, s, dg], [], [], [(D, BF16), (D, F32)], [D], B=B, S=S, name=name)


def glu_bwd2(y, dz1, dz2, *, B, S, name):
    D = y.shape[1]
    c0 = math.sqrt(2.0 / math.pi)

    def fn(y, dz1, dz2):
        _, t = _gelu(y)
        dgelu = 0.5 * (1.0 + t) + 0.5 * y * (1.0 - t * t) * c0 * (1.0 + 3.0 * 0.044715 * y * y)
        return (dz1 + dz2) * dgelu

    return rowwise(fn, [y, dz1, dz2], [], [], [(D, F32)], [], B=B, S=S, name=name)[0]


def silu_rows(c, *, name):
    R, W = c.shape
    return rowwise(lambda c: c * _sigmoid(c), [c], [], [], [(W, F32)], [], B=1, S=R, name=name)[0]


def conv_gate_fwd(up, cw, cb, *, B, S, name):
    F = up.shape[1] // 2
    tn = _tile(F, (256, 128))
    nF = F // tn

    def body(g_ref, v_ref, wg_ref, wv_ref, bg_ref, bv_ref, o_ref):
        rows = lax.broadcasted_iota(jnp.int32, (S, tn), 0)

        def conv(x, w_ref):
            x1 = jnp.where(rows >= 1, pltpu.roll(x, 1, 0), 0.0)
            x2 = jnp.where(rows >= 2, pltpu.roll(x, 2, 0), 0.0)
            return w_ref[2:3, :] * x + w_ref[1:2, :] * x1 + w_ref[0:1, :] * x2

        gc = conv(g_ref[...].astype(F32), wg_ref) + bg_ref[...]
        vc = conv(v_ref[...].astype(F32), wv_ref) + bv_ref[...]
        o_ref[...] = (gc * _sigmoid(gc) * vc).astype(o_ref.dtype)

    def cols(off):
        return pl.BlockSpec((S, tn), lambda b, j: (b, j + off))

    def vec(rows, off):
        return pl.BlockSpec((rows, tn), lambda b, j: (0, j + off))

    return pl.pallas_call(
        body, name=name, out_shape=jax.ShapeDtypeStruct((B * S, F), BF16), grid=(B, nF),
        in_specs=[cols(0), cols(nF), vec(3, 0), vec(3, nF), vec(1, 0), vec(1, nF)],
        out_specs=pl.BlockSpec((S, tn), lambda b, j: (b, j)),
        compiler_params=_params(("parallel", "parallel")),
    )(up, up, cw, cw, cb, cb)


def conv_gate_bwd(up, dact, cw, cb, *, B, S, name):
    F = up.shape[1] // 2
    tn = _tile(F, (256, 128))
    nF = F // tn

    def body(g_ref, v_ref, da_ref, wg_ref, wv_ref, bg_ref, bv_ref, o_ref, dw_ref, db_ref):
        rows = lax.broadcasted_iota(jnp.int32, (S, tn), 0)

        def earlier(x, k):
            return jnp.where(rows >= k, pltpu.roll(x, k, 0), 0.0)

        def later(x, k):
            return jnp.where(rows < S - k, pltpu.roll(x, S - k, 0), 0.0)

        def conv(x, w_ref):
            x1, x2 = earlier(x, 1), earlier(x, 2)
            return w_ref[2:3, :] * x + w_ref[1:2, :] * x1 + w_ref[0:1, :] * x2, x1, x2

        def back(d, x, x1, x2, w_ref, half):
            o_ref[half] = (w_ref[2:3, :] * d + w_ref[1:2, :] * later(d, 1) + w_ref[0:1, :] * later(d, 2)
                           ).astype(o_ref.dtype)
            dw_ref[half] = jnp.concatenate([jnp.sum(d * x2, axis=0, keepdims=True),
                                            jnp.sum(d * x1, axis=0, keepdims=True),
                                            jnp.sum(d * x, axis=0, keepdims=True)], axis=0)
            return jnp.sum(d, axis=0, keepdims=True)

        g, v, da = g_ref[...].astype(F32), v_ref[...].astype(F32), da_ref[...].astype(F32)
        gc, g1, g2 = conv(g, wg_ref)
        vc, v1, v2 = conv(v, wv_ref)
        gc = gc + bg_ref[...]
        vc = vc + bv_ref[...]
        sig = _sigmoid(gc)
        dg = da * vc * (sig * (1.0 + gc * (1.0 - sig)))
        dv = da * (gc * sig)
        db_ref[...] = jnp.concatenate([back(dg, g, g1, g2, wg_ref, 0), back(dv, v, v1, v2, wv_ref, 1)], axis=0)

    def cols(off):
        return pl.BlockSpec((S, tn), lambda b, j: (b, j + off))

    def vec(rows, off):
        return pl.BlockSpec((rows, tn), lambda b, j: (0, j + off))

    return pl.pallas_call(
        body, name=name,
        out_shape=[jax.ShapeDtypeStruct((2, B * S, F), BF16), jax.ShapeDtypeStruct((B, 2, 3, F), F32),
                   jax.ShapeDtypeStruct((B, 2, F), F32)],
        grid=(B, nF),
        in_specs=[cols(0), cols(nF), cols(0), vec(3, 0), vec(3, nF), vec(1, 0), vec(1, nF)],
        out_specs=[pl.BlockSpec((2, S, tn), lambda b, j: (0, b, j)),
                   pl.BlockSpec((None, 2, 3, tn), lambda b, j: (b, 0, 0, j)),
                   pl.BlockSpec((None, 2, tn), lambda b, j: (b, 0, j))],
        compiler_params=_params(("parallel", "parallel")),
    )(up, up, dact, cw, cw, cb, cb)


MASKED_LOG = -1e30


def _split2(x):
    bits = lax.bitcast_convert_type(x, jnp.uint32) & jnp.uint32(0xFFFF0000)
    hi = lax.bitcast_convert_type(bits, F32)
    return hi.astype(BF16), (x - hi).astype(BF16)


def _nt(a, b):
    return lax.dot_general(a, b, (((1,), (1,)), ((), ())), preferred_element_type=F32)


def _tn(a, b):
    return lax.dot_general(a, b, (((0,), (0,)), ((), ())), preferred_element_type=F32)


def _att_scores(q, k, mask, prescaled=False):
    z = _nt(q, k)
    if not prescaled:
        z = z * (HEAD_DIM ** -0.5)
    e = jnp.exp(-jnp.abs(z))
    sp = jnp.log(1.0 + e)
    lb = jnp.minimum(z, 0.0) - sp
    l1 = lb - z
    if mask is not None:
        lb = jnp.where(mask, lb, MASKED_LOG)
        l1 = jnp.where(mask, l1, 0.0)
    return z, lb, l1, e


def _col_to_row(col, eye):
    return jnp.sum(jnp.where(eye, col, 0.0), axis=0, keepdims=True)


def _row_to_col(row, eye):
    return jnp.sum(jnp.where(eye, row, 0.0), axis=1, keepdims=True)


def _wide_consts(T, W):
    r = lax.broadcasted_iota(jnp.int32, (W, W), 0)
    c = lax.broadcasted_iota(jnp.int32, (W, W), 1)
    two = lambda m: jnp.concatenate([m.astype(BF16)] * 2, axis=0)
    qrow = lax.broadcasted_iota(jnp.int32, (T, W), 0)
    kcol = lax.broadcasted_iota(jnp.int32, (T, W), 1)
    er = lax.broadcasted_iota(jnp.int32, (T, T), 0)
    ec = lax.broadcasted_iota(jnp.int32, (T, T), 1)
    return two(r > c), two(r <= c), two(r < c), qrow, kcol, er == ec


def _pair_masks(x, first):
    zero = jnp.zeros_like(x)
    return jnp.where(first, x, zero), jnp.where(first, zero, x)


def attn_fwd_pairs(qkv, shards=(), slots=(), *, B, S, name):
    D = qkv.shape[1] // 3
    H = D // HEAD_DIM
    T = ATT_BLOCK
    W = 2 * T
    nq = S // T
    G = _tile(H, (ATT_HEADS, 2))
    P = G // 2
    LW = 2 * HEAD_DIM * P
    nsec = D // LW
    n = len(shards)
    n_steps = B * nsec

    def body(*refs):
        q_ref, k_ref, v_ref = refs[:3]
        o_ref, l_ref = refs[3 + 2 * n:5 + 2 * n]
        step_id = pl.program_id(0) * nsec + pl.program_id(1)
        if n:
            gather = _ShardGather(refs[3:3 + n], refs[5 + 2 * n:5 + 3 * n], *refs[5 + 3 * n:])
            pl.when(step_id == 0)(gather.send)
            pl.when(step_id == n_steps - 1)(gather.forward)
        later2, _, _, qrow, kcol, eye = _wide_consts(T, W)
        blk = lax.broadcasted_iota(jnp.int32, (nq, T), 0)
        first_q = lax.broadcasted_iota(jnp.int32, (T, 2 * HEAD_DIM), 1) < HEAD_DIM
        first_k = lax.broadcasted_iota(jnp.int32, (W, 2 * HEAD_DIM), 1) < HEAD_DIM

        def lanes(j):
            return slice(j * 2 * HEAD_DIM, (j + 1) * 2 * HEAD_DIM)

        def step(qms, k0, st, mask):
            accs, runs = st
            parts, lbs, sums = [], [], []
            for g in range(G):
                _, lb, l1, _ = _att_scores(qms[g], k_ref[pl.ds(k0, W), lanes(g // 2)], mask, prescaled=True)
                parts.append(jnp.concatenate(_split2(l1), axis=1))
                lbs.append(lb)
                sums.append(jnp.sum(l1, axis=1, keepdims=True))
            suf = jnp.dot(jnp.concatenate(parts, axis=0), later2, preferred_element_type=F32)
            new_accs, new_runs = [], []
            for j in range(P):
                vms = _pair_masks(v_ref[pl.ds(k0, W), lanes(j)], first_k)
                acc = accs[j]
                for h in range(2):
                    g = 2 * j + h
                    w = jnp.exp(lbs[g] + suf[g * T:(g + 1) * T] + runs[g])
                    acc = acc + jnp.dot(w.astype(BF16), vms[h], preferred_element_type=F32)
                    new_runs.append(runs[g] + sums[g])
                new_accs.append(acc)
            return tuple(new_accs), tuple(new_runs)

        def qblock(i, totals):
            q0 = pl.multiple_of(i * T, T)
            qms = []
            for j in range(P):
                qms.extend(_pair_masks(q_ref[pl.ds(q0, T), lanes(j)] * (HEAD_DIM ** -0.5), first_q))
            half = jnp.right_shift(i, 1)
            last = half * W
            k_last = pl.multiple_of(last, W)
            mask = (k_last + kcol) < (q0 + qrow)
            st = (tuple(jnp.zeros((T, 2 * HEAD_DIM), F32) for _ in range(P)),
                  tuple(jnp.zeros((T, 1), F32) for _ in range(G)))
            st = step(qms, k_last, st, mask)

            def kblock(jj, st):
                return step(qms, pl.multiple_of(last - jj * W, W), st, None)

            accs, runs = lax.fori_loop(1, half + 1, kblock, st)
            for j in range(P):
                o_ref[pl.ds(q0, T), lanes(j)] = accs[j].astype(o_ref.dtype)
            return tuple(jnp.where(blk == i, _col_to_row(runs[g], eye), totals[g]) for g in range(G))

        totals = lax.fori_loop(0, nq, qblock, tuple(jnp.zeros((nq, T), F32) for _ in range(G)))
        for g in range(G):
            l_ref[g] = totals[g]
        if n:
            pl.when(step_id == n_steps - 1)(gather.finish)

    def cols(section):
        return pl.BlockSpec((S, LW), lambda b, h: (b, section * nsec + h))

    lspec = pl.BlockSpec((None, G, nq, T), lambda b, h: (b, h, 0, 0))
    dma = pltpu.SemaphoreType.DMA
    return pl.pallas_call(
        body, name=name,
        out_shape=[jax.ShapeDtypeStruct((B * S, D), BF16), jax.ShapeDtypeStruct((B, H, nq, T), F32)]
        + [jax.ShapeDtypeStruct(s.shape, s.dtype) for s in slots],
        grid=(B, nsec), in_specs=[cols(0), cols(1), cols(2)] + _any_specs(2 * n),
        out_specs=[cols(0), lspec] + _any_specs(n),
        input_output_aliases={3 + n + i: 2 + i for i in range(n)},
        scratch_shapes=[dma((3 * n,))] * 4 if n else [],
        compiler_params=_params(("arbitrary", "arbitrary")),
    )(qkv, qkv, qkv, *shards, *slots)


def attn_bwd_pairs(qkv, ltot, do, partials=(), *, B, S, name):
    D = qkv.shape[1] // 3
    H = D // HEAD_DIM
    T = ATT_BLOCK
    W = 2 * T
    nq = S // T
    scale = HEAD_DIM ** -0.5
    G = _tile(H, (ATT_HEADS_BWD, 2))
    P = G // 2
    LW = 2 * HEAD_DIM * P
    nsec = D // LW
    n = len(partials)
    n_steps = B * nsec

    def body(*refs):
        q_ref, k_ref, v_ref, l_ref, do_ref = refs[:5]
        d_ref = refs[5 + n]
        dk_acc, dv_acc = refs[6 + 2 * n:8 + 2 * n]
        step_id = pl.program_id(0) * nsec + pl.program_id(1)
        if n:
            scatter = _ChipScatter(refs[5:5 + n], refs[6 + n:6 + 2 * n], *refs[8 + 2 * n:])
            pl.when(step_id == 0)(scatter.send)
        _, upto2, before2, qrow, kcol, eye = _wide_consts(T, W)
        blk = lax.broadcasted_iota(jnp.int32, (nq, T), 0)
        first_q = lax.broadcasted_iota(jnp.int32, (T, 2 * HEAD_DIM), 1) < HEAD_DIM
        first_k = lax.broadcasted_iota(jnp.int32, (W, 2 * HEAD_DIM), 1) < HEAD_DIM
        dk_acc[...] = jnp.zeros_like(dk_acc)
        dv_acc[...] = jnp.zeros_like(dv_acc)

        def lanes(j):
            return slice(j * 2 * HEAD_DIM, (j + 1) * 2 * HEAD_DIM)

        def step(qms, doms, tots, k0, st, mask):
            dqs, runs_l, runs_d = st
            sc = []
            for g in range(G):
                z, lb, l1, _ = _att_scores(qms[g], k_ref[pl.ds(k0, W), lanes(g // 2)], mask, prescaled=True)
                beta = 0.5 * jnp.tanh(0.5 * z) + 0.5
                omb = 1.0 - beta
                if mask is not None:
                    beta = jnp.where(mask, beta, 0.0)
                dw = _nt(doms[g], v_ref[pl.ds(k0, W), lanes(g // 2)])
                sc.append((lb, jnp.concatenate(_split2(l1), axis=1), jnp.sum(l1, axis=1, keepdims=True), dw, beta, omb))
            pre = jnp.dot(jnp.concatenate([s[1] for s in sc], axis=0), upto2, preferred_element_type=F32)
            dlws = []
            for j in range(P):
                dv = None
                for h in range(2):
                    g = 2 * j + h
                    w = jnp.exp(sc[g][0] + (tots[g] - (pre[g * T:(g + 1) * T] + runs_l[g])))
                    t = _tn(w.astype(BF16), doms[g])
                    dv = t if dv is None else dv + t
                    dlws.append(sc[g][3] * w)
                dv_acc[j, pl.ds(k0, W), :] += dv
            pre_d = jnp.dot(jnp.concatenate([jnp.concatenate(_split2(d), axis=1) for d in dlws], axis=0), before2,
                            preferred_element_type=F32)
            new_dqs, new_l, new_d = [], [], []
            for j in range(P):
                kms = _pair_masks(k_ref[pl.ds(k0, W), lanes(j)], first_k)
                dq, dk = dqs[j], None
                for h in range(2):
                    g = 2 * j + h
                    _, _, rowsum, _, beta, omb = sc[g]
                    dl1 = pre_d[g * T:(g + 1) * T] + runs_d[g]
                    dz = (dlws[g] * omb - dl1 * beta).astype(BF16)
                    dq = dq + jnp.dot(dz, kms[h], preferred_element_type=F32)
                    t = _tn(dz, qms[g])
                    dk = t if dk is None else dk + t
                    new_l.append(runs_l[g] + rowsum)
                    new_d.append(runs_d[g] + jnp.sum(dlws[g], axis=1, keepdims=True))
                dk_acc[j, pl.ds(k0, W), :] += dk
                new_dqs.append(dq)
            return tuple(new_dqs), tuple(new_l), tuple(new_d)

        def qblock(i, carry0):
            q0 = pl.multiple_of(i * T, T)
            qms, doms = [], []
            for j in range(P):
                qms.extend(_pair_masks(q_ref[pl.ds(q0, T), lanes(j)] * scale, first_q))
                doms.extend(_pair_masks(do_ref[pl.ds(q0, T), lanes(j)], first_q))
            tots = [_row_to_col(jnp.sum(jnp.where(blk == i, l_ref[g], 0.0), axis=0, keepdims=True), eye)
                    for g in range(G)]
            z1 = tuple(jnp.zeros((T, 1), F32) for _ in range(G))
            st = (tuple(jnp.zeros((T, 2 * HEAD_DIM), F32) for _ in range(P)), z1, z1)

            def kblock(j, st):
                return step(qms, doms, tots, pl.multiple_of(j * W, W), st, None)

            half = jnp.right_shift(i, 1)
            st = lax.fori_loop(0, half, kblock, st)
            k_last = pl.multiple_of(half * W, W)
            dqs, _, _ = step(qms, doms, tots, k_last, st, (k_last + kcol) < (q0 + qrow))
            for j in range(P):
                d_ref[0, pl.ds(q0, T), lanes(j)] = (dqs[j] * scale).astype(d_ref.dtype)
            return carry0

        lax.fori_loop(0, nq, qblock, 0)
        for j in range(P):
            d_ref[1, :, lanes(j)] = dk_acc[j].astype(d_ref.dtype)
            d_ref[2, :, lanes(j)] = dv_acc[j].astype(d_ref.dtype)
        if n:
            pl.when(step_id == n_steps - 1)(scatter.finish)

    def cols(section):
        return pl.BlockSpec((S, LW), lambda b, h: (b, section * nsec + h))

    lspec = pl.BlockSpec((None, G, nq, T), lambda b, h: (b, h, 0, 0))
    dma = pltpu.SemaphoreType.DMA
    return pl.pallas_call(
        body, name=name,
        out_shape=[jax.ShapeDtypeStruct((3, B * S, D), BF16)]
        + [jax.ShapeDtypeStruct((N_CHIPS - 1,) + a.shape[1:], a.dtype) for a in partials],
        grid=(B, nsec),
        in_specs=[cols(0), cols(1), cols(2), lspec, cols(0)] + _any_specs(n),
        out_specs=[pl.BlockSpec((3, S, LW), lambda b, h: (0, b, h))] + _any_specs(n),
        scratch_shapes=[pltpu.VMEM((P, S, 2 * HEAD_DIM), F32), pltpu.VMEM((P, S, 2 * HEAD_DIM), F32)]
        + ([dma((3 * n,)), dma((3 * n,))] if n else []),
        compiler_params=_params(("arbitrary", "arbitrary")),
    )(qkv, qkv, qkv, ltot, do, *partials)


def _cmul(ar, ai, br, bi):
    return ar * br - ai * bi, ar * bi + ai * br


def _cpow(lr, li, n):
    rr, ri = None, None
    br, bi = lr, li
    while n:
        if n & 1:
            rr, ri = (br, bi) if rr is None else _cmul(rr, ri, br, bi)
        n >>= 1
        if n:
            br, bi = _cmul(br, bi, br, bi)
    return rr, ri


def _ssm_scan(sr, si, lr, li, n_steps, reverse):
    W = sr.shape[1]
    R = SEGMENTS
    lim = -li if reverse else li
    zero = jnp.zeros((R, W), F32)

    def row(k):
        i = (n_steps - 1 - k) if reverse else k
        return pl.multiple_of(i * R, R)

    def local(k, st):
        cr, ci = st
        r0 = row(k)
        pr, pi = _cmul(lr, lim, cr, ci)
        nr = pr + sr[pl.ds(r0, R), :]
        ni = pi + si[pl.ds(r0, R), :]
        sr[pl.ds(r0, R), :] = nr
        si[pl.ds(r0, R), :] = ni
        return nr, ni

    er, ei = lax.fori_loop(0, n_steps, local, (zero, zero), unroll=SCAN_UNROLL)
    lnr, lni = _cpow(lr, lim, n_steps)
    rows = lax.broadcasted_iota(jnp.int32, (R, W), 0)
    cr, ci = zero, zero
    for step in range(1, R):
        tr, ti = _cmul(lnr, lni, cr, ci)
        tr, ti = tr + er, ti + ei
        if reverse:
            seg = R - 1 - step
            tr, ti = pltpu.roll(tr, R - 1, 0), pltpu.roll(ti, R - 1, 0)
        else:
            seg = step
            tr, ti = pltpu.roll(tr, 1, 0), pltpu.roll(ti, 1, 0)
        cr = jnp.where(rows == seg, tr, cr)
        ci = jnp.where(rows == seg, ti, ci)

    def fix(k, st):
        pr, pi = st
        r0 = row(k)
        ar, ai = _cmul(pr, pi, cr, ci)
        sr[pl.ds(r0, R), :] += ar
        si[pl.ds(r0, R), :] += ai
        return _cmul(lr, lim, pr, pi)

    lax.fori_loop(0, n_steps, fix, (lr, lim), unroll=SCAN_UNROLL)
    return cr, ci


def _ssm_specs(S, W):
    CH = GROUPS_PER_BLOCK * SSM_GROUP
    return dict(
        rows=pl.BlockSpec((S, CH), lambda b, j: (b, j)),
        b=pl.BlockSpec((None, CH, W), lambda b, j: (j, 0, 0)),
        c=pl.BlockSpec((None, W, CH), lambda b, j: (j, 0, 0)),
        lam=pl.BlockSpec((None, SEGMENTS, W), lambda b, j: (j, 0, 0)),
        vec=pl.BlockSpec((1, CH), lambda b, j: (0, j)),
    )


def ssm_fwd(u, bre, bim, cre, cim, lr8, li8, dsk, *, B, S, name):
    D = u.shape[1]
    J, CH, W = bre.shape
    n_steps = S // SEGMENTS
    sp = _ssm_specs(S, W)

    def body(u_ref, bre_ref, bim_ref, cre_ref, cim_ref, lr_ref, li_ref, dsk_ref, y_ref, sr, si):
        u = u_ref[...]
        ub = u.astype(BF16)
        sr[...] = jnp.dot(ub, bre_ref[...], preferred_element_type=F32)
        si[...] = jnp.dot(ub, bim_ref[...], preferred_element_type=F32)
        _ssm_scan(sr, si, lr_ref[...], li_ref[...], n_steps, False)
        y = jnp.dot(sr[...].astype(BF16), cre_ref[...], preferred_element_type=F32)
        y = y - jnp.dot(si[...].astype(BF16), cim_ref[...], preferred_element_type=F32)
        y_ref[...] = y + dsk_ref[...] * u

    return pl.pallas_call(
        body, name=name, out_shape=jax.ShapeDtypeStruct((B * S, D), F32), grid=(B, J),
        in_specs=[sp["rows"], sp["b"], sp["b"], sp["c"], sp["c"], sp["lam"], sp["lam"], sp["vec"]],
        out_specs=sp["rows"],
        scratch_shapes=[pltpu.VMEM((S, W), F32), pltpu.VMEM((S, W), F32)],
        compiler_params=_params(("parallel", "parallel")),
    )(u, bre, bim, cre, cim, lr8, li8, dsk)


def ssm_bwd(u, dy, bre, bim, cre, cim, lr8, li8, dsk, *, B, S, name):
    D = u.shape[1]
    J, CH, W = bre.shape
    n_steps = S // SEGMENTS
    sp = _ssm_specs(S, W)

    def body(u_ref, dy_ref, bre_ref, bim_ref, cre_ref, cim_ref, lr_ref, li_ref, dsk_ref,
             du_ref, dbre_ref, dbim_ref, dcre_ref, dcim_ref, dlr_ref, dli_ref, ddsk_ref, sr, si, ar, ai):
        u = u_ref[...]
        dy = dy_ref[...]
        ub = u.astype(BF16)
        dyb = dy.astype(BF16)
        lr, li = lr_ref[...], li_ref[...]
        sr[...] = jnp.dot(ub, bre_ref[...], preferred_element_type=F32)
        si[...] = jnp.dot(ub, bim_ref[...], preferred_element_type=F32)
        cr, ci = _ssm_scan(sr, si, lr, li, n_steps, False)
        ar[...] = _nt(dyb, cre_ref[...])
        ai[...] = -_nt(dyb, cim_ref[...])
        _ssm_scan(ar, ai, lr, li, n_steps, True)

        def dlam(k, st):
            dr, di = st
            r0 = pl.multiple_of((k + 1) * SEGMENTS, SEGMENTS)
            p0 = pl.multiple_of(k * SEGMENTS, SEGMENTS)
            pr, pi = sr[pl.ds(p0, SEGMENTS), :], si[pl.ds(p0, SEGMENTS), :]
            xr, xi = ar[pl.ds(r0, SEGMENTS), :], ai[pl.ds(r0, SEGMENTS), :]
            return dr + pr * xr + pi * xi, di + pr * xi - pi * xr

        xr, xi = ar[0:SEGMENTS, :], ai[0:SEGMENTS, :]
        dr, di = lax.fori_loop(0, n_steps - 1, dlam, (cr * xr + ci * xi, cr * xi - ci * xr), unroll=SCAN_UNROLL)
        dlr_ref[...] = dr
        dli_ref[...] = di
        arb = ar[...].astype(BF16)
        aib = ai[...].astype(BF16)
        du_ref[...] = _nt(arb, bre_ref[...]) + _nt(aib, bim_ref[...]) + dsk_ref[...] * dy
        dbre_ref[...] = _tn(ub, arb)
        dbim_ref[...] = _tn(ub, aib)
        dcre_ref[...] = _tn(sr[...].astype(BF16), dyb)
        dcim_ref[...] = -_tn(si[...].astype(BF16), dyb)
        ddsk_ref[...] = jnp.sum(dy * u, axis=0, keepdims=True)

    def per(shape):
        return pl.BlockSpec((None, None) + shape, lambda b, j: (b, j, 0, 0))

    return pl.pallas_call(
        body, name=name,
        out_shape=[jax.ShapeDtypeStruct((B * S, D), F32),
                   jax.ShapeDtypeStruct((B, J, CH, W), F32), jax.ShapeDtypeStruct((B, J, CH, W), F32),
                   jax.ShapeDtypeStruct((B, J, W, CH), F32), jax.ShapeDtypeStruct((B, J, W, CH), F32),
                   jax.ShapeDtypeStruct((B, J, SEGMENTS, W), F32), jax.ShapeDtypeStruct((B, J, SEGMENTS, W), F32),
                   jax.ShapeDtypeStruct((B, J, 1, CH), F32)],
        grid=(B, J),
        in_specs=[sp["rows"], sp["rows"], sp["b"], sp["b"], sp["c"], sp["c"], sp["lam"], sp["lam"], sp["vec"]],
        out_specs=[sp["rows"], per((CH, W)), per((CH, W)), per((W, CH)), per((W, CH)), per((SEGMENTS, W)),
                   per((SEGMENTS, W)),
                   per((1, CH))],
        scratch_shapes=[pltpu.VMEM((S, W), F32)] * 4,
        compiler_params=_params(("parallel", "parallel")),
    )(u, dy, bre, bim, cre, cim, lr8, li8, dsk)


def _ssm_discretize(a_re, a_im, log_dt, b_re, b_im):
    dt = jnp.exp(log_dt)[:, None]
    er = jnp.exp(a_re * dt)
    lr = er * jnp.cos(a_im * dt)
    li = er * jnp.sin(a_im * dt)
    den = a_re * a_re + a_im * a_im
    fr = ((lr - 1.0) * a_re + li * a_im) / den
    fi = (li * a_re - (lr - 1.0) * a_im) / den
    bbr = fr[..., None] * b_re - fi[..., None] * b_im
    bbi = fr[..., None] * b_im + fi[..., None] * b_re
    return lr, li, bbr, bbi


def _block_diag_in(m):
    G, P, H = m.shape
    J = G // GROUPS_PER_BLOCK
    m = m.reshape(J, GROUPS_PER_BLOCK, P, H).transpose(0, 1, 3, 2)
    eye = jnp.eye(GROUPS_PER_BLOCK, dtype=m.dtype)
    out = m[:, :, :, None, :] * eye[None, :, None, :, None]
    return out.reshape(J, GROUPS_PER_BLOCK * H, GROUPS_PER_BLOCK * P)


def _block_diag_in_grad(d, G, P, H):
    J = G // GROUPS_PER_BLOCK
    d = d.reshape(J, GROUPS_PER_BLOCK, H, GROUPS_PER_BLOCK, P)
    idx = jnp.arange(GROUPS_PER_BLOCK)
    d = d[:, idx, :, idx, :]
    return d.transpose(1, 0, 3, 2).reshape(G, P, H)


def _block_diag_out(m):
    G, H, P = m.shape
    J = G // GROUPS_PER_BLOCK
    m = m.reshape(J, GROUPS_PER_BLOCK, H, P).transpose(0, 1, 3, 2)
    eye = jnp.eye(GROUPS_PER_BLOCK, dtype=m.dtype)
    out = m[:, :, :, None, :] * eye[None, :, None, :, None]
    return out.reshape(J, GROUPS_PER_BLOCK * P, GROUPS_PER_BLOCK * H)


def _block_diag_out_grad(d, G, H, P):
    J = G // GROUPS_PER_BLOCK
    d = d.reshape(J, GROUPS_PER_BLOCK, P, GROUPS_PER_BLOCK, H)
    idx = jnp.arange(GROUPS_PER_BLOCK)
    d = d[:, idx, :, idx, :]
    return d.transpose(1, 0, 3, 2).reshape(G, H, P)


def _interleave(a, B, S):
    L = S // SEGMENTS
    return a.reshape(B, SEGMENTS, L, a.shape[-1]).transpose(0, 2, 1, 3).reshape(B * S, a.shape[-1])


def _deinterleave(a, B, S):
    L = S // SEGMENTS
    return a.reshape(B, L, SEGMENTS, a.shape[-1]).transpose(0, 2, 1, 3).reshape(B * S, a.shape[-1])


def _adamw_math(w, g, m, v):
    m = ADAM_B1 * m + (1.0 - ADAM_B1) * g
    v = ADAM_B2 * v + (1.0 - ADAM_B2) * (g * g)
    m_hat = m / (1.0 - ADAM_B1 ** ADAM_STEP)
    v_hat = v / (1.0 - ADAM_B2 ** ADAM_STEP)
    delta = -ADAM_LR * (m_hat / (jnp.sqrt(v_hat) + ADAM_EPS) + ADAM_WD * w)
    return delta, m, v


def adamw(w, g, m, v, *, name):
    R, C = w.shape
    tr = _tile(R, (max(8, (1 << 18) // C // 8 * 8), 256, 128, 64, 32, 16, 8))

    def body(w_ref, g_ref, m_ref, v_ref, d_ref, nm_ref, nv_ref):
        d, nm, nv = _adamw_math(w_ref[...], g_ref[...], m_ref[...], v_ref[...])
        d_ref[...] = d
        nm_ref[...] = nm
        nv_ref[...] = nv

    spec = pl.BlockSpec((tr, C), lambda i: (i, 0))
    shp = jax.ShapeDtypeStruct((R, C), F32)
    return pl.pallas_call(
        body, name=name, out_shape=[shp, shp, shp], grid=(R // tr,), in_specs=[spec] * 4, out_specs=[spec] * 3,
        compiler_params=_params(("parallel",)),
    )(w, g, m, v)


def _any_specs(n):
    return [pl.BlockSpec(memory_space=pl.ANY) for _ in range(n)]


def _coords():
    return lax.axis_index("x"), lax.axis_index("y"), lax.axis_index("c")


def _flip(v, bit):
    return (v + bit) % 2


def all_gather8(a, *, name):
    shape = a.shape

    def body(a_ref, o_ref, send_sems, recv_sems, local_sem):
        x, y, c = _coords()
        me = 4 * x + 2 * y + c
        mine = pltpu.make_async_copy(a_ref, o_ref.at[me], local_sem)
        mine.start()
        sends = []
        for k in range(1, N_DEV):
            peer = (_flip(x, (k >> 2) & 1), _flip(y, (k >> 1) & 1), _flip(c, k & 1))
            cp = pltpu.make_async_remote_copy(a_ref, o_ref.at[me], send_sems.at[k - 1], recv_sems.at[k - 1],
                                              device_id=peer, device_id_type=MESH)
            cp.start()
            sends.append(cp)
        for k in range(1, N_DEV):
            px, py, pc = _flip(x, (k >> 2) & 1), _flip(y, (k >> 1) & 1), _flip(c, k & 1)
            src = 4 * px + 2 * py + pc
            pltpu.make_async_remote_copy(a_ref, o_ref.at[src], send_sems.at[k - 1], recv_sems.at[k - 1],
                                         device_id=(px, py, pc), device_id_type=MESH).wait_recv()
        for cp in sends:
            cp.wait_send()
        mine.wait()

    return pl.pallas_call(
        body, name=name, out_shape=jax.ShapeDtypeStruct((N_DEV,) + shape, a.dtype),
        in_specs=_any_specs(1), out_specs=pl.BlockSpec(memory_space=pl.ANY),
        scratch_shapes=[pltpu.SemaphoreType.DMA((N_DEV - 1,)), pltpu.SemaphoreType.DMA((N_DEV - 1,)),
                        pltpu.SemaphoreType.DMA(())],
    )(a)


def _chip_of(x, y, p):
    px, py = _flip(x, (p >> 1) & 1), _flip(y, p & 1)
    return 2 * px + py, px, py


class _ShardGather:
    def __init__(self, ins, outs, ici_send, ici_recv, d2d_send, d2d_recv):
        self.ins, self.outs = ins, outs
        self.sems = ici_send, ici_recv, d2d_send, d2d_recv
        self.x, self.y, self.c = _coords()
        self.me = 2 * self.x + self.y

    def _ici(self, i, p, slot):
        half = self.ins[i].shape[0] // 2
        rows = pl.ds(self.c * half, half)
        _, px, py = _chip_of(self.x, self.y, p)
        s = i * 3 + p - 1
        return pltpu.make_async_remote_copy(self.ins[i].at[rows], self.outs[i].at[slot, rows], self.sems[0].at[s],
                                            self.sems[1].at[s], device_id=(px, py, self.c), device_id_type=MESH)

    def _d2d(self, i, p, mine):
        half = self.ins[i].shape[0] // 2
        rows = pl.ds((self.c if mine else 1 - self.c) * half, half)
        src, _, _ = _chip_of(self.x, self.y, p)
        s = i * 3 + p - 1
        part = self.outs[i].at[src, rows]
        return pltpu.make_async_remote_copy(part, part, self.sems[2].at[s], self.sems[3].at[s],
                                            device_id=(self.x, self.y, 1 - self.c), device_id_type=MESH)

    def _each(self):
        return [(i, p) for i in range(len(self.ins)) for p in range(1, N_CHIPS)]

    def send(self):
        for i, p in self._each():
            self._ici(i, p, self.me).start()

    def forward(self):
        for i, p in self._each():
            self._ici(i, p, _chip_of(self.x, self.y, p)[0]).wait_recv()
            self._d2d(i, p, True).start()

    def finish(self):
        for i, p in self._each():
            self._d2d(i, p, False).wait_recv()
        for i, p in self._each():
            self._ici(i, p, self.me).wait_send()
            self._d2d(i, p, True).wait_send()


def gather_chip_shards(arrs, remote, *, name):
    n = len(arrs)
    far = [i for i in range(n) if remote[i]]

    def body(*refs):
        ins, outs = refs[:n], refs[n:2 * n]
        ici_send, ici_recv, d2d_send, d2d_recv, local_sems = refs[2 * n:2 * n + 5]
        bufs = refs[2 * n + 5:]
        me = 2 * lax.axis_index("x") + lax.axis_index("y")
        loads = []
        for i in range(n):
            cp = pltpu.make_async_copy(ins[i], bufs[i], local_sems.at[i])
            cp.start()
            loads.append(cp)
        gather = _ShardGather([ins[i] for i in far], [outs[i] for i in far], ici_send, ici_recv, d2d_send, d2d_recv)
        gather.send()
        stores = []
        for i in range(n):
            loads[i].wait()
            cp = pltpu.make_async_copy(bufs[i], outs[i].at[me], local_sems.at[i])
            cp.start()
            stores.append(cp)
        gather.forward()
        gather.finish()
        for cp in stores:
            cp.wait()

    dma = pltpu.SemaphoreType.DMA
    m = 3 * len(far)
    return pl.pallas_call(
        body, name=name,
        out_shape=[jax.ShapeDtypeStruct((N_CHIPS,) + a.shape, a.dtype) for a in arrs],
        in_specs=_any_specs(n), out_specs=_any_specs(n),
        scratch_shapes=[dma((m,)), dma((m,)), dma((m,)), dma((m,)), dma((n,))]
        + [pltpu.VMEM(a.shape, a.dtype) for a in arrs],
        compiler_params=pltpu.CompilerParams(vmem_limit_bytes=V7X_VMEM_LIMIT),
    )(*arrs)


def swap_halves(arrs, *, name):
    n = len(arrs)

    def body(*refs):
        ins, outs = refs[:n], refs[n:2 * n]
        send_sems, recv_sems = refs[2 * n:]
        x, y, c = _coords()
        cps = []
        for i in range(n):
            half = ins[i].shape[1] // 2
            cp = pltpu.make_async_remote_copy(ins[i].at[:, pl.ds((1 - c) * half, half)], outs[i], send_sems.at[i],
                                              recv_sems.at[i], device_id=(x, y, 1 - c), device_id_type=MESH)
            cp.start()
            cps.append(cp)
        for cp in cps:
            cp.wait()

    dma = pltpu.SemaphoreType.DMA
    return pl.pallas_call(
        body, name=name,
        out_shape=[jax.ShapeDtypeStruct((N_CHIPS, a.shape[1] // 2, a.shape[2]), a.dtype) for a in arrs],
        in_specs=_any_specs(n), out_specs=_any_specs(n), scratch_shapes=[dma((n,)), dma((n,))],
    )(*arrs)


def add_half(g, other, c_idx, *, name, out_dtype):
    _, R, C = g.shape
    half = R // 2
    tr = _tile(half, (256, 128, 64, 32, 16, 8))
    nt = half // tr

    def body(c_ref, g_ref, o_ref, out_ref):
        out_ref[...] = (g_ref[...].astype(F32) + o_ref[...].astype(F32)).astype(out_ref.dtype)

    return pl.pallas_call(
        body, name=name, out_shape=jax.ShapeDtypeStruct((N_CHIPS, half, C), out_dtype),
        grid_spec=pltpu.PrefetchScalarGridSpec(
            num_scalar_prefetch=1, grid=(N_CHIPS, nt),
            in_specs=[pl.BlockSpec((None, tr, C), lambda r, t, c_ref: (r, c_ref[0] * nt + t, 0)),
                      pl.BlockSpec((None, tr, C), lambda r, t, c_ref: (r, t, 0))],
            out_specs=pl.BlockSpec((None, tr, C), lambda r, t, c_ref: (r, t, 0))),
        compiler_params=_params(("parallel", "parallel")),
    )(c_idx, g, other)


class _ChipScatter:
    def __init__(self, ins, outs, send_sems, recv_sems):
        self.ins, self.outs, self.send_sems, self.recv_sems = ins, outs, send_sems, recv_sems
        self.x, self.y, self.c = _coords()

    def _copies(self):
        for i in range(len(self.ins)):
            for p in range(1, N_CHIPS):
                dst, px, py = _chip_of(self.x, self.y, p)
                s = i * 3 + p - 1
                yield pltpu.make_async_remote_copy(self.ins[i].at[dst], self.outs[i].at[p - 1], self.send_sems.at[s],
                                                   self.recv_sems.at[s], device_id=(px, py, self.c), device_id_type=MESH)

    def send(self):
        for cp in self._copies():
            cp.start()

    def finish(self):
        for cp in self._copies():
            cp.wait()


def scatter_to_chips(arrs, *, name):
    n = len(arrs)

    def body(*refs):
        scatter = _ChipScatter(refs[:n], refs[n:2 * n], *refs[2 * n:])
        scatter.send()
        scatter.finish()

    dma = pltpu.SemaphoreType.DMA
    return pl.pallas_call(
        body, name=name,
        out_shape=[jax.ShapeDtypeStruct((N_CHIPS - 1,) + a.shape[1:], a.dtype) for a in arrs],
        in_specs=_any_specs(n), out_specs=_any_specs(n), scratch_shapes=[dma((3 * n,)), dma((3 * n,))],
    )(*arrs)


def add_chips(h, got, r_idx, *, name):
    _, R, C = h.shape
    tr = _tile(R, (256, 128, 64, 32, 16, 8))

    def body(r_ref, h_ref, g_ref, out_ref):
        acc = h_ref[...].astype(F32)
        for p in range(N_CHIPS - 1):
            acc = acc + g_ref[p].astype(F32)
        out_ref[...] = acc

    return pl.pallas_call(
        body, name=name, out_shape=jax.ShapeDtypeStruct((R, C), F32),
        grid_spec=pltpu.PrefetchScalarGridSpec(
            num_scalar_prefetch=1, grid=(R // tr,),
            in_specs=[pl.BlockSpec((None, tr, C), lambda t, r_ref: (r_ref[0], t, 0)),
                      pl.BlockSpec((N_CHIPS - 1, tr, C), lambda t, r_ref: (0, t, 0))],
            out_specs=pl.BlockSpec((tr, C), lambda t, r_ref: (t, 0))),
        compiler_params=_params(("parallel",)),
    )(r_idx, h, got)


def join_halves(arrs, *, name):
    n = len(arrs)

    def body(*refs):
        ins, outs = refs[:n], refs[n:2 * n]
        send_sems, recv_sems, local_sems = refs[2 * n:2 * n + 3]
        bufs = refs[2 * n + 3:]
        x, y, c = _coords()
        loads, sends, stores = [], [], []
        for i in range(n):
            cp = pltpu.make_async_copy(ins[i], bufs[i], local_sems.at[i])
            cp.start()
            loads.append(cp)
        for i in range(n):
            half = ins[i].shape[0]
            cp = pltpu.make_async_remote_copy(ins[i], outs[i].at[pl.ds(c * half, half)], send_sems.at[i], recv_sems.at[i],
                                              device_id=(x, y, 1 - c), device_id_type=MESH)
            cp.start()
            sends.append(cp)
        for i in range(n):
            half = ins[i].shape[0]
            loads[i].wait()
            cp = pltpu.make_async_copy(bufs[i], outs[i].at[pl.ds(c * half, half)], local_sems.at[i])
            cp.start()
            stores.append(cp)
        for i in range(n):
            half = ins[i].shape[0]
            pltpu.make_async_remote_copy(ins[i], outs[i].at[pl.ds((1 - c) * half, half)], send_sems.at[i],
                                         recv_sems.at[i], device_id=(x, y, 1 - c), device_id_type=MESH).wait_recv()
        for i in range(n):
            sends[i].wait_send()
            stores[i].wait()

    dma = pltpu.SemaphoreType.DMA
    return pl.pallas_call(
        body, name=name,
        out_shape=[jax.ShapeDtypeStruct((2 * a.shape[0], a.shape[1]), a.dtype) for a in arrs],
        in_specs=_any_specs(n), out_specs=_any_specs(n),
        scratch_shapes=[dma((n,)), dma((n,)), dma((n,))] + [pltpu.VMEM(a.shape, a.dtype) for a in arrs],
        compiler_params=pltpu.CompilerParams(vmem_limit_bytes=V7X_VMEM_LIMIT),
    )(*arrs)


def pair_sums(grads, wire_dtypes, tag):
    c_idx = jnp.reshape(lax.axis_index("c"), (1,)).astype(jnp.int32)
    theirs = swap_halves(grads, name=f"rs_swap_halves_{tag}")
    return [add_half(g, o, c_idx, name=f"rs_add_half_{tag}{i}", out_dtype=wire_dtypes[i])
            for i, (g, o) in enumerate(zip(grads, theirs))]


def chip_sums(pairs, gots, tag):
    r_idx = jnp.reshape(2 * lax.axis_index("x") + lax.axis_index("y"), (1,)).astype(jnp.int32)
    return [add_chips(h, g, r_idx, name=f"rs_add_chips_{tag}{i}") for i, (h, g) in enumerate(zip(pairs, gots))]


def _chip_major(w, axis):
    n = w.shape[axis] // N_CHIPS
    parts = w.reshape(w.shape[:axis] + (N_CHIPS, n) + w.shape[axis + 1:])
    return jnp.moveaxis(parts, axis, 0)


def _from_chip_major(g, axis):
    g = jnp.moveaxis(g, 0, axis)
    return g.reshape(g.shape[:axis] + (g.shape[axis] * g.shape[axis + 1],) + g.shape[axis + 2:])


def kernel(x, c, norm_mix, norm_ffn, w_mod, b_mod, w_qkv, w_o_attn, w_in_ssm, a_re, a_im, log_dt, b_re, b_im, c_re, c_im, d_skip, w_glu, b_glu, w_o_ssm, w_up, conv_w, conv_b, w_down, norm_out, w_fin, b_fin, loss_target, m_norm_mix, m_norm_ffn, m_w_mod, m_b_mod, m_w_qkv, m_w_o_attn, m_w_in_ssm, m_a_re, m_a_im, m_log_dt, m_b_re, m_b_im, m_c_re, m_c_im, m_d_skip, m_w_glu, m_b_glu, m_w_o_ssm, m_w_up, m_conv_w, m_conv_b, m_w_down, m_norm_out, m_w_fin, m_b_fin, v_norm_mix, v_norm_ffn, v_w_mod, v_b_mod, v_w_qkv, v_w_o_attn, v_w_in_ssm, v_a_re, v_a_im, v_log_dt, v_b_re, v_b_im, v_c_re, v_c_im, v_d_skip, v_w_glu, v_b_glu, v_w_o_ssm, v_w_up, v_conv_w, v_conv_b, v_w_down, v_norm_out, v_w_fin, v_b_fin):
    B, S, D = x.shape
    T = B * S
    F2 = conv_b.shape[1]
    F = F2 // 2
    G, P = a_re.shape[1], a_re.shape[2]
    H = b_re.shape[3]
    mx, my, mc = _coords()
    chip = 2 * mx + my
    dev = 4 * mx + 2 * my + mc
    BG = N_DEV * B
    mod_w = w_mod.shape[2]
    fin_w = w_fin.shape[1]

    c_all = all_gather8(c, name="gather_c").reshape(BG, D)
    c_act = silu_rows(c_all, name="silu_c")
    b_mod_mine = lax.dynamic_slice(b_mod, (0, chip * mod_w), (2, mod_w))
    b_fin_mine = lax.dynamic_slice(b_fin, (chip * fin_w,), (fin_w,))
    cond = [matmul(c_act, w_mod[i], bias=b_mod_mine[i], name=f"mod_proj_{i}") for i in range(2)]
    cond.append(matmul(c_act, w_fin, bias=b_fin_mine, name="fin_proj"))
    cond_all = all_gather8(jnp.concatenate(cond, axis=1), name="gather_cond")
    cond_all = cond_all[::2]
    cond_rows = lax.dynamic_slice(cond_all, (0, dev * B, 0), (N_CHIPS, B, cond_all.shape[2]))
    mods = []
    for i in range(2):
        full = cond_rows[:, :, i * mod_w:(i + 1) * mod_w].transpose(1, 0, 2).reshape(B, N_CHIPS * mod_w)
        mods.append([full[:, k * D:(k + 1) * D] for k in range(6)])
    fin = cond_rows[:, :, 2 * mod_w:].transpose(1, 0, 2).reshape(B, N_CHIPS * fin_w)
    sh_f, sc_f = fin[:, :D], fin[:, D:]

    rows1024 = jnp.concatenate([w_o_attn[0], w_in_ssm[0], w_glu[0], w_o_ssm[0], w_down.reshape(-1, D)], axis=0)
    shards = [w_qkv[0].astype(BF16), rows1024.astype(BF16), w_up[0].astype(BF16), w_up[1].astype(BF16)]
    W_qkv, *own_slots = gather_chip_shards(shards, [True, False, False, False], name="gather_weights")
    Dq = D // N_CHIPS
    Fq = F // N_CHIPS
    small =jnp.concatenate([conv_w.reshape(6, -1), jnp.pad(d_skip, ((0, 0), (0, conv_w.shape[2] - Dq))),
                             jnp.pad(b_glu, ((0, 0), (0, conv_w.shape[2] - Dq)))], axis=0)
    small_all = all_gather8(small, name="gather_small")[::2]
    conv_w_full = _from_chip_major(small_all[:, :6].reshape(N_CHIPS, 2, 3, -1), 2)
    d_skip_full = small_all[:, 6, :Dq].reshape(1, D)
    b_glu_full = small_all[:, 7, :Dq].reshape(D)

    x0 = x.reshape(T, D)
    tgt = loss_target.reshape(T, D)

    def ffn_fwd(xprev, y, gate, i):
        sh2, sc2 = mods[i][3], mods[i][4]
        xin, h2 = res_norm_mod_fwd(xprev, y, gate, norm_ffn[i], sh2, sc2, B=B, S=S, name=f"ffn_norm_{i}")
        up = matmul(h2, W_up[i], b_chips=True, out_dtype=BF16, name=f"ffn_up_{i}")
        act = conv_gate_fwd(up, conv_w_full[i], conv_b[i:i + 1], B=B, S=S, name=f"ffn_conv_{i}")
        yf = matmul(act, W_down[i], name=f"ffn_down_{i}")
        return xin, yf, (xin, h2, up, act, yf)

    sh1, sc1, g1 = mods[0][0], mods[0][1], mods[0][2]
    h1a = norm_mod_fwd(x0, norm_mix[0], sh1, sc1, B=B, S=S, name="att_norm")
    qkv = matmul(h1a, W_qkv, out_dtype=BF16, b_chips=True, name="att_qkv")
    o2, ltot, g_rows, W_up0, W_up1 = attn_fwd_pairs(qkv, shards[1:], own_slots, B=B, S=S, name="att_fwd")
    W_up = [W_up0, W_up1]
    W_o_attn = g_rows[:, 0 * Dq:1 * Dq].reshape(D, D)
    W_in = g_rows[:, 1 * Dq:2 * Dq].reshape(D, D)
    W_glu = g_rows[:, 2 * Dq:3 * Dq].reshape(D, D)
    W_o_ssm = g_rows[:, 3 * Dq:4 * Dq].reshape(D, D)
    W_down = [g_rows[:, 4 * Dq + i * Fq:4 * Dq + (i + 1) * Fq].reshape(F, D) for i in range(2)]
    ya = matmul(o2, W_o_attn, name="att_out")
    x1, yf0, ffn0 = ffn_fwd(x0, ya, g1, 0)

    lr, li, bbr, bbi = _ssm_discretize(a_re[0], a_im[0], log_dt[0], b_re[0], b_im[0])
    J = G // GROUPS_PER_BLOCK
    Wst = GROUPS_PER_BLOCK * P
    bre_blk = _block_diag_in(bbr).astype(BF16)
    bim_blk = _block_diag_in(bbi).astype(BF16)
    cre_blk = _block_diag_out(c_re[0]).astype(BF16)
    cim_blk = _block_diag_out(c_im[0]).astype(BF16)
    lr8 = jnp.broadcast_to(lr.reshape(J, 1, Wst), (J, SEGMENTS, Wst))
    li8 = jnp.broadcast_to(li.reshape(J, 1, Wst), (J, SEGMENTS, Wst))
    sh1s, sc1s, g1s = mods[1][0], mods[1][1], mods[1][2]
    x2, h1s = res_norm_mod_fwd(x1, yf0, mods[0][5], norm_mix[1], sh1s, sc1s, B=B, S=S, name="ssm_norm")
    h1p = _interleave(h1s, B, S)
    u = matmul(h1p, W_in, name="ssm_in")
    y_ssm = ssm_fwd(u, bre_blk, bim_blk, cre_blk, cim_blk, lr8, li8, d_skip_full, B=B, S=S, name="ssm_scan_fwd")
    zb = gelu_fwd(y_ssm, B=B, S=S, name="ssm_gelu")
    s_glu = matmul(zb, W_glu, bias=b_glu_full, name="ssm_glu_proj")
    gb = glu_fwd(y_ssm, s_glu, B=B, S=S, name="ssm_glu")
    ys_p = matmul(gb, W_o_ssm, name="ssm_out")
    ys = _deinterleave(ys_p, B, S)
    x3, yf1, ffn1 = ffn_fwd(x2, ys, g1s, 1)
    x4 = gate_res_fwd(x3, yf1, mods[1][5], B=B, S=S, name="ffn_res_1")

    dx4, dyf1, loss_p, dsh_f, dsc_f, dnorm_out, dg2_1 = final_loss(x4, tgt, norm_out, sh_f, sc_f, yf1, mods[1][5],
                                                                   B=B, S=S, name="loss_head")
    loss = lax.psum(jnp.sum(loss_p), ("x", "y", "c"))

    def ffn_bwd(dxo, dyf, i, saved, y_prev, gate_prev):
        xin, h2, up, act, yf = saved
        sc2 = mods[i][4]
        dact = matmul(dyf, W_down[i], tb=True, out_dtype=BF16, name=f"ffn_down_dx_{i}")
        dW_down = matmul(act, dyf, ta=True, out_dtype=BF16, name=f"ffn_down_dw_{i}")
        dup, dcw, dcb = conv_gate_bwd(up, dact, conv_w_full[i], conv_b[i:i + 1], B=B, S=S, name=f"ffn_conv_bwd_{i}")
        dh2 = matmul(dup, W_up[i], tb=True, b_chips=True, name=f"ffn_up_dx_{i}")
        dW_up = matmul(h2, dup, ta=True, b_chips=True, out_chips=True, out_dtype=BF16, name=f"ffn_up_dw_{i}")
        dxin, dy_prev, dsh2, dsc2, dnf, dgate_prev = norm_mod_bwd_gate(
            dh2, xin, dxo, norm_ffn[i], sc2, y_prev, gate_prev, B=B, S=S, name=f"ffn_norm_bwd_{i}")
        dconv_w = jnp.sum(dcw, axis=0).transpose(1, 0, 2).reshape(3, F2)
        return dxin, dy_prev, dgate_prev, dict(dW_down=dW_down, dW_up=dW_up, dconv_b=jnp.sum(dcb, axis=0).reshape(F2),
                                               dconv_w=dconv_w, dnorm_ffn=jnp.sum(dnf, axis=0), dsh2=dsh2, dsc2=dsc2)

    dx3, dys, dg1s, gf1 = ffn_bwd(dx4, dyf1, 1, ffn1, ys, g1s)
    gf1["dg2"] = dg2_1

    dys_p = _interleave(dys, B, S)
    dgb = matmul(dys_p, W_o_ssm, tb=True, name="ssm_out_dx")
    dW_o_ssm = matmul(gb, dys_p, ta=True, out_dtype=BF16, name="ssm_out_dw")
    ds_glu, dz1, db_glu = glu_bwd1(y_ssm, s_glu, dgb, B=B, S=S, name="ssm_glu_bwd1")
    dz2 = matmul(ds_glu, W_glu, tb=True, name="ssm_glu_dx")
    dW_glu = matmul(zb, ds_glu, ta=True, out_dtype=BF16, name="ssm_glu_dw")
    dy_ssm = glu_bwd2(y_ssm, dz1, dz2, B=B, S=S, name="ssm_glu_bwd2")
    du, dbre, dbim, dcre, dcim, dlr8, dli8, ddsk = ssm_bwd(u, dy_ssm, bre_blk, bim_blk, cre_blk, cim_blk, lr8, li8,
                                                           d_skip_full, B=B, S=S, name="ssm_scan_bwd")
    dub = du.astype(BF16)
    dh1p = matmul(dub, W_in, tb=True, name="ssm_in_dx")
    dW_in = matmul(h1p, dub, ta=True, out_dtype=BF16, name="ssm_in_dw")
    dx2, dyf0, dsh1s, dsc1s, dnm1, dg2_0 = norm_mod_bwd_gate(_deinterleave(dh1p, B, S), x2, dx3, norm_mix[1], sc1s,
                                                             yf0, mods[0][5], B=B, S=S, name="ssm_norm_bwd")
    dlr = jnp.sum(dlr8, axis=(0, 2)).reshape(G, P)
    dli = jnp.sum(dli8, axis=(0, 2)).reshape(G, P)
    dbbr = _block_diag_in_grad(jnp.sum(dbre, axis=0), G, P, H)
    dbbi = _block_diag_in_grad(jnp.sum(dbim, axis=0), G, P, H)
    dc_re = _block_diag_out_grad(jnp.sum(dcre, axis=0), G, H, P)
    dc_im = _block_diag_out_grad(jnp.sum(dcim, axis=0), G, H, P)
    dd_skip = jnp.sum(ddsk, axis=0).reshape(D)

    dx1, dya, dg1, gf0 = ffn_bwd(dx2, dyf0, 0, ffn0, ya, g1)
    gf0["dg2"] = dg2_0

    do2 = matmul(dya, W_o_attn, tb=True, out_dtype=BF16, name="att_out_dx")
    dW_o_attn = matmul(o2, dya, ta=True, out_dtype=BF16, name="att_out_dw")
    g_rows_cm = jnp.concatenate([dW_o_attn.reshape(N_CHIPS, Dq, D), dW_in.reshape(N_CHIPS, Dq, D),
                                 dW_glu.reshape(N_CHIPS, Dq, D), dW_o_ssm.reshape(N_CHIPS, Dq, D),
                                 gf0["dW_down"].reshape(N_CHIPS, Fq, D), gf1["dW_down"].reshape(N_CHIPS, Fq, D)], axis=1)
    pairs_a = pair_sums([g_rows_cm, gf0["dW_up"], gf1["dW_up"]], [BF16, BF16, BF16], "a")
    dqkv, *gots_a = attn_bwd_pairs(qkv, ltot, do2, pairs_a, B=B, S=S, name="att_bwd")
    dh1a = matmul(dqkv, _from_chip_major(W_qkv, 1), tb=True, name="att_qkv_dx")
    dW_qkv = _chip_major(matmul(h1a, dqkv, ta=True, b_chips=True, out_dtype=BF16, name="att_qkv_dw"), 1)
    grad_x, dsh1, dsc1, dnm0 = norm_mod_bwd(dh1a, x0, dx1, norm_mix[0], sc1, B=B, S=S, name="att_norm_bwd")

    dmod_rows = jnp.concatenate([dsh1, dsc1, dg1, gf0["dsh2"], gf0["dsc2"], gf0["dg2"],
                                 dsh1s, dsc1s, dg1s, gf1["dsh2"], gf1["dsc2"], gf1["dg2"], dsh_f, dsc_f], axis=1)
    dmod_all = all_gather8(dmod_rows, name="gather_dmod").reshape(BG, 14 * D)
    grad_w_mod = jnp.stack([
        matmul(c_act, lax.dynamic_slice(dmod_all, (0, i * 6 * D + chip * mod_w), (BG, mod_w)), ta=True,
               name=f"mod_dw_{i}") for i in range(2)])
    grad_w_fin = matmul(c_act, lax.dynamic_slice(dmod_all, (0, 12 * D + chip * fin_w), (BG, fin_w)), ta=True,
                        name="fin_dw")

    parts = [jnp.concatenate([jnp.sum(dnm0, axis=0), jnp.sum(dnm1, axis=0)]),
             jnp.concatenate([gf0["dnorm_ffn"], gf1["dnorm_ffn"]]),
             jnp.sum(dmod_rows[:, :12 * D], axis=0),
             dlr.reshape(-1), dli.reshape(-1), dbbr.reshape(-1), dbbi.reshape(-1), dc_re.reshape(-1), dc_im.reshape(-1),
             dd_skip, jnp.sum(db_glu, axis=0),
             jnp.sum(dnorm_out, axis=0), jnp.sum(dmod_rows[:, 12 * D:], axis=0),
             gf0["dconv_w"].reshape(-1), gf1["dconv_w"].reshape(-1), gf0["dconv_b"], gf1["dconv_b"]]
    sizes = [int(p.shape[0]) for p in parts]
    flat = jnp.concatenate(parts)
    width = 1024
    quantum = N_CHIPS * 16 * width
    padded = -(-flat.shape[0] // quantum) * quantum
    small_cm = jnp.pad(flat, (0, padded - flat.shape[0])).reshape(N_CHIPS, -1, width)

    pairs_b = pair_sums([dW_qkv, small_cm], [BF16, F32], "b")
    gots_b = scatter_to_chips(pairs_b, name="rs_scatter_to_chips")
    r_qkv, r_small, r_rows, r_up0, r_up1 = join_halves(
        chip_sums(pairs_b, gots_b, "b") + chip_sums(pairs_a, gots_a, "a"), name="rs_join_halves")
    grad_w_qkv = r_qkv[None]
    grad_w_o_attn = r_rows[0 * Dq:1 * Dq][None]
    grad_w_in_ssm = r_rows[1 * Dq:2 * Dq][None]
    grad_w_glu = r_rows[2 * Dq:3 * Dq][None]
    grad_w_o_ssm = r_rows[3 * Dq:4 * Dq][None]
    grad_w_down = r_rows[4 * Dq:].reshape(2, Fq, D)
    grad_w_up = jnp.stack([r_up0, r_up1])
    summed = all_gather8(r_small, name="gather_small_grads")[::2].reshape(-1)
    offs = [0]
    for s_ in sizes:
        offs.append(offs[-1] + s_)
    (s_nm, s_nf, s_bmod, s_lr, s_li, s_bbr, s_bbi, s_cre, s_cim, s_dsk, s_bglu, s_no, s_bfin, s_cw0, s_cw1, s_cb0,
     s_cb1) = [summed[offs[i]:offs[i + 1]] for i in range(len(sizes))]
    _, disc_vjp = jax.vjp(_ssm_discretize, a_re[0], a_im[0], log_dt[0], b_re[0], b_im[0])
    ga_re, ga_im, glog_dt, gb_re, gb_im = disc_vjp((s_lr.reshape(G, P), s_li.reshape(G, P), s_bbr.reshape(G, P, H),
                                                    s_bbi.reshape(G, P, H)))
    grad_norm_mix = s_nm.reshape(2, D)
    grad_norm_ffn = s_nf.reshape(2, D)
    grad_b_mod = s_bmod.reshape(2, 6 * D)
    grad_c_re = s_cre.reshape(1, G, H, P)
    grad_c_im = s_cim.reshape(1, G, H, P)
    grad_d_skip = lax.dynamic_slice(s_dsk, (chip * Dq,), (Dq,)).reshape(1, Dq)
    grad_b_glu = lax.dynamic_slice(s_bglu, (chip * Dq,), (Dq,)).reshape(1, Dq)
    cw_full = jnp.stack([s_cw0.reshape(3, F2), s_cw1.reshape(3, F2)])
    grad_conv_w = lax.dynamic_slice(cw_full, (0, 0, chip * (F2 // N_CHIPS)), (2, 3, F2 // N_CHIPS))
    grad_conv_b = jnp.stack([s_cb0, s_cb1])
    grad_norm_out = s_no
    grad_b_fin = s_bfin

    grads = dict(
        norm_mix=grad_norm_mix, norm_ffn=grad_norm_ffn, w_mod=grad_w_mod, b_mod=grad_b_mod, w_qkv=grad_w_qkv,
        w_o_attn=grad_w_o_attn, w_in_ssm=grad_w_in_ssm, a_re=ga_re[None], a_im=ga_im[None], log_dt=glog_dt[None],
        b_re=gb_re[None], b_im=gb_im[None], c_re=grad_c_re, c_im=grad_c_im, d_skip=grad_d_skip, w_glu=grad_w_glu,
        b_glu=grad_b_glu, w_o_ssm=grad_w_o_ssm, w_up=grad_w_up, conv_w=grad_conv_w, conv_b=grad_conv_b,
        w_down=grad_w_down, norm_out=grad_norm_out, w_fin=grad_w_fin, b_fin=grad_b_fin)
    weights = dict(
        norm_mix=norm_mix, norm_ffn=norm_ffn, w_mod=w_mod, b_mod=b_mod, w_qkv=w_qkv, w_o_attn=w_o_attn,
        w_in_ssm=w_in_ssm, a_re=a_re, a_im=a_im, log_dt=log_dt, b_re=b_re, b_im=b_im, c_re=c_re, c_im=c_im,
        d_skip=d_skip, w_glu=w_glu, b_glu=b_glu, w_o_ssm=w_o_ssm, w_up=w_up, conv_w=conv_w, conv_b=conv_b,
        w_down=w_down, norm_out=norm_out, w_fin=w_fin, b_fin=b_fin)
    m_in = dict(
        norm_mix=m_norm_mix, norm_ffn=m_norm_ffn, w_mod=m_w_mod, b_mod=m_b_mod, w_qkv=m_w_qkv, w_o_attn=m_w_o_attn,
        w_in_ssm=m_w_in_ssm, a_re=m_a_re, a_im=m_a_im, log_dt=m_log_dt, b_re=m_b_re, b_im=m_b_im, c_re=m_c_re,
        c_im=m_c_im, d_skip=m_d_skip, w_glu=m_w_glu, b_glu=m_b_glu, w_o_ssm=m_w_o_ssm, w_up=m_w_up, conv_w=m_conv_w,
        conv_b=m_conv_b, w_down=m_w_down, norm_out=m_norm_out, w_fin=m_w_fin, b_fin=m_b_fin)
    v_in = dict(
        norm_mix=v_norm_mix, norm_ffn=v_norm_ffn, w_mod=v_w_mod, b_mod=v_b_mod, w_qkv=v_w_qkv, w_o_attn=v_w_o_attn,
        w_in_ssm=v_w_in_ssm, a_re=v_a_re, a_im=v_a_im, log_dt=v_log_dt, b_re=v_b_re, b_im=v_b_im, c_re=v_c_re,
        c_im=v_c_im, d_skip=v_d_skip, w_glu=v_w_glu, b_glu=v_b_glu, w_o_ssm=v_w_o_ssm, w_up=v_w_up, conv_w=v_conv_w,
        conv_b=v_conv_b, w_down=v_w_down, norm_out=v_norm_out, w_fin=v_w_fin, b_fin=v_b_fin)
    names = list(weights)
    for n_ in names:
        grads[n_] = grads[n_].reshape(weights[n_].shape)

    big = ("w_mod", "w_qkv", "w_o_attn", "w_in_ssm", "w_glu", "w_o_ssm", "w_up", "w_down", "w_fin")
    delta, new_m, new_v = {}, {}, {}
    for n_ in big:
        shp = weights[n_].shape
        two_d = lambda a: a.reshape(-1, shp[-1])
        d_, m_, v_ = adamw(two_d(weights[n_]), two_d(grads[n_]), two_d(m_in[n_]), two_d(v_in[n_]), name=f"adamw_{n_}")
        delta[n_], new_m[n_], new_v[n_] = d_.reshape(shp), m_.reshape(shp), v_.reshape(shp)
    rest = sorted([n_ for n_ in names if n_ not in big], key=lambda n_: math.prod(weights[n_].shape) % 1024 != 0)

    def pack(tree):
        f = jnp.concatenate([tree[n_].reshape(-1) for n_ in rest])
        pad_to = -(-f.shape[0] // (8 * width)) * (8 * width)
        return jnp.pad(f, (0, pad_to - f.shape[0]), constant_values=1.0).reshape(-1, width)

    d_, m_, v_ = adamw(pack(weights), pack(grads), pack(m_in), pack(v_in), name="adamw_small")
    off = 0
    for n_ in rest:
        sz = int(math.prod(weights[n_].shape))
        shp = weights[n_].shape
        delta[n_] = d_.reshape(-1)[off:off + sz].reshape(shp)
        new_m[n_] = m_.reshape(-1)[off:off + sz].reshape(shp)
        new_v[n_] = v_.reshape(-1)[off:off + sz].reshape(shp)
        off += sz

    return (loss, grad_x.reshape(B, S, D), *[grads[n_] for n_ in names], *[delta[n_] for n_ in names],
            *[new_m[n_] for n_ in names], *[new_v[n_] for n_ in names])
```

```python
import math

import jax
import jax.numpy as jnp
from jax import lax
from jax.experimental import pallas as pl
from jax.experimental.pallas import tpu as pltpu

F32 = jnp.float32
BF16 = jnp.bfloat16
MESH = pl.DeviceIdType.MESH

HEAD_DIM = 64
SSM_GROUP = 16
STATE = 64
GROUPS_PER_BLOCK = 8
SEGMENTS = 16
SCAN_UNROLL = 4
EPS = 1e-6
ADAM_LR = 0.001
ADAM_B1 = 0.9
ADAM_B2 = 0.999
ADAM_EPS = 1e-08
ADAM_WD = 0.01
ADAM_STEP = 10
N_CHIPS = 4
N_DEV = 8
V7X_VMEM_LIMIT = 56 * 1024 * 1024
ATT_BLOCK = 128
ATT_HEADS = 8
ATT_HEADS_BWD = 8


def _tile(n, prefs):
    for p in prefs:
        if n % p == 0:
            return p
    return n


def _params(sem, vmem=V7X_VMEM_LIMIT):
    return pltpu.CompilerParams(dimension_semantics=sem, vmem_limit_bytes=vmem)


def matmul(a, b, *, ta=False, tb=False, bias=None, out_dtype=F32, b_chips=False, out_chips=False, name):
    a_parts = a.shape[0] if a.ndim == 3 else 1
    if a_parts > 1:
        assert not ta
        M, K = a.shape[1], a_parts * a.shape[2]
    elif ta:
        K, M = a.shape
    else:
        M, K = a.shape
    b_parts = b.shape[0] if b_chips else 1
    b_rows, b_cols = (b.shape[1], b_parts * b.shape[2]) if b_chips else b.shape
    if tb:
        N, Kb = b_rows, b_cols
    else:
        Kb, N = b_rows, b_cols
    assert K == Kb, (a.shape, b.shape, ta, tb)
    n_cut = math.gcd(N // (N_CHIPS if out_chips else 1), N // (b_parts if not tb else 1))
    k_cut = math.gcd(K // (b_parts if tb else 1), K // a_parts)
    tm = _tile(M, (1024, 1408, 512, 256, 128))
    tn = _tile(n_cut, (1024, 1408, 768, 512, 256, 128))
    tk = k_cut if k_cut <= 2816 else _tile(k_cut, (1024, 512, 256, 128))
    nk = K // tk
    npc = N // N_CHIPS // tn
    npb = N // b_parts // tn
    kpb = K // b_parts // tk
    kpa = K // a_parts // tk
    dims = (((0,) if ta else (1,), (1,) if tb else (0,)), ((), ()))

    def body(*refs):
        a_ref, b_ref = refs[:2]
        bias_ref = refs[2] if bias is not None else None
        o_ref = refs[-2] if nk > 1 else refs[-1]

        def finish(r):
            if bias_ref is not None:
                r = r + bias_ref[...]
            o_ref[...] = r.astype(o_ref.dtype)

        prod = lax.dot_general(a_ref[...].astype(BF16), b_ref[...].astype(BF16), dims, preferred_element_type=F32)
        if nk == 1:
            finish(prod)
            return
        acc_ref = refs[-1]
        k = pl.program_id(2)

        @pl.when(k == 0)
        def _():
            acc_ref[...] = prod

        @pl.when(k > 0)
        def _():
            acc_ref[...] += prod

        @pl.when(k == nk - 1)
        def _():
            finish(acc_ref[...])

    if a_parts > 1:
        a_spec = pl.BlockSpec((None, tm, tk), lambda i, j, k: (lax.div(k, kpa), i, lax.rem(k, kpa)))
    else:
        a_spec = pl.BlockSpec((tk, tm), lambda i, j, k: (k, i)) if ta else pl.BlockSpec((tm, tk), lambda i, j, k: (i, k))
    if not b_chips:
        b_spec = pl.BlockSpec((tn, tk), lambda i, j, k: (j, k)) if tb else pl.BlockSpec((tk, tn), lambda i, j, k: (k, j))
    elif tb:
        b_spec = pl.BlockSpec((None, tn, tk), lambda i, j, k: (lax.div(k, kpb), j, lax.rem(k, kpb)))
    else:
        b_spec = pl.BlockSpec((None, tk, tn), lambda i, j, k: (lax.div(j, npb), k, lax.rem(j, npb)))
    in_specs = [a_spec, b_spec]
    args = [a, b]
    if bias is not None:
        in_specs.append(pl.BlockSpec((1, tn), lambda i, j, k: (0, j)))
        args.append(bias.reshape(1, N).astype(F32))
    if out_chips:
        out_shape = jax.ShapeDtypeStruct((N_CHIPS, M, N // N_CHIPS), out_dtype)
        out_spec = pl.BlockSpec((None, tm, tn), lambda i, j, k: (lax.div(j, npc), i, lax.rem(j, npc)))
    else:
        out_shape = jax.ShapeDtypeStruct((M, N), out_dtype)
        out_spec = pl.BlockSpec((tm, tn), lambda i, j, k: (i, j))
    return pl.pallas_call(
        body, name=name,
        out_shape=out_shape,
        grid=(M // tm, N // tn, nk),
        in_specs=in_specs,
        out_specs=out_spec,
        scratch_shapes=[pltpu.VMEM((tm, tn), F32)] if nk > 1 else [],
        compiler_params=_params(("parallel", "parallel", "arbitrary")),
    )(*args)


def rowwise(fn, tiled, per_seq, glob, out_tiled, out_seq, *, B, S, name, rows=512):
    tm = _tile(S, (rows, 128, 64, 32, 16, 8))
    nt = S // tm
    n_in = len(tiled) + len(per_seq) + len(glob)
    n_ot = len(out_tiled)

    def body(*refs):
        ins = refs[:n_in]
        outs = refs[n_in:]
        vals = fn(*[r[...] for r in ins])
        if not isinstance(vals, (tuple, list)):
            vals = (vals,)
        assert len(vals) == len(outs), (name, len(vals), len(outs))
        for o_ref, v in zip(outs[:n_ot], vals[:n_ot]):
            o_ref[...] = v.astype(o_ref.dtype)
        t = pl.program_id(1)
        for o_ref, v in zip(outs[n_ot:], vals[n_ot:]):
            def first(o_ref=o_ref, v=v):
                o_ref[...] = v.astype(F32)

            def later(o_ref=o_ref, v=v):
                o_ref[...] += v.astype(F32)

            pl.when(t == 0)(first)
            pl.when(t > 0)(later)

    in_specs = [pl.BlockSpec((tm, a.shape[1]), lambda b, t: (b * nt + t, 0)) for a in tiled]
    in_specs += [pl.BlockSpec((None, 1, a.shape[1]), lambda b, t: (b, 0, 0)) for a in per_seq]
    in_specs += [pl.BlockSpec(a.shape, lambda b, t: (0,) * a.ndim) for a in glob]
    out_shape = [jax.ShapeDtypeStruct((B * S, w), dt) for w, dt in out_tiled]
    out_shape += [jax.ShapeDtypeStruct((B, 1, w), F32) for w in out_seq]
    out_specs = [pl.BlockSpec((tm, w), lambda b, t: (b * nt + t, 0)) for w, _ in out_tiled]
    out_specs += [pl.BlockSpec((None, 1, w), lambda b, t: (b, 0, 0)) for w in out_seq]
    res = pl.pallas_call(
        body, name=name, out_shape=out_shape, grid=(B, nt), in_specs=in_specs, out_specs=out_specs,
        compiler_params=_params(("parallel", "arbitrary")),
    )(*tiled, *[a.reshape(B, 1, a.shape[1]) for a in per_seq], *glob)
    res = list(res)
    for i in range(n_ot, len(res)):
        res[i] = res[i].reshape(B, res[i].shape[-1])
    return res


def _rms(x):
    r = lax.rsqrt(jnp.mean(x * x, axis=-1, keepdims=True) + EPS)
    return x * r, r


def norm_mod_fwd(x, g, sh, sc, *, B, S, name):
    def fn(x, sh, sc, g):
        xn, _ = _rms(x)
        return (xn * g) * (1.0 + sc) + sh

    return rowwise(fn, [x], [sh, sc], [g.reshape(1, -1)], [(x.shape[1], BF16)], [], B=B, S=S, name=name)[0]


def _norm_mod_bwd_math(dh, x, sc, g):
    xn, r = _rms(x)
    y = xn * g
    dy = dh * (1.0 + sc)
    dxn = dy * g
    dx = r * (dxn - xn * jnp.mean(dxn * xn, axis=-1, keepdims=True))
    dsh = jnp.sum(dh, axis=0, keepdims=True)
    dsc = jnp.sum(dh * y, axis=0, keepdims=True)
    dg = jnp.sum(dy * xn, axis=0, keepdims=True)
    return dx, dsh, dsc, dg


def norm_mod_bwd(dh, x, dres, g, sc, *, B, S, name):
    D = x.shape[1]

    def fn(dh, x, dres, sc, g):
        dx, dsh, dsc, dg = _norm_mod_bwd_math(dh.astype(F32), x, sc, g)
        return dres + dx, dsh, dsc, dg

    return rowwise(fn, [dh, x, dres], [sc], [g.reshape(1, -1)], [(D, F32)], [D, D, D], B=B, S=S, name=name)


def norm_mod_bwd_gate(dh, x, dres, g, sc, y_prev, gate_prev, *, B, S, name):
    D = x.shape[1]

    def fn(dh, x, dres, y, sc, gate, g):
        dx, dsh, dsc, dg = _norm_mod_bwd_math(dh.astype(F32), x, sc, g)
        dx = dres + dx
        return dx, gate * dx, dsh, dsc, dg, jnp.sum(dx * y, axis=0, keepdims=True)

    return rowwise(fn, [dh, x, dres, y_prev], [sc, gate_prev], [g.reshape(1, -1)], [(D, F32), (D, BF16)],
                   [D, D, D, D], B=B, S=S, name=name)


def gate_res_fwd(x, y, gate, *, B, S, name):
    return rowwise(lambda x, y, g: x + g * y, [x, y], [gate], [], [(x.shape[1], F32)], [], B=B, S=S, name=name)[0]


def res_norm_mod_fwd(x, y, gate, g, sh, sc, *, B, S, name):
    D = x.shape[1]

    def fn(x, y, gate, sh, sc, g):
        x = x + gate * y
        xn, _ = _rms(x)
        return x, (xn * g) * (1.0 + sc) + sh

    return rowwise(fn, [x, y], [gate, sh, sc], [g.reshape(1, -1)], [(D, F32), (D, BF16)], [], B=B, S=S, name=name)


def final_loss(x, tgt, g, sh, sc, y_prev, gate_prev, *, B, S, name):
    D = x.shape[1]

    def fn(x, tgt, y_prev, sh, sc, gate, g):
        xn, _ = _rms(x)
        y = (xn * g) * (1.0 + sc) + sh
        err = y - tgt
        loss = 0.5 * jnp.sum(err * err, axis=0, keepdims=True) * (1.0 / D)
        dx, dsh, dsc, dg = _norm_mod_bwd_math(err * (1.0 / D), x, sc, g)
        return dx, gate * dx, loss, dsh, dsc, dg, jnp.sum(dx * y_prev, axis=0, keepdims=True)

    return rowwise(fn, [x, tgt, y_prev], [sh, sc, gate_prev], [g.reshape(1, -1)], [(D, F32), (D, BF16)],
                   [D, D, D, D, D], B=B, S=S, name=name)


def _gelu(y):
    c0 = math.sqrt(2.0 / math.pi)
    t = jnp.tanh(c0 * (y + 0.044715 * (y * y * y)))
    return 0.5 * y * (1.0 + t), t


def _sigmoid(s):
    return 1.0 / (1.0 + jnp.exp(-s))


def gelu_fwd(y, *, B, S, name):
    return rowwise(lambda y: _gelu(y)[0], [y], [], [], [(y.shape[1], BF16)], [], B=B, S=S, name=name)[0]


def glu_fwd(y, s, *, B, S, name):
    return rowwise(lambda y, s: _gelu(y)[0] * _sigmoid(s), [y, s], [], [], [(y.shape[1], BF16)], [], B=B, S=S,
                   name=name)[0]


def glu_bwd1(y, s, dg, *, B, S, name):
    D = y.shape[1]

    def fn(y, s, dg):
        z = _gelu(y)[0]
        sig = _sigmoid(s)
        ds = dg * z * sig * (1.0 - sig)
        return ds, dg * sig, jnp.sum(ds, axis=0, keepdims=True)

    return rowwise(fn, [y, s, dg], [], [], [(D, BF16), (D, F32)], [D], B=B, S=S, name=name)


def glu_bwd2(y, dz1, dz2, *, B, S, name):
    D = y.shape[1]
    c0 = math.sqrt(2.0 / math.pi)

    def fn(y, dz1, dz2):
        _, t = _gelu(y)
        dgelu = 0.5 * (1.0 + t) + 0.5 * y * (1.0 - t * t) * c0 * (1.0 + 3.0 * 0.044715 * y * y)
        return (dz1 + dz2) * dgelu

    return rowwise(fn, [y, dz1, dz2], [], [], [(D, F32)], [], B=B, S=S, name=name)[0]


def silu_rows(c, *, name):
    R, W = c.shape
    return rowwise(lambda c: c * _sigmoid(c), [c], [], [], [(W, F32)], [], B=1, S=R, name=name)[0]


def conv_gate_fwd(up, cw, cb, *, B, S, name):
    F = up.shape[1] // 2
    tn = _tile(F, (256, 128))
    nF = F // tn

    def body(g_ref, v_ref, wg_ref, wv_ref, bg_ref, bv_ref, o_ref):
        rows = lax.broadcasted_iota(jnp.int32, (S, tn), 0)

        def conv(x, w_ref):
            x1 = jnp.where(rows >= 1, pltpu.roll(x, 1, 0), 0.0)
            x2 = jnp.where(rows >= 2, pltpu.roll(x, 2, 0), 0.0)
            return w_ref[2:3, :] * x + w_ref[1:2, :] * x1 + w_ref[0:1, :] * x2

        gc = conv(g_ref[...].astype(F32), wg_ref) + bg_ref[...]
        vc = conv(v_ref[...].astype(F32), wv_ref) + bv_ref[...]
        o_ref[...] = (gc * _sigmoid(gc) * vc).astype(o_ref.dtype)

    def cols(off):
        return pl.BlockSpec((S, tn), lambda b, j: (b, j + off))

    def vec(rows, off):
        return pl.BlockSpec((rows, tn), lambda b, j: (0, j + off))

    return pl.pallas_call(
        body, name=name, out_shape=jax.ShapeDtypeStruct((B * S, F), BF16), grid=(B, nF),
        in_specs=[cols(0), cols(nF), vec(3, 0), vec(3, nF), vec(1, 0), vec(1, nF)],
        out_specs=pl.BlockSpec((S, tn), lambda b, j: (b, j)),
        compiler_params=_params(("parallel", "parallel")),
    )(up, up, cw, cw, cb, cb)


def conv_gate_bwd(up, dact, cw, cb, *, B, S, name):
    F = up.shape[1] // 2
    tn = _tile(F, (256, 128))
    nF = F // tn

    def body(g_ref, v_ref, da_ref, wg_ref, wv_ref, bg_ref, bv_ref, o_ref, dw_ref, db_ref):
        rows = lax.broadcasted_iota(jnp.int32, (S, tn), 0)

        def earlier(x, k):
            return jnp.where(rows >= k, pltpu.roll(x, k, 0), 0.0)

        def later(x, k):
            return jnp.where(rows < S - k, pltpu.roll(x, S - k, 0), 0.0)

        def conv(x, w_ref):
            x1, x2 = earlier(x, 1), earlier(x, 2)
            return w_ref[2:3, :] * x + w_ref[1:2, :] * x1 + w_ref[0:1, :] * x2, x1, x2

        def back(d, x, x1, x2, w_ref, half):
            o_ref[half] = (w_ref[2:3, :] * d + w_ref[1:2, :] * later(d, 1) + w_ref[0:1, :] * later(d, 2)
                           ).astype(o_ref.dtype)
            dw_ref[half] = jnp.concatenate([jnp.sum(d * x2, axis=0, keepdims=True),
                                            jnp.sum(d * x1, axis=0, keepdims=True),
                                            jnp.sum(d * x, axis=0, keepdims=True)], axis=0)
            return jnp.sum(d, axis=0, keepdims=True)

        g, v, da = g_ref[...].astype(F32), v_ref[...].astype(F32), da_ref[...].astype(F32)
        gc, g1, g2 = conv(g, wg_ref)
        vc, v1, v2 = conv(v, wv_ref)
        gc = gc + bg_ref[...]
        vc = vc + bv_ref[...]
        sig = _sigmoid(gc)
        dg = da * vc * (sig * (1.0 + gc * (1.0 - sig)))
        dv = da * (gc * sig)
        db_ref[...] = jnp.concatenate([back(dg, g, g1, g2, wg_ref, 0), back(dv, v, v1, v2, wv_ref, 1)], axis=0)

    def cols(off):
        return pl.BlockSpec((S, tn), lambda b, j: (b, j + off))

    def vec(rows, off):
        return pl.BlockSpec((rows, tn), lambda b, j: (0, j + off))

    return pl.pallas_call(
        body, name=name,
        out_shape=[jax.ShapeDtypeStruct((2, B * S, F), BF16), jax.ShapeDtypeStruct((B, 2, 3, F), F32),
                   jax.ShapeDtypeStruct((B, 2, F), F32)],
        grid=(B, nF),
        in_specs=[cols(0), cols(nF), cols(0), vec(3, 0), vec(3, nF), vec(1, 0), vec(1, nF)],
        out_specs=[pl.BlockSpec((2, S, tn), lambda b, j: (0, b, j)),
                   pl.BlockSpec((None, 2, 3, tn), lambda b, j: (b, 0, 0, j)),
                   pl.BlockSpec((None, 2, tn), lambda b, j: (b, 0, j))],
        compiler_params=_params(("parallel", "parallel")),
    )(up, up, dact, cw, cw, cb, cb)


MASKED_LOG = -1e30


def _split2(x):
    bits = lax.bitcast_convert_type(x, jnp.uint32) & jnp.uint32(0xFFFF0000)
    hi = lax.bitcast_convert_type(bits, F32)
    return hi.astype(BF16), (x - hi).astype(BF16)


def _nt(a, b):
    return lax.dot_general(a, b, (((1,), (1,)), ((), ())), preferred_element_type=F32)


def _tn(a, b):
    return lax.dot_general(a, b, (((0,), (0,)), ((), ())), preferred_element_type=F32)


def _att_scores(q, k, mask, prescaled=False):
    z = _nt(q, k)
    if not prescaled:
        z = z * (HEAD_DIM ** -0.5)
    e = jnp.exp(-jnp.abs(z))
    sp = jnp.log(1.0 + e)
    lb = jnp.minimum(z, 0.0) - sp
    l1 = lb - z
    if mask is not None:
        lb = jnp.where(mask, lb, MASKED_LOG)
        l1 = jnp.where(mask, l1, 0.0)
    return z, lb, l1, e


def _col_to_row(col, eye):
    return jnp.sum(jnp.where(eye, col, 0.0), axis=0, keepdims=True)


def _row_to_col(row, eye):
    return jnp.sum(jnp.where(eye, row, 0.0), axis=1, keepdims=True)


def _wide_consts(T, W):
    r = lax.broadcasted_iota(jnp.int32, (W, W), 0)
    c = lax.broadcasted_iota(jnp.int32, (W, W), 1)
    two = lambda m: jnp.concatenate([m.astype(BF16)] * 2, axis=0)
    qrow = lax.broadcasted_iota(jnp.int32, (T, W), 0)
    kcol = lax.broadcasted_iota(jnp.int32, (T, W), 1)
    er = lax.broadcasted_iota(jnp.int32, (T, T), 0)
    ec = lax.broadcasted_iota(jnp.int32, (T, T), 1)
    return two(r > c), two(r <= c), two(r < c), qrow, kcol, er == ec


def _pair_masks(x, first):
    zero = jnp.zeros_like(x)
    return jnp.where(first, x, zero), jnp.where(first, zero, x)


def attn_fwd_pairs(qkv, shards=(), slots=(), *, B, S, name):
    D = qkv.shape[1] // 3
    H = D // HEAD_DIM
    T = ATT_BLOCK
    W = 2 * T
    nq = S // T
    G = _tile(H, (ATT_HEADS, 2))
    P = G // 2
    LW = 2 * HEAD_DIM * P
    nsec = D // LW
    n = len(shards)
    n_steps = B * nsec

    def body(*refs):
        q_ref, k_ref, v_ref = refs[:3]
        o_ref, l_ref = refs[3 + 2 * n:5 + 2 * n]
        step_id = pl.program_id(0) * nsec + pl.program_id(1)
        if n:
            gather = _ShardGather(refs[3:3 + n], refs[5 + 2 * n:5 + 3 * n], *refs[5 + 3 * n:])
            pl.when(step_id == 0)(gather.send)
            pl.when(step_id == n_steps - 1)(gather.forward)
        later2, _, _, qrow, kcol, eye = _wide_consts(T, W)
        blk = lax.broadcasted_iota(jnp.int32, (nq, T), 0)
        first_q = lax.broadcasted_iota(jnp.int32, (T, 2 * HEAD_DIM), 1) < HEAD_DIM
        first_k = lax.broadcasted_iota(jnp.int32, (W, 2 * HEAD_DIM), 1) < HEAD_DIM

        def lanes(j):
            return slice(j * 2 * HEAD_DIM, (j + 1) * 2 * HEAD_DIM)

        def step(qms, k0, st, mask):
            accs, runs = st
            parts, lbs, sums = [], [], []
            for g in range(G):
                _, lb, l1, _ = _att_scores(qms[g], k_ref[pl.ds(k0, W), lanes(g // 2)], mask, prescaled=True)
                parts.append(jnp.concatenate(_split2(l1), axis=1))
                lbs.append(lb)
                sums.append(jnp.sum(l1, axis=1, keepdims=True))
            suf = jnp.dot(jnp.concatenate(parts, axis=0), later2, preferred_element_type=F32)
            new_accs, new_runs = [], []
            for j in range(P):
                vms = _pair_masks(v_ref[pl.ds(k0, W), lanes(j)], first_k)
                acc = accs[j]
                for h in range(2):
                    g = 2 * j + h
                    w = jnp.exp(lbs[g] + suf[g * T:(g + 1) * T] + runs[g])
                    acc = acc + jnp.dot(w.astype(BF16), vms[h], preferred_element_type=F32)
                    new_runs.append(runs[g] + sums[g])
                new_accs.append(acc)
            return tuple(new_accs), tuple(new_runs)

        def qblock(i, totals):
            q0 = pl.multiple_of(i * T, T)
            qms = []
            for j in range(P):
                qms.extend(_pair_masks(q_ref[pl.ds(q0, T), lanes(j)] * (HEAD_DIM ** -0.5), first_q))
            half = jnp.right_shift(i, 1)
            last = half * W
            k_last = pl.multiple_of(last, W)
            mask = (k_last + kcol) < (q0 + qrow)
            st = (tuple(jnp.zeros((T, 2 * HEAD_DIM), F32) for _ in range(P)),
                  tuple(jnp.zeros((T, 1), F32) for _ in range(G)))
            st = step(qms, k_last, st, mask)

            def kblock(jj, st):
                return step(qms, pl.multiple_of(last - jj * W, W), st, None)

            accs, runs = lax.fori_loop(1, half + 1, kblock, st)
            for j in range(P):
                o_ref[pl.ds(q0, T), lanes(j)] = accs[j].astype(o_ref.dtype)
            return tuple(jnp.where(blk == i, _col_to_row(runs[g], eye), totals[g]) for g in range(G))

        totals = lax.fori_loop(0, nq, qblock, tuple(jnp.zeros((nq, T), F32) for _ in range(G)))
        for g in range(G):
            l_ref[g] = totals[g]
        if n:
            pl.when(step_id == n_steps - 1)(gather.finish)

    def cols(section):
        return pl.BlockSpec((S, LW), lambda b, h: (b, section * nsec + h))

    lspec = pl.BlockSpec((None, G, nq, T), lambda b, h: (b, h, 0, 0))
    dma = pltpu.SemaphoreType.DMA
    return pl.pallas_call(
        body, name=name,
        out_shape=[jax.ShapeDtypeStruct((B * S, D), BF16), jax.ShapeDtypeStruct((B, H, nq, T), F32)]
        + [jax.ShapeDtypeStruct(s.shape, s.dtype) for s in slots],
        grid=(B, nsec), in_specs=[cols(0), cols(1), cols(2)] + _any_specs(2 * n),
        out_specs=[cols(0), lspec] + _any_specs(n),
        input_output_aliases={3 + n + i: 2 + i for i in range(n)},
        scratch_shapes=[dma((3 * n,))] * 4 if n else [],
        compiler_params=_params(("arbitrary", "arbitrary")),
    )(qkv, qkv, qkv, *shards, *slots)


def attn_bwd_pairs(qkv, ltot, do, partials=(), *, B, S, name):
    D = qkv.shape[1] // 3
    H = D // HEAD_DIM
    T = ATT_BLOCK
    W = 2 * T
    nq = S // T
    scale = HEAD_DIM ** -0.5
    G = _tile(H, (ATT_HEADS_BWD, 2))
    P = G // 2
    LW = 2 * HEAD_DIM * P
    nsec = D // LW
    n = len(partials)
    n_steps = B * nsec

    def body(*refs):
        q_ref, k_ref, v_ref, l_ref, do_ref = refs[:5]
        d_ref = refs[5 + n]
        dk_acc, dv_acc = refs[6 + 2 * n:8 + 2 * n]
        step_id = pl.program_id(0) * nsec + pl.program_id(1)
        if n:
            scatter = _ChipScatter(refs[5:5 + n], refs[6 + n:6 + 2 * n], *refs[8 + 2 * n:])
            pl.when(step_id == 0)(scatter.send)
        _, upto2, before2, qrow, kcol, eye = _wide_consts(T, W)
        blk = lax.broadcasted_iota(jnp.int32, (nq, T), 0)
        first_q = lax.broadcasted_iota(jnp.int32, (T, 2 * HEAD_DIM), 1) < HEAD_DIM
        first_k = lax.broadcasted_iota(jnp.int32, (W, 2 * HEAD_DIM), 1) < HEAD_DIM
        dk_acc[...] = jnp.zeros_like(dk_acc)
        dv_acc[...] = jnp.zeros_like(dv_acc)

        def lanes(j):
            return slice(j * 2 * HEAD_DIM, (j + 1) * 2 * HEAD_DIM)

        def step(qms, doms, tots, k0, st, mask):
            dqs, runs_l, runs_d = st
            sc = []
            for g in range(G):
                z, lb, l1, _ = _att_scores(qms[g], k_ref[pl.ds(k0, W), lanes(g // 2)], mask, prescaled=True)
                beta = 0.5 * jnp.tanh(0.5 * z) + 0.5
                omb = 1.0 - beta
                if mask is not None:
                    beta = jnp.where(mask, beta, 0.0)
                dw = _nt(doms[g], v_ref[pl.ds(k0, W), lanes(g // 2)])
                sc.append((lb, jnp.concatenate(_split2(l1), axis=1), jnp.sum(l1, axis=1, keepdims=True), dw, beta, omb))
            pre = jnp.dot(jnp.concatenate([s[1] for s in sc], axis=0), upto2, preferred_element_type=F32)
            dlws = []
            for j in range(P):
                dv = None
                for h in range(2):
                    g = 2 * j + h
                    w = jnp.exp(sc[g][0] + (tots[g] - (pre[g * T:(g + 1) * T] + runs_l[g])))
                    t = _tn(w.astype(BF16), doms[g])
                    dv = t if dv is None else dv + t
                    dlws.append(sc[g][3] * w)
                dv_acc[j, pl.ds(k0, W), :] += dv
            pre_d = jnp.dot(jnp.concatenate([jnp.concatenate(_split2(d), axis=1) for d in dlws], axis=0), before2,
                            preferred_element_type=F32)
            new_dqs, new_l, new_d = [], [], []
            for j in range(P):
                kms = _pair_masks(k_ref[pl.ds(k0, W), lanes(j)], first_k)
                dq, dk = dqs[j], None
                for h in range(2):
                    g = 2 * j + h
                    _, _, rowsum, _, beta, omb = sc[g]
                    dl1 = pre_d[g * T:(g + 1) * T] + runs_d[g]
                    dz = (dlws[g] * omb - dl1 * beta).astype(BF16)
                    dq = dq + jnp.dot(dz, kms[h], preferred_element_type=F32)
                    t = _tn(dz, qms[g])
                    dk = t if dk is None else dk + t
                    new_l.append(runs_l[g] + rowsum)
                    new_d.append(runs_d[g] + jnp.sum(dlws[g], axis=1, keepdims=True))
                dk_acc[j, pl.ds(k0, W), :] += dk
                new_dqs.append(dq)
            return tuple(new_dqs), tuple(new_l), tuple(new_d)

        def qblock(i, carry0):
            q0 = pl.multiple_of(i * T, T)
            qms, doms = [], []
            for j in range(P):
                qms.extend(_pair_masks(q_ref[pl.ds(q0, T), lanes(j)] * scale, first_q))
                doms.extend(_pair_masks(do_ref[pl.ds(q0, T), lanes(j)], first_q))
            tots = [_row_to_col(jnp.sum(jnp.where(blk == i, l_ref[g], 0.0), axis=0, keepdims=True), eye)
                    for g in range(G)]
            z1 = tuple(jnp.zeros((T, 1), F32) for _ in range(G))
            st = (tuple(jnp.zeros((T, 2 * HEAD_DIM), F32) for _ in range(P)), z1, z1)

            def kblock(j, st):
                return step(qms, doms, tots, pl.multiple_of(j * W, W), st, None)

            half = jnp.right_shift(i, 1)
            st = lax.fori_loop(0, half, kblock, st)
            k_last = pl.multiple_of(half * W, W)
            dqs, _, _ = step(qms, doms, tots, k_last, st, (k_last + kcol) < (q0 + qrow))
            for j in range(P):
                d_ref[0, pl.ds(q0, T), lanes(j)] = (dqs[j] * scale).astype(d_ref.dtype)
            return carry0

        lax.fori_loop(0, nq, qblock, 0)
        for j in range(P):
            d_ref[1, :, lanes(j)] = dk_acc[j].astype(d_ref.dtype)
            d_ref[2, :, lanes(j)] = dv_acc[j].astype(d_ref.dtype)
        if n:
            pl.when(step_id == n_steps - 1)(scatter.finish)

    def cols(section):
        return pl.BlockSpec((S, LW), lambda b, h: (b, section * nsec + h))

    lspec = pl.BlockSpec((None, G, nq, T), lambda b, h: (b, h, 0, 0))
    dma = pltpu.SemaphoreType.DMA
    return pl.pallas_call(
        body, name=name,
        out_shape=[jax.ShapeDtypeStruct((3, B * S, D), BF16)]
        + [jax.ShapeDtypeStruct((N_CHIPS - 1,) + a.shape[1:], a.dtype) for a in partials],
        grid=(B, nsec),
        in_specs=[cols(0), cols(1), cols(2), lspec, cols(0)] + _any_specs(n),
        out_specs=[pl.BlockSpec((3, S, LW), lambda b, h: (0, b, h))] + _any_specs(n),
        scratch_shapes=[pltpu.VMEM((P, S, 2 * HEAD_DIM), F32), pltpu.VMEM((P, S, 2 * HEAD_DIM), F32)]
        + ([dma((3 * n,)), dma((3 * n,))] if n else []),
        compiler_params=_params(("arbitrary", "arbitrary")),
    )(qkv, qkv, qkv, ltot, do, *partials)


def _cmul(ar, ai, br, bi):
    return ar * br - ai * bi, ar * bi + ai * br


def _cpow(lr, li, n):
    rr, ri = None, None
    br, bi = lr, li
    while n:
        if n & 1:
            rr, ri = (br, bi) if rr is None else _cmul(rr, ri, br, bi)
        n >>= 1
        if n:
            br, bi = _cmul(br, bi, br, bi)
    return rr, ri


def _ssm_scan(sr, si, lr, li, n_steps, reverse):
    W = sr.shape[1]
    R = SEGMENTS
    lim = -li if reverse else li
    zero = jnp.zeros((R, W), F32)

    def row(k):
        i = (n_steps - 1 - k) if reverse else k
        return pl.multiple_of(i * R, R)

    def local(k, st):
        cr, ci = st
        r0 = row(k)
        pr, pi = _cmul(lr, lim, cr, ci)
        nr = pr + sr[pl.ds(r0, R), :]
        ni = pi + si[pl.ds(r0, R), :]
        sr[pl.ds(r0, R), :] = nr
        si[pl.ds(r0, R), :] = ni
        return nr, ni

    er, ei = lax.fori_loop(0, n_steps, local, (zero, zero), unroll=SCAN_UNROLL)
    lnr, lni = _cpow(lr, lim, n_steps)
    rows = lax.broadcasted_iota(jnp.int32, (R, W), 0)
    cr, ci = zero, zero
    for step in range(1, R):
        tr, ti = _cmul(lnr, lni, cr, ci)
        tr, ti = tr + er, ti + ei
        if reverse:
            seg = R - 1 - step
            tr, ti = pltpu.roll(tr, R - 1, 0), pltpu.roll(ti, R - 1, 0)
        else:
            seg = step
            tr, ti = pltpu.roll(tr, 1, 0), pltpu.roll(ti, 1, 0)
        cr = jnp.where(rows == seg, tr, cr)
        ci = jnp.where(rows == seg, ti, ci)

    def fix(k, st):
        pr, pi = st
        r0 = row(k)
        ar, ai = _cmul(pr, pi, cr, ci)
        sr[pl.ds(r0, R), :] += ar
        si[pl.ds(r0, R), :] += ai
        return _cmul(lr, lim, pr, pi)

    lax.fori_loop(0, n_steps, fix, (lr, lim), unroll=SCAN_UNROLL)
    return cr, ci


def _ssm_specs(S, W):
    CH = GROUPS_PER_BLOCK * SSM_GROUP
    return dict(
        rows=pl.BlockSpec((S, CH), lambda b, j: (b, j)),
        b=pl.BlockSpec((None, CH, W), lambda b, j: (j, 0, 0)),
        c=pl.BlockSpec((None, W, CH), lambda b, j: (j, 0, 0)),
        lam=pl.BlockSpec((None, SEGMENTS, W), lambda b, j: (j, 0, 0)),
        vec=pl.BlockSpec((1, CH), lambda b, j: (0, j)),
    )


def ssm_fwd(u, bre, bim, cre, cim, lr8, li8, dsk, *, B, S, name):
    D = u.shape[1]
    J, CH, W = bre.shape
    n_steps = S // SEGMENTS
    sp = _ssm_specs(S, W)

    def body(u_ref, bre_ref, bim_ref, cre_ref, cim_ref, lr_ref, li_ref, dsk_ref, y_ref, sr, si):
        u = u_ref[...]
        ub = u.astype(BF16)
        sr[...] = jnp.dot(ub, bre_ref[...], preferred_element_type=F32)
        si[...] = jnp.dot(ub, bim_ref[...], preferred_element_type=F32)
        _ssm_scan(sr, si, lr_ref[...], li_ref[...], n_steps, False)
        y = jnp.dot(sr[...].astype(BF16), cre_ref[...], preferred_element_type=F32)
        y = y - jnp.dot(si[...].astype(BF16), cim_ref[...], preferred_element_type=F32)
        y_ref[...] = y + dsk_ref[...] * u

    return pl.pallas_call(
        body, name=name, out_shape=jax.ShapeDtypeStruct((B * S, D), F32), grid=(B, J),
        in_specs=[sp["rows"], sp["b"], sp["b"], sp["c"], sp["c"], sp["lam"], sp["lam"], sp["vec"]],
        out_specs=sp["rows"],
        scratch_shapes=[pltpu.VMEM((S, W), F32), pltpu.VMEM((S, W), F32)],
        compiler_params=_params(("parallel", "parallel")),
    )(u, bre, bim, cre, cim, lr8, li8, dsk)


def ssm_bwd(u, dy, bre, bim, cre, cim, lr8, li8, dsk, *, B, S, name):
    D = u.shape[1]
    J, CH, W = bre.shape
    n_steps = S // SEGMENTS
    sp = _ssm_specs(S, W)

    def body(u_ref, dy_ref, bre_ref, bim_ref, cre_ref, cim_ref, lr_ref, li_ref, dsk_ref,
             du_ref, dbre_ref, dbim_ref, dcre_ref, dcim_ref, dlr_ref, dli_ref, ddsk_ref, sr, si, ar, ai):
        u = u_ref[...]
        dy = dy_ref[...]
        ub = u.astype(BF16)
        dyb = dy.astype(BF16)
        lr, li = lr_ref[...], li_ref[...]
        sr[...] = jnp.dot(ub, bre_ref[...], preferred_element_type=F32)
        si[...] = jnp.dot(ub, bim_ref[...], preferred_element_type=F32)
        cr, ci = _ssm_scan(sr, si, lr, li, n_steps, False)
        ar[...] = _nt(dyb, cre_ref[...])
        ai[...] = -_nt(dyb, cim_ref[...])
        _ssm_scan(ar, ai, lr, li, n_steps, True)

        def dlam(k, st):
            dr, di = st
            r0 = pl.multiple_of((k + 1) * SEGMENTS, SEGMENTS)
            p0 = pl.multiple_of(k * SEGMENTS, SEGMENTS)
            pr, pi = sr[pl.ds(p0, SEGMENTS), :], si[pl.ds(p0, SEGMENTS), :]
            xr, xi = ar[pl.ds(r0, SEGMENTS), :], ai[pl.ds(r0, SEGMENTS), :]
            return dr + pr * xr + pi * xi, di + pr * xi - pi * xr

        xr, xi = ar[0:SEGMENTS, :], ai[0:SEGMENTS, :]
        dr, di = lax.fori_loop(0, n_steps - 1, dlam, (cr * xr + ci * xi, cr * xi - ci * xr), unroll=SCAN_UNROLL)
        dlr_ref[...] = dr
        dli_ref[...] = di
        arb = ar[...].astype(BF16)
        aib = ai[...].astype(BF16)
        du_ref[...] = _nt(arb, bre_ref[...]) + _nt(aib, bim_ref[...]) + dsk_ref[...] * dy
        dbre_ref[...] = _tn(ub, arb)
        dbim_ref[...] = _tn(ub, aib)
        dcre_ref[...] = _tn(sr[...].astype(BF16), dyb)
        dcim_ref[...] = -_tn(si[...].astype(BF16), dyb)
        ddsk_ref[...] = jnp.sum(dy * u, axis=0, keepdims=True)

    def per(shape):
        return pl.BlockSpec((None, None) + shape, lambda b, j: (b, j, 0, 0))

    return pl.pallas_call(
        body, name=name,
        out_shape=[jax.ShapeDtypeStruct((B * S, D), F32),
                   jax.ShapeDtypeStruct((B, J, CH, W), F32), jax.ShapeDtypeStruct((B, J, CH, W), F32),
                   jax.ShapeDtypeStruct((B, J, W, CH), F32), jax.ShapeDtypeStruct((B, J, W, CH), F32),
                   jax.ShapeDtypeStruct((B, J, SEGMENTS, W), F32), jax.ShapeDtypeStruct((B, J, SEGMENTS, W), F32),
                   jax.ShapeDtypeStruct((B, J, 1, CH), F32)],
        grid=(B, J),
        in_specs=[sp["rows"], sp["rows"], sp["b"], sp["b"], sp["c"], sp["c"], sp["lam"], sp["lam"], sp["vec"]],
        out_specs=[sp["rows"], per((CH, W)), per((CH, W)), per((W, CH)), per((W, CH)), per((SEGMENTS, W)),
                   per((SEGMENTS, W)),
                   per((1, CH))],
        scratch_shapes=[pltpu.VMEM((S, W), F32)] * 4,
        compiler_params=_params(("parallel", "parallel")),
    )(u, dy, bre, bim, cre, cim, lr8, li8, dsk)


def _ssm_discretize(a_re, a_im, log_dt, b_re, b_im):
    dt = jnp.exp(log_dt)[:, None]
    er = jnp.exp(a_re * dt)
    lr = er * jnp.cos(a_im * dt)
    li = er * jnp.sin(a_im * dt)
    den = a_re * a_re + a_im * a_im
    fr = ((lr - 1.0) * a_re + li * a_im) / den
    fi = (li * a_re - (lr - 1.0) * a_im) / den
    bbr = fr[..., None] * b_re - fi[..., None] * b_im
    bbi = fr[..., None] * b_im + fi[..., None] * b_re
    return lr, li, bbr, bbi


def _block_diag_in(m):
    G, P, H = m.shape
    J = G // GROUPS_PER_BLOCK
    m = m.reshape(J, GROUPS_PER_BLOCK, P, H).transpose(0, 1, 3, 2)
    eye = jnp.eye(GROUPS_PER_BLOCK, dtype=m.dtype)
    out = m[:, :, :, None, :] * eye[None, :, None, :, None]
    return out.reshape(J, GROUPS_PER_BLOCK * H, GROUPS_PER_BLOCK * P)


def _block_diag_in_grad(d, G, P, H):
    J = G // GROUPS_PER_BLOCK
    d = d.reshape(J, GROUPS_PER_BLOCK, H, GROUPS_PER_BLOCK, P)
    idx = jnp.arange(GROUPS_PER_BLOCK)
    d = d[:, idx, :, idx, :]
    return d.transpose(1, 0, 3, 2).reshape(G, P, H)


def _block_diag_out(m):
    G, H, P = m.shape
    J = G // GROUPS_PER_BLOCK
    m = m.reshape(J, GROUPS_PER_BLOCK, H, P).transpose(0, 1, 3, 2)
    eye = jnp.eye(GROUPS_PER_BLOCK, dtype=m.dtype)
    out = m[:, :, :, None, :] * eye[None, :, None, :, None]
    return out.reshape(J, GROUPS_PER_BLOCK * P, GROUPS_PER_BLOCK * H)


def _block_diag_out_grad(d, G, H, P):
    J = G // GROUPS_PER_BLOCK
    d = d.reshape(J, GROUPS_PER_BLOCK, P, GROUPS_PER_BLOCK, H)
    idx = jnp.arange(GROUPS_PER_BLOCK)
    d = d[:, idx, :, idx, :]
    return d.transpose(1, 0, 3, 2).reshape(G, H, P)


def _interleave(a, B, S):
    L = S // SEGMENTS
    return a.reshape(B, SEGMENTS, L, a.shape[-1]).transpose(0, 2, 1, 3).reshape(B * S, a.shape[-1])


def _deinterleave(a, B, S):
    L = S // SEGMENTS
    return a.reshape(B, L, SEGMENTS, a.shape[-1]).transpose(0, 2, 1, 3).reshape(B * S, a.shape[-1])


def _adamw_math(w, g, m, v):
    m = ADAM_B1 * m + (1.0 - ADAM_B1) * g
    v = ADAM_B2 * v + (1.0 - ADAM_B2) * (g * g)
    m_hat = m / (1.0 - ADAM_B1 ** ADAM_STEP)
    v_hat = v / (1.0 - ADAM_B2 ** ADAM_STEP)
    delta = -ADAM_LR * (m_hat / (jnp.sqrt(v_hat) + ADAM_EPS) + ADAM_WD * w)
    return delta, m, v


def adamw(w, g, m, v, *, name):
    R, C = w.shape
    tr = _tile(R, (max(8, (1 << 18) // C // 8 * 8), 256, 128, 64, 32, 16, 8))

    def body(w_ref, g_ref, m_ref, v_ref, d_ref, nm_ref, nv_ref):
        d, nm, nv = _adamw_math(w_ref[...], g_ref[...], m_ref[...], v_ref[...])
        d_ref[...] = d
        nm_ref[...] = nm
        nv_ref[...] = nv

    spec = pl.BlockSpec((tr, C), lambda i: (i, 0))
    shp = jax.ShapeDtypeStruct((R, C), F32)
    return pl.pallas_call(
        body, name=name, out_shape=[shp, shp, shp], grid=(R // tr,), in_specs=[spec] * 4, out_specs=[spec] * 3,
        compiler_params=_params(("parallel",)),
    )(w, g, m, v)


def adamw_many(ws, gs, ms, vs, *, name):
    n = len(ws)
    at_least_2d = lambda a: a.reshape(1, -1) if a.ndim == 1 else a
    args = [at_least_2d(a) for group in (ws, gs, ms, vs) for a in group]

    def body(*refs):
        for i in range(n):
            d, nm, nv = _adamw_math(refs[i][...], refs[n + i][...], refs[2 * n + i][...], refs[3 * n + i][...])
            refs[4 * n + 3 * i][...] = d
            refs[4 * n + 3 * i + 1][...] = nm
            refs[4 * n + 3 * i + 2][...] = nv

    out = pl.pallas_call(
        body, name=name, out_shape=[jax.ShapeDtypeStruct(a.shape, F32) for a in args[:n] for _ in range(3)],
        in_specs=[pl.BlockSpec(memory_space=pltpu.VMEM) for _ in args],
        out_specs=[pl.BlockSpec(memory_space=pltpu.VMEM) for _ in range(3 * n)],
        compiler_params=pltpu.CompilerParams(vmem_limit_bytes=V7X_VMEM_LIMIT),
    )(*args)
    return [tuple(o.reshape(w.shape) for o in out[3 * i:3 * i + 3]) for i, w in enumerate(ws)]


def _any_specs(n):
    return [pl.BlockSpec(memory_space=pl.ANY) for _ in range(n)]


def _coords():
    return lax.axis_index("x"), lax.axis_index("y"), lax.axis_index("c")


def _flip(v, bit):
    return (v + bit) % 2


def all_gather8(a, *, name):
    shape = a.shape

    def body(a_ref, o_ref, send_sems, recv_sems, local_sem):
        x, y, c = _coords()
        me = 4 * x + 2 * y + c
        mine = pltpu.make_async_copy(a_ref, o_ref.at[me], local_sem)
        mine.start()
        sends = []
        for k in range(1, N_DEV):
            peer = (_flip(x, (k >> 2) & 1), _flip(y, (k >> 1) & 1), _flip(c, k & 1))
            cp = pltpu.make_async_remote_copy(a_ref, o_ref.at[me], send_sems.at[k - 1], recv_sems.at[k - 1],
                                              device_id=peer, device_id_type=MESH)
            cp.start()
            sends.append(cp)
        for k in range(1, N_DEV):
            px, py, pc = _flip(x, (k >> 2) & 1), _flip(y, (k >> 1) & 1), _flip(c, k & 1)
            src = 4 * px + 2 * py + pc
            pltpu.make_async_remote_copy(a_ref, o_ref.at[src], send_sems.at[k - 1], recv_sems.at[k - 1],
                                         device_id=(px, py, pc), device_id_type=MESH).wait_recv()
        for cp in sends:
            cp.wait_send()
        mine.wait()

    return pl.pallas_call(
        body, name=name, out_shape=jax.ShapeDtypeStruct((N_DEV,) + shape, a.dtype),
        in_specs=_any_specs(1), out_specs=pl.BlockSpec(memory_space=pl.ANY),
        scratch_shapes=[pltpu.SemaphoreType.DMA((N_DEV - 1,)), pltpu.SemaphoreType.DMA((N_DEV - 1,)),
                        pltpu.SemaphoreType.DMA(())],
    )(a)


def _chip_of(x, y, p):
    px, py = _flip(x, (p >> 1) & 1), _flip(y, p & 1)
    return 2 * px + py, px, py


class _ShardGather:
    def __init__(self, ins, outs, ici_send, ici_recv, d2d_send, d2d_recv):
        self.ins, self.outs = ins, outs
        self.sems = ici_send, ici_recv, d2d_send, d2d_recv
        self.x, self.y, self.c = _coords()
        self.me = 2 * self.x + self.y

    def _ici(self, i, p, slot):
        half = self.ins[i].shape[0] // 2
        rows = pl.ds(self.c * half, half)
        _, px, py = _chip_of(self.x, self.y, p)
        s = i * 3 + p - 1
        return pltpu.make_async_remote_copy(self.ins[i].at[rows], self.outs[i].at[slot, rows], self.sems[0].at[s],
                                            self.sems[1].at[s], device_id=(px, py, self.c), device_id_type=MESH)

    def _d2d(self, i, p, mine):
        half = self.ins[i].shape[0] // 2
        rows = pl.ds((self.c if mine else 1 - self.c) * half, half)
        src, _, _ = _chip_of(self.x, self.y, p)
        s = i * 3 + p - 1
        part = self.outs[i].at[src, rows]
        return pltpu.make_async_remote_copy(part, part, self.sems[2].at[s], self.sems[3].at[s],
                                            device_id=(self.x, self.y, 1 - self.c), device_id_type=MESH)

    def _each(self):
        return [(i, p) for i in range(len(self.ins)) for p in range(1, N_CHIPS)]

    def send(self):
        for i, p in self._each():
            self._ici(i, p, self.me).start()

    def forward(self):
        for i, p in self._each():
            self._ici(i, p, _chip_of(self.x, self.y, p)[0]).wait_recv()
            self._d2d(i, p, True).start()

    def finish(self):
        for i, p in self._each():
            self._d2d(i, p, False).wait_recv()
        for i, p in self._each():
            self._ici(i, p, self.me).wait_send()
            self._d2d(i, p, True).wait_send()


def gather_chip_shards(arrs, remote, *, name):
    n = len(arrs)
    far = [i for i in range(n) if remote[i]]

    def body(*refs):
        ins, outs = refs[:n], refs[n:2 * n]
        ici_send, ici_recv, d2d_send, d2d_recv, local_sems = refs[2 * n:2 * n + 5]
        bufs = refs[2 * n + 5:]
        me = 2 * lax.axis_index("x") + lax.axis_index("y")
        loads = []
        for i in range(n):
            cp = pltpu.make_async_copy(ins[i], bufs[i], local_sems.at[i])
            cp.start()
            loads.append(cp)
        gather = _ShardGather([ins[i] for i in far], [outs[i] for i in far], ici_send, ici_recv, d2d_send, d2d_recv)
        gather.send()
        stores = []
        for i in range(n):
            loads[i].wait()
            cp = pltpu.make_async_copy(bufs[i], outs[i].at[me], local_sems.at[i])
            cp.start()
            stores.append(cp)
        gather.forward()
        gather.finish()
        for cp in stores:
            cp.wait()

    dma = pltpu.SemaphoreType.DMA
    m = 3 * len(far)
    return pl.pallas_call(
        body, name=name,
        out_shape=[jax.ShapeDtypeStruct((N_CHIPS,) + a.shape, a.dtype) for a in arrs],
        in_specs=_any_specs(n), out_specs=_any_specs(n),
        scratch_shapes=[dma((m,)), dma((m,)), dma((m,)), dma((m,)), dma((n,))]
        + [pltpu.VMEM(a.shape, a.dtype) for a in arrs],
        compiler_params=pltpu.CompilerParams(vmem_limit_bytes=V7X_VMEM_LIMIT),
    )(*arrs)


def swap_halves(arrs, *, name):
    n = len(arrs)

    def body(*refs):
        ins, outs = refs[:n], refs[n:2 * n]
        send_sems, recv_sems = refs[2 * n:]
        x, y, c = _coords()
        cps = []
        for i in range(n):
            half = ins[i].shape[1] // 2
            cp = pltpu.make_async_remote_copy(ins[i].at[:, pl.ds((1 - c) * half, half)], outs[i], send_sems.at[i],
                                              recv_sems.at[i], device_id=(x, y, 1 - c), device_id_type=MESH)
            cp.start()
            cps.append(cp)
        for cp in cps:
            cp.wait()

    dma = pltpu.SemaphoreType.DMA
    return pl.pallas_call(
        body, name=name,
        out_shape=[jax.ShapeDtypeStruct((N_CHIPS, a.shape[1] // 2, a.shape[2]), a.dtype) for a in arrs],
        in_specs=_any_specs(n), out_specs=_any_specs(n), scratch_shapes=[dma((n,)), dma((n,))],
    )(*arrs)


def add_half(g, other, c_idx, *, name, out_dtype):
    _, R, C = g.shape
    half = R // 2
    tr = _tile(half, (256, 128, 64, 32, 16, 8))
    nt = half // tr

    def body(c_ref, g_ref, o_ref, out_ref):
        out_ref[...] = (g_ref[...].astype(F32) + o_ref[...].astype(F32)).astype(out_ref.dtype)

    return pl.pallas_call(
        body, name=name, out_shape=jax.ShapeDtypeStruct((N_CHIPS, half, C), out_dtype),
        grid_spec=pltpu.PrefetchScalarGridSpec(
            num_scalar_prefetch=1, grid=(N_CHIPS, nt),
            in_specs=[pl.BlockSpec((None, tr, C), lambda r, t, c_ref: (r, c_ref[0] * nt + t, 0)),
                      pl.BlockSpec((None, tr, C), lambda r, t, c_ref: (r, t, 0))],
            out_specs=pl.BlockSpec((None, tr, C), lambda r, t, c_ref: (r, t, 0))),
        compiler_params=_params(("parallel", "parallel")),
    )(c_idx, g, other)


class _ChipScatter:
    def __init__(self, ins, outs, send_sems, recv_sems):
        self.ins, self.outs, self.send_sems, self.recv_sems = ins, outs, send_sems, recv_sems
        self.x, self.y, self.c = _coords()

    def _copies(self):
        for i in range(len(self.ins)):
            for p in range(1, N_CHIPS):
                dst, px, py = _chip_of(self.x, self.y, p)
                s = i * 3 + p - 1
                yield pltpu.make_async_remote_copy(self.ins[i].at[dst], self.outs[i].at[p - 1], self.send_sems.at[s],
                                                   self.recv_sems.at[s], device_id=(px, py, self.c), device_id_type=MESH)

    def send(self):
        for cp in self._copies():
            cp.start()

    def finish(self):
        for cp in self._copies():
            cp.wait()


def scatter_to_chips(arrs, *, name):
    n = len(arrs)

    def body(*refs):
        scatter = _ChipScatter(refs[:n], refs[n:2 * n], *refs[2 * n:])
        scatter.send()
        scatter.finish()

    dma = pltpu.SemaphoreType.DMA
    return pl.pallas_call(
        body, name=name,
        out_shape=[jax.ShapeDtypeStruct((N_CHIPS - 1,) + a.shape[1:], a.dtype) for a in arrs],
        in_specs=_any_specs(n), out_specs=_any_specs(n), scratch_shapes=[dma((3 * n,)), dma((3 * n,))],
    )(*arrs)


def add_chips(h, got, r_idx, *, name):
    _, R, C = h.shape
    tr = _tile(R, (256, 128, 64, 32, 16, 8))

    def body(r_ref, h_ref, g_ref, out_ref):
        acc = h_ref[...].astype(F32)
        for p in range(N_CHIPS - 1):
            acc = acc + g_ref[p].astype(F32)
        out_ref[...] = acc

    return pl.pallas_call(
        body, name=name, out_shape=jax.ShapeDtypeStruct((R, C), F32),
        grid_spec=pltpu.PrefetchScalarGridSpec(
            num_scalar_prefetch=1, grid=(R // tr,),
            in_specs=[pl.BlockSpec((None, tr, C), lambda t, r_ref: (r_ref[0], t, 0)),
                      pl.BlockSpec((N_CHIPS - 1, tr, C), lambda t, r_ref: (0, t, 0))],
            out_specs=pl.BlockSpec((tr, C), lambda t, r_ref: (t, 0))),
        compiler_params=_params(("parallel",)),
    )(r_idx, h, got)


def join_halves(arrs, *, name):
    n = len(arrs)

    def body(*refs):
        ins, outs = refs[:n], refs[n:2 * n]
        send_sems, recv_sems, local_sems = refs[2 * n:2 * n + 3]
        bufs = refs[2 * n + 3:]
        x, y, c = _coords()
        loads, sends, stores = [], [], []
        for i in range(n):
            cp = pltpu.make_async_copy(ins[i], bufs[i], local_sems.at[i])
            cp.start()
            loads.append(cp)
        for i in range(n):
            half = ins[i].shape[0]
            cp = pltpu.make_async_remote_copy(ins[i], outs[i].at[pl.ds(c * half, half)], send_sems.at[i], recv_sems.at[i],
                                              device_id=(x, y, 1 - c), device_id_type=MESH)
            cp.start()
            sends.append(cp)
        for i in range(n):
            half = ins[i].shape[0]
            loads[i].wait()
            cp = pltpu.make_async_copy(bufs[i], outs[i].at[pl.ds(c * half, half)], local_sems.at[i])
            cp.start()
            stores.append(cp)
        for i in range(n):
            half = ins[i].shape[0]
            pltpu.make_async_remote_copy(ins[i], outs[i].at[pl.ds((1 - c) * half, half)], send_sems.at[i],
                                         recv_sems.at[i], device_id=(x, y, 1 - c), device_id_type=MESH).wait_recv()
        for i in range(n):
            sends[i].wait_send()
            stores[i].wait()

    dma = pltpu.SemaphoreType.DMA
    return pl.pallas_call(
        body, name=name,
        out_shape=[jax.ShapeDtypeStruct((2 * a.shape[0], a.shape[1]), a.dtype) for a in arrs],
        in_specs=_any_specs(n), out_specs=_any_specs(n),
        scratch_shapes=[dma((n,)), dma((n,)), dma((n,))] + [pltpu.VMEM(a.shape, a.dtype) for a in arrs],
        compiler_params=pltpu.CompilerParams(vmem_limit_bytes=V7X_VMEM_LIMIT),
    )(*arrs)


def pair_sums(grads, wire_dtypes, tag):
    c_idx = jnp.reshape(lax.axis_index("c"), (1,)).astype(jnp.int32)
    theirs = swap_halves(grads, name=f"rs_swap_halves_{tag}")
    return [add_half(g, o, c_idx, name=f"rs_add_half_{tag}{i}", out_dtype=wire_dtypes[i])
            for i, (g, o) in enumerate(zip(grads, theirs))]


def chip_sums(pairs, gots, tag):
    r_idx = jnp.reshape(2 * lax.axis_index("x") + lax.axis_index("y"), (1,)).astype(jnp.int32)
    return [add_chips(h, g, r_idx, name=f"rs_add_chips_{tag}{i}") for i, (h, g) in enumerate(zip(pairs, gots))]


def _chip_major(w, axis):
    n = w.shape[axis] // N_CHIPS
    parts = w.reshape(w.shape[:axis] + (N_CHIPS, n) + w.shape[axis + 1:])
    return jnp.moveaxis(parts, axis, 0)


def _from_chip_major(g, axis):
    g = jnp.moveaxis(g, 0, axis)
    return g.reshape(g.shape[:axis] + (g.shape[axis] * g.shape[axis + 1],) + g.shape[axis + 2:])


def kernel(x, c, norm_mix, norm_ffn, w_mod, b_mod, w_qkv, w_o_attn, w_in_ssm, a_re, a_im, log_dt, b_re, b_im, c_re, c_im, d_skip, w_glu, b_glu, w_o_ssm, w_up, conv_w, conv_b, w_down, norm_out, w_fin, b_fin, loss_target, m_norm_mix, m_norm_ffn, m_w_mod, m_b_mod, m_w_qkv, m_w_o_attn, m_w_in_ssm, m_a_re, m_a_im, m_log_dt, m_b_re, m_b_im, m_c_re, m_c_im, m_d_skip, m_w_glu, m_b_glu, m_w_o_ssm, m_w_up, m_conv_w, m_conv_b, m_w_down, m_norm_out, m_w_fin, m_b_fin, v_norm_mix, v_norm_ffn, v_w_mod, v_b_mod, v_w_qkv, v_w_o_attn, v_w_in_ssm, v_a_re, v_a_im, v_log_dt, v_b_re, v_b_im, v_c_re, v_c_im, v_d_skip, v_w_glu, v_b_glu, v_w_o_ssm, v_w_up, v_conv_w, v_conv_b, v_w_down, v_norm_out, v_w_fin, v_b_fin):
    B, S, D = x.shape
    T = B * S
    F2 = conv_b.shape[1]
    F = F2 // 2
    G, P = a_re.shape[1], a_re.shape[2]
    H = b_re.shape[3]
    mx, my, mc = _coords()
    chip = 2 * mx + my
    dev = 4 * mx + 2 * my + mc
    BG = N_DEV * B
    mod_w = w_mod.shape[2]
    fin_w = w_fin.shape[1]

    c_all = all_gather8(c, name="gather_c").reshape(BG, D)
    c_act = silu_rows(c_all, name="silu_c")
    b_mod_mine = lax.dynamic_slice(b_mod, (0, chip * mod_w), (2, mod_w))
    b_fin_mine = lax.dynamic_slice(b_fin, (chip * fin_w,), (fin_w,))
    cond = [matmul(c_act, w_mod[i], bias=b_mod_mine[i], name=f"mod_proj_{i}") for i in range(2)]
    cond.append(matmul(c_act, w_fin, bias=b_fin_mine, name="fin_proj"))
    cond_all = all_gather8(jnp.concatenate(cond, axis=1), name="gather_cond")
    cond_all = cond_all[::2]
    cond_rows = lax.dynamic_slice(cond_all, (0, dev * B, 0), (N_CHIPS, B, cond_all.shape[2]))
    mods = []
    for i in range(2):
        full = cond_rows[:, :, i * mod_w:(i + 1) * mod_w].transpose(1, 0, 2).reshape(B, N_CHIPS * mod_w)
        mods.append([full[:, k * D:(k + 1) * D] for k in range(6)])
    fin = cond_rows[:, :, 2 * mod_w:].transpose(1, 0, 2).reshape(B, N_CHIPS * fin_w)
    sh_f, sc_f = fin[:, :D], fin[:, D:]

    rows1024 = jnp.concatenate([w_o_attn[0], w_in_ssm[0], w_glu[0], w_o_ssm[0], w_down.reshape(-1, D)], axis=0)
    shards = [w_qkv[0].astype(BF16), rows1024.astype(BF16), w_up[0].astype(BF16), w_up[1].astype(BF16)]
    W_qkv, *own_slots = gather_chip_shards(shards, [True, False, False, False], name="gather_weights")
    Dq = D // N_CHIPS
    Fq = F // N_CHIPS
    small =jnp.concatenate([conv_w.reshape(6, -1), jnp.pad(d_skip, ((0, 0), (0, conv_w.shape[2] - Dq))),
                             jnp.pad(b_glu, ((0, 0), (0, conv_w.shape[2] - Dq)))], axis=0)
    small_all = all_gather8(small, name="gather_small")[::2]
    conv_w_full = _from_chip_major(small_all[:, :6].reshape(N_CHIPS, 2, 3, -1), 2)
    d_skip_full = small_all[:, 6, :Dq].reshape(1, D)
    b_glu_full = small_all[:, 7, :Dq].reshape(D)

    x0 = x.reshape(T, D)
    tgt = loss_target.reshape(T, D)

    def ffn_fwd(xprev, y, gate, i):
        sh2, sc2 = mods[i][3], mods[i][4]
        xin, h2 = res_norm_mod_fwd(xprev, y, gate, norm_ffn[i], sh2, sc2, B=B, S=S, name=f"ffn_norm_{i}")
        up = matmul(h2, W_up[i], b_chips=True, out_dtype=BF16, name=f"ffn_up_{i}")
        act = conv_gate_fwd(up, conv_w_full[i], conv_b[i:i + 1], B=B, S=S, name=f"ffn_conv_{i}")
        yf = matmul(act, W_down[i], name=f"ffn_down_{i}")
        return xin, yf, (xin, h2, up, act, yf)

    sh1, sc1, g1 = mods[0][0], mods[0][1], mods[0][2]
    h1a = norm_mod_fwd(x0, norm_mix[0], sh1, sc1, B=B, S=S, name="att_norm")
    qkv = matmul(h1a, W_qkv, out_dtype=BF16, b_chips=True, name="att_qkv")
    o2, ltot, g_rows, W_up0, W_up1 = attn_fwd_pairs(qkv, shards[1:], own_slots, B=B, S=S, name="att_fwd")
    W_up = [W_up0, W_up1]
    W_o_attn = g_rows[:, 0 * Dq:1 * Dq].reshape(D, D)
    W_in = g_rows[:, 1 * Dq:2 * Dq].reshape(D, D)
    W_glu = g_rows[:, 2 * Dq:3 * Dq].reshape(D, D)
    W_o_ssm = g_rows[:, 3 * Dq:4 * Dq].reshape(D, D)
    W_down = [g_rows[:, 4 * Dq + i * Fq:4 * Dq + (i + 1) * Fq].reshape(F, D) for i in range(2)]
    ya = matmul(o2, W_o_attn, name="att_out")
    x1, yf0, ffn0 = ffn_fwd(x0, ya, g1, 0)

    lr, li, bbr, bbi = _ssm_discretize(a_re[0], a_im[0], log_dt[0], b_re[0], b_im[0])
    J = G // GROUPS_PER_BLOCK
    Wst = GROUPS_PER_BLOCK * P
    bre_blk = _block_diag_in(bbr).astype(BF16)
    bim_blk = _block_diag_in(bbi).astype(BF16)
    cre_blk = _block_diag_out(c_re[0]).astype(BF16)
    cim_blk = _block_diag_out(c_im[0]).astype(BF16)
    lr8 = jnp.broadcast_to(lr.reshape(J, 1, Wst), (J, SEGMENTS, Wst))
    li8 = jnp.broadcast_to(li.reshape(J, 1, Wst), (J, SEGMENTS, Wst))
    sh1s, sc1s, g1s = mods[1][0], mods[1][1], mods[1][2]
    x2, h1s = res_norm_mod_fwd(x1, yf0, mods[0][5], norm_mix[1], sh1s, sc1s, B=B, S=S, name="ssm_norm")
    h1p = _interleave(h1s, B, S)
    u = matmul(h1p, W_in, name="ssm_in")
    y_ssm = ssm_fwd(u, bre_blk, bim_blk, cre_blk, cim_blk, lr8, li8, d_skip_full, B=B, S=S, name="ssm_scan_fwd")
    zb = gelu_fwd(y_ssm, B=B, S=S, name="ssm_gelu")
    s_glu = matmul(zb, W_glu, bias=b_glu_full, name="ssm_glu_proj")
    gb = glu_fwd(y_ssm, s_glu, B=B, S=S, name="ssm_glu")
    ys_p = matmul(gb, W_o_ssm, name="ssm_out")
    ys = _deinterleave(ys_p, B, S)
    x3, yf1, ffn1 = ffn_fwd(x2, ys, g1s, 1)
    x4 = gate_res_fwd(x3, yf1, mods[1][5], B=B, S=S, name="ffn_res_1")

    dx4, dyf1, loss_p, dsh_f, dsc_f, dnorm_out, dg2_1 = final_loss(x4, tgt, norm_out, sh_f, sc_f, yf1, mods[1][5],
                                                                   B=B, S=S, name="loss_head")
    loss = lax.psum(jnp.sum(loss_p), ("x", "y", "c"))

    def ffn_bwd(dxo, dyf, i, saved, y_prev, gate_prev):
        xin, h2, up, act, yf = saved
        sc2 = mods[i][4]
        dact = matmul(dyf, W_down[i], tb=True, out_dtype=BF16, name=f"ffn_down_dx_{i}")
        dW_down = matmul(act, dyf, ta=True, out_dtype=BF16, name=f"ffn_down_dw_{i}")
        dup, dcw, dcb = conv_gate_bwd(up, dact, conv_w_full[i], conv_b[i:i + 1], B=B, S=S, name=f"ffn_conv_bwd_{i}")
        dh2 = matmul(dup, W_up[i], tb=True, b_chips=True, name=f"ffn_up_dx_{i}")
        dW_up = matmul(h2, dup, ta=True, b_chips=True, out_chips=True, out_dtype=BF16, name=f"ffn_up_dw_{i}")
        dxin, dy_prev, dsh2, dsc2, dnf, dgate_prev = norm_mod_bwd_gate(
            dh2, xin, dxo, norm_ffn[i], sc2, y_prev, gate_prev, B=B, S=S, name=f"ffn_norm_bwd_{i}")
        dconv_w = jnp.sum(dcw, axis=0).transpose(1, 0, 2).reshape(3, F2)
        return dxin, dy_prev, dgate_prev, dict(dW_down=dW_down, dW_up=dW_up, dconv_b=jnp.sum(dcb, axis=0).reshape(F2),
                                               dconv_w=dconv_w, dnorm_ffn=jnp.sum(dnf, axis=0), dsh2=dsh2, dsc2=dsc2)

    dx3, dys, dg1s, gf1 = ffn_bwd(dx4, dyf1, 1, ffn1, ys, g1s)
    gf1["dg2"] = dg2_1

    dys_p = _interleave(dys, B, S)
    dgb = matmul(dys_p, W_o_ssm, tb=True, name="ssm_out_dx")
    dW_o_ssm = matmul(gb, dys_p, ta=True, out_dtype=BF16, name="ssm_out_dw")
    ds_glu, dz1, db_glu = glu_bwd1(y_ssm, s_glu, dgb, B=B, S=S, name="ssm_glu_bwd1")
    dz2 = matmul(ds_glu, W_glu, tb=True, name="ssm_glu_dx")
    dW_glu = matmul(zb, ds_glu, ta=True, out_dtype=BF16, name="ssm_glu_dw")
    dy_ssm = glu_bwd2(y_ssm, dz1, dz2, B=B, S=S, name="ssm_glu_bwd2")
    du, dbre, dbim, dcre, dcim, dlr8, dli8, ddsk = ssm_bwd(u, dy_ssm, bre_blk, bim_blk, cre_blk, cim_blk, lr8, li8,
                                                           d_skip_full, B=B, S=S, name="ssm_scan_bwd")
    dub = du.astype(BF16)
    dh1p = matmul(dub, W_in, tb=True, name="ssm_in_dx")
    dW_in = matmul(h1p, dub, ta=True, out_dtype=BF16, name="ssm_in_dw")
    dx2, dyf0, dsh1s, dsc1s, dnm1, dg2_0 = norm_mod_bwd_gate(_deinterleave(dh1p, B, S), x2, dx3, norm_mix[1], sc1s,
                                                             yf0, mods[0][5], B=B, S=S, name="ssm_norm_bwd")
    dlr = jnp.sum(dlr8, axis=(0, 2)).reshape(G, P)
    dli = jnp.sum(dli8, axis=(0, 2)).reshape(G, P)
    dbbr = _block_diag_in_grad(jnp.sum(dbre, axis=0), G, P, H)
    dbbi = _block_diag_in_grad(jnp.sum(dbim, axis=0), G, P, H)
    dc_re = _block_diag_out_grad(jnp.sum(dcre, axis=0), G, H, P)
    dc_im = _block_diag_out_grad(jnp.sum(dcim, axis=0), G, H, P)
    dd_skip = jnp.sum(ddsk, axis=0).reshape(D)

    dx1, dya, dg1, gf0 = ffn_bwd(dx2, dyf0, 0, ffn0, ya, g1)
    gf0["dg2"] = dg2_0

    do2 = matmul(dya, W_o_attn, tb=True, out_dtype=BF16, name="att_out_dx")
    dW_o_attn = matmul(o2, dya, ta=True, out_dtype=BF16, name="att_out_dw")
    g_rows_cm = jnp.concatenate([dW_o_attn.reshape(N_CHIPS, Dq, D), dW_in.reshape(N_CHIPS, Dq, D),
                                 dW_glu.reshape(N_CHIPS, Dq, D), dW_o_ssm.reshape(N_CHIPS, Dq, D),
                                 gf0["dW_down"].reshape(N_CHIPS, Fq, D), gf1["dW_down"].reshape(N_CHIPS, Fq, D)], axis=1)
    pairs_a = pair_sums([g_rows_cm, gf0["dW_up"], gf1["dW_up"]], [BF16, BF16, BF16], "a")
    dqkv, *gots_a = attn_bwd_pairs(qkv, ltot, do2, pairs_a, B=B, S=S, name="att_bwd")
    dh1a = matmul(dqkv, _from_chip_major(W_qkv, 1), tb=True, name="att_qkv_dx")
    dW_qkv = _chip_major(matmul(h1a, dqkv, ta=True, b_chips=True, out_dtype=BF16, name="att_qkv_dw"), 1)
    grad_x, dsh1, dsc1, dnm0 = norm_mod_bwd(dh1a, x0, dx1, norm_mix[0], sc1, B=B, S=S, name="att_norm_bwd")

    dmod_rows = jnp.concatenate([dsh1, dsc1, dg1, gf0["dsh2"], gf0["dsc2"], gf0["dg2"],
                                 dsh1s, dsc1s, dg1s, gf1["dsh2"], gf1["dsc2"], gf1["dg2"], dsh_f, dsc_f], axis=1)
    dmod_all = all_gather8(dmod_rows, name="gather_dmod").reshape(BG, 14 * D)
    grad_w_mod = jnp.stack([
        matmul(c_act, lax.dynamic_slice(dmod_all, (0, i * 6 * D + chip * mod_w), (BG, mod_w)), ta=True,
               name=f"mod_dw_{i}") for i in range(2)])
    grad_w_fin = matmul(c_act, lax.dynamic_slice(dmod_all, (0, 12 * D + chip * fin_w), (BG, fin_w)), ta=True,
                        name="fin_dw")

    parts = [jnp.concatenate([jnp.sum(dnm0, axis=0), jnp.sum(dnm1, axis=0)]),
             jnp.concatenate([gf0["dnorm_ffn"], gf1["dnorm_ffn"]]),
             jnp.sum(dmod_rows[:, :12 * D], axis=0),
             dlr.reshape(-1), dli.reshape(-1), dbbr.reshape(-1), dbbi.reshape(-1), dc_re.reshape(-1), dc_im.reshape(-1),
             dd_skip, jnp.sum(db_glu, axis=0),
             jnp.sum(dnorm_out, axis=0), jnp.sum(dmod_rows[:, 12 * D:], axis=0),
             gf0["dconv_w"].reshape(-1), gf1["dconv_w"].reshape(-1), gf0["dconv_b"], gf1["dconv_b"]]
    sizes = [int(p.shape[0]) for p in parts]
    flat = jnp.concatenate(parts)
    width = 1024
    quantum = N_CHIPS * 16 * width
    padded = -(-flat.shape[0] // quantum) * quantum
    small_cm = jnp.pad(flat, (0, padded - flat.shape[0])).reshape(N_CHIPS, -1, width)

    pairs_b = pair_sums([dW_qkv, small_cm], [BF16, F32], "b")
    gots_b = scatter_to_chips(pairs_b, name="rs_scatter_to_chips")
    r_qkv, r_small, r_rows, r_up0, r_up1 = join_halves(
        chip_sums(pairs_b, gots_b, "b") + chip_sums(pairs_a, gots_a, "a"), name="rs_join_halves")
    grad_w_qkv = r_qkv[None]
    grad_w_o_attn = r_rows[0 * Dq:1 * Dq][None]
    grad_w_in_ssm = r_rows[1 * Dq:2 * Dq][None]
    grad_w_glu = r_rows[2 * Dq:3 * Dq][None]
    grad_w_o_ssm = r_rows[3 * Dq:4 * Dq][None]
    grad_w_down = r_rows[4 * Dq:].reshape(2, Fq, D)
    grad_w_up = jnp.stack([r_up0, r_up1])
    summed = all_gather8(r_small, name="gather_small_grads")[::2].reshape(-1)
    offs = [0]
    for s_ in sizes:
        offs.append(offs[-1] + s_)
    (s_nm, s_nf, s_bmod, s_lr, s_li, s_bbr, s_bbi, s_cre, s_cim, s_dsk, s_bglu, s_no, s_bfin, s_cw0, s_cw1, s_cb0,
     s_cb1) = [summed[offs[i]:offs[i + 1]] for i in range(len(sizes))]
    _, disc_vjp = jax.vjp(_ssm_discretize, a_re[0], a_im[0], log_dt[0], b_re[0], b_im[0])
    ga_re, ga_im, glog_dt, gb_re, gb_im = disc_vjp((s_lr.reshape(G, P), s_li.reshape(G, P), s_bbr.reshape(G, P, H),
                                                    s_bbi.reshape(G, P, H)))
    grad_norm_mix = s_nm.reshape(2, D)
    grad_norm_ffn = s_nf.reshape(2, D)
    grad_b_mod = s_bmod.reshape(2, 6 * D)
    grad_c_re = s_cre.reshape(1, G, H, P)
    grad_c_im = s_cim.reshape(1, G, H, P)
    grad_d_skip = lax.dynamic_slice(s_dsk, (chip * Dq,), (Dq,)).reshape(1, Dq)
    grad_b_glu = lax.dynamic_slice(s_bglu, (chip * Dq,), (Dq,)).reshape(1, Dq)
    cw_full = jnp.stack([s_cw0.reshape(3, F2), s_cw1.reshape(3, F2)])
    grad_conv_w = lax.dynamic_slice(cw_full, (0, 0, chip * (F2 // N_CHIPS)), (2, 3, F2 // N_CHIPS))
    grad_conv_b = jnp.stack([s_cb0, s_cb1])
    grad_norm_out = s_no
    grad_b_fin = s_bfin

    grads = dict(
        norm_mix=grad_norm_mix, norm_ffn=grad_norm_ffn, w_mod=grad_w_mod, b_mod=grad_b_mod, w_qkv=grad_w_qkv,
        w_o_attn=grad_w_o_attn, w_in_ssm=grad_w_in_ssm, a_re=ga_re[None], a_im=ga_im[None], log_dt=glog_dt[None],
        b_re=gb_re[None], b_im=gb_im[None], c_re=grad_c_re, c_im=grad_c_im, d_skip=grad_d_skip, w_glu=grad_w_glu,
        b_glu=grad_b_glu, w_o_ssm=grad_w_o_ssm, w_up=grad_w_up, conv_w=grad_conv_w, conv_b=grad_conv_b,
        w_down=grad_w_down, norm_out=grad_norm_out, w_fin=grad_w_fin, b_fin=grad_b_fin)
    weights = dict(
        norm_mix=norm_mix, norm_ffn=norm_ffn, w_mod=w_mod, b_mod=b_mod, w_qkv=w_qkv, w_o_attn=w_o_attn,
        w_in_ssm=w_in_ssm, a_re=a_re, a_im=a_im, log_dt=log_dt, b_re=b_re, b_im=b_im, c_re=c_re, c_im=c_im,
        d_skip=d_skip, w_glu=w_glu, b_glu=b_glu, w_o_ssm=w_o_ssm, w_up=w_up, conv_w=conv_w, conv_b=conv_b,
        w_down=w_down, norm_out=norm_out, w_fin=w_fin, b_fin=b_fin)
    m_in = dict(
        norm_mix=m_norm_mix, norm_ffn=m_norm_ffn, w_mod=m_w_mod, b_mod=m_b_mod, w_qkv=m_w_qkv, w_o_attn=m_w_o_attn,
        w_in_ssm=m_w_in_ssm, a_re=m_a_re, a_im=m_a_im, log_dt=m_log_dt, b_re=m_b_re, b_im=m_b_im, c_re=m_c_re,
        c_im=m_c_im, d_skip=m_d_skip, w_glu=m_w_glu, b_glu=m_b_glu, w_o_ssm=m_w_o_ssm, w_up=m_w_up, conv_w=m_conv_w,
        conv_b=m_conv_b, w_down=m_w_down, norm_out=m_norm_out, w_fin=m_w_fin, b_fin=m_b_fin)
    v_in = dict(
        norm_mix=v_norm_mix, norm_ffn=v_norm_ffn, w_mod=v_w_mod, b_mod=v_b_mod, w_qkv=v_w_qkv, w_o_attn=v_w_o_attn,
        w_in_ssm=v_w_in_ssm, a_re=v_a_re, a_im=v_a_im, log_dt=v_log_dt, b_re=v_b_re, b_im=v_b_im, c_re=v_c_re,
        c_im=v_c_im, d_skip=v_d_skip, w_glu=v_w_glu, b_glu=v_b_glu, w_o_ssm=v_w_o_ssm, w_up=v_w_up, conv_w=v_conv_w,
        conv_b=v_conv_b, w_down=v_w_down, norm_out=v_norm_out, w_fin=v_w_fin, b_fin=v_b_fin)
    names = list(weights)
    for n_ in names:
        grads[n_] = grads[n_].reshape(weights[n_].shape)

    big = ("w_mod", "w_qkv", "w_o_attn", "w_in_ssm", "w_glu", "w_o_ssm", "w_up", "w_down", "w_fin")
    delta, new_m, new_v = {}, {}, {}
    for n_ in big:
        shp = weights[n_].shape
        two_d = lambda a: a.reshape(-1, shp[-1])
        d_, m_, v_ = adamw(two_d(weights[n_]), two_d(grads[n_]), two_d(m_in[n_]), two_d(v_in[n_]), name=f"adamw_{n_}")
        delta[n_], new_m[n_], new_v[n_] = d_.reshape(shp), m_.reshape(shp), v_.reshape(shp)
    rest = [n_ for n_ in names if n_ not in big]
    small_out = adamw_many([weights[n_] for n_ in rest], [grads[n_] for n_ in rest], [m_in[n_] for n_ in rest],
                           [v_in[n_] for n_ in rest], name="adamw_small")
    for n_, (d_, m_, v_) in zip(rest, small_out):
        delta[n_], new_m[n_], new_v[n_] = d_, m_, v_

    return (loss, grad_x.reshape(B, S, D), *[grads[n_] for n_ in names], *[delta[n_] for n_ in names],
            *[new_m[n_] for n_ in names], *[new_v[n_] for n_ in names])
```

```python
import math

import jax
import jax.numpy as jnp
from jax import lax
from jax.experimental import pallas as pl
from jax.experimental.pallas import tpu as pltpu

F32 = jnp.float32
BF16 = jnp.bfloat16
MESH = pl.DeviceIdType.MESH

HEAD_DIM = 64
SSM_GROUP = 16
STATE = 64
GROUPS_PER_BLOCK = 8
SEGMENTS = 16
SCAN_UNROLL = 4
EPS = 1e-6
ADAM_LR = 0.001
ADAM_B1 = 0.9
ADAM_B2 = 0.999
ADAM_EPS = 1e-08
ADAM_WD = 0.01
ADAM_STEP = 10
N_CHIPS = 4
N_DEV = 8
V7X_VMEM_LIMIT = 56 * 1024 * 1024
ATT_BLOCK = 128
ATT_HEADS = 8
ATT_HEADS_BWD = 8


def _tile(n, prefs):
    for p in prefs:
        if n % p == 0:
            return p
    return n


def _params(sem, vmem=V7X_VMEM_LIMIT):
    return pltpu.CompilerParams(dimension_semantics=sem, vmem_limit_bytes=vmem)


def matmul(a, b, *, ta=False, tb=False, bias=None, out_dtype=F32, b_chips=False, out_chips=False, name):
    a_parts = a.shape[0] if a.ndim == 3 else 1
    if a_parts > 1:
        assert not ta
        M, K = a.shape[1], a_parts * a.shape[2]
    elif ta:
        K, M = a.shape
    else:
        M, K = a.shape
    b_parts = b.shape[0] if b_chips else 1
    b_rows, b_cols = (b.shape[1], b_parts * b.shape[2]) if b_chips else b.shape
    if tb:
        N, Kb = b_rows, b_cols
    else:
        Kb, N = b_rows, b_cols
    assert K == Kb, (a.shape, b.shape, ta, tb)
    n_cut = math.gcd(N // (N_CHIPS if out_chips else 1), N // (b_parts if not tb else 1))
    k_cut = math.gcd(K // (b_parts if tb else 1), K // a_parts)
    tm = _tile(M, (1024, 1408, 512, 256, 128))
    tn = _tile(n_cut, (1024, 1408, 768, 512, 256, 128))
    tk = k_cut if k_cut <= 2816 else _tile(k_cut, (1024, 512, 256, 128))
    nk = K // tk
    npc = N // N_CHIPS // tn
    npb = N // b_parts // tn
    kpb = K // b_parts // tk
    kpa = K // a_parts // tk
    dims = (((0,) if ta else (1,), (1,) if tb else (0,)), ((), ()))

    def body(*refs):
        a_ref, b_ref = refs[:2]
        bias_ref = refs[2] if bias is not None else None
        o_ref = refs[-2] if nk > 1 else refs[-1]

        def finish(r):
            if bias_ref is not None:
                r = r + bias_ref[...]
            o_ref[...] = r.astype(o_ref.dtype)

        prod = lax.dot_general(a_ref[...].astype(BF16), b_ref[...].astype(BF16), dims, preferred_element_type=F32)
        if nk == 1:
            finish(prod)
            return
        acc_ref = refs[-1]
        k = pl.program_id(2)

        @pl.when(k == 0)
        def _():
            acc_ref[...] = prod

        @pl.when(k > 0)
        def _():
            acc_ref[...] += prod

        @pl.when(k == nk - 1)
        def _():
            finish(acc_ref[...])

    if a_parts > 1:
        a_spec = pl.BlockSpec((None, tm, tk), lambda i, j, k: (lax.div(k, kpa), i, lax.rem(k, kpa)))
    else:
        a_spec = pl.BlockSpec((tk, tm), lambda i, j, k: (k, i)) if ta else pl.BlockSpec((tm, tk), lambda i, j, k: (i, k))
    if not b_chips:
        b_spec = pl.BlockSpec((tn, tk), lambda i, j, k: (j, k)) if tb else pl.BlockSpec((tk, tn), lambda i, j, k: (k, j))
    elif tb:
        b_spec = pl.BlockSpec((None, tn, tk), lambda i, j, k: (lax.div(k, kpb), j, lax.rem(k, kpb)))
    else:
        b_spec = pl.BlockSpec((None, tk, tn), lambda i, j, k: (lax.div(j, npb), k, lax.rem(j, npb)))
    in_specs = [a_spec, b_spec]
    args = [a, b]
    if bias is not None:
        in_specs.append(pl.BlockSpec((1, tn), lambda i, j, k: (0, j)))
        args.append(bias.reshape(1, N).astype(F32))
    if out_chips:
        out_shape = jax.ShapeDtypeStruct((N_CHIPS, M, N // N_CHIPS), out_dtype)
        out_spec = pl.BlockSpec((None, tm, tn), lambda i, j, k: (lax.div(j, npc), i, lax.rem(j, npc)))
    else:
        out_shape = jax.ShapeDtypeStruct((M, N), out_dtype)
        out_spec = pl.BlockSpec((tm, tn), lambda i, j, k: (i, j))
    return pl.pallas_call(
        body, name=name,
        out_shape=out_shape,
        grid=(M // tm, N // tn, nk),
        in_specs=in_specs,
        out_specs=out_spec,
        scratch_shapes=[pltpu.VMEM((tm, tn), F32)] if nk > 1 else [],
        compiler_params=_params(("parallel", "parallel", "arbitrary")),
    )(*args)


def rowwise(fn, tiled, per_seq, glob, out_tiled, out_seq, *, B, S, name, rows=512):
    tm = _tile(S, (rows, 128, 64, 32, 16, 8))
    nt = S // tm
    n_in = len(tiled) + len(per_seq) + len(glob)
    n_ot = len(out_tiled)

    def body(*refs):
        ins = refs[:n_in]
        outs = refs[n_in:]
        vals = fn(*[r[...] for r in ins])
        if not isinstance(vals, (tuple, list)):
            vals = (vals,)
        assert len(vals) == len(outs), (name, len(vals), len(outs))
        for o_ref, v in zip(outs[:n_ot], vals[:n_ot]):
            o_ref[...] = v.astype(o_ref.dtype)
        t = pl.program_id(1)
        for o_ref, v in zip(outs[n_ot:], vals[n_ot:]):
            def first(o_ref=o_ref, v=v):
                o_ref[...] = v.astype(F32)

            def later(o_ref=o_ref, v=v):
                o_ref[...] += v.astype(F32)

            pl.when(t == 0)(first)
            pl.when(t > 0)(later)

    in_specs = [pl.BlockSpec((tm, a.shape[1]), lambda b, t: (b * nt + t, 0)) for a in tiled]
    in_specs += [pl.BlockSpec((None, 1, a.shape[1]), lambda b, t: (b, 0, 0)) for a in per_seq]
    in_specs += [pl.BlockSpec(a.shape, lambda b, t: (0,) * a.ndim) for a in glob]
    out_shape = [jax.ShapeDtypeStruct((B * S, w), dt) for w, dt in out_tiled]
    out_shape += [jax.ShapeDtypeStruct((B, 1, w), F32) for w in out_seq]
    out_specs = [pl.BlockSpec((tm, w), lambda b, t: (b * nt + t, 0)) for w, _ in out_tiled]
    out_specs += [pl.BlockSpec((None, 1, w), lambda b, t: (b, 0, 0)) for w in out_seq]
    res = pl.pallas_call(
        body, name=name, out_shape=out_shape, grid=(B, nt), in_specs=in_specs, out_specs=out_specs,
        compiler_params=_params(("parallel", "arbitrary")),
    )(*tiled, *[a.reshape(B, 1, a.shape[1]) for a in per_seq], *glob)
    res = list(res)
    for i in range(n_ot, len(res)):
        res[i] = res[i].reshape(B, res[i].shape[-1])
    return res


def _rms(x):
    r = lax.rsqrt(jnp.mean(x * x, axis=-1, keepdims=True) + EPS)
    return x * r, r


def norm_mod_fwd(x, g, sh, sc, *, B, S, name):
    def fn(x, sh, sc, g):
        xn, _ = _rms(x)
        return (xn * g) * (1.0 + sc) + sh

    return rowwise(fn, [x], [sh, sc], [g.reshape(1, -1)], [(x.shape[1], BF16)], [], B=B, S=S, name=name)[0]


def _norm_mod_bwd_math(dh, x, sc, g):
    xn, r = _rms(x)
    y = xn * g
    dy = dh * (1.0 + sc)
    dxn = dy * g
    dx = r * (dxn - xn * jnp.mean(dxn * xn, axis=-1, keepdims=True))
    dsh = jnp.sum(dh, axis=0, keepdims=True)
    dsc = jnp.sum(dh * y, axis=0, keepdims=True)
    dg = jnp.sum(dy * xn, axis=0, keepdims=True)
    return dx, dsh, dsc, dg


def norm_mod_bwd(dh, x, dres, g, sc, *, B, S, name):
    D = x.shape[1]

    def fn(dh, x, dres, sc, g):
        dx, dsh, dsc, dg = _norm_mod_bwd_math(dh.astype(F32), x, sc, g)
        return dres + dx, dsh, dsc, dg

    return rowwise(fn, [dh, x, dres], [sc], [g.reshape(1, -1)], [(D, F32)], [D, D, D], B=B, S=S, name=name)


def norm_mod_bwd_gate(dh, x, dres, g, sc, y_prev, gate_prev, *, B, S, name):
    D = x.shape[1]

    def fn(dh, x, dres, y, sc, gate, g):
        dx, dsh, dsc, dg = _norm_mod_bwd_math(dh.astype(F32), x, sc, g)
        dx = dres + dx
        return dx, gate * dx, dsh, dsc, dg, jnp.sum(dx * y, axis=0, keepdims=True)

    return rowwise(fn, [dh, x, dres, y_prev], [sc, gate_prev], [g.reshape(1, -1)], [(D, F32), (D, BF16)],
                   [D, D, D, D], B=B, S=S, name=name)


def gate_res_fwd(x, y, gate, *, B, S, name):
    return rowwise(lambda x, y, g: x + g * y, [x, y], [gate], [], [(x.shape[1], F32)], [], B=B, S=S, name=name)[0]


def res_norm_mod_fwd(x, y, gate, g, sh, sc, *, B, S, name):
    D = x.shape[1]

    def fn(x, y, gate, sh, sc, g):
        x = x + gate * y
        xn, _ = _rms(x)
        return x, (xn * g) * (1.0 + sc) + sh

    return rowwise(fn, [x, y], [gate, sh, sc], [g.reshape(1, -1)], [(D, F32), (D, BF16)], [], B=B, S=S, name=name)


def final_loss(x, tgt, g, sh, sc, y_prev, gate_prev, *, B, S, name):
    D = x.shape[1]

    def fn(x, tgt, y_prev, sh, sc, gate, g):
        xn, _ = _rms(x)
        y = (xn * g) * (1.0 + sc) + sh
        err = y - tgt
        loss = 0.5 * jnp.sum(err * err, axis=0, keepdims=True) * (1.0 / D)
        dx, dsh, dsc, dg = _norm_mod_bwd_math(err * (1.0 / D), x, sc, g)
        return dx, gate * dx, loss, dsh, dsc, dg, jnp.sum(dx * y_prev, axis=0, keepdims=True)

    return rowwise(fn, [x, tgt, y_prev], [sh, sc, gate_prev], [g.reshape(1, -1)], [(D, F32), (D, BF16)],
                   [D, D, D, D, D], B=B, S=S, name=name)


def _gelu(y):
    c0 = math.sqrt(2.0 / math.pi)
    t = jnp.tanh(c0 * (y + 0.044715 * (y * y * y)))
    return 0.5 * y * (1.0 + t), t


def _sigmoid(s):
    return 1.0 / (1.0 + jnp.exp(-s))


def gelu_fwd(y, *, B, S, name):
    return rowwise(lambda y: _gelu(y)[0], [y], [], [], [(y.shape[1], BF16)], [], B=B, S=S, name=name)[0]


def glu_fwd(y, s, *, B, S, name):
    return rowwise(lambda y, s: _gelu(y)[0] * _sigmoid(s), [y, s], [], [], [(y.shape[1], BF16)], [], B=B, S=S,
                   name=name)[0]


def glu_bwd1(y, s, dg, *, B, S, name):
    D = y.shape[1]

    def fn(y, s, dg):
        z = _gelu(y)[0]
        sig = _sigmoid(s)
        ds = dg * z * sig * (1.0 - sig)
        return ds, dg * sig, jnp.sum(ds, axis=0, keepdims=True)

    return rowwise(fn, [y, s, dg], [], [], [(D, BF16), (D, F32)], [D], B=B, S=S, name=name)


def glu_bwd2(y, dz1, dz2, *, B, S, name):
    D = y.shape[1]
    c0 = math.sqrt(2.0 / math.pi)

    def fn(y, dz1, dz2):
        _, t = _gelu(y)
        dgelu = 0.5 * (1.0 + t) + 0.5 * y * (1.0 - t * t) * c0 * (1.0 + 3.0 * 0.044715 * y * y)
        return (dz1 + dz2) * dgelu

    return rowwise(fn, [y, dz1, dz2], [], [], [(D, F32)], [], B=B, S=S, name=name)[0]


def silu_rows(c, *, name):
    R, W = c.shape
    return rowwise(lambda c: c * _sigmoid(c), [c], [], [], [(W, F32)], [], B=1, S=R, name=name)[0]


def conv_gate_fwd(up, cw, cb, *, B, S, name):
    F = up.shape[1] // 2
    tn = _tile(F, (256, 128))
    nF = F // tn

    def body(g_ref, v_ref, wg_ref, wv_ref, bg_ref, bv_ref, o_ref):
        rows = lax.broadcasted_iota(jnp.int32, (S, tn), 0)

        def conv(x, w_ref):
            x1 = jnp.where(rows >= 1, pltpu.roll(x, 1, 0), 0.0)
            x2 = jnp.where(rows >= 2, pltpu.roll(x, 2, 0), 0.0)
            return w_ref[2:3, :] * x + w_ref[1:2, :] * x1 + w_ref[0:1, :] * x2

        gc = conv(g_ref[...].astype(F32), wg_ref) + bg_ref[...]
        vc = conv(v_ref[...].astype(F32), wv_ref) + bv_ref[...]
        o_ref[...] = (gc * _sigmoid(gc) * vc).astype(o_ref.dtype)

    def cols(off):
        return pl.BlockSpec((S, tn), lambda b, j: (b, j + off))

    def vec(rows, off):
        return pl.BlockSpec((rows, tn), lambda b, j: (0, j + off))

    return pl.pallas_call(
        body, name=name, out_shape=jax.ShapeDtypeStruct((B * S, F), BF16), grid=(B, nF),
        in_specs=[cols(0), cols(nF), vec(3, 0), vec(3, nF), vec(1, 0), vec(1, nF)],
        out_specs=pl.BlockSpec((S, tn), lambda b, j: (b, j)),
        compiler_params=_params(("parallel", "parallel")),
    )(up, up, cw, cw, cb, cb)


def conv_gate_bwd(up, dact, cw, cb, *, B, S, name):
    F = up.shape[1] // 2
    tn = _tile(F, (256, 128))
    nF = F // tn

    def body(g_ref, v_ref, da_ref, wg_ref, wv_ref, bg_ref, bv_ref, o_ref, dw_ref, db_ref):
        rows = lax.broadcasted_iota(jnp.int32, (S, tn), 0)

        def earlier(x, k):
            return jnp.where(rows >= k, pltpu.roll(x, k, 0), 0.0)

        def later(x, k):
            return jnp.where(rows < S - k, pltpu.roll(x, S - k, 0), 0.0)

        def conv(x, w_ref):
            x1, x2 = earlier(x, 1), earlier(x, 2)
            return w_ref[2:3, :] * x + w_ref[1:2, :] * x1 + w_ref[0:1, :] * x2, x1, x2

        def back(d, x, x1, x2, w_ref, half):
            o_ref[half] = (w_ref[2:3, :] * d + w_ref[1:2, :] * later(d, 1) + w_ref[0:1, :] * later(d, 2)
                           ).astype(o_ref.dtype)
            dw_ref[half] = jnp.concatenate([jnp.sum(d * x2, axis=0, keepdims=True),
                                            jnp.sum(d * x1, axis=0, keepdims=True),
                                            jnp.sum(d * x, axis=0, keepdims=True)], axis=0)
            return jnp.sum(d, axis=0, keepdims=True)

        g, v, da = g_ref[...].astype(F32), v_ref[...].astype(F32), da_ref[...].astype(F32)
        gc, g1, g2 = conv(g, wg_ref)
        vc, v1, v2 = conv(v, wv_ref)
        gc = gc + bg_ref[...]
        vc = vc + bv_ref[...]
        sig = _sigmoid(gc)
        dg = da * vc * (sig * (1.0 + gc * (1.0 - sig)))
        dv = da * (gc * sig)
        db_ref[...] = jnp.concatenate([back(dg, g, g1, g2, wg_ref, 0), back(dv, v, v1, v2, wv_ref, 1)], axis=0)

    def cols(off):
        return pl.BlockSpec((S, tn), lambda b, j: (b, j + off))

    def vec(rows, off):
        return pl.BlockSpec((rows, tn), lambda b, j: (0, j + off))

    return pl.pallas_call(
        body, name=name,
        out_shape=[jax.ShapeDtypeStruct((2, B * S, F), BF16), jax.ShapeDtypeStruct((B, 2, 3, F), F32),
                   jax.ShapeDtypeStruct((B, 2, F), F32)],
        grid=(B, nF),
        in_specs=[cols(0), cols(nF), cols(0), vec(3, 0), vec(3, nF), vec(1, 0), vec(1, nF)],
        out_specs=[pl.BlockSpec((2, S, tn), lambda b, j: (0, b, j)),
                   pl.BlockSpec((None, 2, 3, tn), lambda b, j: (b, 0, 0, j)),
                   pl.BlockSpec((None, 2, tn), lambda b, j: (b, 0, j))],
        compiler_params=_params(("parallel", "parallel")),
    )(up, up, dact, cw, cw, cb, cb)


MASKED_LOG = -1e30


def _split2(x):
    bits = lax.bitcast_convert_type(x, jnp.uint32) & jnp.uint32(0xFFFF0000)
    hi = lax.bitcast_convert_type(bits, F32)
    return hi.astype(BF16), (x - hi).astype(BF16)


def _nt(a, b):
    return lax.dot_general(a, b, (((1,), (1,)), ((), ())), preferred_element_type=F32)


def _tn(a, b):
    return lax.dot_general(a, b, (((0,), (0,)), ((), ())), preferred_element_type=F32)


def _att_scores(q, k, mask, prescaled=False):
    z = _nt(q, k)
    if not prescaled:
        z = z * (HEAD_DIM ** -0.5)
    e = jnp.exp(-jnp.abs(z))
    sp = jnp.log(1.0 + e)
    lb = jnp.minimum(z, 0.0) - sp
    l1 = lb - z
    if mask is not None:
        lb = jnp.where(mask, lb, MASKED_LOG)
        l1 = jnp.where(mask, l1, 0.0)
    return z, lb, l1, e


def _col_to_row(col, eye):
    return jnp.sum(jnp.where(eye, col, 0.0), axis=0, keepdims=True)


def _row_to_col(row, eye):
    return jnp.sum(jnp.where(eye, row, 0.0), axis=1, keepdims=True)


def _wide_consts(T, W):
    r = lax.broadcasted_iota(jnp.int32, (W, W), 0)
    c = lax.broadcasted_iota(jnp.int32, (W, W), 1)
    two = lambda m: jnp.concatenate([m.astype(BF16)] * 2, axis=0)
    qrow = lax.broadcasted_iota(jnp.int32, (T, W), 0)
    kcol = lax.broadcasted_iota(jnp.int32, (T, W), 1)
    er = lax.broadcasted_iota(jnp.int32, (T, T), 0)
    ec = lax.broadcasted_iota(jnp.int32, (T, T), 1)
    return two(r > c), two(r <= c), two(r < c), qrow, kcol, er == ec


def _pair_masks(x, first):
    zero = jnp.zeros_like(x)
    return jnp.where(first, x, zero), jnp.where(first, zero, x)


def attn_fwd_pairs(qkv, shards=(), slots=(), *, B, S, name):
    D = qkv.shape[1] // 3
    H = D // HEAD_DIM
    T = ATT_BLOCK
    W = 2 * T
    nq = S // T
    G = _tile(H, (ATT_HEADS, 2))
    P = G // 2
    LW = 2 * HEAD_DIM * P
    nsec = D // LW
    n = len(shards)
    n_steps = B * nsec

    def body(*refs):
        q_ref, k_ref, v_ref = refs[:3]
        o_ref, l_ref = refs[3 + 2 * n:5 + 2 * n]
        step_id = pl.program_id(0) * nsec + pl.program_id(1)
        if n:
            gather = _ShardGather(refs[3:3 + n], refs[5 + 2 * n:5 + 3 * n], *refs[5 + 3 * n:])
            pl.when(step_id == 0)(gather.send)
            pl.when(step_id == n_steps - 1)(gather.forward)
        later2, _, _, qrow, kcol, eye = _wide_consts(T, W)
        blk = lax.broadcasted_iota(jnp.int32, (nq, T), 0)
        first_q = lax.broadcasted_iota(jnp.int32, (T, 2 * HEAD_DIM), 1) < HEAD_DIM
        first_k = lax.broadcasted_iota(jnp.int32, (W, 2 * HEAD_DIM), 1) < HEAD_DIM

        def lanes(j):
            return slice(j * 2 * HEAD_DIM, (j + 1) * 2 * HEAD_DIM)

        def step(qms, k0, st, mask):
            accs, runs = st
            parts, lbs, sums = [], [], []
            for g in range(G):
                _, lb, l1, _ = _att_scores(qms[g], k_ref[pl.ds(k0, W), lanes(g // 2)], mask, prescaled=True)
                parts.append(jnp.concatenate(_split2(l1), axis=1))
                lbs.append(lb)
                sums.append(jnp.sum(l1, axis=1, keepdims=True))
            suf = jnp.dot(jnp.concatenate(parts, axis=0), later2, preferred_element_type=F32)
            new_accs, new_runs = [], []
            for j in range(P):
                vms = _pair_masks(v_ref[pl.ds(k0, W), lanes(j)], first_k)
                acc = accs[j]
                for h in range(2):
                    g = 2 * j + h
                    w = jnp.exp(lbs[g] + suf[g * T:(g + 1) * T] + runs[g])
                    acc = acc + jnp.dot(w.astype(BF16), vms[h], preferred_element_type=F32)
                    new_runs.append(runs[g] + sums[g])
                new_accs.append(acc)
            return tuple(new_accs), tuple(new_runs)

        def qblock(i, totals):
            q0 = pl.multiple_of(i * T, T)
            qms = []
            for j in range(P):
                qms.extend(_pair_masks(q_ref[pl.ds(q0, T), lanes(j)] * (HEAD_DIM ** -0.5), first_q))
            half = jnp.right_shift(i, 1)
            last = half * W
            k_last = pl.multiple_of(last, W)
            mask = (k_last + kcol) < (q0 + qrow)
            st = (tuple(jnp.zeros((T, 2 * HEAD_DIM), F32) for _ in range(P)),
                  tuple(jnp.zeros((T, 1), F32) for _ in range(G)))
            st = step(qms, k_last, st, mask)

            def kblock(jj, st):
                return step(qms, pl.multiple_of(last - jj * W, W), st, None)

            accs, runs = lax.fori_loop(1, half + 1, kblock, st)
            for j in range(P):
                o_ref[pl.ds(q0, T), lanes(j)] = accs[j].astype(o_ref.dtype)
            return tuple(jnp.where(blk == i, _col_to_row(runs[g], eye), totals[g]) for g in range(G))

        totals = lax.fori_loop(0, nq, qblock, tuple(jnp.zeros((nq, T), F32) for _ in range(G)))
        for g in range(G):
            l_ref[g] = totals[g]
        if n:
            pl.when(step_id == n_steps - 1)(gather.finish)

    def cols(section):
        return pl.BlockSpec((S, LW), lambda b, h: (b, section * nsec + h))

    lspec = pl.BlockSpec((None, G, nq, T), lambda b, h: (b, h, 0, 0))
    dma = pltpu.SemaphoreType.DMA
    return pl.pallas_call(
        body, name=name,
        out_shape=[jax.ShapeDtypeStruct((B * S, D), BF16), jax.ShapeDtypeStruct((B, H, nq, T), F32)]
        + [jax.ShapeDtypeStruct(s.shape, s.dtype) for s in slots],
        grid=(B, nsec), in_specs=[cols(0), cols(1), cols(2)] + _any_specs(2 * n),
        out_specs=[cols(0), lspec] + _any_specs(n),
        input_output_aliases={3 + n + i: 2 + i for i in range(n)},
        scratch_shapes=[dma((3 * n,))] * 4 if n else [],
        compiler_params=_params(("arbitrary", "arbitrary")),
    )(qkv, qkv, qkv, *shards, *slots)


def attn_bwd_pairs(qkv, ltot, do, partials=(), *, B, S, name):
    D = qkv.shape[1] // 3
    H = D // HEAD_DIM
    T = ATT_BLOCK
    W = 2 * T
    nq = S // T
    scale = HEAD_DIM ** -0.5
    G = _tile(H, (ATT_HEADS_BWD, 2))
    P = G // 2
    LW = 2 * HEAD_DIM * P
    nsec = D // LW
    n = len(partials)
    n_steps = B * nsec

    def body(*refs):
        q_ref, k_ref, v_ref, l_ref, do_ref = refs[:5]
        d_ref = refs[5 + n]
        dk_acc, dv_acc = refs[6 + 2 * n:8 + 2 * n]
        step_id = pl.program_id(0) * nsec + pl.program_id(1)
        if n:
            scatter = _ChipScatter(refs[5:5 + n], refs[6 + n:6 + 2 * n], *refs[8 + 2 * n:])
            pl.when(step_id == 0)(scatter.send)
        _, upto2, before2, qrow, kcol, eye = _wide_consts(T, W)
        blk = lax.broadcasted_iota(jnp.int32, (nq, T), 0)
        first_q = lax.broadcasted_iota(jnp.int32, (T, 2 * HEAD_DIM), 1) < HEAD_DIM
        first_k = lax.broadcasted_iota(jnp.int32, (W, 2 * HEAD_DIM), 1) < HEAD_DIM
        dk_acc[...] = jnp.zeros_like(dk_acc)
        dv_acc[...] = jnp.zeros_like(dv_acc)

        def lanes(j):
            return slice(j * 2 * HEAD_DIM, (j + 1) * 2 * HEAD_DIM)

        def step(qms, doms, tots, k0, st, mask):
            dqs, runs_l, runs_d = st
            sc = []
            for g in range(G):
                z, lb, l1, _ = _att_scores(qms[g], k_ref[pl.ds(k0, W), lanes(g // 2)], mask, prescaled=True)
                beta = 0.5 * jnp.tanh(0.5 * z) + 0.5
                omb = 1.0 - beta
                if mask is not None:
                    beta = jnp.where(mask, beta, 0.0)
                dw = _nt(doms[g], v_ref[pl.ds(k0, W), lanes(g // 2)])
                sc.append((lb, jnp.concatenate(_split2(l1), axis=1), jnp.sum(l1, axis=1, keepdims=True), dw, beta, omb))
            pre = jnp.dot(jnp.concatenate([s[1] for s in sc], axis=0), upto2, preferred_element_type=F32)
            dlws = []
            for j in range(P):
                dv = None
                for h in range(2):
                    g = 2 * j + h
                    w = jnp.exp(sc[g][0] + (tots[g] - (pre[g * T:(g + 1) * T] + runs_l[g])))
                    t = _tn(w.astype(BF16), doms[g])
                    dv = t if dv is None else dv + t
                    dlws.append(sc[g][3] * w)
                dv_acc[j, pl.ds(k0, W), :] += dv
            pre_d = jnp.dot(jnp.concatenate([jnp.concatenate(_split2(d), axis=1) for d in dlws], axis=0), before2,
                            preferred_element_type=F32)
            new_dqs, new_l, new_d = [], [], []
            for j in range(P):
                kms = _pair_masks(k_ref[pl.ds(k0, W), lanes(j)], first_k)
                dq, dk = dqs[j], None
                for h in range(2):
                    g = 2 * j + h
                    _, _, rowsum, _, beta, omb = sc[g]
                    dl1 = pre_d[g * T:(g + 1) * T] + runs_d[g]
                    dz = (dlws[g] * omb - dl1 * beta).astype(BF16)
                    dq = dq + jnp.dot(dz, kms[h], preferred_element_type=F32)
                    t = _tn(dz, qms[g])
                    dk = t if dk is None else dk + t
                    new_l.append(runs_l[g] + rowsum)
                    new_d.append(runs_d[g] + jnp.sum(dlws[g], axis=1, keepdims=True))
                dk_acc[j, pl.ds(k0, W), :] += dk
                new_dqs.append(dq)
            return tuple(new_dqs), tuple(new_l), tuple(new_d)

        def qblock(i, carry0):
            q0 = pl.multiple_of(i * T, T)
            qms, doms = [], []
            for j in range(P):
                qms.extend(_pair_masks(q_ref[pl.ds(q0, T), lanes(j)] * scale, first_q))
                doms.extend(_pair_masks(do_ref[pl.ds(q0, T), lanes(j)], first_q))
            tots = [_row_to_col(jnp.sum(jnp.where(blk == i, l_ref[g], 0.0), axis=0, keepdims=True), eye)
                    for g in range(G)]
            z1 = tuple(jnp.zeros((T, 1), F32) for _ in range(G))
            st = (tuple(jnp.zeros((T, 2 * HEAD_DIM), F32) for _ in range(P)), z1, z1)

            def kblock(j, st):
                return step(qms, doms, tots, pl.multiple_of(j * W, W), st, None)

            half = jnp.right_shift(i, 1)
            st = lax.fori_loop(0, half, kblock, st)
            k_last = pl.multiple_of(half * W, W)
            dqs, _, _ = step(qms, doms, tots, k_last, st, (k_last + kcol) < (q0 + qrow))
            for j in range(P):
                d_ref[0, pl.ds(q0, T), lanes(j)] = (dqs[j] * scale).astype(d_ref.dtype)
            return carry0

        lax.fori_loop(0, nq, qblock, 0)
        for j in range(P):
            d_ref[1, :, lanes(j)] = dk_acc[j].astype(d_ref.dtype)
            d_ref[2, :, lanes(j)] = dv_acc[j].astype(d_ref.dtype)
        if n:
            pl.when(step_id == n_steps - 1)(scatter.finish)

    def cols(section):
        return pl.BlockSpec((S, LW), lambda b, h: (b, section * nsec + h))

    lspec = pl.BlockSpec((None, G, nq, T), lambda b, h: (b, h, 0, 0))
    dma = pltpu.SemaphoreType.DMA
    return pl.pallas_call(
        body, name=name,
        out_shape=[jax.ShapeDtypeStruct((3, B * S, D), BF16)]
        + [jax.ShapeDtypeStruct((N_CHIPS - 1,) + a.shape[1:], a.dtype) for a in partials],
        grid=(B, nsec),
        in_specs=[cols(0), cols(1), cols(2), lspec, cols(0)] + _any_specs(n),
        out_specs=[pl.BlockSpec((3, S, LW), lambda b, h: (0, b, h))] + _any_specs(n),
        scratch_shapes=[pltpu.VMEM((P, S, 2 * HEAD_DIM), F32), pltpu.VMEM((P, S, 2 * HEAD_DIM), F32)]
        + ([dma((3 * n,)), dma((3 * n,))] if n else []),
        compiler_params=_params(("arbitrary", "arbitrary")),
    )(qkv, qkv, qkv, ltot, do, *partials)


def _cmul(ar, ai, br, bi):
    return ar * br - ai * bi, ar * bi + ai * br


def _cpow(lr, li, n):
    rr, ri = None, None
    br, bi = lr, li
    while n:
        if n & 1:
            rr, ri = (br, bi) if rr is None else _cmul(rr, ri, br, bi)
        n >>= 1
        if n:
            br, bi = _cmul(br, bi, br, bi)
    return rr, ri


def _ssm_scan(sr, si, lr, li, n_steps, reverse):
    W = sr.shape[1]
    R = SEGMENTS
    lim = -li if reverse else li
    zero = jnp.zeros((R, W), F32)

    def row(k):
        i = (n_steps - 1 - k) if reverse else k
        return pl.multiple_of(i * R, R)

    def local(k, st):
        cr, ci = st
        r0 = row(k)
        pr, pi = _cmul(lr, lim, cr, ci)
        nr = pr + sr[pl.ds(r0, R), :]
        ni = pi + si[pl.ds(r0, R), :]
        sr[pl.ds(r0, R), :] = nr
        si[pl.ds(r0, R), :] = ni
        return nr, ni

    er, ei = lax.fori_loop(0, n_steps, local, (zero, zero), unroll=SCAN_UNROLL)
    lnr, lni = _cpow(lr, lim, n_steps)
    rows = lax.broadcasted_iota(jnp.int32, (R, W), 0)
    cr, ci = zero, zero
    for step in range(1, R):
        tr, ti = _cmul(lnr, lni, cr, ci)
        tr, ti = tr + er, ti + ei
        if reverse:
            seg = R - 1 - step
            tr, ti = pltpu.roll(tr, R - 1, 0), pltpu.roll(ti, R - 1, 0)
        else:
            seg = step
            tr, ti = pltpu.roll(tr, 1, 0), pltpu.roll(ti, 1, 0)
        cr = jnp.where(rows == seg, tr, cr)
        ci = jnp.where(rows == seg, ti, ci)

    def fix(k, st):
        pr, pi = st
        r0 = row(k)
        ar, ai = _cmul(pr, pi, cr, ci)
        sr[pl.ds(r0, R), :] += ar
        si[pl.ds(r0, R), :] += ai
        return _cmul(lr, lim, pr, pi)

    lax.fori_loop(0, n_steps, fix, (lr, lim), unroll=SCAN_UNROLL)
    return cr, ci


def _ssm_specs(S, W):
    CH = GROUPS_PER_BLOCK * SSM_GROUP
    return dict(
        rows=pl.BlockSpec((S, CH), lambda b, j: (b, j)),
        b=pl.BlockSpec((None, CH, W), lambda b, j: (j, 0, 0)),
        c=pl.BlockSpec((None, W, CH), lambda b, j: (j, 0, 0)),
        lam=pl.BlockSpec((None, SEGMENTS, W), lambda b, j: (j, 0, 0)),
        vec=pl.BlockSpec((1, CH), lambda b, j: (0, j)),
    )


def ssm_fwd(u, bre, bim, cre, cim, lr8, li8, dsk, *, B, S, name):
    D = u.shape[1]
    J, CH, W = bre.shape
    n_steps = S // SEGMENTS
    sp = _ssm_specs(S, W)

    def body(u_ref, bre_ref, bim_ref, cre_ref, cim_ref, lr_ref, li_ref, dsk_ref, y_ref, sr, si):
        u = u_ref[...]
        ub = u.astype(BF16)
        sr[...] = jnp.dot(ub, bre_ref[...], preferred_element_type=F32)
        si[...] = jnp.dot(ub, bim_ref[...], preferred_element_type=F32)
        _ssm_scan(sr, si, lr_ref[...], li_ref[...], n_steps, False)
        y = jnp.dot(sr[...].astype(BF16), cre_ref[...], preferred_element_type=F32)
        y = y - jnp.dot(si[...].astype(BF16), cim_ref[...], preferred_element_type=F32)
        y_ref[...] = y + dsk_ref[...] * u

    return pl.pallas_call(
        body, name=name, out_shape=jax.ShapeDtypeStruct((B * S, D), F32), grid=(B, J),
        in_specs=[sp["rows"], sp["b"], sp["b"], sp["c"], sp["c"], sp["lam"], sp["lam"], sp["vec"]],
        out_specs=sp["rows"],
        scratch_shapes=[pltpu.VMEM((S, W), F32), pltpu.VMEM((S, W), F32)],
        compiler_params=_params(("parallel", "parallel")),
    )(u, bre, bim, cre, cim, lr8, li8, dsk)


def ssm_bwd(u, dy, bre, bim, cre, cim, lr8, li8, dsk, *, B, S, name):
    D = u.shape[1]
    J, CH, W = bre.shape
    n_steps = S // SEGMENTS
    sp = _ssm_specs(S, W)

    def body(u_ref, dy_ref, bre_ref, bim_ref, cre_ref, cim_ref, lr_ref, li_ref, dsk_ref,
             du_ref, dbre_ref, dbim_ref, dcre_ref, dcim_ref, dlr_ref, dli_ref, ddsk_ref, sr, si, ar, ai):
        u = u_ref[...]
        dy = dy_ref[...]
        ub = u.astype(BF16)
        dyb = dy.astype(BF16)
        lr, li = lr_ref[...], li_ref[...]
        sr[...] = jnp.dot(ub, bre_ref[...], preferred_element_type=F32)
        si[...] = jnp.dot(ub, bim_ref[...], preferred_element_type=F32)
        cr, ci = _ssm_scan(sr, si, lr, li, n_steps, False)
        ar[...] = _nt(dyb, cre_ref[...])
        ai[...] = -_nt(dyb, cim_ref[...])
        _ssm_scan(ar, ai, lr, li, n_steps, True)

        def dlam(k, st):
            dr, di = st
            r0 = pl.multiple_of((k + 1) * SEGMENTS, SEGMENTS)
            p0 = pl.multiple_of(k * SEGMENTS, SEGMENTS)
            pr, pi = sr[pl.ds(p0, SEGMENTS), :], si[pl.ds(p0, SEGMENTS), :]
            xr, xi = ar[pl.ds(r0, SEGMENTS), :], ai[pl.ds(r0, SEGMENTS), :]
            return dr + pr * xr + pi * xi, di + pr * xi - pi * xr

        xr, xi = ar[0:SEGMENTS, :], ai[0:SEGMENTS, :]
        dr, di = lax.fori_loop(0, n_steps - 1, dlam, (cr * xr + ci * xi, cr * xi - ci * xr), unroll=SCAN_UNROLL)
        dlr_ref[...] = dr
        dli_ref[...] = di
        arb = ar[...].astype(BF16)
        aib = ai[...].astype(BF16)
        du_ref[...] = _nt(arb, bre_ref[...]) + _nt(aib, bim_ref[...]) + dsk_ref[...] * dy
        dbre_ref[...] = _tn(ub, arb)
        dbim_ref[...] = _tn(ub, aib)
        dcre_ref[...] = _tn(sr[...].astype(BF16), dyb)
        dcim_ref[...] = -_tn(si[...].astype(BF16), dyb)
        ddsk_ref[...] = jnp.sum(dy * u, axis=0, keepdims=True)

    def per(shape):
        return pl.BlockSpec((None, None) + shape, lambda b, j: (b, j, 0, 0))

    return pl.pallas_call(
        body, name=name,
        out_shape=[jax.ShapeDtypeStruct((B * S, D), F32),
                   jax.ShapeDtypeStruct((B, J, CH, W), F32), jax.ShapeDtypeStruct((B, J, CH, W), F32),
                   jax.ShapeDtypeStruct((B, J, W, CH), F32), jax.ShapeDtypeStruct((B, J, W, CH), F32),
                   jax.ShapeDtypeStruct((B, J, SEGMENTS, W), F32), jax.ShapeDtypeStruct((B, J, SEGMENTS, W), F32),
                   jax.ShapeDtypeStruct((B, J, 1, CH), F32)],
        grid=(B, J),
        in_specs=[sp["rows"], sp["rows"], sp["b"], sp["b"], sp["c"], sp["c"], sp["lam"], sp["lam"], sp["vec"]],
        out_specs=[sp["rows"], per((CH, W)), per((CH, W)), per((W, CH)), per((W, CH)), per((SEGMENTS, W)),
                   per((SEGMENTS, W)),
                   per((1, CH))],
        scratch_shapes=[pltpu.VMEM((S, W), F32)] * 4,
        compiler_params=_params(("parallel", "parallel")),
    )(u, dy, bre, bim, cre, cim, lr8, li8, dsk)


def _ssm_discretize(a_re, a_im, log_dt, b_re, b_im):
    dt = jnp.exp(log_dt)[:, None]
    er = jnp.exp(a_re * dt)
    lr = er * jnp.cos(a_im * dt)
    li = er * jnp.sin(a_im * dt)
    den = a_re * a_re + a_im * a_im
    fr = ((lr - 1.0) * a_re + li * a_im) / den
    fi = (li * a_re - (lr - 1.0) * a_im) / den
    bbr = fr[..., None] * b_re - fi[..., None] * b_im
    bbi = fr[..., None] * b_im + fi[..., None] * b_re
    return lr, li, bbr, bbi


def _block_diag_in(m):
    G, P, H = m.shape
    J = G // GROUPS_PER_BLOCK
    m = m.reshape(J, GROUPS_PER_BLOCK, P, H).transpose(0, 1, 3, 2)
    eye = jnp.eye(GROUPS_PER_BLOCK, dtype=m.dtype)
    out = m[:, :, :, None, :] * eye[None, :, None, :, None]
    return out.reshape(J, GROUPS_PER_BLOCK * H, GROUPS_PER_BLOCK * P)


def _block_diag_in_grad(d, G, P, H):
    J = G // GROUPS_PER_BLOCK
    d = d.reshape(J, GROUPS_PER_BLOCK, H, GROUPS_PER_BLOCK, P)
    idx = jnp.arange(GROUPS_PER_BLOCK)
    d = d[:, idx, :, idx, :]
    return d.transpose(1, 0, 3, 2).reshape(G, P, H)


def _block_diag_out(m):
    G, H, P = m.shape
    J = G // GROUPS_PER_BLOCK
    m = m.reshape(J, GROUPS_PER_BLOCK, H, P).transpose(0, 1, 3, 2)
    eye = jnp.eye(GROUPS_PER_BLOCK, dtype=m.dtype)
    out = m[:, :, :, None, :] * eye[None, :, None, :, None]
    return out.reshape(J, GROUPS_PER_BLOCK * P, GROUPS_PER_BLOCK * H)


def _block_diag_out_grad(d, G, H, P):
    J = G // GROUPS_PER_BLOCK
    d = d.reshape(J, GROUPS_PER_BLOCK, P, GROUPS_PER_BLOCK, H)
    idx = jnp.arange(GROUPS_PER_BLOCK)
    d = d[:, idx, :, idx, :]
    return d.transpose(1, 0, 3, 2).reshape(G, H, P)


def _interleave(a, B, S):
    L = S // SEGMENTS
    return a.reshape(B, SEGMENTS, L, a.shape[-1]).transpose(0, 2, 1, 3).reshape(B * S, a.shape[-1])


def _deinterleave(a, B, S):
    L = S // SEGMENTS
    return a.reshape(B, L, SEGMENTS, a.shape[-1]).transpose(0, 2, 1, 3).reshape(B * S, a.shape[-1])


def _adamw_math(w, g, m, v):
    m = ADAM_B1 * m + (1.0 - ADAM_B1) * g
    v = ADAM_B2 * v + (1.0 - ADAM_B2) * (g * g)
    m_hat = m / (1.0 - ADAM_B1 ** ADAM_STEP)
    v_hat = v / (1.0 - ADAM_B2 ** ADAM_STEP)
    delta = -ADAM_LR * (m_hat / (jnp.sqrt(v_hat) + ADAM_EPS) + ADAM_WD * w)
    return delta, m, v


def adamw(w, g, m, v, *, name):
    R, C = w.shape
    tr = _tile(R, (max(8, (1 << 18) // C // 8 * 8), 256, 128, 64, 32, 16, 8))

    def body(w_ref, g_ref, m_ref, v_ref, d_ref, nm_ref, nv_ref):
        d, nm, nv = _adamw_math(w_ref[...], g_ref[...], m_ref[...], v_ref[...])
        d_ref[...] = d
        nm_ref[...] = nm
        nv_ref[...] = nv

    spec = pl.BlockSpec((tr, C), lambda i: (i, 0))
    shp = jax.ShapeDtypeStruct((R, C), F32)
    return pl.pallas_call(
        body, name=name, out_shape=[shp, shp, shp], grid=(R // tr,), in_specs=[spec] * 4, out_specs=[spec] * 3,
        compiler_params=_params(("parallel",)),
    )(w, g, m, v)


def adamw_many(ws, gs, ms, vs, *, name):
    n = len(ws)
    at_least_2d = lambda a: a.reshape(1, -1) if a.ndim == 1 else a
    args = [at_least_2d(a) for group in (ws, gs, ms, vs) for a in group]

    def body(*refs):
        for i in range(n):
            d, nm, nv = _adamw_math(refs[i][...], refs[n + i][...], refs[2 * n + i][...], refs[3 * n + i][...])
            refs[4 * n + 3 * i][...] = d
            refs[4 * n + 3 * i + 1][...] = nm
            refs[4 * n + 3 * i + 2][...] = nv

    out = pl.pallas_call(
        body, name=name, out_shape=[jax.ShapeDtypeStruct(a.shape, F32) for a in args[:n] for _ in range(3)],
        in_specs=[pl.BlockSpec(memory_space=pltpu.VMEM) for _ in args],
        out_specs=[pl.BlockSpec(memory_space=pltpu.VMEM) for _ in range(3 * n)],
        compiler_params=pltpu.CompilerParams(vmem_limit_bytes=V7X_VMEM_LIMIT),
    )(*args)
    return [tuple(o.reshape(w.shape) for o in out[3 * i:3 * i + 3]) for i, w in enumerate(ws)]


def _any_specs(n):
    return [pl.BlockSpec(memory_space=pl.ANY) for _ in range(n)]


def _coords():
    return lax.axis_index("x"), lax.axis_index("y"), lax.axis_index("c")


def _flip(v, bit):
    return (v + bit) % 2


def all_gather8(a, *, name):
    shape = a.shape

    def body(a_ref, o_ref, send_sems, recv_sems, local_sem, buf):
        x, y, c = _coords()
        me = 4 * x + 2 * y + c
        load = pltpu.make_async_copy(a_ref, buf, local_sem)
        load.start()
        sends = []
        for k in range(1, N_DEV):
            peer = (_flip(x, (k >> 2) & 1), _flip(y, (k >> 1) & 1), _flip(c, k & 1))
            cp = pltpu.make_async_remote_copy(a_ref, o_ref.at[me], send_sems.at[k - 1], recv_sems.at[k - 1],
                                              device_id=peer, device_id_type=MESH)
            cp.start()
            sends.append(cp)
        load.wait()
        mine = pltpu.make_async_copy(buf, o_ref.at[me], local_sem)
        mine.start()
        for k in range(1, N_DEV):
            px, py, pc = _flip(x, (k >> 2) & 1), _flip(y, (k >> 1) & 1), _flip(c, k & 1)
            src = 4 * px + 2 * py + pc
            pltpu.make_async_remote_copy(a_ref, o_ref.at[src], send_sems.at[k - 1], recv_sems.at[k - 1],
                                         device_id=(px, py, pc), device_id_type=MESH).wait_recv()
        for cp in sends:
            cp.wait_send()
        mine.wait()

    return pl.pallas_call(
        body, name=name, out_shape=jax.ShapeDtypeStruct((N_DEV,) + shape, a.dtype),
        in_specs=_any_specs(1), out_specs=pl.BlockSpec(memory_space=pl.ANY),
        scratch_shapes=[pltpu.SemaphoreType.DMA((N_DEV - 1,)), pltpu.SemaphoreType.DMA((N_DEV - 1,)),
                        pltpu.SemaphoreType.DMA(()), pltpu.VMEM(shape, a.dtype)],
    )(a)


def _chip_of(x, y, p):
    px, py = _flip(x, (p >> 1) & 1), _flip(y, p & 1)
    return 2 * px + py, px, py


class _ShardGather:
    def __init__(self, ins, outs, ici_send, ici_recv, d2d_send, d2d_recv):
        self.ins, self.outs = ins, outs
        self.sems = ici_send, ici_recv, d2d_send, d2d_recv
        self.x, self.y, self.c = _coords()
        self.me = 2 * self.x + self.y

    def _ici(self, i, p, slot):
        half = self.ins[i].shape[0] // 2
        rows = pl.ds(self.c * half, half)
        _, px, py = _chip_of(self.x, self.y, p)
        s = i * 3 + p - 1
        return pltpu.make_async_remote_copy(self.ins[i].at[rows], self.outs[i].at[slot, rows], self.sems[0].at[s],
                                            self.sems[1].at[s], device_id=(px, py, self.c), device_id_type=MESH)

    def _d2d(self, i, p, mine):
        half = self.ins[i].shape[0] // 2
        rows = pl.ds((self.c if mine else 1 - self.c) * half, half)
        src, _, _ = _chip_of(self.x, self.y, p)
        s = i * 3 + p - 1
        part = self.outs[i].at[src, rows]
        return pltpu.make_async_remote_copy(part, part, self.sems[2].at[s], self.sems[3].at[s],
                                            device_id=(self.x, self.y, 1 - self.c), device_id_type=MESH)

    def _each(self):
        return [(i, p) for i in range(len(self.ins)) for p in range(1, N_CHIPS)]

    def send(self):
        for i, p in self._each():
            self._ici(i, p, self.me).start()

    def forward(self):
        for i, p in self._each():
            self._ici(i, p, _chip_of(self.x, self.y, p)[0]).wait_recv()
            self._d2d(i, p, True).start()

    def finish(self):
        for i, p in self._each():
            self._d2d(i, p, False).wait_recv()
        for i, p in self._each():
            self._ici(i, p, self.me).wait_send()
            self._d2d(i, p, True).wait_send()


def gather_chip_shards(arrs, remote, *, name):
    n = len(arrs)
    far = [i for i in range(n) if remote[i]]

    def body(*refs):
        ins, outs = refs[:n], refs[n:2 * n]
        ici_send, ici_recv, d2d_send, d2d_recv, local_sems = refs[2 * n:2 * n + 5]
        bufs = refs[2 * n + 5:]
        me = 2 * lax.axis_index("x") + lax.axis_index("y")
        loads = []
        for i in range(n):
            cp = pltpu.make_async_copy(ins[i], bufs[i], local_sems.at[i])
            cp.start()
            loads.append(cp)
        gather = _ShardGather([ins[i] for i in far], [outs[i] for i in far], ici_send, ici_recv, d2d_send, d2d_recv)
        gather.send()
        stores = []
        for i in range(n):
            loads[i].wait()
            cp = pltpu.make_async_copy(bufs[i], outs[i].at[me], local_sems.at[i])
            cp.start()
            stores.append(cp)
        gather.forward()
        gather.finish()
        for cp in stores:
            cp.wait()

    dma = pltpu.SemaphoreType.DMA
    m = 3 * len(far)
    return pl.pallas_call(
        body, name=name,
        out_shape=[jax.ShapeDtypeStruct((N_CHIPS,) + a.shape, a.dtype) for a in arrs],
        in_specs=_any_specs(n), out_specs=_any_specs(n),
        scratch_shapes=[dma((m,)), dma((m,)), dma((m,)), dma((m,)), dma((n,))]
        + [pltpu.VMEM(a.shape, a.dtype) for a in arrs],
        compiler_params=pltpu.CompilerParams(vmem_limit_bytes=V7X_VMEM_LIMIT),
    )(*arrs)


def swap_halves(arrs, *, name):
    n = len(arrs)

    def body(*refs):
        ins, outs = refs[:n], refs[n:2 * n]
        send_sems, recv_sems = refs[2 * n:]
        x, y, c = _coords()
        cps = []
        for i in range(n):
            half = ins[i].shape[1] // 2
            cp = pltpu.make_async_remote_copy(ins[i].at[:, pl.ds((1 - c) * half, half)], outs[i], send_sems.at[i],
                                              recv_sems.at[i], device_id=(x, y, 1 - c), device_id_type=MESH)
            cp.start()
            cps.append(cp)
        for cp in cps:
            cp.wait()

    dma = pltpu.SemaphoreType.DMA
    return pl.pallas_call(
        body, name=name,
        out_shape=[jax.ShapeDtypeStruct((N_CHIPS, a.shape[1] // 2, a.shape[2]), a.dtype) for a in arrs],
        in_specs=_any_specs(n), out_specs=_any_specs(n), scratch_shapes=[dma((n,)), dma((n,))],
    )(*arrs)


def add_half(g, other, c_idx, *, name, out_dtype):
    _, R, C = g.shape
    half = R // 2
    tr = _tile(half, (256, 128, 64, 32, 16, 8))
    nt = half // tr

    def body(c_ref, g_ref, o_ref, out_ref):
        out_ref[...] = (g_ref[...].astype(F32) + o_ref[...].astype(F32)).astype(out_ref.dtype)

    return pl.pallas_call(
        body, name=name, out_shape=jax.ShapeDtypeStruct((N_CHIPS, half, C), out_dtype),
        grid_spec=pltpu.PrefetchScalarGridSpec(
            num_scalar_prefetch=1, grid=(N_CHIPS, nt),
            in_specs=[pl.BlockSpec((None, tr, C), lambda r, t, c_ref: (r, c_ref[0] * nt + t, 0)),
                      pl.BlockSpec((None, tr, C), lambda r, t, c_ref: (r, t, 0))],
            out_specs=pl.BlockSpec((None, tr, C), lambda r, t, c_ref: (r, t, 0))),
        compiler_params=_params(("parallel", "parallel")),
    )(c_idx, g, other)


class _ChipScatter:
    def __init__(self, ins, outs, send_sems, recv_sems):
        self.ins, self.outs, self.send_sems, self.recv_sems = ins, outs, send_sems, recv_sems
        self.x, self.y, self.c = _coords()

    def _copies(self):
        for i in range(len(self.ins)):
            for p in range(1, N_CHIPS):
                dst, px, py = _chip_of(self.x, self.y, p)
                s = i * 3 + p - 1
                yield pltpu.make_async_remote_copy(self.ins[i].at[dst], self.outs[i].at[p - 1], self.send_sems.at[s],
                                                   self.recv_sems.at[s], device_id=(px, py, self.c), device_id_type=MESH)

    def send(self):
        for cp in self._copies():
            cp.start()

    def finish(self):
        for cp in self._copies():
            cp.wait()


def scatter_to_chips(arrs, *, name):
    n = len(arrs)

    def body(*refs):
        scatter = _ChipScatter(refs[:n], refs[n:2 * n], *refs[2 * n:])
        scatter.send()
        scatter.finish()

    dma = pltpu.SemaphoreType.DMA
    return pl.pallas_call(
        body, name=name,
        out_shape=[jax.ShapeDtypeStruct((N_CHIPS - 1,) + a.shape[1:], a.dtype) for a in arrs],
        in_specs=_any_specs(n), out_specs=_any_specs(n), scratch_shapes=[dma((3 * n,)), dma((3 * n,))],
    )(*arrs)


def add_chips(h, got, r_idx, *, name):
    _, R, C = h.shape
    tr = _tile(R, (256, 128, 64, 32, 16, 8))

    def body(r_ref, h_ref, g_ref, out_ref):
        acc = h_ref[...].astype(F32)
        for p in range(N_CHIPS - 1):
            acc = acc + g_ref[p].astype(F32)
        out_ref[...] = acc

    return pl.pallas_call(
        body, name=name, out_shape=jax.ShapeDtypeStruct((R, C), F32),
        grid_spec=pltpu.PrefetchScalarGridSpec(
            num_scalar_prefetch=1, grid=(R // tr,),
            in_specs=[pl.BlockSpec((None, tr, C), lambda t, r_ref: (r_ref[0], t, 0)),
                      pl.BlockSpec((N_CHIPS - 1, tr, C), lambda t, r_ref: (0, t, 0))],
            out_specs=pl.BlockSpec((tr, C), lambda t, r_ref: (t, 0))),
        compiler_params=_params(("parallel",)),
    )(r_idx, h, got)


def join_halves(arrs, *, name):
    n = len(arrs)

    def body(*refs):
        ins, outs = refs[:n], refs[n:2 * n]
        send_sems, recv_sems, local_sems = refs[2 * n:2 * n + 3]
        bufs = refs[2 * n + 3:]
        x, y, c = _coords()
        loads, sends, stores = [], [], []
        for i in range(n):
            cp = pltpu.make_async_copy(ins[i], bufs[i], local_sems.at[i])
            cp.start()
            loads.append(cp)
        for i in range(n):
            half = ins[i].shape[0]
            cp = pltpu.make_async_remote_copy(ins[i], outs[i].at[pl.ds(c * half, half)], send_sems.at[i], recv_sems.at[i],
                                              device_id=(x, y, 1 - c), device_id_type=MESH)
            cp.start()
            sends.append(cp)
        for i in range(n):
            half = ins[i].shape[0]
            loads[i].wait()
            cp = pltpu.make_async_copy(bufs[i], outs[i].at[pl.ds(c * half, half)], local_sems.at[i])
            cp.start()
            stores.append(cp)
        for i in range(n):
            half = ins[i].shape[0]
            pltpu.make_async_remote_copy(ins[i], outs[i].at[pl.ds((1 - c) * half, half)], send_sems.at[i],
                                         recv_sems.at[i], device_id=(x, y, 1 - c), device_id_type=MESH).wait_recv()
        for i in range(n):
            sends[i].wait_send()
            stores[i].wait()

    dma = pltpu.SemaphoreType.DMA
    return pl.pallas_call(
        body, name=name,
        out_shape=[jax.ShapeDtypeStruct((2 * a.shape[0], a.shape[1]), a.dtype) for a in arrs],
        in_specs=_any_specs(n), out_specs=_any_specs(n),
        scratch_shapes=[dma((n,)), dma((n,)), dma((n,))] + [pltpu.VMEM(a.shape, a.dtype) for a in arrs],
        compiler_params=pltpu.CompilerParams(vmem_limit_bytes=V7X_VMEM_LIMIT),
    )(*arrs)


def pair_sums(grads, wire_dtypes, tag):
    c_idx = jnp.reshape(lax.axis_index("c"), (1,)).astype(jnp.int32)
    theirs = swap_halves(grads, name=f"rs_swap_halves_{tag}")
    return [add_half(g, o, c_idx, name=f"rs_add_half_{tag}{i}", out_dtype=wire_dtypes[i])
            for i, (g, o) in enumerate(zip(grads, theirs))]


def chip_sums(pairs, gots, tag):
    r_idx = jnp.reshape(2 * lax.axis_index("x") + lax.axis_index("y"), (1,)).astype(jnp.int32)
    return [add_chips(h, g, r_idx, name=f"rs_add_chips_{tag}{i}") for i, (h, g) in enumerate(zip(pairs, gots))]


def _chip_major(w, axis):
    n = w.shape[axis] // N_CHIPS
    parts = w.reshape(w.shape[:axis] + (N_CHIPS, n) + w.shape[axis + 1:])
    return jnp.moveaxis(parts, axis, 0)


def _from_chip_major(g, axis):
    g = jnp.moveaxis(g, 0, axis)
    return g.reshape(g.shape[:axis] + (g.shape[axis] * g.shape[axis + 1],) + g.shape[axis + 2:])


def kernel(x, c, norm_mix, norm_ffn, w_mod, b_mod, w_qkv, w_o_attn, w_in_ssm, a_re, a_im, log_dt, b_re, b_im, c_re, c_im, d_skip, w_glu, b_glu, w_o_ssm, w_up, conv_w, conv_b, w_down, norm_out, w_fin, b_fin, loss_target, m_norm_mix, m_norm_ffn, m_w_mod, m_b_mod, m_w_qkv, m_w_o_attn, m_w_in_ssm, m_a_re, m_a_im, m_log_dt, m_b_re, m_b_im, m_c_re, m_c_im, m_d_skip, m_w_glu, m_b_glu, m_w_o_ssm, m_w_up, m_conv_w, m_conv_b, m_w_down, m_norm_out, m_w_fin, m_b_fin, v_norm_mix, v_norm_ffn, v_w_mod, v_b_mod, v_w_qkv, v_w_o_attn, v_w_in_ssm, v_a_re, v_a_im, v_log_dt, v_b_re, v_b_im, v_c_re, v_c_im, v_d_skip, v_w_glu, v_b_glu, v_w_o_ssm, v_w_up, v_conv_w, v_conv_b, v_w_down, v_norm_out, v_w_fin, v_b_fin):
    B, S, D = x.shape
    T = B * S
    F2 = conv_b.shape[1]
    F = F2 // 2
    G, P = a_re.shape[1], a_re.shape[2]
    H = b_re.shape[3]
    mx, my, mc = _coords()
    chip = 2 * mx + my
    dev = 4 * mx + 2 * my + mc
    BG = N_DEV * B
    mod_w = w_mod.shape[2]
    fin_w = w_fin.shape[1]

    c_all = all_gather8(c, name="gather_c").reshape(BG, D)
    c_act = silu_rows(c_all, name="silu_c")
    b_mod_mine = lax.dynamic_slice(b_mod, (0, chip * mod_w), (2, mod_w))
    b_fin_mine = lax.dynamic_slice(b_fin, (chip * fin_w,), (fin_w,))
    cond = [matmul(c_act, w_mod[i], bias=b_mod_mine[i], name=f"mod_proj_{i}") for i in range(2)]
    cond.append(matmul(c_act, w_fin, bias=b_fin_mine, name="fin_proj"))
    cond_all = all_gather8(jnp.concatenate(cond, axis=1), name="gather_cond")
    cond_all = cond_all[::2]
    cond_rows = lax.dynamic_slice(cond_all, (0, dev * B, 0), (N_CHIPS, B, cond_all.shape[2]))
    mods = []
    for i in range(2):
        full = cond_rows[:, :, i * mod_w:(i + 1) * mod_w].transpose(1, 0, 2).reshape(B, N_CHIPS * mod_w)
        mods.append([full[:, k * D:(k + 1) * D] for k in range(6)])
    fin = cond_rows[:, :, 2 * mod_w:].transpose(1, 0, 2).reshape(B, N_CHIPS * fin_w)
    sh_f, sc_f = fin[:, :D], fin[:, D:]

    rows1024 = jnp.concatenate([w_o_attn[0], w_in_ssm[0], w_glu[0], w_o_ssm[0], w_down.reshape(-1, D)], axis=0)
    shards = [w_qkv[0].astype(BF16), rows1024.astype(BF16), w_up[0].astype(BF16), w_up[1].astype(BF16)]
    W_qkv, *own_slots = gather_chip_shards(shards, [True, False, False, False], name="gather_weights")
    Dq = D // N_CHIPS
    Fq = F // N_CHIPS
    small = jnp.concatenate([conv_w.reshape(6, -1), jnp.pad(d_skip, ((0, 0), (0, conv_w.shape[2] - Dq))),
                             jnp.pad(b_glu, ((0, 0), (0, conv_w.shape[2] - Dq)))], axis=0)
    small_all = all_gather8(small, name="gather_small")[::2]
    conv_w_full = _from_chip_major(small_all[:, :6].reshape(N_CHIPS, 2, 3, -1), 2)
    d_skip_full = small_all[:, 6, :Dq].reshape(1, D)
    b_glu_full = small_all[:, 7, :Dq].reshape(D)

    x0 = x.reshape(T, D)
    tgt = loss_target.reshape(T, D)

    def ffn_fwd(xprev, y, gate, i):
        sh2, sc2 = mods[i][3], mods[i][4]
        xin, h2 = res_norm_mod_fwd(xprev, y, gate, norm_ffn[i], sh2, sc2, B=B, S=S, name=f"ffn_norm_{i}")
        up = matmul(h2, W_up[i], b_chips=True, out_dtype=BF16, name=f"ffn_up_{i}")
        act = conv_gate_fwd(up, conv_w_full[i], conv_b[i:i + 1], B=B, S=S, name=f"ffn_conv_{i}")
        yf = matmul(act, W_down[i], name=f"ffn_down_{i}")
        return xin, yf, (xin, h2, up, act, yf)

    sh1, sc1, g1 = mods[0][0], mods[0][1], mods[0][2]
    h1a = norm_mod_fwd(x0, norm_mix[0], sh1, sc1, B=B, S=S, name="att_norm")
    qkv = matmul(h1a, W_qkv, out_dtype=BF16, b_chips=True, name="att_qkv")
    o2, ltot, g_rows, W_up0, W_up1 = attn_fwd_pairs(qkv, shards[1:], own_slots, B=B, S=S, name="att_fwd")
    W_up = [W_up0, W_up1]
    W_o_attn = g_rows[:, 0 * Dq:1 * Dq].reshape(D, D)
    W_in = g_rows[:, 1 * Dq:2 * Dq].reshape(D, D)
    W_glu = g_rows[:, 2 * Dq:3 * Dq].reshape(D, D)
    W_o_ssm = g_rows[:, 3 * Dq:4 * Dq].reshape(D, D)
    W_down = [g_rows[:, 4 * Dq + i * Fq:4 * Dq + (i + 1) * Fq].reshape(F, D) for i in range(2)]
    ya = matmul(o2, W_o_attn, name="att_out")
    x1, yf0, ffn0 = ffn_fwd(x0, ya, g1, 0)

    lr, li, bbr, bbi = _ssm_discretize(a_re[0], a_im[0], log_dt[0], b_re[0], b_im[0])
    J = G // GROUPS_PER_BLOCK
    Wst = GROUPS_PER_BLOCK * P
    bre_blk = _block_diag_in(bbr).astype(BF16)
    bim_blk = _block_diag_in(bbi).astype(BF16)
    cre_blk = _block_diag_out(c_re[0]).astype(BF16)
    cim_blk = _block_diag_out(c_im[0]).astype(BF16)
    lr8 = jnp.broadcast_to(lr.reshape(J, 1, Wst), (J, SEGMENTS, Wst))
    li8 = jnp.broadcast_to(li.reshape(J, 1, Wst), (J, SEGMENTS, Wst))
    sh1s, sc1s, g1s = mods[1][0], mods[1][1], mods[1][2]
    x2, h1s = res_norm_mod_fwd(x1, yf0, mods[0][5], norm_mix[1], sh1s, sc1s, B=B, S=S, name="ssm_norm")
    h1p = _interleave(h1s, B, S)
    u = matmul(h1p, W_in, name="ssm_in")
    y_ssm = ssm_fwd(u, bre_blk, bim_blk, cre_blk, cim_blk, lr8, li8, d_skip_full, B=B, S=S, name="ssm_scan_fwd")
    zb = gelu_fwd(y_ssm, B=B, S=S, name="ssm_gelu")
    s_glu = matmul(zb, W_glu, bias=b_glu_full, name="ssm_glu_proj")
    gb = glu_fwd(y_ssm, s_glu, B=B, S=S, name="ssm_glu")
    ys_p = matmul(gb, W_o_ssm, name="ssm_out")
    ys = _deinterleave(ys_p, B, S)
    x3, yf1, ffn1 = ffn_fwd(x2, ys, g1s, 1)
    x4 = gate_res_fwd(x3, yf1, mods[1][5], B=B, S=S, name="ffn_res_1")

    dx4, dyf1, loss_p, dsh_f, dsc_f, dnorm_out, dg2_1 = final_loss(x4, tgt, norm_out, sh_f, sc_f, yf1, mods[1][5],
                                                                   B=B, S=S, name="loss_head")
    loss = lax.psum(jnp.sum(loss_p), ("x", "y", "c"))

    def ffn_bwd(dxo, dyf, i, saved, y_prev, gate_prev):
        xin, h2, up, act, yf = saved
        sc2 = mods[i][4]
        dact = matmul(dyf, W_down[i], tb=True, out_dtype=BF16, name=f"ffn_down_dx_{i}")
        dW_down = matmul(act, dyf, ta=True, out_dtype=BF16, name=f"ffn_down_dw_{i}")
        dup, dcw, dcb = conv_gate_bwd(up, dact, conv_w_full[i], conv_b[i:i + 1], B=B, S=S, name=f"ffn_conv_bwd_{i}")
        dh2 = matmul(dup, W_up[i], tb=True, b_chips=True, name=f"ffn_up_dx_{i}")
        dW_up = matmul(h2, dup, ta=True, b_chips=True, out_chips=True, out_dtype=BF16, name=f"ffn_up_dw_{i}")
        dxin, dy_prev, dsh2, dsc2, dnf, dgate_prev = norm_mod_bwd_gate(
            dh2, xin, dxo, norm_ffn[i], sc2, y_prev, gate_prev, B=B, S=S, name=f"ffn_norm_bwd_{i}")
        dconv_w = jnp.sum(dcw, axis=0).transpose(1, 0, 2).reshape(3, F2)
        return dxin, dy_prev, dgate_prev, dict(dW_down=dW_down, dW_up=dW_up, dconv_b=jnp.sum(dcb, axis=0).reshape(F2),
                                               dconv_w=dconv_w, dnorm_ffn=jnp.sum(dnf, axis=0), dsh2=dsh2, dsc2=dsc2)

    dx3, dys, dg1s, gf1 = ffn_bwd(dx4, dyf1, 1, ffn1, ys, g1s)
    gf1["dg2"] = dg2_1

    dys_p = _interleave(dys, B, S)
    dgb = matmul(dys_p, W_o_ssm, tb=True, name="ssm_out_dx")
    dW_o_ssm = matmul(gb, dys_p, ta=True, out_dtype=BF16, name="ssm_out_dw")
    ds_glu, dz1, db_glu = glu_bwd1(y_ssm, s_glu, dgb, B=B, S=S, name="ssm_glu_bwd1")
    dz2 = matmul(ds_glu, W_glu, tb=True, name="ssm_glu_dx")
    dW_glu = matmul(zb, ds_glu, ta=True, out_dtype=BF16, name="ssm_glu_dw")
    dy_ssm = glu_bwd2(y_ssm, dz1, dz2, B=B, S=S, name="ssm_glu_bwd2")
    du, dbre, dbim, dcre, dcim, dlr8, dli8, ddsk = ssm_bwd(u, dy_ssm, bre_blk, bim_blk, cre_blk, cim_blk, lr8, li8,
                                                           d_skip_full, B=B, S=S, name="ssm_scan_bwd")
    dub = du.astype(BF16)
    dh1p = matmul(dub, W_in, tb=True, name="ssm_in_dx")
    dW_in = matmul(h1p, dub, ta=True, out_dtype=BF16, name="ssm_in_dw")
    dx2, dyf0, dsh1s, dsc1s, dnm1, dg2_0 = norm_mod_bwd_gate(_deinterleave(dh1p, B, S), x2, dx3, norm_mix[1], sc1s,
                                                             yf0, mods[0][5], B=B, S=S, name="ssm_norm_bwd")
    dlr = jnp.sum(dlr8, axis=(0, 2)).reshape(G, P)
    dli = jnp.sum(dli8, axis=(0, 2)).reshape(G, P)
    dbbr = _block_diag_in_grad(jnp.sum(dbre, axis=0), G, P, H)
    dbbi = _block_diag_in_grad(jnp.sum(dbim, axis=0), G, P, H)
    dc_re = _block_diag_out_grad(jnp.sum(dcre, axis=0), G, H, P)
    dc_im = _block_diag_out_grad(jnp.sum(dcim, axis=0), G, H, P)
    dd_skip = jnp.sum(ddsk, axis=0).reshape(D)

    dx1, dya, dg1, gf0 = ffn_bwd(dx2, dyf0, 0, ffn0, ya, g1)
    gf0["dg2"] = dg2_0

    do2 = matmul(dya, W_o_attn, tb=True, out_dtype=BF16, name="att_out_dx")
    dW_o_attn = matmul(o2, dya, ta=True, out_dtype=BF16, name="att_out_dw")
    g_rows_cm = jnp.concatenate([dW_o_attn.reshape(N_CHIPS, Dq, D), dW_in.reshape(N_CHIPS, Dq, D),
                                 dW_glu.reshape(N_CHIPS, Dq, D), dW_o_ssm.reshape(N_CHIPS, Dq, D),
                                 gf0["dW_down"].reshape(N_CHIPS, Fq, D), gf1["dW_down"].reshape(N_CHIPS, Fq, D)], axis=1)
    pairs_a = pair_sums([g_rows_cm, gf0["dW_up"], gf1["dW_up"]], [BF16, BF16, BF16], "a")
    dqkv, *gots_a = attn_bwd_pairs(qkv, ltot, do2, pairs_a, B=B, S=S, name="att_bwd")
    dh1a = matmul(dqkv, _from_chip_major(W_qkv, 1), tb=True, name="att_qkv_dx")
    dW_qkv = _chip_major(matmul(h1a, dqkv, ta=True, b_chips=True, out_dtype=BF16, name="att_qkv_dw"), 1)
    grad_x, dsh1, dsc1, dnm0 = norm_mod_bwd(dh1a, x0, dx1, norm_mix[0], sc1, B=B, S=S, name="att_norm_bwd")

    dmod_rows = jnp.concatenate([dsh1, dsc1, dg1, gf0["dsh2"], gf0["dsc2"], gf0["dg2"],
                                 dsh1s, dsc1s, dg1s, gf1["dsh2"], gf1["dsc2"], gf1["dg2"], dsh_f, dsc_f], axis=1)
    dmod_all = all_gather8(dmod_rows, name="gather_dmod").reshape(BG, 14 * D)
    grad_w_mod = jnp.stack([
        matmul(c_act, lax.dynamic_slice(dmod_all, (0, i * 6 * D + chip * mod_w), (BG, mod_w)), ta=True,
               name=f"mod_dw_{i}") for i in range(2)])
    grad_w_fin = matmul(c_act, lax.dynamic_slice(dmod_all, (0, 12 * D + chip * fin_w), (BG, fin_w)), ta=True,
                        name="fin_dw")

    parts = [jnp.concatenate([jnp.sum(dnm0, axis=0), jnp.sum(dnm1, axis=0)]),
             jnp.concatenate([gf0["dnorm_ffn"], gf1["dnorm_ffn"]]),
             jnp.sum(dmod_rows[:, :12 * D], axis=0),
             dlr.reshape(-1), dli.reshape(-1), dbbr.reshape(-1), dbbi.reshape(-1), dc_re.reshape(-1), dc_im.reshape(-1),
             dd_skip, jnp.sum(db_glu, axis=0),
             jnp.sum(dnorm_out, axis=0), jnp.sum(dmod_rows[:, 12 * D:], axis=0),
             gf0["dconv_w"].reshape(-1), gf1["dconv_w"].reshape(-1), gf0["dconv_b"], gf1["dconv_b"]]
    sizes = [int(p.shape[0]) for p in parts]
    flat = jnp.concatenate(parts)
    width = 1024
    quantum = N_CHIPS * 16 * width
    padded = -(-flat.shape[0] // quantum) * quantum
    small_cm = jnp.pad(flat, (0, padded - flat.shape[0])).reshape(N_CHIPS, -1, width)

    pairs_b = pair_sums([dW_qkv, small_cm], [BF16, F32], "b")
    gots_b = scatter_to_chips(pairs_b, name="rs_scatter_to_chips")
    r_qkv, r_small, r_rows, r_up0, r_up1 = join_halves(
        chip_sums(pairs_b, gots_b, "b") + chip_sums(pairs_a, gots_a, "a"), name="rs_join_halves")
    grad_w_qkv = r_qkv[None]
    grad_w_o_attn = r_rows[0 * Dq:1 * Dq][None]
    grad_w_in_ssm = r_rows[1 * Dq:2 * Dq][None]
    grad_w_glu = r_rows[2 * Dq:3 * Dq][None]
    grad_w_o_ssm = r_rows[3 * Dq:4 * Dq][None]
    grad_w_down = r_rows[4 * Dq:].reshape(2, Fq, D)
    grad_w_up = jnp.stack([r_up0, r_up1])
    summed = all_gather8(r_small, name="gather_small_grads")[::2].reshape(-1)
    offs = [0]
    for s_ in sizes:
        offs.append(offs[-1] + s_)
    (s_nm, s_nf, s_bmod, s_lr, s_li, s_bbr, s_bbi, s_cre, s_cim, s_dsk, s_bglu, s_no, s_bfin, s_cw0, s_cw1, s_cb0,
     s_cb1) = [summed[offs[i]:offs[i + 1]] for i in range(len(sizes))]
    _, disc_vjp = jax.vjp(_ssm_discretize, a_re[0], a_im[0], log_dt[0], b_re[0], b_im[0])
    ga_re, ga_im, glog_dt, gb_re, gb_im = disc_vjp((s_lr.reshape(G, P), s_li.reshape(G, P), s_bbr.reshape(G, P, H),
                                                    s_bbi.reshape(G, P, H)))
    grad_norm_mix = s_nm.reshape(2, D)
    grad_norm_ffn = s_nf.reshape(2, D)
    grad_b_mod = s_bmod.reshape(2, 6 * D)
    grad_c_re = s_cre.reshape(1, G, H, P)
    grad_c_im = s_cim.reshape(1, G, H, P)
    grad_d_skip = lax.dynamic_slice(s_dsk, (chip * Dq,), (Dq,)).reshape(1, Dq)
    grad_b_glu = lax.dynamic_slice(s_bglu, (chip * Dq,), (Dq,)).reshape(1, Dq)
    cw_full = jnp.stack([s_cw0.reshape(3, F2), s_cw1.reshape(3, F2)])
    grad_conv_w = lax.dynamic_slice(cw_full, (0, 0, chip * (F2 // N_CHIPS)), (2, 3, F2 // N_CHIPS))
    grad_conv_b = jnp.stack([s_cb0, s_cb1])
    grad_norm_out = s_no
    grad_b_fin = s_bfin

    grads = dict(
        norm_mix=grad_norm_mix, norm_ffn=grad_norm_ffn, w_mod=grad_w_mod, b_mod=grad_b_mod, w_qkv=grad_w_qkv,
        w_o_attn=grad_w_o_attn, w_in_ssm=grad_w_in_ssm, a_re=ga_re[None], a_im=ga_im[None], log_dt=glog_dt[None],
        b_re=gb_re[None], b_im=gb_im[None], c_re=grad_c_re, c_im=grad_c_im, d_skip=grad_d_skip, w_glu=grad_w_glu,
        b_glu=grad_b_glu, w_o_ssm=grad_w_o_ssm, w_up=grad_w_up, conv_w=grad_conv_w, conv_b=grad_conv_b,
        w_down=grad_w_down, norm_out=grad_norm_out, w_fin=grad_w_fin, b_fin=grad_b_fin)
    weights = dict(
        norm_mix=norm_mix, norm_ffn=norm_ffn, w_mod=w_mod, b_mod=b_mod, w_qkv=w_qkv, w_o_attn=w_o_attn,
        w_in_ssm=w_in_ssm, a_re=a_re, a_im=a_im, log_dt=log_dt, b_re=b_re, b_im=b_im, c_re=c_re, c_im=c_im,
        d_skip=d_skip, w_glu=w_glu, b_glu=b_glu, w_o_ssm=w_o_ssm, w_up=w_up, conv_w=conv_w, conv_b=conv_b,
        w_down=w_down, norm_out=norm_out, w_fin=w_fin, b_fin=b_fin)
    m_in = dict(
        norm_mix=m_norm_mix, norm_ffn=m_norm_ffn, w_mod=m_w_mod, b_mod=m_b_mod, w_qkv=m_w_qkv, w_o_attn=m_w_o_attn,
        w_in_ssm=m_w_in_ssm, a_re=m_a_re, a_im=m_a_im, log_dt=m_log_dt, b_re=m_b_re, b_im=m_b_im, c_re=m_c_re,
        c_im=m_c_im, d_skip=m_d_skip, w_glu=m_w_glu, b_glu=m_b_glu, w_o_ssm=m_w_o_ssm, w_up=m_w_up, conv_w=m_conv_w,
        conv_b=m_conv_b, w_down=m_w_down, norm_out=m_norm_out, w_fin=m_w_fin, b_fin=m_b_fin)
    v_in = dict(
        norm_mix=v_norm_mix, norm_ffn=v_norm_ffn, w_mod=v_w_mod, b_mod=v_b_mod, w_qkv=v_w_qkv, w_o_attn=v_w_o_attn,
        w_in_ssm=v_w_in_ssm, a_re=v_a_re, a_im=v_a_im, log_dt=v_log_dt, b_re=v_b_re, b_im=v_b_im, c_re=v_c_re,
        c_im=v_c_im, d_skip=v_d_skip, w_glu=v_w_glu, b_glu=v_b_glu, w_o_ssm=v_w_o_ssm, w_up=v_w_up, conv_w=v_conv_w,
        conv_b=v_conv_b, w_down=v_w_down, norm_out=v_norm_out, w_fin=v_w_fin, b_fin=v_b_fin)
    names = list(weights)
    for n_ in names:
        grads[n_] = grads[n_].reshape(weights[n_].shape)

    big = ("w_mod", "w_qkv", "w_o_attn", "w_in_ssm", "w_glu", "w_o_ssm", "w_up", "w_down", "w_fin")
    delta, new_m, new_v = {}, {}, {}
    for n_ in big:
        shp = weights[n_].shape
        two_d = lambda a: a.reshape(-1, shp[-1])
        d_, m_, v_ = adamw(two_d(weights[n_]), two_d(grads[n_]), two_d(m_in[n_]), two_d(v_in[n_]), name=f"adamw_{n_}")
        delta[n_], new_m[n_], new_v[n_] = d_.reshape(shp), m_.reshape(shp), v_.reshape(shp)
    rest = [n_ for n_ in names if n_ not in big]
    small_out = adamw_many([weights[n_] for n_ in rest], [grads[n_] for n_ in rest], [m_in[n_] for n_ in rest],
                           [v_in[n_] for n_ in rest], name="adamw_small")
    for n_, (d_, m_, v_) in zip(rest, small_out):
        delta[n_], new_m[n_], new_v[n_] = d_, m_, v_

    return (loss, grad_x.reshape(B, S, D), *[grads[n_] for n_ in names], *[delta[n_] for n_ in names],
            *[new_m[n_] for n_ in names], *[new_v[n_] for n_ in names])
```

```python
import math

import jax
import jax.numpy as jnp
from jax import lax
from jax.experimental import pallas as pl
from jax.experimental.pallas import tpu as pltpu

F32 = jnp.float32
BF16 = jnp.bfloat16
MESH = pl.DeviceIdType.MESH

HEAD_DIM = 64
SSM_GROUP = 16
STATE = 64
GROUPS_PER_BLOCK = 8
SEGMENTS = 16
SCAN_UNROLL = 4
EPS = 1e-6
ADAM_LR = 0.001
ADAM_B1 = 0.9
ADAM_B2 = 0.999
ADAM_EPS = 1e-08
ADAM_WD = 0.01
ADAM_STEP = 10
N_CHIPS = 4
N_DEV = 8
V7X_VMEM_LIMIT = 56 * 1024 * 1024
ATT_BLOCK = 128
ATT_HEADS = 8
ATT_HEADS_BWD = 8


def _tile(n, prefs):
    for p in prefs:
        if n % p == 0:
            return p
    return n


def _row_tile(n, cap=1024, mult=16):
    for t in range(min(n, cap) // mult * mult, 0, -mult):
        if n % t == 0:
            return t
    return n


def _params(sem, vmem=V7X_VMEM_LIMIT):
    return pltpu.CompilerParams(dimension_semantics=sem, vmem_limit_bytes=vmem)


def matmul(a, b, *, ta=False, tb=False, bias=None, out_dtype=F32, b_chips=False, out_chips=False, name):
    a_parts = a.shape[0] if a.ndim == 3 else 1
    if a_parts > 1:
        assert not ta
        M, K = a.shape[1], a_parts * a.shape[2]
    elif ta:
        K, M = a.shape
    else:
        M, K = a.shape
    b_parts = b.shape[0] if b_chips else 1
    b_rows, b_cols = (b.shape[1], b_parts * b.shape[2]) if b_chips else b.shape
    if tb:
        N, Kb = b_rows, b_cols
    else:
        Kb, N = b_rows, b_cols
    assert K == Kb, (a.shape, b.shape, ta, tb)
    n_cut = math.gcd(N // (N_CHIPS if out_chips else 1), N // (b_parts if not tb else 1))
    k_cut = math.gcd(K // (b_parts if tb else 1), K // a_parts)
    tm = _tile(M, (1024, 1408, 512, 256, 128))
    tn = _tile(n_cut, (1024, 1408, 768, 512, 256, 128))
    tk = k_cut if k_cut <= 2816 else _tile(k_cut, (1024, 512, 256, 128))
    nk = K // tk
    npc = N // N_CHIPS // tn
    npb = N // b_parts // tn
    kpb = K // b_parts // tk
    kpa = K // a_parts // tk
    dims = (((0,) if ta else (1,), (1,) if tb else (0,)), ((), ()))

    def body(*refs):
        a_ref, b_ref = refs[:2]
        bias_ref = refs[2] if bias is not None else None
        o_ref = refs[-2] if nk > 1 else refs[-1]

        def finish(r):
            if bias_ref is not None:
                r = r + bias_ref[...]
            o_ref[...] = r.astype(o_ref.dtype)

        prod = lax.dot_general(a_ref[...].astype(BF16), b_ref[...].astype(BF16), dims, preferred_element_type=F32)
        if nk == 1:
            finish(prod)
            return
        acc_ref = refs[-1]
        k = pl.program_id(2)

        @pl.when(k == 0)
        def _():
            acc_ref[...] = prod

        @pl.when(k > 0)
        def _():
            acc_ref[...] += prod

        @pl.when(k == nk - 1)
        def _():
            finish(acc_ref[...])

    if a_parts > 1:
        a_spec = pl.BlockSpec((None, tm, tk), lambda i, j, k: (lax.div(k, kpa), i, lax.rem(k, kpa)))
    else:
        a_spec = pl.BlockSpec((tk, tm), lambda i, j, k: (k, i)) if ta else pl.BlockSpec((tm, tk), lambda i, j, k: (i, k))
    if not b_chips:
        b_spec = pl.BlockSpec((tn, tk), lambda i, j, k: (j, k)) if tb else pl.BlockSpec((tk, tn), lambda i, j, k: (k, j))
    elif tb:
        b_spec = pl.BlockSpec((None, tn, tk), lambda i, j, k: (lax.div(k, kpb), j, lax.rem(k, kpb)))
    else:
        b_spec = pl.BlockSpec((None, tk, tn), lambda i, j, k: (lax.div(j, npb), k, lax.rem(j, npb)))
    in_specs = [a_spec, b_spec]
    args = [a, b]
    if bias is not None:
        in_specs.append(pl.BlockSpec((1, tn), lambda i, j, k: (0, j)))
        args.append(bias.reshape(1, N).astype(F32))
    if out_chips:
        out_shape = jax.ShapeDtypeStruct((N_CHIPS, M, N // N_CHIPS), out_dtype)
        out_spec = pl.BlockSpec((None, tm, tn), lambda i, j, k: (lax.div(j, npc), i, lax.rem(j, npc)))
    else:
        out_shape = jax.ShapeDtypeStruct((M, N), out_dtype)
        out_spec = pl.BlockSpec((tm, tn), lambda i, j, k: (i, j))
    return pl.pallas_call(
        body, name=name,
        out_shape=out_shape,
        grid=(M // tm, N // tn, nk),
        in_specs=in_specs,
        out_specs=out_spec,
        scratch_shapes=[pltpu.VMEM((tm, tn), F32)] if nk > 1 else [],
        compiler_params=_params(("parallel", "parallel", "arbitrary")),
    )(*args)


def rowwise(fn, tiled, per_seq, glob, out_tiled, out_seq, *, B, S, name, rows=512):
    tm = _tile(S, (rows, 128, 64, 32, 16, 8))
    nt = S // tm
    n_in = len(tiled) + len(per_seq) + len(glob)
    n_ot = len(out_tiled)

    def body(*refs):
        ins = refs[:n_in]
        outs = refs[n_in:]
        vals = fn(*[r[...] for r in ins])
        if not isinstance(vals, (tuple, list)):
            vals = (vals,)
        assert len(vals) == len(outs), (name, len(vals), len(outs))
        for o_ref, v in zip(outs[:n_ot], vals[:n_ot]):
            o_ref[...] = v.astype(o_ref.dtype)
        t = pl.program_id(1)
        for o_ref, v in zip(outs[n_ot:], vals[n_ot:]):
            def first(o_ref=o_ref, v=v):
                o_ref[...] = v.astype(F32)

            def later(o_ref=o_ref, v=v):
                o_ref[...] += v.astype(F32)

            pl.when(t == 0)(first)
            pl.when(t > 0)(later)

    in_specs = [pl.BlockSpec((tm, a.shape[1]), lambda b, t: (b * nt + t, 0)) for a in tiled]
    in_specs += [pl.BlockSpec((None, 1, a.shape[1]), lambda b, t: (b, 0, 0)) for a in per_seq]
    in_specs += [pl.BlockSpec(a.shape, lambda b, t: (0,) * a.ndim) for a in glob]
    out_shape = [jax.ShapeDtypeStruct((B * S, w), dt) for w, dt in out_tiled]
    out_shape += [jax.ShapeDtypeStruct((B, 1, w), F32) for w in out_seq]
    out_specs = [pl.BlockSpec((tm, w), lambda b, t: (b * nt + t, 0)) for w, _ in out_tiled]
    out_specs += [pl.BlockSpec((None, 1, w), lambda b, t: (b, 0, 0)) for w in out_seq]
    res = pl.pallas_call(
        body, name=name, out_shape=out_shape, grid=(B, nt), in_specs=in_specs, out_specs=out_specs,
        compiler_params=_params(("parallel", "arbitrary")),
    )(*tiled, *[a.reshape(B, 1, a.shape[1]) for a in per_seq], *glob)
    res = list(res)
    for i in range(n_ot, len(res)):
        res[i] = res[i].reshape(B, res[i].shape[-1])
    return res


def _rms(x):
    r = lax.rsqrt(jnp.mean(x * x, axis=-1, keepdims=True) + EPS)
    return x * r, r


def norm_mod_fwd(x, g, sh, sc, *, B, S, name):
    def fn(x, sh, sc, g):
        xn, _ = _rms(x)
        return (xn * g) * (1.0 + sc) + sh

    return rowwise(fn, [x], [sh, sc], [g.reshape(1, -1)], [(x.shape[1], BF16)], [], B=B, S=S, name=name)[0]


def _norm_mod_bwd_math(dh, x, sc, g):
    xn, r = _rms(x)
    y = xn * g
    dy = dh * (1.0 + sc)
    dxn = dy * g
    dx = r * (dxn - xn * jnp.mean(dxn * xn, axis=-1, keepdims=True))
    dsh = jnp.sum(dh, axis=0, keepdims=True)
    dsc = jnp.sum(dh * y, axis=0, keepdims=True)
    dg = jnp.sum(dy * xn, axis=0, keepdims=True)
    return dx, dsh, dsc, dg


def norm_mod_bwd(dh, x, dres, g, sc, *, B, S, name):
    D = x.shape[1]

    def fn(dh, x, dres, sc, g):
        dx, dsh, dsc, dg = _norm_mod_bwd_math(dh.astype(F32), x, sc, g)
        return dres + dx, dsh, dsc, dg

    return rowwise(fn, [dh, x, dres], [sc], [g.reshape(1, -1)], [(D, F32)], [D, D, D], B=B, S=S, name=name)


def norm_mod_bwd_gate(dh, x, dres, g, sc, y_prev, gate_prev, *, B, S, name):
    D = x.shape[1]

    def fn(dh, x, dres, y, sc, gate, g):
        dx, dsh, dsc, dg = _norm_mod_bwd_math(dh.astype(F32), x, sc, g)
        dx = dres + dx
        return dx, gate * dx, dsh, dsc, dg, jnp.sum(dx * y, axis=0, keepdims=True)

    return rowwise(fn, [dh, x, dres, y_prev], [sc, gate_prev], [g.reshape(1, -1)], [(D, F32), (D, BF16)],
                   [D, D, D, D], B=B, S=S, name=name)


def gate_res_fwd(x, y, gate, *, B, S, name):
    return rowwise(lambda x, y, g: x + g * y, [x, y], [gate], [], [(x.shape[1], F32)], [], B=B, S=S, name=name)[0]


def res_norm_mod_fwd(x, y, gate, g, sh, sc, *, B, S, name):
    D = x.shape[1]

    def fn(x, y, gate, sh, sc, g):
        x = x + gate * y
        xn, _ = _rms(x)
        return x, (xn * g) * (1.0 + sc) + sh

    return rowwise(fn, [x, y], [gate, sh, sc], [g.reshape(1, -1)], [(D, F32), (D, BF16)], [], B=B, S=S, name=name)


def final_loss(x, tgt, g, sh, sc, y_prev, gate_prev, *, B, S, name):
    D = x.shape[1]

    def fn(x, tgt, y_prev, sh, sc, gate, g):
        xn, _ = _rms(x)
        y = (xn * g) * (1.0 + sc) + sh
        err = y - tgt
        loss = 0.5 * jnp.sum(err * err, axis=0, keepdims=True) * (1.0 / D)
        dx, dsh, dsc, dg = _norm_mod_bwd_math(err * (1.0 / D), x, sc, g)
        return dx, gate * dx, loss, dsh, dsc, dg, jnp.sum(dx * y_prev, axis=0, keepdims=True)

    return rowwise(fn, [x, tgt, y_prev], [sh, sc, gate_prev], [g.reshape(1, -1)], [(D, F32), (D, BF16)],
                   [D, D, D, D, D], B=B, S=S, name=name)


def _gelu(y):
    c0 = math.sqrt(2.0 / math.pi)
    t = jnp.tanh(c0 * (y + 0.044715 * (y * y * y)))
    return 0.5 * y * (1.0 + t), t


def _sigmoid(s):
    return 1.0 / (1.0 + jnp.exp(-s))


def gelu_fwd(y, *, B, S, name):
    return rowwise(lambda y: _gelu(y)[0], [y], [], [], [(y.shape[1], BF16)], [], B=B, S=S, name=name)[0]


def glu_fwd(y, s, *, B, S, name):
    return rowwise(lambda y, s: _gelu(y)[0] * _sigmoid(s), [y, s], [], [], [(y.shape[1], BF16)], [], B=B, S=S,
                   name=name)[0]


def glu_bwd1(y, s, dg, *, B, S, name):
    D = y.shape[1]

    def fn(y, s, dg):
        z = _gelu(y)[0]
        sig = _sigmoid(s)
        ds = dg * z * sig * (1.0 - sig)
        return ds, dg * sig, jnp.sum(ds, axis=0, keepdims=True)

    return rowwise(fn, [y, s, dg], [], [], [(D, BF16), (D, F32)], [D], B=B, S=S, name=name)


def glu_bwd2(y, dz1, dz2, *, B, S, name):
    D = y.shape[1]
    c0 = math.sqrt(2.0 / math.pi)

    def fn(y, dz1, dz2):
        _, t = _gelu(y)
        dgelu = 0.5 * (1.0 + t) + 0.5 * y * (1.0 - t * t) * c0 * (1.0 + 3.0 * 0.044715 * y * y)
        return (dz1 + dz2) * dgelu

    return rowwise(fn, [y, dz1, dz2], [], [], [(D, F32)], [], B=B, S=S, name=name)[0]


def silu_rows(c, *, name):
    R, W = c.shape
    return rowwise(lambda c: c * _sigmoid(c), [c], [], [], [(W, F32)], [], B=1, S=R, name=name)[0]


def conv_gate_fwd(up, cw, cb, *, B, S, name):
    F = up.shape[1] // 2
    tn = _tile(F, (256, 128))
    nF = F // tn

    def body(g_ref, v_ref, wg_ref, wv_ref, bg_ref, bv_ref, o_ref):
        rows = lax.broadcasted_iota(jnp.int32, (S, tn), 0)

        def conv(x, w_ref):
            x1 = jnp.where(rows >= 1, pltpu.roll(x, 1, 0), 0.0)
            x2 = jnp.where(rows >= 2, pltpu.roll(x, 2, 0), 0.0)
            return w_ref[2:3, :] * x + w_ref[1:2, :] * x1 + w_ref[0:1, :] * x2

        gc = conv(g_ref[...].astype(F32), wg_ref) + bg_ref[...]
        vc = conv(v_ref[...].astype(F32), wv_ref) + bv_ref[...]
        o_ref[...] = (gc * _sigmoid(gc) * vc).astype(o_ref.dtype)

    def cols(off):
        return pl.BlockSpec((S, tn), lambda b, j: (b, j + off))

    def vec(rows, off):
        return pl.BlockSpec((rows, tn), lambda b, j: (0, j + off))

    return pl.pallas_call(
        body, name=name, out_shape=jax.ShapeDtypeStruct((B * S, F), BF16), grid=(B, nF),
        in_specs=[cols(0), cols(nF), vec(3, 0), vec(3, nF), vec(1, 0), vec(1, nF)],
        out_specs=pl.BlockSpec((S, tn), lambda b, j: (b, j)),
        compiler_params=_params(("parallel", "parallel")),
    )(up, up, cw, cw, cb, cb)


def conv_gate_bwd(up, dact, cw, cb, *, B, S, name):
    F = up.shape[1] // 2
    tn = _tile(F, (256, 128))
    nF = F // tn

    def body(g_ref, v_ref, da_ref, wg_ref, wv_ref, bg_ref, bv_ref, o_ref, dw_ref, db_ref):
        rows = lax.broadcasted_iota(jnp.int32, (S, tn), 0)

        def earlier(x, k):
            return jnp.where(rows >= k, pltpu.roll(x, k, 0), 0.0)

        def later(x, k):
            return jnp.where(rows < S - k, pltpu.roll(x, S - k, 0), 0.0)

        def conv(x, w_ref):
            x1, x2 = earlier(x, 1), earlier(x, 2)
            return w_ref[2:3, :] * x + w_ref[1:2, :] * x1 + w_ref[0:1, :] * x2, x1, x2

        def back(d, x, x1, x2, w_ref, half):
            o_ref[half] = (w_ref[2:3, :] * d + w_ref[1:2, :] * later(d, 1) + w_ref[0:1, :] * later(d, 2)
                           ).astype(o_ref.dtype)
            dw_ref[half] = jnp.concatenate([jnp.sum(d * x2, axis=0, keepdims=True),
                                            jnp.sum(d * x1, axis=0, keepdims=True),
                                            jnp.sum(d * x, axis=0, keepdims=True)], axis=0)
            return jnp.sum(d, axis=0, keepdims=True)

        g, v, da = g_ref[...].astype(F32), v_ref[...].astype(F32), da_ref[...].astype(F32)
        gc, g1, g2 = conv(g, wg_ref)
        vc, v1, v2 = conv(v, wv_ref)
        gc = gc + bg_ref[...]
        vc = vc + bv_ref[...]
        sig = _sigmoid(gc)
        dg = da * vc * (sig * (1.0 + gc * (1.0 - sig)))
        dv = da * (gc * sig)
        db_ref[...] = jnp.concatenate([back(dg, g, g1, g2, wg_ref, 0), back(dv, v, v1, v2, wv_ref, 1)], axis=0)

    def cols(off):
        return pl.BlockSpec((S, tn), lambda b, j: (b, j + off))

    def vec(rows, off):
        return pl.BlockSpec((rows, tn), lambda b, j: (0, j + off))

    return pl.pallas_call(
        body, name=name,
        out_shape=[jax.ShapeDtypeStruct((2, B * S, F), BF16), jax.ShapeDtypeStruct((B, 2, 3, F), F32),
                   jax.ShapeDtypeStruct((B, 2, F), F32)],
        grid=(B, nF),
        in_specs=[cols(0), cols(nF), cols(0), vec(3, 0), vec(3, nF), vec(1, 0), vec(1, nF)],
        out_specs=[pl.BlockSpec((2, S, tn), lambda b, j: (0, b, j)),
                   pl.BlockSpec((None, 2, 3, tn), lambda b, j: (b, 0, 0, j)),
                   pl.BlockSpec((None, 2, tn), lambda b, j: (b, 0, j))],
        compiler_params=_params(("parallel", "parallel")),
    )(up, up, dact, cw, cw, cb, cb)


MASKED_LOG = -1e30


def _split2(x):
    bits = lax.bitcast_convert_type(x, jnp.uint32) & jnp.uint32(0xFFFF0000)
    hi = lax.bitcast_convert_type(bits, F32)
    return hi.astype(BF16), (x - hi).astype(BF16)


def _nt(a, b):
    return lax.dot_general(a, b, (((1,), (1,)), ((), ())), preferred_element_type=F32)


def _tn(a, b):
    return lax.dot_general(a, b, (((0,), (0,)), ((), ())), preferred_element_type=F32)


def _att_scores(q, k, mask, prescaled=False):
    z = _nt(q, k)
    if not prescaled:
        z = z * (HEAD_DIM ** -0.5)
    e = jnp.exp(-jnp.abs(z))
    sp = jnp.log(1.0 + e)
    lb = jnp.minimum(z, 0.0) - sp
    l1 = lb - z
    if mask is not None:
        lb = jnp.where(mask, lb, MASKED_LOG)
        l1 = jnp.where(mask, l1, 0.0)
    return z, lb, l1, e


def _col_to_row(col, eye):
    return jnp.sum(jnp.where(eye, col, 0.0), axis=0, keepdims=True)


def _row_to_col(row, eye):
    return jnp.sum(jnp.where(eye, row, 0.0), axis=1, keepdims=True)


def _wide_consts(T, W):
    r = lax.broadcasted_iota(jnp.int32, (W, W), 0)
    c = lax.broadcasted_iota(jnp.int32, (W, W), 1)
    two = lambda m: jnp.concatenate([m.astype(BF16)] * 2, axis=0)
    qrow = lax.broadcasted_iota(jnp.int32, (T, W), 0)
    kcol = lax.broadcasted_iota(jnp.int32, (T, W), 1)
    er = lax.broadcasted_iota(jnp.int32, (T, T), 0)
    ec = lax.broadcasted_iota(jnp.int32, (T, T), 1)
    return two(r > c), two(r <= c), two(r < c), qrow, kcol, er == ec


def _pair_masks(x, first):
    zero = jnp.zeros_like(x)
    return jnp.where(first, x, zero), jnp.where(first, zero, x)


def attn_fwd_pairs(qkv, shards=(), slots=(), *, B, S, name):
    D = qkv.shape[1] // 3
    H = D // HEAD_DIM
    T = ATT_BLOCK
    W = 2 * T
    nq = S // T
    G = _tile(H, (ATT_HEADS, 2))
    P = G // 2
    LW = 2 * HEAD_DIM * P
    nsec = D // LW
    n = len(shards)
    n_steps = B * nsec

    def body(*refs):
        q_ref, k_ref, v_ref = refs[:3]
        o_ref, l_ref = refs[3 + 2 * n:5 + 2 * n]
        step_id = pl.program_id(0) * nsec + pl.program_id(1)
        if n:
            gather = _ShardGather(refs[3:3 + n], refs[5 + 2 * n:5 + 3 * n], *refs[5 + 3 * n:])
            pl.when(step_id == 0)(gather.send)
            pl.when(step_id == n_steps - 1)(gather.forward)
        later2, _, _, qrow, kcol, eye = _wide_consts(T, W)
        blk = lax.broadcasted_iota(jnp.int32, (nq, T), 0)
        first_q = lax.broadcasted_iota(jnp.int32, (T, 2 * HEAD_DIM), 1) < HEAD_DIM
        first_k = lax.broadcasted_iota(jnp.int32, (W, 2 * HEAD_DIM), 1) < HEAD_DIM

        def lanes(j):
            return slice(j * 2 * HEAD_DIM, (j + 1) * 2 * HEAD_DIM)

        def step(qms, k0, st, mask):
            accs, runs = st
            parts, lbs, sums = [], [], []
            for g in range(G):
                _, lb, l1, _ = _att_scores(qms[g], k_ref[pl.ds(k0, W), lanes(g // 2)], mask, prescaled=True)
                parts.append(jnp.concatenate(_split2(l1), axis=1))
                lbs.append(lb)
                sums.append(jnp.sum(l1, axis=1, keepdims=True))
            suf = jnp.dot(jnp.concatenate(parts, axis=0), later2, preferred_element_type=F32)
            new_accs, new_runs = [], []
            for j in range(P):
                vms = _pair_masks(v_ref[pl.ds(k0, W), lanes(j)], first_k)
                acc = accs[j]
                for h in range(2):
                    g = 2 * j + h
                    w = jnp.exp(lbs[g] + suf[g * T:(g + 1) * T] + runs[g])
                    acc = acc + jnp.dot(w.astype(BF16), vms[h], preferred_element_type=F32)
                    new_runs.append(runs[g] + sums[g])
                new_accs.append(acc)
            return tuple(new_accs), tuple(new_runs)

        def qblock(i, totals):
            q0 = pl.multiple_of(i * T, T)
            qms = []
            for j in range(P):
                qms.extend(_pair_masks(q_ref[pl.ds(q0, T), lanes(j)] * (HEAD_DIM ** -0.5), first_q))
            half = jnp.right_shift(i, 1)
            last = half * W
            k_last = pl.multiple_of(last, W)
            mask = (k_last + kcol) < (q0 + qrow)
            st = (tuple(jnp.zeros((T, 2 * HEAD_DIM), F32) for _ in range(P)),
                  tuple(jnp.zeros((T, 1), F32) for _ in range(G)))
            st = step(qms, k_last, st, mask)

            def kblock(jj, st):
                return step(qms, pl.multiple_of(last - jj * W, W), st, None)

            accs, runs = lax.fori_loop(1, half + 1, kblock, st)
            for j in range(P):
                o_ref[pl.ds(q0, T), lanes(j)] = accs[j].astype(o_ref.dtype)
            return tuple(jnp.where(blk == i, _col_to_row(runs[g], eye), totals[g]) for g in range(G))

        totals = lax.fori_loop(0, nq, qblock, tuple(jnp.zeros((nq, T), F32) for _ in range(G)))
        for g in range(G):
            l_ref[g] = totals[g]
        if n:
            pl.when(step_id == n_steps - 1)(gather.finish)

    def cols(section):
        return pl.BlockSpec((S, LW), lambda b, h: (b, section * nsec + h))

    lspec = pl.BlockSpec((None, G, nq, T), lambda b, h: (b, h, 0, 0))
    dma = pltpu.SemaphoreType.DMA
    return pl.pallas_call(
        body, name=name,
        out_shape=[jax.ShapeDtypeStruct((B * S, D), BF16), jax.ShapeDtypeStruct((B, H, nq, T), F32)]
        + [jax.ShapeDtypeStruct(s.shape, s.dtype) for s in slots],
        grid=(B, nsec), in_specs=[cols(0), cols(1), cols(2)] + _any_specs(2 * n),
        out_specs=[cols(0), lspec] + _any_specs(n),
        input_output_aliases={3 + n + i: 2 + i for i in range(n)},
        scratch_shapes=[dma((3 * n,))] * 4 if n else [],
        compiler_params=_params(("arbitrary", "arbitrary")),
    )(qkv, qkv, qkv, *shards, *slots)


def attn_bwd_pairs(qkv, ltot, do, partials=(), *, B, S, name):
    D = qkv.shape[1] // 3
    H = D // HEAD_DIM
    T = ATT_BLOCK
    W = 2 * T
    nq = S // T
    scale = HEAD_DIM ** -0.5
    G = _tile(H, (ATT_HEADS_BWD, 2))
    P = G // 2
    LW = 2 * HEAD_DIM * P
    nsec = D // LW
    n = len(partials)
    n_steps = B * nsec

    def body(*refs):
        q_ref, k_ref, v_ref, l_ref, do_ref = refs[:5]
        d_ref = refs[5 + n]
        dk_acc, dv_acc = refs[6 + 2 * n:8 + 2 * n]
        step_id = pl.program_id(0) * nsec + pl.program_id(1)
        if n:
            scatter = _ChipScatter(refs[5:5 + n], refs[6 + n:6 + 2 * n], *refs[8 + 2 * n:])
            pl.when(step_id == 0)(scatter.send)
        _, upto2, before2, qrow, kcol, eye = _wide_consts(T, W)
        blk = lax.broadcasted_iota(jnp.int32, (nq, T), 0)
        first_q = lax.broadcasted_iota(jnp.int32, (T, 2 * HEAD_DIM), 1) < HEAD_DIM
        first_k = lax.broadcasted_iota(jnp.int32, (W, 2 * HEAD_DIM), 1) < HEAD_DIM
        dk_acc[...] = jnp.zeros_like(dk_acc)
        dv_acc[...] = jnp.zeros_like(dv_acc)

        def lanes(j):
            return slice(j * 2 * HEAD_DIM, (j + 1) * 2 * HEAD_DIM)

        def step(qms, doms, tots, k0, st, mask):
            dqs, runs_l, runs_d = st
            sc = []
            for g in range(G):
                z, lb, l1, _ = _att_scores(qms[g], k_ref[pl.ds(k0, W), lanes(g // 2)], mask, prescaled=True)
                beta = 0.5 * jnp.tanh(0.5 * z) + 0.5
                omb = 1.0 - beta
                if mask is not None:
                    beta = jnp.where(mask, beta, 0.0)
                dw = _nt(doms[g], v_ref[pl.ds(k0, W), lanes(g // 2)])
                sc.append((lb, jnp.concatenate(_split2(l1), axis=1), jnp.sum(l1, axis=1, keepdims=True), dw, beta, omb))
            pre = jnp.dot(jnp.concatenate([s[1] for s in sc], axis=0), upto2, preferred_element_type=F32)
            dlws = []
            for j in range(P):
                dv = None
                for h in range(2):
                    g = 2 * j + h
                    w = jnp.exp(sc[g][0] + (tots[g] - (pre[g * T:(g + 1) * T] + runs_l[g])))
                    t = _tn(w.astype(BF16), doms[g])
                    dv = t if dv is None else dv + t
                    dlws.append(sc[g][3] * w)
                dv_acc[j, pl.ds(k0, W), :] += dv
            pre_d = jnp.dot(jnp.concatenate([jnp.concatenate(_split2(d), axis=1) for d in dlws], axis=0), before2,
                            preferred_element_type=F32)
            new_dqs, new_l, new_d = [], [], []
            for j in range(P):
                kms = _pair_masks(k_ref[pl.ds(k0, W), lanes(j)], first_k)
                dq, dk = dqs[j], None
                for h in range(2):
                    g = 2 * j + h
                    _, _, rowsum, _, beta, omb = sc[g]
                    dl1 = pre_d[g * T:(g + 1) * T] + runs_d[g]
                    dz = (dlws[g] * omb - dl1 * beta).astype(BF16)
                    dq = dq + jnp.dot(dz, kms[h], preferred_element_type=F32)
                    t = _tn(dz, qms[g])
                    dk = t if dk is None else dk + t
                    new_l.append(runs_l[g] + rowsum)
                    new_d.append(runs_d[g] + jnp.sum(dlws[g], axis=1, keepdims=True))
                dk_acc[j, pl.ds(k0, W), :] += dk
                new_dqs.append(dq)
            return tuple(new_dqs), tuple(new_l), tuple(new_d)

        def qblock(i, carry0):
            q0 = pl.multiple_of(i * T, T)
            qms, doms = [], []
            for j in range(P):
                qms.extend(_pair_masks(q_ref[pl.ds(q0, T), lanes(j)] * scale, first_q))
                doms.extend(_pair_masks(do_ref[pl.ds(q0, T), lanes(j)], first_q))
            tots = [_row_to_col(jnp.sum(jnp.where(blk == i, l_ref[g], 0.0), axis=0, keepdims=True), eye)
                    for g in range(G)]
            z1 = tuple(jnp.zeros((T, 1), F32) for _ in range(G))
            st = (tuple(jnp.zeros((T, 2 * HEAD_DIM), F32) for _ in range(P)), z1, z1)

            def kblock(j, st):
                return step(qms, doms, tots, pl.multiple_of(j * W, W), st, None)

            half = jnp.right_shift(i, 1)
            st = lax.fori_loop(0, half, kblock, st)
            k_last = pl.multiple_of(half * W, W)
            dqs, _, _ = step(qms, doms, tots, k_last, st, (k_last + kcol) < (q0 + qrow))
            for j in range(P):
                d_ref[0, pl.ds(q0, T), lanes(j)] = (dqs[j] * scale).astype(d_ref.dtype)
            return carry0

        lax.fori_loop(0, nq, qblock, 0)
        for j in range(P):
            d_ref[1, :, lanes(j)] = dk_acc[j].astype(d_ref.dtype)
            d_ref[2, :, lanes(j)] = dv_acc[j].astype(d_ref.dtype)
        if n:
            pl.when(step_id == n_steps - 1)(scatter.finish)

    def cols(section):
        return pl.BlockSpec((S, LW), lambda b, h: (b, section * nsec + h))

    lspec = pl.BlockSpec((None, G, nq, T), lambda b, h: (b, h, 0, 0))
    dma = pltpu.SemaphoreType.DMA
    return pl.pallas_call(
        body, name=name,
        out_shape=[jax.ShapeDtypeStruct((3, B * S, D), BF16)]
        + [jax.ShapeDtypeStruct((N_CHIPS - 1,) + a.shape[1:], a.dtype) for a in partials],
        grid=(B, nsec),
        in_specs=[cols(0), cols(1), cols(2), lspec, cols(0)] + _any_specs(n),
        out_specs=[pl.BlockSpec((3, S, LW), lambda b, h: (0, b, h))] + _any_specs(n),
        scratch_shapes=[pltpu.VMEM((P, S, 2 * HEAD_DIM), F32), pltpu.VMEM((P, S, 2 * HEAD_DIM), F32)]
        + ([dma((3 * n,)), dma((3 * n,))] if n else []),
        compiler_params=_params(("arbitrary", "arbitrary")),
    )(qkv, qkv, qkv, ltot, do, *partials)


def _cmul(ar, ai, br, bi):
    return ar * br - ai * bi, ar * bi + ai * br


def _cpow(lr, li, n):
    rr, ri = None, None
    br, bi = lr, li
    while n:
        if n & 1:
            rr, ri = (br, bi) if rr is None else _cmul(rr, ri, br, bi)
        n >>= 1
        if n:
            br, bi = _cmul(br, bi, br, bi)
    return rr, ri


def _ssm_scan(sr, si, lr, li, n_steps, reverse):
    W = sr.shape[1]
    R = SEGMENTS
    lim = -li if reverse else li
    zero = jnp.zeros((R, W), F32)

    def row(k):
        i = (n_steps - 1 - k) if reverse else k
        return pl.multiple_of(i * R, R)

    def local(k, st):
        cr, ci = st
        r0 = row(k)
        pr, pi = _cmul(lr, lim, cr, ci)
        nr = pr + sr[pl.ds(r0, R), :]
        ni = pi + si[pl.ds(r0, R), :]
        sr[pl.ds(r0, R), :] = nr
        si[pl.ds(r0, R), :] = ni
        return nr, ni

    er, ei = lax.fori_loop(0, n_steps, local, (zero, zero), unroll=SCAN_UNROLL)
    lnr, lni = _cpow(lr, lim, n_steps)
    rows = lax.broadcasted_iota(jnp.int32, (R, W), 0)
    cr, ci = zero, zero
    for step in range(1, R):
        tr, ti = _cmul(lnr, lni, cr, ci)
        tr, ti = tr + er, ti + ei
        if reverse:
            seg = R - 1 - step
            tr, ti = pltpu.roll(tr, R - 1, 0), pltpu.roll(ti, R - 1, 0)
        else:
            seg = step
            tr, ti = pltpu.roll(tr, 1, 0), pltpu.roll(ti, 1, 0)
        cr = jnp.where(rows == seg, tr, cr)
        ci = jnp.where(rows == seg, ti, ci)

    def fix(k, st):
        pr, pi = st
        r0 = row(k)
        ar, ai = _cmul(pr, pi, cr, ci)
        sr[pl.ds(r0, R), :] += ar
        si[pl.ds(r0, R), :] += ai
        return _cmul(lr, lim, pr, pi)

    lax.fori_loop(0, n_steps, fix, (lr, lim), unroll=SCAN_UNROLL)
    return cr, ci


def _ssm_specs(S, W):
    CH = GROUPS_PER_BLOCK * SSM_GROUP
    return dict(
        rows=pl.BlockSpec((S, CH), lambda b, j: (b, j)),
        b=pl.BlockSpec((None, CH, W), lambda b, j: (j, 0, 0)),
        c=pl.BlockSpec((None, W, CH), lambda b, j: (j, 0, 0)),
        lam=pl.BlockSpec((None, SEGMENTS, W), lambda b, j: (j, 0, 0)),
        vec=pl.BlockSpec((1, CH), lambda b, j: (0, j)),
    )


def ssm_fwd(u, bre, bim, cre, cim, lr8, li8, dsk, *, B, S, name):
    D = u.shape[1]
    J, CH, W = bre.shape
    n_steps = S // SEGMENTS
    sp = _ssm_specs(S, W)

    def body(u_ref, bre_ref, bim_ref, cre_ref, cim_ref, lr_ref, li_ref, dsk_ref, y_ref, sr, si):
        u = u_ref[...]
        ub = u.astype(BF16)
        sr[...] = jnp.dot(ub, bre_ref[...], preferred_element_type=F32)
        si[...] = jnp.dot(ub, bim_ref[...], preferred_element_type=F32)
        _ssm_scan(sr, si, lr_ref[...], li_ref[...], n_steps, False)
        y = jnp.dot(sr[...].astype(BF16), cre_ref[...], preferred_element_type=F32)
        y = y - jnp.dot(si[...].astype(BF16), cim_ref[...], preferred_element_type=F32)
        y_ref[...] = y + dsk_ref[...] * u

    return pl.pallas_call(
        body, name=name, out_shape=jax.ShapeDtypeStruct((B * S, D), F32), grid=(B, J),
        in_specs=[sp["rows"], sp["b"], sp["b"], sp["c"], sp["c"], sp["lam"], sp["lam"], sp["vec"]],
        out_specs=sp["rows"],
        scratch_shapes=[pltpu.VMEM((S, W), F32), pltpu.VMEM((S, W), F32)],
        compiler_params=_params(("parallel", "parallel")),
    )(u, bre, bim, cre, cim, lr8, li8, dsk)


def ssm_bwd(u, dy, bre, bim, cre, cim, lr8, li8, dsk, *, B, S, name):
    D = u.shape[1]
    J, CH, W = bre.shape
    n_steps = S // SEGMENTS
    sp = _ssm_specs(S, W)

    def body(u_ref, dy_ref, bre_ref, bim_ref, cre_ref, cim_ref, lr_ref, li_ref, dsk_ref,
             du_ref, dbre_ref, dbim_ref, dcre_ref, dcim_ref, dlr_ref, dli_ref, ddsk_ref, sr, si, ar, ai):
        u = u_ref[...]
        dy = dy_ref[...]
        ub = u.astype(BF16)
        dyb = dy.astype(BF16)
        lr, li = lr_ref[...], li_ref[...]
        sr[...] = jnp.dot(ub, bre_ref[...], preferred_element_type=F32)
        si[...] = jnp.dot(ub, bim_ref[...], preferred_element_type=F32)
        cr, ci = _ssm_scan(sr, si, lr, li, n_steps, False)
        ar[...] = _nt(dyb, cre_ref[...])
        ai[...] = -_nt(dyb, cim_ref[...])
        _ssm_scan(ar, ai, lr, li, n_steps, True)

        def dlam(k, st):
            dr, di = st
            r0 = pl.multiple_of((k + 1) * SEGMENTS, SEGMENTS)
            p0 = pl.multiple_of(k * SEGMENTS, SEGMENTS)
            pr, pi = sr[pl.ds(p0, SEGMENTS), :], si[pl.ds(p0, SEGMENTS), :]
            xr, xi = ar[pl.ds(r0, SEGMENTS), :], ai[pl.ds(r0, SEGMENTS), :]
            return dr + pr * xr + pi * xi, di + pr * xi - pi * xr

        xr, xi = ar[0:SEGMENTS, :], ai[0:SEGMENTS, :]
        dr, di = lax.fori_loop(0, n_steps - 1, dlam, (cr * xr + ci * xi, cr * xi - ci * xr), unroll=SCAN_UNROLL)
        dlr_ref[...] = dr
        dli_ref[...] = di
        arb = ar[...].astype(BF16)
        aib = ai[...].astype(BF16)
        du_ref[...] = _nt(arb, bre_ref[...]) + _nt(aib, bim_ref[...]) + dsk_ref[...] * dy
        dbre_ref[...] = _tn(ub, arb)
        dbim_ref[...] = _tn(ub, aib)
        dcre_ref[...] = _tn(sr[...].astype(BF16), dyb)
        dcim_ref[...] = -_tn(si[...].astype(BF16), dyb)
        ddsk_ref[...] = jnp.sum(dy * u, axis=0, keepdims=True)

    def per(shape):
        return pl.BlockSpec((None, None) + shape, lambda b, j: (b, j, 0, 0))

    return pl.pallas_call(
        body, name=name,
        out_shape=[jax.ShapeDtypeStruct((B * S, D), F32),
                   jax.ShapeDtypeStruct((B, J, CH, W), F32), jax.ShapeDtypeStruct((B, J, CH, W), F32),
                   jax.ShapeDtypeStruct((B, J, W, CH), F32), jax.ShapeDtypeStruct((B, J, W, CH), F32),
                   jax.ShapeDtypeStruct((B, J, SEGMENTS, W), F32), jax.ShapeDtypeStruct((B, J, SEGMENTS, W), F32),
                   jax.ShapeDtypeStruct((B, J, 1, CH), F32)],
        grid=(B, J),
        in_specs=[sp["rows"], sp["rows"], sp["b"], sp["b"], sp["c"], sp["c"], sp["lam"], sp["lam"], sp["vec"]],
        out_specs=[sp["rows"], per((CH, W)), per((CH, W)), per((W, CH)), per((W, CH)), per((SEGMENTS, W)),
                   per((SEGMENTS, W)),
                   per((1, CH))],
        scratch_shapes=[pltpu.VMEM((S, W), F32)] * 4,
        compiler_params=_params(("parallel", "parallel")),
    )(u, dy, bre, bim, cre, cim, lr8, li8, dsk)


def _ssm_discretize(a_re, a_im, log_dt, b_re, b_im):
    dt = jnp.exp(log_dt)[:, None]
    er = jnp.exp(a_re * dt)
    lr = er * jnp.cos(a_im * dt)
    li = er * jnp.sin(a_im * dt)
    den = a_re * a_re + a_im * a_im
    fr = ((lr - 1.0) * a_re + li * a_im) / den
    fi = (li * a_re - (lr - 1.0) * a_im) / den
    bbr = fr[..., None] * b_re - fi[..., None] * b_im
    bbi = fr[..., None] * b_im + fi[..., None] * b_re
    return lr, li, bbr, bbi


def _block_diag_in(m):
    G, P, H = m.shape
    J = G // GROUPS_PER_BLOCK
    m = m.reshape(J, GROUPS_PER_BLOCK, P, H).transpose(0, 1, 3, 2)
    eye = jnp.eye(GROUPS_PER_BLOCK, dtype=m.dtype)
    out = m[:, :, :, None, :] * eye[None, :, None, :, None]
    return out.reshape(J, GROUPS_PER_BLOCK * H, GROUPS_PER_BLOCK * P)


def _block_diag_in_grad(d, G, P, H):
    J = G // GROUPS_PER_BLOCK
    d = d.reshape(J, GROUPS_PER_BLOCK, H, GROUPS_PER_BLOCK, P)
    idx = jnp.arange(GROUPS_PER_BLOCK)
    d = d[:, idx, :, idx, :]
    return d.transpose(1, 0, 3, 2).reshape(G, P, H)


def _block_diag_out(m):
    G, H, P = m.shape
    J = G // GROUPS_PER_BLOCK
    m = m.reshape(J, GROUPS_PER_BLOCK, H, P).transpose(0, 1, 3, 2)
    eye = jnp.eye(GROUPS_PER_BLOCK, dtype=m.dtype)
    out = m[:, :, :, None, :] * eye[None, :, None, :, None]
    return out.reshape(J, GROUPS_PER_BLOCK * P, GROUPS_PER_BLOCK * H)


def _block_diag_out_grad(d, G, H, P):
    J = G // GROUPS_PER_BLOCK
    d = d.reshape(J, GROUPS_PER_BLOCK, P, GROUPS_PER_BLOCK, H)
    idx = jnp.arange(GROUPS_PER_BLOCK)
    d = d[:, idx, :, idx, :]
    return d.transpose(1, 0, 3, 2).reshape(G, H, P)


def _interleave(a, B, S):
    L = S // SEGMENTS
    return a.reshape(B, SEGMENTS, L, a.shape[-1]).transpose(0, 2, 1, 3).reshape(B * S, a.shape[-1])


def _deinterleave(a, B, S):
    L = S // SEGMENTS
    return a.reshape(B, L, SEGMENTS, a.shape[-1]).transpose(0, 2, 1, 3).reshape(B * S, a.shape[-1])


def _adamw_math(w, g, m, v):
    m = ADAM_B1 * m + (1.0 - ADAM_B1) * g
    v = ADAM_B2 * v + (1.0 - ADAM_B2) * (g * g)
    m_hat = m / (1.0 - ADAM_B1 ** ADAM_STEP)
    v_hat = v / (1.0 - ADAM_B2 ** ADAM_STEP)
    delta = -ADAM_LR * (m_hat / (jnp.sqrt(v_hat) + ADAM_EPS) + ADAM_WD * w)
    return delta, m, v


def adamw(w, g, m, v, *, name):
    R, C = w.shape
    tr = _tile(R, (max(8, (1 << 18) // C // 8 * 8), 256, 128, 64, 32, 16, 8))

    def body(w_ref, g_ref, m_ref, v_ref, d_ref, nm_ref, nv_ref):
        d, nm, nv = _adamw_math(w_ref[...], g_ref[...], m_ref[...], v_ref[...])
        d_ref[...] = d
        nm_ref[...] = nm
        nv_ref[...] = nv

    spec = pl.BlockSpec((tr, C), lambda i: (i, 0))
    shp = jax.ShapeDtypeStruct((R, C), F32)
    return pl.pallas_call(
        body, name=name, out_shape=[shp, shp, shp], grid=(R // tr,), in_specs=[spec] * 4, out_specs=[spec] * 3,
        compiler_params=_params(("parallel",)),
    )(w, g, m, v)


def adamw_many(ws, gs, ms, vs, *, name):
    n = len(ws)
    at_least_2d = lambda a: a.reshape(1, -1) if a.ndim == 1 else a
    args = [at_least_2d(a) for group in (ws, gs, ms, vs) for a in group]

    def body(*refs):
        for i in range(n):
            d, nm, nv = _adamw_math(refs[i][...], refs[n + i][...], refs[2 * n + i][...], refs[3 * n + i][...])
            refs[4 * n + 3 * i][...] = d
            refs[4 * n + 3 * i + 1][...] = nm
            refs[4 * n + 3 * i + 2][...] = nv

    out = pl.pallas_call(
        body, name=name, out_shape=[jax.ShapeDtypeStruct(a.shape, F32) for a in args[:n] for _ in range(3)],
        in_specs=[pl.BlockSpec(memory_space=pltpu.VMEM) for _ in args],
        out_specs=[pl.BlockSpec(memory_space=pltpu.VMEM) for _ in range(3 * n)],
        compiler_params=pltpu.CompilerParams(vmem_limit_bytes=V7X_VMEM_LIMIT),
    )(*args)
    return [tuple(o.reshape(w.shape) for o in out[3 * i:3 * i + 3]) for i, w in enumerate(ws)]


def _any_specs(n):
    return [pl.BlockSpec(memory_space=pl.ANY) for _ in range(n)]


def _coords():
    return lax.axis_index("x"), lax.axis_index("y"), lax.axis_index("c")


def _flip(v, bit):
    return (v + bit) % 2


def all_gather8(a, *, name):
    shape = a.shape

    def body(a_ref, o_ref, send_sems, recv_sems, local_sem, buf):
        x, y, c = _coords()
        me = 4 * x + 2 * y + c
        load = pltpu.make_async_copy(a_ref, buf, local_sem)
        load.start()
        sends = []
        for k in range(1, N_DEV):
            peer = (_flip(x, (k >> 2) & 1), _flip(y, (k >> 1) & 1), _flip(c, k & 1))
            cp = pltpu.make_async_remote_copy(a_ref, o_ref.at[me], send_sems.at[k - 1], recv_sems.at[k - 1],
                                              device_id=peer, device_id_type=MESH)
            cp.start()
            sends.append(cp)
        load.wait()
        mine = pltpu.make_async_copy(buf, o_ref.at[me], local_sem)
        mine.start()
        for k in range(1, N_DEV):
            px, py, pc = _flip(x, (k >> 2) & 1), _flip(y, (k >> 1) & 1), _flip(c, k & 1)
            src = 4 * px + 2 * py + pc
            pltpu.make_async_remote_copy(a_ref, o_ref.at[src], send_sems.at[k - 1], recv_sems.at[k - 1],
                                         device_id=(px, py, pc), device_id_type=MESH).wait_recv()
        for cp in sends:
            cp.wait_send()
        mine.wait()

    return pl.pallas_call(
        body, name=name, out_shape=jax.ShapeDtypeStruct((N_DEV,) + shape, a.dtype),
        in_specs=_any_specs(1), out_specs=pl.BlockSpec(memory_space=pl.ANY),
        scratch_shapes=[pltpu.SemaphoreType.DMA((N_DEV - 1,)), pltpu.SemaphoreType.DMA((N_DEV - 1,)),
                        pltpu.SemaphoreType.DMA(()), pltpu.VMEM(shape, a.dtype)],
    )(a)


def _chip_of(x, y, p):
    px, py = _flip(x, (p >> 1) & 1), _flip(y, p & 1)
    return 2 * px + py, px, py


class _ShardGather:
    def __init__(self, ins, outs, ici_send, ici_recv, d2d_send, d2d_recv):
        self.ins, self.outs = ins, outs
        self.sems = ici_send, ici_recv, d2d_send, d2d_recv
        self.x, self.y, self.c = _coords()
        self.me = 2 * self.x + self.y

    def _ici(self, i, p, slot):
        half = self.ins[i].shape[0] // 2
        rows = pl.ds(self.c * half, half)
        _, px, py = _chip_of(self.x, self.y, p)
        s = i * 3 + p - 1
        return pltpu.make_async_remote_copy(self.ins[i].at[rows], self.outs[i].at[slot, rows], self.sems[0].at[s],
                                            self.sems[1].at[s], device_id=(px, py, self.c), device_id_type=MESH)

    def _d2d(self, i, p, mine):
        half = self.ins[i].shape[0] // 2
        rows = pl.ds((self.c if mine else 1 - self.c) * half, half)
        src, _, _ = _chip_of(self.x, self.y, p)
        s = i * 3 + p - 1
        part = self.outs[i].at[src, rows]
        return pltpu.make_async_remote_copy(part, part, self.sems[2].at[s], self.sems[3].at[s],
                                            device_id=(self.x, self.y, 1 - self.c), device_id_type=MESH)

    def _each(self):
        return [(i, p) for i in range(len(self.ins)) for p in range(1, N_CHIPS)]

    def send(self):
        for i, p in self._each():
            self._ici(i, p, self.me).start()

    def forward(self):
        for i, p in self._each():
            self._ici(i, p, _chip_of(self.x, self.y, p)[0]).wait_recv()
            self._d2d(i, p, True).start()

    def finish(self):
        for i, p in self._each():
            self._d2d(i, p, False).wait_recv()
        for i, p in self._each():
            self._ici(i, p, self.me).wait_send()
            self._d2d(i, p, True).wait_send()


def gather_chip_shards(arrs, remote, *, name):
    n = len(arrs)
    far = [i for i in range(n) if remote[i]]

    def body(*refs):
        ins, outs = refs[:n], refs[n:2 * n]
        ici_send, ici_recv, d2d_send, d2d_recv, local_sems = refs[2 * n:2 * n + 5]
        bufs = refs[2 * n + 5:]
        me = 2 * lax.axis_index("x") + lax.axis_index("y")
        loads = []
        for i in range(n):
            cp = pltpu.make_async_copy(ins[i], bufs[i], local_sems.at[i])
            cp.start()
            loads.append(cp)
        gather = _ShardGather([ins[i] for i in far], [outs[i] for i in far], ici_send, ici_recv, d2d_send, d2d_recv)
        gather.send()
        stores = []
        for i in range(n):
            loads[i].wait()
            cp = pltpu.make_async_copy(bufs[i], outs[i].at[me], local_sems.at[i])
            cp.start()
            stores.append(cp)
        gather.forward()
        gather.finish()
        for cp in stores:
            cp.wait()

    dma = pltpu.SemaphoreType.DMA
    m = 3 * len(far)
    return pl.pallas_call(
        body, name=name,
        out_shape=[jax.ShapeDtypeStruct((N_CHIPS,) + a.shape, a.dtype) for a in arrs],
        in_specs=_any_specs(n), out_specs=_any_specs(n),
        scratch_shapes=[dma((m,)), dma((m,)), dma((m,)), dma((m,)), dma((n,))]
        + [pltpu.VMEM(a.shape, a.dtype) for a in arrs],
        compiler_params=pltpu.CompilerParams(vmem_limit_bytes=V7X_VMEM_LIMIT),
    )(*arrs)


def swap_halves(arrs, *, name):
    n = len(arrs)

    def body(*refs):
        ins, outs = refs[:n], refs[n:2 * n]
        send_sems, recv_sems = refs[2 * n:]
        x, y, c = _coords()
        cps = []
        for i in range(n):
            half = ins[i].shape[1] // 2
            cp = pltpu.make_async_remote_copy(ins[i].at[:, pl.ds((1 - c) * half, half)], outs[i], send_sems.at[i],
                                              recv_sems.at[i], device_id=(x, y, 1 - c), device_id_type=MESH)
            cp.start()
            cps.append(cp)
        for cp in cps:
            cp.wait()

    dma = pltpu.SemaphoreType.DMA
    return pl.pallas_call(
        body, name=name,
        out_shape=[jax.ShapeDtypeStruct((N_CHIPS, a.shape[1] // 2, a.shape[2]), a.dtype) for a in arrs],
        in_specs=_any_specs(n), out_specs=_any_specs(n), scratch_shapes=[dma((n,)), dma((n,))],
    )(*arrs)


def add_half(g, other, c_idx, *, name, out_dtype):
    _, R, C = g.shape
    half = R // 2
    tr = _row_tile(half)
    nt = half // tr

    def body(c_ref, g_ref, o_ref, out_ref):
        out_ref[...] = (g_ref[...].astype(F32) + o_ref[...].astype(F32)).astype(out_ref.dtype)

    return pl.pallas_call(
        body, name=name, out_shape=jax.ShapeDtypeStruct((N_CHIPS, half, C), out_dtype),
        grid_spec=pltpu.PrefetchScalarGridSpec(
            num_scalar_prefetch=1, grid=(N_CHIPS, nt),
            in_specs=[pl.BlockSpec((None, tr, C), lambda r, t, c_ref: (r, c_ref[0] * nt + t, 0)),
                      pl.BlockSpec((None, tr, C), lambda r, t, c_ref: (r, t, 0))],
            out_specs=pl.BlockSpec((None, tr, C), lambda r, t, c_ref: (r, t, 0))),
        compiler_params=_params(("parallel", "parallel")),
    )(c_idx, g, other)


class _ChipScatter:
    def __init__(self, ins, outs, send_sems, recv_sems):
        self.ins, self.outs, self.send_sems, self.recv_sems = ins, outs, send_sems, recv_sems
        self.x, self.y, self.c = _coords()

    def _copies(self):
        for i in range(len(self.ins)):
            for p in range(1, N_CHIPS):
                dst, px, py = _chip_of(self.x, self.y, p)
                s = i * 3 + p - 1
                yield pltpu.make_async_remote_copy(self.ins[i].at[dst], self.outs[i].at[p - 1], self.send_sems.at[s],
                                                   self.recv_sems.at[s], device_id=(px, py, self.c), device_id_type=MESH)

    def send(self):
        for cp in self._copies():
            cp.start()

    def finish(self):
        for cp in self._copies():
            cp.wait()


def scatter_to_chips(arrs, *, name):
    n = len(arrs)

    def body(*refs):
        scatter = _ChipScatter(refs[:n], refs[n:2 * n], *refs[2 * n:])
        scatter.send()
        scatter.finish()

    dma = pltpu.SemaphoreType.DMA
    return pl.pallas_call(
        body, name=name,
        out_shape=[jax.ShapeDtypeStruct((N_CHIPS - 1,) + a.shape[1:], a.dtype) for a in arrs],
        in_specs=_any_specs(n), out_specs=_any_specs(n), scratch_shapes=[dma((3 * n,)), dma((3 * n,))],
    )(*arrs)


def add_chips(h, got, r_idx, *, name):
    _, R, C = h.shape
    tr = _row_tile(R)

    def body(r_ref, h_ref, g_ref, out_ref):
        acc = h_ref[...].astype(F32)
        for p in range(N_CHIPS - 1):
            acc = acc + g_ref[p].astype(F32)
        out_ref[...] = acc

    return pl.pallas_call(
        body, name=name, out_shape=jax.ShapeDtypeStruct((R, C), F32),
        grid_spec=pltpu.PrefetchScalarGridSpec(
            num_scalar_prefetch=1, grid=(R // tr,),
            in_specs=[pl.BlockSpec((None, tr, C), lambda t, r_ref: (r_ref[0], t, 0)),
                      pl.BlockSpec((N_CHIPS - 1, tr, C), lambda t, r_ref: (0, t, 0))],
            out_specs=pl.BlockSpec((tr, C), lambda t, r_ref: (t, 0))),
        compiler_params=_params(("parallel",)),
    )(r_idx, h, got)


def join_halves(arrs, *, name):
    n = len(arrs)

    def body(*refs):
        ins, outs = refs[:n], refs[n:2 * n]
        send_sems, recv_sems, local_sems = refs[2 * n:2 * n + 3]
        bufs = refs[2 * n + 3:]
        x, y, c = _coords()
        loads, sends, stores = [], [], []
        for i in range(n):
            cp = pltpu.make_async_copy(ins[i], bufs[i], local_sems.at[i])
            cp.start()
            loads.append(cp)
        for i in range(n):
            half = ins[i].shape[0]
            cp = pltpu.make_async_remote_copy(ins[i], outs[i].at[pl.ds(c * half, half)], send_sems.at[i], recv_sems.at[i],
                                              device_id=(x, y, 1 - c), device_id_type=MESH)
            cp.start()
            sends.append(cp)
        for i in range(n):
            half = ins[i].shape[0]
            loads[i].wait()
            cp = pltpu.make_async_copy(bufs[i], outs[i].at[pl.ds(c * half, half)], local_sems.at[i])
            cp.start()
            stores.append(cp)
        for i in range(n):
            half = ins[i].shape[0]
            pltpu.make_async_remote_copy(ins[i], outs[i].at[pl.ds((1 - c) * half, half)], send_sems.at[i],
                                         recv_sems.at[i], device_id=(x, y, 1 - c), device_id_type=MESH).wait_recv()
        for i in range(n):
            sends[i].wait_send()
            stores[i].wait()

    dma = pltpu.SemaphoreType.DMA
    return pl.pallas_call(
        body, name=name,
        out_shape=[jax.ShapeDtypeStruct((2 * a.shape[0], a.shape[1]), a.dtype) for a in arrs],
        in_specs=_any_specs(n), out_specs=_any_specs(n),
        scratch_shapes=[dma((n,)), dma((n,)), dma((n,))] + [pltpu.VMEM(a.shape, a.dtype) for a in arrs],
        compiler_params=pltpu.CompilerParams(vmem_limit_bytes=V7X_VMEM_LIMIT),
    )(*arrs)


def pair_sums(grads, wire_dtypes, tag):
    c_idx = jnp.reshape(lax.axis_index("c"), (1,)).astype(jnp.int32)
    theirs = swap_halves(grads, name=f"rs_swap_halves_{tag}")
    return [add_half(g, o, c_idx, name=f"rs_add_half_{tag}{i}", out_dtype=wire_dtypes[i])
            for i, (g, o) in enumerate(zip(grads, theirs))]


def chip_sums(pairs, gots, tag):
    r_idx = jnp.reshape(2 * lax.axis_index("x") + lax.axis_index("y"), (1,)).astype(jnp.int32)
    return [add_chips(h, g, r_idx, name=f"rs_add_chips_{tag}{i}") for i, (h, g) in enumerate(zip(pairs, gots))]


def _chip_major(w, axis):
    n = w.shape[axis] // N_CHIPS
    parts = w.reshape(w.shape[:axis] + (N_CHIPS, n) + w.shape[axis + 1:])
    return jnp.moveaxis(parts, axis, 0)


def _from_chip_major(g, axis):
    g = jnp.moveaxis(g, 0, axis)
    return g.reshape(g.shape[:axis] + (g.shape[axis] * g.shape[axis + 1],) + g.shape[axis + 2:])


def kernel(x, c, norm_mix, norm_ffn, w_mod, b_mod, w_qkv, w_o_attn, w_in_ssm, a_re, a_im, log_dt, b_re, b_im, c_re, c_im, d_skip, w_glu, b_glu, w_o_ssm, w_up, conv_w, conv_b, w_down, norm_out, w_fin, b_fin, loss_target, m_norm_mix, m_norm_ffn, m_w_mod, m_b_mod, m_w_qkv, m_w_o_attn, m_w_in_ssm, m_a_re, m_a_im, m_log_dt, m_b_re, m_b_im, m_c_re, m_c_im, m_d_skip, m_w_glu, m_b_glu, m_w_o_ssm, m_w_up, m_conv_w, m_conv_b, m_w_down, m_norm_out, m_w_fin, m_b_fin, v_norm_mix, v_norm_ffn, v_w_mod, v_b_mod, v_w_qkv, v_w_o_attn, v_w_in_ssm, v_a_re, v_a_im, v_log_dt, v_b_re, v_b_im, v_c_re, v_c_im, v_d_skip, v_w_glu, v_b_glu, v_w_o_ssm, v_w_up, v_conv_w, v_conv_b, v_w_down, v_norm_out, v_w_fin, v_b_fin):
    B, S, D = x.shape
    T = B * S
    F2 = conv_b.shape[1]
    F = F2 // 2
    G, P = a_re.shape[1], a_re.shape[2]
    H = b_re.shape[3]
    mx, my, mc = _coords()
    chip = 2 * mx + my
    dev = 4 * mx + 2 * my + mc
    BG = N_DEV * B
    mod_w = w_mod.shape[2]
    fin_w = w_fin.shape[1]

    c_all = all_gather8(c, name="gather_c").reshape(BG, D)
    c_act = silu_rows(c_all, name="silu_c")
    b_mod_mine = lax.dynamic_slice(b_mod, (0, chip * mod_w), (2, mod_w))
    b_fin_mine = lax.dynamic_slice(b_fin, (chip * fin_w,), (fin_w,))
    cond = [matmul(c_act, w_mod[i], bias=b_mod_mine[i], name=f"mod_proj_{i}") for i in range(2)]
    cond.append(matmul(c_act, w_fin, bias=b_fin_mine, name="fin_proj"))
    cond_all = all_gather8(jnp.concatenate(cond, axis=1), name="gather_cond")
    cond_all = cond_all[::2]
    cond_rows = lax.dynamic_slice(cond_all, (0, dev * B, 0), (N_CHIPS, B, cond_all.shape[2]))
    mods = []
    for i in range(2):
        full = cond_rows[:, :, i * mod_w:(i + 1) * mod_w].transpose(1, 0, 2).reshape(B, N_CHIPS * mod_w)
        mods.append([full[:, k * D:(k + 1) * D] for k in range(6)])
    fin = cond_rows[:, :, 2 * mod_w:].transpose(1, 0, 2).reshape(B, N_CHIPS * fin_w)
    sh_f, sc_f = fin[:, :D], fin[:, D:]

    rows1024 = jnp.concatenate([w_o_attn[0], w_in_ssm[0], w_glu[0], w_o_ssm[0], w_down.reshape(-1, D)], axis=0)
    shards = [w_qkv[0].astype(BF16), rows1024.astype(BF16), w_up[0].astype(BF16), w_up[1].astype(BF16)]
    W_qkv, *own_slots = gather_chip_shards(shards, [True, False, False, False], name="gather_weights")
    Dq = D // N_CHIPS
    Fq = F // N_CHIPS
    small = jnp.concatenate([conv_w.reshape(6, -1), jnp.pad(d_skip, ((0, 0), (0, conv_w.shape[2] - Dq))),
                             jnp.pad(b_glu, ((0, 0), (0, conv_w.shape[2] - Dq)))], axis=0)
    small_all = all_gather8(small, name="gather_small")[::2]
    conv_w_full = _from_chip_major(small_all[:, :6].reshape(N_CHIPS, 2, 3, -1), 2)
    d_skip_full = small_all[:, 6, :Dq].reshape(1, D)
    b_glu_full = small_all[:, 7, :Dq].reshape(D)

    x0 = x.reshape(T, D)
    tgt = loss_target.reshape(T, D)

    def ffn_fwd(xprev, y, gate, i):
        sh2, sc2 = mods[i][3], mods[i][4]
        xin, h2 = res_norm_mod_fwd(xprev, y, gate, norm_ffn[i], sh2, sc2, B=B, S=S, name=f"ffn_norm_{i}")
        up = matmul(h2, W_up[i], b_chips=True, out_dtype=BF16, name=f"ffn_up_{i}")
        act = conv_gate_fwd(up, conv_w_full[i], conv_b[i:i + 1], B=B, S=S, name=f"ffn_conv_{i}")
        yf = matmul(act, W_down[i], name=f"ffn_down_{i}")
        return xin, yf, (xin, h2, up, act, yf)

    sh1, sc1, g1 = mods[0][0], mods[0][1], mods[0][2]
    h1a = norm_mod_fwd(x0, norm_mix[0], sh1, sc1, B=B, S=S, name="att_norm")
    qkv = matmul(h1a, W_qkv, out_dtype=BF16, b_chips=True, name="att_qkv")
    o2, ltot, g_rows, W_up0, W_up1 = attn_fwd_pairs(qkv, shards[1:], own_slots, B=B, S=S, name="att_fwd")
    W_up = [W_up0, W_up1]
    W_o_attn = g_rows[:, 0 * Dq:1 * Dq].reshape(D, D)
    W_in = g_rows[:, 1 * Dq:2 * Dq].reshape(D, D)
    W_glu = g_rows[:, 2 * Dq:3 * Dq].reshape(D, D)
    W_o_ssm = g_rows[:, 3 * Dq:4 * Dq].reshape(D, D)
    W_down = [g_rows[:, 4 * Dq + i * Fq:4 * Dq + (i + 1) * Fq].reshape(F, D) for i in range(2)]
    ya = matmul(o2, W_o_attn, name="att_out")
    x1, yf0, ffn0 = ffn_fwd(x0, ya, g1, 0)

    lr, li, bbr, bbi = _ssm_discretize(a_re[0], a_im[0], log_dt[0], b_re[0], b_im[0])
    J = G // GROUPS_PER_BLOCK
    Wst = GROUPS_PER_BLOCK * P
    bre_blk = _block_diag_in(bbr).astype(BF16)
    bim_blk = _block_diag_in(bbi).astype(BF16)
    cre_blk = _block_diag_out(c_re[0]).astype(BF16)
    cim_blk = _block_diag_out(c_im[0]).astype(BF16)
    lr8 = jnp.broadcast_to(lr.reshape(J, 1, Wst), (J, SEGMENTS, Wst))
    li8 = jnp.broadcast_to(li.reshape(J, 1, Wst), (J, SEGMENTS, Wst))
    sh1s, sc1s, g1s = mods[1][0], mods[1][1], mods[1][2]
    x2, h1s = res_norm_mod_fwd(x1, yf0, mods[0][5], norm_mix[1], sh1s, sc1s, B=B, S=S, name="ssm_norm")
    h1p = _interleave(h1s, B, S)
    u = matmul(h1p, W_in, name="ssm_in")
    y_ssm = ssm_fwd(u, bre_blk, bim_blk, cre_blk, cim_blk, lr8, li8, d_skip_full, B=B, S=S, name="ssm_scan_fwd")
    zb = gelu_fwd(y_ssm, B=B, S=S, name="ssm_gelu")
    s_glu = matmul(zb, W_glu, bias=b_glu_full, name="ssm_glu_proj")
    gb = glu_fwd(y_ssm, s_glu, B=B, S=S, name="ssm_glu")
    ys_p = matmul(gb, W_o_ssm, name="ssm_out")
    ys = _deinterleave(ys_p, B, S)
    x3, yf1, ffn1 = ffn_fwd(x2, ys, g1s, 1)
    x4 = gate_res_fwd(x3, yf1, mods[1][5], B=B, S=S, name="ffn_res_1")

    dx4, dyf1, loss_p, dsh_f, dsc_f, dnorm_out, dg2_1 = final_loss(x4, tgt, norm_out, sh_f, sc_f, yf1, mods[1][5],
                                                                   B=B, S=S, name="loss_head")
    loss = lax.psum(jnp.sum(loss_p), ("x", "y", "c"))

    def ffn_bwd(dxo, dyf, i, saved, y_prev, gate_prev):
        xin, h2, up, act, yf = saved
        sc2 = mods[i][4]
        dact = matmul(dyf, W_down[i], tb=True, out_dtype=BF16, name=f"ffn_down_dx_{i}")
        dW_down = matmul(act, dyf, ta=True, out_dtype=BF16, name=f"ffn_down_dw_{i}")
        dup, dcw, dcb = conv_gate_bwd(up, dact, conv_w_full[i], conv_b[i:i + 1], B=B, S=S, name=f"ffn_conv_bwd_{i}")
        dh2 = matmul(dup, W_up[i], tb=True, b_chips=True, name=f"ffn_up_dx_{i}")
        dW_up = matmul(h2, dup, ta=True, b_chips=True, out_chips=True, out_dtype=BF16, name=f"ffn_up_dw_{i}")
        dxin, dy_prev, dsh2, dsc2, dnf, dgate_prev = norm_mod_bwd_gate(
            dh2, xin, dxo, norm_ffn[i], sc2, y_prev, gate_prev, B=B, S=S, name=f"ffn_norm_bwd_{i}")
        dconv_w = jnp.sum(dcw, axis=0).transpose(1, 0, 2).reshape(3, F2)
        return dxin, dy_prev, dgate_prev, dict(dW_down=dW_down, dW_up=dW_up, dconv_b=jnp.sum(dcb, axis=0).reshape(F2),
                                               dconv_w=dconv_w, dnorm_ffn=jnp.sum(dnf, axis=0), dsh2=dsh2, dsc2=dsc2)

    dx3, dys, dg1s, gf1 = ffn_bwd(dx4, dyf1, 1, ffn1, ys, g1s)
    gf1["dg2"] = dg2_1

    dys_p = _interleave(dys, B, S)
    dgb = matmul(dys_p, W_o_ssm, tb=True, name="ssm_out_dx")
    dW_o_ssm = matmul(gb, dys_p, ta=True, out_dtype=BF16, name="ssm_out_dw")
    ds_glu, dz1, db_glu = glu_bwd1(y_ssm, s_glu, dgb, B=B, S=S, name="ssm_glu_bwd1")
    dz2 = matmul(ds_glu, W_glu, tb=True, name="ssm_glu_dx")
    dW_glu = matmul(zb, ds_glu, ta=True, out_dtype=BF16, name="ssm_glu_dw")
    dy_ssm = glu_bwd2(y_ssm, dz1, dz2, B=B, S=S, name="ssm_glu_bwd2")
    du, dbre, dbim, dcre, dcim, dlr8, dli8, ddsk = ssm_bwd(u, dy_ssm, bre_blk, bim_blk, cre_blk, cim_blk, lr8, li8,
                                                           d_skip_full, B=B, S=S, name="ssm_scan_bwd")
    dub = du.astype(BF16)
    dh1p = matmul(dub, W_in, tb=True, name="ssm_in_dx")
    dW_in = matmul(h1p, dub, ta=True, out_dtype=BF16, name="ssm_in_dw")
    dx2, dyf0, dsh1s, dsc1s, dnm1, dg2_0 = norm_mod_bwd_gate(_deinterleave(dh1p, B, S), x2, dx3, norm_mix[1], sc1s,
                                                             yf0, mods[0][5], B=B, S=S, name="ssm_norm_bwd")
    dlr = jnp.sum(dlr8, axis=(0, 2)).reshape(G, P)
    dli = jnp.sum(dli8, axis=(0, 2)).reshape(G, P)
    dbbr = _block_diag_in_grad(jnp.sum(dbre, axis=0), G, P, H)
    dbbi = _block_diag_in_grad(jnp.sum(dbim, axis=0), G, P, H)
    dc_re = _block_diag_out_grad(jnp.sum(dcre, axis=0), G, H, P)
    dc_im = _block_diag_out_grad(jnp.sum(dcim, axis=0), G, H, P)
    dd_skip = jnp.sum(ddsk, axis=0).reshape(D)

    dx1, dya, dg1, gf0 = ffn_bwd(dx2, dyf0, 0, ffn0, ya, g1)
    gf0["dg2"] = dg2_0

    do2 = matmul(dya, W_o_attn, tb=True, out_dtype=BF16, name="att_out_dx")
    dW_o_attn = matmul(o2, dya, ta=True, out_dtype=BF16, name="att_out_dw")
    g_rows_cm = jnp.concatenate([dW_o_attn.reshape(N_CHIPS, Dq, D), dW_in.reshape(N_CHIPS, Dq, D),
                                 dW_glu.reshape(N_CHIPS, Dq, D), dW_o_ssm.reshape(N_CHIPS, Dq, D),
                                 gf0["dW_down"].reshape(N_CHIPS, Fq, D), gf1["dW_down"].reshape(N_CHIPS, Fq, D)], axis=1)
    pairs_a = pair_sums([g_rows_cm, gf0["dW_up"], gf1["dW_up"]], [BF16, BF16, BF16], "a")
    dqkv, *gots_a = attn_bwd_pairs(qkv, ltot, do2, pairs_a, B=B, S=S, name="att_bwd")
    dh1a = matmul(dqkv, _from_chip_major(W_qkv, 1), tb=True, name="att_qkv_dx")
    dW_qkv = _chip_major(matmul(h1a, dqkv, ta=True, b_chips=True, out_dtype=BF16, name="att_qkv_dw"), 1)
    grad_x, dsh1, dsc1, dnm0 = norm_mod_bwd(dh1a, x0, dx1, norm_mix[0], sc1, B=B, S=S, name="att_norm_bwd")

    dmod_rows = jnp.concatenate([dsh1, dsc1, dg1, gf0["dsh2"], gf0["dsc2"], gf0["dg2"],
                                 dsh1s, dsc1s, dg1s, gf1["dsh2"], gf1["dsc2"], gf1["dg2"], dsh_f, dsc_f], axis=1)
    dmod_all = all_gather8(dmod_rows, name="gather_dmod").reshape(BG, 14 * D)
    grad_w_mod = jnp.stack([
        matmul(c_act, lax.dynamic_slice(dmod_all, (0, i * 6 * D + chip * mod_w), (BG, mod_w)), ta=True,
               name=f"mod_dw_{i}") for i in range(2)])
    grad_w_fin = matmul(c_act, lax.dynamic_slice(dmod_all, (0, 12 * D + chip * fin_w), (BG, fin_w)), ta=True,
                        name="fin_dw")

    parts = [jnp.concatenate([jnp.sum(dnm0, axis=0), jnp.sum(dnm1, axis=0)]),
             jnp.concatenate([gf0["dnorm_ffn"], gf1["dnorm_ffn"]]),
             jnp.sum(dmod_rows[:, :12 * D], axis=0),
             dlr.reshape(-1), dli.reshape(-1), dbbr.reshape(-1), dbbi.reshape(-1), dc_re.reshape(-1), dc_im.reshape(-1),
             dd_skip, jnp.sum(db_glu, axis=0),
             jnp.sum(dnorm_out, axis=0), jnp.sum(dmod_rows[:, 12 * D:], axis=0),
             gf0["dconv_w"].reshape(-1), gf1["dconv_w"].reshape(-1), gf0["dconv_b"], gf1["dconv_b"]]
    sizes = [int(p.shape[0]) for p in parts]
    flat = jnp.concatenate(parts)
    width = 1024
    quantum = N_CHIPS * 16 * width
    padded = -(-flat.shape[0] // quantum) * quantum
    small_cm = jnp.pad(flat, (0, padded - flat.shape[0])).reshape(N_CHIPS, -1, width)

    pairs_b = pair_sums([dW_qkv, small_cm], [BF16, F32], "b")
    gots_b = scatter_to_chips(pairs_b, name="rs_scatter_to_chips")
    r_qkv, r_small, r_rows, r_up0, r_up1 = join_halves(
        chip_sums(pairs_b, gots_b, "b") + chip_sums(pairs_a, gots_a, "a"), name="rs_join_halves")
    grad_w_qkv = r_qkv[None]
    grad_w_o_attn = r_rows[0 * Dq:1 * Dq][None]
    grad_w_in_ssm = r_rows[1 * Dq:2 * Dq][None]
    grad_w_glu = r_rows[2 * Dq:3 * Dq][None]
    grad_w_o_ssm = r_rows[3 * Dq:4 * Dq][None]
    grad_w_down = r_rows[4 * Dq:].reshape(2, Fq, D)
    grad_w_up = jnp.stack([r_up0, r_up1])
    summed = all_gather8(r_small, name="gather_small_grads")[::2].reshape(-1)
    offs = [0]
    for s_ in sizes:
        offs.append(offs[-1] + s_)
    (s_nm, s_nf, s_bmod, s_lr, s_li, s_bbr, s_bbi, s_cre, s_cim, s_dsk, s_bglu, s_no, s_bfin, s_cw0, s_cw1, s_cb0,
     s_cb1) = [summed[offs[i]:offs[i + 1]] for i in range(len(sizes))]
    _, disc_vjp = jax.vjp(_ssm_discretize, a_re[0], a_im[0], log_dt[0], b_re[0], b_im[0])
    ga_re, ga_im, glog_dt, gb_re, gb_im = disc_vjp((s_lr.reshape(G, P), s_li.reshape(G, P), s_bbr.reshape(G, P, H),
                                                    s_bbi.reshape(G, P, H)))
    grad_norm_mix = s_nm.reshape(2, D)
    grad_norm_ffn = s_nf.reshape(2, D)
    grad_b_mod = s_bmod.reshape(2, 6 * D)
    grad_c_re = s_cre.reshape(1, G, H, P)
    grad_c_im = s_cim.reshape(1, G, H, P)
    grad_d_skip = lax.dynamic_slice(s_dsk, (chip * Dq,), (Dq,)).reshape(1, Dq)
    grad_b_glu = lax.dynamic_slice(s_bglu, (chip * Dq,), (Dq,)).reshape(1, Dq)
    cw_full = jnp.stack([s_cw0.reshape(3, F2), s_cw1.reshape(3, F2)])
    grad_conv_w = lax.dynamic_slice(cw_full, (0, 0, chip * (F2 // N_CHIPS)), (2, 3, F2 // N_CHIPS))
    grad_conv_b = jnp.stack([s_cb0, s_cb1])
    grad_norm_out = s_no
    grad_b_fin = s_bfin

    grads = dict(
        norm_mix=grad_norm_mix, norm_ffn=grad_norm_ffn, w_mod=grad_w_mod, b_mod=grad_b_mod, w_qkv=grad_w_qkv,
        w_o_attn=grad_w_o_attn, w_in_ssm=grad_w_in_ssm, a_re=ga_re[None], a_im=ga_im[None], log_dt=glog_dt[None],
        b_re=gb_re[None], b_im=gb_im[None], c_re=grad_c_re, c_im=grad_c_im, d_skip=grad_d_skip, w_glu=grad_w_glu,
        b_glu=grad_b_glu, w_o_ssm=grad_w_o_ssm, w_up=grad_w_up, conv_w=grad_conv_w, conv_b=grad_conv_b,
        w_down=grad_w_down, norm_out=grad_norm_out, w_fin=grad_w_fin, b_fin=grad_b_fin)
    weights = dict(
        norm_mix=norm_mix, norm_ffn=norm_ffn, w_mod=w_mod, b_mod=b_mod, w_qkv=w_qkv, w_o_attn=w_o_attn,
        w_in_ssm=w_in_ssm, a_re=a_re, a_im=a_im, log_dt=log_dt, b_re=b_re, b_im=b_im, c_re=c_re, c_im=c_im,
        d_skip=d_skip, w_glu=w_glu, b_glu=b_glu, w_o_ssm=w_o_ssm, w_up=w_up, conv_w=conv_w, conv_b=conv_b,
        w_down=w_down, norm_out=norm_out, w_fin=w_fin, b_fin=b_fin)
    m_in = dict(
        norm_mix=m_norm_mix, norm_ffn=m_norm_ffn, w_mod=m_w_mod, b_mod=m_b_mod, w_qkv=m_w_qkv, w_o_attn=m_w_o_attn,
        w_in_ssm=m_w_in_ssm, a_re=m_a_re, a_im=m_a_im, log_dt=m_log_dt, b_re=m_b_re, b_im=m_b_im, c_re=m_c_re,
        c_im=m_c_im, d_skip=m_d_skip, w_glu=m_w_glu, b_glu=m_b_glu, w_o_ssm=m_w_o_ssm, w_up=m_w_up, conv_w=m_conv_w,
        conv_b=m_conv_b, w_down=m_w_down, norm_out=m_norm_out, w_fin=m_w_fin, b_fin=m_b_fin)
    v_in = dict(
        norm_mix=v_norm_mix, norm_ffn=v_norm_ffn, w_mod=v_w_mod, b_mod=v_b_mod, w_qkv=v_w_qkv, w_o_attn=v_w_o_attn,
        w_in_ssm=v_w_in_ssm, a_re=v_a_re, a_im=v_a_im, log_dt=v_log_dt, b_re=v_b_re, b_im=v_b_im, c_re=v_c_re,
        c_im=v_c_im, d_skip=v_d_skip, w_glu=v_w_glu, b_glu=v_b_glu, w_o_ssm=v_w_o_ssm, w_up=v_w_up, conv_w=v_conv_w,
        conv_b=v_conv_b, w_down=v_w_down, norm_out=v_norm_out, w_fin=v_w_fin, b_fin=v_b_fin)
    names = list(weights)
    for n_ in names:
        grads[n_] = grads[n_].reshape(weights[n_].shape)

    big = ("w_mod", "w_qkv", "w_o_attn", "w_in_ssm", "w_glu", "w_o_ssm", "w_up", "w_down", "w_fin")
    delta, new_m, new_v = {}, {}, {}
    for n_ in big:
        shp = weights[n_].shape
        two_d = lambda a: a.reshape(-1, shp[-1])
        d_, m_, v_ = adamw(two_d(weights[n_]), two_d(grads[n_]), two_d(m_in[n_]), two_d(v_in[n_]), name=f"adamw_{n_}")
        delta[n_], new_m[n_], new_v[n_] = d_.reshape(shp), m_.reshape(shp), v_.reshape(shp)
    rest = [n_ for n_ in names if n_ not in big]
    small_out = adamw_many([weights[n_] for n_ in rest], [grads[n_] for n_ in rest], [m_in[n_] for n_ in rest],
                           [v_in[n_] for n_ in rest], name="adamw_small")
    for n_, (d_, m_, v_) in zip(rest, small_out):
        delta[n_], new_m[n_], new_v[n_] = d_, m_, v_

    return (loss, grad_x.reshape(B, S, D), *[grads[n_] for n_ in names], *[delta[n_] for n_ in names],
            *[new_m[n_] for n_ in names], *[new_v[n_] for n_ in names])
```

```python
import math

import jax
import jax.numpy as jnp
from jax import lax
from jax.experimental import pallas as pl
from jax.experimental.pallas import tpu as pltpu

F32 = jnp.float32
BF16 = jnp.bfloat16
MESH = pl.DeviceIdType.MESH

HEAD_DIM = 64
SSM_GROUP = 16
STATE = 64
GROUPS_PER_BLOCK = 8
SEGMENTS = 16
SCAN_UNROLL = 4
EPS = 1e-6
ADAM_LR = 0.001
ADAM_B1 = 0.9
ADAM_B2 = 0.999
ADAM_EPS = 1e-08
ADAM_WD = 0.01
ADAM_STEP = 10
N_CHIPS = 4
N_DEV = 8
V7X_VMEM_LIMIT = 56 * 1024 * 1024
ATT_BLOCK = 128
ATT_HEADS = 8
ATT_HEADS_BWD = 8


def _tile(n, prefs):
    for p in prefs:
        if n % p == 0:
            return p
    return n


def _row_tile(n, cap=1024, mult=16):
    for t in range(min(n, cap) // mult * mult, 0, -mult):
        if n % t == 0:
            return t
    return n


def _params(sem, vmem=V7X_VMEM_LIMIT):
    return pltpu.CompilerParams(dimension_semantics=sem, vmem_limit_bytes=vmem)


def matmul(a, b, *, ta=False, tb=False, bias=None, out_dtype=F32, b_chips=False, out_chips=False, name):
    a_parts = a.shape[0] if a.ndim == 3 else 1
    if a_parts > 1:
        assert not ta
        M, K = a.shape[1], a_parts * a.shape[2]
    elif ta:
        K, M = a.shape
    else:
        M, K = a.shape
    b_parts = b.shape[0] if b_chips else 1
    b_rows, b_cols = (b.shape[1], b_parts * b.shape[2]) if b_chips else b.shape
    if tb:
        N, Kb = b_rows, b_cols
    else:
        Kb, N = b_rows, b_cols
    assert K == Kb, (a.shape, b.shape, ta, tb)
    n_cut = math.gcd(N // (N_CHIPS if out_chips else 1), N // (b_parts if not tb else 1))
    k_cut = math.gcd(K // (b_parts if tb else 1), K // a_parts)
    tm = _tile(M, (1024, 1408, 512, 256, 128))
    tn = _tile(n_cut, (1024, 1408, 768, 512, 256, 128))
    tk = k_cut if k_cut <= 2816 else _tile(k_cut, (1024, 512, 256, 128))
    nk = K // tk
    npc = N // N_CHIPS // tn
    npb = N // b_parts // tn
    kpb = K // b_parts // tk
    kpa = K // a_parts // tk
    dims = (((0,) if ta else (1,), (1,) if tb else (0,)), ((), ()))

    def body(*refs):
        a_ref, b_ref = refs[:2]
        bias_ref = refs[2] if bias is not None else None
        o_ref = refs[-2] if nk > 1 else refs[-1]

        def finish(r):
            if bias_ref is not None:
                r = r + bias_ref[...]
            o_ref[...] = r.astype(o_ref.dtype)

        prod = lax.dot_general(a_ref[...].astype(BF16), b_ref[...].astype(BF16), dims, preferred_element_type=F32)
        if nk == 1:
            finish(prod)
            return
        acc_ref = refs[-1]
        k = pl.program_id(2)

        @pl.when(k == 0)
        def _():
            acc_ref[...] = prod

        @pl.when(k > 0)
        def _():
            acc_ref[...] += prod

        @pl.when(k == nk - 1)
        def _():
            finish(acc_ref[...])

    if a_parts > 1:
        a_spec = pl.BlockSpec((None, tm, tk), lambda i, j, k: (lax.div(k, kpa), i, lax.rem(k, kpa)))
    else:
        a_spec = pl.BlockSpec((tk, tm), lambda i, j, k: (k, i)) if ta else pl.BlockSpec((tm, tk), lambda i, j, k: (i, k))
    if not b_chips:
        b_spec = pl.BlockSpec((tn, tk), lambda i, j, k: (j, k)) if tb else pl.BlockSpec((tk, tn), lambda i, j, k: (k, j))
    elif tb:
        b_spec = pl.BlockSpec((None, tn, tk), lambda i, j, k: (lax.div(k, kpb), j, lax.rem(k, kpb)))
    else:
        b_spec = pl.BlockSpec((None, tk, tn), lambda i, j, k: (lax.div(j, npb), k, lax.rem(j, npb)))
    in_specs = [a_spec, b_spec]
    args = [a, b]
    if bias is not None:
        in_specs.append(pl.BlockSpec((1, tn), lambda i, j, k: (0, j)))
        args.append(bias.reshape(1, N).astype(F32))
    if out_chips:
        out_shape = jax.ShapeDtypeStruct((N_CHIPS, M, N // N_CHIPS), out_dtype)
        out_spec = pl.BlockSpec((None, tm, tn), lambda i, j, k: (lax.div(j, npc), i, lax.rem(j, npc)))
    else:
        out_shape = jax.ShapeDtypeStruct((M, N), out_dtype)
        out_spec = pl.BlockSpec((tm, tn), lambda i, j, k: (i, j))
    return pl.pallas_call(
        body, name=name,
        out_shape=out_shape,
        grid=(M // tm, N // tn, nk),
        in_specs=in_specs,
        out_specs=out_spec,
        scratch_shapes=[pltpu.VMEM((tm, tn), F32)] if nk > 1 else [],
        compiler_params=_params(("parallel", "parallel", "arbitrary")),
    )(*args)


def rowwise(fn, tiled, per_seq, glob, out_tiled, out_seq, *, B, S, name, rows=512):
    tm = _tile(S, (rows, 128, 64, 32, 16, 8))
    nt = S // tm
    n_in = len(tiled) + len(per_seq) + len(glob)
    n_ot = len(out_tiled)

    def body(*refs):
        ins = refs[:n_in]
        outs = refs[n_in:]
        vals = fn(*[r[...] for r in ins])
        if not isinstance(vals, (tuple, list)):
            vals = (vals,)
        assert len(vals) == len(outs), (name, len(vals), len(outs))
        for o_ref, v in zip(outs[:n_ot], vals[:n_ot]):
            o_ref[...] = v.astype(o_ref.dtype)
        t = pl.program_id(1)
        for o_ref, v in zip(outs[n_ot:], vals[n_ot:]):
            def first(o_ref=o_ref, v=v):
                o_ref[...] = v.astype(F32)

            def later(o_ref=o_ref, v=v):
                o_ref[...] += v.astype(F32)

            pl.when(t == 0)(first)
            pl.when(t > 0)(later)

    in_specs = [pl.BlockSpec((tm, a.shape[1]), lambda b, t: (b * nt + t, 0)) for a in tiled]
    in_specs += [pl.BlockSpec((None, 1, a.shape[1]), lambda b, t: (b, 0, 0)) for a in per_seq]
    in_specs += [pl.BlockSpec(a.shape, lambda b, t: (0,) * a.ndim) for a in glob]
    out_shape = [jax.ShapeDtypeStruct((B * S, w), dt) for w, dt in out_tiled]
    out_shape += [jax.ShapeDtypeStruct((B, 1, w), F32) for w in out_seq]
    out_specs = [pl.BlockSpec((tm, w), lambda b, t: (b * nt + t, 0)) for w, _ in out_tiled]
    out_specs += [pl.BlockSpec((None, 1, w), lambda b, t: (b, 0, 0)) for w in out_seq]
    res = pl.pallas_call(
        body, name=name, out_shape=out_shape, grid=(B, nt), in_specs=in_specs, out_specs=out_specs,
        compiler_params=_params(("parallel", "arbitrary")),
    )(*tiled, *[a.reshape(B, 1, a.shape[1]) for a in per_seq], *glob)
    res = list(res)
    for i in range(n_ot, len(res)):
        res[i] = res[i].reshape(B, res[i].shape[-1])
    return res


def _rms(x):
    r = lax.rsqrt(jnp.mean(x * x, axis=-1, keepdims=True) + EPS)
    return x * r, r


def norm_mod_fwd(x, g, sh, sc, *, B, S, name):
    def fn(x, sh, sc, g):
        xn, _ = _rms(x)
        return (xn * g) * (1.0 + sc) + sh

    return rowwise(fn, [x], [sh, sc], [g.reshape(1, -1)], [(x.shape[1], BF16)], [], B=B, S=S, name=name)[0]


def _norm_mod_bwd_math(dh, x, sc, g):
    xn, r = _rms(x)
    y = xn * g
    dy = dh * (1.0 + sc)
    dxn = dy * g
    dx = r * (dxn - xn * jnp.mean(dxn * xn, axis=-1, keepdims=True))
    dsh = jnp.sum(dh, axis=0, keepdims=True)
    dsc = jnp.sum(dh * y, axis=0, keepdims=True)
    dg = jnp.sum(dy * xn, axis=0, keepdims=True)
    return dx, dsh, dsc, dg


def norm_mod_bwd(dh, x, dres, g, sc, *, B, S, name):
    D = x.shape[1]

    def fn(dh, x, dres, sc, g):
        dx, dsh, dsc, dg = _norm_mod_bwd_math(dh.astype(F32), x, sc, g)
        return dres + dx, dsh, dsc, dg

    return rowwise(fn, [dh, x, dres], [sc], [g.reshape(1, -1)], [(D, F32)], [D, D, D], B=B, S=S, name=name)


def norm_mod_bwd_gate(dh, x, dres, g, sc, y_prev, gate_prev, *, B, S, name):
    D = x.shape[1]

    def fn(dh, x, dres, y, sc, gate, g):
        dx, dsh, dsc, dg = _norm_mod_bwd_math(dh.astype(F32), x, sc, g)
        dx = dres + dx
        return dx, gate * dx, dsh, dsc, dg, jnp.sum(dx * y, axis=0, keepdims=True)

    return rowwise(fn, [dh, x, dres, y_prev], [sc, gate_prev], [g.reshape(1, -1)], [(D, F32), (D, BF16)],
                   [D, D, D, D], B=B, S=S, name=name)


def gate_res_fwd(x, y, gate, *, B, S, name):
    return rowwise(lambda x, y, g: x + g * y, [x, y], [gate], [], [(x.shape[1], F32)], [], B=B, S=S, name=name)[0]


def res_norm_mod_fwd(x, y, gate, g, sh, sc, *, B, S, name):
    D = x.shape[1]

    def fn(x, y, gate, sh, sc, g):
        x = x + gate * y
        xn, _ = _rms(x)
        return x, (xn * g) * (1.0 + sc) + sh

    return rowwise(fn, [x, y], [gate, sh, sc], [g.reshape(1, -1)], [(D, F32), (D, BF16)], [], B=B, S=S, name=name)


def final_loss(x, tgt, g, sh, sc, y_prev, gate_prev, *, B, S, name):
    D = x.shape[1]

    def fn(x, tgt, y_prev, sh, sc, gate, g):
        xn, _ = _rms(x)
        y = (xn * g) * (1.0 + sc) + sh
        err = y - tgt
        loss = 0.5 * jnp.sum(err * err, axis=0, keepdims=True) * (1.0 / D)
        dx, dsh, dsc, dg = _norm_mod_bwd_math(err * (1.0 / D), x, sc, g)
        return dx, gate * dx, loss, dsh, dsc, dg, jnp.sum(dx * y_prev, axis=0, keepdims=True)

    return rowwise(fn, [x, tgt, y_prev], [sh, sc, gate_prev], [g.reshape(1, -1)], [(D, F32), (D, BF16)],
                   [D, D, D, D, D], B=B, S=S, name=name)


def _gelu(y):
    c0 = math.sqrt(2.0 / math.pi)
    t = jnp.tanh(c0 * (y + 0.044715 * (y * y * y)))
    return 0.5 * y * (1.0 + t), t


def _sigmoid(s):
    return 0.5 * jnp.tanh(0.5 * s) + 0.5


def gelu_fwd(y, *, B, S, name):
    return rowwise(lambda y: _gelu(y)[0], [y], [], [], [(y.shape[1], BF16)], [], B=B, S=S, name=name)[0]


def glu_fwd(y, s, *, B, S, name):
    return rowwise(lambda y, s: _gelu(y)[0] * _sigmoid(s), [y, s], [], [], [(y.shape[1], BF16)], [], B=B, S=S,
                   name=name)[0]


def glu_bwd1(y, s, dg, *, B, S, name):
    D = y.shape[1]

    def fn(y, s, dg):
        z = _gelu(y)[0]
        sig = _sigmoid(s)
        ds = dg * z * sig * (1.0 - sig)
        return ds, dg * sig, jnp.sum(ds, axis=0, keepdims=True)

    return rowwise(fn, [y, s, dg], [], [], [(D, BF16), (D, F32)], [D], B=B, S=S, name=name)


def glu_bwd2(y, dz1, dz2, *, B, S, name):
    D = y.shape[1]
    c0 = math.sqrt(2.0 / math.pi)

    def fn(y, dz1, dz2):
        _, t = _gelu(y)
        dgelu = 0.5 * (1.0 + t) + 0.5 * y * (1.0 - t * t) * c0 * (1.0 + 3.0 * 0.044715 * y * y)
        return (dz1 + dz2) * dgelu

    return rowwise(fn, [y, dz1, dz2], [], [], [(D, F32)], [], B=B, S=S, name=name)[0]


def silu_rows(c, *, name):
    R, W = c.shape
    return rowwise(lambda c: c * _sigmoid(c), [c], [], [], [(W, F32)], [], B=1, S=R, name=name)[0]


def conv_gate_fwd(up, cw, cb, *, B, S, name):
    F = up.shape[1] // 2
    tn = _tile(F, (256, 128))
    nF = F // tn

    def body(g_ref, v_ref, wg_ref, wv_ref, bg_ref, bv_ref, o_ref):
        rows = lax.broadcasted_iota(jnp.int32, (S, tn), 0)

        def conv(x, w_ref):
            x1 = jnp.where(rows >= 1, pltpu.roll(x, 1, 0), 0.0)
            x2 = jnp.where(rows >= 2, pltpu.roll(x, 2, 0), 0.0)
            return w_ref[2:3, :] * x + w_ref[1:2, :] * x1 + w_ref[0:1, :] * x2

        gc = conv(g_ref[...].astype(F32), wg_ref) + bg_ref[...]
        vc = conv(v_ref[...].astype(F32), wv_ref) + bv_ref[...]
        o_ref[...] = (gc * _sigmoid(gc) * vc).astype(o_ref.dtype)

    def cols(off):
        return pl.BlockSpec((S, tn), lambda b, j: (b, j + off))

    def vec(rows, off):
        return pl.BlockSpec((rows, tn), lambda b, j: (0, j + off))

    return pl.pallas_call(
        body, name=name, out_shape=jax.ShapeDtypeStruct((B * S, F), BF16), grid=(B, nF),
        in_specs=[cols(0), cols(nF), vec(3, 0), vec(3, nF), vec(1, 0), vec(1, nF)],
        out_specs=pl.BlockSpec((S, tn), lambda b, j: (b, j)),
        compiler_params=_params(("parallel", "parallel")),
    )(up, up, cw, cw, cb, cb)


def conv_gate_bwd(up, dact, cw, cb, *, B, S, name):
    F = up.shape[1] // 2
    tn = _tile(F, (256, 128))
    nF = F // tn

    def body(g_ref, v_ref, da_ref, wg_ref, wv_ref, bg_ref, bv_ref, o_ref, dw_ref, db_ref):
        rows = lax.broadcasted_iota(jnp.int32, (S, tn), 0)

        def earlier(x, k):
            return jnp.where(rows >= k, pltpu.roll(x, k, 0), 0.0)

        def later(x, k):
            return jnp.where(rows < S - k, pltpu.roll(x, S - k, 0), 0.0)

        def conv(x, w_ref):
            x1, x2 = earlier(x, 1), earlier(x, 2)
            return w_ref[2:3, :] * x + w_ref[1:2, :] * x1 + w_ref[0:1, :] * x2, x1, x2

        def back(d, x, x1, x2, w_ref, half):
            o_ref[half] = (w_ref[2:3, :] * d + w_ref[1:2, :] * later(d, 1) + w_ref[0:1, :] * later(d, 2)
                           ).astype(o_ref.dtype)
            dw_ref[half] = jnp.concatenate([jnp.sum(d * x2, axis=0, keepdims=True),
                                            jnp.sum(d * x1, axis=0, keepdims=True),
                                            jnp.sum(d * x, axis=0, keepdims=True)], axis=0)
            return jnp.sum(d, axis=0, keepdims=True)

        g, v, da = g_ref[...].astype(F32), v_ref[...].astype(F32), da_ref[...].astype(F32)
        gc, g1, g2 = conv(g, wg_ref)
        vc, v1, v2 = conv(v, wv_ref)
        gc = gc + bg_ref[...]
        vc = vc + bv_ref[...]
        sig = _sigmoid(gc)
        dg = da * vc * (sig * (1.0 + gc * (1.0 - sig)))
        dv = da * (gc * sig)
        db_ref[...] = jnp.concatenate([back(dg, g, g1, g2, wg_ref, 0), back(dv, v, v1, v2, wv_ref, 1)], axis=0)

    def cols(off):
        return pl.BlockSpec((S, tn), lambda b, j: (b, j + off))

    def vec(rows, off):
        return pl.BlockSpec((rows, tn), lambda b, j: (0, j + off))

    return pl.pallas_call(
        body, name=name,
        out_shape=[jax.ShapeDtypeStruct((2, B * S, F), BF16), jax.ShapeDtypeStruct((B, 2, 3, F), F32),
                   jax.ShapeDtypeStruct((B, 2, F), F32)],
        grid=(B, nF),
        in_specs=[cols(0), cols(nF), cols(0), vec(3, 0), vec(3, nF), vec(1, 0), vec(1, nF)],
        out_specs=[pl.BlockSpec((2, S, tn), lambda b, j: (0, b, j)),
                   pl.BlockSpec((None, 2, 3, tn), lambda b, j: (b, 0, 0, j)),
                   pl.BlockSpec((None, 2, tn), lambda b, j: (b, 0, j))],
        compiler_params=_params(("parallel", "parallel")),
    )(up, up, dact, cw, cw, cb, cb)


MASKED_LOG = -1e30


def _split2(x):
    bits = lax.bitcast_convert_type(x, jnp.uint32) & jnp.uint32(0xFFFF0000)
    hi = lax.bitcast_convert_type(bits, F32)
    return hi.astype(BF16), (x - hi).astype(BF16)


def _nt(a, b):
    return lax.dot_general(a, b, (((1,), (1,)), ((), ())), preferred_element_type=F32)


def _tn(a, b):
    return lax.dot_general(a, b, (((0,), (0,)), ((), ())), preferred_element_type=F32)


def _att_scores(q, k, mask, prescaled=False):
    z = _nt(q, k)
    if not prescaled:
        z = z * (HEAD_DIM ** -0.5)
    e = jnp.exp(-jnp.abs(z))
    sp = jnp.log(1.0 + e)
    lb = jnp.minimum(z, 0.0) - sp
    l1 = lb - z
    if mask is not None:
        lb = jnp.where(mask, lb, MASKED_LOG)
        l1 = jnp.where(mask, l1, 0.0)
    return z, lb, l1, e


def _col_to_row(col, eye):
    return jnp.sum(jnp.where(eye, col, 0.0), axis=0, keepdims=True)


def _row_to_col(row, eye):
    return jnp.sum(jnp.where(eye, row, 0.0), axis=1, keepdims=True)


def _wide_consts(T, W):
    r = lax.broadcasted_iota(jnp.int32, (W, W), 0)
    c = lax.broadcasted_iota(jnp.int32, (W, W), 1)
    two = lambda m: jnp.concatenate([m.astype(BF16)] * 2, axis=0)
    qrow = lax.broadcasted_iota(jnp.int32, (T, W), 0)
    kcol = lax.broadcasted_iota(jnp.int32, (T, W), 1)
    er = lax.broadcasted_iota(jnp.int32, (T, T), 0)
    ec = lax.broadcasted_iota(jnp.int32, (T, T), 1)
    return two(r > c), two(r <= c), two(r < c), qrow, kcol, er == ec


def _pair_masks(x, first):
    zero = jnp.zeros_like(x)
    return jnp.where(first, x, zero), jnp.where(first, zero, x)


def attn_fwd_pairs(qkv, shards=(), slots=(), *, B, S, name):
    D = qkv.shape[1] // 3
    H = D // HEAD_DIM
    T = ATT_BLOCK
    W = 2 * T
    nq = S // T
    G = _tile(H, (ATT_HEADS, 2))
    P = G // 2
    LW = 2 * HEAD_DIM * P
    nsec = D // LW
    n = len(shards)
    n_steps = B * nsec

    def body(*refs):
        q_ref, k_ref, v_ref = refs[:3]
        o_ref, l_ref = refs[3 + 2 * n:5 + 2 * n]
        step_id = pl.program_id(0) * nsec + pl.program_id(1)
        if n:
            gather = _ShardGather(refs[3:3 + n], refs[5 + 2 * n:5 + 3 * n], *refs[5 + 3 * n:])
            pl.when(step_id == 0)(gather.send)
            pl.when(step_id == n_steps - 1)(gather.forward)
        later2, _, _, qrow, kcol, eye = _wide_consts(T, W)
        blk = lax.broadcasted_iota(jnp.int32, (nq, T), 0)
        first_q = lax.broadcasted_iota(jnp.int32, (T, 2 * HEAD_DIM), 1) < HEAD_DIM
        first_k = lax.broadcasted_iota(jnp.int32, (W, 2 * HEAD_DIM), 1) < HEAD_DIM

        def lanes(j):
            return slice(j * 2 * HEAD_DIM, (j + 1) * 2 * HEAD_DIM)

        def step(qms, k0, st, mask):
            accs, runs = st
            parts, lbs, sums = [], [], []
            for g in range(G):
                _, lb, l1, _ = _att_scores(qms[g], k_ref[pl.ds(k0, W), lanes(g // 2)], mask, prescaled=True)
                parts.append(jnp.concatenate(_split2(l1), axis=1))
                lbs.append(lb)
                sums.append(jnp.sum(l1, axis=1, keepdims=True))
            suf = jnp.dot(jnp.concatenate(parts, axis=0), later2, preferred_element_type=F32)
            new_accs, new_runs = [], []
            for j in range(P):
                vms = _pair_masks(v_ref[pl.ds(k0, W), lanes(j)], first_k)
                acc = accs[j]
                for h in range(2):
                    g = 2 * j + h
                    w = jnp.exp(lbs[g] + suf[g * T:(g + 1) * T] + runs[g])
                    acc = acc + jnp.dot(w.astype(BF16), vms[h], preferred_element_type=F32)
                    new_runs.append(runs[g] + sums[g])
                new_accs.append(acc)
            return tuple(new_accs), tuple(new_runs)

        def qblock(i, totals):
            q0 = pl.multiple_of(i * T, T)
            qms = []
            for j in range(P):
                qms.extend(_pair_masks(q_ref[pl.ds(q0, T), lanes(j)] * (HEAD_DIM ** -0.5), first_q))
            half = jnp.right_shift(i, 1)
            last = half * W
            k_last = pl.multiple_of(last, W)
            mask = (k_last + kcol) < (q0 + qrow)
            st = (tuple(jnp.zeros((T, 2 * HEAD_DIM), F32) for _ in range(P)),
                  tuple(jnp.zeros((T, 1), F32) for _ in range(G)))
            st = step(qms, k_last, st, mask)

            def kblock(jj, st):
                return step(qms, pl.multiple_of(last - jj * W, W), st, None)

            accs, runs = lax.fori_loop(1, half + 1, kblock, st)
            for j in range(P):
                o_ref[pl.ds(q0, T), lanes(j)] = accs[j].astype(o_ref.dtype)
            return tuple(jnp.where(blk == i, _col_to_row(runs[g], eye), totals[g]) for g in range(G))

        totals = lax.fori_loop(0, nq, qblock, tuple(jnp.zeros((nq, T), F32) for _ in range(G)))
        for g in range(G):
            l_ref[g] = totals[g]
        if n:
            pl.when(step_id == n_steps - 1)(gather.finish)

    def cols(section):
        return pl.BlockSpec((S, LW), lambda b, h: (b, section * nsec + h))

    lspec = pl.BlockSpec((None, G, nq, T), lambda b, h: (b, h, 0, 0))
    dma = pltpu.SemaphoreType.DMA
    return pl.pallas_call(
        body, name=name,
        out_shape=[jax.ShapeDtypeStruct((B * S, D), BF16), jax.ShapeDtypeStruct((B, H, nq, T), F32)]
        + [jax.ShapeDtypeStruct(s.shape, s.dtype) for s in slots],
        grid=(B, nsec), in_specs=[cols(0), cols(1), cols(2)] + _any_specs(2 * n),
        out_specs=[cols(0), lspec] + _any_specs(n),
        input_output_aliases={3 + n + i: 2 + i for i in range(n)},
        scratch_shapes=[dma((3 * n,))] * 4 if n else [],
        compiler_params=_params(("arbitrary", "arbitrary")),
    )(qkv, qkv, qkv, *shards, *slots)


def attn_bwd_pairs(qkv, ltot, do, partials=(), *, B, S, name):
    D = qkv.shape[1] // 3
    H = D // HEAD_DIM
    T = ATT_BLOCK
    W = 2 * T
    nq = S // T
    scale = HEAD_DIM ** -0.5
    G = _tile(H, (ATT_HEADS_BWD, 2))
    P = G // 2
    LW = 2 * HEAD_DIM * P
    nsec = D // LW
    n = len(partials)
    n_steps = B * nsec

    def body(*refs):
        q_ref, k_ref, v_ref, l_ref, do_ref = refs[:5]
        d_ref = refs[5 + n]
        dk_acc, dv_acc = refs[6 + 2 * n:8 + 2 * n]
        step_id = pl.program_id(0) * nsec + pl.program_id(1)
        if n:
            scatter = _ChipScatter(refs[5:5 + n], refs[6 + n:6 + 2 * n], *refs[8 + 2 * n:])
            pl.when(step_id == 0)(scatter.send)
        _, upto2, before2, qrow, kcol, eye = _wide_consts(T, W)
        blk = lax.broadcasted_iota(jnp.int32, (nq, T), 0)
        first_q = lax.broadcasted_iota(jnp.int32, (T, 2 * HEAD_DIM), 1) < HEAD_DIM
        first_k = lax.broadcasted_iota(jnp.int32, (W, 2 * HEAD_DIM), 1) < HEAD_DIM
        dk_acc[...] = jnp.zeros_like(dk_acc)
        dv_acc[...] = jnp.zeros_like(dv_acc)

        def lanes(j):
            return slice(j * 2 * HEAD_DIM, (j + 1) * 2 * HEAD_DIM)

        def step(qms, doms, tots, k0, st, mask):
            dqs, runs_l, runs_d = st
            sc = []
            for g in range(G):
                z, lb, l1, _ = _att_scores(qms[g], k_ref[pl.ds(k0, W), lanes(g // 2)], mask, prescaled=True)
                beta = 0.5 * jnp.tanh(0.5 * z) + 0.5
                omb = 1.0 - beta
                if mask is not None:
                    beta = jnp.where(mask, beta, 0.0)
                dw = _nt(doms[g], v_ref[pl.ds(k0, W), lanes(g // 2)])
                sc.append((lb, jnp.concatenate(_split2(l1), axis=1), jnp.sum(l1, axis=1, keepdims=True), dw, beta, omb))
            pre = jnp.dot(jnp.concatenate([s[1] for s in sc], axis=0), upto2, preferred_element_type=F32)
            dlws = []
            for j in range(P):
                dv = None
                for h in range(2):
                    g = 2 * j + h
                    w = jnp.exp(sc[g][0] + (tots[g] - (pre[g * T:(g + 1) * T] + runs_l[g])))
                    t = _tn(w.astype(BF16), doms[g])
                    dv = t if dv is None else dv + t
                    dlws.append(sc[g][3] * w)
                dv_acc[j, pl.ds(k0, W), :] += dv
            pre_d = jnp.dot(jnp.concatenate([jnp.concatenate(_split2(d), axis=1) for d in dlws], axis=0), before2,
                            preferred_element_type=F32)
            new_dqs, new_l, new_d = [], [], []
            for j in range(P):
                kms = _pair_masks(k_ref[pl.ds(k0, W), lanes(j)], first_k)
                dq, dk = dqs[j], None
                for h in range(2):
                    g = 2 * j + h
                    _, _, rowsum, _, beta, omb = sc[g]
                    dl1 = pre_d[g * T:(g + 1) * T] + runs_d[g]
                    dz = (dlws[g] * omb - dl1 * beta).astype(BF16)
                    dq = dq + jnp.dot(dz, kms[h], preferred_element_type=F32)
                    t = _tn(dz, qms[g])
                    dk = t if dk is None else dk + t
                    new_l.append(runs_l[g] + rowsum)
                    new_d.append(runs_d[g] + jnp.sum(dlws[g], axis=1, keepdims=True))
                dk_acc[j, pl.ds(k0, W), :] += dk
                new_dqs.append(dq)
            return tuple(new_dqs), tuple(new_l), tuple(new_d)

        def qblock(i, carry0):
            q0 = pl.multiple_of(i * T, T)
            qms, doms = [], []
            for j in range(P):
                qms.extend(_pair_masks(q_ref[pl.ds(q0, T), lanes(j)] * scale, first_q))
                doms.extend(_pair_masks(do_ref[pl.ds(q0, T), lanes(j)], first_q))
            tots = [_row_to_col(jnp.sum(jnp.where(blk == i, l_ref[g], 0.0), axis=0, keepdims=True), eye)
                    for g in range(G)]
            z1 = tuple(jnp.zeros((T, 1), F32) for _ in range(G))
            st = (tuple(jnp.zeros((T, 2 * HEAD_DIM), F32) for _ in range(P)), z1, z1)

            def kblock(j, st):
                return step(qms, doms, tots, pl.multiple_of(j * W, W), st, None)

            half = jnp.right_shift(i, 1)
            st = lax.fori_loop(0, half, kblock, st)
            k_last = pl.multiple_of(half * W, W)
            dqs, _, _ = step(qms, doms, tots, k_last, st, (k_last + kcol) < (q0 + qrow))
            for j in range(P):
                d_ref[0, pl.ds(q0, T), lanes(j)] = (dqs[j] * scale).astype(d_ref.dtype)
            return carry0

        lax.fori_loop(0, nq, qblock, 0)
        for j in range(P):
            d_ref[1, :, lanes(j)] = dk_acc[j].astype(d_ref.dtype)
            d_ref[2, :, lanes(j)] = dv_acc[j].astype(d_ref.dtype)
        if n:
            pl.when(step_id == n_steps - 1)(scatter.finish)

    def cols(section):
        return pl.BlockSpec((S, LW), lambda b, h: (b, section * nsec + h))

    lspec = pl.BlockSpec((None, G, nq, T), lambda b, h: (b, h, 0, 0))
    dma = pltpu.SemaphoreType.DMA
    return pl.pallas_call(
        body, name=name,
        out_shape=[jax.ShapeDtypeStruct((3, B * S, D), BF16)]
        + [jax.ShapeDtypeStruct((N_CHIPS - 1,) + a.shape[1:], a.dtype) for a in partials],
        grid=(B, nsec),
        in_specs=[cols(0), cols(1), cols(2), lspec, cols(0)] + _any_specs(n),
        out_specs=[pl.BlockSpec((3, S, LW), lambda b, h: (0, b, h))] + _any_specs(n),
        scratch_shapes=[pltpu.VMEM((P, S, 2 * HEAD_DIM), F32), pltpu.VMEM((P, S, 2 * HEAD_DIM), F32)]
        + ([dma((3 * n,)), dma((3 * n,))] if n else []),
        compiler_params=_params(("arbitrary", "arbitrary")),
    )(qkv, qkv, qkv, ltot, do, *partials)


def _cmul(ar, ai, br, bi):
    return ar * br - ai * bi, ar * bi + ai * br


def _cpow(lr, li, n):
    rr, ri = None, None
    br, bi = lr, li
    while n:
        if n & 1:
            rr, ri = (br, bi) if rr is None else _cmul(rr, ri, br, bi)
        n >>= 1
        if n:
            br, bi = _cmul(br, bi, br, bi)
    return rr, ri


def _ssm_scan(sr, si, lr, li, n_steps, reverse):
    W = sr.shape[1]
    R = SEGMENTS
    lim = -li if reverse else li
    zero = jnp.zeros((R, W), F32)

    def row(k):
        i = (n_steps - 1 - k) if reverse else k
        return pl.multiple_of(i * R, R)

    def local(k, st):
        cr, ci = st
        r0 = row(k)
        pr, pi = _cmul(lr, lim, cr, ci)
        nr = pr + sr[pl.ds(r0, R), :]
        ni = pi + si[pl.ds(r0, R), :]
        sr[pl.ds(r0, R), :] = nr
        si[pl.ds(r0, R), :] = ni
        return nr, ni

    er, ei = lax.fori_loop(0, n_steps, local, (zero, zero), unroll=SCAN_UNROLL)
    lnr, lni = _cpow(lr, lim, n_steps)
    rows = lax.broadcasted_iota(jnp.int32, (R, W), 0)
    cr, ci = zero, zero
    for step in range(1, R):
        tr, ti = _cmul(lnr, lni, cr, ci)
        tr, ti = tr + er, ti + ei
        if reverse:
            seg = R - 1 - step
            tr, ti = pltpu.roll(tr, R - 1, 0), pltpu.roll(ti, R - 1, 0)
        else:
            seg = step
            tr, ti = pltpu.roll(tr, 1, 0), pltpu.roll(ti, 1, 0)
        cr = jnp.where(rows == seg, tr, cr)
        ci = jnp.where(rows == seg, ti, ci)

    def fix(k, st):
        pr, pi = st
        r0 = row(k)
        ar, ai = _cmul(pr, pi, cr, ci)
        sr[pl.ds(r0, R), :] += ar
        si[pl.ds(r0, R), :] += ai
        return _cmul(lr, lim, pr, pi)

    lax.fori_loop(0, n_steps, fix, (lr, lim), unroll=SCAN_UNROLL)
    return cr, ci


def _ssm_specs(S, W):
    CH = GROUPS_PER_BLOCK * SSM_GROUP
    return dict(
        rows=pl.BlockSpec((S, CH), lambda b, j: (b, j)),
        b=pl.BlockSpec((None, CH, W), lambda b, j: (j, 0, 0)),
        c=pl.BlockSpec((None, W, CH), lambda b, j: (j, 0, 0)),
        lam=pl.BlockSpec((None, SEGMENTS, W), lambda b, j: (j, 0, 0)),
        vec=pl.BlockSpec((1, CH), lambda b, j: (0, j)),
    )


def ssm_fwd(u, bre, bim, cre, cim, lr8, li8, dsk, *, B, S, name):
    D = u.shape[1]
    J, CH, W = bre.shape
    n_steps = S // SEGMENTS
    sp = _ssm_specs(S, W)

    def body(u_ref, bre_ref, bim_ref, cre_ref, cim_ref, lr_ref, li_ref, dsk_ref, y_ref, sr, si):
        u = u_ref[...]
        ub = u.astype(BF16)
        sr[...] = jnp.dot(ub, bre_ref[...], preferred_element_type=F32)
        si[...] = jnp.dot(ub, bim_ref[...], preferred_element_type=F32)
        _ssm_scan(sr, si, lr_ref[...], li_ref[...], n_steps, False)
        y = jnp.dot(sr[...].astype(BF16), cre_ref[...], preferred_element_type=F32)
        y = y - jnp.dot(si[...].astype(BF16), cim_ref[...], preferred_element_type=F32)
        y_ref[...] = y + dsk_ref[...] * u

    return pl.pallas_call(
        body, name=name, out_shape=jax.ShapeDtypeStruct((B * S, D), F32), grid=(B, J),
        in_specs=[sp["rows"], sp["b"], sp["b"], sp["c"], sp["c"], sp["lam"], sp["lam"], sp["vec"]],
        out_specs=sp["rows"],
        scratch_shapes=[pltpu.VMEM((S, W), F32), pltpu.VMEM((S, W), F32)],
        compiler_params=_params(("parallel", "parallel")),
    )(u, bre, bim, cre, cim, lr8, li8, dsk)


def ssm_bwd(u, dy, bre, bim, cre, cim, lr8, li8, dsk, *, B, S, name):
    D = u.shape[1]
    J, CH, W = bre.shape
    n_steps = S // SEGMENTS
    sp = _ssm_specs(S, W)

    def body(u_ref, dy_ref, bre_ref, bim_ref, cre_ref, cim_ref, lr_ref, li_ref, dsk_ref,
             du_ref, dbre_ref, dbim_ref, dcre_ref, dcim_ref, dlr_ref, dli_ref, ddsk_ref, sr, si, ar, ai):
        u = u_ref[...]
        dy = dy_ref[...]
        ub = u.astype(BF16)
        dyb = dy.astype(BF16)
        lr, li = lr_ref[...], li_ref[...]
        sr[...] = jnp.dot(ub, bre_ref[...], preferred_element_type=F32)
        si[...] = jnp.dot(ub, bim_ref[...], preferred_element_type=F32)
        cr, ci = _ssm_scan(sr, si, lr, li, n_steps, False)
        ar[...] = _nt(dyb, cre_ref[...])
        ai[...] = -_nt(dyb, cim_ref[...])
        _ssm_scan(ar, ai, lr, li, n_steps, True)

        def dlam(k, st):
            dr, di = st
            r0 = pl.multiple_of((k + 1) * SEGMENTS, SEGMENTS)
            p0 = pl.multiple_of(k * SEGMENTS, SEGMENTS)
            pr, pi = sr[pl.ds(p0, SEGMENTS), :], si[pl.ds(p0, SEGMENTS), :]
            xr, xi = ar[pl.ds(r0, SEGMENTS), :], ai[pl.ds(r0, SEGMENTS), :]
            return dr + pr * xr + pi * xi, di + pr * xi - pi * xr

        xr, xi = ar[0:SEGMENTS, :], ai[0:SEGMENTS, :]
        dr, di = lax.fori_loop(0, n_steps - 1, dlam, (cr * xr + ci * xi, cr * xi - ci * xr), unroll=SCAN_UNROLL)
        dlr_ref[...] = dr
        dli_ref[...] = di
        arb = ar[...].astype(BF16)
        aib = ai[...].astype(BF16)
        du_ref[...] = _nt(arb, bre_ref[...]) + _nt(aib, bim_ref[...]) + dsk_ref[...] * dy
        dbre_ref[...] = _tn(ub, arb)
        dbim_ref[...] = _tn(ub, aib)
        dcre_ref[...] = _tn(sr[...].astype(BF16), dyb)
        dcim_ref[...] = -_tn(si[...].astype(BF16), dyb)
        ddsk_ref[...] = jnp.sum(dy * u, axis=0, keepdims=True)

    def per(shape):
        return pl.BlockSpec((None, None) + shape, lambda b, j: (b, j, 0, 0))

    return pl.pallas_call(
        body, name=name,
        out_shape=[jax.ShapeDtypeStruct((B * S, D), F32),
                   jax.ShapeDtypeStruct((B, J, CH, W), F32), jax.ShapeDtypeStruct((B, J, CH, W), F32),
                   jax.ShapeDtypeStruct((B, J, W, CH), F32), jax.ShapeDtypeStruct((B, J, W, CH), F32),
                   jax.ShapeDtypeStruct((B, J, SEGMENTS, W), F32), jax.ShapeDtypeStruct((B, J, SEGMENTS, W), F32),
                   jax.ShapeDtypeStruct((B, J, 1, CH), F32)],
        grid=(B, J),
        in_specs=[sp["rows"], sp["rows"], sp["b"], sp["b"], sp["c"], sp["c"], sp["lam"], sp["lam"], sp["vec"]],
        out_specs=[sp["rows"], per((CH, W)), per((CH, W)), per((W, CH)), per((W, CH)), per((SEGMENTS, W)),
                   per((SEGMENTS, W)),
                   per((1, CH))],
        scratch_shapes=[pltpu.VMEM((S, W), F32)] * 4,
        compiler_params=_params(("parallel", "parallel")),
    )(u, dy, bre, bim, cre, cim, lr8, li8, dsk)


def _ssm_discretize(a_re, a_im, log_dt, b_re, b_im):
    dt = jnp.exp(log_dt)[:, None]
    er = jnp.exp(a_re * dt)
    lr = er * jnp.cos(a_im * dt)
    li = er * jnp.sin(a_im * dt)
    den = a_re * a_re + a_im * a_im
    fr = ((lr - 1.0) * a_re + li * a_im) / den
    fi = (li * a_re - (lr - 1.0) * a_im) / den
    bbr = fr[..., None] * b_re - fi[..., None] * b_im
    bbi = fr[..., None] * b_im + fi[..., None] * b_re
    return lr, li, bbr, bbi


def _block_diag_in(m):
    G, P, H = m.shape
    J = G // GROUPS_PER_BLOCK
    m = m.reshape(J, GROUPS_PER_BLOCK, P, H).transpose(0, 1, 3, 2)
    eye = jnp.eye(GROUPS_PER_BLOCK, dtype=m.dtype)
    out = m[:, :, :, None, :] * eye[None, :, None, :, None]
    return out.reshape(J, GROUPS_PER_BLOCK * H, GROUPS_PER_BLOCK * P)


def _block_diag_in_grad(d, G, P, H):
    J = G // GROUPS_PER_BLOCK
    d = d.reshape(J, GROUPS_PER_BLOCK, H, GROUPS_PER_BLOCK, P)
    idx = jnp.arange(GROUPS_PER_BLOCK)
    d = d[:, idx, :, idx, :]
    return d.transpose(1, 0, 3, 2).reshape(G, P, H)


def _block_diag_out(m):
    G, H, P = m.shape
    J = G // GROUPS_PER_BLOCK
    m = m.reshape(J, GROUPS_PER_BLOCK, H, P).transpose(0, 1, 3, 2)
    eye = jnp.eye(GROUPS_PER_BLOCK, dtype=m.dtype)
    out = m[:, :, :, None, :] * eye[None, :, None, :, None]
    return out.reshape(J, GROUPS_PER_BLOCK * P, GROUPS_PER_BLOCK * H)


def _block_diag_out_grad(d, G, H, P):
    J = G // GROUPS_PER_BLOCK
    d = d.reshape(J, GROUPS_PER_BLOCK, P, GROUPS_PER_BLOCK, H)
    idx = jnp.arange(GROUPS_PER_BLOCK)
    d = d[:, idx, :, idx, :]
    return d.transpose(1, 0, 3, 2).reshape(G, H, P)


def _interleave(a, B, S):
    L = S // SEGMENTS
    return a.reshape(B, SEGMENTS, L, a.shape[-1]).transpose(0, 2, 1, 3).reshape(B * S, a.shape[-1])


def _deinterleave(a, B, S):
    L = S // SEGMENTS
    return a.reshape(B, L, SEGMENTS, a.shape[-1]).transpose(0, 2, 1, 3).reshape(B * S, a.shape[-1])


def _adamw_math(w, g, m, v):
    m = ADAM_B1 * m + (1.0 - ADAM_B1) * g
    v = ADAM_B2 * v + (1.0 - ADAM_B2) * (g * g)
    m_hat = m / (1.0 - ADAM_B1 ** ADAM_STEP)
    v_hat = v / (1.0 - ADAM_B2 ** ADAM_STEP)
    delta = -ADAM_LR * (m_hat / (jnp.sqrt(v_hat) + ADAM_EPS) + ADAM_WD * w)
    return delta, m, v


def adamw(w, g, m, v, *, name):
    R, C = w.shape
    tr = _tile(R, (max(8, (1 << 18) // C // 8 * 8), 256, 128, 64, 32, 16, 8))

    def body(w_ref, g_ref, m_ref, v_ref, d_ref, nm_ref, nv_ref):
        d, nm, nv = _adamw_math(w_ref[...], g_ref[...], m_ref[...], v_ref[...])
        d_ref[...] = d
        nm_ref[...] = nm
        nv_ref[...] = nv

    spec = pl.BlockSpec((tr, C), lambda i: (i, 0))
    shp = jax.ShapeDtypeStruct((R, C), F32)
    return pl.pallas_call(
        body, name=name, out_shape=[shp, shp, shp], grid=(R // tr,), in_specs=[spec] * 4, out_specs=[spec] * 3,
        compiler_params=_params(("parallel",)),
    )(w, g, m, v)


def adamw_many(ws, gs, ms, vs, *, name):
    n = len(ws)
    at_least_2d = lambda a: a.reshape(1, -1) if a.ndim == 1 else a
    args = [at_least_2d(a) for group in (ws, gs, ms, vs) for a in group]

    def body(*refs):
        for i in range(n):
            d, nm, nv = _adamw_math(refs[i][...], refs[n + i][...], refs[2 * n + i][...], refs[3 * n + i][...])
            refs[4 * n + 3 * i][...] = d
            refs[4 * n + 3 * i + 1][...] = nm
            refs[4 * n + 3 * i + 2][...] = nv

    out = pl.pallas_call(
        body, name=name, out_shape=[jax.ShapeDtypeStruct(a.shape, F32) for a in args[:n] for _ in range(3)],
        in_specs=[pl.BlockSpec(memory_space=pltpu.VMEM) for _ in args],
        out_specs=[pl.BlockSpec(memory_space=pltpu.VMEM) for _ in range(3 * n)],
        compiler_params=pltpu.CompilerParams(vmem_limit_bytes=V7X_VMEM_LIMIT),
    )(*args)
    return [tuple(o.reshape(w.shape) for o in out[3 * i:3 * i + 3]) for i, w in enumerate(ws)]


def _any_specs(n):
    return [pl.BlockSpec(memory_space=pl.ANY) for _ in range(n)]


def _coords():
    return lax.axis_index("x"), lax.axis_index("y"), lax.axis_index("c")


def _flip(v, bit):
    return (v + bit) % 2


def all_gather8(a, *, name):
    shape = a.shape

    def body(a_ref, o_ref, send_sems, recv_sems, local_sem, buf):
        x, y, c = _coords()
        me = 4 * x + 2 * y + c
        load = pltpu.make_async_copy(a_ref, buf, local_sem)
        load.start()
        sends = []
        for k in range(1, N_DEV):
            peer = (_flip(x, (k >> 2) & 1), _flip(y, (k >> 1) & 1), _flip(c, k & 1))
            cp = pltpu.make_async_remote_copy(a_ref, o_ref.at[me], send_sems.at[k - 1], recv_sems.at[k - 1],
                                              device_id=peer, device_id_type=MESH)
            cp.start()
            sends.append(cp)
        load.wait()
        mine = pltpu.make_async_copy(buf, o_ref.at[me], local_sem)
        mine.start()
        for k in range(1, N_DEV):
            px, py, pc = _flip(x, (k >> 2) & 1), _flip(y, (k >> 1) & 1), _flip(c, k & 1)
            src = 4 * px + 2 * py + pc
            pltpu.make_async_remote_copy(a_ref, o_ref.at[src], send_sems.at[k - 1], recv_sems.at[k - 1],
                                         device_id=(px, py, pc), device_id_type=MESH).wait_recv()
        for cp in sends:
            cp.wait_send()
        mine.wait()

    return pl.pallas_call(
        body, name=name, out_shape=jax.ShapeDtypeStruct((N_DEV,) + shape, a.dtype),
        in_specs=_any_specs(1), out_specs=pl.BlockSpec(memory_space=pl.ANY),
        scratch_shapes=[pltpu.SemaphoreType.DMA((N_DEV - 1,)), pltpu.SemaphoreType.DMA((N_DEV - 1,)),
                        pltpu.SemaphoreType.DMA(()), pltpu.VMEM(shape, a.dtype)],
    )(a)


def _chip_of(x, y, p):
    px, py = _flip(x, (p >> 1) & 1), _flip(y, p & 1)
    return 2 * px + py, px, py


class _ShardGather:
    def __init__(self, ins, outs, ici_send, ici_recv, d2d_send, d2d_recv):
        self.ins, self.outs = ins, outs
        self.sems = ici_send, ici_recv, d2d_send, d2d_recv
        self.x, self.y, self.c = _coords()
        self.me = 2 * self.x + self.y

    def _ici(self, i, p, slot):
        half = self.ins[i].shape[0] // 2
        rows = pl.ds(self.c * half, half)
        _, px, py = _chip_of(self.x, self.y, p)
        s = i * 3 + p - 1
        return pltpu.make_async_remote_copy(self.ins[i].at[rows], self.outs[i].at[slot, rows], self.sems[0].at[s],
                                            self.sems[1].at[s], device_id=(px, py, self.c), device_id_type=MESH)

    def _d2d(self, i, p, mine):
        half = self.ins[i].shape[0] // 2
        rows = pl.ds((self.c if mine else 1 - self.c) * half, half)
        src, _, _ = _chip_of(self.x, self.y, p)
        s = i * 3 + p - 1
        part = self.outs[i].at[src, rows]
        return pltpu.make_async_remote_copy(part, part, self.sems[2].at[s], self.sems[3].at[s],
                                            device_id=(self.x, self.y, 1 - self.c), device_id_type=MESH)

    def _each(self):
        return [(i, p) for i in range(len(self.ins)) for p in range(1, N_CHIPS)]

    def send(self):
        for i, p in self._each():
            self._ici(i, p, self.me).start()

    def forward(self):
        for i, p in self._each():
            self._ici(i, p, _chip_of(self.x, self.y, p)[0]).wait_recv()
            self._d2d(i, p, True).start()

    def finish(self):
        for i, p in self._each():
            self._d2d(i, p, False).wait_recv()
        for i, p in self._each():
            self._ici(i, p, self.me).wait_send()
            self._d2d(i, p, True).wait_send()


def gather_chip_shards(arrs, remote, *, name):
    n = len(arrs)
    far = [i for i in range(n) if remote[i]]

    def body(*refs):
        ins, outs = refs[:n], refs[n:2 * n]
        ici_send, ici_recv, d2d_send, d2d_recv, local_sems = refs[2 * n:2 * n + 5]
        bufs = refs[2 * n + 5:]
        me = 2 * lax.axis_index("x") + lax.axis_index("y")
        loads = []
        for i in range(n):
            cp = pltpu.make_async_copy(ins[i], bufs[i], local_sems.at[i])
            cp.start()
            loads.append(cp)
        gather = _ShardGather([ins[i] for i in far], [outs[i] for i in far], ici_send, ici_recv, d2d_send, d2d_recv)
        gather.send()
        stores = []
        for i in range(n):
            loads[i].wait()
            cp = pltpu.make_async_copy(bufs[i], outs[i].at[me], local_sems.at[i])
            cp.start()
            stores.append(cp)
        gather.forward()
        gather.finish()
        for cp in stores:
            cp.wait()

    dma = pltpu.SemaphoreType.DMA
    m = 3 * len(far)
    return pl.pallas_call(
        body, name=name,
        out_shape=[jax.ShapeDtypeStruct((N_CHIPS,) + a.shape, a.dtype) for a in arrs],
        in_specs=_any_specs(n), out_specs=_any_specs(n),
        scratch_shapes=[dma((m,)), dma((m,)), dma((m,)), dma((m,)), dma((n,))]
        + [pltpu.VMEM(a.shape, a.dtype) for a in arrs],
        compiler_params=pltpu.CompilerParams(vmem_limit_bytes=V7X_VMEM_LIMIT),
    )(*arrs)


def swap_halves(arrs, *, name):
    n = len(arrs)

    def body(*refs):
        ins, outs = refs[:n], refs[n:2 * n]
        send_sems, recv_sems = refs[2 * n:]
        x, y, c = _coords()
        cps = []
        for i in range(n):
            half = ins[i].shape[1] // 2
            cp = pltpu.make_async_remote_copy(ins[i].at[:, pl.ds((1 - c) * half, half)], outs[i], send_sems.at[i],
                                              recv_sems.at[i], device_id=(x, y, 1 - c), device_id_type=MESH)
            cp.start()
            cps.append(cp)
        for cp in cps:
            cp.wait()

    dma = pltpu.SemaphoreType.DMA
    return pl.pallas_call(
        body, name=name,
        out_shape=[jax.ShapeDtypeStruct((N_CHIPS, a.shape[1] // 2, a.shape[2]), a.dtype) for a in arrs],
        in_specs=_any_specs(n), out_specs=_any_specs(n), scratch_shapes=[dma((n,)), dma((n,))],
    )(*arrs)


def add_half(g, other, c_idx, *, name, out_dtype):
    _, R, C = g.shape
    half = R // 2
    tr = _row_tile(half)
    nt = half // tr

    def body(c_ref, g_ref, o_ref, out_ref):
        out_ref[...] = (g_ref[...].astype(F32) + o_ref[...].astype(F32)).astype(out_ref.dtype)

    return pl.pallas_call(
        body, name=name, out_shape=jax.ShapeDtypeStruct((N_CHIPS, half, C), out_dtype),
        grid_spec=pltpu.PrefetchScalarGridSpec(
            num_scalar_prefetch=1, grid=(N_CHIPS, nt),
            in_specs=[pl.BlockSpec((None, tr, C), lambda r, t, c_ref: (r, c_ref[0] * nt + t, 0)),
                      pl.BlockSpec((None, tr, C), lambda r, t, c_ref: (r, t, 0))],
            out_specs=pl.BlockSpec((None, tr, C), lambda r, t, c_ref: (r, t, 0))),
        compiler_params=_params(("parallel", "parallel")),
    )(c_idx, g, other)


class _ChipScatter:
    def __init__(self, ins, outs, send_sems, recv_sems):
        self.ins, self.outs, self.send_sems, self.recv_sems = ins, outs, send_sems, recv_sems
        self.x, self.y, self.c = _coords()

    def _copies(self):
        for i in range(len(self.ins)):
            for p in range(1, N_CHIPS):
                dst, px, py = _chip_of(self.x, self.y, p)
                s = i * 3 + p - 1
                yield pltpu.make_async_remote_copy(self.ins[i].at[dst], self.outs[i].at[p - 1], self.send_sems.at[s],
                                                   self.recv_sems.at[s], device_id=(px, py, self.c), device_id_type=MESH)

    def send(self):
        for cp in self._copies():
            cp.start()

    def finish(self):
        for cp in self._copies():
            cp.wait()


def scatter_to_chips(arrs, *, name):
    n = len(arrs)

    def body(*refs):
        scatter = _ChipScatter(refs[:n], refs[n:2 * n], *refs[2 * n:])
        scatter.send()
        scatter.finish()

    dma = pltpu.SemaphoreType.DMA
    return pl.pallas_call(
        body, name=name,
        out_shape=[jax.ShapeDtypeStruct((N_CHIPS - 1,) + a.shape[1:], a.dtype) for a in arrs],
        in_specs=_any_specs(n), out_specs=_any_specs(n), scratch_shapes=[dma((3 * n,)), dma((3 * n,))],
    )(*arrs)


def add_chips(h, got, r_idx, *, name):
    _, R, C = h.shape
    tr = _row_tile(R)

    def body(r_ref, h_ref, g_ref, out_ref):
        acc = h_ref[...].astype(F32)
        for p in range(N_CHIPS - 1):
            acc = acc + g_ref[p].astype(F32)
        out_ref[...] = acc

    return pl.pallas_call(
        body, name=name, out_shape=jax.ShapeDtypeStruct((R, C), F32),
        grid_spec=pltpu.PrefetchScalarGridSpec(
            num_scalar_prefetch=1, grid=(R // tr,),
            in_specs=[pl.BlockSpec((None, tr, C), lambda t, r_ref: (r_ref[0], t, 0)),
                      pl.BlockSpec((N_CHIPS - 1, tr, C), lambda t, r_ref: (0, t, 0))],
            out_specs=pl.BlockSpec((tr, C), lambda t, r_ref: (t, 0))),
        compiler_params=_params(("parallel",)),
    )(r_idx, h, got)


def join_halves(arrs, *, name):
    n = len(arrs)

    def body(*refs):
        ins, outs = refs[:n], refs[n:2 * n]
        send_sems, recv_sems, local_sems = refs[2 * n:2 * n + 3]
        bufs = refs[2 * n + 3:]
        x, y, c = _coords()
        loads, sends, stores = [], [], []
        for i in range(n):
            cp = pltpu.make_async_copy(ins[i], bufs[i], local_sems.at[i])
            cp.start()
            loads.append(cp)
        for i in range(n):
            half = ins[i].shape[0]
            cp = pltpu.make_async_remote_copy(ins[i], outs[i].at[pl.ds(c * half, half)], send_sems.at[i], recv_sems.at[i],
                                              device_id=(x, y, 1 - c), device_id_type=MESH)
            cp.start()
            sends.append(cp)
        for i in range(n):
            half = ins[i].shape[0]
            loads[i].wait()
            cp = pltpu.make_async_copy(bufs[i], outs[i].at[pl.ds(c * half, half)], local_sems.at[i])
            cp.start()
            stores.append(cp)
        for i in range(n):
            half = ins[i].shape[0]
            pltpu.make_async_remote_copy(ins[i], outs[i].at[pl.ds((1 - c) * half, half)], send_sems.at[i],
                                         recv_sems.at[i], device_id=(x, y, 1 - c), device_id_type=MESH).wait_recv()
        for i in range(n):
            sends[i].wait_send()
            stores[i].wait()

    dma = pltpu.SemaphoreType.DMA
    return pl.pallas_call(
        body, name=name,
        out_shape=[jax.ShapeDtypeStruct((2 * a.shape[0], a.shape[1]), a.dtype) for a in arrs],
        in_specs=_any_specs(n), out_specs=_any_specs(n),
        scratch_shapes=[dma((n,)), dma((n,)), dma((n,))] + [pltpu.VMEM(a.shape, a.dtype) for a in arrs],
        compiler_params=pltpu.CompilerParams(vmem_limit_bytes=V7X_VMEM_LIMIT),
    )(*arrs)


def pair_sums(grads, wire_dtypes, tag):
    c_idx = jnp.reshape(lax.axis_index("c"), (1,)).astype(jnp.int32)
    theirs = swap_halves(grads, name=f"rs_swap_halves_{tag}")
    return [add_half(g, o, c_idx, name=f"rs_add_half_{tag}{i}", out_dtype=wire_dtypes[i])
            for i, (g, o) in enumerate(zip(grads, theirs))]


def chip_sums(pairs, gots, tag):
    r_idx = jnp.reshape(2 * lax.axis_index("x") + lax.axis_index("y"), (1,)).astype(jnp.int32)
    return [add_chips(h, g, r_idx, name=f"rs_add_chips_{tag}{i}") for i, (h, g) in enumerate(zip(pairs, gots))]


def _chip_major(w, axis):
    n = w.shape[axis] // N_CHIPS
    parts = w.reshape(w.shape[:axis] + (N_CHIPS, n) + w.shape[axis + 1:])
    return jnp.moveaxis(parts, axis, 0)


def _from_chip_major(g, axis):
    g = jnp.moveaxis(g, 0, axis)
    return g.reshape(g.shape[:axis] + (g.shape[axis] * g.shape[axis + 1],) + g.shape[axis + 2:])


def kernel(x, c, norm_mix, norm_ffn, w_mod, b_mod, w_qkv, w_o_attn, w_in_ssm, a_re, a_im, log_dt, b_re, b_im, c_re, c_im, d_skip, w_glu, b_glu, w_o_ssm, w_up, conv_w, conv_b, w_down, norm_out, w_fin, b_fin, loss_target, m_norm_mix, m_norm_ffn, m_w_mod, m_b_mod, m_w_qkv, m_w_o_attn, m_w_in_ssm, m_a_re, m_a_im, m_log_dt, m_b_re, m_b_im, m_c_re, m_c_im, m_d_skip, m_w_glu, m_b_glu, m_w_o_ssm, m_w_up, m_conv_w, m_conv_b, m_w_down, m_norm_out, m_w_fin, m_b_fin, v_norm_mix, v_norm_ffn, v_w_mod, v_b_mod, v_w_qkv, v_w_o_attn, v_w_in_ssm, v_a_re, v_a_im, v_log_dt, v_b_re, v_b_im, v_c_re, v_c_im, v_d_skip, v_w_glu, v_b_glu, v_w_o_ssm, v_w_up, v_conv_w, v_conv_b, v_w_down, v_norm_out, v_w_fin, v_b_fin):
    B, S, D = x.shape
    T = B * S
    F2 = conv_b.shape[1]
    F = F2 // 2
    G, P = a_re.shape[1], a_re.shape[2]
    H = b_re.shape[3]
    mx, my, mc = _coords()
    chip = 2 * mx + my
    dev = 4 * mx + 2 * my + mc
    BG = N_DEV * B
    mod_w = w_mod.shape[2]
    fin_w = w_fin.shape[1]

    c_all = all_gather8(c, name="gather_c").reshape(BG, D)
    c_act = silu_rows(c_all, name="silu_c")
    b_mod_mine = lax.dynamic_slice(b_mod, (0, chip * mod_w), (2, mod_w))
    b_fin_mine = lax.dynamic_slice(b_fin, (chip * fin_w,), (fin_w,))
    cond = [matmul(c_act, w_mod[i], bias=b_mod_mine[i], name=f"mod_proj_{i}") for i in range(2)]
    cond.append(matmul(c_act, w_fin, bias=b_fin_mine, name="fin_proj"))
    cond_all = all_gather8(jnp.concatenate(cond, axis=1), name="gather_cond")
    cond_all = cond_all[::2]
    cond_rows = lax.dynamic_slice(cond_all, (0, dev * B, 0), (N_CHIPS, B, cond_all.shape[2]))
    mods = []
    for i in range(2):
        full = cond_rows[:, :, i * mod_w:(i + 1) * mod_w].transpose(1, 0, 2).reshape(B, N_CHIPS * mod_w)
        mods.append([full[:, k * D:(k + 1) * D] for k in range(6)])
    fin = cond_rows[:, :, 2 * mod_w:].transpose(1, 0, 2).reshape(B, N_CHIPS * fin_w)
    sh_f, sc_f = fin[:, :D], fin[:, D:]

    rows1024 = jnp.concatenate([w_o_attn[0], w_in_ssm[0], w_glu[0], w_o_ssm[0], w_down.reshape(-1, D)], axis=0)
    shards = [w_qkv[0].astype(BF16), rows1024.astype(BF16), w_up[0].astype(BF16), w_up[1].astype(BF16)]
    W_qkv, *own_slots = gather_chip_shards(shards, [True, False, False, False], name="gather_weights")
    Dq = D // N_CHIPS
    Fq = F // N_CHIPS
    small = jnp.concatenate([conv_w.reshape(6, -1), jnp.pad(d_skip, ((0, 0), (0, conv_w.shape[2] - Dq))),
                             jnp.pad(b_glu, ((0, 0), (0, conv_w.shape[2] - Dq)))], axis=0)
    small_all = all_gather8(small, name="gather_small")[::2]
    conv_w_full = _from_chip_major(small_all[:, :6].reshape(N_CHIPS, 2, 3, -1), 2)
    d_skip_full = small_all[:, 6, :Dq].reshape(1, D)
    b_glu_full = small_all[:, 7, :Dq].reshape(D)

    x0 = x.reshape(T, D)
    tgt = loss_target.reshape(T, D)

    def ffn_fwd(xprev, y, gate, i):
        sh2, sc2 = mods[i][3], mods[i][4]
        xin, h2 = res_norm_mod_fwd(xprev, y, gate, norm_ffn[i], sh2, sc2, B=B, S=S, name=f"ffn_norm_{i}")
        up = matmul(h2, W_up[i], b_chips=True, out_dtype=BF16, name=f"ffn_up_{i}")
        act = conv_gate_fwd(up, conv_w_full[i], conv_b[i:i + 1], B=B, S=S, name=f"ffn_conv_{i}")
        yf = matmul(act, W_down[i], name=f"ffn_down_{i}")
        return xin, yf, (xin, h2, up, act, yf)

    sh1, sc1, g1 = mods[0][0], mods[0][1], mods[0][2]
    h1a = norm_mod_fwd(x0, norm_mix[0], sh1, sc1, B=B, S=S, name="att_norm")
    qkv = matmul(h1a, W_qkv, out_dtype=BF16, b_chips=True, name="att_qkv")
    o2, ltot, g_rows, W_up0, W_up1 = attn_fwd_pairs(qkv, shards[1:], own_slots, B=B, S=S, name="att_fwd")
    W_up = [W_up0, W_up1]
    W_o_attn = g_rows[:, 0 * Dq:1 * Dq].reshape(D, D)
    W_in = g_rows[:, 1 * Dq:2 * Dq].reshape(D, D)
    W_glu = g_rows[:, 2 * Dq:3 * Dq].reshape(D, D)
    W_o_ssm = g_rows[:, 3 * Dq:4 * Dq].reshape(D, D)
    W_down = [g_rows[:, 4 * Dq + i * Fq:4 * Dq + (i + 1) * Fq].reshape(F, D) for i in range(2)]
    ya = matmul(o2, W_o_attn, name="att_out")
    x1, yf0, ffn0 = ffn_fwd(x0, ya, g1, 0)

    lr, li, bbr, bbi = _ssm_discretize(a_re[0], a_im[0], log_dt[0], b_re[0], b_im[0])
    J = G // GROUPS_PER_BLOCK
    Wst = GROUPS_PER_BLOCK * P
    bre_blk = _block_diag_in(bbr).astype(BF16)
    bim_blk = _block_diag_in(bbi).astype(BF16)
    cre_blk = _block_diag_out(c_re[0]).astype(BF16)
    cim_blk = _block_diag_out(c_im[0]).astype(BF16)
    lr8 = jnp.broadcast_to(lr.reshape(J, 1, Wst), (J, SEGMENTS, Wst))
    li8 = jnp.broadcast_to(li.reshape(J, 1, Wst), (J, SEGMENTS, Wst))
    sh1s, sc1s, g1s = mods[1][0], mods[1][1], mods[1][2]
    x2, h1s = res_norm_mod_fwd(x1, yf0, mods[0][5], norm_mix[1], sh1s, sc1s, B=B, S=S, name="ssm_norm")
    h1p = _interleave(h1s, B, S)
    u = matmul(h1p, W_in, name="ssm_in")
    y_ssm = ssm_fwd(u, bre_blk, bim_blk, cre_blk, cim_blk, lr8, li8, d_skip_full, B=B, S=S, name="ssm_scan_fwd")
    zb = gelu_fwd(y_ssm, B=B, S=S, name="ssm_gelu")
    s_glu = matmul(zb, W_glu, bias=b_glu_full, name="ssm_glu_proj")
    gb = glu_fwd(y_ssm, s_glu, B=B, S=S, name="ssm_glu")
    ys_p = matmul(gb, W_o_ssm, name="ssm_out")
    ys = _deinterleave(ys_p, B, S)
    x3, yf1, ffn1 = ffn_fwd(x2, ys, g1s, 1)
    x4 = gate_res_fwd(x3, yf1, mods[1][5], B=B, S=S, name="ffn_res_1")

    dx4, dyf1, loss_p, dsh_f, dsc_f, dnorm_out, dg2_1 = final_loss(x4, tgt, norm_out, sh_f, sc_f, yf1, mods[1][5],
                                                                   B=B, S=S, name="loss_head")
    loss = lax.psum(jnp.sum(loss_p), ("x", "y", "c"))

    def ffn_bwd(dxo, dyf, i, saved, y_prev, gate_prev):
        xin, h2, up, act, yf = saved
        sc2 = mods[i][4]
        dact = matmul(dyf, W_down[i], tb=True, out_dtype=BF16, name=f"ffn_down_dx_{i}")
        dW_down = matmul(act, dyf, ta=True, out_dtype=BF16, name=f"ffn_down_dw_{i}")
        dup, dcw, dcb = conv_gate_bwd(up, dact, conv_w_full[i], conv_b[i:i + 1], B=B, S=S, name=f"ffn_conv_bwd_{i}")
        dh2 = matmul(dup, W_up[i], tb=True, b_chips=True, name=f"ffn_up_dx_{i}")
        dW_up = matmul(h2, dup, ta=True, b_chips=True, out_chips=True, out_dtype=BF16, name=f"ffn_up_dw_{i}")
        dxin, dy_prev, dsh2, dsc2, dnf, dgate_prev = norm_mod_bwd_gate(
            dh2, xin, dxo, norm_ffn[i], sc2, y_prev, gate_prev, B=B, S=S, name=f"ffn_norm_bwd_{i}")
        dconv_w = jnp.sum(dcw, axis=0).transpose(1, 0, 2).reshape(3, F2)
        return dxin, dy_prev, dgate_prev, dict(dW_down=dW_down, dW_up=dW_up, dconv_b=jnp.sum(dcb, axis=0).reshape(F2),
                                               dconv_w=dconv_w, dnorm_ffn=jnp.sum(dnf, axis=0), dsh2=dsh2, dsc2=dsc2)

    dx3, dys, dg1s, gf1 = ffn_bwd(dx4, dyf1, 1, ffn1, ys, g1s)
    gf1["dg2"] = dg2_1

    dys_p = _interleave(dys, B, S)
    dgb = matmul(dys_p, W_o_ssm, tb=True, name="ssm_out_dx")
    dW_o_ssm = matmul(gb, dys_p, ta=True, out_dtype=BF16, name="ssm_out_dw")
    ds_glu, dz1, db_glu = glu_bwd1(y_ssm, s_glu, dgb, B=B, S=S, name="ssm_glu_bwd1")
    dz2 = matmul(ds_glu, W_glu, tb=True, name="ssm_glu_dx")
    dW_glu = matmul(zb, ds_glu, ta=True, out_dtype=BF16, name="ssm_glu_dw")
    dy_ssm = glu_bwd2(y_ssm, dz1, dz2, B=B, S=S, name="ssm_glu_bwd2")
    du, dbre, dbim, dcre, dcim, dlr8, dli8, ddsk = ssm_bwd(u, dy_ssm, bre_blk, bim_blk, cre_blk, cim_blk, lr8, li8,
                                                           d_skip_full, B=B, S=S, name="ssm_scan_bwd")
    dub = du.astype(BF16)
    dh1p = matmul(dub, W_in, tb=True, name="ssm_in_dx")
    dW_in = matmul(h1p, dub, ta=True, out_dtype=BF16, name="ssm_in_dw")
    dx2, dyf0, dsh1s, dsc1s, dnm1, dg2_0 = norm_mod_bwd_gate(_deinterleave(dh1p, B, S), x2, dx3, norm_mix[1], sc1s,
                                                             yf0, mods[0][5], B=B, S=S, name="ssm_norm_bwd")
    dlr = jnp.sum(dlr8, axis=(0, 2)).reshape(G, P)
    dli = jnp.sum(dli8, axis=(0, 2)).reshape(G, P)
    dbbr = _block_diag_in_grad(jnp.sum(dbre, axis=0), G, P, H)
    dbbi = _block_diag_in_grad(jnp.sum(dbim, axis=0), G, P, H)
    dc_re = _block_diag_out_grad(jnp.sum(dcre, axis=0), G, H, P)
    dc_im = _block_diag_out_grad(jnp.sum(dcim, axis=0), G, H, P)
    dd_skip = jnp.sum(ddsk, axis=0).reshape(D)

    dx1, dya, dg1, gf0 = ffn_bwd(dx2, dyf0, 0, ffn0, ya, g1)
    gf0["dg2"] = dg2_0

    do2 = matmul(dya, W_o_attn, tb=True, out_dtype=BF16, name="att_out_dx")
    dW_o_attn = matmul(o2, dya, ta=True, out_dtype=BF16, name="att_out_dw")
    g_rows_cm = jnp.concatenate([dW_o_attn.reshape(N_CHIPS, Dq, D), dW_in.reshape(N_CHIPS, Dq, D),
                                 dW_glu.reshape(N_CHIPS, Dq, D), dW_o_ssm.reshape(N_CHIPS, Dq, D),
                                 gf0["dW_down"].reshape(N_CHIPS, Fq, D), gf1["dW_down"].reshape(N_CHIPS, Fq, D)], axis=1)
    pairs_a = pair_sums([g_rows_cm, gf0["dW_up"], gf1["dW_up"]], [BF16, BF16, BF16], "a")
    dqkv, *gots_a = attn_bwd_pairs(qkv, ltot, do2, pairs_a, B=B, S=S, name="att_bwd")
    dh1a = matmul(dqkv, _from_chip_major(W_qkv, 1), tb=True, name="att_qkv_dx")
    dW_qkv = _chip_major(matmul(h1a, dqkv, ta=True, b_chips=True, out_dtype=BF16, name="att_qkv_dw"), 1)
    grad_x, dsh1, dsc1, dnm0 = norm_mod_bwd(dh1a, x0, dx1, norm_mix[0], sc1, B=B, S=S, name="att_norm_bwd")

    dmod_rows = jnp.concatenate([dsh1, dsc1, dg1, gf0["dsh2"], gf0["dsc2"], gf0["dg2"],
                                 dsh1s, dsc1s, dg1s, gf1["dsh2"], gf1["dsc2"], gf1["dg2"], dsh_f, dsc_f], axis=1)
    dmod_all = all_gather8(dmod_rows, name="gather_dmod").reshape(BG, 14 * D)
    grad_w_mod = jnp.stack([
        matmul(c_act, lax.dynamic_slice(dmod_all, (0, i * 6 * D + chip * mod_w), (BG, mod_w)), ta=True,
               name=f"mod_dw_{i}") for i in range(2)])
    grad_w_fin = matmul(c_act, lax.dynamic_slice(dmod_all, (0, 12 * D + chip * fin_w), (BG, fin_w)), ta=True,
                        name="fin_dw")

    parts = [jnp.concatenate([jnp.sum(dnm0, axis=0), jnp.sum(dnm1, axis=0)]),
             jnp.concatenate([gf0["dnorm_ffn"], gf1["dnorm_ffn"]]),
             jnp.sum(dmod_rows[:, :12 * D], axis=0),
             dlr.reshape(-1), dli.reshape(-1), dbbr.reshape(-1), dbbi.reshape(-1), dc_re.reshape(-1), dc_im.reshape(-1),
             dd_skip, jnp.sum(db_glu, axis=0),
             jnp.sum(dnorm_out, axis=0), jnp.sum(dmod_rows[:, 12 * D:], axis=0),
             gf0["dconv_w"].reshape(-1), gf1["dconv_w"].reshape(-1), gf0["dconv_b"], gf1["dconv_b"]]
    sizes = [int(p.shape[0]) for p in parts]
    flat = jnp.concatenate(parts)
    width = 1024
    quantum = N_CHIPS * 16 * width
    padded = -(-flat.shape[0] // quantum) * quantum
    small_cm = jnp.pad(flat, (0, padded - flat.shape[0])).reshape(N_CHIPS, -1, width)

    pairs_b = pair_sums([dW_qkv, small_cm], [BF16, F32], "b")
    gots_b = scatter_to_chips(pairs_b, name="rs_scatter_to_chips")
    r_qkv, r_small, r_rows, r_up0, r_up1 = join_halves(
        chip_sums(pairs_b, gots_b, "b") + chip_sums(pairs_a, gots_a, "a"), name="rs_join_halves")
    grad_w_qkv = r_qkv[None]
    grad_w_o_attn = r_rows[0 * Dq:1 * Dq][None]
    grad_w_in_ssm = r_rows[1 * Dq:2 * Dq][None]
    grad_w_glu = r_rows[2 * Dq:3 * Dq][None]
    grad_w_o_ssm = r_rows[3 * Dq:4 * Dq][None]
    grad_w_down = r_rows[4 * Dq:].reshape(2, Fq, D)
    grad_w_up = jnp.stack([r_up0, r_up1])
    summed = all_gather8(r_small, name="gather_small_grads")[::2].reshape(-1)
    offs = [0]
    for s_ in sizes:
        offs.append(offs[-1] + s_)
    (s_nm, s_nf, s_bmod, s_lr, s_li, s_bbr, s_bbi, s_cre, s_cim, s_dsk, s_bglu, s_no, s_bfin, s_cw0, s_cw1, s_cb0,
     s_cb1) = [summed[offs[i]:offs[i + 1]] for i in range(len(sizes))]
    _, disc_vjp = jax.vjp(_ssm_discretize, a_re[0], a_im[0], log_dt[0], b_re[0], b_im[0])
    ga_re, ga_im, glog_dt, gb_re, gb_im = disc_vjp((s_lr.reshape(G, P), s_li.reshape(G, P), s_bbr.reshape(G, P, H),
                                                    s_bbi.reshape(G, P, H)))
    grad_norm_mix = s_nm.reshape(2, D)
    grad_norm_ffn = s_nf.reshape(2, D)
    grad_b_mod = s_bmod.reshape(2, 6 * D)
    grad_c_re = s_cre.reshape(1, G, H, P)
    grad_c_im = s_cim.reshape(1, G, H, P)
    grad_d_skip = lax.dynamic_slice(s_dsk, (chip * Dq,), (Dq,)).reshape(1, Dq)
    grad_b_glu = lax.dynamic_slice(s_bglu, (chip * Dq,), (Dq,)).reshape(1, Dq)
    cw_full = jnp.stack([s_cw0.reshape(3, F2), s_cw1.reshape(3, F2)])
    grad_conv_w = lax.dynamic_slice(cw_full, (0, 0, chip * (F2 // N_CHIPS)), (2, 3, F2 // N_CHIPS))
    grad_conv_b = jnp.stack([s_cb0, s_cb1])
    grad_norm_out = s_no
    grad_b_fin = s_bfin

    grads = dict(
        norm_mix=grad_norm_mix, norm_ffn=grad_norm_ffn, w_mod=grad_w_mod, b_mod=grad_b_mod, w_qkv=grad_w_qkv,
        w_o_attn=grad_w_o_attn, w_in_ssm=grad_w_in_ssm, a_re=ga_re[None], a_im=ga_im[None], log_dt=glog_dt[None],
        b_re=gb_re[None], b_im=gb_im[None], c_re=grad_c_re, c_im=grad_c_im, d_skip=grad_d_skip, w_glu=grad_w_glu,
        b_glu=grad_b_glu, w_o_ssm=grad_w_o_ssm, w_up=grad_w_up, conv_w=grad_conv_w, conv_b=grad_conv_b,
        w_down=grad_w_down, norm_out=grad_norm_out, w_fin=grad_w_fin, b_fin=grad_b_fin)
    weights = dict(
        norm_mix=norm_mix, norm_ffn=norm_ffn, w_mod=w_mod, b_mod=b_mod, w_qkv=w_qkv, w_o_attn=w_o_attn,
        w_in_ssm=w_in_ssm, a_re=a_re, a_im=a_im, log_dt=log_dt, b_re=b_re, b_im=b_im, c_re=c_re, c_im=c_im,
        d_skip=d_skip, w_glu=w_glu, b_glu=b_glu, w_o_ssm=w_o_ssm, w_up=w_up, conv_w=conv_w, conv_b=conv_b,
        w_down=w_down, norm_out=norm_out, w_fin=w_fin, b_fin=b_fin)
    m_in = dict(
        norm_mix=m_norm_mix, norm_ffn=m_norm_ffn, w_mod=m_w_mod, b_mod=m_b_mod, w_qkv=m_w_qkv, w_o_attn=m_w_o_attn,
        w_in_ssm=m_w_in_ssm, a_re=m_a_re, a_im=m_a_im, log_dt=m_log_dt, b_re=m_b_re, b_im=m_b_im, c_re=m_c_re,
        c_im=m_c_im, d_skip=m_d_skip, w_glu=m_w_glu, b_glu=m_b_glu, w_o_ssm=m_w_o_ssm, w_up=m_w_up, conv_w=m_conv_w,
        conv_b=m_conv_b, w_down=m_w_down, norm_out=m_norm_out, w_fin=m_w_fin, b_fin=m_b_fin)
    v_in = dict(
        norm_mix=v_norm_mix, norm_ffn=v_norm_ffn, w_mod=v_w_mod, b_mod=v_b_mod, w_qkv=v_w_qkv, w_o_attn=v_w_o_attn,
        w_in_ssm=v_w_in_ssm, a_re=v_a_re, a_im=v_a_im, log_dt=v_log_dt, b_re=v_b_re, b_im=v_b_im, c_re=v_c_re,
        c_im=v_c_im, d_skip=v_d_skip, w_glu=v_w_glu, b_glu=v_b_glu, w_o_ssm=v_w_o_ssm, w_up=v_w_up, conv_w=v_conv_w,
        conv_b=v_conv_b, w_down=v_w_down, norm_out=v_norm_out, w_fin=v_w_fin, b_fin=v_b_fin)
    names = list(weights)
    for n_ in names:
        grads[n_] = grads[n_].reshape(weights[n_].shape)

    big = ("w_mod", "w_qkv", "w_o_attn", "w_in_ssm", "w_glu", "w_o_ssm", "w_up", "w_down", "w_fin")
    delta, new_m, new_v = {}, {}, {}
    for n_ in big:
        shp = weights[n_].shape
        two_d = lambda a: a.reshape(-1, shp[-1])
        d_, m_, v_ = adamw(two_d(weights[n_]), two_d(grads[n_]), two_d(m_in[n_]), two_d(v_in[n_]), name=f"adamw_{n_}")
        delta[n_], new_m[n_], new_v[n_] = d_.reshape(shp), m_.reshape(shp), v_.reshape(shp)
    rest = [n_ for n_ in names if n_ not in big]
    small_out = adamw_many([weights[n_] for n_ in rest], [grads[n_] for n_ in rest], [m_in[n_] for n_ in rest],
                           [v_in[n_] for n_ in rest], name="adamw_small")
    for n_, (d_, m_, v_) in zip(rest, small_out):
        delta[n_], new_m[n_], new_v[n_] = d_, m_, v_

    return (loss, grad_x.reshape(B, S, D), *[grads[n_] for n_ in names], *[delta[n_] for n_ in names],
            *[new_m[n_] for n_ in names], *[new_v[n_] for n_ in names])
```

```python
import math

import jax
import jax.numpy as jnp
from jax import lax
from jax.experimental import pallas as pl
from jax.experimental.pallas import tpu as pltpu

F32 = jnp.float32
BF16 = jnp.bfloat16
MESH = pl.DeviceIdType.MESH

HEAD_DIM = 64
SSM_GROUP = 16
STATE = 64
GROUPS_PER_BLOCK = 8
SEGMENTS = 16
SCAN_UNROLL = 4
EPS = 1e-6
ADAM_LR = 0.001
ADAM_B1 = 0.9
ADAM_B2 = 0.999
ADAM_EPS = 1e-08
ADAM_WD = 0.01
ADAM_STEP = 10
N_CHIPS = 4
N_DEV = 8
V7X_VMEM_LIMIT = 56 * 1024 * 1024
ATT_BLOCK = 128
ATT_HEADS = 8
ATT_HEADS_BWD = 8


def _tile(n, prefs):
    for p in prefs:
        if n % p == 0:
            return p
    return n


def _row_tile(n, cap=1024, mult=16):
    for t in range(min(n, cap) // mult * mult, 0, -mult):
        if n % t == 0:
            return t
    return n


def _params(sem, vmem=V7X_VMEM_LIMIT):
    return pltpu.CompilerParams(dimension_semantics=sem, vmem_limit_bytes=vmem)


def matmul(a, b, *, ta=False, tb=False, bias=None, out_dtype=F32, b_chips=False, out_chips=False, name):
    a_parts = a.shape[0] if a.ndim == 3 else 1
    if a_parts > 1:
        assert not ta
        M, K = a.shape[1], a_parts * a.shape[2]
    elif ta:
        K, M = a.shape
    else:
        M, K = a.shape
    b_parts = b.shape[0] if b_chips else 1
    b_rows, b_cols = (b.shape[1], b_parts * b.shape[2]) if b_chips else b.shape
    if tb:
        N, Kb = b_rows, b_cols
    else:
        Kb, N = b_rows, b_cols
    assert K == Kb, (a.shape, b.shape, ta, tb)
    n_cut = math.gcd(N // (N_CHIPS if out_chips else 1), N // (b_parts if not tb else 1))
    k_cut = math.gcd(K // (b_parts if tb else 1), K // a_parts)
    tm = _tile(M, (1024, 1408, 512, 256, 128))
    tn = _tile(n_cut, (1024, 1408, 768, 512, 256, 128))
    tk = k_cut if k_cut <= 2816 else _tile(k_cut, (1024, 512, 256, 128))
    nk = K // tk
    npc = N // N_CHIPS // tn
    npb = N // b_parts // tn
    kpb = K // b_parts // tk
    kpa = K // a_parts // tk
    dims = (((0,) if ta else (1,), (1,) if tb else (0,)), ((), ()))

    def body(*refs):
        a_ref, b_ref = refs[:2]
        bias_ref = refs[2] if bias is not None else None
        o_ref = refs[-2] if nk > 1 else refs[-1]

        def finish(r):
            if bias_ref is not None:
                r = r + bias_ref[...]
            o_ref[...] = r.astype(o_ref.dtype)

        prod = lax.dot_general(a_ref[...].astype(BF16), b_ref[...].astype(BF16), dims, preferred_element_type=F32)
        if nk == 1:
            finish(prod)
            return
        acc_ref = refs[-1]
        k = pl.program_id(2)

        @pl.when(k == 0)
        def _():
            acc_ref[...] = prod

        @pl.when(k > 0)
        def _():
            acc_ref[...] += prod

        @pl.when(k == nk - 1)
        def _():
            finish(acc_ref[...])

    if a_parts > 1:
        a_spec = pl.BlockSpec((None, tm, tk), lambda i, j, k: (lax.div(k, kpa), i, lax.rem(k, kpa)))
    else:
        a_spec = pl.BlockSpec((tk, tm), lambda i, j, k: (k, i)) if ta else pl.BlockSpec((tm, tk), lambda i, j, k: (i, k))
    if not b_chips:
        b_spec = pl.BlockSpec((tn, tk), lambda i, j, k: (j, k)) if tb else pl.BlockSpec((tk, tn), lambda i, j, k: (k, j))
    elif tb:
        b_spec = pl.BlockSpec((None, tn, tk), lambda i, j, k: (lax.div(k, kpb), j, lax.rem(k, kpb)))
    else:
        b_spec = pl.BlockSpec((None, tk, tn), lambda i, j, k: (lax.div(j, npb), k, lax.rem(j, npb)))
    in_specs = [a_spec, b_spec]
    args = [a, b]
    if bias is not None:
        in_specs.append(pl.BlockSpec((1, tn), lambda i, j, k: (0, j)))
        args.append(bias.reshape(1, N).astype(F32))
    if out_chips:
        out_shape = jax.ShapeDtypeStruct((N_CHIPS, M, N // N_CHIPS), out_dtype)
        out_spec = pl.BlockSpec((None, tm, tn), lambda i, j, k: (lax.div(j, npc), i, lax.rem(j, npc)))
    else:
        out_shape = jax.ShapeDtypeStruct((M, N), out_dtype)
        out_spec = pl.BlockSpec((tm, tn), lambda i, j, k: (i, j))
    return pl.pallas_call(
        body, name=name,
        out_shape=out_shape,
        grid=(M // tm, N // tn, nk),
        in_specs=in_specs,
        out_specs=out_spec,
        scratch_shapes=[pltpu.VMEM((tm, tn), F32)] if nk > 1 else [],
        compiler_params=_params(("parallel", "parallel", "arbitrary")),
    )(*args)


def rowwise(fn, tiled, per_seq, glob, out_tiled, out_seq, *, B, S, name, rows=512):
    tm = _tile(S, (rows, 128, 64, 32, 16, 8))
    nt = S // tm
    n_in = len(tiled) + len(per_seq) + len(glob)
    n_ot = len(out_tiled)

    def body(*refs):
        ins = refs[:n_in]
        outs = refs[n_in:]
        vals = fn(*[r[...] for r in ins])
        if not isinstance(vals, (tuple, list)):
            vals = (vals,)
        assert len(vals) == len(outs), (name, len(vals), len(outs))
        for o_ref, v in zip(outs[:n_ot], vals[:n_ot]):
            o_ref[...] = v.astype(o_ref.dtype)
        t = pl.program_id(1)
        for o_ref, v in zip(outs[n_ot:], vals[n_ot:]):
            def first(o_ref=o_ref, v=v):
                o_ref[...] = v.astype(F32)

            def later(o_ref=o_ref, v=v):
                o_ref[...] += v.astype(F32)

            pl.when(t == 0)(first)
            pl.when(t > 0)(later)

    in_specs = [pl.BlockSpec((tm, a.shape[1]), lambda b, t: (b * nt + t, 0)) for a in tiled]
    in_specs += [pl.BlockSpec((None, 1, a.shape[1]), lambda b, t: (b, 0, 0)) for a in per_seq]
    in_specs += [pl.BlockSpec(a.shape, lambda b, t: (0,) * a.ndim) for a in glob]
    out_shape = [jax.ShapeDtypeStruct((B * S, w), dt) for w, dt in out_tiled]
    out_shape += [jax.ShapeDtypeStruct((B, 1, w), F32) for w in out_seq]
    out_specs = [pl.BlockSpec((tm, w), lambda b, t: (b * nt + t, 0)) for w, _ in out_tiled]
    out_specs += [pl.BlockSpec((None, 1, w), lambda b, t: (b, 0, 0)) for w in out_seq]
    res = pl.pallas_call(
        body, name=name, out_shape=out_shape, grid=(B, nt), in_specs=in_specs, out_specs=out_specs,
        compiler_params=_params(("parallel", "arbitrary")),
    )(*tiled, *[a.reshape(B, 1, a.shape[1]) for a in per_seq], *glob)
    res = list(res)
    for i in range(n_ot, len(res)):
        res[i] = res[i].reshape(B, res[i].shape[-1])
    return res


def _rms(x):
    r = lax.rsqrt(jnp.mean(x * x, axis=-1, keepdims=True) + EPS)
    return x * r, r


def norm_mod_fwd(x, g, sh, sc, *, B, S, name):
    def fn(x, sh, sc, g):
        xn, _ = _rms(x)
        return (xn * g) * (1.0 + sc) + sh

    return rowwise(fn, [x], [sh, sc], [g.reshape(1, -1)], [(x.shape[1], BF16)], [], B=B, S=S, name=name)[0]


def _norm_mod_bwd_math(dh, x, sc, g):
    xn, r = _rms(x)
    y = xn * g
    dy = dh * (1.0 + sc)
    dxn = dy * g
    dx = r * (dxn - xn * jnp.mean(dxn * xn, axis=-1, keepdims=True))
    dsh = jnp.sum(dh, axis=0, keepdims=True)
    dsc = jnp.sum(dh * y, axis=0, keepdims=True)
    dg = jnp.sum(dy * xn, axis=0, keepdims=True)
    return dx, dsh, dsc, dg


def norm_mod_bwd(dh, x, dres, g, sc, *, B, S, name):
    D = x.shape[1]

    def fn(dh, x, dres, sc, g):
        dx, dsh, dsc, dg = _norm_mod_bwd_math(dh.astype(F32), x, sc, g)
        return dres + dx, dsh, dsc, dg

    return rowwise(fn, [dh, x, dres], [sc], [g.reshape(1, -1)], [(D, F32)], [D, D, D], B=B, S=S, name=name)


def norm_mod_bwd_gate(dh, x, dres, g, sc, y_prev, gate_prev, *, B, S, name):
    D = x.shape[1]

    def fn(dh, x, dres, y, sc, gate, g):
        dx, dsh, dsc, dg = _norm_mod_bwd_math(dh.astype(F32), x, sc, g)
        dx = dres + dx
        return dx, gate * dx, dsh, dsc, dg, jnp.sum(dx * y, axis=0, keepdims=True)

    return rowwise(fn, [dh, x, dres, y_prev], [sc, gate_prev], [g.reshape(1, -1)], [(D, F32), (D, BF16)],
                   [D, D, D, D], B=B, S=S, name=name)


def gate_res_fwd(x, y, gate, *, B, S, name):
    return rowwise(lambda x, y, g: x + g * y, [x, y], [gate], [], [(x.shape[1], F32)], [], B=B, S=S, name=name)[0]


def res_norm_mod_fwd(x, y, gate, g, sh, sc, *, B, S, name):
    D = x.shape[1]

    def fn(x, y, gate, sh, sc, g):
        x = x + gate * y
        xn, _ = _rms(x)
        return x, (xn * g) * (1.0 + sc) + sh

    return rowwise(fn, [x, y], [gate, sh, sc], [g.reshape(1, -1)], [(D, F32), (D, BF16)], [], B=B, S=S, name=name)


def final_loss(x, tgt, g, sh, sc, y_prev, gate_prev, *, B, S, name):
    D = x.shape[1]

    def fn(x, tgt, y_prev, sh, sc, gate, g):
        xn, _ = _rms(x)
        y = (xn * g) * (1.0 + sc) + sh
        err = y - tgt
        loss = 0.5 * jnp.sum(err * err, axis=0, keepdims=True) * (1.0 / D)
        dx, dsh, dsc, dg = _norm_mod_bwd_math(err * (1.0 / D), x, sc, g)
        return dx, gate * dx, loss, dsh, dsc, dg, jnp.sum(dx * y_prev, axis=0, keepdims=True)

    return rowwise(fn, [x, tgt, y_prev], [sh, sc, gate_prev], [g.reshape(1, -1)], [(D, F32), (D, BF16)],
                   [D, D, D, D, D], B=B, S=S, name=name)


def _gelu(y):
    c0 = math.sqrt(2.0 / math.pi)
    t = jnp.tanh(c0 * (y + 0.044715 * (y * y * y)))
    return 0.5 * y * (1.0 + t), t


def _sigmoid(s):
    return 1.0 / (1.0 + jnp.exp(-s))


def gelu_fwd(y, *, B, S, name):
    return rowwise(lambda y: _gelu(y)[0], [y], [], [], [(y.shape[1], BF16)], [], B=B, S=S, name=name)[0]


def glu_fwd(y, s, *, B, S, name):
    return rowwise(lambda y, s: _gelu(y)[0] * _sigmoid(s), [y, s], [], [], [(y.shape[1], BF16)], [], B=B, S=S,
                   name=name)[0]


def glu_bwd1(y, s, dg, *, B, S, name):
    D = y.shape[1]

    def fn(y, s, dg):
        z = _gelu(y)[0]
        sig = _sigmoid(s)
        ds = dg * z * sig * (1.0 - sig)
        return ds, dg * sig, jnp.sum(ds, axis=0, keepdims=True)

    return rowwise(fn, [y, s, dg], [], [], [(D, BF16), (D, F32)], [D], B=B, S=S, name=name)


def glu_bwd2(y, dz1, dz2, *, B, S, name):
    D = y.shape[1]
    c0 = math.sqrt(2.0 / math.pi)

    def fn(y, dz1, dz2):
        _, t = _gelu(y)
        dgelu = 0.5 * (1.0 + t) + 0.5 * y * (1.0 - t * t) * c0 * (1.0 + 3.0 * 0.044715 * y * y)
        return (dz1 + dz2) * dgelu

    return rowwise(fn, [y, dz1, dz2], [], [], [(D, F32)], [], B=B, S=S, name=name)[0]


def silu_rows(c, *, name):
    R, W = c.shape
    return rowwise(lambda c: c * _sigmoid(c), [c], [], [], [(W, F32)], [], B=1, S=R, name=name)[0]


def conv_gate_fwd(up, cw, cb, *, B, S, name):
    F = up.shape[1] // 2
    tn = _tile(F, (256, 128))
    nF = F // tn

    def body(g_ref, v_ref, wg_ref, wv_ref, bg_ref, bv_ref, o_ref):
        rows = lax.broadcasted_iota(jnp.int32, (S, tn), 0)

        def conv(x, w_ref):
            x1 = jnp.where(rows >= 1, pltpu.roll(x, 1, 0), 0.0)
            x2 = jnp.where(rows >= 2, pltpu.roll(x, 2, 0), 0.0)
            return w_ref[2:3, :] * x + w_ref[1:2, :] * x1 + w_ref[0:1, :] * x2

        gc = conv(g_ref[...].astype(F32), wg_ref) + bg_ref[...]
        vc = conv(v_ref[...].astype(F32), wv_ref) + bv_ref[...]
        o_ref[...] = (gc * _sigmoid(gc) * vc).astype(o_ref.dtype)

    def cols(off):
        return pl.BlockSpec((S, tn), lambda b, j: (b, j + off))

    def vec(rows, off):
        return pl.BlockSpec((rows, tn), lambda b, j: (0, j + off))

    return pl.pallas_call(
        body, name=name, out_shape=jax.ShapeDtypeStruct((B * S, F), BF16), grid=(B, nF),
        in_specs=[cols(0), cols(nF), vec(3, 0), vec(3, nF), vec(1, 0), vec(1, nF)],
        out_specs=pl.BlockSpec((S, tn), lambda b, j: (b, j)),
        compiler_params=_params(("parallel", "parallel")),
    )(up, up, cw, cw, cb, cb)


def conv_gate_bwd(up, dact, cw, cb, *, B, S, name):
    F = up.shape[1] // 2
    tn = _tile(F, (256, 128))
    nF = F // tn

    def body(g_ref, v_ref, da_ref, wg_ref, wv_ref, bg_ref, bv_ref, o_ref, dw_ref, db_ref):
        rows = lax.broadcasted_iota(jnp.int32, (S, tn), 0)

        def earlier(x, k):
            return jnp.where(rows >= k, pltpu.roll(x, k, 0), 0.0)

        def later(x, k):
            return jnp.where(rows < S - k, pltpu.roll(x, S - k, 0), 0.0)

        def conv(x, w_ref):
            x1, x2 = earlier(x, 1), earlier(x, 2)
            return w_ref[2:3, :] * x + w_ref[1:2, :] * x1 + w_ref[0:1, :] * x2, x1, x2

        def back(d, x, x1, x2, w_ref, half):
            o_ref[half] = (w_ref[2:3, :] * d + w_ref[1:2, :] * later(d, 1) + w_ref[0:1, :] * later(d, 2)
                           ).astype(o_ref.dtype)
            dw_ref[half] = jnp.concatenate([jnp.sum(d * x2, axis=0, keepdims=True),
                                            jnp.sum(d * x1, axis=0, keepdims=True),
                                            jnp.sum(d * x, axis=0, keepdims=True)], axis=0)
            return jnp.sum(d, axis=0, keepdims=True)

        g, v, da = g_ref[...].astype(F32), v_ref[...].astype(F32), da_ref[...].astype(F32)
        gc, g1, g2 = conv(g, wg_ref)
        vc, v1, v2 = conv(v, wv_ref)
        gc = gc + bg_ref[...]
        vc = vc + bv_ref[...]
        sig = _sigmoid(gc)
        dg = da * vc * (sig * (1.0 + gc * (1.0 - sig)))
        dv = da * (gc * sig)
        db_ref[...] = jnp.concatenate([back(dg, g, g1, g2, wg_ref, 0), back(dv, v, v1, v2, wv_ref, 1)], axis=0)

    def cols(off):
        return pl.BlockSpec((S, tn), lambda b, j: (b, j + off))

    def vec(rows, off):
        return pl.BlockSpec((rows, tn), lambda b, j: (0, j + off))

    return pl.pallas_call(
        body, name=name,
        out_shape=[jax.ShapeDtypeStruct((2, B * S, F), BF16), jax.ShapeDtypeStruct((B, 2, 3, F), F32),
                   jax.ShapeDtypeStruct((B, 2, F), F32)],
        grid=(B, nF),
        in_specs=[cols(0), cols(nF), cols(0), vec(3, 0), vec(3, nF), vec(1, 0), vec(1, nF)],
        out_specs=[pl.BlockSpec((2, S, tn), lambda b, j: (0, b, j)),
                   pl.BlockSpec((None, 2, 3, tn), lambda b, j: (b, 0, 0, j)),
                   pl.BlockSpec((None, 2, tn), lambda b, j: (b, 0, j))],
        compiler_params=_params(("parallel", "parallel")),
    )(up, up, dact, cw, cw, cb, cb)


MASKED_LOG = -1e30


def _split2(x):
    bits = lax.bitcast_convert_type(x, jnp.uint32) & jnp.uint32(0xFFFF0000)
    hi = lax.bitcast_convert_type(bits, F32)
    return hi.astype(BF16), (x - hi).astype(BF16)


def _nt(a, b):
    return lax.dot_general(a, b, (((1,), (1,)), ((), ())), preferred_element_type=F32)


def _tn(a, b):
    return lax.dot_general(a, b, (((0,), (0,)), ((), ())), preferred_element_type=F32)


def _att_scores(q, k, mask, prescaled=False):
    z = _nt(q, k)
    if not prescaled:
        z = z * (HEAD_DIM ** -0.5)
    e = jnp.exp(-jnp.abs(z))
    sp = jnp.log(1.0 + e)
    lb = jnp.minimum(z, 0.0) - sp
    l1 = lb - z
    if mask is not None:
        lb = jnp.where(mask, lb, MASKED_LOG)
        l1 = jnp.where(mask, l1, 0.0)
    return z, lb, l1, e


def _col_to_row(col, eye):
    return jnp.sum(jnp.where(eye, col, 0.0), axis=0, keepdims=True)


def _row_to_col(row, eye):
    return jnp.sum(jnp.where(eye, row, 0.0), axis=1, keepdims=True)


def _wide_consts(T, W):
    r = lax.broadcasted_iota(jnp.int32, (W, W), 0)
    c = lax.broadcasted_iota(jnp.int32, (W, W), 1)
    two = lambda m: jnp.concatenate([m.astype(BF16)] * 2, axis=0)
    qrow = lax.broadcasted_iota(jnp.int32, (T, W), 0)
    kcol = lax.broadcasted_iota(jnp.int32, (T, W), 1)
    er = lax.broadcasted_iota(jnp.int32, (T, T), 0)
    ec = lax.broadcasted_iota(jnp.int32, (T, T), 1)
    return two(r > c), two(r <= c), two(r < c), qrow, kcol, er == ec


def _pair_masks(x, first):
    zero = jnp.zeros_like(x)
    return jnp.where(first, x, zero), jnp.where(first, zero, x)


def attn_fwd_pairs(qkv, shards=(), slots=(), *, B, S, name):
    D = qkv.shape[1] // 3
    H = D // HEAD_DIM
    T = ATT_BLOCK
    W = 2 * T
    nq = S // T
    G = _tile(H, (ATT_HEADS, 2))
    P = G // 2
    LW = 2 * HEAD_DIM * P
    nsec = D // LW
    n = len(shards)
    n_steps = B * nsec

    def body(*refs):
        q_ref, k_ref, v_ref = refs[:3]
        o_ref, l_ref = refs[3 + 2 * n:5 + 2 * n]
        step_id = pl.program_id(0) * nsec + pl.program_id(1)
        if n:
            gather = _ShardGather(refs[3:3 + n], refs[5 + 2 * n:5 + 3 * n], *refs[5 + 3 * n:])
            pl.when(step_id == 0)(gather.send)
            pl.when(step_id == n_steps - 1)(gather.forward)
        later2, _, _, qrow, kcol, eye = _wide_consts(T, W)
        blk = lax.broadcasted_iota(jnp.int32, (nq, T), 0)
        first_q = lax.broadcasted_iota(jnp.int32, (T, 2 * HEAD_DIM), 1) < HEAD_DIM
        first_k = lax.broadcasted_iota(jnp.int32, (W, 2 * HEAD_DIM), 1) < HEAD_DIM

        def lanes(j):
            return slice(j * 2 * HEAD_DIM, (j + 1) * 2 * HEAD_DIM)

        def step(qms, k0, st, mask):
            accs, runs = st
            parts, lbs, sums = [], [], []
            for g in range(G):
                _, lb, l1, _ = _att_scores(qms[g], k_ref[pl.ds(k0, W), lanes(g // 2)], mask, prescaled=True)
                parts.append(jnp.concatenate(_split2(l1), axis=1))
                lbs.append(lb)
                sums.append(jnp.sum(l1, axis=1, keepdims=True))
            suf = jnp.dot(jnp.concatenate(parts, axis=0), later2, preferred_element_type=F32)
            new_accs, new_runs = [], []
            for j in range(P):
                vms = _pair_masks(v_ref[pl.ds(k0, W), lanes(j)], first_k)
                acc = accs[j]
                for h in range(2):
                    g = 2 * j + h
                    w = jnp.exp(lbs[g] + suf[g * T:(g + 1) * T] + runs[g])
                    acc = acc + jnp.dot(w.astype(BF16), vms[h], preferred_element_type=F32)
                    new_runs.append(runs[g] + sums[g])
                new_accs.append(acc)
            return tuple(new_accs), tuple(new_runs)

        def qblock(i, totals):
            q0 = pl.multiple_of(i * T, T)
            qms = []
            for j in range(P):
                qms.extend(_pair_masks(q_ref[pl.ds(q0, T), lanes(j)] * (HEAD_DIM ** -0.5), first_q))
            half = jnp.right_shift(i, 1)
            last = half * W
            k_last = pl.multiple_of(last, W)
            mask = (k_last + kcol) < (q0 + qrow)
            st = (tuple(jnp.zeros((T, 2 * HEAD_DIM), F32) for _ in range(P)),
                  tuple(jnp.zeros((T, 1), F32) for _ in range(G)))
            st = step(qms, k_last, st, mask)

            def kblock(jj, st):
                return step(qms, pl.multiple_of(last - jj * W, W), st, None)

            accs, runs = lax.fori_loop(1, half + 1, kblock, st)
            for j in range(P):
                o_ref[pl.ds(q0, T), lanes(j)] = accs[j].astype(o_ref.dtype)
            return tuple(jnp.where(blk == i, _col_to_row(runs[g], eye), totals[g]) for g in range(G))

        totals = lax.fori_loop(0, nq, qblock, tuple(jnp.zeros((nq, T), F32) for _ in range(G)))
        for g in range(G):
            l_ref[g] = totals[g]
        if n:
            pl.when(step_id == n_steps - 1)(gather.finish)

    def cols(section):
        return pl.BlockSpec((S, LW), lambda b, h: (b, section * nsec + h))

    lspec = pl.BlockSpec((None, G, nq, T), lambda b, h: (b, h, 0, 0))
    dma = pltpu.SemaphoreType.DMA
    return pl.pallas_call(
        body, name=name,
        out_shape=[jax.ShapeDtypeStruct((B * S, D), BF16), jax.ShapeDtypeStruct((B, H, nq, T), F32)]
        + [jax.ShapeDtypeStruct(s.shape, s.dtype) for s in slots],
        grid=(B, nsec), in_specs=[cols(0), cols(1), cols(2)] + _any_specs(2 * n),
        out_specs=[cols(0), lspec] + _any_specs(n),
        input_output_aliases={3 + n + i: 2 + i for i in range(n)},
        scratch_shapes=[dma((3 * n,))] * 4 if n else [],
        compiler_params=_params(("arbitrary", "arbitrary")),
    )(qkv, qkv, qkv, *shards, *slots)


def attn_bwd_pairs(qkv, ltot, do, partials=(), *, B, S, name):
    D = qkv.shape[1] // 3
    H = D // HEAD_DIM
    T = ATT_BLOCK
    W = 2 * T
    nq = S // T
    scale = HEAD_DIM ** -0.5
    G = _tile(H, (ATT_HEADS_BWD, 2))
    P = G // 2
    LW = 2 * HEAD_DIM * P
    nsec = D // LW
    n = len(partials)
    n_steps = B * nsec

    def body(*refs):
        q_ref, k_ref, v_ref, l_ref, do_ref = refs[:5]
        d_ref = refs[5 + n]
        dk_acc, dv_acc = refs[6 + 2 * n:8 + 2 * n]
        step_id = pl.program_id(0) * nsec + pl.program_id(1)
        if n:
            scatter = _ChipScatter(refs[5:5 + n], refs[6 + n:6 + 2 * n], *refs[8 + 2 * n:])
            pl.when(step_id == 0)(scatter.send)
        _, upto2, before2, qrow, kcol, eye = _wide_consts(T, W)
        blk = lax.broadcasted_iota(jnp.int32, (nq, T), 0)
        first_q = lax.broadcasted_iota(jnp.int32, (T, 2 * HEAD_DIM), 1) < HEAD_DIM
        first_k = lax.broadcasted_iota(jnp.int32, (W, 2 * HEAD_DIM), 1) < HEAD_DIM
        dk_acc[...] = jnp.zeros_like(dk_acc)
        dv_acc[...] = jnp.zeros_like(dv_acc)

        def lanes(j):
            return slice(j * 2 * HEAD_DIM, (j + 1) * 2 * HEAD_DIM)

        def step(qms, doms, tots, k0, st, mask):
            dqs, runs_l, runs_d = st
            sc = []
            for g in range(G):
                z, lb, l1, _ = _att_scores(qms[g], k_ref[pl.ds(k0, W), lanes(g // 2)], mask, prescaled=True)
                beta = 0.5 * jnp.tanh(0.5 * z) + 0.5
                omb = 1.0 - beta
                if mask is not None:
                    beta = jnp.where(mask, beta, 0.0)
                dw = _nt(doms[g], v_ref[pl.ds(k0, W), lanes(g // 2)])
                sc.append((lb, jnp.concatenate(_split2(l1), axis=1), jnp.sum(l1, axis=1, keepdims=True), dw, beta, omb))
            pre = jnp.dot(jnp.concatenate([s[1] for s in sc], axis=0), upto2, preferred_element_type=F32)
            dlws = []
            for j in range(P):
                dv = None
                for h in range(2):
                    g = 2 * j + h
                    w = jnp.exp(sc[g][0] + (tots[g] - (pre[g * T:(g + 1) * T] + runs_l[g])))
                    t = _tn(w.astype(BF16), doms[g])
                    dv = t if dv is None else dv + t
                    dlws.append(sc[g][3] * w)
                dv_acc[j, pl.ds(k0, W), :] += dv
            pre_d = jnp.dot(jnp.concatenate([jnp.concatenate(_split2(d), axis=1) for d in dlws], axis=0), before2,
                            preferred_element_type=F32)
            new_dqs, new_l, new_d = [], [], []
            for j in range(P):
                kms = _pair_masks(k_ref[pl.ds(k0, W), lanes(j)], first_k)
                dq, dk = dqs[j], None
                for h in range(2):
                    g = 2 * j + h
                    _, _, rowsum, _, beta, omb = sc[g]
                    dl1 = pre_d[g * T:(g + 1) * T] + runs_d[g]
                    dz = (dlws[g] * omb - dl1 * beta).astype(BF16)
                    dq = dq + jnp.dot(dz, kms[h], preferred_element_type=F32)
                    t = _tn(dz, qms[g])
                    dk = t if dk is None else dk + t
                    new_l.append(runs_l[g] + rowsum)
                    new_d.append(runs_d[g] + jnp.sum(dlws[g], axis=1, keepdims=True))
                dk_acc[j, pl.ds(k0, W), :] += dk
                new_dqs.append(dq)
            return tuple(new_dqs), tuple(new_l), tuple(new_d)

        def qblock(i, carry0):
            q0 = pl.multiple_of(i * T, T)
            qms, doms = [], []
            for j in range(P):
                qms.extend(_pair_masks(q_ref[pl.ds(q0, T), lanes(j)] * scale, first_q))
                doms.extend(_pair_masks(do_ref[pl.ds(q0, T), lanes(j)], first_q))
            tots = [_row_to_col(jnp.sum(jnp.where(blk == i, l_ref[g], 0.0), axis=0, keepdims=True), eye)
                    for g in range(G)]
            z1 = tuple(jnp.zeros((T, 1), F32) for _ in range(G))
            st = (tuple(jnp.zeros((T, 2 * HEAD_DIM), F32) for _ in range(P)), z1, z1)

            def kblock(j, st):
                return step(qms, doms, tots, pl.multiple_of(j * W, W), st, None)

            half = jnp.right_shift(i, 1)
            st = lax.fori_loop(0, half, kblock, st)
            k_last = pl.multiple_of(half * W, W)
            dqs, _, _ = step(qms, doms, tots, k_last, st, (k_last + kcol) < (q0 + qrow))
            for j in range(P):
                d_ref[0, pl.ds(q0, T), lanes(j)] = (dqs[j] * scale).astype(d_ref.dtype)
            return carry0

        lax.fori_loop(0, nq, qblock, 0)
        for j in range(P):
            d_ref[1, :, lanes(j)] = dk_acc[j].astype(d_ref.dtype)
            d_ref[2, :, lanes(j)] = dv_acc[j].astype(d_ref.dtype)
        if n:
            pl.when(step_id == n_steps - 1)(scatter.finish)

    def cols(section):
        return pl.BlockSpec((S, LW), lambda b, h: (b, section * nsec + h))

    lspec = pl.BlockSpec((None, G, nq, T), lambda b, h: (b, h, 0, 0))
    dma = pltpu.SemaphoreType.DMA
    return pl.pallas_call(
        body, name=name,
        out_shape=[jax.ShapeDtypeStruct((3, B * S, D), BF16)]
        + [jax.ShapeDtypeStruct((N_CHIPS - 1,) + a.shape[1:], a.dtype) for a in partials],
        grid=(B, nsec),
        in_specs=[cols(0), cols(1), cols(2), lspec, cols(0)] + _any_specs(n),
        out_specs=[pl.BlockSpec((3, S, LW), lambda b, h: (0, b, h))] + _any_specs(n),
        scratch_shapes=[pltpu.VMEM((P, S, 2 * HEAD_DIM), F32), pltpu.VMEM((P, S, 2 * HEAD_DIM), F32)]
        + ([dma((3 * n,)), dma((3 * n,))] if n else []),
        compiler_params=_params(("arbitrary", "arbitrary")),
    )(qkv, qkv, qkv, ltot, do, *partials)


def _cmul(ar, ai, br, bi):
    return ar * br - ai * bi, ar * bi + ai * br


def _cpow(lr, li, n):
    rr, ri = None, None
    br, bi = lr, li
    while n:
        if n & 1:
            rr, ri = (br, bi) if rr is None else _cmul(rr, ri, br, bi)
        n >>= 1
        if n:
            br, bi = _cmul(br, bi, br, bi)
    return rr, ri


def _ssm_scan(sr, si, lr, li, n_steps, reverse):
    W = sr.shape[1]
    R = SEGMENTS
    lim = -li if reverse else li
    zero = jnp.zeros((R, W), F32)

    def row(k):
        i = (n_steps - 1 - k) if reverse else k
        return pl.multiple_of(i * R, R)

    def local(k, st):
        cr, ci = st
        r0 = row(k)
        pr, pi = _cmul(lr, lim, cr, ci)
        nr = pr + sr[pl.ds(r0, R), :]
        ni = pi + si[pl.ds(r0, R), :]
        sr[pl.ds(r0, R), :] = nr
        si[pl.ds(r0, R), :] = ni
        return nr, ni

    er, ei = lax.fori_loop(0, n_steps, local, (zero, zero), unroll=SCAN_UNROLL)
    lnr, lni = _cpow(lr, lim, n_steps)
    rows = lax.broadcasted_iota(jnp.int32, (R, W), 0)
    cr, ci = zero, zero
    for step in range(1, R):
        tr, ti = _cmul(lnr, lni, cr, ci)
        tr, ti = tr + er, ti + ei
        if reverse:
            seg = R - 1 - step
            tr, ti = pltpu.roll(tr, R - 1, 0), pltpu.roll(ti, R - 1, 0)
        else:
            seg = step
            tr, ti = pltpu.roll(tr, 1, 0), pltpu.roll(ti, 1, 0)
        cr = jnp.where(rows == seg, tr, cr)
        ci = jnp.where(rows == seg, ti, ci)

    def fix(k, st):
        pr, pi = st
        r0 = row(k)
        ar, ai = _cmul(pr, pi, cr, ci)
        sr[pl.ds(r0, R), :] += ar
        si[pl.ds(r0, R), :] += ai
        return _cmul(lr, lim, pr, pi)

    lax.fori_loop(0, n_steps, fix, (lr, lim), unroll=SCAN_UNROLL)
    return cr, ci


def _ssm_specs(S, W):
    CH = GROUPS_PER_BLOCK * SSM_GROUP
    return dict(
        rows=pl.BlockSpec((S, CH), lambda b, j: (b, j)),
        b=pl.BlockSpec((None, CH, W), lambda b, j: (j, 0, 0)),
        c=pl.BlockSpec((None, W, CH), lambda b, j: (j, 0, 0)),
        lam=pl.BlockSpec((None, SEGMENTS, W), lambda b, j: (j, 0, 0)),
        vec=pl.BlockSpec((1, CH), lambda b, j: (0, j)),
    )


def ssm_fwd(u, bre, bim, cre, cim, lr8, li8, dsk, *, B, S, name):
    D = u.shape[1]
    J, CH, W = bre.shape
    n_steps = S // SEGMENTS
    sp = _ssm_specs(S, W)

    def body(u_ref, bre_ref, bim_ref, cre_ref, cim_ref, lr_ref, li_ref, dsk_ref, y_ref, sr, si):
        u = u_ref[...]
        ub = u.astype(BF16)
        sr[...] = jnp.dot(ub, bre_ref[...], preferred_element_type=F32)
        si[...] = jnp.dot(ub, bim_ref[...], preferred_element_type=F32)
        _ssm_scan(sr, si, lr_ref[...], li_ref[...], n_steps, False)
        y = jnp.dot(sr[...].astype(BF16), cre_ref[...], preferred_element_type=F32)
        y = y - jnp.dot(si[...].astype(BF16), cim_ref[...], preferred_element_type=F32)
        y_ref[...] = y + dsk_ref[...] * u

    return pl.pallas_call(
        body, name=name, out_shape=jax.ShapeDtypeStruct((B * S, D), F32), grid=(B, J),
        in_specs=[sp["rows"], sp["b"], sp["b"], sp["c"], sp["c"], sp["lam"], sp["lam"], sp["vec"]],
        out_specs=sp["rows"],
        scratch_shapes=[pltpu.VMEM((S, W), F32), pltpu.VMEM((S, W), F32)],
        compiler_params=_params(("parallel", "parallel")),
    )(u, bre, bim, cre, cim, lr8, li8, dsk)


def ssm_bwd(u, dy, bre, bim, cre, cim, lr8, li8, dsk, *, B, S, name):
    D = u.shape[1]
    J, CH, W = bre.shape
    n_steps = S // SEGMENTS
    sp = _ssm_specs(S, W)

    def body(u_ref, dy_ref, bre_ref, bim_ref, cre_ref, cim_ref, lr_ref, li_ref, dsk_ref,
             du_ref, dbre_ref, dbim_ref, dcre_ref, dcim_ref, dlr_ref, dli_ref, ddsk_ref, sr, si, ar, ai):
        u = u_ref[...]
        dy = dy_ref[...]
        ub = u.astype(BF16)
        dyb = dy.astype(BF16)
        lr, li = lr_ref[...], li_ref[...]
        sr[...] = jnp.dot(ub, bre_ref[...], preferred_element_type=F32)
        si[...] = jnp.dot(ub, bim_ref[...], preferred_element_type=F32)
        cr, ci = _ssm_scan(sr, si, lr, li, n_steps, False)
        ar[...] = _nt(dyb, cre_ref[...])
        ai[...] = -_nt(dyb, cim_ref[...])
        _ssm_scan(ar, ai, lr, li, n_steps, True)

        def dlam(k, st):
            dr, di = st
            r0 = pl.multiple_of((k + 1) * SEGMENTS, SEGMENTS)
            p0 = pl.multiple_of(k * SEGMENTS, SEGMENTS)
            pr, pi = sr[pl.ds(p0, SEGMENTS), :], si[pl.ds(p0, SEGMENTS), :]
            xr, xi = ar[pl.ds(r0, SEGMENTS), :], ai[pl.ds(r0, SEGMENTS), :]
            return dr + pr * xr + pi * xi, di + pr * xi - pi * xr

        xr, xi = ar[0:SEGMENTS, :], ai[0:SEGMENTS, :]
        dr, di = lax.fori_loop(0, n_steps - 1, dlam, (cr * xr + ci * xi, cr * xi - ci * xr), unroll=SCAN_UNROLL)
        dlr_ref[...] = dr
        dli_ref[...] = di
        arb = ar[...].astype(BF16)
        aib = ai[...].astype(BF16)
        du_ref[...] = _nt(arb, bre_ref[...]) + _nt(aib, bim_ref[...]) + dsk_ref[...] * dy
        dbre_ref[...] = _tn(ub, arb)
        dbim_ref[...] = _tn(ub, aib)
        dcre_ref[...] = _tn(sr[...].astype(BF16), dyb)
        dcim_ref[...] = -_tn(si[...].astype(BF16), dyb)
        ddsk_ref[...] = jnp.sum(dy * u, axis=0, keepdims=True)

    def per(shape):
        return pl.BlockSpec((None, None) + shape, lambda b, j: (b, j, 0, 0))

    return pl.pallas_call(
        body, name=name,
        out_shape=[jax.ShapeDtypeStruct((B * S, D), F32),
                   jax.ShapeDtypeStruct((B, J, CH, W), F32), jax.ShapeDtypeStruct((B, J, CH, W), F32),
                   jax.ShapeDtypeStruct((B, J, W, CH), F32), jax.ShapeDtypeStruct((B, J, W, CH), F32),
                   jax.ShapeDtypeStruct((B, J, SEGMENTS, W), F32), jax.ShapeDtypeStruct((B, J, SEGMENTS, W), F32),
                   jax.ShapeDtypeStruct((B, J, 1, CH), F32)],
        grid=(B, J),
        in_specs=[sp["rows"], sp["rows"], sp["b"], sp["b"], sp["c"], sp["c"], sp["lam"], sp["lam"], sp["vec"]],
        out_specs=[sp["rows"], per((CH, W)), per((CH, W)), per((W, CH)), per((W, CH)), per((SEGMENTS, W)),
                   per((SEGMENTS, W)),
                   per((1, CH))],
        scratch_shapes=[pltpu.VMEM((S, W), F32)] * 4,
        compiler_params=_params(("parallel", "parallel")),
    )(u, dy, bre, bim, cre, cim, lr8, li8, dsk)


def _ssm_discretize(a_re, a_im, log_dt, b_re, b_im):
    dt = jnp.exp(log_dt)[:, None]
    er = jnp.exp(a_re * dt)
    lr = er * jnp.cos(a_im * dt)
    li = er * jnp.sin(a_im * dt)
    den = a_re * a_re + a_im * a_im
    fr = ((lr - 1.0) * a_re + li * a_im) / den
    fi = (li * a_re - (lr - 1.0) * a_im) / den
    bbr = fr[..., None] * b_re - fi[..., None] * b_im
    bbi = fr[..., None] * b_im + fi[..., None] * b_re
    return lr, li, bbr, bbi


def _block_diag_in(m):
    G, P, H = m.shape
    J = G // GROUPS_PER_BLOCK
    m = m.reshape(J, GROUPS_PER_BLOCK, P, H).transpose(0, 1, 3, 2)
    eye = jnp.eye(GROUPS_PER_BLOCK, dtype=m.dtype)
    out = m[:, :, :, None, :] * eye[None, :, None, :, None]
    return out.reshape(J, GROUPS_PER_BLOCK * H, GROUPS_PER_BLOCK * P)


def _block_diag_in_grad(d, G, P, H):
    J = G // GROUPS_PER_BLOCK
    d = d.reshape(J, GROUPS_PER_BLOCK, H, GROUPS_PER_BLOCK, P)
    idx = jnp.arange(GROUPS_PER_BLOCK)
    d = d[:, idx, :, idx, :]
    return d.transpose(1, 0, 3, 2).reshape(G, P, H)


def _block_diag_out(m):
    G, H, P = m.shape
    J = G // GROUPS_PER_BLOCK
    m = m.reshape(J, GROUPS_PER_BLOCK, H, P).transpose(0, 1, 3, 2)
    eye = jnp.eye(GROUPS_PER_BLOCK, dtype=m.dtype)
    out = m[:, :, :, None, :] * eye[None, :, None, :, None]
    return out.reshape(J, GROUPS_PER_BLOCK * P, GROUPS_PER_BLOCK * H)


def _block_diag_out_grad(d, G, H, P):
    J = G // GROUPS_PER_BLOCK
    d = d.reshape(J, GROUPS_PER_BLOCK, P, GROUPS_PER_BLOCK, H)
    idx = jnp.arange(GROUPS_PER_BLOCK)
    d = d[:, idx, :, idx, :]
    return d.transpose(1, 0, 3, 2).reshape(G, H, P)


def _interleave(a, B, S):
    L = S // SEGMENTS
    return a.reshape(B, SEGMENTS, L, a.shape[-1]).transpose(0, 2, 1, 3).reshape(B * S, a.shape[-1])


def _deinterleave(a, B, S):
    L = S // SEGMENTS
    return a.reshape(B, L, SEGMENTS, a.shape[-1]).transpose(0, 2, 1, 3).reshape(B * S, a.shape[-1])


def _adamw_math(w, g, m, v):
    m = ADAM_B1 * m + (1.0 - ADAM_B1) * g
    v = ADAM_B2 * v + (1.0 - ADAM_B2) * (g * g)
    m_hat = m / (1.0 - ADAM_B1 ** ADAM_STEP)
    v_hat = v / (1.0 - ADAM_B2 ** ADAM_STEP)
    delta = -ADAM_LR * (m_hat / (jnp.sqrt(v_hat) + ADAM_EPS) + ADAM_WD * w)
    return delta, m, v


def adamw(w, g, m, v, *, name):
    R, C = w.shape
    tr = _tile(R, (max(8, (1 << 18) // C // 8 * 8), 256, 128, 64, 32, 16, 8))

    def body(w_ref, g_ref, m_ref, v_ref, d_ref, nm_ref, nv_ref):
        d, nm, nv = _adamw_math(w_ref[...], g_ref[...], m_ref[...], v_ref[...])
        d_ref[...] = d
        nm_ref[...] = nm
        nv_ref[...] = nv

    spec = pl.BlockSpec((tr, C), lambda i: (i, 0))
    shp = jax.ShapeDtypeStruct((R, C), F32)
    return pl.pallas_call(
        body, name=name, out_shape=[shp, shp, shp], grid=(R // tr,), in_specs=[spec] * 4, out_specs=[spec] * 3,
        compiler_params=_params(("parallel",)),
    )(w, g, m, v)


def adamw_many(ws, gs, ms, vs, *, name):
    n = len(ws)
    at_least_2d = lambda a: a.reshape(1, -1) if a.ndim == 1 else a
    args = [at_least_2d(a) for group in (ws, gs, ms, vs) for a in group]

    def body(*refs):
        for i in range(n):
            d, nm, nv = _adamw_math(refs[i][...], refs[n + i][...], refs[2 * n + i][...], refs[3 * n + i][...])
            refs[4 * n + 3 * i][...] = d
            refs[4 * n + 3 * i + 1][...] = nm
            refs[4 * n + 3 * i + 2][...] = nv

    out = pl.pallas_call(
        body, name=name, out_shape=[jax.ShapeDtypeStruct(a.shape, F32) for a in args[:n] for _ in range(3)],
        in_specs=[pl.BlockSpec(memory_space=pltpu.VMEM) for _ in args],
        out_specs=[pl.BlockSpec(memory_space=pltpu.VMEM) for _ in range(3 * n)],
        compiler_params=pltpu.CompilerParams(vmem_limit_bytes=V7X_VMEM_LIMIT),
    )(*args)
    return [tuple(o.reshape(w.shape) for o in out[3 * i:3 * i + 3]) for i, w in enumerate(ws)]


def _any_specs(n):
    return [pl.BlockSpec(memory_space=pl.ANY) for _ in range(n)]


def _coords():
    return lax.axis_index("x"), lax.axis_index("y"), lax.axis_index("c")


def _flip(v, bit):
    return (v + bit) % 2


def all_gather8(a, *, name):
    shape = a.shape

    def body(a_ref, o_ref, send_sems, recv_sems, local_sem, buf):
        x, y, c = _coords()
        me = 4 * x + 2 * y + c
        load = pltpu.make_async_copy(a_ref, buf, local_sem)
        load.start()
        sends = []
        for k in range(1, N_DEV):
            peer = (_flip(x, (k >> 2) & 1), _flip(y, (k >> 1) & 1), _flip(c, k & 1))
            cp = pltpu.make_async_remote_copy(a_ref, o_ref.at[me], send_sems.at[k - 1], recv_sems.at[k - 1],
                                              device_id=peer, device_id_type=MESH)
            cp.start()
            sends.append(cp)
        load.wait()
        mine = pltpu.make_async_copy(buf, o_ref.at[me], local_sem)
        mine.start()
        for k in range(1, N_DEV):
            px, py, pc = _flip(x, (k >> 2) & 1), _flip(y, (k >> 1) & 1), _flip(c, k & 1)
            src = 4 * px + 2 * py + pc
            pltpu.make_async_remote_copy(a_ref, o_ref.at[src], send_sems.at[k - 1], recv_sems.at[k - 1],
                                         device_id=(px, py, pc), device_id_type=MESH).wait_recv()
        for cp in sends:
            cp.wait_send()
        mine.wait()

    return pl.pallas_call(
        body, name=name, out_shape=jax.ShapeDtypeStruct((N_DEV,) + shape, a.dtype),
        in_specs=_any_specs(1), out_specs=pl.BlockSpec(memory_space=pl.ANY),
        scratch_shapes=[pltpu.SemaphoreType.DMA((N_DEV - 1,)), pltpu.SemaphoreType.DMA((N_DEV - 1,)),
                        pltpu.SemaphoreType.DMA(()), pltpu.VMEM(shape, a.dtype)],
    )(a)


def _chip_of(x, y, p):
    px, py = _flip(x, (p >> 1) & 1), _flip(y, p & 1)
    return 2 * px + py, px, py


class _ShardGather:
    def __init__(self, ins, outs, ici_send, ici_recv, d2d_send, d2d_recv):
        self.ins, self.outs = ins, outs
        self.sems = ici_send, ici_recv, d2d_send, d2d_recv
        self.x, self.y, self.c = _coords()
        self.me = 2 * self.x + self.y

    def _ici(self, i, p, slot):
        half = self.ins[i].shape[0] // 2
        rows = pl.ds(self.c * half, half)
        _, px, py = _chip_of(self.x, self.y, p)
        s = i * 3 + p - 1
        return pltpu.make_async_remote_copy(self.ins[i].at[rows], self.outs[i].at[slot, rows], self.sems[0].at[s],
                                            self.sems[1].at[s], device_id=(px, py, self.c), device_id_type=MESH)

    def _d2d(self, i, p, mine):
        half = self.ins[i].shape[0] // 2
        rows = pl.ds((self.c if mine else 1 - self.c) * half, half)
        src, _, _ = _chip_of(self.x, self.y, p)
        s = i * 3 + p - 1
        part = self.outs[i].at[src, rows]
        return pltpu.make_async_remote_copy(part, part, self.sems[2].at[s], self.sems[3].at[s],
                                            device_id=(self.x, self.y, 1 - self.c), device_id_type=MESH)

    def _each(self):
        return [(i, p) for i in range(len(self.ins)) for p in range(1, N_CHIPS)]

    def send(self):
        for i, p in self._each():
            self._ici(i, p, self.me).start()

    def forward(self):
        for i, p in self._each():
            self._ici(i, p, _chip_of(self.x, self.y, p)[0]).wait_recv()
            self._d2d(i, p, True).start()

    def finish(self):
        for i, p in self._each():
            self._d2d(i, p, False).wait_recv()
        for i, p in self._each():
            self._ici(i, p, self.me).wait_send()
            self._d2d(i, p, True).wait_send()


def gather_chip_shards(arrs, remote, everyone=(), *, name):
    n = len(arrs)
    e = len(everyone)
    far = [i for i in range(n) if remote[i]]

    def body(*refs):
        ins, ev_ins, outs, ev_outs = refs[:n], refs[n:n + e], refs[n + e:2 * n + e], refs[2 * n + e:2 * (n + e)]
        ici_send, ici_recv, d2d_send, d2d_recv, local_sems, ev_send, ev_recv = refs[2 * (n + e):2 * (n + e) + 7]
        bufs = refs[2 * (n + e) + 7:2 * (n + e) + 7 + n]
        ev_bufs = refs[2 * (n + e) + 7 + n:]
        x, y, c = _coords()
        dev = 4 * x + 2 * y + c

        def ev_copy(i, k, slot):
            peer = (_flip(x, (k >> 2) & 1), _flip(y, (k >> 1) & 1), _flip(c, k & 1))
            s = i * (N_DEV - 1) + k - 1
            return pltpu.make_async_remote_copy(ev_ins[i], ev_outs[i].at[slot], ev_send.at[s], ev_recv.at[s],
                                                device_id=peer, device_id_type=MESH)

        ev_loads = []
        for i in range(e):
            cp = pltpu.make_async_copy(ev_ins[i], ev_bufs[i], local_sems.at[n + i])
            cp.start()
            ev_loads.append(cp)
            for k in range(1, N_DEV):
                ev_copy(i, k, dev).start()
        me = 2 * x + y
        loads = []
        for i in range(n):
            cp = pltpu.make_async_copy(ins[i], bufs[i], local_sems.at[i])
            cp.start()
            loads.append(cp)
        gather = _ShardGather([ins[i] for i in far], [outs[i] for i in far], ici_send, ici_recv, d2d_send, d2d_recv)
        gather.send()
        stores = []
        for i in range(n):
            loads[i].wait()
            cp = pltpu.make_async_copy(bufs[i], outs[i].at[me], local_sems.at[i])
            cp.start()
            stores.append(cp)
        ev_stores = []
        for i in range(e):
            ev_loads[i].wait()
            cp = pltpu.make_async_copy(ev_bufs[i], ev_outs[i].at[dev], local_sems.at[n + i])
            cp.start()
            ev_stores.append(cp)
        gather.forward()
        gather.finish()
        for i in range(e):
            for k in range(1, N_DEV):
                src = 4 * _flip(x, (k >> 2) & 1) + 2 * _flip(y, (k >> 1) & 1) + _flip(c, k & 1)
                ev_copy(i, k, src).wait_recv()
            for k in range(1, N_DEV):
                ev_copy(i, k, dev).wait_send()
        for cp in stores + ev_stores:
            cp.wait()

    dma = pltpu.SemaphoreType.DMA
    m = 3 * len(far)
    return pl.pallas_call(
        body, name=name,
        out_shape=[jax.ShapeDtypeStruct((N_CHIPS,) + a.shape, a.dtype) for a in arrs]
        + [jax.ShapeDtypeStruct((N_DEV,) + a.shape, a.dtype) for a in everyone],
        in_specs=_any_specs(n + e), out_specs=_any_specs(n + e),
        scratch_shapes=[dma((m,)), dma((m,)), dma((m,)), dma((m,)), dma((n + e,)),
                        dma((max(1, (N_DEV - 1) * e),)), dma((max(1, (N_DEV - 1) * e),))]
        + [pltpu.VMEM(a.shape, a.dtype) for a in arrs] + [pltpu.VMEM(a.shape, a.dtype) for a in everyone],
        compiler_params=pltpu.CompilerParams(vmem_limit_bytes=V7X_VMEM_LIMIT),
    )(*arrs, *everyone)


def swap_halves(arrs, *, name):
    n = len(arrs)

    def body(*refs):
        ins, outs = refs[:n], refs[n:2 * n]
        send_sems, recv_sems = refs[2 * n:]
        x, y, c = _coords()
        cps = []
        for i in range(n):
            half = ins[i].shape[1] // 2
            cp = pltpu.make_async_remote_copy(ins[i].at[:, pl.ds((1 - c) * half, half)], outs[i], send_sems.at[i],
                                              recv_sems.at[i], device_id=(x, y, 1 - c), device_id_type=MESH)
            cp.start()
            cps.append(cp)
        for cp in cps:
            cp.wait()

    dma = pltpu.SemaphoreType.DMA
    return pl.pallas_call(
        body, name=name,
        out_shape=[jax.ShapeDtypeStruct((N_CHIPS, a.shape[1] // 2, a.shape[2]), a.dtype) for a in arrs],
        in_specs=_any_specs(n), out_specs=_any_specs(n), scratch_shapes=[dma((n,)), dma((n,))],
    )(*arrs)


def add_half(g, other, c_idx, *, name, out_dtype):
    _, R, C = g.shape
    half = R // 2
    tr = _row_tile(half)
    nt = half // tr

    def body(c_ref, g_ref, o_ref, out_ref):
        out_ref[...] = (g_ref[...].astype(F32) + o_ref[...].astype(F32)).astype(out_ref.dtype)

    return pl.pallas_call(
        body, name=name, out_shape=jax.ShapeDtypeStruct((N_CHIPS, half, C), out_dtype),
        grid_spec=pltpu.PrefetchScalarGridSpec(
            num_scalar_prefetch=1, grid=(N_CHIPS, nt),
            in_specs=[pl.BlockSpec((None, tr, C), lambda r, t, c_ref: (r, c_ref[0] * nt + t, 0)),
                      pl.BlockSpec((None, tr, C), lambda r, t, c_ref: (r, t, 0))],
            out_specs=pl.BlockSpec((None, tr, C), lambda r, t, c_ref: (r, t, 0))),
        compiler_params=_params(("parallel", "parallel")),
    )(c_idx, g, other)


class _ChipScatter:
    def __init__(self, ins, outs, send_sems, recv_sems):
        self.ins, self.outs, self.send_sems, self.recv_sems = ins, outs, send_sems, recv_sems
        self.x, self.y, self.c = _coords()

    def _copies(self):
        for i in range(len(self.ins)):
            for p in range(1, N_CHIPS):
                dst, px, py = _chip_of(self.x, self.y, p)
                s = i * 3 + p - 1
                yield pltpu.make_async_remote_copy(self.ins[i].at[dst], self.outs[i].at[p - 1], self.send_sems.at[s],
                                                   self.recv_sems.at[s], device_id=(px, py, self.c), device_id_type=MESH)

    def send(self):
        for cp in self._copies():
            cp.start()

    def finish(self):
        for cp in self._copies():
            cp.wait()


def scatter_to_chips(arrs, *, name):
    n = len(arrs)

    def body(*refs):
        scatter = _ChipScatter(refs[:n], refs[n:2 * n], *refs[2 * n:])
        scatter.send()
        scatter.finish()

    dma = pltpu.SemaphoreType.DMA
    return pl.pallas_call(
        body, name=name,
        out_shape=[jax.ShapeDtypeStruct((N_CHIPS - 1,) + a.shape[1:], a.dtype) for a in arrs],
        in_specs=_any_specs(n), out_specs=_any_specs(n), scratch_shapes=[dma((3 * n,)), dma((3 * n,))],
    )(*arrs)


def add_chips(h, got, r_idx, *, name):
    _, R, C = h.shape
    tr = _row_tile(R)

    def body(r_ref, h_ref, g_ref, out_ref):
        acc = h_ref[...].astype(F32)
        for p in range(N_CHIPS - 1):
            acc = acc + g_ref[p].astype(F32)
        out_ref[...] = acc

    return pl.pallas_call(
        body, name=name, out_shape=jax.ShapeDtypeStruct((R, C), F32),
        grid_spec=pltpu.PrefetchScalarGridSpec(
            num_scalar_prefetch=1, grid=(R // tr,),
            in_specs=[pl.BlockSpec((None, tr, C), lambda t, r_ref: (r_ref[0], t, 0)),
                      pl.BlockSpec((N_CHIPS - 1, tr, C), lambda t, r_ref: (0, t, 0))],
            out_specs=pl.BlockSpec((tr, C), lambda t, r_ref: (t, 0))),
        compiler_params=_params(("parallel",)),
    )(r_idx, h, got)


def join_halves(arrs, *, name):
    n = len(arrs)

    def body(*refs):
        ins, outs = refs[:n], refs[n:2 * n]
        send_sems, recv_sems, local_sems = refs[2 * n:2 * n + 3]
        bufs = refs[2 * n + 3:]
        x, y, c = _coords()
        loads, sends, stores = [], [], []
        for i in range(n):
            cp = pltpu.make_async_copy(ins[i], bufs[i], local_sems.at[i])
            cp.start()
            loads.append(cp)
        for i in range(n):
            half = ins[i].shape[0]
            cp = pltpu.make_async_remote_copy(ins[i], outs[i].at[pl.ds(c * half, half)], send_sems.at[i], recv_sems.at[i],
                                              device_id=(x, y, 1 - c), device_id_type=MESH)
            cp.start()
            sends.append(cp)
        for i in range(n):
            half = ins[i].shape[0]
            loads[i].wait()
            cp = pltpu.make_async_copy(bufs[i], outs[i].at[pl.ds(c * half, half)], local_sems.at[i])
            cp.start()
            stores.append(cp)
        for i in range(n):
            half = ins[i].shape[0]
            pltpu.make_async_remote_copy(ins[i], outs[i].at[pl.ds((1 - c) * half, half)], send_sems.at[i],
                                         recv_sems.at[i], device_id=(x, y, 1 - c), device_id_type=MESH).wait_recv()
        for i in range(n):
            sends[i].wait_send()
            stores[i].wait()

    dma = pltpu.SemaphoreType.DMA
    return pl.pallas_call(
        body, name=name,
        out_shape=[jax.ShapeDtypeStruct((2 * a.shape[0], a.shape[1]), a.dtype) for a in arrs],
        in_specs=_any_specs(n), out_specs=_any_specs(n),
        scratch_shapes=[dma((n,)), dma((n,)), dma((n,))] + [pltpu.VMEM(a.shape, a.dtype) for a in arrs],
        compiler_params=pltpu.CompilerParams(vmem_limit_bytes=V7X_VMEM_LIMIT),
    )(*arrs)


def pair_sums(grads, wire_dtypes, tag):
    c_idx = jnp.reshape(lax.axis_index("c"), (1,)).astype(jnp.int32)
    theirs = swap_halves(grads, name=f"rs_swap_halves_{tag}")
    return [add_half(g, o, c_idx, name=f"rs_add_half_{tag}{i}", out_dtype=wire_dtypes[i])
            for i, (g, o) in enumerate(zip(grads, theirs))]


def chip_sums(pairs, gots, tag):
    r_idx = jnp.reshape(2 * lax.axis_index("x") + lax.axis_index("y"), (1,)).astype(jnp.int32)
    return [add_chips(h, g, r_idx, name=f"rs_add_chips_{tag}{i}") for i, (h, g) in enumerate(zip(pairs, gots))]


def _chip_major(w, axis):
    n = w.shape[axis] // N_CHIPS
    parts = w.reshape(w.shape[:axis] + (N_CHIPS, n) + w.shape[axis + 1:])
    return jnp.moveaxis(parts, axis, 0)


def _from_chip_major(g, axis):
    g = jnp.moveaxis(g, 0, axis)
    return g.reshape(g.shape[:axis] + (g.shape[axis] * g.shape[axis + 1],) + g.shape[axis + 2:])


def kernel(x, c, norm_mix, norm_ffn, w_mod, b_mod, w_qkv, w_o_attn, w_in_ssm, a_re, a_im, log_dt, b_re, b_im, c_re, c_im, d_skip, w_glu, b_glu, w_o_ssm, w_up, conv_w, conv_b, w_down, norm_out, w_fin, b_fin, loss_target, m_norm_mix, m_norm_ffn, m_w_mod, m_b_mod, m_w_qkv, m_w_o_attn, m_w_in_ssm, m_a_re, m_a_im, m_log_dt, m_b_re, m_b_im, m_c_re, m_c_im, m_d_skip, m_w_glu, m_b_glu, m_w_o_ssm, m_w_up, m_conv_w, m_conv_b, m_w_down, m_norm_out, m_w_fin, m_b_fin, v_norm_mix, v_norm_ffn, v_w_mod, v_b_mod, v_w_qkv, v_w_o_attn, v_w_in_ssm, v_a_re, v_a_im, v_log_dt, v_b_re, v_b_im, v_c_re, v_c_im, v_d_skip, v_w_glu, v_b_glu, v_w_o_ssm, v_w_up, v_conv_w, v_conv_b, v_w_down, v_norm_out, v_w_fin, v_b_fin):
    B, S, D = x.shape
    T = B * S
    F2 = conv_b.shape[1]
    F = F2 // 2
    G, P = a_re.shape[1], a_re.shape[2]
    H = b_re.shape[3]
    mx, my, mc = _coords()
    chip = 2 * mx + my
    dev = 4 * mx + 2 * my + mc
    BG = N_DEV * B
    mod_w = w_mod.shape[2]
    fin_w = w_fin.shape[1]

    Dq = D // N_CHIPS
    Fq = F // N_CHIPS
    rows1024 = jnp.concatenate([w_o_attn[0], w_in_ssm[0], w_glu[0], w_o_ssm[0], w_down.reshape(-1, D)], axis=0)
    shards = [w_qkv[0].astype(BF16), rows1024.astype(BF16), w_up[0].astype(BF16), w_up[1].astype(BF16)]
    small = jnp.concatenate([conv_w.reshape(6, -1), jnp.pad(d_skip, ((0, 0), (0, conv_w.shape[2] - Dq))),
                             jnp.pad(b_glu, ((0, 0), (0, conv_w.shape[2] - Dq)))], axis=0)
    W_qkv, *gathered = gather_chip_shards(shards, [True, False, False, False], [c, small], name="gather_weights")
    own_slots, c_all, small_all = gathered[:3], gathered[3].reshape(BG, D), gathered[4][::2]

    c_act = silu_rows(c_all, name="silu_c")
    b_mod_mine = lax.dynamic_slice(b_mod, (0, chip * mod_w), (2, mod_w))
    b_fin_mine = lax.dynamic_slice(b_fin, (chip * fin_w,), (fin_w,))
    cond = [matmul(c_act, w_mod[i], bias=b_mod_mine[i], name=f"mod_proj_{i}") for i in range(2)]
    cond.append(matmul(c_act, w_fin, bias=b_fin_mine, name="fin_proj"))
    cond_all = all_gather8(jnp.concatenate(cond, axis=1), name="gather_cond")
    cond_all = cond_all[::2]
    cond_rows = lax.dynamic_slice(cond_all, (0, dev * B, 0), (N_CHIPS, B, cond_all.shape[2]))
    mods = []
    for i in range(2):
        full = cond_rows[:, :, i * mod_w:(i + 1) * mod_w].transpose(1, 0, 2).reshape(B, N_CHIPS * mod_w)
        mods.append([full[:, k * D:(k + 1) * D] for k in range(6)])
    fin = cond_rows[:, :, 2 * mod_w:].transpose(1, 0, 2).reshape(B, N_CHIPS * fin_w)
    sh_f, sc_f = fin[:, :D], fin[:, D:]

    conv_w_full = _from_chip_major(small_all[:, :6].reshape(N_CHIPS, 2, 3, -1), 2)
    d_skip_full = small_all[:, 6, :Dq].reshape(1, D)
    b_glu_full = small_all[:, 7, :Dq].reshape(D)

    x0 = x.reshape(T, D)
    tgt = loss_target.reshape(T, D)

    def ffn_fwd(xprev, y, gate, i):
        sh2, sc2 = mods[i][3], mods[i][4]
        xin, h2 = res_norm_mod_fwd(xprev, y, gate, norm_ffn[i], sh2, sc2, B=B, S=S, name=f"ffn_norm_{i}")
        up = matmul(h2, W_up[i], b_chips=True, out_dtype=BF16, name=f"ffn_up_{i}")
        act = conv_gate_fwd(up, conv_w_full[i], conv_b[i:i + 1], B=B, S=S, name=f"ffn_conv_{i}")
        yf = matmul(act, W_down[i], name=f"ffn_down_{i}")
        return xin, yf, (xin, h2, up, act, yf)

    sh1, sc1, g1 = mods[0][0], mods[0][1], mods[0][2]
    h1a = norm_mod_fwd(x0, norm_mix[0], sh1, sc1, B=B, S=S, name="att_norm")
    qkv = matmul(h1a, W_qkv, out_dtype=BF16, b_chips=True, name="att_qkv")
    o2, ltot, g_rows, W_up0, W_up1 = attn_fwd_pairs(qkv, shards[1:], own_slots, B=B, S=S, name="att_fwd")
    W_up = [W_up0, W_up1]
    W_o_attn = g_rows[:, 0 * Dq:1 * Dq].reshape(D, D)
    W_in = g_rows[:, 1 * Dq:2 * Dq].reshape(D, D)
    W_glu = g_rows[:, 2 * Dq:3 * Dq].reshape(D, D)
    W_o_ssm = g_rows[:, 3 * Dq:4 * Dq].reshape(D, D)
    W_down = [g_rows[:, 4 * Dq + i * Fq:4 * Dq + (i + 1) * Fq].reshape(F, D) for i in range(2)]
    ya = matmul(o2, W_o_attn, name="att_out")
    x1, yf0, ffn0 = ffn_fwd(x0, ya, g1, 0)

    lr, li, bbr, bbi = _ssm_discretize(a_re[0], a_im[0], log_dt[0], b_re[0], b_im[0])
    J = G // GROUPS_PER_BLOCK
    Wst = GROUPS_PER_BLOCK * P
    bre_blk = _block_diag_in(bbr).astype(BF16)
    bim_blk = _block_diag_in(bbi).astype(BF16)
    cre_blk = _block_diag_out(c_re[0]).astype(BF16)
    cim_blk = _block_diag_out(c_im[0]).astype(BF16)
    lr8 = jnp.broadcast_to(lr.reshape(J, 1, Wst), (J, SEGMENTS, Wst))
    li8 = jnp.broadcast_to(li.reshape(J, 1, Wst), (J, SEGMENTS, Wst))
    sh1s, sc1s, g1s = mods[1][0], mods[1][1], mods[1][2]
    x2, h1s = res_norm_mod_fwd(x1, yf0, mods[0][5], norm_mix[1], sh1s, sc1s, B=B, S=S, name="ssm_norm")
    h1p = _interleave(h1s, B, S)
    u = matmul(h1p, W_in, name="ssm_in")
    y_ssm = ssm_fwd(u, bre_blk, bim_blk, cre_blk, cim_blk, lr8, li8, d_skip_full, B=B, S=S, name="ssm_scan_fwd")
    zb = gelu_fwd(y_ssm, B=B, S=S, name="ssm_gelu")
    s_glu = matmul(zb, W_glu, bias=b_glu_full, name="ssm_glu_proj")
    gb = glu_fwd(y_ssm, s_glu, B=B, S=S, name="ssm_glu")
    ys_p = matmul(gb, W_o_ssm, name="ssm_out")
    ys = _deinterleave(ys_p, B, S)
    x3, yf1, ffn1 = ffn_fwd(x2, ys, g1s, 1)
    x4 = gate_res_fwd(x3, yf1, mods[1][5], B=B, S=S, name="ffn_res_1")

    dx4, dyf1, loss_p, dsh_f, dsc_f, dnorm_out, dg2_1 = final_loss(x4, tgt, norm_out, sh_f, sc_f, yf1, mods[1][5],
                                                                   B=B, S=S, name="loss_head")
    loss = lax.psum(jnp.sum(loss_p), ("x", "y", "c"))

    def ffn_bwd(dxo, dyf, i, saved, y_prev, gate_prev):
        xin, h2, up, act, yf = saved
        sc2 = mods[i][4]
        dact = matmul(dyf, W_down[i], tb=True, out_dtype=BF16, name=f"ffn_down_dx_{i}")
        dW_down = matmul(act, dyf, ta=True, out_dtype=BF16, name=f"ffn_down_dw_{i}")
        dup, dcw, dcb = conv_gate_bwd(up, dact, conv_w_full[i], conv_b[i:i + 1], B=B, S=S, name=f"ffn_conv_bwd_{i}")
        dh2 = matmul(dup, W_up[i], tb=True, b_chips=True, name=f"ffn_up_dx_{i}")
        dW_up = matmul(h2, dup, ta=True, b_chips=True, out_chips=True, out_dtype=BF16, name=f"ffn_up_dw_{i}")
        dxin, dy_prev, dsh2, dsc2, dnf, dgate_prev = norm_mod_bwd_gate(
            dh2, xin, dxo, norm_ffn[i], sc2, y_prev, gate_prev, B=B, S=S, name=f"ffn_norm_bwd_{i}")
        dconv_w = jnp.sum(dcw, axis=0).transpose(1, 0, 2).reshape(3, F2)
        return dxin, dy_prev, dgate_prev, dict(dW_down=dW_down, dW_up=dW_up, dconv_b=jnp.sum(dcb, axis=0).reshape(F2),
                                               dconv_w=dconv_w, dnorm_ffn=jnp.sum(dnf, axis=0), dsh2=dsh2, dsc2=dsc2)

    dx3, dys, dg1s, gf1 = ffn_bwd(dx4, dyf1, 1, ffn1, ys, g1s)
    gf1["dg2"] = dg2_1

    dys_p = _interleave(dys, B, S)
    dgb = matmul(dys_p, W_o_ssm, tb=True, name="ssm_out_dx")
    dW_o_ssm = matmul(gb, dys_p, ta=True, out_dtype=BF16, name="ssm_out_dw")
    ds_glu, dz1, db_glu = glu_bwd1(y_ssm, s_glu, dgb, B=B, S=S, name="ssm_glu_bwd1")
    dz2 = matmul(ds_glu, W_glu, tb=True, name="ssm_glu_dx")
    dW_glu = matmul(zb, ds_glu, ta=True, out_dtype=BF16, name="ssm_glu_dw")
    dy_ssm = glu_bwd2(y_ssm, dz1, dz2, B=B, S=S, name="ssm_glu_bwd2")
    du, dbre, dbim, dcre, dcim, dlr8, dli8, ddsk = ssm_bwd(u, dy_ssm, bre_blk, bim_blk, cre_blk, cim_blk, lr8, li8,
                                                           d_skip_full, B=B, S=S, name="ssm_scan_bwd")
    dub = du.astype(BF16)
    dh1p = matmul(dub, W_in, tb=True, name="ssm_in_dx")
    dW_in = matmul(h1p, dub, ta=True, out_dtype=BF16, name="ssm_in_dw")
    dx2, dyf0, dsh1s, dsc1s, dnm1, dg2_0 = norm_mod_bwd_gate(_deinterleave(dh1p, B, S), x2, dx3, norm_mix[1], sc1s,
                                                             yf0, mods[0][5], B=B, S=S, name="ssm_norm_bwd")
    dlr = jnp.sum(dlr8, axis=(0, 2)).reshape(G, P)
    dli = jnp.sum(dli8, axis=(0, 2)).reshape(G, P)
    dbbr = _block_diag_in_grad(jnp.sum(dbre, axis=0), G, P, H)
    dbbi = _block_diag_in_grad(jnp.sum(dbim, axis=0), G, P, H)
    dc_re = _block_diag_out_grad(jnp.sum(dcre, axis=0), G, H, P)
    dc_im = _block_diag_out_grad(jnp.sum(dcim, axis=0), G, H, P)
    dd_skip = jnp.sum(ddsk, axis=0).reshape(D)

    dx1, dya, dg1, gf0 = ffn_bwd(dx2, dyf0, 0, ffn0, ya, g1)
    gf0["dg2"] = dg2_0

    do2 = matmul(dya, W_o_attn, tb=True, out_dtype=BF16, name="att_out_dx")
    dW_o_attn = matmul(o2, dya, ta=True, out_dtype=BF16, name="att_out_dw")
    g_rows_cm = jnp.concatenate([dW_o_attn.reshape(N_CHIPS, Dq, D), dW_in.reshape(N_CHIPS, Dq, D),
                                 dW_glu.reshape(N_CHIPS, Dq, D), dW_o_ssm.reshape(N_CHIPS, Dq, D),
                                 gf0["dW_down"].reshape(N_CHIPS, Fq, D), gf1["dW_down"].reshape(N_CHIPS, Fq, D)], axis=1)
    pairs_a = pair_sums([g_rows_cm, gf0["dW_up"], gf1["dW_up"]], [BF16, BF16, BF16], "a")
    dqkv, *gots_a = attn_bwd_pairs(qkv, ltot, do2, pairs_a, B=B, S=S, name="att_bwd")
    dh1a = matmul(dqkv, _from_chip_major(W_qkv, 1), tb=True, name="att_qkv_dx")
    dW_qkv = _chip_major(matmul(h1a, dqkv, ta=True, b_chips=True, out_dtype=BF16, name="att_qkv_dw"), 1)
    grad_x, dsh1, dsc1, dnm0 = norm_mod_bwd(dh1a, x0, dx1, norm_mix[0], sc1, B=B, S=S, name="att_norm_bwd")

    dmod_rows = jnp.concatenate([dsh1, dsc1, dg1, gf0["dsh2"], gf0["dsc2"], gf0["dg2"],
                                 dsh1s, dsc1s, dg1s, gf1["dsh2"], gf1["dsc2"], gf1["dg2"], dsh_f, dsc_f], axis=1)
    dmod_all = all_gather8(dmod_rows, name="gather_dmod").reshape(BG, 14 * D)
    grad_w_mod = jnp.stack([
        matmul(c_act, lax.dynamic_slice(dmod_all, (0, i * 6 * D + chip * mod_w), (BG, mod_w)), ta=True,
               name=f"mod_dw_{i}") for i in range(2)])
    grad_w_fin = matmul(c_act, lax.dynamic_slice(dmod_all, (0, 12 * D + chip * fin_w), (BG, fin_w)), ta=True,
                        name="fin_dw")

    parts = [jnp.concatenate([jnp.sum(dnm0, axis=0), jnp.sum(dnm1, axis=0)]),
             jnp.concatenate([gf0["dnorm_ffn"], gf1["dnorm_ffn"]]),
             jnp.sum(dmod_rows[:, :12 * D], axis=0),
             dlr.reshape(-1), dli.reshape(-1), dbbr.reshape(-1), dbbi.reshape(-1), dc_re.reshape(-1), dc_im.reshape(-1),
             dd_skip, jnp.sum(db_glu, axis=0),
             jnp.sum(dnorm_out, axis=0), jnp.sum(dmod_rows[:, 12 * D:], axis=0),
             gf0["dconv_w"].reshape(-1), gf1["dconv_w"].reshape(-1), gf0["dconv_b"], gf1["dconv_b"]]
    sizes = [int(p.shape[0]) for p in parts]
    flat = jnp.concatenate(parts)
    width = 1024
    quantum = N_CHIPS * 16 * width
    padded = -(-flat.shape[0] // quantum) * quantum
    small_cm = jnp.pad(flat, (0, padded - flat.shape[0])).reshape(N_CHIPS, -1, width)

    pairs_b = pair_sums([dW_qkv, small_cm], [BF16, F32], "b")
    gots_b = scatter_to_chips(pairs_b, name="rs_scatter_to_chips")
    r_qkv, r_small, r_rows, r_up0, r_up1 = join_halves(
        chip_sums(pairs_b, gots_b, "b") + chip_sums(pairs_a, gots_a, "a"), name="rs_join_halves")
    grad_w_qkv = r_qkv[None]
    grad_w_o_attn = r_rows[0 * Dq:1 * Dq][None]
    grad_w_in_ssm = r_rows[1 * Dq:2 * Dq][None]
    grad_w_glu = r_rows[2 * Dq:3 * Dq][None]
    grad_w_o_ssm = r_rows[3 * Dq:4 * Dq][None]
    grad_w_down = r_rows[4 * Dq:].reshape(2, Fq, D)
    grad_w_up = jnp.stack([r_up0, r_up1])
    summed = all_gather8(r_small, name="gather_small_grads")[::2].reshape(-1)
    offs = [0]
    for s_ in sizes:
        offs.append(offs[-1] + s_)
    (s_nm, s_nf, s_bmod, s_lr, s_li, s_bbr, s_bbi, s_cre, s_cim, s_dsk, s_bglu, s_no, s_bfin, s_cw0, s_cw1, s_cb0,
     s_cb1) = [summed[offs[i]:offs[i + 1]] for i in range(len(sizes))]
    _, disc_vjp = jax.vjp(_ssm_discretize, a_re[0], a_im[0], log_dt[0], b_re[0], b_im[0])
    ga_re, ga_im, glog_dt, gb_re, gb_im = disc_vjp((s_lr.reshape(G, P), s_li.reshape(G, P), s_bbr.reshape(G, P, H),
                                                    s_bbi.reshape(G, P, H)))
    grad_norm_mix = s_nm.reshape(2, D)
    grad_norm_ffn = s_nf.reshape(2, D)
    grad_b_mod = s_bmod.reshape(2, 6 * D)
    grad_c_re = s_cre.reshape(1, G, H, P)
    grad_c_im = s_cim.reshape(1, G, H, P)
    grad_d_skip = lax.dynamic_slice(s_dsk, (chip * Dq,), (Dq,)).reshape(1, Dq)
    grad_b_glu = lax.dynamic_slice(s_bglu, (chip * Dq,), (Dq,)).reshape(1, Dq)
    cw_full = jnp.stack([s_cw0.reshape(3, F2), s_cw1.reshape(3, F2)])
    grad_conv_w = lax.dynamic_slice(cw_full, (0, 0, chip * (F2 // N_CHIPS)), (2, 3, F2 // N_CHIPS))
    grad_conv_b = jnp.stack([s_cb0, s_cb1])
    grad_norm_out = s_no
    grad_b_fin = s_bfin

    grads = dict(
        norm_mix=grad_norm_mix, norm_ffn=grad_norm_ffn, w_mod=grad_w_mod, b_mod=grad_b_mod, w_qkv=grad_w_qkv,
        w_o_attn=grad_w_o_attn, w_in_ssm=grad_w_in_ssm, a_re=ga_re[None], a_im=ga_im[None], log_dt=glog_dt[None],
        b_re=gb_re[None], b_im=gb_im[None], c_re=grad_c_re, c_im=grad_c_im, d_skip=grad_d_skip, w_glu=grad_w_glu,
        b_glu=grad_b_glu, w_o_ssm=grad_w_o_ssm, w_up=grad_w_up, conv_w=grad_conv_w, conv_b=grad_conv_b,
        w_down=grad_w_down, norm_out=grad_norm_out, w_fin=grad_w_fin, b_fin=grad_b_fin)
    weights = dict(
        norm_mix=norm_mix, norm_ffn=norm_ffn, w_mod=w_mod, b_mod=b_mod, w_qkv=w_qkv, w_o_attn=w_o_attn,
        w_in_ssm=w_in_ssm, a_re=a_re, a_im=a_im, log_dt=log_dt, b_re=b_re, b_im=b_im, c_re=c_re, c_im=c_im,
        d_skip=d_skip, w_glu=w_glu, b_glu=b_glu, w_o_ssm=w_o_ssm, w_up=w_up, conv_w=conv_w, conv_b=conv_b,
        w_down=w_down, norm_out=norm_out, w_fin=w_fin, b_fin=b_fin)
    m_in = dict(
        norm_mix=m_norm_mix, norm_ffn=m_norm_ffn, w_mod=m_w_mod, b_mod=m_b_mod, w_qkv=m_w_qkv, w_o_attn=m_w_o_attn,
        w_in_ssm=m_w_in_ssm, a_re=m_a_re, a_im=m_a_im, log_dt=m_log_dt, b_re=m_b_re, b_im=m_b_im, c_re=m_c_re,
        c_im=m_c_im, d_skip=m_d_skip, w_glu=m_w_glu, b_glu=m_b_glu, w_o_ssm=m_w_o_ssm, w_up=m_w_up, conv_w=m_conv_w,
        conv_b=m_conv_b, w_down=m_w_down, norm_out=m_norm_out, w_fin=m_w_fin, b_fin=m_b_fin)
    v_in = dict(
        norm_mix=v_norm_mix, norm_ffn=v_norm_ffn, w_mod=v_w_mod, b_mod=v_b_mod, w_qkv=v_w_qkv, w_o_attn=v_w_o_attn,
        w_in_ssm=v_w_in_ssm, a_re=v_a_re, a_im=v_a_im, log_dt=v_log_dt, b_re=v_b_re, b_im=v_b_im, c_re=v_c_re,
        c_im=v_c_im, d_skip=v_d_skip, w_glu=v_w_glu, b_glu=v_b_glu, w_o_ssm=v_w_o_ssm, w_up=v_w_up, conv_w=v_conv_w,
        conv_b=v_conv_b, w_down=v_w_down, norm_out=v_norm_out, w_fin=v_w_fin, b_fin=v_b_fin)
    names = list(weights)
    for n_ in names:
        grads[n_] = grads[n_].reshape(weights[n_].shape)

    big = ("w_mod", "w_qkv", "w_o_attn", "w_in_ssm", "w_glu", "w_o_ssm", "w_up", "w_down", "w_fin")
    delta, new_m, new_v = {}, {}, {}
    for n_ in big:
        shp = weights[n_].shape
        two_d = lambda a: a.reshape(-1, shp[-1])
        d_, m_, v_ = adamw(two_d(weights[n_]), two_d(grads[n_]), two_d(m_in[n_]), two_d(v_in[n_]), name=f"adamw_{n_}")
        delta[n_], new_m[n_], new_v[n_] = d_.reshape(shp), m_.reshape(shp), v_.reshape(shp)
    rest = [n_ for n_ in names if n_ not in big]
    small_out = adamw_many([weights[n_] for n_ in rest], [grads[n_] for n_ in rest], [m_in[n_] for n_ in rest],
                           [v_in[n_] for n_ in rest], name="adamw_small")
    for n_, (d_, m_, v_) in zip(rest, small_out):
        delta[n_], new_m[n_], new_v[n_] = d_, m_, v_

    return (loss, grad_x.reshape(B, S, D), *[grads[n_] for n_ in names], *[delta[n_] for n_ in names],
            *[new_m[n_] for n_ in names], *[new_v[n_] for n_ in names])
```
